```python
import math
import jax, jax.numpy as jnp
from jax import lax
import numpy as np

D_MODEL = 1024
BATCH = 8
SEQ = 8192
DEPTH = 1

HEAD_DIM = 64
N_HEADS_A = D_MODEL // (2 * HEAD_DIM)
N_KV_HEADS_A = N_HEADS_A // 4
N_HEADS_B = D_MODEL // (2 * HEAD_DIM)
WIDTH_A = N_HEADS_A * HEAD_DIM
WIDTH_B = N_HEADS_B * HEAD_DIM
MIX_WIDTH = WIDTH_A + WIDTH_B
D_IN_PROJ = (N_HEADS_A + 2 * N_KV_HEADS_A + 3 * N_HEADS_B) * HEAD_DIM
WINDOW_A = 128
DILATED_BRANCHES = ((128, 1), (512, 4), (2048, 16))
BLOCK = 128
ROPE_THETA = 150000.0
REL_BUCKETS = 32
REL_MAX_DISTANCE = 2048
D_FF = -(-8 * D_MODEL // (3 * 256)) * 256
NORM_EPS = 1e-5

kernel_name = "hymba_swa_sink_dilated_hybrid"


def rmsnorm(x, g):
    xf = x.astype(jnp.float32)
    y = xf * lax.rsqrt(jnp.mean(xf * xf, axis=-1, keepdims=True) + NORM_EPS)
    return (y * g.astype(jnp.float32)).astype(x.dtype)


def rope(t, seq_len):
    half = t.shape[-1] // 2
    inv_freq = ROPE_THETA ** (-jnp.arange(half, dtype=jnp.float32) / half)
    ang = jnp.arange(seq_len, dtype=jnp.float32)[:, None] * inv_freq[None, :]
    cos = jnp.cos(ang)[None, :, None, :].astype(t.dtype)
    sin = jnp.sin(ang)[None, :, None, :].astype(t.dtype)
    t1, t2 = t[..., :half], t[..., half:]
    return jnp.concatenate([t1 * cos - t2 * sin, t1 * sin + t2 * cos], axis=-1)


def t5_bucket(dist):
    max_exact = REL_BUCKETS // 2
    df = jnp.maximum(dist, 1).astype(jnp.float32)
    large = max_exact + (jnp.log(df / max_exact) / math.log(REL_MAX_DISTANCE / max_exact)
                         * (REL_BUCKETS - max_exact)).astype(jnp.int32)
    large = jnp.minimum(large, REL_BUCKETS - 1)
    return jnp.where(dist < max_exact, dist, large)


def banded_attention(q, k, v, n_back, bias=None, sinks=None):
    N, H, L, D = q.shape
    Hkv = k.shape[1]
    G = H // Hkv
    nb = L // BLOCK
    qb = q.reshape(N, Hkv, G, nb, BLOCK, D) * (D ** -0.5)

    def windows(t):
        tb = t.reshape(N, Hkv, nb, BLOCK, D)
        prev = jnp.pad(tb, ((0, 0), (0, 0), (1, 0), (0, 0), (0, 0)))[:, :, :-1]
        return jnp.concatenate([prev, tb], axis=3)

    kw, vw = windows(k), windows(v)
    s = jnp.einsum('nkgbqd,nkbsd->nkgbqs', qb, kw).astype(jnp.float32)
    delta = BLOCK + jnp.arange(BLOCK)[:, None] - jnp.arange(2 * BLOCK)[None, :]
    in_band = (delta >= 0) & (delta <= n_back)
    key_exists = (jnp.arange(nb)[:, None] > 0) | (jnp.arange(2 * BLOCK)[None, :] >= BLOCK)
    mask = in_band[None] & key_exists[:, None, :]
    if bias is not None:
        b = bias.astype(jnp.float32)[:, jnp.clip(delta, 0, n_back)]
        s = s + b.reshape(1, Hkv, G, 1, BLOCK, 2 * BLOCK)
    s = jnp.where(mask, s, -jnp.inf)
    m = jnp.max(s, axis=-1)
    if sinks is not None:
        sk = sinks.astype(jnp.float32).reshape(1, Hkv, G, 1, 1)
        m = jnp.maximum(m, sk)
    p = jnp.exp(s - m[..., None])
    denom = jnp.sum(p, axis=-1)
    if sinks is not None:
        denom = denom + jnp.exp(sk - m)
    o = jnp.einsum('nkgbqs,nkbsd->nkgbqd', p, vw.astype(jnp.float32)) / denom[..., None]
    lse = m + jnp.log(denom)
    return o.reshape(N, H, L, D).astype(q.dtype), lse.reshape(N, H, L)


def sliding_window_sink_gqa(q, k, v, sinks):
    S = q.shape[1]
    q, k = rope(q, S), rope(k, S)
    o, _ = banded_attention(q.transpose(0, 2, 1, 3), k.transpose(0, 2, 1, 3),
                            v.transpose(0, 2, 1, 3), WINDOW_A - 1, sinks=sinks)
    return o.transpose(0, 2, 1, 3)


def dilated_mixture(q, k, v, rel_table):
    B, S, H, D = q.shape
    outs, lses = [], []
    for window, dil in DILATED_BRANCHES:
        n_back = window // dil
        span = dil * BLOCK
        s_pad = -(-S // span) * span
        L = s_pad // dil

        def to_sub(t):
            t = jnp.pad(t, ((0, 0), (0, s_pad - S), (0, 0), (0, 0)))
            return t.reshape(B, L, dil, H, D).transpose(0, 2, 3, 1, 4).reshape(B * dil, H, L, D)

        bias = rel_table[t5_bucket(jnp.arange(n_back + 1) * dil)].T
        o, lse = banded_attention(to_sub(q), to_sub(k), to_sub(v), n_back, bias=bias)
        o = o.reshape(B, dil, H, L, D).transpose(0, 3, 1, 2, 4).reshape(B, s_pad, H, D)[:, :S]
        lse = lse.reshape(B, dil, H, L).transpose(0, 3, 1, 2).reshape(B, s_pad, H)[:, :S]
        outs.append(o)
        lses.append(lse)
    w = jax.nn.softmax(jnp.stack(lses, axis=0), axis=0)
    out = jnp.sum(w[..., None] * jnp.stack(outs, axis=0).astype(jnp.float32), axis=0)
    return out.astype(q.dtype)


def _fwd_setup_inputs(seed: int = 0) -> dict:
    key = jax.random.key(seed)
    ks = jax.random.split(key, 16)
    f32 = jnp.float32
    nrm = lambda k, shape, scale: jax.random.normal(k, shape, f32) * scale
    return {
        "x": jax.random.normal(ks[0], (BATCH, SEQ, D_MODEL), f32),
        "g_attn": 1.0 + nrm(ks[1], (DEPTH, D_MODEL), 0.01),
        "w_in": nrm(ks[2], (DEPTH, D_MODEL, D_IN_PROJ), D_MODEL ** -0.5),
        "b_in": nrm(ks[3], (DEPTH, D_IN_PROJ), 0.01),
        "sinks": nrm(ks[4], (DEPTH, N_HEADS_A), 0.5),
        "rel_table": nrm(ks[5], (REL_BUCKETS, N_HEADS_B), 0.5),
        "g_out_a": 1.0 + nrm(ks[6], (DEPTH, WIDTH_A), 0.01),
        "g_out_b": 1.0 + nrm(ks[7], (DEPTH, WIDTH_B), 0.01),
        "w_o": nrm(ks[8], (DEPTH, MIX_WIDTH, D_MODEL), MIX_WIDTH ** -0.5),
        "g_ffn": 1.0 + nrm(ks[9], (DEPTH, D_MODEL), 0.01),
        "w_gate": nrm(ks[10], (DEPTH, D_MODEL, D_FF), D_MODEL ** -0.5),
        "w_up": nrm(ks[11], (DEPTH, D_MODEL, D_FF), D_MODEL ** -0.5),
        "w_down": nrm(ks[12], (DEPTH, D_FF, D_MODEL), D_FF ** -0.5),
        "g_final": 1.0 + nrm(ks[13], (D_MODEL,), 0.01),
    }


def _fwd_reference(x, g_attn, w_in, b_in, sinks, rel_table, g_out_a, g_out_b, w_o,
              g_ffn, w_gate, w_up, w_down, g_final):
    B, S, _ = x.shape
    splits = np.cumsum([WIDTH_A, N_KV_HEADS_A * HEAD_DIM, N_KV_HEADS_A * HEAD_DIM,
                        WIDTH_B, WIDTH_B])
    for l in range(DEPTH):
        h = rmsnorm(x, g_attn[l])
        proj = jnp.einsum('bsd,de->bse', h, w_in[l]) + b_in[l]
        qa, ka, va, qb, kb, vb = jnp.split(proj, splits, axis=-1)
        qa = qa.reshape(B, S, N_HEADS_A, HEAD_DIM)
        ka = ka.reshape(B, S, N_KV_HEADS_A, HEAD_DIM)
        va = va.reshape(B, S, N_KV_HEADS_A, HEAD_DIM)
        qb = qb.reshape(B, S, N_HEADS_B, HEAD_DIM)
        kb = kb.reshape(B, S, N_HEADS_B, HEAD_DIM)
        vb = vb.reshape(B, S, N_HEADS_B, HEAD_DIM)
        oa = sliding_window_sink_gqa(qa, ka, va, sinks[l]).reshape(B, S, WIDTH_A)
        ob = dilated_mixture(qb, kb, vb, rel_table).reshape(B, S, WIDTH_B)
        mixed = jnp.concatenate([rmsnorm(oa, g_out_a[l]), rmsnorm(ob, g_out_b[l])], axis=-1)
        x = x + jnp.einsum('bse,ed->bsd', mixed, w_o[l])
        h = rmsnorm(x, g_ffn[l])
        act = jax.nn.silu(jnp.einsum('bsd,df->bsf', h, w_gate[l])) * jnp.einsum('bsd,df->bsf', h, w_up[l])
        x = x + jnp.einsum('bsf,fd->bsd', act, w_down[l])
    return rmsnorm(x, g_final)


import jax as _jax
import jax.numpy as _jnp

TWIN_FORMAT = 'train_step'
FWD_PARAMS = ['x', 'g_attn', 'w_in', 'b_in', 'sinks', 'rel_table', 'g_out_a', 'g_out_b', 'w_o', 'g_ffn', 'w_gate', 'w_up', 'w_down', 'g_final']
TWIN_WEIGHTS = ['g_attn', 'w_in', 'b_in', 'sinks', 'rel_table', 'g_out_a', 'g_out_b', 'w_o', 'g_ffn', 'w_gate', 'w_up', 'w_down', 'g_final']
TWIN_DIFF_INPUT = 'x'
TWIN_INPUTS = ['x', 'g_attn', 'w_in', 'b_in', 'sinks', 'rel_table', 'g_out_a', 'g_out_b', 'w_o', 'g_ffn', 'w_gate', 'w_up', 'w_down', 'g_final', 'loss_target', 'm_g_attn', 'm_w_in', 'm_b_in', 'm_sinks', 'm_rel_table', 'm_g_out_a', 'm_g_out_b', 'm_w_o', 'm_g_ffn', 'm_w_gate', 'm_w_up', 'm_w_down', 'm_g_final', 'v_g_attn', 'v_w_in', 'v_b_in', 'v_sinks', 'v_rel_table', 'v_g_out_a', 'v_g_out_b', 'v_w_o', 'v_g_ffn', 'v_w_gate', 'v_w_up', 'v_w_down', 'v_g_final']
TWIN_OUTPUTS = ['loss', 'grad_x', 'grad_g_attn', 'grad_w_in', 'grad_b_in', 'grad_sinks', 'grad_rel_table', 'grad_g_out_a', 'grad_g_out_b', 'grad_w_o', 'grad_g_ffn', 'grad_w_gate', 'grad_w_up', 'grad_w_down', 'grad_g_final', 'delta_g_attn', 'delta_w_in', 'delta_b_in', 'delta_sinks', 'delta_rel_table', 'delta_g_out_a', 'delta_g_out_b', 'delta_w_o', 'delta_g_ffn', 'delta_w_gate', 'delta_w_up', 'delta_w_down', 'delta_g_final', 'new_m_g_attn', 'new_m_w_in', 'new_m_b_in', 'new_m_sinks', 'new_m_rel_table', 'new_m_g_out_a', 'new_m_g_out_b', 'new_m_w_o', 'new_m_g_ffn', 'new_m_w_gate', 'new_m_w_up', 'new_m_w_down', 'new_m_g_final', 'new_v_g_attn', 'new_v_w_in', 'new_v_b_in', 'new_v_sinks', 'new_v_rel_table', 'new_v_g_out_a', 'new_v_g_out_b', 'new_v_w_o', 'new_v_g_ffn', 'new_v_w_gate', 'new_v_w_up', 'new_v_w_down', 'new_v_g_final']
TWIN_LEAF_KINDS = {'loss': 'loss', 'grad_x': 'grad_x', 'grad_g_attn': 'grad_w', 'grad_w_in': 'grad_w', 'grad_b_in': 'grad_w', 'grad_sinks': 'grad_w', 'grad_rel_table': 'grad_w', 'grad_g_out_a': 'grad_w', 'grad_g_out_b': 'grad_w', 'grad_w_o': 'grad_w', 'grad_g_ffn': 'grad_w', 'grad_w_gate': 'grad_w', 'grad_w_up': 'grad_w', 'grad_w_down': 'grad_w', 'grad_g_final': 'grad_w', 'delta_g_attn': 'delta_w', 'delta_w_in': 'delta_w', 'delta_b_in': 'delta_w', 'delta_sinks': 'delta_w', 'delta_rel_table': 'delta_w', 'delta_g_out_a': 'delta_w', 'delta_g_out_b': 'delta_w', 'delta_w_o': 'delta_w', 'delta_g_ffn': 'delta_w', 'delta_w_gate': 'delta_w', 'delta_w_up': 'delta_w', 'delta_w_down': 'delta_w', 'delta_g_final': 'delta_w', 'new_m_g_attn': 'new_m', 'new_m_w_in': 'new_m', 'new_m_b_in': 'new_m', 'new_m_sinks': 'new_m', 'new_m_rel_table': 'new_m', 'new_m_g_out_a': 'new_m', 'new_m_g_out_b': 'new_m', 'new_m_w_o': 'new_m', 'new_m_g_ffn': 'new_m', 'new_m_w_gate': 'new_m', 'new_m_w_up': 'new_m', 'new_m_w_down': 'new_m', 'new_m_g_final': 'new_m', 'new_v_g_attn': 'new_v', 'new_v_w_in': 'new_v', 'new_v_b_in': 'new_v', 'new_v_sinks': 'new_v', 'new_v_rel_table': 'new_v', 'new_v_g_out_a': 'new_v', 'new_v_g_out_b': 'new_v', 'new_v_w_o': 'new_v', 'new_v_g_ffn': 'new_v', 'new_v_w_gate': 'new_v', 'new_v_w_up': 'new_v', 'new_v_w_down': 'new_v', 'new_v_g_final': 'new_v'}


def _forward(args):
    return _fwd_reference(*[args[k] for k in FWD_PARAMS])


def _output_shape():
    out = _jax.eval_shape(lambda: _forward(_fwd_setup_inputs(0)))
    return out.shape, out.dtype

N_MICROBATCH = 1
ADAM_LR = 0.001
ADAM_B1 = 0.9
ADAM_B2 = 0.999
ADAM_EPS = 1e-08
ADAM_WD = 0.01
ADAM_STEP = 10
PER_EXAMPLE_BATCH_AXIS = {'x': 0, 'loss_target': 0}
SHARED_INPUTS = []
_WEIGHT_DTYPES = {'g_attn': _jnp.float32, 'w_in': _jnp.float32, 'b_in': _jnp.float32, 'sinks': _jnp.float32, 'rel_table': _jnp.float32, 'g_out_a': _jnp.float32, 'g_out_b': _jnp.float32, 'w_o': _jnp.float32, 'g_ffn': _jnp.float32, 'w_gate': _jnp.float32, 'w_up': _jnp.float32, 'w_down': _jnp.float32, 'g_final': _jnp.float32}
MOMENT_SCALE = {'g_attn': 3.198494e-01, 'w_in': 2.246762e-01, 'b_in': 1.555161e+00, 'sinks': 4.260978e-02, 'rel_table': 2.547816e-01, 'g_out_a': 1.930746e-01, 'g_out_b': 1.944243e-01, 'w_o': 1.911355e-01, 'g_ffn': 1.426484e-01, 'w_gate': 6.077704e-02, 'w_up': 5.860823e-02, 'w_down': 9.723557e-02, 'g_final': 6.404307e+01}


def _to_microbatches(a, axis):
    t = _jnp.moveaxis(a, axis, 0)
    t = t.reshape((N_MICROBATCH, t.shape[0] // N_MICROBATCH) + t.shape[1:])
    return _jnp.moveaxis(t, 1, axis + 1)


def setup_inputs(seed: int = 0) -> dict:
    inp = _fwd_setup_inputs(seed)
    key = _jax.random.fold_in(_jax.random.key(seed), 7919)
    shape, _ = _output_shape()
    out = dict(inp)
    out["loss_target"] = _jax.random.normal(_jax.random.fold_in(key, 0), shape, _jnp.float32)
    for i, name in enumerate(TWIN_WEIGHTS):
        w = inp[name].astype(_jnp.float32)
        if MOMENT_SCALE is None:
            s = _jnp.sqrt(_jnp.mean(_jnp.square(w)) + 1e-30)
        else:
            s = MOMENT_SCALE[name]
        km, kv = _jax.random.split(_jax.random.fold_in(key, i + 1))
        out[name] = w
        out["m_" + name] = s * _jax.random.normal(km, w.shape, _jnp.float32)
        out["v_" + name] = (s * s) * _jax.random.uniform(kv, w.shape, _jnp.float32, 0.5, 1.5)
    if N_MICROBATCH > 1:
        for name, axis in PER_EXAMPLE_BATCH_AXIS.items():
            out[name] = _to_microbatches(out[name], axis)
    return {'x': out['x'], 'g_attn': out['g_attn'], 'w_in': out['w_in'], 'b_in': out['b_in'], 'sinks': out['sinks'], 'rel_table': out['rel_table'], 'g_out_a': out['g_out_a'], 'g_out_b': out['g_out_b'], 'w_o': out['w_o'], 'g_ffn': out['g_ffn'], 'w_gate': out['w_gate'], 'w_up': out['w_up'], 'w_down': out['w_down'], 'g_final': out['g_final'], 'loss_target': out['loss_target'], 'm_g_attn': out['m_g_attn'], 'm_w_in': out['m_w_in'], 'm_b_in': out['m_b_in'], 'm_sinks': out['m_sinks'], 'm_rel_table': out['m_rel_table'], 'm_g_out_a': out['m_g_out_a'], 'm_g_out_b': out['m_g_out_b'], 'm_w_o': out['m_w_o'], 'm_g_ffn': out['m_g_ffn'], 'm_w_gate': out['m_w_gate'], 'm_w_up': out['m_w_up'], 'm_w_down': out['m_w_down'], 'm_g_final': out['m_g_final'], 'v_g_attn': out['v_g_attn'], 'v_w_in': out['v_w_in'], 'v_b_in': out['v_b_in'], 'v_sinks': out['v_sinks'], 'v_rel_table': out['v_rel_table'], 'v_g_out_a': out['v_g_out_a'], 'v_g_out_b': out['v_g_out_b'], 'v_w_o': out['v_w_o'], 'v_g_ffn': out['v_g_ffn'], 'v_w_gate': out['v_w_gate'], 'v_w_up': out['v_w_up'], 'v_w_down': out['v_w_down'], 'v_g_final': out['v_g_final']}


def _loss(weights, diff, rest, loss_target):
    with _jax.named_scope("forward"):
        args = {**rest, TWIN_DIFF_INPUT: diff, **{k: w.astype(_WEIGHT_DTYPES[k]) for k, w in weights.items()}}
        y = _forward(args)
    with _jax.named_scope("loss_head"):
        err = _jnp.square(y.astype(_jnp.float32) - loss_target)
        return 0.5 * _jnp.sum(_jnp.mean(err, axis=-1)) if err.ndim else 0.5 * err


def _adamw(w, g, m, v):
    m = ADAM_B1 * m + (1.0 - ADAM_B1) * g
    v = ADAM_B2 * v + (1.0 - ADAM_B2) * _jnp.square(g)
    m_hat = m / (1.0 - ADAM_B1 ** ADAM_STEP)
    v_hat = v / (1.0 - ADAM_B2 ** ADAM_STEP)
    delta = -ADAM_LR * (m_hat / (_jnp.sqrt(v_hat) + ADAM_EPS) + ADAM_WD * w)
    return delta, m, v


def reference(x, g_attn, w_in, b_in, sinks, rel_table, g_out_a, g_out_b, w_o, g_ffn, w_gate, w_up, w_down, g_final, loss_target, m_g_attn, m_w_in, m_b_in, m_sinks, m_rel_table, m_g_out_a, m_g_out_b, m_w_o, m_g_ffn, m_w_gate, m_w_up, m_w_down, m_g_final, v_g_attn, v_w_in, v_b_in, v_sinks, v_rel_table, v_g_out_a, v_g_out_b, v_w_o, v_g_ffn, v_w_gate, v_w_up, v_w_down, v_g_final):
    given = dict(x=x, g_attn=g_attn, w_in=w_in, b_in=b_in, sinks=sinks, rel_table=rel_table, g_out_a=g_out_a, g_out_b=g_out_b, w_o=w_o, g_ffn=g_ffn, w_gate=w_gate, w_up=w_up, w_down=w_down, g_final=g_final, loss_target=loss_target, m_g_attn=m_g_attn, m_w_in=m_w_in, m_b_in=m_b_in, m_sinks=m_sinks, m_rel_table=m_rel_table, m_g_out_a=m_g_out_a, m_g_out_b=m_g_out_b, m_w_o=m_w_o, m_g_ffn=m_g_ffn, m_w_gate=m_w_gate, m_w_up=m_w_up, m_w_down=m_w_down, m_g_final=m_g_final, v_g_attn=v_g_attn, v_w_in=v_w_in, v_b_in=v_b_in, v_sinks=v_sinks, v_rel_table=v_rel_table, v_g_out_a=v_g_out_a, v_g_out_b=v_g_out_b, v_w_o=v_w_o, v_g_ffn=v_g_ffn, v_w_gate=v_w_gate, v_w_up=v_w_up, v_w_down=v_w_down, v_g_final=v_g_final)
    weights = {n: given[n] for n in TWIN_WEIGHTS}
    shared = {n: given[n] for n in SHARED_INPUTS}
    per_example = {n: given[n] for n in ['x']}
    grad_fn = _jax.value_and_grad(_loss, argnums=(0, 1))

    def one_microbatch(ex, loss_target):
        ex = dict(ex)
        diff = ex.pop(TWIN_DIFF_INPUT)
        return grad_fn(weights, diff, {**shared, **ex}, loss_target)

    if N_MICROBATCH == 1:
        loss, (grad_w, grad_x) = one_microbatch(per_example, given["loss_target"])
    else:
        def body(carry, xs):
            loss_sum, grad_sum = carry
            l_k, (gw_k, gx_k) = one_microbatch(xs[0], xs[1])
            with _jax.named_scope("update"):
                return (loss_sum + l_k, _jax.tree.map(_jnp.add, grad_sum, gw_k)), gx_k

        init = (_jnp.zeros((), _jnp.float32), _jax.tree.map(_jnp.zeros_like, weights))
        (loss, grad_w), grad_x = _jax.lax.scan(body, init, (per_example, given["loss_target"]))
    with _jax.named_scope("update"):
        delta_w, new_m, new_v = {}, {}, {}
        for n in TWIN_WEIGHTS:
            delta_w[n], new_m[n], new_v[n] = _adamw(weights[n], grad_w[n], given["m_" + n], given["v_" + n])
    return (loss, grad_x, *[grad_w[n] for n in TWIN_WEIGHTS], *[delta_w[n] for n in TWIN_WEIGHTS],
            *[new_m[n] for n in TWIN_WEIGHTS], *[new_v[n] for n in TWIN_WEIGHTS])
```

```python
import functools
import math

import jax
import jax.numpy as jnp
from jax import lax
from jax.experimental import pallas as pl
from jax.experimental.pallas import tpu as pltpu

F32 = jnp.float32
BF16 = jnp.bfloat16

N_DEV = 8
D_MODEL = 1024
HEAD_DIM = 64
N_HEADS = 8
PAIR = 2 * HEAD_DIM
WIDTH = N_HEADS * HEAD_DIM
D_IN = 2304
D_INP = 2560
D_FF = 2816
BLK = 128
ROPE_THETA = 150000.0
REL_BUCKETS = 32
REL_MAX_DISTANCE = 2048
EPS = 1e-5
NEG = -1e30
BRANCHES = ((128, 1), (512, 4), (2048, 16))
Q_SCALE = HEAD_DIM ** -0.5

ADAM_LR = 0.001
ADAM_B1 = 0.9
ADAM_B2 = 0.999
ADAM_EPS = 1e-08
ADAM_WD = 0.01
ADAM_STEP = 10

VMEM_LIMIT = 56 * 1024 * 1024
MESH = pl.DeviceIdType.MESH

NT = (((1,), (1,)), ((), ()))
TN = (((0,), (0,)), ((), ()))

SEG_ROWS = (288, 128, 352, 352, 352)
FLAT_ROWS = sum(SEG_ROWS)
SMALL_ROWS = 56


def _params(sem=None):
    return pltpu.CompilerParams(dimension_semantics=sem, vmem_limit_bytes=VMEM_LIMIT)


def _rms_bwd(dh, xh, r, g):
    u = dh * g
    return r * (u - xh * jnp.mean(u * xh, axis=-1, keepdims=True))


def _rope_rot(t, first):
    return jnp.where(first, pltpu.roll(t, 96, 1), pltpu.roll(t, 32, 1))


def _norm_proj(x, g, w, b, cos, sin, *, tm=512):
    T = x.shape[0]
    segs = ((0, 512, True, Q_SCALE), (512, 256, True, 1.0), (768, 256, False, 1.0),
            (1024, 512, False, Q_SCALE), (1536, 512, False, 1.0), (2048, 512, False, 1.0))

    def body(x_ref, g_ref, w_ref, b_ref, cos_ref, sin_ref, h_ref, qa_ref, ka_ref, va_ref, qb_ref, kb_ref, vb_ref):
        xv = x_ref[...]
        r = lax.rsqrt(jnp.mean(xv * xv, axis=-1, keepdims=True) + EPS)
        h = (xv * r * g_ref[...]).astype(BF16)
        h_ref[...] = h
        cosv = cos_ref[...]
        sinv = sin_ref[...]
        lane = lax.broadcasted_iota(jnp.int32, (tm, PAIR), 1)
        first = (lane % HEAD_DIM) < (HEAD_DIM // 2)
        outs = (qa_ref, ka_ref, va_ref, qb_ref, kb_ref, vb_ref)
        for (off, width, rot, scale), o_ref in zip(segs, outs):
            for c in range(0, width, 256):
                y = jnp.dot(h, w_ref[:, off + c:off + c + 256], preferred_element_type=F32) + b_ref[:, off + c:off + c + 256]
                for j in range(0, 256, PAIR):
                    t = y[:, j:j + PAIR]
                    if rot:
                        t = t * cosv + _rope_rot(t, first) * sinv
                    if scale != 1.0:
                        t = t * scale
                    o_ref[:, c + j:c + j + PAIR] = t.astype(BF16)

    row = lambda w_: pl.BlockSpec((tm, w_), lambda i: (i, 0))
    full = lambda a: pl.BlockSpec(a.shape, lambda i: (0, 0))
    return pl.pallas_call(
        body, name="norm_proj", grid=(T // tm,),
        in_specs=[row(D_MODEL), full(g), full(w), full(b), row(PAIR), row(PAIR)],
        out_specs=[row(D_MODEL), row(512), row(256), row(256), row(512), row(512), row(512)],
        out_shape=[jax.ShapeDtypeStruct((T, n), BF16) for n in (D_MODEL, 512, 256, 256, 512, 512, 512)],
        compiler_params=_params(("arbitrary",)),
    )(x, g, w, b, cos, sin)


def _attn_specs(dil, kvw, nb, clamp):
    qi = (lambda r, i: (jnp.minimum(i, nb - 1), r)) if clamp else (lambda r, i: (i, r))
    q_spec = pl.BlockSpec((BLK, WIDTH), qi)
    kp_spec = pl.BlockSpec((BLK, kvw), lambda r, i: (jnp.minimum(jnp.maximum(i - 1, 0), nb - 1), r))
    kc_spec = pl.BlockSpec((BLK, kvw), qi)
    b_spec = pl.BlockSpec((1, N_HEADS, BLK, 2 * BLK), lambda r, i: (jnp.where(i == 0, 1, 0), 0, 0, 0))
    return q_spec, kp_spec, kc_spec, b_spec


def _attn_fwd(q, k, v, bias, sinks, *, dil, kv_pairs, use_sink, name):
    T = q.shape[0]
    L = T // dil
    nb = L // BLK
    kvw = kv_pairs * PAIR
    rep = 4 // kv_pairs

    def body(sink_ref, q_ref, kp_ref, kc_ref, vp_ref, vc_ref, b_ref, o_ref, lse_ref):
        lo = lax.broadcasted_iota(jnp.int32, (1, PAIR), 1) < HEAD_DIM
        for hp in range(4):
            sl = slice(hp * PAIR, (hp + 1) * PAIR)
            ksl = slice((hp // rep) * PAIR, (hp // rep + 1) * PAIR)
            qp = q_ref[:, sl]
            kk = jnp.concatenate([kp_ref[:, ksl], kc_ref[:, ksl]], axis=0)
            vv = jnp.concatenate([vp_ref[:, ksl], vc_ref[:, ksl]], axis=0)
            o_pair = None
            lse_pair = None
            for e in range(2):
                h = 2 * hp + e
                msk = lo if e == 0 else jnp.logical_not(lo)
                qm = jnp.where(msk, qp, jnp.zeros_like(qp))
                s = lax.dot_general(qm, kk, NT, preferred_element_type=F32) + b_ref[0, h]
                m = jnp.max(s, axis=-1, keepdims=True)
                if use_sink:
                    sk = sink_ref[h]
                    m = jnp.maximum(m, sk)
                p = jnp.exp(s - m)
                l = jnp.sum(p, axis=-1, keepdims=True)
                if use_sink:
                    l = l + jnp.exp(sk - m)
                vm = jnp.where(msk, vv, jnp.zeros_like(vv))
                oe = jnp.dot(p.astype(BF16), vm, preferred_element_type=F32) * (1.0 / l)
                ls = m + jnp.log(l)
                if e == 0:
                    o_pair = oe
                    lse_pair = jnp.broadcast_to(ls, (BLK, PAIR))
                else:
                    o_pair = o_pair + oe
                    lse_pair = jnp.where(msk, ls, lse_pair)
            o_ref[:, sl] = o_pair.astype(BF16)
            lse_ref[:, sl] = lse_pair

    q_spec, kp_spec, kc_spec, b_spec = _attn_specs(dil, kvw, nb, False)
    o, lse = pl.pallas_call(
        body, name=name, grid=(dil, nb),
        in_specs=[pl.BlockSpec(memory_space=pltpu.SMEM), q_spec, kp_spec, kc_spec, kp_spec, kc_spec, b_spec],
        out_specs=[q_spec, q_spec],
        out_shape=[jax.ShapeDtypeStruct((L, dil * WIDTH), BF16), jax.ShapeDtypeStruct((L, dil * WIDTH), F32)],
        compiler_params=_params(("arbitrary", "arbitrary")),
    )(sinks, q.reshape(L, dil * WIDTH), k.reshape(L, dil * kvw), k.reshape(L, dil * kvw),
      v.reshape(L, dil * kvw), v.reshape(L, dil * kvw), bias)
    return o.reshape(T, WIDTH), lse.reshape(T, WIDTH)


def _attn_bwd(q, k, v, o, do, lse, bias, sinks, *, dil, kv_pairs, use_sink, name):
    T = q.shape[0]
    L = T // dil
    nb = L // BLK
    kvw = kv_pairs * PAIR
    rep = 4 // kv_pairs

    def body(sink_ref, q_ref, kp_ref, kc_ref, vp_ref, vc_ref, o_ref, do_ref, lse_ref, b_ref,
             dq_ref, dk_ref, dv_ref, dsum_ref, dsk_ref, ck_ref, cv_ref):
        r = pl.program_id(0)
        i = pl.program_id(1)

        @pl.when((r == 0) & (i == 0))
        def _():
            dsum_ref[...] = jnp.zeros_like(dsum_ref)
            dsk_ref[...] = jnp.zeros_like(dsk_ref)

        @pl.when(i == 0)
        def _():
            ck_ref[...] = jnp.zeros_like(ck_ref)
            cv_ref[...] = jnp.zeros_like(cv_ref)

        @pl.when(i < nb)
        def _():
            lo = lax.broadcasted_iota(jnp.int32, (1, PAIR), 1) < HEAD_DIM
            dks = [None] * kv_pairs
            dvs = [None] * kv_pairs
            for hp in range(4):
                kvp = hp // rep
                sl = slice(hp * PAIR, (hp + 1) * PAIR)
                ksl = slice(kvp * PAIR, (kvp + 1) * PAIR)
                qp = q_ref[:, sl]
                dop = do_ref[:, sl]
                prod = dop.astype(F32) * o_ref[:, sl].astype(F32)
                kk = jnp.concatenate([kp_ref[:, ksl], kc_ref[:, ksl]], axis=0)
                vv = jnp.concatenate([vp_ref[:, ksl], vc_ref[:, ksl]], axis=0)
                dq_pair = None
                c_pair = None
                for e in range(2):
                    h = 2 * hp + e
                    msk = lo if e == 0 else jnp.logical_not(lo)
                    qm = jnp.where(msk, qp, jnp.zeros_like(qp))
                    dom = jnp.where(msk, dop, jnp.zeros_like(dop))
                    km = jnp.where(msk, kk, jnp.zeros_like(kk))
                    s = lax.dot_general(qm, kk, NT, preferred_element_type=F32) + b_ref[0, h]
                    ls = lse_ref[:, h * HEAD_DIM:h * HEAD_DIM + 1]
                    p = jnp.exp(s - ls)
                    dp = lax.dot_general(dom, vv, NT, preferred_element_type=F32)
                    delta = jnp.sum(jnp.where(msk, prod, 0.0), axis=-1, keepdims=True)
                    ds = p * (dp - delta)
                    dsum_ref[h] += ds
                    if use_sink:
                        ce = jnp.exp(sink_ref[h] - ls) * delta
                        c_pair = jnp.broadcast_to(ce, (BLK, PAIR)) if e == 0 else jnp.where(msk, ce, c_pair)
                    dsb = ds.astype(BF16)
                    dqe = jnp.dot(dsb, km, preferred_element_type=F32)
                    dke = lax.dot_general(dsb, qm, TN, preferred_element_type=F32)
                    dve = lax.dot_general(p.astype(BF16), dom, TN, preferred_element_type=F32)
                    dq_pair = dqe if e == 0 else dq_pair + dqe
                    dks[kvp] = dke if dks[kvp] is None else dks[kvp] + dke
                    dvs[kvp] = dve if dvs[kvp] is None else dvs[kvp] + dve
                dq_ref[:, sl] = (dq_pair * Q_SCALE).astype(BF16)
                if use_sink:
                    dsk_ref[:, sl] += c_pair
            for kvp in range(kv_pairs):
                ksl = slice(kvp * PAIR, (kvp + 1) * PAIR)
                dk_ref[:, ksl] = (ck_ref[:, ksl] + dks[kvp][:BLK]).astype(BF16)
                dv_ref[:, ksl] = (cv_ref[:, ksl] + dvs[kvp][:BLK]).astype(BF16)
                ck_ref[:, ksl] = dks[kvp][BLK:]
                cv_ref[:, ksl] = dvs[kvp][BLK:]

        @pl.when(i == nb)
        def _():
            dk_ref[...] = ck_ref[...].astype(BF16)
            dv_ref[...] = cv_ref[...].astype(BF16)

    q_spec, kp_spec, kc_spec, b_spec = _attn_specs(dil, kvw, nb, True)
    dkv_spec = pl.BlockSpec((BLK, kvw), lambda r, i: (jnp.maximum(i - 1, 0), r))
    qv = lambda a: a.reshape(L, dil * WIDTH)
    kvv = lambda a: a.reshape(L, dil * kvw)
    dq, dk, dv, dsum, dsk = pl.pallas_call(
        body, name=name, grid=(dil, nb + 1),
        in_specs=[pl.BlockSpec(memory_space=pltpu.SMEM), q_spec, kp_spec, kc_spec, kp_spec, kc_spec,
                  q_spec, q_spec, q_spec, b_spec],
        out_specs=[q_spec, dkv_spec, dkv_spec,
                   pl.BlockSpec((N_HEADS, BLK, 2 * BLK), lambda r, i: (0, 0, 0)),
                   pl.BlockSpec((BLK, WIDTH), lambda r, i: (0, 0))],
        out_shape=[jax.ShapeDtypeStruct((L, dil * WIDTH), BF16),
                   jax.ShapeDtypeStruct((L, dil * kvw), BF16),
                   jax.ShapeDtypeStruct((L, dil * kvw), BF16),
                   jax.ShapeDtypeStruct((N_HEADS, BLK, 2 * BLK), F32),
                   jax.ShapeDtypeStruct((BLK, WIDTH), F32)],
        scratch_shapes=[pltpu.VMEM((BLK, kvw), F32), pltpu.VMEM((BLK, kvw), F32)],
        compiler_params=_params(("arbitrary", "arbitrary")),
    )(sinks, qv(q), kvv(k), kvv(k), kvv(v), kvv(v), qv(o), qv(do), qv(lse), bias)
    return dq.reshape(T, WIDTH), dk.reshape(T, kvw), dv.reshape(T, kvw), dsum, dsk


def _merge_wo(x, oa, o1, o2, o3, l1, l2, l3, ga, gb, wo, *, tm=512):
    T = x.shape[0]

    def body(x_ref, oa_ref, o1_ref, o2_ref, o3_ref, l1_ref, l2_ref, l3_ref, ga_ref, gb_ref, wo_ref,
             x2_ref, ob_ref, lse_ref, mix_ref):
        la, lb, lc = l1_ref[...], l2_ref[...], l3_ref[...]
        m = jnp.maximum(jnp.maximum(la, lb), lc)
        ea, eb, ec = jnp.exp(la - m), jnp.exp(lb - m), jnp.exp(lc - m)
        den = ea + eb + ec
        inv = 1.0 / den
        ob = (ea * o1_ref[...].astype(F32) + eb * o2_ref[...].astype(F32) + ec * o3_ref[...].astype(F32)) * inv
        lse_ref[...] = m + jnp.log(den)
        ob_ref[...] = ob.astype(BF16)
        oav = oa_ref[...].astype(F32)
        ra = lax.rsqrt(jnp.mean(oav * oav, axis=-1, keepdims=True) + EPS)
        rb = lax.rsqrt(jnp.mean(ob * ob, axis=-1, keepdims=True) + EPS)
        mix_ref[:, :WIDTH] = (oav * ra * ga_ref[...]).astype(BF16)
        mix_ref[:, WIDTH:] = (ob * rb * gb_ref[...]).astype(BF16)
        x2_ref[...] = x_ref[...] + jnp.dot(mix_ref[...], wo_ref[...], preferred_element_type=F32)

    row = lambda w_: pl.BlockSpec((tm, w_), lambda i: (i, 0))
    full = lambda a: pl.BlockSpec(a.shape, lambda i: (0, 0))
    return pl.pallas_call(
        body, name="merge_wo", grid=(T // tm,),
        in_specs=[row(D_MODEL)] + [row(WIDTH)] * 7 + [full(ga), full(gb), full(wo)],
        out_specs=[row(D_MODEL), row(WIDTH), row(WIDTH), row(D_MODEL)],
        out_shape=[jax.ShapeDtypeStruct((T, D_MODEL), F32), jax.ShapeDtypeStruct((T, WIDTH), BF16),
                   jax.ShapeDtypeStruct((T, WIDTH), F32), jax.ShapeDtypeStruct((T, D_MODEL), BF16)],
        compiler_params=_params(("arbitrary",)),
    )(x, oa, o1, o2, o3, l1, l2, l3, ga, gb, wo)


def _ffn_up(x2, g, wg, wu, *, tm=1024, fc=256):
    T = x2.shape[0]

    def body(x_ref, g_ref, wg_ref, wu_ref, h_ref, gate_ref, up_ref, act_ref):
        @pl.when(pl.program_id(1) == 0)
        def _():
            xv = x_ref[...]
            r = lax.rsqrt(jnp.mean(xv * xv, axis=-1, keepdims=True) + EPS)
            h_ref[...] = (xv * r * g_ref[...]).astype(BF16)

        h = h_ref[...]
        gt = jnp.dot(h, wg_ref[...], preferred_element_type=F32)
        u = jnp.dot(h, wu_ref[...], preferred_element_type=F32)
        gate_ref[...] = gt.astype(BF16)
        up_ref[...] = u.astype(BF16)
        act_ref[...] = (gt * (1.0 / (1.0 + jnp.exp(-gt))) * u).astype(BF16)

    rowd = pl.BlockSpec((tm, D_MODEL), lambda i, c: (i, 0))
    wcol = pl.BlockSpec((D_MODEL, fc), lambda i, c: (0, c))
    oc = pl.BlockSpec((tm, fc), lambda i, c: (i, c))
    return pl.pallas_call(
        body, name="ffn_up", grid=(T // tm, D_FF // fc),
        in_specs=[rowd, pl.BlockSpec(g.shape, lambda i, c: (0, 0)), wcol, wcol],
        out_specs=[rowd, oc, oc, oc],
        out_shape=[jax.ShapeDtypeStruct((T, D_MODEL), BF16)] + [jax.ShapeDtypeStruct((T, D_FF), BF16)] * 3,
        compiler_params=_params(("arbitrary", "arbitrary")),
    )(x2, g, wg, wu)


def _ffn_down_loss(act, wd, x2, tgt, g, *, tm=1024, fk=256, rc=256):
    T = x2.shape[0]
    nk = D_FF // fk

    def body(act_ref, wd_ref, x2_ref, tgt_ref, g_ref, dx_ref, dxb_ref, loss_ref, dg_ref, acc_ref):
        i = pl.program_id(0)
        k = pl.program_id(1)

        @pl.when((i == 0) & (k == 0))
        def _():
            loss_ref[...] = jnp.zeros_like(loss_ref)
            dg_ref[...] = jnp.zeros_like(dg_ref)

        @pl.when(k == 0)
        def _():
            acc_ref[...] = jnp.zeros_like(acc_ref)

        acc_ref[...] += jnp.dot(act_ref[...], wd_ref[...], preferred_element_type=F32)

        @pl.when(k == nk - 1)
        def _():
            gv = g_ref[...]
            lsum = jnp.zeros((1, 1), F32)
            dgs = jnp.zeros((1, D_MODEL), F32)
            for c in range(0, tm, rc):
                x3 = x2_ref[c:c + rc, :] + acc_ref[c:c + rc, :]
                r = lax.rsqrt(jnp.mean(x3 * x3, axis=-1, keepdims=True) + EPS)
                xh = x3 * r
                diff = xh * gv - tgt_ref[c:c + rc, :]
                lsum = lsum + jnp.sum(jnp.sum(diff * diff, axis=-1, keepdims=True), axis=0, keepdims=True)
                dy = diff * (1.0 / D_MODEL)
                dgs = dgs + jnp.sum(dy * xh, axis=0, keepdims=True)
                dx = _rms_bwd(dy, xh, r, gv)
                dx_ref[c:c + rc, :] = dx
                dxb_ref[c:c + rc, :] = dx.astype(BF16)
            loss_ref[...] += lsum * (0.5 / D_MODEL)
            dg_ref[...] += dgs

    rowd = pl.BlockSpec((tm, D_MODEL), lambda i, k: (i, 0))
    return pl.pallas_call(
        body, name="ffn_down_loss", grid=(T // tm, nk),
        in_specs=[pl.BlockSpec((tm, fk), lambda i, k: (i, k)), pl.BlockSpec((fk, D_MODEL), lambda i, k: (k, 0)),
                  rowd, rowd, pl.BlockSpec(g.shape, lambda i, k: (0, 0))],
        out_specs=[rowd, rowd, pl.BlockSpec((1, 1), lambda i, k: (0, 0)), pl.BlockSpec((1, D_MODEL), lambda i, k: (0, 0))],
        out_shape=[jax.ShapeDtypeStruct((T, D_MODEL), F32), jax.ShapeDtypeStruct((T, D_MODEL), BF16),
                   jax.ShapeDtypeStruct((1, 1), F32), jax.ShapeDtypeStruct((1, D_MODEL), F32)],
        scratch_shapes=[pltpu.VMEM((tm, D_MODEL), F32)],
        compiler_params=_params(("arbitrary", "arbitrary")),
    )(act, wd, x2, tgt, g)


def _ffn_bwd(dx3, dx3b, gate, up, wd, wg, wu, x2, g, *, tm=1024, fc=256, rc=256):
    T = x2.shape[0]
    nc = D_FF // fc

    def body(dx_ref, dxb_ref, gate_ref, up_ref, wd_ref, wg_ref, wu_ref, x2_ref, g_ref,
             dx2_ref, dx2b_ref, dgate_ref, dup_ref, dg_ref, acc_ref):
        i = pl.program_id(0)
        c = pl.program_id(1)

        @pl.when((i == 0) & (c == 0))
        def _():
            dg_ref[...] = jnp.zeros_like(dg_ref)

        dact = lax.dot_general(dxb_ref[...], wd_ref[...], NT, preferred_element_type=F32)
        gt = gate_ref[...].astype(F32)
        u = up_ref[...].astype(F32)
        sg = 1.0 / (1.0 + jnp.exp(-gt))
        dgate = (dact * u * sg * (1.0 + gt * (1.0 - sg))).astype(BF16)
        dup = (dact * gt * sg).astype(BF16)
        dgate_ref[...] = dgate
        dup_ref[...] = dup
        part = (lax.dot_general(dgate, wg_ref[...], NT, preferred_element_type=F32)
                + lax.dot_general(dup, wu_ref[...], NT, preferred_element_type=F32))

        @pl.when(c == 0)
        def _():
            acc_ref[...] = part

        @pl.when(c > 0)
        def _():
            acc_ref[...] += part

        @pl.when(c == nc - 1)
        def _():
            gv = g_ref[...]
            dgs = jnp.zeros((1, D_MODEL), F32)
            for s in range(0, tm, rc):
                xv = x2_ref[s:s + rc, :]
                r = lax.rsqrt(jnp.mean(xv * xv, axis=-1, keepdims=True) + EPS)
                xh = xv * r
                dh = acc_ref[s:s + rc, :]
                dgs = dgs + jnp.sum(dh * xh, axis=0, keepdims=True)
                d = dx_ref[s:s + rc, :] + _rms_bwd(dh, xh, r, gv)
                dx2_ref[s:s + rc, :] = d
                dx2b_ref[s:s + rc, :] = d.astype(BF16)
            dg_ref[...] += dgs

    rowd = pl.BlockSpec((tm, D_MODEL), lambda i, c: (i, 0))
    oc = pl.BlockSpec((tm, fc), lambda i, c: (i, c))
    wcol = pl.BlockSpec((D_MODEL, fc), lambda i, c: (0, c))
    return pl.pallas_call(
        body, name="ffn_bwd", grid=(T // tm, nc),
        in_specs=[rowd, rowd, oc, oc, pl.BlockSpec((fc, D_MODEL), lambda i, c: (c, 0)), wcol, wcol, rowd,
                  pl.BlockSpec(g.shape, lambda i, c: (0, 0))],
        out_specs=[rowd, rowd, oc, oc, pl.BlockSpec((1, D_MODEL), lambda i, c: (0, 0))],
        out_shape=[jax.ShapeDtypeStruct((T, D_MODEL), F32), jax.ShapeDtypeStruct((T, D_MODEL), BF16),
                   jax.ShapeDtypeStruct((T, D_FF), BF16), jax.ShapeDtypeStruct((T, D_FF), BF16),
                   jax.ShapeDtypeStruct((1, D_MODEL), F32)],
        scratch_shapes=[pltpu.VMEM((tm, D_MODEL), F32)],
        compiler_params=_params(("arbitrary", "arbitrary")),
    )(dx3, dx3b, gate, up, wd, wg, wu, x2, g)


def _matmul_tn(a, b, *, tk, tn, tt=512, name):
    T, K = a.shape
    N = b.shape[1]
    nt = T // tt

    def body(a_ref, b_ref, o_ref):
        part = lax.dot_general(a_ref[...], b_ref[...], TN, preferred_element_type=F32)

        @pl.when(pl.program_id(2) == 0)
        def _():
            o_ref[...] = part

        @pl.when(pl.program_id(2) > 0)
        def _():
            o_ref[...] += part

    return pl.pallas_call(
        body, name=name, grid=(K // tk, N // tn, nt),
        in_specs=[pl.BlockSpec((tt, tk), lambda i, j, t: (t, i)), pl.BlockSpec((tt, tn), lambda i, j, t: (t, j))],
        out_specs=pl.BlockSpec((tk, tn), lambda i, j, t: (i, j)),
        out_shape=jax.ShapeDtypeStruct((K, N), F32),
        compiler_params=_params(("arbitrary", "arbitrary", "arbitrary")),
    )(a, b)


def _wo_bwd(dx2b, wo, oa, ob, ga, gb, *, tm=512):
    T = dx2b.shape[0]

    def body(dx_ref, wo_ref, oa_ref, ob_ref, ga_ref, gb_ref, doa_ref, dob_ref, dga_ref, dgb_ref):
        @pl.when(pl.program_id(0) == 0)
        def _():
            dga_ref[...] = jnp.zeros_like(dga_ref)
            dgb_ref[...] = jnp.zeros_like(dgb_ref)

        dm = lax.dot_general(dx_ref[...], wo_ref[...], NT, preferred_element_type=F32)
        for o_ref, g_ref, do_ref, dg_ref, sl in ((oa_ref, ga_ref, doa_ref, dga_ref, slice(0, WIDTH)),
                                                 (ob_ref, gb_ref, dob_ref, dgb_ref, slice(WIDTH, 2 * WIDTH))):
            ov = o_ref[...].astype(F32)
            r = lax.rsqrt(jnp.mean(ov * ov, axis=-1, keepdims=True) + EPS)
            xh = ov * r
            d = dm[:, sl]
            dg_ref[...] += jnp.sum(d * xh, axis=0, keepdims=True)
            do_ref[...] = _rms_bwd(d, xh, r, g_ref[...]).astype(BF16)

    row = lambda w_: pl.BlockSpec((tm, w_), lambda i: (i, 0))
    full = lambda a: pl.BlockSpec(a.shape, lambda i: (0, 0))
    return pl.pallas_call(
        body, name="wo_bwd", grid=(T // tm,),
        in_specs=[row(D_MODEL), full(wo), row(WIDTH), row(WIDTH), full(ga), full(gb)],
        out_specs=[row(WIDTH), row(WIDTH), pl.BlockSpec((1, WIDTH), lambda i: (0, 0)), pl.BlockSpec((1, WIDTH), lambda i: (0, 0))],
        out_shape=[jax.ShapeDtypeStruct((T, WIDTH), BF16), jax.ShapeDtypeStruct((T, WIDTH), BF16),
                   jax.ShapeDtypeStruct((1, WIDTH), F32), jax.ShapeDtypeStruct((1, WIDTH), F32)],
        compiler_params=_params(("arbitrary",)),
    )(dx2b, wo, oa, ob, ga, gb)


def _dproj(dqa, dka, dva, dqs, dks, dvs, cos, sin, *, tm=512):
    T = dqa.shape[0]

    def body(dqa_ref, dka_ref, dva_ref, q1, q2, q3, k1, k2, k3, v1, v2, v3, cos_ref, sin_ref, dp_ref, db_ref):
        @pl.when(pl.program_id(0) == 0)
        def _():
            db_ref[...] = jnp.zeros_like(db_ref)

        cosv = cos_ref[...]
        sinv = sin_ref[...]
        lane = lax.broadcasted_iota(jnp.int32, (tm, PAIR), 1)
        first = (lane % HEAD_DIM) < (HEAD_DIM // 2)

        def put(off, val):
            dp_ref[:, off:off + PAIR] = val.astype(BF16)
            db_ref[:, off:off + PAIR] += jnp.sum(val, axis=0, keepdims=True)

        for src, off, width in ((dqa_ref, 0, 512), (dka_ref, 512, 256)):
            for j in range(0, width, PAIR):
                d = src[:, j:j + PAIR].astype(F32)
                put(off + j, d * cosv - _rope_rot(d, first) * sinv)
        for j in range(0, 256, PAIR):
            put(768 + j, dva_ref[:, j:j + PAIR].astype(F32))
        for (a, b, c), off in (((q1, q2, q3), 1024), ((k1, k2, k3), 1536), ((v1, v2, v3), 2048)):
            for j in range(0, WIDTH, PAIR):
                sl = slice(j, j + PAIR)
                put(off + j, a[:, sl].astype(F32) + b[:, sl].astype(F32) + c[:, sl].astype(F32))

    row = lambda w_: pl.BlockSpec((tm, w_), lambda i: (i, 0))
    return pl.pallas_call(
        body, name="dproj", grid=(T // tm,),
        in_specs=[row(512), row(256), row(256)] + [row(WIDTH)] * 9 + [row(PAIR), row(PAIR)],
        out_specs=[row(D_INP), pl.BlockSpec((1, D_INP), lambda i: (0, 0))],
        out_shape=[jax.ShapeDtypeStruct((T, D_INP), BF16), jax.ShapeDtypeStruct((1, D_INP), F32)],
        compiler_params=_params(("arbitrary",)),
    )(dqa, dka, dva, *dqs, *dks, *dvs, cos, sin)


def _inproj_bwd(dp, w, x, dx2, g, *, tm=512):
    T = x.shape[0]

    def body(dp_ref, w_ref, x_ref, dx2_ref, g_ref, gx_ref, dg_ref):
        @pl.when(pl.program_id(0) == 0)
        def _():
            dg_ref[...] = jnp.zeros_like(dg_ref)

        dh = lax.dot_general(dp_ref[...], w_ref[...], NT, preferred_element_type=F32)
        xv = x_ref[...]
        r = lax.rsqrt(jnp.mean(xv * xv, axis=-1, keepdims=True) + EPS)
        xh = xv * r
        dg_ref[...] += jnp.sum(dh * xh, axis=0, keepdims=True)
        gx_ref[...] = dx2_ref[...] + _rms_bwd(dh, xh, r, g_ref[...])

    row = lambda w_: pl.BlockSpec((tm, w_), lambda i: (i, 0))
    full = lambda a: pl.BlockSpec(a.shape, lambda i: (0, 0))
    return pl.pallas_call(
        body, name="inproj_bwd", grid=(T // tm,),
        in_specs=[row(D_INP), full(w), row(D_MODEL), row(D_MODEL), full(g)],
        out_specs=[row(D_MODEL), pl.BlockSpec((1, D_MODEL), lambda i: (0, 0))],
        out_shape=[jax.ShapeDtypeStruct((T, D_MODEL), F32), jax.ShapeDtypeStruct((1, D_MODEL), F32)],
        compiler_params=_params(("arbitrary",)),
    )(dp, w, x, dx2, g)


def _bias_sink_grads(dsums, bmaps, dsk):
    def body(s1, s2, s3, m1, m2, m3, dsk_ref, drel_ref, dsink_ref):
        row = lax.broadcasted_iota(jnp.int32, (N_HEADS, 128), 0)
        lane = lax.broadcasted_iota(jnp.int32, (N_HEADS, 128), 1)
        out = jnp.zeros((N_HEADS, 128), F32)
        for s_ref, m_ref in ((s1, m1), (s2, m2), (s3, m3)):
            bm = m_ref[...]
            for h in range(N_HEADS):
                a = s_ref[h]
                for b in range(REL_BUCKETS):
                    v = jnp.sum(jnp.sum(jnp.where(bm == b, a, 0.0), axis=-1, keepdims=True), axis=0, keepdims=True)
                    out = out + jnp.where((row == h) & (lane == b), v, 0.0)
        drel_ref[...] = out
        dsink_ref[...] = -jnp.sum(dsk_ref[...], axis=0, keepdims=True)

    vm = pl.BlockSpec(memory_space=pltpu.VMEM)
    return pl.pallas_call(
        body, name="bias_sink_grads",
        in_specs=[vm] * 7, out_specs=[vm, vm],
        out_shape=[jax.ShapeDtypeStruct((N_HEADS, 128), F32), jax.ShapeDtypeStruct((1, WIDTH), F32)],
        compiler_params=_params(),
    )(*dsums, *bmaps, dsk)


def _all_gather(blk, *, name):
    R, C = blk.shape

    def body(x_ref, out_ref, send_sems, recv_sems, local_sem):
        x, y, c = lax.axis_index("x"), lax.axis_index("y"), lax.axis_index("c")
        me, sibling = (x, y, c), (x, y, 1 - c)
        chips = [(1 - x, y), (x, 1 - y), (1 - x, 1 - y)]

        def slot(px, py, pc):
            return out_ref.at[4 * px + 2 * py + pc]

        def copy(k, block, to, src=None):
            return pltpu.make_async_remote_copy(
                src_ref=slot(*block) if src is None else src, dst_ref=slot(*block),
                send_sem=send_sems.at[k], recv_sem=recv_sems.at[k], device_id=to, device_id_type=MESH)

        mine = pltpu.make_async_copy(x_ref, slot(*me), local_sem)
        mine.start()
        first = [copy(0, me, sibling, src=x_ref)]
        first += [copy(1 + j, me, (*chip, c), src=x_ref) for j, chip in enumerate(chips)]
        for cp in first:
            cp.start()
        passed = [copy(4 + j, (*chip, c), sibling) for j, chip in enumerate(chips)]
        for j, chip in enumerate(chips):
            copy(1 + j, (*chip, c), me).wait_recv()
            passed[j].start()
        copy(0, sibling, me).wait_recv()
        for j, chip in enumerate(chips):
            copy(4 + j, (*chip, 1 - c), me).wait_recv()
        for cp in first + passed:
            cp.wait_send()
        mine.wait()

    return pl.pallas_call(
        body, name=name,
        in_specs=[pl.BlockSpec(memory_space=pl.ANY)], out_specs=pl.BlockSpec(memory_space=pl.ANY),
        out_shape=jax.ShapeDtypeStruct((N_DEV, R, C), blk.dtype),
        scratch_shapes=[pltpu.SemaphoreType.DMA((7,)), pltpu.SemaphoreType.DMA((7,)), pltpu.SemaphoreType.DMA],
        compiler_params=pltpu.CompilerParams(has_side_effects=True),
    )(blk)


def _scatter_blocks(parts, *, name):
    _, R, C = parts.shape

    def body(p_ref, out_ref, send_sems, recv_sems, local_sem):
        x, y, c = lax.axis_index("x"), lax.axis_index("y"), lax.axis_index("c")
        mine = 4 * x + 2 * y + c
        peers = [(x ^ (k >> 2), y ^ ((k >> 1) & 1), c ^ (k & 1)) for k in range(1, N_DEV)]

        def copy(k, peer):
            dest = 4 * peer[0] + 2 * peer[1] + peer[2]
            return pltpu.make_async_remote_copy(
                src_ref=p_ref.at[dest], dst_ref=out_ref.at[mine],
                send_sem=send_sems.at[k], recv_sem=recv_sems.at[k], device_id=peer, device_id_type=MESH)

        own = pltpu.make_async_copy(p_ref.at[mine], out_ref.at[mine], local_sem)
        own.start()
        cps = [copy(k, peer) for k, peer in enumerate(peers)]
        for cp in cps:
            cp.start()
        for k, peer in enumerate(peers):
            src = 4 * peer[0] + 2 * peer[1] + peer[2]
            pltpu.make_async_remote_copy(
                src_ref=p_ref.at[src], dst_ref=out_ref.at[src],
                send_sem=send_sems.at[k], recv_sem=recv_sems.at[k], device_id=peer, device_id_type=MESH).wait_recv()
        for cp in cps:
            cp.wait_send()
        own.wait()

    return pl.pallas_call(
        body, name=name,
        in_specs=[pl.BlockSpec(memory_space=pl.ANY)], out_specs=pl.BlockSpec(memory_space=pl.ANY),
        out_shape=jax.ShapeDtypeStruct(parts.shape, parts.dtype),
        scratch_shapes=[pltpu.SemaphoreType.DMA((7,)), pltpu.SemaphoreType.DMA((7,)), pltpu.SemaphoreType.DMA],
        compiler_params=pltpu.CompilerParams(has_side_effects=True),
    )(parts)


def _adam_math(w, g, m, v):
    m = ADAM_B1 * m + (1.0 - ADAM_B1) * g
    v = ADAM_B2 * v + (1.0 - ADAM_B2) * (g * g)
    m_hat = m / (1.0 - ADAM_B1 ** ADAM_STEP)
    v_hat = v / (1.0 - ADAM_B2 ** ADAM_STEP)
    delta = -ADAM_LR * (m_hat / (jnp.sqrt(v_hat) + ADAM_EPS) + ADAM_WD * w)
    return delta, m, v


def _adamw(parts, w, m, v, *, name):
    R, C = w.shape
    tr = R // 2
    assert tr % 16 == 0

    def body(p_ref, w_ref, m_ref, v_ref, g_ref, d_ref, nm_ref, nv_ref):
        g = p_ref[0].astype(F32)
        for s in range(1, N_DEV):
            g = g + p_ref[s].astype(F32)
        d, nm, nv = _adam_math(w_ref[...], g, m_ref[...], v_ref[...])
        g_ref[...] = g
        d_ref[...] = d
        nm_ref[...] = nm
        nv_ref[...] = nv

    blk = pl.BlockSpec((tr, C), lambda i: (i, 0))
    return pl.pallas_call(
        body, name=name, grid=(R // tr,),
        in_specs=[pl.BlockSpec((N_DEV, tr, C), lambda i: (0, i, 0)), blk, blk, blk],
        out_specs=[blk] * 4, out_shape=[jax.ShapeDtypeStruct((R, C), F32)] * 4,
        compiler_params=_params(("arbitrary",)),
    )(parts, w, m, v)


def _adamw_small(parts, w, m, v):
    def body(p_ref, w_ref, m_ref, v_ref, g_ref, d_ref, nm_ref, nv_ref):
        g = p_ref[0]
        for s in range(1, N_DEV):
            g = g + p_ref[s]
        d, nm, nv = _adam_math(w_ref[...], g, m_ref[...], v_ref[...])
        g_ref[...] = g
        d_ref[...] = d
        nm_ref[...] = nm
        nv_ref[...] = nv

    vm = pl.BlockSpec(memory_space=pltpu.VMEM)
    return pl.pallas_call(
        body, name="adamw_small", in_specs=[vm] * 4, out_specs=[vm] * 4,
        out_shape=[jax.ShapeDtypeStruct((SMALL_ROWS, 128), F32)] * 4, compiler_params=_params(),
    )(parts, w, m, v)


def _t5_bucket(dist):
    max_exact = REL_BUCKETS // 2
    df = jnp.maximum(dist, 1).astype(F32)
    large = max_exact + (jnp.log(df / max_exact) / math.log(REL_MAX_DISTANCE / max_exact)
                         * (REL_BUCKETS - max_exact)).astype(jnp.int32)
    large = jnp.minimum(large, REL_BUCKETS - 1)
    return jnp.where(dist < max_exact, dist, large)


def _band_tables(rel_table, dil, n_back):
    qi = jnp.arange(BLK)[:, None]
    kj = jnp.arange(2 * BLK)[None, :]
    delta = BLK + qi - kj
    in_band = (delta >= 0) & (delta <= n_back)
    if rel_table is None:
        vals = jnp.zeros((N_HEADS, BLK, 2 * BLK), F32)
        bmap = None
    else:
        bucket = _t5_bucket(jnp.clip(delta, 0, n_back) * dil)
        vals = jnp.zeros((N_HEADS, BLK, 2 * BLK), F32)
        for b in range(REL_BUCKETS):
            vals = jnp.where((bucket == b)[None], rel_table[b][:, None, None], vals)
        bmap = jnp.where(in_band, bucket, -1).astype(jnp.int32)
    later = jnp.where(in_band[None], vals, NEG)
    first = jnp.where((in_band & (kj >= BLK))[None], vals, NEG)
    return jnp.stack([later, first]), bmap


def _rope_tables(T):
    half = HEAD_DIM // 2
    inv_freq = ROPE_THETA ** (-jnp.arange(half, dtype=F32) / half)
    ang = jnp.arange(T, dtype=F32)[:, None] * inv_freq[None, :]
    cos, sin = jnp.cos(ang), jnp.sin(ang)
    return jnp.tile(cos, (1, 4)), jnp.tile(jnp.concatenate([-sin, sin], axis=1), (1, 2))


def _dup_heads(a):
    h0, h1 = a[..., :HEAD_DIM], a[..., HEAD_DIM:]
    return jnp.concatenate([h0, h0, h1, h1], axis=-1)


def _widen_in(a):
    return jnp.concatenate([a[..., :512], _dup_heads(a[..., 512:640]), _dup_heads(a[..., 640:768]), a[..., 768:]], axis=-1)


def _fold_in(a):
    def fold(t):
        return jnp.concatenate([t[..., 0:64] + t[..., 64:128], t[..., 128:192] + t[..., 192:256]], axis=-1)
    return jnp.concatenate([a[..., :512], fold(a[..., 512:768]), fold(a[..., 768:1024]), a[..., 1024:]], axis=-1)


def _local_step(x, tgt, g_attn, win, b_in, sinks, rel_table, g_out_a, g_out_b, wo, g_ffn, wg, wu, wd, g_final):
    T = x.shape[0]
    cos, sin = _rope_tables(T)
    winp = _widen_in(win)
    binp = _widen_in(b_in)
    g_final2 = g_final.reshape(1, D_MODEL)
    sink8 = sinks.reshape(N_HEADS)

    bias_a, _ = _band_tables(None, 1, BLK - 1)
    tabs = [_band_tables(rel_table, dil, window // dil) for window, dil in BRANCHES]

    h1, qa, ka, va, qb, kb, vb = _norm_proj(x, g_attn, winp, binp, cos, sin)
    oa, lse_a = _attn_fwd(qa, ka, va, bias_a, sink8, dil=1, kv_pairs=2, use_sink=True, name="attn_a_fwd")
    outs = [_attn_fwd(qb, kb, vb, tabs[n][0], sink8, dil=dil, kv_pairs=4, use_sink=False, name=f"attn_b{n}_fwd")
            for n, (_, dil) in enumerate(BRANCHES)]
    x2, ob, lse_b, mixed = _merge_wo(x, oa, outs[0][0], outs[1][0], outs[2][0], outs[0][1], outs[1][1], outs[2][1],
                                     g_out_a, g_out_b, wo)
    h2, gate, up, act = _ffn_up(x2, g_ffn, wg, wu)
    dx3, dx3b, loss, dg_final = _ffn_down_loss(act, wd, x2, tgt, g_final2)

    dx2, dx2b, dgate, dup, dg_ffn = _ffn_bwd(dx3, dx3b, gate, up, wd, wg, wu, x2, g_ffn)
    dwd = _matmul_tn(act, dx3b, tk=1408, tn=1024, name="dw_down")
    dwg = _matmul_tn(h2, dgate, tk=1024, tn=1408, name="dw_gate")
    dwu = _matmul_tn(h2, dup, tk=1024, tn=1408, name="dw_up")
    dwo = _matmul_tn(mixed, dx2b, tk=1024, tn=1024, name="dw_o")
    doa, dob, dg_out_a, dg_out_b = _wo_bwd(dx2b, wo, oa, ob, g_out_a, g_out_b)

    dqa, dka, dva, _, dsk = _attn_bwd(qa, ka, va, oa, doa, lse_a, bias_a, sink8, dil=1, kv_pairs=2, use_sink=True,
                                      name="attn_a_bwd")
    res = [_attn_bwd(qb, kb, vb, ob, dob, lse_b, tabs[n][0], sink8, dil=dil, kv_pairs=4, use_sink=False,
                     name=f"attn_b{n}_bwd") for n, (_, dil) in enumerate(BRANCHES)]
    dp, dbp = _dproj(dqa, dka, dva, [r[0] for r in res], [r[1] for r in res], [r[2] for r in res], cos, sin)
    grad_x, dg_attn = _inproj_bwd(dp, winp, x, dx2, g_attn)
    dwin = _fold_in(_matmul_tn(h1, dp, tk=1024, tn=1280, name="dw_in"))
    drel, dsink = _bias_sink_grads([r[3] for r in res], [t[1] for t in tabs], dsk)

    small = dict(
        g_attn=dg_attn, b_in=_fold_in(dbp), sinks=dsink[:, ::HEAD_DIM], rel_table=drel[:, :REL_BUCKETS].T,
        g_out_a=dg_out_a, g_out_b=dg_out_b, g_ffn=dg_ffn, g_final=dg_final.reshape(D_MODEL))
    return loss[0, 0], grad_x, dict(w_in=dwin, w_o=dwo, w_gate=dwg, w_up=dwu, w_down=dwd), small


SMALL_NAMES = ("g_attn", "b_in", "sinks", "rel_table", "g_out_a", "g_out_b", "g_ffn", "g_final")


def _pack_small(vals):
    flat = jnp.concatenate([vals[n].reshape(-1).astype(F32) for n in SMALL_NAMES])
    return jnp.pad(flat, (0, SMALL_ROWS * 128 - flat.shape[0])).reshape(SMALL_ROWS, 128)


def _unpack_small(packed, like):
    flat = packed.reshape(-1)
    out, off = {}, 0
    for n in SMALL_NAMES:
        size = like[n].size
        out[n] = flat[off:off + size].reshape(like[n].shape)
        off += size
    return out


def _cols_to_rows(a):
    return a.reshape(a.shape[1], D_MODEL)


def kernel(x, g_attn, w_in, b_in, sinks, rel_table, g_out_a, g_out_b, w_o, g_ffn, w_gate, w_up, w_down, g_final, loss_target, m_g_attn, m_w_in, m_b_in, m_sinks, m_rel_table, m_g_out_a, m_g_out_b, m_w_o, m_g_ffn, m_w_gate, m_w_up, m_w_down, m_g_final, v_g_attn, v_w_in, v_b_in, v_sinks, v_rel_table, v_g_out_a, v_g_out_b, v_w_o, v_g_ffn, v_w_gate, v_w_up, v_w_down, v_g_final):
    T = x.shape[1]
    flat = jnp.concatenate([_cols_to_rows(w_in[0].astype(BF16)), w_o[0].astype(BF16), _cols_to_rows(w_gate[0].astype(BF16)),
                            _cols_to_rows(w_up[0].astype(BF16)), w_down[0].astype(BF16)], axis=0)
    gw = _all_gather(flat, name="gather_weights")
    o0, o1, o2, o3, o4 = 0, 288, 416, 768, 1120

    def cols(seg, n):
        return seg.reshape(N_DEV, D_MODEL, n).transpose(1, 0, 2).reshape(D_MODEL, N_DEV * n)

    win = cols(gw[:, o0:o1], 288)
    wo = gw[:, o1:o2].reshape(D_MODEL, D_MODEL)
    wg = cols(gw[:, o2:o3], 352)
    wu = cols(gw[:, o3:o4], 352)
    wd = gw[:, o4:].reshape(D_FF, D_MODEL)

    loss_part, grad_x, dws, small = _local_step(
        x[0], loss_target[0], g_attn, win, b_in, sinks, rel_table, g_out_a, g_out_b, wo, g_ffn, wg, wu, wd, g_final)
    loss = lax.psum(loss_part, ("x", "y", "c"))

    def col_parts(g, n):
        return g.astype(BF16).reshape(D_MODEL, N_DEV, n).transpose(1, 0, 2).reshape(N_DEV, n, D_MODEL)

    parts = jnp.concatenate([
        col_parts(dws["w_in"], 288), dws["w_o"].astype(BF16).reshape(N_DEV, 128, D_MODEL),
        col_parts(dws["w_gate"], 352), col_parts(dws["w_up"], 352),
        dws["w_down"].astype(BF16).reshape(N_DEV, 352, D_MODEL)], axis=1)
    got = _scatter_blocks(parts, name="scatter_grads")

    shards = dict(w_in=(w_in, m_w_in, v_w_in, o0, o1, True), w_o=(w_o, m_w_o, v_w_o, o1, o2, False),
                  w_gate=(w_gate, m_w_gate, v_w_gate, o2, o3, True), w_up=(w_up, m_w_up, v_w_up, o3, o4, True),
                  w_down=(w_down, m_w_down, v_w_down, o4, FLAT_ROWS, False))
    big = {}
    for n, (w, m, v, lo, hi, by_cols) in shards.items():
        p = got[:, lo:hi]
        if by_cols:
            p = p.reshape(N_DEV, D_MODEL, hi - lo)
        big[n] = [a[None] for a in _adamw(p, w[0], m[0], v[0], name="adamw_" + n)]

    ws = dict(g_attn=g_attn, b_in=b_in, sinks=sinks, rel_table=rel_table, g_out_a=g_out_a, g_out_b=g_out_b,
              g_ffn=g_ffn, g_final=g_final)
    ms = dict(g_attn=m_g_attn, b_in=m_b_in, sinks=m_sinks, rel_table=m_rel_table, g_out_a=m_g_out_a,
              g_out_b=m_g_out_b, g_ffn=m_g_ffn, g_final=m_g_final)
    vs = dict(g_attn=v_g_attn, b_in=v_b_in, sinks=v_sinks, rel_table=v_rel_table, g_out_a=v_g_out_a,
              g_out_b=v_g_out_b, g_ffn=v_g_ffn, g_final=v_g_final)
    sparts = _all_gather(_pack_small(small), name="gather_small")
    sm = [_unpack_small(a, ws) for a in _adamw_small(sparts, _pack_small(ws), _pack_small(ms), _pack_small(vs))]

    order = ("g_attn", "w_in", "b_in", "sinks", "rel_table", "g_out_a", "g_out_b", "w_o", "g_ffn", "w_gate", "w_up",
             "w_down", "g_final")
    outs = [loss, grad_x[None]]
    for k in range(4):
        outs += [big[n][k] if n in big else sm[k][n] for n in order]
    return tuple(outs)
```

```python
import functools
import math

import jax
import jax.numpy as jnp
from jax import lax
from jax.experimental import pallas as pl
from jax.experimental.pallas import tpu as pltpu

F32 = jnp.float32
BF16 = jnp.bfloat16

N_DEV = 8
D_MODEL = 1024
HEAD_DIM = 64
N_HEADS = 8
PAIR = 2 * HEAD_DIM
WIDTH = N_HEADS * HEAD_DIM
D_IN = 2304
D_INP = 2560
D_FF = 2816
BLK = 128
ROPE_THETA = 150000.0
REL_BUCKETS = 32
REL_MAX_DISTANCE = 2048
EPS = 1e-5
NEG = -1e30
BRANCHES = ((128, 1), (512, 4), (2048, 16))
Q_SCALE = HEAD_DIM ** -0.5

ADAM_LR = 0.001
ADAM_B1 = 0.9
ADAM_B2 = 0.999
ADAM_EPS = 1e-08
ADAM_WD = 0.01
ADAM_STEP = 10

VMEM_LIMIT = 56 * 1024 * 1024
MESH = pl.DeviceIdType.MESH

NT = (((1,), (1,)), ((), ()))
TN = (((0,), (0,)), ((), ()))

SEG_ROWS = (288, 128, 352, 352, 352)
FLAT_ROWS = sum(SEG_ROWS)
SMALL_ROWS = 56


def _params(sem=None):
    return pltpu.CompilerParams(dimension_semantics=sem, vmem_limit_bytes=VMEM_LIMIT)


def _rms_bwd(dh, xh, r, g):
    u = dh * g
    return r * (u - xh * jnp.mean(u * xh, axis=-1, keepdims=True))


def _rope_rot(t, first):
    return jnp.where(first, pltpu.roll(t, 96, 1), pltpu.roll(t, 32, 1))


N_CHUNK = WIDTH // PAIR


def _scr(tm):
    return pltpu.VMEM((N_CHUNK, tm, PAIR), F32)


def _scr_get(scr):
    return jnp.concatenate([scr[j] for j in range(N_CHUNK)], axis=1)


def _scr_put(scr, val):
    for j in range(N_CHUNK):
        scr[j] = val[:, j * PAIR:(j + 1) * PAIR]


def _unstride(view_ref, scr, dil, tm):
    n = tm // dil
    for r in range(dil):
        for j in range(N_CHUNK):
            col = r * WIDTH + j * PAIR
            scr.at[j][pl.ds(r, n, stride=dil), :] = view_ref[:, col:col + PAIR].astype(F32)


def _restride(scr, out_ref, dil, tm):
    n = tm // dil
    for r in range(dil):
        for j in range(N_CHUNK):
            col = r * WIDTH + j * PAIR
            rows = scr[j] if dil == 1 else scr.at[j][pl.ds(r, n, stride=dil), :]
            out_ref[:, col:col + PAIR] = rows.astype(out_ref.dtype)


def _view_specs(tm):
    return [pl.BlockSpec((tm // dil, dil * WIDTH), lambda i: (i, 0)) for _, dil in BRANCHES]


def _view_shapes(T, dtype):
    return [jax.ShapeDtypeStruct((T // dil, dil * WIDTH), dtype) for _, dil in BRANCHES]


def _norm_proj(x, g, w, b, cos, sin, *, tm=512):
    T = x.shape[0]

    def body(x_ref, g_ref, w_ref, b_ref, cos_ref, sin_ref, h_ref, qa_ref, ka_ref, va_ref, *rest):
        outs_b, ys = rest[:9], rest[9]
        xv = x_ref[...]
        r = lax.rsqrt(jnp.mean(xv * xv, axis=-1, keepdims=True) + EPS)
        h = (xv * r * g_ref[...]).astype(BF16)
        h_ref[...] = h
        cosv = cos_ref[...]
        sinv = sin_ref[...]
        lane = lax.broadcasted_iota(jnp.int32, (tm, PAIR), 1)
        first = (lane % HEAD_DIM) < (HEAD_DIM // 2)

        def proj(off):
            return jnp.dot(h, w_ref[:, off:off + 256], preferred_element_type=F32) + b_ref[:, off:off + 256]

        for (off, width, rot, scale), o_ref in zip(((0, 512, True, Q_SCALE), (512, 256, True, 1.0), (768, 256, False, 1.0)),
                                                   (qa_ref, ka_ref, va_ref)):
            for c in range(0, width, 256):
                y = proj(off + c)
                for j in range(0, 256, PAIR):
                    t = y[:, j:j + PAIR]
                    if rot:
                        t = t * cosv + _rope_rot(t, first) * sinv
                    if scale != 1.0:
                        t = t * scale
                    o_ref[:, c + j:c + j + PAIR] = t.astype(BF16)
        for n, (off, scale) in enumerate(((1024, Q_SCALE), (1536, 1.0), (2048, 1.0))):
            for c in range(0, WIDTH, 256):
                y = proj(off + c)
                y = y * scale if scale != 1.0 else y
                for j in range(0, 256, PAIR):
                    ys[(c + j) // PAIR] = y[:, j:j + PAIR]
            for (_, dil), o_ref in zip(BRANCHES, outs_b[3 * n:3 * n + 3]):
                _restride(ys, o_ref, dil, tm)

    row = lambda w_: pl.BlockSpec((tm, w_), lambda i: (i, 0))
    full = lambda a: pl.BlockSpec(a.shape, lambda i: (0, 0))
    return pl.pallas_call(
        body, name="norm_proj", grid=(T // tm,),
        in_specs=[row(D_MODEL), full(g), full(w), full(b), row(PAIR), row(PAIR)],
        out_specs=[row(D_MODEL), row(512), row(256), row(256)] + _view_specs(tm) * 3,
        out_shape=[jax.ShapeDtypeStruct((T, n), BF16) for n in (D_MODEL, 512, 256, 256)] + _view_shapes(T, BF16) * 3,
        scratch_shapes=[_scr(tm)],
        compiler_params=_params(("arbitrary",)),
    )(x, g, w, b, cos, sin)


def _attn_specs(dil, kvw, nb, clamp):
    qi = (lambda r, i: (jnp.minimum(i, nb - 1), r)) if clamp else (lambda r, i: (i, r))
    q_spec = pl.BlockSpec((BLK, WIDTH), qi)
    kp_spec = pl.BlockSpec((BLK, kvw), lambda r, i: (jnp.minimum(jnp.maximum(i - 1, 0), nb - 1), r))
    kc_spec = pl.BlockSpec((BLK, kvw), qi)
    b_spec = pl.BlockSpec((1, N_HEADS, BLK, 2 * BLK), lambda r, i: (jnp.where(i == 0, 1, 0), 0, 0, 0))
    return q_spec, kp_spec, kc_spec, b_spec


def _attn_fwd(q, k, v, bias, sinks, *, dil, kv_pairs, use_sink, name):
    L = q.shape[0]
    nb = L // BLK
    kvw = kv_pairs * PAIR
    rep = 4 // kv_pairs

    def body(sink_ref, q_ref, kp_ref, kc_ref, vp_ref, vc_ref, b_ref, o_ref, lse_ref):
        lo = lax.broadcasted_iota(jnp.int32, (1, PAIR), 1) < HEAD_DIM
        for hp in range(4):
            sl = slice(hp * PAIR, (hp + 1) * PAIR)
            ksl = slice((hp // rep) * PAIR, (hp // rep + 1) * PAIR)
            qp = q_ref[:, sl]
            kk = jnp.concatenate([kp_ref[:, ksl], kc_ref[:, ksl]], axis=0)
            vv = jnp.concatenate([vp_ref[:, ksl], vc_ref[:, ksl]], axis=0)
            o_pair = None
            lse_pair = None
            for e in range(2):
                h = 2 * hp + e
                msk = lo if e == 0 else jnp.logical_not(lo)
                qm = jnp.where(msk, qp, jnp.zeros_like(qp))
                s = lax.dot_general(qm, kk, NT, preferred_element_type=F32) + b_ref[0, h]
                m = jnp.max(s, axis=-1, keepdims=True)
                if use_sink:
                    sk = sink_ref[h]
                    m = jnp.maximum(m, sk)
                p = jnp.exp(s - m)
                l = jnp.sum(p, axis=-1, keepdims=True)
                if use_sink:
                    l = l + jnp.exp(sk - m)
                vm = jnp.where(msk, vv, jnp.zeros_like(vv))
                oe = jnp.dot(p.astype(BF16), vm, preferred_element_type=F32) * (1.0 / l)
                ls = m + jnp.log(l)
                if e == 0:
                    o_pair = oe
                    lse_pair = jnp.broadcast_to(ls, (BLK, PAIR))
                else:
                    o_pair = o_pair + oe
                    lse_pair = jnp.where(msk, ls, lse_pair)
            o_ref[:, sl] = o_pair.astype(BF16)
            lse_ref[:, sl] = lse_pair

    q_spec, kp_spec, kc_spec, b_spec = _attn_specs(dil, kvw, nb, False)
    return pl.pallas_call(
        body, name=name, grid=(dil, nb),
        in_specs=[pl.BlockSpec(memory_space=pltpu.SMEM), q_spec, kp_spec, kc_spec, kp_spec, kc_spec, b_spec],
        out_specs=[q_spec, q_spec],
        out_shape=[jax.ShapeDtypeStruct((L, dil * WIDTH), BF16), jax.ShapeDtypeStruct((L, dil * WIDTH), F32)],
        compiler_params=_params(("arbitrary", "arbitrary")),
    )(sinks, q, k, k, v, v, bias)


def _attn_bwd(q, k, v, o, do, lse, bias, sinks, *, dil, kv_pairs, use_sink, name):
    L = q.shape[0]
    nb = L // BLK
    kvw = kv_pairs * PAIR
    rep = 4 // kv_pairs

    def body(sink_ref, q_ref, kp_ref, kc_ref, vp_ref, vc_ref, o_ref, do_ref, lse_ref, b_ref,
             dq_ref, dk_ref, dv_ref, dsum_ref, dsk_ref, ck_ref, cv_ref):
        r = pl.program_id(0)
        i = pl.program_id(1)

        @pl.when((r == 0) & (i == 0))
        def _():
            dsum_ref[...] = jnp.zeros_like(dsum_ref)
            dsk_ref[...] = jnp.zeros_like(dsk_ref)

        @pl.when(i == 0)
        def _():
            ck_ref[...] = jnp.zeros_like(ck_ref)
            cv_ref[...] = jnp.zeros_like(cv_ref)

        @pl.when(i < nb)
        def _():
            lo = lax.broadcasted_iota(jnp.int32, (1, PAIR), 1) < HEAD_DIM
            dks = [None] * kv_pairs
            dvs = [None] * kv_pairs
            for hp in range(4):
                kvp = hp // rep
                sl = slice(hp * PAIR, (hp + 1) * PAIR)
                ksl = slice(kvp * PAIR, (kvp + 1) * PAIR)
                qp = q_ref[:, sl]
                dop = do_ref[:, sl]
                prod = dop.astype(F32) * o_ref[:, sl].astype(F32)
                kk = jnp.concatenate([kp_ref[:, ksl], kc_ref[:, ksl]], axis=0)
                vv = jnp.concatenate([vp_ref[:, ksl], vc_ref[:, ksl]], axis=0)
                dq_pair = None
                c_pair = None
                for e in range(2):
                    h = 2 * hp + e
                    msk = lo if e == 0 else jnp.logical_not(lo)
                    qm = jnp.where(msk, qp, jnp.zeros_like(qp))
                    dom = jnp.where(msk, dop, jnp.zeros_like(dop))
                    km = jnp.where(msk, kk, jnp.zeros_like(kk))
                    s = lax.dot_general(qm, kk, NT, preferred_element_type=F32) + b_ref[0, h]
                    ls = lse_ref[:, h * HEAD_DIM:h * HEAD_DIM + 1]
                    p = jnp.exp(s - ls)
                    dp = lax.dot_general(dom, vv, NT, preferred_element_type=F32)
                    delta = jnp.sum(jnp.where(msk, prod, 0.0), axis=-1, keepdims=True)
                    ds = p * (dp - delta)
                    dsum_ref[h] += ds
                    if use_sink:
                        ce = jnp.exp(sink_ref[h] - ls) * delta
                        c_pair = jnp.broadcast_to(ce, (BLK, PAIR)) if e == 0 else jnp.where(msk, ce, c_pair)
                    dsb = ds.astype(BF16)
                    dqe = jnp.dot(dsb, km, preferred_element_type=F32)
                    dke = lax.dot_general(dsb, qm, TN, preferred_element_type=F32)
                    dve = lax.dot_general(p.astype(BF16), dom, TN, preferred_element_type=F32)
                    dq_pair = dqe if e == 0 else dq_pair + dqe
                    dks[kvp] = dke if dks[kvp] is None else dks[kvp] + dke
                    dvs[kvp] = dve if dvs[kvp] is None else dvs[kvp] + dve
                dq_ref[:, sl] = (dq_pair * Q_SCALE).astype(BF16)
                if use_sink:
                    dsk_ref[:, sl] += c_pair
            for kvp in range(kv_pairs):
                ksl = slice(kvp * PAIR, (kvp + 1) * PAIR)
                dk_ref[:, ksl] = (ck_ref[:, ksl] + dks[kvp][:BLK]).astype(BF16)
                dv_ref[:, ksl] = (cv_ref[:, ksl] + dvs[kvp][:BLK]).astype(BF16)
                ck_ref[:, ksl] = dks[kvp][BLK:]
                cv_ref[:, ksl] = dvs[kvp][BLK:]

        @pl.when(i == nb)
        def _():
            dk_ref[...] = ck_ref[...].astype(BF16)
            dv_ref[...] = cv_ref[...].astype(BF16)

    q_spec, kp_spec, kc_spec, b_spec = _attn_specs(dil, kvw, nb, True)
    dkv_spec = pl.BlockSpec((BLK, kvw), lambda r, i: (jnp.maximum(i - 1, 0), r))
    return pl.pallas_call(
        body, name=name, grid=(dil, nb + 1),
        in_specs=[pl.BlockSpec(memory_space=pltpu.SMEM), q_spec, kp_spec, kc_spec, kp_spec, kc_spec,
                  q_spec, q_spec, q_spec, b_spec],
        out_specs=[q_spec, dkv_spec, dkv_spec,
                   pl.BlockSpec((N_HEADS, BLK, 2 * BLK), lambda r, i: (0, 0, 0)),
                   pl.BlockSpec((BLK, WIDTH), lambda r, i: (0, 0))],
        out_shape=[jax.ShapeDtypeStruct((L, dil * WIDTH), BF16),
                   jax.ShapeDtypeStruct((L, dil * kvw), BF16),
                   jax.ShapeDtypeStruct((L, dil * kvw), BF16),
                   jax.ShapeDtypeStruct((N_HEADS, BLK, 2 * BLK), F32),
                   jax.ShapeDtypeStruct((BLK, WIDTH), F32)],
        scratch_shapes=[pltpu.VMEM((BLK, kvw), F32), pltpu.VMEM((BLK, kvw), F32)],
        compiler_params=_params(("arbitrary", "arbitrary")),
    )(sinks, q, k, k, v, v, o, do, lse, bias)


def _merge_wo(x, oa, o1, o2, o3, l1, l2, l3, ga, gb, wo, *, tm=512):
    T = x.shape[0]

    def body(x_ref, oa_ref, o1_ref, o2_ref, o3_ref, l1_ref, l2_ref, l3_ref, ga_ref, gb_ref, wo_ref,
             x2_ref, mix_ref, ob1_ref, ob4_ref, ob16_ref, ls1_ref, ls4_ref, ls16_ref, so2, so3, sl2, sl3):
        _unstride(o2_ref, so2, BRANCHES[1][1], tm)
        _unstride(o3_ref, so3, BRANCHES[2][1], tm)
        _unstride(l2_ref, sl2, BRANCHES[1][1], tm)
        _unstride(l3_ref, sl3, BRANCHES[2][1], tm)
        la, lb, lc = l1_ref[...], _scr_get(sl2), _scr_get(sl3)
        m = jnp.maximum(jnp.maximum(la, lb), lc)
        ea, eb, ec = jnp.exp(la - m), jnp.exp(lb - m), jnp.exp(lc - m)
        den = ea + eb + ec
        inv = 1.0 / den
        ob = (ea * o1_ref[...].astype(F32) + eb * _scr_get(so2) + ec * _scr_get(so3)) * inv
        _scr_put(so2, ob)
        _scr_put(sl2, m + jnp.log(den))
        for (_, dil), o_ref, l_ref in zip(BRANCHES, (ob1_ref, ob4_ref, ob16_ref), (ls1_ref, ls4_ref, ls16_ref)):
            _restride(so2, o_ref, dil, tm)
            _restride(sl2, l_ref, dil, tm)
        oav = oa_ref[...].astype(F32)
        ra = lax.rsqrt(jnp.mean(oav * oav, axis=-1, keepdims=True) + EPS)
        rb = lax.rsqrt(jnp.mean(ob * ob, axis=-1, keepdims=True) + EPS)
        mix_ref[:, :WIDTH] = (oav * ra * ga_ref[...]).astype(BF16)
        mix_ref[:, WIDTH:] = (ob * rb * gb_ref[...]).astype(BF16)
        x2_ref[...] = x_ref[...] + jnp.dot(mix_ref[...], wo_ref[...], preferred_element_type=F32)

    row = lambda w_: pl.BlockSpec((tm, w_), lambda i: (i, 0))
    full = lambda a: pl.BlockSpec(a.shape, lambda i: (0, 0))
    return pl.pallas_call(
        body, name="merge_wo", grid=(T // tm,),
        in_specs=[row(D_MODEL), row(WIDTH)] + _view_specs(tm) * 2 + [full(ga), full(gb), full(wo)],
        out_specs=[row(D_MODEL), row(D_MODEL)] + _view_specs(tm) * 2,
        out_shape=[jax.ShapeDtypeStruct((T, D_MODEL), F32), jax.ShapeDtypeStruct((T, D_MODEL), BF16)]
        + _view_shapes(T, BF16) + _view_shapes(T, F32),
        scratch_shapes=[_scr(tm)] * 4,
        compiler_params=_params(("arbitrary",)),
    )(x, oa, o1, o2, o3, l1, l2, l3, ga, gb, wo)


def _ffn_up(x2, g, wg, wu, *, tm=1024, fc=256):
    T = x2.shape[0]

    def body(x_ref, g_ref, wg_ref, wu_ref, h_ref, gate_ref, up_ref, act_ref):
        @pl.when(pl.program_id(1) == 0)
        def _():
            xv = x_ref[...]
            r = lax.rsqrt(jnp.mean(xv * xv, axis=-1, keepdims=True) + EPS)
            h_ref[...] = (xv * r * g_ref[...]).astype(BF16)

        h = h_ref[...]
        gt = jnp.dot(h, wg_ref[...], preferred_element_type=F32)
        u = jnp.dot(h, wu_ref[...], preferred_element_type=F32)
        gate_ref[...] = gt.astype(BF16)
        up_ref[...] = u.astype(BF16)
        act_ref[...] = (gt * (1.0 / (1.0 + jnp.exp(-gt))) * u).astype(BF16)

    rowd = pl.BlockSpec((tm, D_MODEL), lambda i, c: (i, 0))
    wcol = pl.BlockSpec((D_MODEL, fc), lambda i, c: (0, c))
    oc = pl.BlockSpec((tm, fc), lambda i, c: (i, c))
    return pl.pallas_call(
        body, name="ffn_up", grid=(T // tm, D_FF // fc),
        in_specs=[rowd, pl.BlockSpec(g.shape, lambda i, c: (0, 0)), wcol, wcol],
        out_specs=[rowd, oc, oc, oc],
        out_shape=[jax.ShapeDtypeStruct((T, D_MODEL), BF16)] + [jax.ShapeDtypeStruct((T, D_FF), BF16)] * 3,
        compiler_params=_params(("arbitrary", "arbitrary")),
    )(x2, g, wg, wu)


def _ffn_down_loss(act, wd, x2, tgt, g, *, tm=1024, fk=256, rc=256):
    T = x2.shape[0]
    nk = D_FF // fk

    def body(act_ref, wd_ref, x2_ref, tgt_ref, g_ref, dx_ref, dxb_ref, loss_ref, dg_ref, acc_ref):
        i = pl.program_id(0)
        k = pl.program_id(1)

        @pl.when((i == 0) & (k == 0))
        def _():
            loss_ref[...] = jnp.zeros_like(loss_ref)
            dg_ref[...] = jnp.zeros_like(dg_ref)

        @pl.when(k == 0)
        def _():
            acc_ref[...] = jnp.zeros_like(acc_ref)

        acc_ref[...] += jnp.dot(act_ref[...], wd_ref[...], preferred_element_type=F32)

        @pl.when(k == nk - 1)
        def _():
            gv = g_ref[...]
            lsum = jnp.zeros((1, 1), F32)
            dgs = jnp.zeros((1, D_MODEL), F32)
            for c in range(0, tm, rc):
                x3 = x2_ref[c:c + rc, :] + acc_ref[c:c + rc, :]
                r = lax.rsqrt(jnp.mean(x3 * x3, axis=-1, keepdims=True) + EPS)
                xh = x3 * r
                diff = xh * gv - tgt_ref[c:c + rc, :]
                lsum = lsum + jnp.sum(jnp.sum(diff * diff, axis=-1, keepdims=True), axis=0, keepdims=True)
                dy = diff * (1.0 / D_MODEL)
                dgs = dgs + jnp.sum(dy * xh, axis=0, keepdims=True)
                dx = _rms_bwd(dy, xh, r, gv)
                dx_ref[c:c + rc, :] = dx
                dxb_ref[c:c + rc, :] = dx.astype(BF16)
            loss_ref[...] += lsum * (0.5 / D_MODEL)
            dg_ref[...] += dgs

    rowd = pl.BlockSpec((tm, D_MODEL), lambda i, k: (i, 0))
    return pl.pallas_call(
        body, name="ffn_down_loss", grid=(T // tm, nk),
        in_specs=[pl.BlockSpec((tm, fk), lambda i, k: (i, k)), pl.BlockSpec((fk, D_MODEL), lambda i, k: (k, 0)),
                  rowd, rowd, pl.BlockSpec(g.shape, lambda i, k: (0, 0))],
        out_specs=[rowd, rowd, pl.BlockSpec((1, 1), lambda i, k: (0, 0)), pl.BlockSpec((1, D_MODEL), lambda i, k: (0, 0))],
        out_shape=[jax.ShapeDtypeStruct((T, D_MODEL), F32), jax.ShapeDtypeStruct((T, D_MODEL), BF16),
                   jax.ShapeDtypeStruct((1, 1), F32), jax.ShapeDtypeStruct((1, D_MODEL), F32)],
        scratch_shapes=[pltpu.VMEM((tm, D_MODEL), F32)],
        compiler_params=_params(("arbitrary", "arbitrary")),
    )(act, wd, x2, tgt, g)


def _ffn_bwd(dx3, dx3b, gate, up, wd, wg, wu, x2, g, *, tm=1024, fc=256, rc=256):
    T = x2.shape[0]
    nc = D_FF // fc

    def body(dx_ref, dxb_ref, gate_ref, up_ref, wd_ref, wg_ref, wu_ref, x2_ref, g_ref,
             dx2_ref, dx2b_ref, dgate_ref, dup_ref, dg_ref, acc_ref):
        i = pl.program_id(0)
        c = pl.program_id(1)

        @pl.when((i == 0) & (c == 0))
        def _():
            dg_ref[...] = jnp.zeros_like(dg_ref)

        dact = lax.dot_general(dxb_ref[...], wd_ref[...], NT, preferred_element_type=F32)
        gt = gate_ref[...].astype(F32)
        u = up_ref[...].astype(F32)
        sg = 1.0 / (1.0 + jnp.exp(-gt))
        dgate = (dact * u * sg * (1.0 + gt * (1.0 - sg))).astype(BF16)
        dup = (dact * gt * sg).astype(BF16)
        dgate_ref[...] = dgate
        dup_ref[...] = dup
        part = (lax.dot_general(dgate, wg_ref[...], NT, preferred_element_type=F32)
                + lax.dot_general(dup, wu_ref[...], NT, preferred_element_type=F32))

        @pl.when(c == 0)
        def _():
            acc_ref[...] = part

        @pl.when(c > 0)
        def _():
            acc_ref[...] += part

        @pl.when(c == nc - 1)
        def _():
            gv = g_ref[...]
            dgs = jnp.zeros((1, D_MODEL), F32)
            for s in range(0, tm, rc):
                xv = x2_ref[s:s + rc, :]
                r = lax.rsqrt(jnp.mean(xv * xv, axis=-1, keepdims=True) + EPS)
                xh = xv * r
                dh = acc_ref[s:s + rc, :]
                dgs = dgs + jnp.sum(dh * xh, axis=0, keepdims=True)
                d = dx_ref[s:s + rc, :] + _rms_bwd(dh, xh, r, gv)
                dx2_ref[s:s + rc, :] = d
                dx2b_ref[s:s + rc, :] = d.astype(BF16)
            dg_ref[...] += dgs

    rowd = pl.BlockSpec((tm, D_MODEL), lambda i, c: (i, 0))
    oc = pl.BlockSpec((tm, fc), lambda i, c: (i, c))
    wcol = pl.BlockSpec((D_MODEL, fc), lambda i, c: (0, c))
    return pl.pallas_call(
        body, name="ffn_bwd", grid=(T // tm, nc),
        in_specs=[rowd, rowd, oc, oc, pl.BlockSpec((fc, D_MODEL), lambda i, c: (c, 0)), wcol, wcol, rowd,
                  pl.BlockSpec(g.shape, lambda i, c: (0, 0))],
        out_specs=[rowd, rowd, oc, oc, pl.BlockSpec((1, D_MODEL), lambda i, c: (0, 0))],
        out_shape=[jax.ShapeDtypeStruct((T, D_MODEL), F32), jax.ShapeDtypeStruct((T, D_MODEL), BF16),
                   jax.ShapeDtypeStruct((T, D_FF), BF16), jax.ShapeDtypeStruct((T, D_FF), BF16),
                   jax.ShapeDtypeStruct((1, D_MODEL), F32)],
        scratch_shapes=[pltpu.VMEM((tm, D_MODEL), F32)],
        compiler_params=_params(("arbitrary", "arbitrary")),
    )(dx3, dx3b, gate, up, wd, wg, wu, x2, g)


def _matmul_tn(a, b, *, tk, tn, tt=512, out_dtype=BF16, name):
    T, K = a.shape
    N = b.shape[1]
    nt = T // tt

    def body(a_ref, b_ref, o_ref, acc_ref):
        part = lax.dot_general(a_ref[...], b_ref[...], TN, preferred_element_type=F32)

        @pl.when(pl.program_id(2) == 0)
        def _():
            acc_ref[...] = part

        @pl.when(pl.program_id(2) > 0)
        def _():
            acc_ref[...] += part

        @pl.when(pl.program_id(2) == nt - 1)
        def _():
            o_ref[...] = acc_ref[...].astype(out_dtype)

    return pl.pallas_call(
        body, name=name, grid=(K // tk, N // tn, nt),
        in_specs=[pl.BlockSpec((tt, tk), lambda i, j, t: (t, i)), pl.BlockSpec((tt, tn), lambda i, j, t: (t, j))],
        out_specs=pl.BlockSpec((tk, tn), lambda i, j, t: (i, j)),
        out_shape=jax.ShapeDtypeStruct((K, N), out_dtype),
        scratch_shapes=[pltpu.VMEM((tk, tn), F32)],
        compiler_params=_params(("arbitrary", "arbitrary", "arbitrary")),
    )(a, b)


def _wo_bwd(dx2b, wo, oa, ob, ga, gb, *, tm=512):
    T = dx2b.shape[0]

    def body(dx_ref, wo_ref, oa_ref, ob_ref, ga_ref, gb_ref, doa_ref, dob1_ref, dob4_ref, dob16_ref, dga_ref, dgb_ref, scr):
        @pl.when(pl.program_id(0) == 0)
        def _():
            dga_ref[...] = jnp.zeros_like(dga_ref)
            dgb_ref[...] = jnp.zeros_like(dgb_ref)

        dm = lax.dot_general(dx_ref[...], wo_ref[...], NT, preferred_element_type=F32)
        for o_ref, g_ref, dg_ref, sl in ((oa_ref, ga_ref, dga_ref, slice(0, WIDTH)),
                                         (ob_ref, gb_ref, dgb_ref, slice(WIDTH, 2 * WIDTH))):
            ov = o_ref[...].astype(F32)
            r = lax.rsqrt(jnp.mean(ov * ov, axis=-1, keepdims=True) + EPS)
            xh = ov * r
            d = dm[:, sl]
            dg_ref[...] += jnp.sum(d * xh, axis=0, keepdims=True)
            do = _rms_bwd(d, xh, r, g_ref[...])
            if o_ref is oa_ref:
                doa_ref[...] = do.astype(BF16)
            else:
                _scr_put(scr, do)
                for (_, dil), v_ref in zip(BRANCHES, (dob1_ref, dob4_ref, dob16_ref)):
                    _restride(scr, v_ref, dil, tm)

    row = lambda w_: pl.BlockSpec((tm, w_), lambda i: (i, 0))
    full = lambda a: pl.BlockSpec(a.shape, lambda i: (0, 0))
    return pl.pallas_call(
        body, name="wo_bwd", grid=(T // tm,),
        in_specs=[row(D_MODEL), full(wo), row(WIDTH), row(WIDTH), full(ga), full(gb)],
        out_specs=[row(WIDTH)] + _view_specs(tm)
        + [pl.BlockSpec((1, WIDTH), lambda i: (0, 0)), pl.BlockSpec((1, WIDTH), lambda i: (0, 0))],
        out_shape=[jax.ShapeDtypeStruct((T, WIDTH), BF16)] + _view_shapes(T, BF16)
        + [jax.ShapeDtypeStruct((1, WIDTH), F32), jax.ShapeDtypeStruct((1, WIDTH), F32)],
        scratch_shapes=[_scr(tm)],
        compiler_params=_params(("arbitrary",)),
    )(dx2b, wo, oa, ob, ga, gb)


def _dproj(dqa, dka, dva, dqs, dks, dvs, cos, sin, *, tm=512):
    T = dqa.shape[0]

    def body(dqa_ref, dka_ref, dva_ref, q1, q2, q3, k1, k2, k3, v1, v2, v3, cos_ref, sin_ref, dp_ref, db_ref, acc, tmp):
        @pl.when(pl.program_id(0) == 0)
        def _():
            db_ref[...] = jnp.zeros_like(db_ref)

        cosv = cos_ref[...]
        sinv = sin_ref[...]
        lane = lax.broadcasted_iota(jnp.int32, (tm, PAIR), 1)
        first = (lane % HEAD_DIM) < (HEAD_DIM // 2)

        def put(off, val):
            dp_ref[:, off:off + PAIR] = val.astype(BF16)
            db_ref[:, off:off + PAIR] += jnp.sum(val, axis=0, keepdims=True)

        for src, off, width in ((dqa_ref, 0, 512), (dka_ref, 512, 256)):
            for j in range(0, width, PAIR):
                d = src[:, j:j + PAIR].astype(F32)
                put(off + j, d * cosv - _rope_rot(d, first) * sinv)
        for j in range(0, 256, PAIR):
            put(768 + j, dva_ref[:, j:j + PAIR].astype(F32))
        for (a, b, c), off in (((q1, q2, q3), 1024), ((k1, k2, k3), 1536), ((v1, v2, v3), 2048)):
            _unstride(b, acc, BRANCHES[1][1], tm)
            _unstride(c, tmp, BRANCHES[2][1], tm)
            for j in range(N_CHUNK):
                put(off + j * PAIR, a[:, j * PAIR:(j + 1) * PAIR].astype(F32) + acc[j] + tmp[j])

    row = lambda w_: pl.BlockSpec((tm, w_), lambda i: (i, 0))
    return pl.pallas_call(
        body, name="dproj", grid=(T // tm,),
        in_specs=[row(512), row(256), row(256)] + _view_specs(tm) * 3 + [row(PAIR), row(PAIR)],
        out_specs=[row(D_INP), pl.BlockSpec((1, D_INP), lambda i: (0, 0))],
        out_shape=[jax.ShapeDtypeStruct((T, D_INP), BF16), jax.ShapeDtypeStruct((1, D_INP), F32)],
        scratch_shapes=[_scr(tm)] * 2,
        compiler_params=_params(("arbitrary",)),
    )(dqa, dka, dva, *dqs, *dks, *dvs, cos, sin)


def _inproj_bwd(dp, w, x, dx2, g, *, tm=512):
    T = x.shape[0]

    def body(dp_ref, w_ref, x_ref, dx2_ref, g_ref, gx_ref, dg_ref):
        @pl.when(pl.program_id(0) == 0)
        def _():
            dg_ref[...] = jnp.zeros_like(dg_ref)

        dh = lax.dot_general(dp_ref[...], w_ref[...], NT, preferred_element_type=F32)
        xv = x_ref[...]
        r = lax.rsqrt(jnp.mean(xv * xv, axis=-1, keepdims=True) + EPS)
        xh = xv * r
        dg_ref[...] += jnp.sum(dh * xh, axis=0, keepdims=True)
        gx_ref[...] = dx2_ref[...] + _rms_bwd(dh, xh, r, g_ref[...])

    row = lambda w_: pl.BlockSpec((tm, w_), lambda i: (i, 0))
    full = lambda a: pl.BlockSpec(a.shape, lambda i: (0, 0))
    return pl.pallas_call(
        body, name="inproj_bwd", grid=(T // tm,),
        in_specs=[row(D_INP), full(w), row(D_MODEL), row(D_MODEL), full(g)],
        out_specs=[row(D_MODEL), pl.BlockSpec((1, D_MODEL), lambda i: (0, 0))],
        out_shape=[jax.ShapeDtypeStruct((T, D_MODEL), F32), jax.ShapeDtypeStruct((1, D_MODEL), F32)],
        compiler_params=_params(("arbitrary",)),
    )(dp, w, x, dx2, g)


def _bias_sink_grads(dsums, bmaps, dsk):
    def body(s1, s2, s3, m1, m2, m3, dsk_ref, drel_ref, dsink_ref):
        row = lax.broadcasted_iota(jnp.int32, (N_HEADS, 128), 0)
        lane = lax.broadcasted_iota(jnp.int32, (N_HEADS, 128), 1)
        out = jnp.zeros((N_HEADS, 128), F32)
        for s_ref, m_ref in ((s1, m1), (s2, m2), (s3, m3)):
            bm = m_ref[...]
            for h in range(N_HEADS):
                a = s_ref[h]
                for b in range(REL_BUCKETS):
                    v = jnp.sum(jnp.sum(jnp.where(bm == b, a, 0.0), axis=-1, keepdims=True), axis=0, keepdims=True)
                    out = out + jnp.where((row == h) & (lane == b), v, 0.0)
        drel_ref[...] = out
        dsink_ref[...] = -jnp.sum(dsk_ref[...], axis=0, keepdims=True)

    vm = pl.BlockSpec(memory_space=pltpu.VMEM)
    return pl.pallas_call(
        body, name="bias_sink_grads",
        in_specs=[vm] * 7, out_specs=[vm, vm],
        out_shape=[jax.ShapeDtypeStruct((N_HEADS, 128), F32), jax.ShapeDtypeStruct((1, WIDTH), F32)],
        compiler_params=_params(),
    )(*dsums, *bmaps, dsk)


def _all_gather(blk, *, name):
    R, C = blk.shape

    def body(x_ref, out_ref, send_sems, recv_sems, local_sem):
        x, y, c = lax.axis_index("x"), lax.axis_index("y"), lax.axis_index("c")
        me, sibling = (x, y, c), (x, y, 1 - c)
        chips = [(1 - x, y), (x, 1 - y), (1 - x, 1 - y)]

        def slot(px, py, pc):
            return out_ref.at[4 * px + 2 * py + pc]

        def copy(k, block, to, src=None):
            return pltpu.make_async_remote_copy(
                src_ref=slot(*block) if src is None else src, dst_ref=slot(*block),
                send_sem=send_sems.at[k], recv_sem=recv_sems.at[k], device_id=to, device_id_type=MESH)

        mine = pltpu.make_async_copy(x_ref, slot(*me), local_sem)
        mine.start()
        first = [copy(0, me, sibling, src=x_ref)]
        first += [copy(1 + j, me, (*chip, c), src=x_ref) for j, chip in enumerate(chips)]
        for cp in first:
            cp.start()
        passed = [copy(4 + j, (*chip, c), sibling) for j, chip in enumerate(chips)]
        for j, chip in enumerate(chips):
            copy(1 + j, (*chip, c), me).wait_recv()
            passed[j].start()
        copy(0, sibling, me).wait_recv()
        for j, chip in enumerate(chips):
            copy(4 + j, (*chip, 1 - c), me).wait_recv()
        for cp in first + passed:
            cp.wait_send()
        mine.wait()

    return pl.pallas_call(
        body, name=name,
        in_specs=[pl.BlockSpec(memory_space=pl.ANY)], out_specs=pl.BlockSpec(memory_space=pl.ANY),
        out_shape=jax.ShapeDtypeStruct((N_DEV, R, C), blk.dtype),
        scratch_shapes=[pltpu.SemaphoreType.DMA((7,)), pltpu.SemaphoreType.DMA((7,)), pltpu.SemaphoreType.DMA],
        compiler_params=pltpu.CompilerParams(has_side_effects=True),
    )(blk)


def _scatter_blocks(parts, *, name):
    _, R, C = parts.shape

    def body(p_ref, out_ref, send_sems, recv_sems, local_sem):
        x, y, c = lax.axis_index("x"), lax.axis_index("y"), lax.axis_index("c")
        mine = 4 * x + 2 * y + c
        peers = [(x ^ (k >> 2), y ^ ((k >> 1) & 1), c ^ (k & 1)) for k in range(1, N_DEV)]

        def copy(k, peer):
            dest = 4 * peer[0] + 2 * peer[1] + peer[2]
            return pltpu.make_async_remote_copy(
                src_ref=p_ref.at[dest], dst_ref=out_ref.at[mine],
                send_sem=send_sems.at[k], recv_sem=recv_sems.at[k], device_id=peer, device_id_type=MESH)

        own = pltpu.make_async_copy(p_ref.at[mine], out_ref.at[mine], local_sem)
        own.start()
        cps = [copy(k, peer) for k, peer in enumerate(peers)]
        for cp in cps:
            cp.start()
        for k, peer in enumerate(peers):
            src = 4 * peer[0] + 2 * peer[1] + peer[2]
            pltpu.make_async_remote_copy(
                src_ref=p_ref.at[src], dst_ref=out_ref.at[src],
                send_sem=send_sems.at[k], recv_sem=recv_sems.at[k], device_id=peer, device_id_type=MESH).wait_recv()
        for cp in cps:
            cp.wait_send()
        own.wait()

    return pl.pallas_call(
        body, name=name,
        in_specs=[pl.BlockSpec(memory_space=pl.ANY)], out_specs=pl.BlockSpec(memory_space=pl.ANY),
        out_shape=jax.ShapeDtypeStruct(parts.shape, parts.dtype),
        scratch_shapes=[pltpu.SemaphoreType.DMA((7,)), pltpu.SemaphoreType.DMA((7,)), pltpu.SemaphoreType.DMA],
        compiler_params=pltpu.CompilerParams(has_side_effects=True),
    )(parts)


def _adam_math(w, g, m, v):
    m = ADAM_B1 * m + (1.0 - ADAM_B1) * g
    v = ADAM_B2 * v + (1.0 - ADAM_B2) * (g * g)
    m_hat = m / (1.0 - ADAM_B1 ** ADAM_STEP)
    v_hat = v / (1.0 - ADAM_B2 ** ADAM_STEP)
    delta = -ADAM_LR * (m_hat / (jnp.sqrt(v_hat) + ADAM_EPS) + ADAM_WD * w)
    return delta, m, v


def _adamw(parts, w, m, v, *, name):
    R, C = w.shape
    tr = R // 2
    assert tr % 16 == 0

    def body(p_ref, w_ref, m_ref, v_ref, g_ref, d_ref, nm_ref, nv_ref):
        g = p_ref[0].astype(F32)
        for s in range(1, N_DEV):
            g = g + p_ref[s].astype(F32)
        d, nm, nv = _adam_math(w_ref[...], g, m_ref[...], v_ref[...])
        g_ref[...] = g
        d_ref[...] = d
        nm_ref[...] = nm
        nv_ref[...] = nv

    blk = pl.BlockSpec((tr, C), lambda i: (i, 0))
    return pl.pallas_call(
        body, name=name, grid=(R // tr,),
        in_specs=[pl.BlockSpec((N_DEV, tr, C), lambda i: (0, i, 0)), blk, blk, blk],
        out_specs=[blk] * 4, out_shape=[jax.ShapeDtypeStruct((R, C), F32)] * 4,
        compiler_params=_params(("arbitrary",)),
    )(parts, w, m, v)


def _adamw_small(parts, w, m, v):
    def body(p_ref, w_ref, m_ref, v_ref, g_ref, d_ref, nm_ref, nv_ref):
        g = p_ref[0]
        for s in range(1, N_DEV):
            g = g + p_ref[s]
        d, nm, nv = _adam_math(w_ref[...], g, m_ref[...], v_ref[...])
        g_ref[...] = g
        d_ref[...] = d
        nm_ref[...] = nm
        nv_ref[...] = nv

    vm = pl.BlockSpec(memory_space=pltpu.VMEM)
    return pl.pallas_call(
        body, name="adamw_small", in_specs=[vm] * 4, out_specs=[vm] * 4,
        out_shape=[jax.ShapeDtypeStruct((SMALL_ROWS, 128), F32)] * 4, compiler_params=_params(),
    )(parts, w, m, v)


def _t5_bucket(dist):
    max_exact = REL_BUCKETS // 2
    df = jnp.maximum(dist, 1).astype(F32)
    large = max_exact + (jnp.log(df / max_exact) / math.log(REL_MAX_DISTANCE / max_exact)
                         * (REL_BUCKETS - max_exact)).astype(jnp.int32)
    large = jnp.minimum(large, REL_BUCKETS - 1)
    return jnp.where(dist < max_exact, dist, large)


def _band_tables(rel_table, dil, n_back):
    qi = jnp.arange(BLK)[:, None]
    kj = jnp.arange(2 * BLK)[None, :]
    delta = BLK + qi - kj
    in_band = (delta >= 0) & (delta <= n_back)
    if rel_table is None:
        vals = jnp.zeros((N_HEADS, BLK, 2 * BLK), F32)
        bmap = None
    else:
        bucket = _t5_bucket(jnp.clip(delta, 0, n_back) * dil)
        vals = jnp.zeros((N_HEADS, BLK, 2 * BLK), F32)
        for b in range(REL_BUCKETS):
            vals = jnp.where((bucket == b)[None], rel_table[b][:, None, None], vals)
        bmap = jnp.where(in_band, bucket, -1).astype(jnp.int32)
    later = jnp.where(in_band[None], vals, NEG)
    first = jnp.where((in_band & (kj >= BLK))[None], vals, NEG)
    return jnp.stack([later, first]), bmap


def _rope_tables(T):
    half = HEAD_DIM // 2
    inv_freq = ROPE_THETA ** (-jnp.arange(half, dtype=F32) / half)
    ang = jnp.arange(T, dtype=F32)[:, None] * inv_freq[None, :]
    cos, sin = jnp.cos(ang), jnp.sin(ang)
    return jnp.tile(cos, (1, 4)), jnp.tile(jnp.concatenate([-sin, sin], axis=1), (1, 2))


def _dup_heads(a):
    h0, h1 = a[..., :HEAD_DIM], a[..., HEAD_DIM:]
    return jnp.concatenate([h0, h0, h1, h1], axis=-1)


def _widen_in(a):
    return jnp.concatenate([a[..., :512], _dup_heads(a[..., 512:640]), _dup_heads(a[..., 640:768]), a[..., 768:]], axis=-1)


def _fold_in(a):
    def fold(t):
        return jnp.concatenate([t[..., 0:64] + t[..., 64:128], t[..., 128:192] + t[..., 192:256]], axis=-1)
    return jnp.concatenate([a[..., :512], fold(a[..., 512:768]), fold(a[..., 768:1024]), a[..., 1024:]], axis=-1)


def _local_step(x, tgt, g_attn, win, b_in, sinks, rel_table, g_out_a, g_out_b, wo, g_ffn, wg, wu, wd, g_final):
    T = x.shape[0]
    cos, sin = _rope_tables(T)
    winp = _widen_in(win)
    binp = _widen_in(b_in)
    g_final2 = g_final.reshape(1, D_MODEL)
    sink8 = sinks.reshape(N_HEADS)

    bias_a, _ = _band_tables(None, 1, BLK - 1)
    tabs = [_band_tables(rel_table, dil, window // dil) for window, dil in BRANCHES]

    h1, qa, ka, va, *qkv_b = _norm_proj(x, g_attn, winp, binp, cos, sin)
    qbs, kbs, vbs = qkv_b[0:3], qkv_b[3:6], qkv_b[6:9]
    oa, lse_a = _attn_fwd(qa, ka, va, bias_a, sink8, dil=1, kv_pairs=2, use_sink=True, name="attn_a_fwd")
    outs = [_attn_fwd(qbs[n], kbs[n], vbs[n], tabs[n][0], sink8, dil=dil, kv_pairs=4, use_sink=False,
                      name=f"attn_b{n}_fwd") for n, (_, dil) in enumerate(BRANCHES)]
    x2, mixed, *ob_lse = _merge_wo(x, oa, outs[0][0], outs[1][0], outs[2][0], outs[0][1], outs[1][1], outs[2][1],
                                   g_out_a, g_out_b, wo)
    obs, lses = ob_lse[0:3], ob_lse[3:6]
    h2, gate, up, act = _ffn_up(x2, g_ffn, wg, wu)
    dx3, dx3b, loss, dg_final = _ffn_down_loss(act, wd, x2, tgt, g_final2)

    dx2, dx2b, dgate, dup, dg_ffn = _ffn_bwd(dx3, dx3b, gate, up, wd, wg, wu, x2, g_ffn)
    dwd = _matmul_tn(act, dx3b, tk=1408, tn=1024, name="dw_down")
    dwg = _matmul_tn(h2, dgate, tk=1024, tn=1408, name="dw_gate")
    dwu = _matmul_tn(h2, dup, tk=1024, tn=1408, name="dw_up")
    dwo = _matmul_tn(mixed, dx2b, tk=1024, tn=1024, name="dw_o")
    doa, *dobs, dg_out_a, dg_out_b = _wo_bwd(dx2b, wo, oa, obs[0], g_out_a, g_out_b)

    dqa, dka, dva, _, dsk = _attn_bwd(qa, ka, va, oa, doa, lse_a, bias_a, sink8, dil=1, kv_pairs=2, use_sink=True,
                                      name="attn_a_bwd")
    res = [_attn_bwd(qbs[n], kbs[n], vbs[n], obs[n], dobs[n], lses[n], tabs[n][0], sink8, dil=dil, kv_pairs=4,
                     use_sink=False, name=f"attn_b{n}_bwd") for n, (_, dil) in enumerate(BRANCHES)]
    dp, dbp = _dproj(dqa, dka, dva, [r[0] for r in res], [r[1] for r in res], [r[2] for r in res], cos, sin)
    grad_x, dg_attn = _inproj_bwd(dp, winp, x, dx2, g_attn)
    dwin = _fold_in(_matmul_tn(h1, dp, tk=1024, tn=1280, out_dtype=F32, name="dw_in"))
    drel, dsink = _bias_sink_grads([r[3] for r in res], [t[1] for t in tabs], dsk)

    small = dict(
        g_attn=dg_attn, b_in=_fold_in(dbp), sinks=dsink[:, ::HEAD_DIM], rel_table=drel[:, :REL_BUCKETS].T,
        g_out_a=dg_out_a, g_out_b=dg_out_b, g_ffn=dg_ffn, g_final=dg_final.reshape(D_MODEL))
    return loss[0, 0], grad_x, dict(w_in=dwin, w_o=dwo, w_gate=dwg, w_up=dwu, w_down=dwd), small


SMALL_NAMES = ("g_attn", "b_in", "sinks", "rel_table", "g_out_a", "g_out_b", "g_ffn", "g_final")


def _pack_small(vals):
    flat = jnp.concatenate([vals[n].reshape(-1).astype(F32) for n in SMALL_NAMES])
    return jnp.pad(flat, (0, SMALL_ROWS * 128 - flat.shape[0])).reshape(SMALL_ROWS, 128)


def _unpack_small(packed, like):
    flat = packed.reshape(-1)
    out, off = {}, 0
    for n in SMALL_NAMES:
        size = like[n].size
        out[n] = flat[off:off + size].reshape(like[n].shape)
        off += size
    return out


def _cols_to_rows(a):
    return a.reshape(a.shape[1], D_MODEL)


def kernel(x, g_attn, w_in, b_in, sinks, rel_table, g_out_a, g_out_b, w_o, g_ffn, w_gate, w_up, w_down, g_final, loss_target, m_g_attn, m_w_in, m_b_in, m_sinks, m_rel_table, m_g_out_a, m_g_out_b, m_w_o, m_g_ffn, m_w_gate, m_w_up, m_w_down, m_g_final, v_g_attn, v_w_in, v_b_in, v_sinks, v_rel_table, v_g_out_a, v_g_out_b, v_w_o, v_g_ffn, v_w_gate, v_w_up, v_w_down, v_g_final):
    T = x.shape[1]
    flat = jnp.concatenate([_cols_to_rows(w_in[0].astype(BF16)), w_o[0].astype(BF16), _cols_to_rows(w_gate[0].astype(BF16)),
                            _cols_to_rows(w_up[0].astype(BF16)), w_down[0].astype(BF16)], axis=0)
    gw = _all_gather(flat, name="gather_weights")
    o0, o1, o2, o3, o4 = 0, 288, 416, 768, 1120

    def cols(seg, n):
        return seg.reshape(N_DEV, D_MODEL, n).transpose(1, 0, 2).reshape(D_MODEL, N_DEV * n)

    win = cols(gw[:, o0:o1], 288)
    wo = gw[:, o1:o2].reshape(D_MODEL, D_MODEL)
    wg = cols(gw[:, o2:o3], 352)
    wu = cols(gw[:, o3:o4], 352)
    wd = gw[:, o4:].reshape(D_FF, D_MODEL)

    loss_part, grad_x, dws, small = _local_step(
        x[0], loss_target[0], g_attn, win, b_in, sinks, rel_table, g_out_a, g_out_b, wo, g_ffn, wg, wu, wd, g_final)
    loss = lax.psum(loss_part, ("x", "y", "c"))

    def col_parts(g, n):
        return g.astype(BF16).reshape(D_MODEL, N_DEV, n).transpose(1, 0, 2).reshape(N_DEV, n, D_MODEL)

    parts = jnp.concatenate([
        col_parts(dws["w_in"], 288), dws["w_o"].astype(BF16).reshape(N_DEV, 128, D_MODEL),
        col_parts(dws["w_gate"], 352), col_parts(dws["w_up"], 352),
        dws["w_down"].astype(BF16).reshape(N_DEV, 352, D_MODEL)], axis=1)
    got = _scatter_blocks(parts, name="scatter_grads")

    shards = dict(w_in=(w_in, m_w_in, v_w_in, o0, o1, True), w_o=(w_o, m_w_o, v_w_o, o1, o2, False),
                  w_gate=(w_gate, m_w_gate, v_w_gate, o2, o3, True), w_up=(w_up, m_w_up, v_w_up, o3, o4, True),
                  w_down=(w_down, m_w_down, v_w_down, o4, FLAT_ROWS, False))
    big = {}
    for n, (w, m, v, lo, hi, by_cols) in shards.items():
        p = got[:, lo:hi]
        if by_cols:
            p = p.reshape(N_DEV, D_MODEL, hi - lo)
        big[n] = [a[None] for a in _adamw(p, w[0], m[0], v[0], name="adamw_" + n)]

    ws = dict(g_attn=g_attn, b_in=b_in, sinks=sinks, rel_table=rel_table, g_out_a=g_out_a, g_out_b=g_out_b,
              g_ffn=g_ffn, g_final=g_final)
    ms = dict(g_attn=m_g_attn, b_in=m_b_in, sinks=m_sinks, rel_table=m_rel_table, g_out_a=m_g_out_a,
              g_out_b=m_g_out_b, g_ffn=m_g_ffn, g_final=m_g_final)
    vs = dict(g_attn=v_g_attn, b_in=v_b_in, sinks=v_sinks, rel_table=v_rel_table, g_out_a=v_g_out_a,
              g_out_b=v_g_out_b, g_ffn=v_g_ffn, g_final=v_g_final)
    sparts = _all_gather(_pack_small(small), name="gather_small")
    sm = [_unpack_small(a, ws) for a in _adamw_small(sparts, _pack_small(ws), _pack_small(ms), _pack_small(vs))]

    order = ("g_attn", "w_in", "b_in", "sinks", "rel_table", "g_out_a", "g_out_b", "w_o", "g_ffn", "w_gate", "w_up",
             "w_down", "g_final")
    outs = [loss, grad_x[None]]
    for k in range(4):
        outs += [big[n][k] if n in big else sm[k][n] for n in order]
    return tuple(outs)
```

```python
import functools
import math

import jax
import jax.numpy as jnp
from jax import lax
from jax.experimental import pallas as pl
from jax.experimental.pallas import tpu as pltpu

F32 = jnp.float32
BF16 = jnp.bfloat16

N_DEV = 8
D_MODEL = 1024
HEAD_DIM = 64
N_HEADS = 8
PAIR = 2 * HEAD_DIM
WIDTH = N_HEADS * HEAD_DIM
D_IN = 2304
D_INP = 2560
D_FF = 2816
BLK = 128
ROPE_THETA = 150000.0
REL_BUCKETS = 32
REL_MAX_DISTANCE = 2048
EPS = 1e-5
NEG = -1e30
BRANCHES = ((128, 1), (512, 4), (2048, 16))
Q_SCALE = HEAD_DIM ** -0.5

ADAM_LR = 0.001
ADAM_B1 = 0.9
ADAM_B2 = 0.999
ADAM_EPS = 1e-08
ADAM_WD = 0.01
ADAM_STEP = 10

VMEM_LIMIT = 56 * 1024 * 1024
MESH = pl.DeviceIdType.MESH

NT = (((1,), (1,)), ((), ()))
TN = (((0,), (0,)), ((), ()))

SEG_ROWS = (288, 128, 352, 352, 352)
FLAT_ROWS = sum(SEG_ROWS)
SMALL_ROWS = 56


def _params(sem=None):
    return pltpu.CompilerParams(dimension_semantics=sem, vmem_limit_bytes=VMEM_LIMIT)


def _rms_bwd(dh, xh, r, g):
    u = dh * g
    return r * (u - xh * jnp.mean(u * xh, axis=-1, keepdims=True))


def _rope_rot(t, first):
    return jnp.where(first, pltpu.roll(t, 96, 1), pltpu.roll(t, 32, 1))


N_CHUNK = WIDTH // PAIR


def _scr(tm):
    return pltpu.VMEM((N_CHUNK, tm, PAIR), F32)


def _scr_get(scr):
    return jnp.concatenate([scr[j] for j in range(N_CHUNK)], axis=1)


def _scr_put(scr, val):
    for j in range(N_CHUNK):
        scr[j] = val[:, j * PAIR:(j + 1) * PAIR]


def _unstride(view_ref, scr, dil, tm):
    n = tm // dil
    for r in range(dil):
        for j in range(N_CHUNK):
            col = r * WIDTH + j * PAIR
            scr.at[j][pl.ds(r, n, stride=dil), :] = view_ref[:, col:col + PAIR].astype(F32)


def _restride(scr, out_ref, dil, tm):
    n = tm // dil
    for r in range(dil):
        for j in range(N_CHUNK):
            col = r * WIDTH + j * PAIR
            rows = scr[j] if dil == 1 else scr.at[j][pl.ds(r, n, stride=dil), :]
            out_ref[:, col:col + PAIR] = rows.astype(out_ref.dtype)


def _view_specs(tm):
    return [pl.BlockSpec((tm // dil, dil * WIDTH), lambda i: (i, 0)) for _, dil in BRANCHES]


def _view_shapes(T, dtype):
    return [jax.ShapeDtypeStruct((T // dil, dil * WIDTH), dtype) for _, dil in BRANCHES]


def _norm_proj(x, g, w, b, cos, sin, *, tm=512):
    T = x.shape[0]

    def body(x_ref, g_ref, w_ref, b_ref, cos_ref, sin_ref, h_ref, qa_ref, ka_ref, va_ref, *rest):
        outs_b, ys = rest[:9], rest[9]
        xv = x_ref[...]
        r = lax.rsqrt(jnp.mean(xv * xv, axis=-1, keepdims=True) + EPS)
        h = (xv * r * g_ref[...]).astype(BF16)
        h_ref[...] = h
        cosv = cos_ref[...]
        sinv = sin_ref[...]
        lane = lax.broadcasted_iota(jnp.int32, (tm, PAIR), 1)
        first = (lane % HEAD_DIM) < (HEAD_DIM // 2)

        def proj(off):
            return jnp.dot(h, w_ref[:, off:off + 256], preferred_element_type=F32) + b_ref[:, off:off + 256]

        for (off, width, rot, scale), o_ref in zip(((0, 512, True, Q_SCALE), (512, 256, True, 1.0), (768, 256, False, 1.0)),
                                                   (qa_ref, ka_ref, va_ref)):
            for c in range(0, width, 256):
                y = proj(off + c)
                for j in range(0, 256, PAIR):
                    t = y[:, j:j + PAIR]
                    if rot:
                        t = t * cosv + _rope_rot(t, first) * sinv
                    if scale != 1.0:
                        t = t * scale
                    o_ref[:, c + j:c + j + PAIR] = t.astype(BF16)
        for n, (off, scale) in enumerate(((1024, Q_SCALE), (1536, 1.0), (2048, 1.0))):
            for c in range(0, WIDTH, 256):
                y = proj(off + c)
                y = y * scale if scale != 1.0 else y
                for j in range(0, 256, PAIR):
                    ys[(c + j) // PAIR] = y[:, j:j + PAIR]
            for (_, dil), o_ref in zip(BRANCHES, outs_b[3 * n:3 * n + 3]):
                _restride(ys, o_ref, dil, tm)

    row = lambda w_: pl.BlockSpec((tm, w_), lambda i: (i, 0))
    full = lambda a: pl.BlockSpec(a.shape, lambda i: (0, 0))
    return pl.pallas_call(
        body, name="norm_proj", grid=(T // tm,),
        in_specs=[row(D_MODEL), full(g), full(w), full(b), row(PAIR), row(PAIR)],
        out_specs=[row(D_MODEL), row(512), row(256), row(256)] + _view_specs(tm) * 3,
        out_shape=[jax.ShapeDtypeStruct((T, n), BF16) for n in (D_MODEL, 512, 256, 256)] + _view_shapes(T, BF16) * 3,
        scratch_shapes=[_scr(tm)],
        compiler_params=_params(("arbitrary",)),
    )(x, g, w, b, cos, sin)


def _attn_specs(dil, kvw, nb, clamp):
    qi = (lambda r, i: (jnp.minimum(i, nb - 1), r)) if clamp else (lambda r, i: (i, r))
    q_spec = pl.BlockSpec((BLK, WIDTH), qi)
    kp_spec = pl.BlockSpec((BLK, kvw), lambda r, i: (jnp.minimum(jnp.maximum(i - 1, 0), nb - 1), r))
    kc_spec = pl.BlockSpec((BLK, kvw), qi)
    b_spec = pl.BlockSpec((1, N_HEADS, BLK, 2 * BLK), lambda r, i: (jnp.where(i == 0, 1, 0), 0, 0, 0))
    return q_spec, kp_spec, kc_spec, b_spec


def _attn_fwd(q, k, v, bias, sinks, *, dil, kv_pairs, use_sink, name):
    L = q.shape[0]
    nb = L // BLK
    kvw = kv_pairs * PAIR
    rep = 4 // kv_pairs

    def body(sink_ref, q_ref, kp_ref, kc_ref, vp_ref, vc_ref, b_ref, o_ref, lse_ref):
        lo = lax.broadcasted_iota(jnp.int32, (1, PAIR), 1) < HEAD_DIM
        for hp in range(4):
            sl = slice(hp * PAIR, (hp + 1) * PAIR)
            ksl = slice((hp // rep) * PAIR, (hp // rep + 1) * PAIR)
            qp = q_ref[:, sl]
            kk = jnp.concatenate([kp_ref[:, ksl], kc_ref[:, ksl]], axis=0)
            vv = jnp.concatenate([vp_ref[:, ksl], vc_ref[:, ksl]], axis=0)
            o_pair = None
            lse_pair = None
            for e in range(2):
                h = 2 * hp + e
                msk = lo if e == 0 else jnp.logical_not(lo)
                qm = jnp.where(msk, qp, jnp.zeros_like(qp))
                s = lax.dot_general(qm, kk, NT, preferred_element_type=F32) + b_ref[0, h]
                m = jnp.max(s, axis=-1, keepdims=True)
                if use_sink:
                    sk = sink_ref[h]
                    m = jnp.maximum(m, sk)
                p = jnp.exp(s - m)
                l = jnp.sum(p, axis=-1, keepdims=True)
                if use_sink:
                    l = l + jnp.exp(sk - m)
                vm = jnp.where(msk, vv, jnp.zeros_like(vv))
                oe = jnp.dot(p.astype(BF16), vm, preferred_element_type=F32) * (1.0 / l)
                ls = m + jnp.log(l)
                if e == 0:
                    o_pair = oe
                    lse_pair = jnp.broadcast_to(ls, (BLK, PAIR))
                else:
                    o_pair = o_pair + oe
                    lse_pair = jnp.where(msk, ls, lse_pair)
            o_ref[:, sl] = o_pair.astype(BF16)
            lse_ref[:, sl] = lse_pair

    q_spec, kp_spec, kc_spec, b_spec = _attn_specs(dil, kvw, nb, False)
    return pl.pallas_call(
        body, name=name, grid=(dil, nb),
        in_specs=[pl.BlockSpec(memory_space=pltpu.SMEM), q_spec, kp_spec, kc_spec, kp_spec, kc_spec, b_spec],
        out_specs=[q_spec, q_spec],
        out_shape=[jax.ShapeDtypeStruct((L, dil * WIDTH), BF16), jax.ShapeDtypeStruct((L, dil * WIDTH), F32)],
        compiler_params=_params(("arbitrary", "arbitrary")),
    )(sinks, q, k, k, v, v, bias)


def _attn_bwd(q, k, v, o, do, lse, bias, sinks, *, dil, kv_pairs, use_sink, name):
    L = q.shape[0]
    nb = L // BLK
    kvw = kv_pairs * PAIR
    rep = 4 // kv_pairs

    def body(sink_ref, q_ref, kp_ref, kc_ref, vp_ref, vc_ref, o_ref, do_ref, lse_ref, b_ref,
             dq_ref, dk_ref, dv_ref, dsum_ref, dsk_ref, ck_ref, cv_ref):
        r = pl.program_id(0)
        i = pl.program_id(1)

        @pl.when((r == 0) & (i == 0))
        def _():
            dsum_ref[...] = jnp.zeros_like(dsum_ref)
            dsk_ref[...] = jnp.zeros_like(dsk_ref)

        @pl.when(i == 0)
        def _():
            ck_ref[...] = jnp.zeros_like(ck_ref)
            cv_ref[...] = jnp.zeros_like(cv_ref)

        @pl.when(i < nb)
        def _():
            lo = lax.broadcasted_iota(jnp.int32, (1, PAIR), 1) < HEAD_DIM
            dks = [None] * kv_pairs
            dvs = [None] * kv_pairs
            for hp in range(4):
                kvp = hp // rep
                sl = slice(hp * PAIR, (hp + 1) * PAIR)
                ksl = slice(kvp * PAIR, (kvp + 1) * PAIR)
                qp = q_ref[:, sl]
                dop = do_ref[:, sl]
                prod = dop.astype(F32) * o_ref[:, sl].astype(F32)
                kk = jnp.concatenate([kp_ref[:, ksl], kc_ref[:, ksl]], axis=0)
                vv = jnp.concatenate([vp_ref[:, ksl], vc_ref[:, ksl]], axis=0)
                dq_pair = None
                c_pair = None
                for e in range(2):
                    h = 2 * hp + e
                    msk = lo if e == 0 else jnp.logical_not(lo)
                    qm = jnp.where(msk, qp, jnp.zeros_like(qp))
                    dom = jnp.where(msk, dop, jnp.zeros_like(dop))
                    km = jnp.where(msk, kk, jnp.zeros_like(kk))
                    s = lax.dot_general(qm, kk, NT, preferred_element_type=F32) + b_ref[0, h]
                    ls = lse_ref[:, h * HEAD_DIM:h * HEAD_DIM + 1]
                    p = jnp.exp(s - ls)
                    dp = lax.dot_general(dom, vv, NT, preferred_element_type=F32)
                    delta = jnp.sum(jnp.where(msk, prod, 0.0), axis=-1, keepdims=True)
                    ds = p * (dp - delta)
                    dsum_ref[h] += ds
                    if use_sink:
                        ce = jnp.exp(sink_ref[h] - ls) * delta
                        c_pair = jnp.broadcast_to(ce, (BLK, PAIR)) if e == 0 else jnp.where(msk, ce, c_pair)
                    dsb = ds.astype(BF16)
                    dqe = jnp.dot(dsb, km, preferred_element_type=F32)
                    dke = lax.dot_general(dsb, qm, TN, preferred_element_type=F32)
                    dve = lax.dot_general(p.astype(BF16), dom, TN, preferred_element_type=F32)
                    dq_pair = dqe if e == 0 else dq_pair + dqe
                    dks[kvp] = dke if dks[kvp] is None else dks[kvp] + dke
                    dvs[kvp] = dve if dvs[kvp] is None else dvs[kvp] + dve
                dq_ref[:, sl] = (dq_pair * Q_SCALE).astype(BF16)
                if use_sink:
                    dsk_ref[:, sl] += c_pair
            for kvp in range(kv_pairs):
                ksl = slice(kvp * PAIR, (kvp + 1) * PAIR)
                dk_ref[:, ksl] = (ck_ref[:, ksl] + dks[kvp][:BLK]).astype(BF16)
                dv_ref[:, ksl] = (cv_ref[:, ksl] + dvs[kvp][:BLK]).astype(BF16)
                ck_ref[:, ksl] = dks[kvp][BLK:]
                cv_ref[:, ksl] = dvs[kvp][BLK:]

        @pl.when(i == nb)
        def _():
            dk_ref[...] = ck_ref[...].astype(BF16)
            dv_ref[...] = cv_ref[...].astype(BF16)

    q_spec, kp_spec, kc_spec, b_spec = _attn_specs(dil, kvw, nb, True)
    dkv_spec = pl.BlockSpec((BLK, kvw), lambda r, i: (jnp.maximum(i - 1, 0), r))
    return pl.pallas_call(
        body, name=name, grid=(dil, nb + 1),
        in_specs=[pl.BlockSpec(memory_space=pltpu.SMEM), q_spec, kp_spec, kc_spec, kp_spec, kc_spec,
                  q_spec, q_spec, q_spec, b_spec],
        out_specs=[q_spec, dkv_spec, dkv_spec,
                   pl.BlockSpec((N_HEADS, BLK, 2 * BLK), lambda r, i: (0, 0, 0)),
                   pl.BlockSpec((BLK, WIDTH), lambda r, i: (0, 0))],
        out_shape=[jax.ShapeDtypeStruct((L, dil * WIDTH), BF16),
                   jax.ShapeDtypeStruct((L, dil * kvw), BF16),
                   jax.ShapeDtypeStruct((L, dil * kvw), BF16),
                   jax.ShapeDtypeStruct((N_HEADS, BLK, 2 * BLK), F32),
                   jax.ShapeDtypeStruct((BLK, WIDTH), F32)],
        scratch_shapes=[pltpu.VMEM((BLK, kvw), F32), pltpu.VMEM((BLK, kvw), F32)],
        compiler_params=_params(("arbitrary", "arbitrary")),
    )(sinks, q, k, k, v, v, o, do, lse, bias)


def _merge_wo(x, oa, o1, o2, o3, l1, l2, l3, ga, gb, wo, gf, *, tm=512):
    T = x.shape[0]

    def body(x_ref, oa_ref, o1_ref, o2_ref, o3_ref, l1_ref, l2_ref, l3_ref, ga_ref, gb_ref, wo_ref, gf_ref,
             x2_ref, mix_ref, h2_ref, ob1_ref, ob4_ref, ob16_ref, ls1_ref, ls4_ref, ls16_ref, so2, so3, sl2, sl3):
        _unstride(o2_ref, so2, BRANCHES[1][1], tm)
        _unstride(o3_ref, so3, BRANCHES[2][1], tm)
        _unstride(l2_ref, sl2, BRANCHES[1][1], tm)
        _unstride(l3_ref, sl3, BRANCHES[2][1], tm)
        la, lb, lc = l1_ref[...], _scr_get(sl2), _scr_get(sl3)
        m = jnp.maximum(jnp.maximum(la, lb), lc)
        ea, eb, ec = jnp.exp(la - m), jnp.exp(lb - m), jnp.exp(lc - m)
        den = ea + eb + ec
        inv = 1.0 / den
        ob = (ea * o1_ref[...].astype(F32) + eb * _scr_get(so2) + ec * _scr_get(so3)) * inv
        _scr_put(so2, ob)
        _scr_put(sl2, m + jnp.log(den))
        for (_, dil), o_ref, l_ref in zip(BRANCHES, (ob1_ref, ob4_ref, ob16_ref), (ls1_ref, ls4_ref, ls16_ref)):
            _restride(so2, o_ref, dil, tm)
            _restride(sl2, l_ref, dil, tm)
        oav = oa_ref[...].astype(F32)
        ra = lax.rsqrt(jnp.mean(oav * oav, axis=-1, keepdims=True) + EPS)
        rb = lax.rsqrt(jnp.mean(ob * ob, axis=-1, keepdims=True) + EPS)
        mix_ref[:, :WIDTH] = (oav * ra * ga_ref[...]).astype(BF16)
        mix_ref[:, WIDTH:] = (ob * rb * gb_ref[...]).astype(BF16)
        x2 = x_ref[...] + jnp.dot(mix_ref[...], wo_ref[...], preferred_element_type=F32)
        x2_ref[...] = x2
        r2 = lax.rsqrt(jnp.mean(x2 * x2, axis=-1, keepdims=True) + EPS)
        h2_ref[...] = (x2 * r2 * gf_ref[...]).astype(BF16)

    row = lambda w_: pl.BlockSpec((tm, w_), lambda i: (i, 0))
    full = lambda a: pl.BlockSpec(a.shape, lambda i: (0, 0))
    return pl.pallas_call(
        body, name="merge_wo", grid=(T // tm,),
        in_specs=[row(D_MODEL), row(WIDTH)] + _view_specs(tm) * 2 + [full(ga), full(gb), full(wo), full(gf)],
        out_specs=[row(D_MODEL), row(D_MODEL), row(D_MODEL)] + _view_specs(tm) * 2,
        out_shape=[jax.ShapeDtypeStruct((T, D_MODEL), F32), jax.ShapeDtypeStruct((T, D_MODEL), BF16),
                   jax.ShapeDtypeStruct((T, D_MODEL), BF16)] + _view_shapes(T, BF16) + _view_shapes(T, F32),
        scratch_shapes=[_scr(tm)] * 4,
        compiler_params=_params(("arbitrary",)),
    )(x, oa, o1, o2, o3, l1, l2, l3, ga, gb, wo, gf)


def _ffn_up(h2, wg, wu, *, tm=1024, fc=256, rc=512):
    T = h2.shape[0]

    def body(h_ref, wg_ref, wu_ref, gate_ref, up_ref, act_ref):
        for s in range(0, tm, rc):
            h = h_ref[s:s + rc, :]
            gt = jnp.dot(h, wg_ref[...], preferred_element_type=F32)
            u = jnp.dot(h, wu_ref[...], preferred_element_type=F32)
            gate_ref[s:s + rc, :] = gt.astype(BF16)
            up_ref[s:s + rc, :] = u.astype(BF16)
            act_ref[s:s + rc, :] = (gt * (1.0 / (1.0 + jnp.exp(-gt))) * u).astype(BF16)

    rowd = pl.BlockSpec((tm, D_MODEL), lambda i, c: (i, 0))
    wcol = pl.BlockSpec((D_MODEL, fc), lambda i, c: (0, c))
    oc = pl.BlockSpec((tm, fc), lambda i, c: (i, c))
    return pl.pallas_call(
        body, name="ffn_up", grid=(T // tm, D_FF // fc),
        in_specs=[rowd, wcol, wcol],
        out_specs=[oc, oc, oc],
        out_shape=[jax.ShapeDtypeStruct((T, D_FF), BF16)] * 3,
        compiler_params=_params(("arbitrary", "arbitrary")),
    )(h2, wg, wu)


def _ffn_down_loss(act, wd, x2, tgt, g, *, tm=512, rc=256):
    T = x2.shape[0]

    def body(act_ref, wd_ref, x2_ref, tgt_ref, g_ref, dx_ref, dxb_ref, loss_ref, dg_ref):
        @pl.when(pl.program_id(0) == 0)
        def _():
            loss_ref[...] = jnp.zeros_like(loss_ref)
            dg_ref[...] = jnp.zeros_like(dg_ref)

        gv = g_ref[...]
        lsum = jnp.zeros((1, 1), F32)
        dgs = jnp.zeros((1, D_MODEL), F32)
        for c in range(0, tm, rc):
            x3 = x2_ref[c:c + rc, :] + jnp.dot(act_ref[c:c + rc, :], wd_ref[...], preferred_element_type=F32)
            r = lax.rsqrt(jnp.mean(x3 * x3, axis=-1, keepdims=True) + EPS)
            xh = x3 * r
            diff = xh * gv - tgt_ref[c:c + rc, :]
            lsum = lsum + jnp.sum(jnp.sum(diff * diff, axis=-1, keepdims=True), axis=0, keepdims=True)
            dy = diff * (1.0 / D_MODEL)
            dgs = dgs + jnp.sum(dy * xh, axis=0, keepdims=True)
            dx = _rms_bwd(dy, xh, r, gv)
            dx_ref[c:c + rc, :] = dx
            dxb_ref[c:c + rc, :] = dx.astype(BF16)
        loss_ref[...] += lsum * (0.5 / D_MODEL)
        dg_ref[...] += dgs

    rowd = pl.BlockSpec((tm, D_MODEL), lambda i: (i, 0))
    return pl.pallas_call(
        body, name="ffn_down_loss", grid=(T // tm,),
        in_specs=[pl.BlockSpec((tm, D_FF), lambda i: (i, 0)), pl.BlockSpec((D_FF, D_MODEL), lambda i: (0, 0)),
                  rowd, rowd, pl.BlockSpec(g.shape, lambda i: (0, 0))],
        out_specs=[rowd, rowd, pl.BlockSpec((1, 1), lambda i: (0, 0)), pl.BlockSpec((1, D_MODEL), lambda i: (0, 0))],
        out_shape=[jax.ShapeDtypeStruct((T, D_MODEL), F32), jax.ShapeDtypeStruct((T, D_MODEL), BF16),
                   jax.ShapeDtypeStruct((1, 1), F32), jax.ShapeDtypeStruct((1, D_MODEL), F32)],
        compiler_params=_params(("arbitrary",)),
    )(act, wd, x2, tgt, g)


def _ffn_bwd_act(dx3b, gate, up, wd, *, tm=1024, fc=256, rc=512):
    T = dx3b.shape[0]

    def body(dxb_ref, gate_ref, up_ref, wd_ref, dgate_ref, dup_ref):
        for s in range(0, tm, rc):
            dact = lax.dot_general(dxb_ref[s:s + rc, :], wd_ref[...], NT, preferred_element_type=F32)
            gt = gate_ref[s:s + rc, :].astype(F32)
            u = up_ref[s:s + rc, :].astype(F32)
            sg = 1.0 / (1.0 + jnp.exp(-gt))
            dgate_ref[s:s + rc, :] = (dact * u * sg * (1.0 + gt * (1.0 - sg))).astype(BF16)
            dup_ref[s:s + rc, :] = (dact * gt * sg).astype(BF16)

    rowd = pl.BlockSpec((tm, D_MODEL), lambda i, c: (i, 0))
    oc = pl.BlockSpec((tm, fc), lambda i, c: (i, c))
    return pl.pallas_call(
        body, name="ffn_bwd_act", grid=(T // tm, D_FF // fc),
        in_specs=[rowd, oc, oc, pl.BlockSpec((fc, D_MODEL), lambda i, c: (c, 0))],
        out_specs=[oc, oc],
        out_shape=[jax.ShapeDtypeStruct((T, D_FF), BF16), jax.ShapeDtypeStruct((T, D_FF), BF16)],
        compiler_params=_params(("arbitrary", "arbitrary")),
    )(dx3b, gate, up, wd)


def _ffn_bwd_in(dgate, dup, wg, wu, x2, dx3, g, *, tm=512, rc=256):
    T = x2.shape[0]

    def body(dgate_ref, dup_ref, wg_ref, wu_ref, x2_ref, dx_ref, g_ref, dx2_ref, dx2b_ref, dg_ref):
        @pl.when(pl.program_id(0) == 0)
        def _():
            dg_ref[...] = jnp.zeros_like(dg_ref)

        gv = g_ref[...]
        dgs = jnp.zeros((1, D_MODEL), F32)
        for s in range(0, tm, rc):
            dh = (lax.dot_general(dgate_ref[s:s + rc, :], wg_ref[...], NT, preferred_element_type=F32)
                  + lax.dot_general(dup_ref[s:s + rc, :], wu_ref[...], NT, preferred_element_type=F32))
            xv = x2_ref[s:s + rc, :]
            r = lax.rsqrt(jnp.mean(xv * xv, axis=-1, keepdims=True) + EPS)
            xh = xv * r
            dgs = dgs + jnp.sum(dh * xh, axis=0, keepdims=True)
            d = dx_ref[s:s + rc, :] + _rms_bwd(dh, xh, r, gv)
            dx2_ref[s:s + rc, :] = d
            dx2b_ref[s:s + rc, :] = d.astype(BF16)
        dg_ref[...] += dgs

    rowd = pl.BlockSpec((tm, D_MODEL), lambda i: (i, 0))
    rowf = pl.BlockSpec((tm, D_FF), lambda i: (i, 0))
    wfull = pl.BlockSpec((D_MODEL, D_FF), lambda i: (0, 0))
    return pl.pallas_call(
        body, name="ffn_bwd_in", grid=(T // tm,),
        in_specs=[rowf, rowf, wfull, wfull, rowd, rowd, pl.BlockSpec(g.shape, lambda i: (0, 0))],
        out_specs=[rowd, rowd, pl.BlockSpec((1, D_MODEL), lambda i: (0, 0))],
        out_shape=[jax.ShapeDtypeStruct((T, D_MODEL), F32), jax.ShapeDtypeStruct((T, D_MODEL), BF16),
                   jax.ShapeDtypeStruct((1, D_MODEL), F32)],
        compiler_params=_params(("arbitrary",)),
    )(dgate, dup, wg, wu, x2, dx3, g)


def _matmul_tn(a, b, *, tk, tn, tt=2048, out_dtype=BF16, name):
    T, K = a.shape
    N = b.shape[1]
    nt = T // tt

    def body(a_ref, b_ref, o_ref, acc_ref):
        part = lax.dot_general(a_ref[...], b_ref[...], TN, preferred_element_type=F32)

        @pl.when(pl.program_id(2) == 0)
        def _():
            acc_ref[...] = part

        @pl.when(pl.program_id(2) > 0)
        def _():
            acc_ref[...] += part

        @pl.when(pl.program_id(2) == nt - 1)
        def _():
            o_ref[...] = acc_ref[...].astype(out_dtype)

    return pl.pallas_call(
        body, name=name, grid=(K // tk, N // tn, nt),
        in_specs=[pl.BlockSpec((tt, tk), lambda i, j, t: (t, i)), pl.BlockSpec((tt, tn), lambda i, j, t: (t, j))],
        out_specs=pl.BlockSpec((tk, tn), lambda i, j, t: (i, j)),
        out_shape=jax.ShapeDtypeStruct((K, N), out_dtype),
        scratch_shapes=[pltpu.VMEM((tk, tn), F32)],
        compiler_params=_params(("arbitrary", "arbitrary", "arbitrary")),
    )(a, b)


def _wo_bwd(dx2b, wo, oa, ob, ga, gb, *, tm=512):
    T = dx2b.shape[0]

    def body(dx_ref, wo_ref, oa_ref, ob_ref, ga_ref, gb_ref, doa_ref, dob1_ref, dob4_ref, dob16_ref, dga_ref, dgb_ref, scr):
        @pl.when(pl.program_id(0) == 0)
        def _():
            dga_ref[...] = jnp.zeros_like(dga_ref)
            dgb_ref[...] = jnp.zeros_like(dgb_ref)

        dm = lax.dot_general(dx_ref[...], wo_ref[...], NT, preferred_element_type=F32)
        for o_ref, g_ref, dg_ref, sl in ((oa_ref, ga_ref, dga_ref, slice(0, WIDTH)),
                                         (ob_ref, gb_ref, dgb_ref, slice(WIDTH, 2 * WIDTH))):
            ov = o_ref[...].astype(F32)
            r = lax.rsqrt(jnp.mean(ov * ov, axis=-1, keepdims=True) + EPS)
            xh = ov * r
            d = dm[:, sl]
            dg_ref[...] += jnp.sum(d * xh, axis=0, keepdims=True)
            do = _rms_bwd(d, xh, r, g_ref[...])
            if o_ref is oa_ref:
                doa_ref[...] = do.astype(BF16)
            else:
                _scr_put(scr, do)
                for (_, dil), v_ref in zip(BRANCHES, (dob1_ref, dob4_ref, dob16_ref)):
                    _restride(scr, v_ref, dil, tm)

    row = lambda w_: pl.BlockSpec((tm, w_), lambda i: (i, 0))
    full = lambda a: pl.BlockSpec(a.shape, lambda i: (0, 0))
    return pl.pallas_call(
        body, name="wo_bwd", grid=(T // tm,),
        in_specs=[row(D_MODEL), full(wo), row(WIDTH), row(WIDTH), full(ga), full(gb)],
        out_specs=[row(WIDTH)] + _view_specs(tm)
        + [pl.BlockSpec((1, WIDTH), lambda i: (0, 0)), pl.BlockSpec((1, WIDTH), lambda i: (0, 0))],
        out_shape=[jax.ShapeDtypeStruct((T, WIDTH), BF16)] + _view_shapes(T, BF16)
        + [jax.ShapeDtypeStruct((1, WIDTH), F32), jax.ShapeDtypeStruct((1, WIDTH), F32)],
        scratch_shapes=[_scr(tm)],
        compiler_params=_params(("arbitrary",)),
    )(dx2b, wo, oa, ob, ga, gb)


def _dproj(dqa, dka, dva, dqs, dks, dvs, cos, sin, *, tm=512):
    T = dqa.shape[0]

    def body(dqa_ref, dka_ref, dva_ref, q1, q2, q3, k1, k2, k3, v1, v2, v3, cos_ref, sin_ref, dp_ref, db_ref, acc, tmp):
        @pl.when(pl.program_id(0) == 0)
        def _():
            db_ref[...] = jnp.zeros_like(db_ref)

        cosv = cos_ref[...]
        sinv = sin_ref[...]
        lane = lax.broadcasted_iota(jnp.int32, (tm, PAIR), 1)
        first = (lane % HEAD_DIM) < (HEAD_DIM // 2)

        def put(off, val):
            dp_ref[:, off:off + PAIR] = val.astype(BF16)
            db_ref[:, off:off + PAIR] += jnp.sum(val, axis=0, keepdims=True)

        for src, off, width in ((dqa_ref, 0, 512), (dka_ref, 512, 256)):
            for j in range(0, width, PAIR):
                d = src[:, j:j + PAIR].astype(F32)
                put(off + j, d * cosv - _rope_rot(d, first) * sinv)
        for j in range(0, 256, PAIR):
            put(768 + j, dva_ref[:, j:j + PAIR].astype(F32))
        for (a, b, c), off in (((q1, q2, q3), 1024), ((k1, k2, k3), 1536), ((v1, v2, v3), 2048)):
            _unstride(b, acc, BRANCHES[1][1], tm)
            _unstride(c, tmp, BRANCHES[2][1], tm)
            for j in range(N_CHUNK):
                put(off + j * PAIR, a[:, j * PAIR:(j + 1) * PAIR].astype(F32) + acc[j] + tmp[j])

    row = lambda w_: pl.BlockSpec((tm, w_), lambda i: (i, 0))
    return pl.pallas_call(
        body, name="dproj", grid=(T // tm,),
        in_specs=[row(512), row(256), row(256)] + _view_specs(tm) * 3 + [row(PAIR), row(PAIR)],
        out_specs=[row(D_INP), pl.BlockSpec((1, D_INP), lambda i: (0, 0))],
        out_shape=[jax.ShapeDtypeStruct((T, D_INP), BF16), jax.ShapeDtypeStruct((1, D_INP), F32)],
        scratch_shapes=[_scr(tm)] * 2,
        compiler_params=_params(("arbitrary",)),
    )(dqa, dka, dva, *dqs, *dks, *dvs, cos, sin)


def _inproj_bwd(dp, w, x, dx2, g, *, tm=512):
    T = x.shape[0]

    def body(dp_ref, w_ref, x_ref, dx2_ref, g_ref, gx_ref, dg_ref):
        @pl.when(pl.program_id(0) == 0)
        def _():
            dg_ref[...] = jnp.zeros_like(dg_ref)

        dh = lax.dot_general(dp_ref[...], w_ref[...], NT, preferred_element_type=F32)
        xv = x_ref[...]
        r = lax.rsqrt(jnp.mean(xv * xv, axis=-1, keepdims=True) + EPS)
        xh = xv * r
        dg_ref[...] += jnp.sum(dh * xh, axis=0, keepdims=True)
        gx_ref[...] = dx2_ref[...] + _rms_bwd(dh, xh, r, g_ref[...])

    row = lambda w_: pl.BlockSpec((tm, w_), lambda i: (i, 0))
    full = lambda a: pl.BlockSpec(a.shape, lambda i: (0, 0))
    return pl.pallas_call(
        body, name="inproj_bwd", grid=(T // tm,),
        in_specs=[row(D_INP), full(w), row(D_MODEL), row(D_MODEL), full(g)],
        out_specs=[row(D_MODEL), pl.BlockSpec((1, D_MODEL), lambda i: (0, 0))],
        out_shape=[jax.ShapeDtypeStruct((T, D_MODEL), F32), jax.ShapeDtypeStruct((1, D_MODEL), F32)],
        compiler_params=_params(("arbitrary",)),
    )(dp, w, x, dx2, g)


def _bias_sink_grads(dsums, bmaps, dsk):
    def body(s1, s2, s3, m1, m2, m3, dsk_ref, drel_ref, dsink_ref):
        row = lax.broadcasted_iota(jnp.int32, (N_HEADS, 128), 0)
        lane = lax.broadcasted_iota(jnp.int32, (N_HEADS, 128), 1)
        out = jnp.zeros((N_HEADS, 128), F32)
        for s_ref, m_ref in ((s1, m1), (s2, m2), (s3, m3)):
            bm = m_ref[...]
            for h in range(N_HEADS):
                a = s_ref[h]
                for b in range(REL_BUCKETS):
                    v = jnp.sum(jnp.sum(jnp.where(bm == b, a, 0.0), axis=-1, keepdims=True), axis=0, keepdims=True)
                    out = out + jnp.where((row == h) & (lane == b), v, 0.0)
        drel_ref[...] = out
        dsink_ref[...] = -jnp.sum(dsk_ref[...], axis=0, keepdims=True)

    vm = pl.BlockSpec(memory_space=pltpu.VMEM)
    return pl.pallas_call(
        body, name="bias_sink_grads",
        in_specs=[vm] * 7, out_specs=[vm, vm],
        out_shape=[jax.ShapeDtypeStruct((N_HEADS, 128), F32), jax.ShapeDtypeStruct((1, WIDTH), F32)],
        compiler_params=_params(),
    )(*dsums, *bmaps, dsk)


def _all_gather(blk, *, name):
    R, C = blk.shape

    def body(x_ref, out_ref, send_sems, recv_sems, local_sem):
        x, y, c = lax.axis_index("x"), lax.axis_index("y"), lax.axis_index("c")
        me, sibling = (x, y, c), (x, y, 1 - c)
        chips = [(1 - x, y), (x, 1 - y), (1 - x, 1 - y)]

        def slot(px, py, pc):
            return out_ref.at[4 * px + 2 * py + pc]

        def copy(k, block, to, src=None):
            return pltpu.make_async_remote_copy(
                src_ref=slot(*block) if src is None else src, dst_ref=slot(*block),
                send_sem=send_sems.at[k], recv_sem=recv_sems.at[k], device_id=to, device_id_type=MESH)

        mine = pltpu.make_async_copy(x_ref, slot(*me), local_sem)
        mine.start()
        first = [copy(0, me, sibling, src=x_ref)]
        first += [copy(1 + j, me, (*chip, c), src=x_ref) for j, chip in enumerate(chips)]
        for cp in first:
            cp.start()
        passed = [copy(4 + j, (*chip, c), sibling) for j, chip in enumerate(chips)]
        for j, chip in enumerate(chips):
            copy(1 + j, (*chip, c), me).wait_recv()
            passed[j].start()
        copy(0, sibling, me).wait_recv()
        for j, chip in enumerate(chips):
            copy(4 + j, (*chip, 1 - c), me).wait_recv()
        for cp in first + passed:
            cp.wait_send()
        mine.wait()

    return pl.pallas_call(
        body, name=name,
        in_specs=[pl.BlockSpec(memory_space=pl.ANY)], out_specs=pl.BlockSpec(memory_space=pl.ANY),
        out_shape=jax.ShapeDtypeStruct((N_DEV, R, C), blk.dtype),
        scratch_shapes=[pltpu.SemaphoreType.DMA((7,)), pltpu.SemaphoreType.DMA((7,)), pltpu.SemaphoreType.DMA],
        compiler_params=pltpu.CompilerParams(has_side_effects=True),
    )(blk)


def _scatter_blocks(parts, *, name):
    _, R, C = parts.shape

    def body(p_ref, out_ref, send_sems, recv_sems, local_sem):
        x, y, c = lax.axis_index("x"), lax.axis_index("y"), lax.axis_index("c")
        mine = 4 * x + 2 * y + c
        peers = [(x ^ (k >> 2), y ^ ((k >> 1) & 1), c ^ (k & 1)) for k in range(1, N_DEV)]

        def copy(k, peer):
            dest = 4 * peer[0] + 2 * peer[1] + peer[2]
            return pltpu.make_async_remote_copy(
                src_ref=p_ref.at[dest], dst_ref=out_ref.at[mine],
                send_sem=send_sems.at[k], recv_sem=recv_sems.at[k], device_id=peer, device_id_type=MESH)

        own = pltpu.make_async_copy(p_ref.at[mine], out_ref.at[mine], local_sem)
        own.start()
        cps = [copy(k, peer) for k, peer in enumerate(peers)]
        for cp in cps:
            cp.start()
        for k, peer in enumerate(peers):
            src = 4 * peer[0] + 2 * peer[1] + peer[2]
            pltpu.make_async_remote_copy(
                src_ref=p_ref.at[src], dst_ref=out_ref.at[src],
                send_sem=send_sems.at[k], recv_sem=recv_sems.at[k], device_id=peer, device_id_type=MESH).wait_recv()
        for cp in cps:
            cp.wait_send()
        own.wait()

    return pl.pallas_call(
        body, name=name,
        in_specs=[pl.BlockSpec(memory_space=pl.ANY)], out_specs=pl.BlockSpec(memory_space=pl.ANY),
        out_shape=jax.ShapeDtypeStruct(parts.shape, parts.dtype),
        scratch_shapes=[pltpu.SemaphoreType.DMA((7,)), pltpu.SemaphoreType.DMA((7,)), pltpu.SemaphoreType.DMA],
        compiler_params=pltpu.CompilerParams(has_side_effects=True),
    )(parts)


def _adam_math(w, g, m, v):
    m = ADAM_B1 * m + (1.0 - ADAM_B1) * g
    v = ADAM_B2 * v + (1.0 - ADAM_B2) * (g * g)
    m_hat = m / (1.0 - ADAM_B1 ** ADAM_STEP)
    v_hat = v / (1.0 - ADAM_B2 ** ADAM_STEP)
    delta = -ADAM_LR * (m_hat / (jnp.sqrt(v_hat) + ADAM_EPS) + ADAM_WD * w)
    return delta, m, v


def _adamw(parts, w, m, v, *, name):
    R, C = w.shape
    tr = R // 2
    assert tr % 16 == 0

    def body(p_ref, w_ref, m_ref, v_ref, g_ref, d_ref, nm_ref, nv_ref):
        g = p_ref[0].astype(F32)
        for s in range(1, N_DEV):
            g = g + p_ref[s].astype(F32)
        d, nm, nv = _adam_math(w_ref[...], g, m_ref[...], v_ref[...])
        g_ref[...] = g
        d_ref[...] = d
        nm_ref[...] = nm
        nv_ref[...] = nv

    blk = pl.BlockSpec((tr, C), lambda i: (i, 0))
    return pl.pallas_call(
        body, name=name, grid=(R // tr,),
        in_specs=[pl.BlockSpec((N_DEV, tr, C), lambda i: (0, i, 0)), blk, blk, blk],
        out_specs=[blk] * 4, out_shape=[jax.ShapeDtypeStruct((R, C), F32)] * 4,
        compiler_params=_params(("arbitrary",)),
    )(parts, w, m, v)


def _adamw_small(parts, w, m, v):
    def body(p_ref, w_ref, m_ref, v_ref, g_ref, d_ref, nm_ref, nv_ref):
        g = p_ref[0]
        for s in range(1, N_DEV):
            g = g + p_ref[s]
        d, nm, nv = _adam_math(w_ref[...], g, m_ref[...], v_ref[...])
        g_ref[...] = g
        d_ref[...] = d
        nm_ref[...] = nm
        nv_ref[...] = nv

    vm = pl.BlockSpec(memory_space=pltpu.VMEM)
    return pl.pallas_call(
        body, name="adamw_small", in_specs=[vm] * 4, out_specs=[vm] * 4,
        out_shape=[jax.ShapeDtypeStruct((SMALL_ROWS, 128), F32)] * 4, compiler_params=_params(),
    )(parts, w, m, v)


def _t5_bucket(dist):
    max_exact = REL_BUCKETS // 2
    df = jnp.maximum(dist, 1).astype(F32)
    large = max_exact + (jnp.log(df / max_exact) / math.log(REL_MAX_DISTANCE / max_exact)
                         * (REL_BUCKETS - max_exact)).astype(jnp.int32)
    large = jnp.minimum(large, REL_BUCKETS - 1)
    return jnp.where(dist < max_exact, dist, large)


def _band_tables(rel_table, dil, n_back):
    qi = jnp.arange(BLK)[:, None]
    kj = jnp.arange(2 * BLK)[None, :]
    delta = BLK + qi - kj
    in_band = (delta >= 0) & (delta <= n_back)
    if rel_table is None:
        vals = jnp.zeros((N_HEADS, BLK, 2 * BLK), F32)
        bmap = None
    else:
        bucket = _t5_bucket(jnp.clip(delta, 0, n_back) * dil)
        vals = jnp.zeros((N_HEADS, BLK, 2 * BLK), F32)
        for b in range(REL_BUCKETS):
            vals = jnp.where((bucket == b)[None], rel_table[b][:, None, None], vals)
        bmap = jnp.where(in_band, bucket, -1).astype(jnp.int32)
    later = jnp.where(in_band[None], vals, NEG)
    first = jnp.where((in_band & (kj >= BLK))[None], vals, NEG)
    return jnp.stack([later, first]), bmap


def _rope_tables(T):
    half = HEAD_DIM // 2
    inv_freq = ROPE_THETA ** (-jnp.arange(half, dtype=F32) / half)
    ang = jnp.arange(T, dtype=F32)[:, None] * inv_freq[None, :]
    cos, sin = jnp.cos(ang), jnp.sin(ang)
    return jnp.tile(cos, (1, 4)), jnp.tile(jnp.concatenate([-sin, sin], axis=1), (1, 2))


def _dup_heads(a):
    h0, h1 = a[..., :HEAD_DIM], a[..., HEAD_DIM:]
    return jnp.concatenate([h0, h0, h1, h1], axis=-1)


def _widen_in(a):
    return jnp.concatenate([a[..., :512], _dup_heads(a[..., 512:640]), _dup_heads(a[..., 640:768]), a[..., 768:]], axis=-1)


def _fold_in(a):
    def fold(t):
        return jnp.concatenate([t[..., 0:64] + t[..., 64:128], t[..., 128:192] + t[..., 192:256]], axis=-1)
    return jnp.concatenate([a[..., :512], fold(a[..., 512:768]), fold(a[..., 768:1024]), a[..., 1024:]], axis=-1)


def _local_step(x, tgt, g_attn, win, b_in, sinks, rel_table, g_out_a, g_out_b, wo, g_ffn, wg, wu, wd, g_final):
    T = x.shape[0]
    cos, sin = _rope_tables(T)
    winp = _widen_in(win)
    binp = _widen_in(b_in)
    g_final2 = g_final.reshape(1, D_MODEL)
    sink8 = sinks.reshape(N_HEADS)

    bias_a, _ = _band_tables(None, 1, BLK - 1)
    tabs = [_band_tables(rel_table, dil, window // dil) for window, dil in BRANCHES]

    h1, qa, ka, va, *qkv_b = _norm_proj(x, g_attn, winp, binp, cos, sin)
    qbs, kbs, vbs = qkv_b[0:3], qkv_b[3:6], qkv_b[6:9]
    oa, lse_a = _attn_fwd(qa, ka, va, bias_a, sink8, dil=1, kv_pairs=2, use_sink=True, name="attn_a_fwd")
    outs = [_attn_fwd(qbs[n], kbs[n], vbs[n], tabs[n][0], sink8, dil=dil, kv_pairs=4, use_sink=False,
                      name=f"attn_b{n}_fwd") for n, (_, dil) in enumerate(BRANCHES)]
    x2, mixed, h2, *ob_lse = _merge_wo(x, oa, outs[0][0], outs[1][0], outs[2][0], outs[0][1], outs[1][1], outs[2][1],
                                       g_out_a, g_out_b, wo, g_ffn)
    obs, lses = ob_lse[0:3], ob_lse[3:6]
    gate, up, act = _ffn_up(h2, wg, wu)
    dx3, dx3b, loss, dg_final = _ffn_down_loss(act, wd, x2, tgt, g_final2)

    dgate, dup = _ffn_bwd_act(dx3b, gate, up, wd)
    dx2, dx2b, dg_ffn = _ffn_bwd_in(dgate, dup, wg, wu, x2, dx3, g_ffn)
    dwd = _matmul_tn(act, dx3b, tk=1408, tn=1024, name="dw_down")
    dwg = _matmul_tn(h2, dgate, tk=1024, tn=1408, name="dw_gate")
    dwu = _matmul_tn(h2, dup, tk=1024, tn=1408, name="dw_up")
    dwo = _matmul_tn(mixed, dx2b, tk=1024, tn=1024, name="dw_o")
    doa, *dobs, dg_out_a, dg_out_b = _wo_bwd(dx2b, wo, oa, obs[0], g_out_a, g_out_b)

    dqa, dka, dva, _, dsk = _attn_bwd(qa, ka, va, oa, doa, lse_a, bias_a, sink8, dil=1, kv_pairs=2, use_sink=True,
                                      name="attn_a_bwd")
    res = [_attn_bwd(qbs[n], kbs[n], vbs[n], obs[n], dobs[n], lses[n], tabs[n][0], sink8, dil=dil, kv_pairs=4,
                     use_sink=False, name=f"attn_b{n}_bwd") for n, (_, dil) in enumerate(BRANCHES)]
    dp, dbp = _dproj(dqa, dka, dva, [r[0] for r in res], [r[1] for r in res], [r[2] for r in res], cos, sin)
    grad_x, dg_attn = _inproj_bwd(dp, winp, x, dx2, g_attn)
    dwin = _fold_in(_matmul_tn(h1, dp, tk=1024, tn=1280, out_dtype=F32, name="dw_in"))
    drel, dsink = _bias_sink_grads([r[3] for r in res], [t[1] for t in tabs], dsk)

    small = dict(
        g_attn=dg_attn, b_in=_fold_in(dbp), sinks=dsink[:, ::HEAD_DIM], rel_table=drel[:, :REL_BUCKETS].T,
        g_out_a=dg_out_a, g_out_b=dg_out_b, g_ffn=dg_ffn, g_final=dg_final.reshape(D_MODEL))
    return loss[0, 0], grad_x, dict(w_in=dwin, w_o=dwo, w_gate=dwg, w_up=dwu, w_down=dwd), small


SMALL_NAMES = ("g_attn", "b_in", "sinks", "rel_table", "g_out_a", "g_out_b", "g_ffn", "g_final")


def _pack_small(vals):
    flat = jnp.concatenate([vals[n].reshape(-1).astype(F32) for n in SMALL_NAMES])
    return jnp.pad(flat, (0, SMALL_ROWS * 128 - flat.shape[0])).reshape(SMALL_ROWS, 128)


def _unpack_small(packed, like):
    flat = packed.reshape(-1)
    out, off = {}, 0
    for n in SMALL_NAMES:
        size = like[n].size
        out[n] = flat[off:off + size].reshape(like[n].shape)
        off += size
    return out


def _cols_to_rows(a):
    return a.reshape(a.shape[1], D_MODEL)


def kernel(x, g_attn, w_in, b_in, sinks, rel_table, g_out_a, g_out_b, w_o, g_ffn, w_gate, w_up, w_down, g_final, loss_target, m_g_attn, m_w_in, m_b_in, m_sinks, m_rel_table, m_g_out_a, m_g_out_b, m_w_o, m_g_ffn, m_w_gate, m_w_up, m_w_down, m_g_final, v_g_attn, v_w_in, v_b_in, v_sinks, v_rel_table, v_g_out_a, v_g_out_b, v_w_o, v_g_ffn, v_w_gate, v_w_up, v_w_down, v_g_final):
    T = x.shape[1]
    flat = jnp.concatenate([_cols_to_rows(w_in[0].astype(BF16)), w_o[0].astype(BF16), _cols_to_rows(w_gate[0].astype(BF16)),
                            _cols_to_rows(w_up[0].astype(BF16)), w_down[0].astype(BF16)], axis=0)
    gw = _all_gather(flat, name="gather_weights")
    o0, o1, o2, o3, o4 = 0, 288, 416, 768, 1120

    def cols(seg, n):
        return seg.reshape(N_DEV, D_MODEL, n).transpose(1, 0, 2).reshape(D_MODEL, N_DEV * n)

    win = cols(gw[:, o0:o1], 288)
    wo = gw[:, o1:o2].reshape(D_MODEL, D_MODEL)
    wg = cols(gw[:, o2:o3], 352)
    wu = cols(gw[:, o3:o4], 352)
    wd = gw[:, o4:].reshape(D_FF, D_MODEL)

    loss_part, grad_x, dws, small = _local_step(
        x[0], loss_target[0], g_attn, win, b_in, sinks, rel_table, g_out_a, g_out_b, wo, g_ffn, wg, wu, wd, g_final)
    loss = lax.psum(loss_part, ("x", "y", "c"))

    def col_parts(g, n):
        return g.astype(BF16).reshape(D_MODEL, N_DEV, n).transpose(1, 0, 2).reshape(N_DEV, n, D_MODEL)

    parts = jnp.concatenate([
        col_parts(dws["w_in"], 288), dws["w_o"].astype(BF16).reshape(N_DEV, 128, D_MODEL),
        col_parts(dws["w_gate"], 352), col_parts(dws["w_up"], 352),
        dws["w_down"].astype(BF16).reshape(N_DEV, 352, D_MODEL)], axis=1)
    got = _scatter_blocks(parts, name="scatter_grads")

    shards = dict(w_in=(w_in, m_w_in, v_w_in, o0, o1, True), w_o=(w_o, m_w_o, v_w_o, o1, o2, False),
                  w_gate=(w_gate, m_w_gate, v_w_gate, o2, o3, True), w_up=(w_up, m_w_up, v_w_up, o3, o4, True),
                  w_down=(w_down, m_w_down, v_w_down, o4, FLAT_ROWS, False))
    big = {}
    for n, (w, m, v, lo, hi, by_cols) in shards.items():
        p = got[:, lo:hi]
        if by_cols:
            p = p.reshape(N_DEV, D_MODEL, hi - lo)
        big[n] = [a[None] for a in _adamw(p, w[0], m[0], v[0], name="adamw_" + n)]

    ws = dict(g_attn=g_attn, b_in=b_in, sinks=sinks, rel_table=rel_table, g_out_a=g_out_a, g_out_b=g_out_b,
              g_ffn=g_ffn, g_final=g_final)
    ms = dict(g_attn=m_g_attn, b_in=m_b_in, sinks=m_sinks, rel_table=m_rel_table, g_out_a=m_g_out_a,
              g_out_b=m_g_out_b, g_ffn=m_g_ffn, g_final=m_g_final)
    vs = dict(g_attn=v_g_attn, b_in=v_b_in, sinks=v_sinks, rel_table=v_rel_table, g_out_a=v_g_out_a,
              g_out_b=v_g_out_b, g_ffn=v_g_ffn, g_final=v_g_final)
    sparts = _all_gather(_pack_small(small), name="gather_small")
    sm = [_unpack_small(a, ws) for a in _adamw_small(sparts, _pack_small(ws), _pack_small(ms), _pack_small(vs))]

    order = ("g_attn", "w_in", "b_in", "sinks", "rel_table", "g_out_a", "g_out_b", "w_o", "g_ffn", "w_gate", "w_up",
             "w_down", "g_final")
    outs = [loss, grad_x[None]]
    for k in range(4):
        outs += [big[n][k] if n in big else sm[k][n] for n in order]
    return tuple(outs)
```

```python
import functools
import math

import jax
import jax.numpy as jnp
from jax import lax
from jax.experimental import pallas as pl
from jax.experimental.pallas import tpu as pltpu

F32 = jnp.float32
BF16 = jnp.bfloat16

N_DEV = 8
D_MODEL = 1024
HEAD_DIM = 64
N_HEADS = 8
PAIR = 2 * HEAD_DIM
WIDTH = N_HEADS * HEAD_DIM
D_IN = 2304
D_INP = 2560
D_FF = 2816
BLK = 128
ROPE_THETA = 150000.0
REL_BUCKETS = 32
REL_MAX_DISTANCE = 2048
EPS = 1e-5
NEG = -1e30
BRANCHES = ((128, 1), (512, 4), (2048, 16))
Q_SCALE = HEAD_DIM ** -0.5

ADAM_LR = 0.001
ADAM_B1 = 0.9
ADAM_B2 = 0.999
ADAM_EPS = 1e-08
ADAM_WD = 0.01
ADAM_STEP = 10

VMEM_LIMIT = 56 * 1024 * 1024
MESH = pl.DeviceIdType.MESH

NT = (((1,), (1,)), ((), ()))
TN = (((0,), (0,)), ((), ()))

REST_ROWS = 128 + 3 * 352
SMALL_ROWS = 56


def _params(sem=None):
    return pltpu.CompilerParams(dimension_semantics=sem, vmem_limit_bytes=VMEM_LIMIT)


def _rms_bwd(dh, xh, r, g):
    u = dh * g
    return r * (u - xh * jnp.mean(u * xh, axis=-1, keepdims=True))


def _rope_rot(t, first):
    return jnp.where(first, pltpu.roll(t, 96, 1), pltpu.roll(t, 32, 1))


N_CHUNK = WIDTH // PAIR


def _scr(tm):
    return pltpu.VMEM((N_CHUNK, tm, PAIR), F32)


def _scr_get(scr):
    return jnp.concatenate([scr[j] for j in range(N_CHUNK)], axis=1)


def _scr_put(scr, val):
    for j in range(N_CHUNK):
        scr[j] = val[:, j * PAIR:(j + 1) * PAIR]


def _unstride(view_ref, scr, dil, tm):
    n = tm // dil
    for r in range(dil):
        for j in range(N_CHUNK):
            col = r * WIDTH + j * PAIR
            scr.at[j][pl.ds(r, n, stride=dil), :] = view_ref[:, col:col + PAIR].astype(F32)


def _restride(scr, out_ref, dil, tm):
    n = tm // dil
    for r in range(dil):
        for j in range(N_CHUNK):
            col = r * WIDTH + j * PAIR
            rows = scr[j] if dil == 1 else scr.at[j][pl.ds(r, n, stride=dil), :]
            out_ref[:, col:col + PAIR] = rows.astype(out_ref.dtype)


def _view_specs(tm):
    return [pl.BlockSpec((tm // dil, dil * WIDTH), lambda i: (i, 0)) for _, dil in BRANCHES]


def _view_shapes(T, dtype):
    return [jax.ShapeDtypeStruct((T // dil, dil * WIDTH), dtype) for _, dil in BRANCHES]


def _norm_proj(x, g, w, b, cos, sin, *, tm=512):
    T = x.shape[0]

    def body(x_ref, g_ref, w_ref, b_ref, cos_ref, sin_ref, h_ref, qa_ref, ka_ref, va_ref, *rest):
        outs_b, ys = rest[:9], rest[9]
        xv = x_ref[...]
        r = lax.rsqrt(jnp.mean(xv * xv, axis=-1, keepdims=True) + EPS)
        h = (xv * r * g_ref[...]).astype(BF16)
        h_ref[...] = h
        cosv = cos_ref[...]
        sinv = sin_ref[...]
        lane = lax.broadcasted_iota(jnp.int32, (tm, PAIR), 1)
        first = (lane % HEAD_DIM) < (HEAD_DIM // 2)

        def proj(off):
            return jnp.dot(h, w_ref[:, off:off + 256], preferred_element_type=F32) + b_ref[:, off:off + 256]

        for (off, width, rot, scale), o_ref in zip(((0, 512, True, Q_SCALE), (512, 256, True, 1.0), (768, 256, False, 1.0)),
                                                   (qa_ref, ka_ref, va_ref)):
            for c in range(0, width, 256):
                y = proj(off + c)
                for j in range(0, 256, PAIR):
                    t = y[:, j:j + PAIR]
                    if rot:
                        t = t * cosv + _rope_rot(t, first) * sinv
                    if scale != 1.0:
                        t = t * scale
                    o_ref[:, c + j:c + j + PAIR] = t.astype(BF16)
        for n, (off, scale) in enumerate(((1024, Q_SCALE), (1536, 1.0), (2048, 1.0))):
            for c in range(0, WIDTH, 256):
                y = proj(off + c)
                y = y * scale if scale != 1.0 else y
                for j in range(0, 256, PAIR):
                    ys[(c + j) // PAIR] = y[:, j:j + PAIR]
            for (_, dil), o_ref in zip(BRANCHES, outs_b[3 * n:3 * n + 3]):
                _restride(ys, o_ref, dil, tm)

    row = lambda w_: pl.BlockSpec((tm, w_), lambda i: (i, 0))
    full = lambda a: pl.BlockSpec(a.shape, lambda i: (0, 0))
    return pl.pallas_call(
        body, name="norm_proj", grid=(T // tm,),
        in_specs=[row(D_MODEL), full(g), full(w), full(b), row(PAIR), row(PAIR)],
        out_specs=[row(D_MODEL), row(512), row(256), row(256)] + _view_specs(tm) * 3,
        out_shape=[jax.ShapeDtypeStruct((T, n), BF16) for n in (D_MODEL, 512, 256, 256)] + _view_shapes(T, BF16) * 3,
        scratch_shapes=[_scr(tm)],
        compiler_params=_params(("arbitrary",)),
    )(x, g, w, b, cos, sin)


def _attn_specs(dil, kvw, nb, clamp):
    qi = (lambda r, i: (jnp.minimum(i, nb - 1), r)) if clamp else (lambda r, i: (i, r))
    q_spec = pl.BlockSpec((BLK, WIDTH), qi)
    kp_spec = pl.BlockSpec((BLK, kvw), lambda r, i: (jnp.minimum(jnp.maximum(i - 1, 0), nb - 1), r))
    kc_spec = pl.BlockSpec((BLK, kvw), qi)
    b_spec = pl.BlockSpec((1, N_HEADS, BLK, 2 * BLK), lambda r, i: (jnp.where(i == 0, 1, 0), 0, 0, 0))
    return q_spec, kp_spec, kc_spec, b_spec


def _attn_fwd(q, k, v, bias, sinks, *, dil, kv_pairs, use_sink, name):
    L = q.shape[0]
    nb = L // BLK
    kvw = kv_pairs * PAIR
    rep = 4 // kv_pairs

    def body(sink_ref, q_ref, kp_ref, kc_ref, vp_ref, vc_ref, b_ref, o_ref, lse_ref):
        lo = lax.broadcasted_iota(jnp.int32, (1, PAIR), 1) < HEAD_DIM
        for hp in range(4):
            sl = slice(hp * PAIR, (hp + 1) * PAIR)
            ksl = slice((hp // rep) * PAIR, (hp // rep + 1) * PAIR)
            qp = q_ref[:, sl]
            kk = jnp.concatenate([kp_ref[:, ksl], kc_ref[:, ksl]], axis=0)
            vv = jnp.concatenate([vp_ref[:, ksl], vc_ref[:, ksl]], axis=0)
            o_pair = None
            lse_pair = None
            for e in range(2):
                h = 2 * hp + e
                msk = lo if e == 0 else jnp.logical_not(lo)
                qm = jnp.where(msk, qp, jnp.zeros_like(qp))
                s = lax.dot_general(qm, kk, NT, preferred_element_type=F32) + b_ref[0, h]
                m = jnp.max(s, axis=-1, keepdims=True)
                if use_sink:
                    sk = sink_ref[h]
                    m = jnp.maximum(m, sk)
                p = jnp.exp(s - m)
                l = jnp.sum(p, axis=-1, keepdims=True)
                if use_sink:
                    l = l + jnp.exp(sk - m)
                vm = jnp.where(msk, vv, jnp.zeros_like(vv))
                oe = jnp.dot(p.astype(BF16), vm, preferred_element_type=F32) * (1.0 / l)
                ls = m + jnp.log(l)
                if e == 0:
                    o_pair = oe
                    lse_pair = jnp.broadcast_to(ls, (BLK, PAIR))
                else:
                    o_pair = o_pair + oe
                    lse_pair = jnp.where(msk, ls, lse_pair)
            o_ref[:, sl] = o_pair.astype(BF16)
            lse_ref[:, sl] = lse_pair

    q_spec, kp_spec, kc_spec, b_spec = _attn_specs(dil, kvw, nb, False)
    return pl.pallas_call(
        body, name=name, grid=(dil, nb),
        in_specs=[pl.BlockSpec(memory_space=pltpu.SMEM), q_spec, kp_spec, kc_spec, kp_spec, kc_spec, b_spec],
        out_specs=[q_spec, q_spec],
        out_shape=[jax.ShapeDtypeStruct((L, dil * WIDTH), BF16), jax.ShapeDtypeStruct((L, dil * WIDTH), F32)],
        compiler_params=_params(("arbitrary", "arbitrary")),
    )(sinks, q, k, k, v, v, bias)


def _attn_bwd(q, k, v, o, do, lse, bias, sinks, *, dil, kv_pairs, use_sink, name):
    L = q.shape[0]
    nb = L // BLK
    kvw = kv_pairs * PAIR
    rep = 4 // kv_pairs

    def body(sink_ref, q_ref, kp_ref, kc_ref, vp_ref, vc_ref, o_ref, do_ref, lse_ref, b_ref,
             dq_ref, dk_ref, dv_ref, dsum_ref, dsk_ref, ck_ref, cv_ref):
        r = pl.program_id(0)
        i = pl.program_id(1)

        @pl.when((r == 0) & (i == 0))
        def _():
            dsum_ref[...] = jnp.zeros_like(dsum_ref)
            dsk_ref[...] = jnp.zeros_like(dsk_ref)

        @pl.when(i == 0)
        def _():
            ck_ref[...] = jnp.zeros_like(ck_ref)
            cv_ref[...] = jnp.zeros_like(cv_ref)

        @pl.when(i < nb)
        def _():
            lo = lax.broadcasted_iota(jnp.int32, (1, PAIR), 1) < HEAD_DIM
            dks = [None] * kv_pairs
            dvs = [None] * kv_pairs
            for hp in range(4):
                kvp = hp // rep
                sl = slice(hp * PAIR, (hp + 1) * PAIR)
                ksl = slice(kvp * PAIR, (kvp + 1) * PAIR)
                qp = q_ref[:, sl]
                dop = do_ref[:, sl]
                prod = dop.astype(F32) * o_ref[:, sl].astype(F32)
                kk = jnp.concatenate([kp_ref[:, ksl], kc_ref[:, ksl]], axis=0)
                vv = jnp.concatenate([vp_ref[:, ksl], vc_ref[:, ksl]], axis=0)
                dq_pair = None
                c_pair = None
                for e in range(2):
                    h = 2 * hp + e
                    msk = lo if e == 0 else jnp.logical_not(lo)
                    qm = jnp.where(msk, qp, jnp.zeros_like(qp))
                    dom = jnp.where(msk, dop, jnp.zeros_like(dop))
                    km = jnp.where(msk, kk, jnp.zeros_like(kk))
                    s = lax.dot_general(qm, kk, NT, preferred_element_type=F32) + b_ref[0, h]
                    ls = lse_ref[:, h * HEAD_DIM:h * HEAD_DIM + 1]
                    p = jnp.exp(s - ls)
                    dp = lax.dot_general(dom, vv, NT, preferred_element_type=F32)
                    delta = jnp.sum(jnp.where(msk, prod, 0.0), axis=-1, keepdims=True)
                    ds = p * (dp - delta)
                    dsum_ref[h] += ds
                    if use_sink:
                        ce = jnp.exp(sink_ref[h] - ls) * delta
                        c_pair = jnp.broadcast_to(ce, (BLK, PAIR)) if e == 0 else jnp.where(msk, ce, c_pair)
                    dsb = ds.astype(BF16)
                    dqe = jnp.dot(dsb, km, preferred_element_type=F32)
                    dke = lax.dot_general(dsb, qm, TN, preferred_element_type=F32)
                    dve = lax.dot_general(p.astype(BF16), dom, TN, preferred_element_type=F32)
                    dq_pair = dqe if e == 0 else dq_pair + dqe
                    dks[kvp] = dke if dks[kvp] is None else dks[kvp] + dke
                    dvs[kvp] = dve if dvs[kvp] is None else dvs[kvp] + dve
                dq_ref[:, sl] = (dq_pair * Q_SCALE).astype(BF16)
                if use_sink:
                    dsk_ref[:, sl] += c_pair
            for kvp in range(kv_pairs):
                ksl = slice(kvp * PAIR, (kvp + 1) * PAIR)
                dk_ref[:, ksl] = (ck_ref[:, ksl] + dks[kvp][:BLK]).astype(BF16)
                dv_ref[:, ksl] = (cv_ref[:, ksl] + dvs[kvp][:BLK]).astype(BF16)
                ck_ref[:, ksl] = dks[kvp][BLK:]
                cv_ref[:, ksl] = dvs[kvp][BLK:]

        @pl.when(i == nb)
        def _():
            dk_ref[...] = ck_ref[...].astype(BF16)
            dv_ref[...] = cv_ref[...].astype(BF16)

    q_spec, kp_spec, kc_spec, b_spec = _attn_specs(dil, kvw, nb, True)
    dkv_spec = pl.BlockSpec((BLK, kvw), lambda r, i: (jnp.maximum(i - 1, 0), r))
    return pl.pallas_call(
        body, name=name, grid=(dil, nb + 1),
        in_specs=[pl.BlockSpec(memory_space=pltpu.SMEM), q_spec, kp_spec, kc_spec, kp_spec, kc_spec,
                  q_spec, q_spec, q_spec, b_spec],
        out_specs=[q_spec, dkv_spec, dkv_spec,
                   pl.BlockSpec((N_HEADS, BLK, 2 * BLK), lambda r, i: (0, 0, 0)),
                   pl.BlockSpec((BLK, WIDTH), lambda r, i: (0, 0))],
        out_shape=[jax.ShapeDtypeStruct((L, dil * WIDTH), BF16),
                   jax.ShapeDtypeStruct((L, dil * kvw), BF16),
                   jax.ShapeDtypeStruct((L, dil * kvw), BF16),
                   jax.ShapeDtypeStruct((N_HEADS, BLK, 2 * BLK), F32),
                   jax.ShapeDtypeStruct((BLK, WIDTH), F32)],
        scratch_shapes=[pltpu.VMEM((BLK, kvw), F32), pltpu.VMEM((BLK, kvw), F32)],
        compiler_params=_params(("arbitrary", "arbitrary")),
    )(sinks, q, k, k, v, v, o, do, lse, bias)


def _merge_wo(x, oa, o1, o2, o3, l1, l2, l3, ga, gb, wo, gf, *, tm=512):
    T = x.shape[0]

    def body(x_ref, oa_ref, o1_ref, o2_ref, o3_ref, l1_ref, l2_ref, l3_ref, ga_ref, gb_ref, wo_ref, gf_ref,
             x2_ref, mix_ref, h2_ref, ob1_ref, ob4_ref, ob16_ref, ls1_ref, ls4_ref, ls16_ref, so2, so3, sl2, sl3):
        _unstride(o2_ref, so2, BRANCHES[1][1], tm)
        _unstride(o3_ref, so3, BRANCHES[2][1], tm)
        _unstride(l2_ref, sl2, BRANCHES[1][1], tm)
        _unstride(l3_ref, sl3, BRANCHES[2][1], tm)
        la, lb, lc = l1_ref[...], _scr_get(sl2), _scr_get(sl3)
        m = jnp.maximum(jnp.maximum(la, lb), lc)
        ea, eb, ec = jnp.exp(la - m), jnp.exp(lb - m), jnp.exp(lc - m)
        den = ea + eb + ec
        inv = 1.0 / den
        ob = (ea * o1_ref[...].astype(F32) + eb * _scr_get(so2) + ec * _scr_get(so3)) * inv
        _scr_put(so2, ob)
        _scr_put(sl2, m + jnp.log(den))
        for (_, dil), o_ref, l_ref in zip(BRANCHES, (ob1_ref, ob4_ref, ob16_ref), (ls1_ref, ls4_ref, ls16_ref)):
            _restride(so2, o_ref, dil, tm)
            _restride(sl2, l_ref, dil, tm)
        oav = oa_ref[...].astype(F32)
        ra = lax.rsqrt(jnp.mean(oav * oav, axis=-1, keepdims=True) + EPS)
        rb = lax.rsqrt(jnp.mean(ob * ob, axis=-1, keepdims=True) + EPS)
        mix_ref[:, :WIDTH] = (oav * ra * ga_ref[...]).astype(BF16)
        mix_ref[:, WIDTH:] = (ob * rb * gb_ref[...]).astype(BF16)
        x2 = x_ref[...] + jnp.dot(mix_ref[...], wo_ref[...], preferred_element_type=F32)
        x2_ref[...] = x2
        r2 = lax.rsqrt(jnp.mean(x2 * x2, axis=-1, keepdims=True) + EPS)
        h2_ref[...] = (x2 * r2 * gf_ref[...]).astype(BF16)

    row = lambda w_: pl.BlockSpec((tm, w_), lambda i: (i, 0))
    full = lambda a: pl.BlockSpec(a.shape, lambda i: (0, 0))
    return pl.pallas_call(
        body, name="merge_wo", grid=(T // tm,),
        in_specs=[row(D_MODEL), row(WIDTH)] + _view_specs(tm) * 2 + [full(ga), full(gb), full(wo), full(gf)],
        out_specs=[row(D_MODEL), row(D_MODEL), row(D_MODEL)] + _view_specs(tm) * 2,
        out_shape=[jax.ShapeDtypeStruct((T, D_MODEL), F32), jax.ShapeDtypeStruct((T, D_MODEL), BF16),
                   jax.ShapeDtypeStruct((T, D_MODEL), BF16)] + _view_shapes(T, BF16) + _view_shapes(T, F32),
        scratch_shapes=[_scr(tm)] * 4,
        compiler_params=_params(("arbitrary",)),
    )(x, oa, o1, o2, o3, l1, l2, l3, ga, gb, wo, gf)


def _ffn_up(h2, wg, wu, *, tm=1024, fc=256, rc=512):
    T = h2.shape[0]

    def body(h_ref, wg_ref, wu_ref, gate_ref, up_ref, act_ref):
        for s in range(0, tm, rc):
            h = h_ref[s:s + rc, :]
            gt = jnp.dot(h, wg_ref[...], preferred_element_type=F32)
            u = jnp.dot(h, wu_ref[...], preferred_element_type=F32)
            gate_ref[s:s + rc, :] = gt.astype(BF16)
            up_ref[s:s + rc, :] = u.astype(BF16)
            act_ref[s:s + rc, :] = (gt * (1.0 / (1.0 + jnp.exp(-gt))) * u).astype(BF16)

    rowd = pl.BlockSpec((tm, D_MODEL), lambda i, c: (i, 0))
    wcol = pl.BlockSpec((D_MODEL, fc), lambda i, c: (0, c))
    oc = pl.BlockSpec((tm, fc), lambda i, c: (i, c))
    return pl.pallas_call(
        body, name="ffn_up", grid=(T // tm, D_FF // fc),
        in_specs=[rowd, wcol, wcol],
        out_specs=[oc, oc, oc],
        out_shape=[jax.ShapeDtypeStruct((T, D_FF), BF16)] * 3,
        compiler_params=_params(("arbitrary", "arbitrary")),
    )(h2, wg, wu)


def _ffn_down_loss(act, wd, x2, tgt, g, *, tm=512, rc=256):
    T = x2.shape[0]

    def body(act_ref, wd_ref, x2_ref, tgt_ref, g_ref, dx_ref, dxb_ref, loss_ref, dg_ref):
        @pl.when(pl.program_id(0) == 0)
        def _():
            loss_ref[...] = jnp.zeros_like(loss_ref)
            dg_ref[...] = jnp.zeros_like(dg_ref)

        gv = g_ref[...]
        lsum = jnp.zeros((1, 1), F32)
        dgs = jnp.zeros((1, D_MODEL), F32)
        for c in range(0, tm, rc):
            x3 = x2_ref[c:c + rc, :] + jnp.dot(act_ref[c:c + rc, :], wd_ref[...], preferred_element_type=F32)
            r = lax.rsqrt(jnp.mean(x3 * x3, axis=-1, keepdims=True) + EPS)
            xh = x3 * r
            diff = xh * gv - tgt_ref[c:c + rc, :]
            lsum = lsum + jnp.sum(jnp.sum(diff * diff, axis=-1, keepdims=True), axis=0, keepdims=True)
            dy = diff * (1.0 / D_MODEL)
            dgs = dgs + jnp.sum(dy * xh, axis=0, keepdims=True)
            dx = _rms_bwd(dy, xh, r, gv)
            dx_ref[c:c + rc, :] = dx
            dxb_ref[c:c + rc, :] = dx.astype(BF16)
        loss_ref[...] += lsum * (0.5 / D_MODEL)
        dg_ref[...] += dgs

    rowd = pl.BlockSpec((tm, D_MODEL), lambda i: (i, 0))
    return pl.pallas_call(
        body, name="ffn_down_loss", grid=(T // tm,),
        in_specs=[pl.BlockSpec((tm, D_FF), lambda i: (i, 0)), pl.BlockSpec((D_FF, D_MODEL), lambda i: (0, 0)),
                  rowd, rowd, pl.BlockSpec(g.shape, lambda i: (0, 0))],
        out_specs=[rowd, rowd, pl.BlockSpec((1, 1), lambda i: (0, 0)), pl.BlockSpec((1, D_MODEL), lambda i: (0, 0))],
        out_shape=[jax.ShapeDtypeStruct((T, D_MODEL), F32), jax.ShapeDtypeStruct((T, D_MODEL), BF16),
                   jax.ShapeDtypeStruct((1, 1), F32), jax.ShapeDtypeStruct((1, D_MODEL), F32)],
        compiler_params=_params(("arbitrary",)),
    )(act, wd, x2, tgt, g)


def _ffn_bwd_act(dx3b, gate, up, wd, *, tm=1024, fc=256, rc=512):
    T = dx3b.shape[0]

    def body(dxb_ref, gate_ref, up_ref, wd_ref, dgate_ref, dup_ref):
        for s in range(0, tm, rc):
            dact = lax.dot_general(dxb_ref[s:s + rc, :], wd_ref[...], NT, preferred_element_type=F32)
            gt = gate_ref[s:s + rc, :].astype(F32)
            u = up_ref[s:s + rc, :].astype(F32)
            sg = 1.0 / (1.0 + jnp.exp(-gt))
            dgate_ref[s:s + rc, :] = (dact * u * sg * (1.0 + gt * (1.0 - sg))).astype(BF16)
            dup_ref[s:s + rc, :] = (dact * gt * sg).astype(BF16)

    rowd = pl.BlockSpec((tm, D_MODEL), lambda i, c: (i, 0))
    oc = pl.BlockSpec((tm, fc), lambda i, c: (i, c))
    return pl.pallas_call(
        body, name="ffn_bwd_act", grid=(T // tm, D_FF // fc),
        in_specs=[rowd, oc, oc, pl.BlockSpec((fc, D_MODEL), lambda i, c: (c, 0))],
        out_specs=[oc, oc],
        out_shape=[jax.ShapeDtypeStruct((T, D_FF), BF16), jax.ShapeDtypeStruct((T, D_FF), BF16)],
        compiler_params=_params(("arbitrary", "arbitrary")),
    )(dx3b, gate, up, wd)


def _ffn_bwd_in(dgate, dup, wg, wu, x2, dx3, g, *, tm=512, rc=256):
    T = x2.shape[0]

    def body(dgate_ref, dup_ref, wg_ref, wu_ref, x2_ref, dx_ref, g_ref, dx2_ref, dx2b_ref, dg_ref):
        @pl.when(pl.program_id(0) == 0)
        def _():
            dg_ref[...] = jnp.zeros_like(dg_ref)

        gv = g_ref[...]
        dgs = jnp.zeros((1, D_MODEL), F32)
        for s in range(0, tm, rc):
            dh = (lax.dot_general(dgate_ref[s:s + rc, :], wg_ref[...], NT, preferred_element_type=F32)
                  + lax.dot_general(dup_ref[s:s + rc, :], wu_ref[...], NT, preferred_element_type=F32))
            xv = x2_ref[s:s + rc, :]
            r = lax.rsqrt(jnp.mean(xv * xv, axis=-1, keepdims=True) + EPS)
            xh = xv * r
            dgs = dgs + jnp.sum(dh * xh, axis=0, keepdims=True)
            d = dx_ref[s:s + rc, :] + _rms_bwd(dh, xh, r, gv)
            dx2_ref[s:s + rc, :] = d
            dx2b_ref[s:s + rc, :] = d.astype(BF16)
        dg_ref[...] += dgs

    rowd = pl.BlockSpec((tm, D_MODEL), lambda i: (i, 0))
    rowf = pl.BlockSpec((tm, D_FF), lambda i: (i, 0))
    wfull = pl.BlockSpec((D_MODEL, D_FF), lambda i: (0, 0))
    return pl.pallas_call(
        body, name="ffn_bwd_in", grid=(T // tm,),
        in_specs=[rowf, rowf, wfull, wfull, rowd, rowd, pl.BlockSpec(g.shape, lambda i: (0, 0))],
        out_specs=[rowd, rowd, pl.BlockSpec((1, D_MODEL), lambda i: (0, 0))],
        out_shape=[jax.ShapeDtypeStruct((T, D_MODEL), F32), jax.ShapeDtypeStruct((T, D_MODEL), BF16),
                   jax.ShapeDtypeStruct((1, D_MODEL), F32)],
        compiler_params=_params(("arbitrary",)),
    )(dgate, dup, wg, wu, x2, dx3, g)


def _matmul_tn(a, b, *, tk, tn, tt=2048, out_dtype=BF16, name):
    T, K = a.shape
    N = b.shape[1]
    nt = T // tt

    def body(a_ref, b_ref, o_ref, acc_ref):
        part = lax.dot_general(a_ref[...], b_ref[...], TN, preferred_element_type=F32)

        @pl.when(pl.program_id(2) == 0)
        def _():
            acc_ref[...] = part

        @pl.when(pl.program_id(2) > 0)
        def _():
            acc_ref[...] += part

        @pl.when(pl.program_id(2) == nt - 1)
        def _():
            o_ref[...] = acc_ref[...].astype(out_dtype)

    return pl.pallas_call(
        body, name=name, grid=(K // tk, N // tn, nt),
        in_specs=[pl.BlockSpec((tt, tk), lambda i, j, t: (t, i)), pl.BlockSpec((tt, tn), lambda i, j, t: (t, j))],
        out_specs=pl.BlockSpec((tk, tn), lambda i, j, t: (i, j)),
        out_shape=jax.ShapeDtypeStruct((K, N), out_dtype),
        scratch_shapes=[pltpu.VMEM((tk, tn), F32)],
        compiler_params=_params(("arbitrary", "arbitrary", "arbitrary")),
    )(a, b)


def _wo_bwd(dx2b, wo, oa, ob, ga, gb, *, tm=512):
    T = dx2b.shape[0]

    def body(dx_ref, wo_ref, oa_ref, ob_ref, ga_ref, gb_ref, doa_ref, dob1_ref, dob4_ref, dob16_ref, dga_ref, dgb_ref, scr):
        @pl.when(pl.program_id(0) == 0)
        def _():
            dga_ref[...] = jnp.zeros_like(dga_ref)
            dgb_ref[...] = jnp.zeros_like(dgb_ref)

        dm = lax.dot_general(dx_ref[...], wo_ref[...], NT, preferred_element_type=F32)
        for o_ref, g_ref, dg_ref, sl in ((oa_ref, ga_ref, dga_ref, slice(0, WIDTH)),
                                         (ob_ref, gb_ref, dgb_ref, slice(WIDTH, 2 * WIDTH))):
            ov = o_ref[...].astype(F32)
            r = lax.rsqrt(jnp.mean(ov * ov, axis=-1, keepdims=True) + EPS)
            xh = ov * r
            d = dm[:, sl]
            dg_ref[...] += jnp.sum(d * xh, axis=0, keepdims=True)
            do = _rms_bwd(d, xh, r, g_ref[...])
            if o_ref is oa_ref:
                doa_ref[...] = do.astype(BF16)
            else:
                _scr_put(scr, do)
                for (_, dil), v_ref in zip(BRANCHES, (dob1_ref, dob4_ref, dob16_ref)):
                    _restride(scr, v_ref, dil, tm)

    row = lambda w_: pl.BlockSpec((tm, w_), lambda i: (i, 0))
    full = lambda a: pl.BlockSpec(a.shape, lambda i: (0, 0))
    return pl.pallas_call(
        body, name="wo_bwd", grid=(T // tm,),
        in_specs=[row(D_MODEL), full(wo), row(WIDTH), row(WIDTH), full(ga), full(gb)],
        out_specs=[row(WIDTH)] + _view_specs(tm)
        + [pl.BlockSpec((1, WIDTH), lambda i: (0, 0)), pl.BlockSpec((1, WIDTH), lambda i: (0, 0))],
        out_shape=[jax.ShapeDtypeStruct((T, WIDTH), BF16)] + _view_shapes(T, BF16)
        + [jax.ShapeDtypeStruct((1, WIDTH), F32), jax.ShapeDtypeStruct((1, WIDTH), F32)],
        scratch_shapes=[_scr(tm)],
        compiler_params=_params(("arbitrary",)),
    )(dx2b, wo, oa, ob, ga, gb)


def _dproj(dqa, dka, dva, dqs, dks, dvs, cos, sin, *, tm=512):
    T = dqa.shape[0]

    def body(dqa_ref, dka_ref, dva_ref, q1, q2, q3, k1, k2, k3, v1, v2, v3, cos_ref, sin_ref, dp_ref, db_ref, acc, tmp):
        @pl.when(pl.program_id(0) == 0)
        def _():
            db_ref[...] = jnp.zeros_like(db_ref)

        cosv = cos_ref[...]
        sinv = sin_ref[...]
        lane = lax.broadcasted_iota(jnp.int32, (tm, PAIR), 1)
        first = (lane % HEAD_DIM) < (HEAD_DIM // 2)

        def put(off, val):
            dp_ref[:, off:off + PAIR] = val.astype(BF16)
            db_ref[:, off:off + PAIR] += jnp.sum(val, axis=0, keepdims=True)

        for src, off, width in ((dqa_ref, 0, 512), (dka_ref, 512, 256)):
            for j in range(0, width, PAIR):
                d = src[:, j:j + PAIR].astype(F32)
                put(off + j, d * cosv - _rope_rot(d, first) * sinv)
        for j in range(0, 256, PAIR):
            put(768 + j, dva_ref[:, j:j + PAIR].astype(F32))
        for (a, b, c), off in (((q1, q2, q3), 1024), ((k1, k2, k3), 1536), ((v1, v2, v3), 2048)):
            _unstride(b, acc, BRANCHES[1][1], tm)
            _unstride(c, tmp, BRANCHES[2][1], tm)
            for j in range(N_CHUNK):
                put(off + j * PAIR, a[:, j * PAIR:(j + 1) * PAIR].astype(F32) + acc[j] + tmp[j])

    row = lambda w_: pl.BlockSpec((tm, w_), lambda i: (i, 0))
    return pl.pallas_call(
        body, name="dproj", grid=(T // tm,),
        in_specs=[row(512), row(256), row(256)] + _view_specs(tm) * 3 + [row(PAIR), row(PAIR)],
        out_specs=[row(D_INP), pl.BlockSpec((1, D_INP), lambda i: (0, 0))],
        out_shape=[jax.ShapeDtypeStruct((T, D_INP), BF16), jax.ShapeDtypeStruct((1, D_INP), F32)],
        scratch_shapes=[_scr(tm)] * 2,
        compiler_params=_params(("arbitrary",)),
    )(dqa, dka, dva, *dqs, *dks, *dvs, cos, sin)


def _inproj_bwd(dp, w, x, dx2, g, *, tm=512):
    T = x.shape[0]

    def body(dp_ref, w_ref, x_ref, dx2_ref, g_ref, gx_ref, dg_ref):
        @pl.when(pl.program_id(0) == 0)
        def _():
            dg_ref[...] = jnp.zeros_like(dg_ref)

        dh = lax.dot_general(dp_ref[...], w_ref[...], NT, preferred_element_type=F32)
        xv = x_ref[...]
        r = lax.rsqrt(jnp.mean(xv * xv, axis=-1, keepdims=True) + EPS)
        xh = xv * r
        dg_ref[...] += jnp.sum(dh * xh, axis=0, keepdims=True)
        gx_ref[...] = dx2_ref[...] + _rms_bwd(dh, xh, r, g_ref[...])

    row = lambda w_: pl.BlockSpec((tm, w_), lambda i: (i, 0))
    full = lambda a: pl.BlockSpec(a.shape, lambda i: (0, 0))
    return pl.pallas_call(
        body, name="inproj_bwd", grid=(T // tm,),
        in_specs=[row(D_INP), full(w), row(D_MODEL), row(D_MODEL), full(g)],
        out_specs=[row(D_MODEL), pl.BlockSpec((1, D_MODEL), lambda i: (0, 0))],
        out_shape=[jax.ShapeDtypeStruct((T, D_MODEL), F32), jax.ShapeDtypeStruct((1, D_MODEL), F32)],
        compiler_params=_params(("arbitrary",)),
    )(dp, w, x, dx2, g)


def _bias_sink_grads(dsums, bmaps, dsk):
    def body(s1, s2, s3, m1, m2, m3, dsk_ref, drel_ref, dsink_ref):
        row = lax.broadcasted_iota(jnp.int32, (N_HEADS, 128), 0)
        lane = lax.broadcasted_iota(jnp.int32, (N_HEADS, 128), 1)
        out = jnp.zeros((N_HEADS, 128), F32)
        for s_ref, m_ref in ((s1, m1), (s2, m2), (s3, m3)):
            bm = m_ref[...]
            for h in range(N_HEADS):
                a = s_ref[h]
                for b in range(REL_BUCKETS):
                    v = jnp.sum(jnp.sum(jnp.where(bm == b, a, 0.0), axis=-1, keepdims=True), axis=0, keepdims=True)
                    out = out + jnp.where((row == h) & (lane == b), v, 0.0)
        drel_ref[...] = out
        dsink_ref[...] = -jnp.sum(dsk_ref[...], axis=0, keepdims=True)

    vm = pl.BlockSpec(memory_space=pltpu.VMEM)
    return pl.pallas_call(
        body, name="bias_sink_grads",
        in_specs=[vm] * 7, out_specs=[vm, vm],
        out_shape=[jax.ShapeDtypeStruct((N_HEADS, 128), F32), jax.ShapeDtypeStruct((1, WIDTH), F32)],
        compiler_params=_params(),
    )(*dsums, *bmaps, dsk)


def _all_gather(blk, *, name):
    R, C = blk.shape

    def body(x_ref, out_ref, send_sems, recv_sems, local_sem):
        x, y, c = lax.axis_index("x"), lax.axis_index("y"), lax.axis_index("c")
        me, sibling = (x, y, c), (x, y, 1 - c)
        chips = [(1 - x, y), (x, 1 - y), (1 - x, 1 - y)]

        def slot(px, py, pc):
            return out_ref.at[4 * px + 2 * py + pc]

        def copy(k, block, to, src=None):
            return pltpu.make_async_remote_copy(
                src_ref=slot(*block) if src is None else src, dst_ref=slot(*block),
                send_sem=send_sems.at[k], recv_sem=recv_sems.at[k], device_id=to, device_id_type=MESH)

        mine = pltpu.make_async_copy(x_ref, slot(*me), local_sem)
        mine.start()
        first = [copy(0, me, sibling, src=x_ref)]
        first += [copy(1 + j, me, (*chip, c), src=x_ref) for j, chip in enumerate(chips)]
        for cp in first:
            cp.start()
        passed = [copy(4 + j, (*chip, c), sibling) for j, chip in enumerate(chips)]
        for j, chip in enumerate(chips):
            copy(1 + j, (*chip, c), me).wait_recv()
            passed[j].start()
        copy(0, sibling, me).wait_recv()
        for j, chip in enumerate(chips):
            copy(4 + j, (*chip, 1 - c), me).wait_recv()
        for cp in first + passed:
            cp.wait_send()
        mine.wait()

    return pl.pallas_call(
        body, name=name,
        in_specs=[pl.BlockSpec(memory_space=pl.ANY)], out_specs=pl.BlockSpec(memory_space=pl.ANY),
        out_shape=jax.ShapeDtypeStruct((N_DEV, R, C), blk.dtype),
        scratch_shapes=[pltpu.SemaphoreType.DMA((7,)), pltpu.SemaphoreType.DMA((7,)), pltpu.SemaphoreType.DMA],
        compiler_params=pltpu.CompilerParams(has_side_effects=True),
    )(blk)


def _scatter_blocks(parts, *, name):
    _, R, C = parts.shape

    def body(p_ref, out_ref, send_sems, recv_sems, local_sem):
        x, y, c = lax.axis_index("x"), lax.axis_index("y"), lax.axis_index("c")
        mine = 4 * x + 2 * y + c
        peers = [(x ^ (k >> 2), y ^ ((k >> 1) & 1), c ^ (k & 1)) for k in range(1, N_DEV)]

        def copy(k, peer):
            dest = 4 * peer[0] + 2 * peer[1] + peer[2]
            return pltpu.make_async_remote_copy(
                src_ref=p_ref.at[dest], dst_ref=out_ref.at[mine],
                send_sem=send_sems.at[k], recv_sem=recv_sems.at[k], device_id=peer, device_id_type=MESH)

        own = pltpu.make_async_copy(p_ref.at[mine], out_ref.at[mine], local_sem)
        own.start()
        cps = [copy(k, peer) for k, peer in enumerate(peers)]
        for cp in cps:
            cp.start()
        for k, peer in enumerate(peers):
            src = 4 * peer[0] + 2 * peer[1] + peer[2]
            pltpu.make_async_remote_copy(
                src_ref=p_ref.at[src], dst_ref=out_ref.at[src],
                send_sem=send_sems.at[k], recv_sem=recv_sems.at[k], device_id=peer, device_id_type=MESH).wait_recv()
        for cp in cps:
            cp.wait_send()
        own.wait()

    return pl.pallas_call(
        body, name=name,
        in_specs=[pl.BlockSpec(memory_space=pl.ANY)], out_specs=pl.BlockSpec(memory_space=pl.ANY),
        out_shape=jax.ShapeDtypeStruct(parts.shape, parts.dtype),
        scratch_shapes=[pltpu.SemaphoreType.DMA((7,)), pltpu.SemaphoreType.DMA((7,)), pltpu.SemaphoreType.DMA],
        compiler_params=pltpu.CompilerParams(has_side_effects=True),
    )(parts)


def _peers(x, y, c):
    return [(x ^ (k >> 2), y ^ ((k >> 1) & 1), c ^ (k & 1)) for k in range(1, N_DEV)]


_HBM = pl.BlockSpec(memory_space=pltpu.HBM)
_SEM = pl.BlockSpec(memory_space=pltpu.SEMAPHORE)
_EFFECT = pltpu.SideEffectType.DATAFLOW_SIDE_EFFECTING


def _exchange_start(src, *, gather, name):
    R, C = src.shape[-2:]

    def body(src_ref, land_ref, send_sems, recv_sems, src_thru, land_thru, token):
        x, y, c = lax.axis_index("x"), lax.axis_index("y"), lax.axis_index("c")
        mine = 4 * x + 2 * y + c
        for k, peer in enumerate(_peers(x, y, c)):
            dest = 4 * peer[0] + 2 * peer[1] + peer[2]
            pltpu.make_async_remote_copy(
                src_ref=src_ref if gather else src_ref.at[dest], dst_ref=land_ref.at[mine],
                send_sem=send_sems.at[k], recv_sem=recv_sems.at[k], device_id=peer, device_id_type=MESH).start()
        token[...] = jnp.zeros_like(token)

    return pl.pallas_call(
        body, name=name,
        out_shape=(pltpu.SemaphoreType.DMA((N_DEV - 1,)), pltpu.SemaphoreType.DMA((N_DEV - 1,)),
                   pltpu.HBM(src.shape, src.dtype), pltpu.HBM((N_DEV, R, C), src.dtype),
                   jax.ShapeDtypeStruct((8, 128), F32)),
        in_specs=(_HBM, _HBM), out_specs=(_SEM, _SEM, _HBM, _HBM, pl.BlockSpec(memory_space=pltpu.VMEM)),
        input_output_aliases={0: 2, 1: 3},
        compiler_params=pltpu.CompilerParams(has_side_effects=_EFFECT),
    )(pltpu.with_memory_space_constraint(src, pltpu.HBM),
      pltpu.with_memory_space_constraint(lax.empty((N_DEV, R, C), src.dtype), pltpu.HBM))


def _exchange_wait(send_sems, recv_sems, src_thru, land_thru, after, *, gather, name):
    def body(src_ref, land_ref, send_sems, recv_sems, after_ref, src_dead, got_ref):
        x, y, c = lax.axis_index("x"), lax.axis_index("y"), lax.axis_index("c")
        mine = 4 * x + 2 * y + c
        for k, peer in enumerate(_peers(x, y, c)):
            other = 4 * peer[0] + 2 * peer[1] + peer[2]
            copy = pltpu.make_async_remote_copy(
                src_ref=src_ref if gather else src_ref.at[other], dst_ref=land_ref.at[other],
                send_sem=send_sems.at[k], recv_sem=recv_sems.at[k], device_id=peer, device_id_type=MESH)
            copy.wait_send()
            copy.wait_recv()

    return pl.pallas_call(
        body, name=name,
        out_shape=(pltpu.HBM(src_thru.shape, src_thru.dtype), pltpu.HBM(land_thru.shape, land_thru.dtype)),
        in_specs=(_HBM, _HBM, _SEM, _SEM, pl.BlockSpec(memory_space=pl.ANY)), out_specs=(_HBM, _HBM),
        input_output_aliases={0: 0, 1: 1},
        compiler_params=pltpu.CompilerParams(has_side_effects=_EFFECT),
    )(src_thru, land_thru, send_sems, recv_sems, after)[1]


def _fill_own(got, own):
    mine = 4 * lax.axis_index("x") + 2 * lax.axis_index("y") + lax.axis_index("c")
    return lax.dynamic_update_slice(got, own[None], (mine, 0, 0))


def _adam_math(w, g, m, v):
    m = ADAM_B1 * m + (1.0 - ADAM_B1) * g
    v = ADAM_B2 * v + (1.0 - ADAM_B2) * (g * g)
    m_hat = m / (1.0 - ADAM_B1 ** ADAM_STEP)
    v_hat = v / (1.0 - ADAM_B2 ** ADAM_STEP)
    delta = -ADAM_LR * (m_hat / (jnp.sqrt(v_hat) + ADAM_EPS) + ADAM_WD * w)
    return delta, m, v


def _adamw(parts, w, m, v, *, name):
    R, C = w.shape
    tr = R // 2
    assert tr % 16 == 0

    def body(p_ref, w_ref, m_ref, v_ref, g_ref, d_ref, nm_ref, nv_ref):
        g = p_ref[0].astype(F32)
        for s in range(1, N_DEV):
            g = g + p_ref[s].astype(F32)
        d, nm, nv = _adam_math(w_ref[...], g, m_ref[...], v_ref[...])
        g_ref[...] = g
        d_ref[...] = d
        nm_ref[...] = nm
        nv_ref[...] = nv

    blk = pl.BlockSpec((tr, C), lambda i: (i, 0))
    return pl.pallas_call(
        body, name=name, grid=(R // tr,),
        in_specs=[pl.BlockSpec((N_DEV, tr, C), lambda i: (0, i, 0)), blk, blk, blk],
        out_specs=[blk] * 4, out_shape=[jax.ShapeDtypeStruct((R, C), F32)] * 4,
        compiler_params=_params(("arbitrary",)),
    )(parts, w, m, v)


def _adamw_small(parts, w, m, v):
    def body(p_ref, w_ref, m_ref, v_ref, g_ref, d_ref, nm_ref, nv_ref):
        g = p_ref[0]
        for s in range(1, N_DEV):
            g = g + p_ref[s]
        d, nm, nv = _adam_math(w_ref[...], g, m_ref[...], v_ref[...])
        g_ref[...] = g
        d_ref[...] = d
        nm_ref[...] = nm
        nv_ref[...] = nv

    vm = pl.BlockSpec(memory_space=pltpu.VMEM)
    return pl.pallas_call(
        body, name="adamw_small", in_specs=[vm] * 4, out_specs=[vm] * 4,
        out_shape=[jax.ShapeDtypeStruct((SMALL_ROWS, 128), F32)] * 4, compiler_params=_params(),
    )(parts, w, m, v)


def _t5_bucket(dist):
    max_exact = REL_BUCKETS // 2
    df = jnp.maximum(dist, 1).astype(F32)
    large = max_exact + (jnp.log(df / max_exact) / math.log(REL_MAX_DISTANCE / max_exact)
                         * (REL_BUCKETS - max_exact)).astype(jnp.int32)
    large = jnp.minimum(large, REL_BUCKETS - 1)
    return jnp.where(dist < max_exact, dist, large)


def _band_tables(rel_table, dil, n_back):
    qi = jnp.arange(BLK)[:, None]
    kj = jnp.arange(2 * BLK)[None, :]
    delta = BLK + qi - kj
    in_band = (delta >= 0) & (delta <= n_back)
    if rel_table is None:
        vals = jnp.zeros((N_HEADS, BLK, 2 * BLK), F32)
        bmap = None
    else:
        bucket = _t5_bucket(jnp.clip(delta, 0, n_back) * dil)
        vals = jnp.zeros((N_HEADS, BLK, 2 * BLK), F32)
        for b in range(REL_BUCKETS):
            vals = jnp.where((bucket == b)[None], rel_table[b][:, None, None], vals)
        bmap = jnp.where(in_band, bucket, -1).astype(jnp.int32)
    later = jnp.where(in_band[None], vals, NEG)
    first = jnp.where((in_band & (kj >= BLK))[None], vals, NEG)
    return jnp.stack([later, first]), bmap


def _rope_tables(T):
    half = HEAD_DIM // 2
    inv_freq = ROPE_THETA ** (-jnp.arange(half, dtype=F32) / half)
    ang = jnp.arange(T, dtype=F32)[:, None] * inv_freq[None, :]
    cos, sin = jnp.cos(ang), jnp.sin(ang)
    return jnp.tile(cos, (1, 4)), jnp.tile(jnp.concatenate([-sin, sin], axis=1), (1, 2))


def _dup_heads(a):
    h0, h1 = a[..., :HEAD_DIM], a[..., HEAD_DIM:]
    return jnp.concatenate([h0, h0, h1, h1], axis=-1)


def _widen_in(a):
    return jnp.concatenate([a[..., :512], _dup_heads(a[..., 512:640]), _dup_heads(a[..., 640:768]), a[..., 768:]], axis=-1)


def _fold_in(a):
    def fold(t):
        return jnp.concatenate([t[..., 0:64] + t[..., 64:128], t[..., 128:192] + t[..., 192:256]], axis=-1)
    return jnp.concatenate([a[..., :512], fold(a[..., 512:768]), fold(a[..., 768:1024]), a[..., 1024:]], axis=-1)


def _local_step(x, tgt, g_attn, win, b_in, sinks, rel_table, g_out_a, g_out_b, g_ffn, g_final, token, rest_fn, early_fn):
    T = x.shape[0]
    cos, sin = _rope_tables(T)
    cos = cos + token[0, 0]
    winp = _widen_in(win)
    binp = _widen_in(b_in)
    g_final2 = g_final.reshape(1, D_MODEL)
    sink8 = sinks.reshape(N_HEADS)

    bias_a, _ = _band_tables(None, 1, BLK - 1)
    tabs = [_band_tables(rel_table, dil, window // dil) for window, dil in BRANCHES]

    h1, qa, ka, va, *qkv_b = _norm_proj(x, g_attn, winp, binp, cos, sin)
    qbs, kbs, vbs = qkv_b[0:3], qkv_b[3:6], qkv_b[6:9]
    oa, lse_a = _attn_fwd(qa, ka, va, bias_a, sink8, dil=1, kv_pairs=2, use_sink=True, name="attn_a_fwd")
    outs = [_attn_fwd(qbs[n], kbs[n], vbs[n], tabs[n][0], sink8, dil=dil, kv_pairs=4, use_sink=False,
                      name=f"attn_b{n}_fwd") for n, (_, dil) in enumerate(BRANCHES)]
    wo, wg, wu, wd = rest_fn(outs[2][1])
    x2, mixed, h2, *ob_lse = _merge_wo(x, oa, outs[0][0], outs[1][0], outs[2][0], outs[0][1], outs[1][1], outs[2][1],
                                       g_out_a, g_out_b, wo, g_ffn)
    obs, lses = ob_lse[0:3], ob_lse[3:6]
    gate, up, act = _ffn_up(h2, wg, wu)
    dx3, dx3b, loss, dg_final = _ffn_down_loss(act, wd, x2, tgt, g_final2)

    dgate, dup = _ffn_bwd_act(dx3b, gate, up, wd)
    dx2, dx2b, dg_ffn = _ffn_bwd_in(dgate, dup, wg, wu, x2, dx3, g_ffn)
    dwd = _matmul_tn(act, dx3b, tk=1408, tn=1024, name="dw_down")
    dwg = _matmul_tn(h2, dgate, tk=1024, tn=1408, name="dw_gate")
    dwu = _matmul_tn(h2, dup, tk=1024, tn=1408, name="dw_up")
    dwo = _matmul_tn(mixed, dx2b, tk=1024, tn=1024, name="dw_o")
    early, token2 = early_fn(dict(w_o=dwo, w_gate=dwg, w_up=dwu, w_down=dwd))
    doa, *dobs, dg_out_a, dg_out_b = _wo_bwd(dx2b, wo, oa, obs[0], g_out_a + token2[0, 0], g_out_b)

    dqa, dka, dva, _, dsk = _attn_bwd(qa, ka, va, oa, doa, lse_a, bias_a, sink8, dil=1, kv_pairs=2, use_sink=True,
                                      name="attn_a_bwd")
    res = [_attn_bwd(qbs[n], kbs[n], vbs[n], obs[n], dobs[n], lses[n], tabs[n][0], sink8, dil=dil, kv_pairs=4,
                     use_sink=False, name=f"attn_b{n}_bwd") for n, (_, dil) in enumerate(BRANCHES)]
    dp, dbp = _dproj(dqa, dka, dva, [r[0] for r in res], [r[1] for r in res], [r[2] for r in res], cos, sin)
    grad_x, dg_attn = _inproj_bwd(dp, winp, x, dx2, g_attn)
    dwin = _fold_in(_matmul_tn(h1, dp, tk=1024, tn=1280, out_dtype=F32, name="dw_in"))
    drel, dsink = _bias_sink_grads([r[3] for r in res], [t[1] for t in tabs], dsk)

    small = dict(
        g_attn=dg_attn, b_in=_fold_in(dbp), sinks=dsink[:, ::HEAD_DIM], rel_table=drel[:, :REL_BUCKETS].T,
        g_out_a=dg_out_a, g_out_b=dg_out_b, g_ffn=dg_ffn, g_final=dg_final.reshape(D_MODEL))
    return loss[0, 0], grad_x, dwin, early, small


SMALL_NAMES = ("g_attn", "b_in", "sinks", "rel_table", "g_out_a", "g_out_b", "g_ffn", "g_final")


def _pack_small(vals):
    flat = jnp.concatenate([vals[n].reshape(-1).astype(F32) for n in SMALL_NAMES])
    return jnp.pad(flat, (0, SMALL_ROWS * 128 - flat.shape[0])).reshape(SMALL_ROWS, 128)


def _unpack_small(packed, like):
    flat = packed.reshape(-1)
    out, off = {}, 0
    for n in SMALL_NAMES:
        size = like[n].size
        out[n] = flat[off:off + size].reshape(like[n].shape)
        off += size
    return out


def _cols_to_rows(a):
    return a.reshape(a.shape[1], D_MODEL)


def kernel(x, g_attn, w_in, b_in, sinks, rel_table, g_out_a, g_out_b, w_o, g_ffn, w_gate, w_up, w_down, g_final, loss_target, m_g_attn, m_w_in, m_b_in, m_sinks, m_rel_table, m_g_out_a, m_g_out_b, m_w_o, m_g_ffn, m_w_gate, m_w_up, m_w_down, m_g_final, v_g_attn, v_w_in, v_b_in, v_sinks, v_rel_table, v_g_out_a, v_g_out_b, v_w_o, v_g_ffn, v_w_gate, v_w_up, v_w_down, v_g_final):
    mine = 4 * lax.axis_index("x") + 2 * lax.axis_index("y") + lax.axis_index("c")
    o1, o2, o3, o4 = 128, 480, 832, REST_ROWS

    def cols(seg, n):
        return seg.reshape(N_DEV, D_MODEL, n).transpose(1, 0, 2).reshape(D_MODEL, N_DEV * n)

    def col_parts(g, n):
        return g.astype(BF16).reshape(D_MODEL, N_DEV, n).transpose(1, 0, 2).reshape(N_DEV, n, D_MODEL)

    rest = jnp.concatenate([w_o[0].astype(BF16), _cols_to_rows(w_gate[0].astype(BF16)),
                            _cols_to_rows(w_up[0].astype(BF16)), w_down[0].astype(BF16)], axis=0)
    *rest_state, token = _exchange_start(rest, gather=True, name="gather_rest_start")
    win = cols(_all_gather(_cols_to_rows(w_in[0].astype(BF16)), name="gather_w_in"), 288)

    def rest_fn(after):
        gw = _fill_own(_exchange_wait(*rest_state, after, gather=True, name="gather_rest_wait"), rest)
        return (gw[:, :o1].reshape(D_MODEL, D_MODEL), cols(gw[:, o1:o2], 352), cols(gw[:, o2:o3], 352),
                gw[:, o3:].reshape(D_FF, D_MODEL))

    def early_fn(dws):
        parts = jnp.concatenate([dws["w_o"].reshape(N_DEV, 128, D_MODEL), col_parts(dws["w_gate"], 352),
                                 col_parts(dws["w_up"], 352), dws["w_down"].reshape(N_DEV, 352, D_MODEL)], axis=1)
        own = lax.dynamic_index_in_dim(parts, mine, 0, keepdims=False)
        *state, token2 = _exchange_start(parts, gather=False, name="scatter_rest_start")
        return (state, own), token2

    loss_part, grad_x, dwin, (early_state, early_own), small = _local_step(
        x[0], loss_target[0], g_attn, win, b_in, sinks, rel_table, g_out_a, g_out_b, g_ffn, g_final, token,
        rest_fn, early_fn)
    loss = lax.psum(loss_part, ("x", "y", "c"))

    got_in = _scatter_blocks(col_parts(dwin, 288), name="scatter_w_in").reshape(N_DEV, D_MODEL, 288)
    got = _fill_own(_exchange_wait(*early_state, got_in, gather=False, name="scatter_rest_wait"), early_own)

    big = dict(w_in=[a[None] for a in _adamw(got_in, w_in[0], m_w_in[0], v_w_in[0], name="adamw_w_in")])
    for n, (w, m, v, lo, hi, by_cols) in dict(
            w_o=(w_o, m_w_o, v_w_o, 0, o1, False), w_gate=(w_gate, m_w_gate, v_w_gate, o1, o2, True),
            w_up=(w_up, m_w_up, v_w_up, o2, o3, True), w_down=(w_down, m_w_down, v_w_down, o3, o4, False)).items():
        p = got[:, lo:hi]
        if by_cols:
            p = p.reshape(N_DEV, D_MODEL, hi - lo)
        big[n] = [a[None] for a in _adamw(p, w[0], m[0], v[0], name="adamw_" + n)]

    ws = dict(g_attn=g_attn, b_in=b_in, sinks=sinks, rel_table=rel_table, g_out_a=g_out_a, g_out_b=g_out_b,
              g_ffn=g_ffn, g_final=g_final)
    ms = dict(g_attn=m_g_attn, b_in=m_b_in, sinks=m_sinks, rel_table=m_rel_table, g_out_a=m_g_out_a,
              g_out_b=m_g_out_b, g_ffn=m_g_ffn, g_final=m_g_final)
    vs = dict(g_attn=v_g_attn, b_in=v_b_in, sinks=v_sinks, rel_table=v_rel_table, g_out_a=v_g_out_a,
              g_out_b=v_g_out_b, g_ffn=v_g_ffn, g_final=v_g_final)
    sparts = _all_gather(_pack_small(small), name="gather_small")
    sm = [_unpack_small(a, ws) for a in _adamw_small(sparts, _pack_small(ws), _pack_small(ms), _pack_small(vs))]

    order = ("g_attn", "w_in", "b_in", "sinks", "rel_table", "g_out_a", "g_out_b", "w_o", "g_ffn", "w_gate", "w_up",
             "w_down", "g_final")
    outs = [loss, grad_x[None]]
    for k in range(4):
        outs += [big[n][k] if n in big else sm[k][n] for n in order]
    return tuple(outs)
```

```python
import functools
import math

import jax
import jax.numpy as jnp
from jax import lax
from jax.experimental import pallas as pl
from jax.experimental.pallas import tpu as pltpu

F32 = jnp.float32
BF16 = jnp.bfloat16

N_DEV = 8
D_MODEL = 1024
HEAD_DIM = 64
N_HEADS = 8
PAIR = 2 * HEAD_DIM
WIDTH = N_HEADS * HEAD_DIM
D_IN = 2304
D_INP = 2560
D_FF = 2816
BLK = 128
ROPE_THETA = 150000.0
REL_BUCKETS = 32
REL_MAX_DISTANCE = 2048
EPS = 1e-5
NEG = -1e30
BRANCHES = ((128, 1), (512, 4), (2048, 16))
Q_SCALE = HEAD_DIM ** -0.5

ADAM_LR = 0.001
ADAM_B1 = 0.9
ADAM_B2 = 0.999
ADAM_EPS = 1e-08
ADAM_WD = 0.01
ADAM_STEP = 10

VMEM_LIMIT = 56 * 1024 * 1024
MESH = pl.DeviceIdType.MESH

NT = (((1,), (1,)), ((), ()))
TN = (((0,), (0,)), ((), ()))

REST_ROWS = 128 + 3 * 352
SMALL_ROWS = 56


def _params(sem=None):
    return pltpu.CompilerParams(dimension_semantics=sem, vmem_limit_bytes=VMEM_LIMIT)


def _rms_bwd(dh, xh, r, g):
    u = dh * g
    return r * (u - xh * jnp.mean(u * xh, axis=-1, keepdims=True))


def _rope_rot(t, first):
    return jnp.where(first, pltpu.roll(t, 96, 1), pltpu.roll(t, 32, 1))


N_CHUNK = WIDTH // PAIR


def _scr(tm):
    return pltpu.VMEM((N_CHUNK, tm, PAIR), F32)


def _scr_get(scr):
    return jnp.concatenate([scr[j] for j in range(N_CHUNK)], axis=1)


def _scr_put(scr, val):
    for j in range(N_CHUNK):
        scr[j] = val[:, j * PAIR:(j + 1) * PAIR]


def _unstride(view_ref, scr, dil, tm):
    n = tm // dil
    for r in range(dil):
        for j in range(N_CHUNK):
            col = r * WIDTH + j * PAIR
            scr.at[j][pl.ds(r, n, stride=dil), :] = view_ref[:, col:col + PAIR].astype(F32)


def _restride(scr, out_ref, dil, tm):
    n = tm // dil
    for r in range(dil):
        for j in range(N_CHUNK):
            col = r * WIDTH + j * PAIR
            rows = scr[j] if dil == 1 else scr.at[j][pl.ds(r, n, stride=dil), :]
            out_ref[:, col:col + PAIR] = rows.astype(out_ref.dtype)


def _view_specs(tm):
    return [pl.BlockSpec((tm // dil, dil * WIDTH), lambda i: (i, 0)) for _, dil in BRANCHES]


def _view_shapes(T, dtype):
    return [jax.ShapeDtypeStruct((T // dil, dil * WIDTH), dtype) for _, dil in BRANCHES]


def _norm_proj(x, g, w, b, cos, sin, *, tm=512):
    T = x.shape[0]

    def body(x_ref, g_ref, w_ref, b_ref, cos_ref, sin_ref, h_ref, qa_ref, ka_ref, va_ref, *rest):
        outs_b, ys = rest[:9], rest[9]
        xv = x_ref[...]
        r = lax.rsqrt(jnp.mean(xv * xv, axis=-1, keepdims=True) + EPS)
        h = (xv * r * g_ref[...]).astype(BF16)
        h_ref[...] = h
        cosv = cos_ref[...]
        sinv = sin_ref[...]
        lane = lax.broadcasted_iota(jnp.int32, (tm, PAIR), 1)
        first = (lane % HEAD_DIM) < (HEAD_DIM // 2)

        def proj(off):
            return (lax.dot_general(h, w_ref[off:off + 256, :], NT, preferred_element_type=F32)
                    + b_ref[:, off:off + 256])

        for (off, width, rot, scale), o_ref in zip(((0, 512, True, Q_SCALE), (512, 256, True, 1.0), (768, 256, False, 1.0)),
                                                   (qa_ref, ka_ref, va_ref)):
            for c in range(0, width, 256):
                y = proj(off + c)
                for j in range(0, 256, PAIR):
                    t = y[:, j:j + PAIR]
                    if rot:
                        t = t * cosv + _rope_rot(t, first) * sinv
                    if scale != 1.0:
                        t = t * scale
                    o_ref[:, c + j:c + j + PAIR] = t.astype(BF16)
        for n, (off, scale) in enumerate(((1024, Q_SCALE), (1536, 1.0), (2048, 1.0))):
            for c in range(0, WIDTH, 256):
                y = proj(off + c)
                y = y * scale if scale != 1.0 else y
                for j in range(0, 256, PAIR):
                    ys[(c + j) // PAIR] = y[:, j:j + PAIR]
            for (_, dil), o_ref in zip(BRANCHES, outs_b[3 * n:3 * n + 3]):
                _restride(ys, o_ref, dil, tm)

    row = lambda w_: pl.BlockSpec((tm, w_), lambda i: (i, 0))
    full = lambda a: pl.BlockSpec(a.shape, lambda i: (0, 0))
    return pl.pallas_call(
        body, name="norm_proj", grid=(T // tm,),
        in_specs=[row(D_MODEL), full(g), full(w), full(b), row(PAIR), row(PAIR)],
        out_specs=[row(D_MODEL), row(512), row(256), row(256)] + _view_specs(tm) * 3,
        out_shape=[jax.ShapeDtypeStruct((T, n), BF16) for n in (D_MODEL, 512, 256, 256)] + _view_shapes(T, BF16) * 3,
        scratch_shapes=[_scr(tm)],
        compiler_params=_params(("arbitrary",)),
    )(x, g, w, b, cos, sin)


def _attn_specs(dil, kvw, nb, clamp):
    qi = (lambda r, i: (jnp.minimum(i, nb - 1), r)) if clamp else (lambda r, i: (i, r))
    q_spec = pl.BlockSpec((BLK, WIDTH), qi)
    kp_spec = pl.BlockSpec((BLK, kvw), lambda r, i: (jnp.minimum(jnp.maximum(i - 1, 0), nb - 1), r))
    kc_spec = pl.BlockSpec((BLK, kvw), qi)
    b_spec = pl.BlockSpec((1, N_HEADS, BLK, 2 * BLK), lambda r, i: (jnp.where(i == 0, 1, 0), 0, 0, 0))
    return q_spec, kp_spec, kc_spec, b_spec


def _attn_fwd(q, k, v, bias, sinks, *, dil, kv_pairs, use_sink, name):
    L = q.shape[0]
    nb = L // BLK
    kvw = kv_pairs * PAIR
    rep = 4 // kv_pairs

    def body(sink_ref, q_ref, kp_ref, kc_ref, vp_ref, vc_ref, b_ref, o_ref, lse_ref):
        lo = lax.broadcasted_iota(jnp.int32, (1, PAIR), 1) < HEAD_DIM
        for hp in range(4):
            sl = slice(hp * PAIR, (hp + 1) * PAIR)
            ksl = slice((hp // rep) * PAIR, (hp // rep + 1) * PAIR)
            qp = q_ref[:, sl]
            kk = jnp.concatenate([kp_ref[:, ksl], kc_ref[:, ksl]], axis=0)
            vv = jnp.concatenate([vp_ref[:, ksl], vc_ref[:, ksl]], axis=0)
            o_pair = None
            lse_pair = None
            for e in range(2):
                h = 2 * hp + e
                msk = lo if e == 0 else jnp.logical_not(lo)
                qm = jnp.where(msk, qp, jnp.zeros_like(qp))
                s = lax.dot_general(qm, kk, NT, preferred_element_type=F32) + b_ref[0, h]
                m = jnp.max(s, axis=-1, keepdims=True)
                if use_sink:
                    sk = sink_ref[h]
                    m = jnp.maximum(m, sk)
                p = jnp.exp(s - m)
                l = jnp.sum(p, axis=-1, keepdims=True)
                if use_sink:
                    l = l + jnp.exp(sk - m)
                vm = jnp.where(msk, vv, jnp.zeros_like(vv))
                oe = jnp.dot(p.astype(BF16), vm, preferred_element_type=F32) * (1.0 / l)
                ls = m + jnp.log(l)
                if e == 0:
                    o_pair = oe
                    lse_pair = jnp.broadcast_to(ls, (BLK, PAIR))
                else:
                    o_pair = o_pair + oe
                    lse_pair = jnp.where(msk, ls, lse_pair)
            o_ref[:, sl] = o_pair.astype(BF16)
            lse_ref[:, sl] = lse_pair

    q_spec, kp_spec, kc_spec, b_spec = _attn_specs(dil, kvw, nb, False)
    return pl.pallas_call(
        body, name=name, grid=(dil, nb),
        in_specs=[pl.BlockSpec(memory_space=pltpu.SMEM), q_spec, kp_spec, kc_spec, kp_spec, kc_spec, b_spec],
        out_specs=[q_spec, q_spec],
        out_shape=[jax.ShapeDtypeStruct((L, dil * WIDTH), BF16), jax.ShapeDtypeStruct((L, dil * WIDTH), F32)],
        compiler_params=_params(("arbitrary", "arbitrary")),
    )(sinks, q, k, k, v, v, bias)


def _attn_bwd(q, k, v, o, do, lse, bias, sinks, *, dil, kv_pairs, use_sink, name):
    L = q.shape[0]
    nb = L // BLK
    kvw = kv_pairs * PAIR
    rep = 4 // kv_pairs

    def body(sink_ref, q_ref, kp_ref, kc_ref, vp_ref, vc_ref, o_ref, do_ref, lse_ref, b_ref,
             dq_ref, dk_ref, dv_ref, dsum_ref, dsk_ref, ck_ref, cv_ref):
        r = pl.program_id(0)
        i = pl.program_id(1)

        @pl.when((r == 0) & (i == 0))
        def _():
            dsum_ref[...] = jnp.zeros_like(dsum_ref)
            dsk_ref[...] = jnp.zeros_like(dsk_ref)

        @pl.when(i == 0)
        def _():
            ck_ref[...] = jnp.zeros_like(ck_ref)
            cv_ref[...] = jnp.zeros_like(cv_ref)

        @pl.when(i < nb)
        def _():
            lo = lax.broadcasted_iota(jnp.int32, (1, PAIR), 1) < HEAD_DIM
            dks = [None] * kv_pairs
            dvs = [None] * kv_pairs
            for hp in range(4):
                kvp = hp // rep
                sl = slice(hp * PAIR, (hp + 1) * PAIR)
                ksl = slice(kvp * PAIR, (kvp + 1) * PAIR)
                qp = q_ref[:, sl]
                dop = do_ref[:, sl]
                prod = dop.astype(F32) * o_ref[:, sl].astype(F32)
                kk = jnp.concatenate([kp_ref[:, ksl], kc_ref[:, ksl]], axis=0)
                vv = jnp.concatenate([vp_ref[:, ksl], vc_ref[:, ksl]], axis=0)
                dq_pair = None
                c_pair = None
                qms, doms, dsbs, pbs = [], [], [], []
                for e in range(2):
                    h = 2 * hp + e
                    msk = lo if e == 0 else jnp.logical_not(lo)
                    qm = jnp.where(msk, qp, jnp.zeros_like(qp))
                    dom = jnp.where(msk, dop, jnp.zeros_like(dop))
                    km = jnp.where(msk, kk, jnp.zeros_like(kk))
                    s = lax.dot_general(qm, kk, NT, preferred_element_type=F32) + b_ref[0, h]
                    ls = lse_ref[:, h * HEAD_DIM:h * HEAD_DIM + 1]
                    p = jnp.exp(s - ls)
                    dp = lax.dot_general(dom, vv, NT, preferred_element_type=F32)
                    delta = jnp.sum(jnp.where(msk, prod, 0.0), axis=-1, keepdims=True)
                    ds = p * (dp - delta)
                    if use_sink:
                        ce = jnp.exp(sink_ref[h] - ls) * delta
                        c_pair = jnp.broadcast_to(ce, (BLK, PAIR)) if e == 0 else jnp.where(msk, ce, c_pair)
                    else:
                        dsum_ref[h] += ds
                    dsb = ds.astype(BF16)
                    dqe = jnp.dot(dsb, km, preferred_element_type=F32)
                    dq_pair = dqe if e == 0 else dq_pair + dqe
                    qms.append(qm)
                    doms.append(dom)
                    dsbs.append(dsb)
                    pbs.append(p.astype(BF16))
                dke = lax.dot_general(jnp.concatenate(dsbs, axis=0), jnp.concatenate(qms, axis=0), TN,
                                      preferred_element_type=F32)
                dve = lax.dot_general(jnp.concatenate(pbs, axis=0), jnp.concatenate(doms, axis=0), TN,
                                      preferred_element_type=F32)
                dks[kvp] = dke if dks[kvp] is None else dks[kvp] + dke
                dvs[kvp] = dve if dvs[kvp] is None else dvs[kvp] + dve
                dq_ref[:, sl] = (dq_pair * Q_SCALE).astype(BF16)
                if use_sink:
                    dsk_ref[:, sl] += c_pair
            for kvp in range(kv_pairs):
                ksl = slice(kvp * PAIR, (kvp + 1) * PAIR)
                dk_ref[:, ksl] = (ck_ref[:, ksl] + dks[kvp][:BLK]).astype(BF16)
                dv_ref[:, ksl] = (cv_ref[:, ksl] + dvs[kvp][:BLK]).astype(BF16)
                ck_ref[:, ksl] = dks[kvp][BLK:]
                cv_ref[:, ksl] = dvs[kvp][BLK:]

        @pl.when(i == nb)
        def _():
            dk_ref[...] = ck_ref[...].astype(BF16)
            dv_ref[...] = cv_ref[...].astype(BF16)

    q_spec, kp_spec, kc_spec, b_spec = _attn_specs(dil, kvw, nb, True)
    dkv_spec = pl.BlockSpec((BLK, kvw), lambda r, i: (jnp.maximum(i - 1, 0), r))
    return pl.pallas_call(
        body, name=name, grid=(dil, nb + 1),
        in_specs=[pl.BlockSpec(memory_space=pltpu.SMEM), q_spec, kp_spec, kc_spec, kp_spec, kc_spec,
                  q_spec, q_spec, q_spec, b_spec],
        out_specs=[q_spec, dkv_spec, dkv_spec,
                   pl.BlockSpec((N_HEADS, BLK, 2 * BLK), lambda r, i: (0, 0, 0)),
                   pl.BlockSpec((BLK, WIDTH), lambda r, i: (0, 0))],
        out_shape=[jax.ShapeDtypeStruct((L, dil * WIDTH), BF16),
                   jax.ShapeDtypeStruct((L, dil * kvw), BF16),
                   jax.ShapeDtypeStruct((L, dil * kvw), BF16),
                   jax.ShapeDtypeStruct((N_HEADS, BLK, 2 * BLK), F32),
                   jax.ShapeDtypeStruct((BLK, WIDTH), F32)],
        scratch_shapes=[pltpu.VMEM((BLK, kvw), F32), pltpu.VMEM((BLK, kvw), F32)],
        compiler_params=_params(("arbitrary", "arbitrary")),
    )(sinks, q, k, k, v, v, o, do, lse, bias)


def _merge_wo(x, oa, o1, o2, o3, l1, l2, l3, ga, gb, wo, gf, *, tm=512):
    T = x.shape[0]

    def body(x_ref, oa_ref, o1_ref, o2_ref, o3_ref, l1_ref, l2_ref, l3_ref, ga_ref, gb_ref, wo_ref, gf_ref,
             x2_ref, mix_ref, h2_ref, ob1_ref, ob4_ref, ob16_ref, ls1_ref, ls4_ref, ls16_ref, so2, so3, sl2, sl3):
        _unstride(o2_ref, so2, BRANCHES[1][1], tm)
        _unstride(o3_ref, so3, BRANCHES[2][1], tm)
        _unstride(l2_ref, sl2, BRANCHES[1][1], tm)
        _unstride(l3_ref, sl3, BRANCHES[2][1], tm)
        la, lb, lc = l1_ref[...], _scr_get(sl2), _scr_get(sl3)
        m = jnp.maximum(jnp.maximum(la, lb), lc)
        ea, eb, ec = jnp.exp(la - m), jnp.exp(lb - m), jnp.exp(lc - m)
        den = ea + eb + ec
        inv = 1.0 / den
        ob = (ea * o1_ref[...].astype(F32) + eb * _scr_get(so2) + ec * _scr_get(so3)) * inv
        _scr_put(so2, ob)
        _scr_put(sl2, m + jnp.log(den))
        for (_, dil), o_ref, l_ref in zip(BRANCHES, (ob1_ref, ob4_ref, ob16_ref), (ls1_ref, ls4_ref, ls16_ref)):
            _restride(so2, o_ref, dil, tm)
            _restride(sl2, l_ref, dil, tm)
        oav = oa_ref[...].astype(F32)
        ra = lax.rsqrt(jnp.mean(oav * oav, axis=-1, keepdims=True) + EPS)
        rb = lax.rsqrt(jnp.mean(ob * ob, axis=-1, keepdims=True) + EPS)
        mix_ref[:, :WIDTH] = (oav * ra * ga_ref[...]).astype(BF16)
        mix_ref[:, WIDTH:] = (ob * rb * gb_ref[...]).astype(BF16)
        x2 = x_ref[...] + jnp.dot(mix_ref[...], wo_ref[...], preferred_element_type=F32)
        x2_ref[...] = x2
        r2 = lax.rsqrt(jnp.mean(x2 * x2, axis=-1, keepdims=True) + EPS)
        h2_ref[...] = (x2 * r2 * gf_ref[...]).astype(BF16)

    row = lambda w_: pl.BlockSpec((tm, w_), lambda i: (i, 0))
    full = lambda a: pl.BlockSpec(a.shape, lambda i: (0, 0))
    return pl.pallas_call(
        body, name="merge_wo", grid=(T // tm,),
        in_specs=[row(D_MODEL), row(WIDTH)] + _view_specs(tm) * 2 + [full(ga), full(gb), full(wo), full(gf)],
        out_specs=[row(D_MODEL), row(D_MODEL), row(D_MODEL)] + _view_specs(tm) * 2,
        out_shape=[jax.ShapeDtypeStruct((T, D_MODEL), F32), jax.ShapeDtypeStruct((T, D_MODEL), BF16),
                   jax.ShapeDtypeStruct((T, D_MODEL), BF16)] + _view_shapes(T, BF16) + _view_shapes(T, F32),
        scratch_shapes=[_scr(tm)] * 4,
        compiler_params=_params(("arbitrary",)),
    )(x, oa, o1, o2, o3, l1, l2, l3, ga, gb, wo, gf)


def _ffn_up(h2, wgt, wut, *, tm=1024, fc=256, rc=512):
    T = h2.shape[0]

    def body(h_ref, wg_ref, wu_ref, gate_ref, up_ref, act_ref):
        for s in range(0, tm, rc):
            h = h_ref[s:s + rc, :]
            gt = lax.dot_general(h, wg_ref[...], NT, preferred_element_type=F32)
            u = lax.dot_general(h, wu_ref[...], NT, preferred_element_type=F32)
            gate_ref[s:s + rc, :] = gt.astype(BF16)
            up_ref[s:s + rc, :] = u.astype(BF16)
            act_ref[s:s + rc, :] = (gt * (1.0 / (1.0 + jnp.exp(-gt))) * u).astype(BF16)

    rowd = pl.BlockSpec((tm, D_MODEL), lambda i, c: (i, 0))
    wrow = pl.BlockSpec((fc, D_MODEL), lambda i, c: (c, 0))
    oc = pl.BlockSpec((tm, fc), lambda i, c: (i, c))
    return pl.pallas_call(
        body, name="ffn_up", grid=(T // tm, D_FF // fc),
        in_specs=[rowd, wrow, wrow],
        out_specs=[oc, oc, oc],
        out_shape=[jax.ShapeDtypeStruct((T, D_FF), BF16)] * 3,
        compiler_params=_params(("arbitrary", "arbitrary")),
    )(h2, wgt, wut)


def _ffn_down_loss(act, wd, x2, tgt, g, *, tm=512, rc=256):
    T = x2.shape[0]

    def body(act_ref, wd_ref, x2_ref, tgt_ref, g_ref, dx_ref, dxb_ref, loss_ref, dg_ref):
        @pl.when(pl.program_id(0) == 0)
        def _():
            loss_ref[...] = jnp.zeros_like(loss_ref)
            dg_ref[...] = jnp.zeros_like(dg_ref)

        gv = g_ref[...]
        lsum = jnp.zeros((1, 1), F32)
        dgs = jnp.zeros((1, D_MODEL), F32)
        for c in range(0, tm, rc):
            x3 = x2_ref[c:c + rc, :] + jnp.dot(act_ref[c:c + rc, :], wd_ref[...], preferred_element_type=F32)
            r = lax.rsqrt(jnp.mean(x3 * x3, axis=-1, keepdims=True) + EPS)
            xh = x3 * r
            diff = xh * gv - tgt_ref[c:c + rc, :]
            lsum = lsum + jnp.sum(jnp.sum(diff * diff, axis=-1, keepdims=True), axis=0, keepdims=True)
            dy = diff * (1.0 / D_MODEL)
            dgs = dgs + jnp.sum(dy * xh, axis=0, keepdims=True)
            dx = _rms_bwd(dy, xh, r, gv)
            dx_ref[c:c + rc, :] = dx
            dxb_ref[c:c + rc, :] = dx.astype(BF16)
        loss_ref[...] += lsum * (0.5 / D_MODEL)
        dg_ref[...] += dgs

    rowd = pl.BlockSpec((tm, D_MODEL), lambda i: (i, 0))
    return pl.pallas_call(
        body, name="ffn_down_loss", grid=(T // tm,),
        in_specs=[pl.BlockSpec((tm, D_FF), lambda i: (i, 0)), pl.BlockSpec((D_FF, D_MODEL), lambda i: (0, 0)),
                  rowd, rowd, pl.BlockSpec(g.shape, lambda i: (0, 0))],
        out_specs=[rowd, rowd, pl.BlockSpec((1, 1), lambda i: (0, 0)), pl.BlockSpec((1, D_MODEL), lambda i: (0, 0))],
        out_shape=[jax.ShapeDtypeStruct((T, D_MODEL), F32), jax.ShapeDtypeStruct((T, D_MODEL), BF16),
                   jax.ShapeDtypeStruct((1, 1), F32), jax.ShapeDtypeStruct((1, D_MODEL), F32)],
        compiler_params=_params(("arbitrary",)),
    )(act, wd, x2, tgt, g)


def _ffn_bwd_act(dx3b, gate, up, wd, *, tm=1024, fc=256, rc=512):
    T = dx3b.shape[0]

    def body(dxb_ref, gate_ref, up_ref, wd_ref, dgate_ref, dup_ref):
        for s in range(0, tm, rc):
            dact = lax.dot_general(dxb_ref[s:s + rc, :], wd_ref[...], NT, preferred_element_type=F32)
            gt = gate_ref[s:s + rc, :].astype(F32)
            u = up_ref[s:s + rc, :].astype(F32)
            sg = 1.0 / (1.0 + jnp.exp(-gt))
            dgate_ref[s:s + rc, :] = (dact * u * sg * (1.0 + gt * (1.0 - sg))).astype(BF16)
            dup_ref[s:s + rc, :] = (dact * gt * sg).astype(BF16)

    rowd = pl.BlockSpec((tm, D_MODEL), lambda i, c: (i, 0))
    oc = pl.BlockSpec((tm, fc), lambda i, c: (i, c))
    return pl.pallas_call(
        body, name="ffn_bwd_act", grid=(T // tm, D_FF // fc),
        in_specs=[rowd, oc, oc, pl.BlockSpec((fc, D_MODEL), lambda i, c: (c, 0))],
        out_specs=[oc, oc],
        out_shape=[jax.ShapeDtypeStruct((T, D_FF), BF16), jax.ShapeDtypeStruct((T, D_FF), BF16)],
        compiler_params=_params(("arbitrary", "arbitrary")),
    )(dx3b, gate, up, wd)


def _ffn_bwd_in(dgate, dup, wgt, wut, x2, dx3, g, *, tm=512, rc=256):
    T = x2.shape[0]

    def body(dgate_ref, dup_ref, wg_ref, wu_ref, x2_ref, dx_ref, g_ref, dx2_ref, dx2b_ref, dg_ref):
        @pl.when(pl.program_id(0) == 0)
        def _():
            dg_ref[...] = jnp.zeros_like(dg_ref)

        gv = g_ref[...]
        dgs = jnp.zeros((1, D_MODEL), F32)
        for s in range(0, tm, rc):
            dh = (jnp.dot(dgate_ref[s:s + rc, :], wg_ref[...], preferred_element_type=F32)
                  + jnp.dot(dup_ref[s:s + rc, :], wu_ref[...], preferred_element_type=F32))
            xv = x2_ref[s:s + rc, :]
            r = lax.rsqrt(jnp.mean(xv * xv, axis=-1, keepdims=True) + EPS)
            xh = xv * r
            dgs = dgs + jnp.sum(dh * xh, axis=0, keepdims=True)
            d = dx_ref[s:s + rc, :] + _rms_bwd(dh, xh, r, gv)
            dx2_ref[s:s + rc, :] = d
            dx2b_ref[s:s + rc, :] = d.astype(BF16)
        dg_ref[...] += dgs

    rowd = pl.BlockSpec((tm, D_MODEL), lambda i: (i, 0))
    rowf = pl.BlockSpec((tm, D_FF), lambda i: (i, 0))
    wfull = pl.BlockSpec((D_FF, D_MODEL), lambda i: (0, 0))
    return pl.pallas_call(
        body, name="ffn_bwd_in", grid=(T // tm,),
        in_specs=[rowf, rowf, wfull, wfull, rowd, rowd, pl.BlockSpec(g.shape, lambda i: (0, 0))],
        out_specs=[rowd, rowd, pl.BlockSpec((1, D_MODEL), lambda i: (0, 0))],
        out_shape=[jax.ShapeDtypeStruct((T, D_MODEL), F32), jax.ShapeDtypeStruct((T, D_MODEL), BF16),
                   jax.ShapeDtypeStruct((1, D_MODEL), F32)],
        compiler_params=_params(("arbitrary",)),
    )(dgate, dup, wgt, wut, x2, dx3, g)


def _matmul_tn(a, b, *, tk, tn, tt=2048, out_dtype=BF16, name):
    T, K = a.shape
    N = b.shape[1]
    nt = T // tt

    def body(a_ref, b_ref, o_ref, acc_ref):
        part = lax.dot_general(a_ref[...], b_ref[...], TN, preferred_element_type=F32)

        @pl.when(pl.program_id(2) == 0)
        def _():
            acc_ref[...] = part

        @pl.when(pl.program_id(2) > 0)
        def _():
            acc_ref[...] += part

        @pl.when(pl.program_id(2) == nt - 1)
        def _():
            o_ref[...] = acc_ref[...].astype(out_dtype)

    return pl.pallas_call(
        body, name=name, grid=(K // tk, N // tn, nt),
        in_specs=[pl.BlockSpec((tt, tk), lambda i, j, t: (t, i)), pl.BlockSpec((tt, tn), lambda i, j, t: (t, j))],
        out_specs=pl.BlockSpec((tk, tn), lambda i, j, t: (i, j)),
        out_shape=jax.ShapeDtypeStruct((K, N), out_dtype),
        scratch_shapes=[pltpu.VMEM((tk, tn), F32)],
        compiler_params=_params(("arbitrary", "arbitrary", "arbitrary")),
    )(a, b)


def _wo_bwd(dx2b, wo, oa, ob, ga, gb, *, tm=512):
    T = dx2b.shape[0]

    def body(dx_ref, wo_ref, oa_ref, ob_ref, ga_ref, gb_ref, doa_ref, dob1_ref, dob4_ref, dob16_ref, dga_ref, dgb_ref, scr):
        @pl.when(pl.program_id(0) == 0)
        def _():
            dga_ref[...] = jnp.zeros_like(dga_ref)
            dgb_ref[...] = jnp.zeros_like(dgb_ref)

        dm = lax.dot_general(dx_ref[...], wo_ref[...], NT, preferred_element_type=F32)
        for o_ref, g_ref, dg_ref, sl in ((oa_ref, ga_ref, dga_ref, slice(0, WIDTH)),
                                         (ob_ref, gb_ref, dgb_ref, slice(WIDTH, 2 * WIDTH))):
            ov = o_ref[...].astype(F32)
            r = lax.rsqrt(jnp.mean(ov * ov, axis=-1, keepdims=True) + EPS)
            xh = ov * r
            d = dm[:, sl]
            dg_ref[...] += jnp.sum(d * xh, axis=0, keepdims=True)
            do = _rms_bwd(d, xh, r, g_ref[...])
            if o_ref is oa_ref:
                doa_ref[...] = do.astype(BF16)
            else:
                _scr_put(scr, do)
                for (_, dil), v_ref in zip(BRANCHES, (dob1_ref, dob4_ref, dob16_ref)):
                    _restride(scr, v_ref, dil, tm)

    row = lambda w_: pl.BlockSpec((tm, w_), lambda i: (i, 0))
    full = lambda a: pl.BlockSpec(a.shape, lambda i: (0, 0))
    return pl.pallas_call(
        body, name="wo_bwd", grid=(T // tm,),
        in_specs=[row(D_MODEL), full(wo), row(WIDTH), row(WIDTH), full(ga), full(gb)],
        out_specs=[row(WIDTH)] + _view_specs(tm)
        + [pl.BlockSpec((1, WIDTH), lambda i: (0, 0)), pl.BlockSpec((1, WIDTH), lambda i: (0, 0))],
        out_shape=[jax.ShapeDtypeStruct((T, WIDTH), BF16)] + _view_shapes(T, BF16)
        + [jax.ShapeDtypeStruct((1, WIDTH), F32), jax.ShapeDtypeStruct((1, WIDTH), F32)],
        scratch_shapes=[_scr(tm)],
        compiler_params=_params(("arbitrary",)),
    )(dx2b, wo, oa, ob, ga, gb)


def _dproj(dqa, dka, dva, dqs, dks, dvs, cos, sin, *, tm=512):
    T = dqa.shape[0]

    def body(dqa_ref, dka_ref, dva_ref, q1, q2, q3, k1, k2, k3, v1, v2, v3, cos_ref, sin_ref, dp_ref, db_ref, acc, tmp):
        @pl.when(pl.program_id(0) == 0)
        def _():
            db_ref[...] = jnp.zeros_like(db_ref)

        cosv = cos_ref[...]
        sinv = sin_ref[...]
        lane = lax.broadcasted_iota(jnp.int32, (tm, PAIR), 1)
        first = (lane % HEAD_DIM) < (HEAD_DIM // 2)

        def put(off, val):
            dp_ref[:, off:off + PAIR] = val.astype(BF16)
            db_ref[:, off:off + PAIR] += jnp.sum(val, axis=0, keepdims=True)

        for src, off, width in ((dqa_ref, 0, 512), (dka_ref, 512, 256)):
            for j in range(0, width, PAIR):
                d = src[:, j:j + PAIR].astype(F32)
                put(off + j, d * cosv - _rope_rot(d, first) * sinv)
        for j in range(0, 256, PAIR):
            put(768 + j, dva_ref[:, j:j + PAIR].astype(F32))
        for (a, b, c), off in (((q1, q2, q3), 1024), ((k1, k2, k3), 1536), ((v1, v2, v3), 2048)):
            _unstride(b, acc, BRANCHES[1][1], tm)
            _unstride(c, tmp, BRANCHES[2][1], tm)
            for j in range(N_CHUNK):
                put(off + j * PAIR, a[:, j * PAIR:(j + 1) * PAIR].astype(F32) + acc[j] + tmp[j])

    row = lambda w_: pl.BlockSpec((tm, w_), lambda i: (i, 0))
    return pl.pallas_call(
        body, name="dproj", grid=(T // tm,),
        in_specs=[row(512), row(256), row(256)] + _view_specs(tm) * 3 + [row(PAIR), row(PAIR)],
        out_specs=[row(D_INP), pl.BlockSpec((1, D_INP), lambda i: (0, 0))],
        out_shape=[jax.ShapeDtypeStruct((T, D_INP), BF16), jax.ShapeDtypeStruct((1, D_INP), F32)],
        scratch_shapes=[_scr(tm)] * 2,
        compiler_params=_params(("arbitrary",)),
    )(dqa, dka, dva, *dqs, *dks, *dvs, cos, sin)


def _inproj_bwd(dp, w, x, dx2, g, *, tm=512):
    T = x.shape[0]

    def body(dp_ref, w_ref, x_ref, dx2_ref, g_ref, gx_ref, dg_ref):
        @pl.when(pl.program_id(0) == 0)
        def _():
            dg_ref[...] = jnp.zeros_like(dg_ref)

        dh = jnp.dot(dp_ref[...], w_ref[...], preferred_element_type=F32)
        xv = x_ref[...]
        r = lax.rsqrt(jnp.mean(xv * xv, axis=-1, keepdims=True) + EPS)
        xh = xv * r
        dg_ref[...] += jnp.sum(dh * xh, axis=0, keepdims=True)
        gx_ref[...] = dx2_ref[...] + _rms_bwd(dh, xh, r, g_ref[...])

    row = lambda w_: pl.BlockSpec((tm, w_), lambda i: (i, 0))
    full = lambda a: pl.BlockSpec(a.shape, lambda i: (0, 0))
    return pl.pallas_call(
        body, name="inproj_bwd", grid=(T // tm,),
        in_specs=[row(D_INP), full(w), row(D_MODEL), row(D_MODEL), full(g)],
        out_specs=[row(D_MODEL), pl.BlockSpec((1, D_MODEL), lambda i: (0, 0))],
        out_shape=[jax.ShapeDtypeStruct((T, D_MODEL), F32), jax.ShapeDtypeStruct((1, D_MODEL), F32)],
        compiler_params=_params(("arbitrary",)),
    )(dp, w, x, dx2, g)


def _bias_sink_grads(dsums, bmaps, dsk):
    def body(s1, s2, s3, m1, m2, m3, dsk_ref, drel_ref, dsink_ref):
        row = lax.broadcasted_iota(jnp.int32, (N_HEADS, 128), 0)
        lane = lax.broadcasted_iota(jnp.int32, (N_HEADS, 128), 1)
        out = jnp.zeros((N_HEADS, 128), F32)
        for s_ref, m_ref in ((s1, m1), (s2, m2), (s3, m3)):
            bm = m_ref[...]
            for h in range(N_HEADS):
                a = s_ref[h]
                for b in range(REL_BUCKETS):
                    v = jnp.sum(jnp.sum(jnp.where(bm == b, a, 0.0), axis=-1, keepdims=True), axis=0, keepdims=True)
                    out = out + jnp.where((row == h) & (lane == b), v, 0.0)
        drel_ref[...] = out
        dsink_ref[...] = -jnp.sum(dsk_ref[...], axis=0, keepdims=True)

    vm = pl.BlockSpec(memory_space=pltpu.VMEM)
    return pl.pallas_call(
        body, name="bias_sink_grads",
        in_specs=[vm] * 7, out_specs=[vm, vm],
        out_shape=[jax.ShapeDtypeStruct((N_HEADS, 128), F32), jax.ShapeDtypeStruct((1, WIDTH), F32)],
        compiler_params=_params(),
    )(*dsums, *bmaps, dsk)


def _all_gather(blk, *, name):
    R, C = blk.shape

    def body(x_ref, out_ref, send_sems, recv_sems, local_sem):
        x, y, c = lax.axis_index("x"), lax.axis_index("y"), lax.axis_index("c")
        me, sibling = (x, y, c), (x, y, 1 - c)
        chips = [(1 - x, y), (x, 1 - y), (1 - x, 1 - y)]

        def slot(px, py, pc):
            return out_ref.at[4 * px + 2 * py + pc]

        def copy(k, block, to, src=None):
            return pltpu.make_async_remote_copy(
                src_ref=slot(*block) if src is None else src, dst_ref=slot(*block),
                send_sem=send_sems.at[k], recv_sem=recv_sems.at[k], device_id=to, device_id_type=MESH)

        mine = pltpu.make_async_copy(x_ref, slot(*me), local_sem)
        mine.start()
        first = [copy(0, me, sibling, src=x_ref)]
        first += [copy(1 + j, me, (*chip, c), src=x_ref) for j, chip in enumerate(chips)]
        for cp in first:
            cp.start()
        passed = [copy(4 + j, (*chip, c), sibling) for j, chip in enumerate(chips)]
        for j, chip in enumerate(chips):
            copy(1 + j, (*chip, c), me).wait_recv()
            passed[j].start()
        copy(0, sibling, me).wait_recv()
        for j, chip in enumerate(chips):
            copy(4 + j, (*chip, 1 - c), me).wait_recv()
        for cp in first + passed:
            cp.wait_send()
        mine.wait()

    return pl.pallas_call(
        body, name=name,
        in_specs=[pl.BlockSpec(memory_space=pl.ANY)], out_specs=pl.BlockSpec(memory_space=pl.ANY),
        out_shape=jax.ShapeDtypeStruct((N_DEV, R, C), blk.dtype),
        scratch_shapes=[pltpu.SemaphoreType.DMA((7,)), pltpu.SemaphoreType.DMA((7,)), pltpu.SemaphoreType.DMA],
        compiler_params=pltpu.CompilerParams(has_side_effects=True),
    )(blk)


def _scatter_blocks(parts, *, name):
    _, R, C = parts.shape

    def body(p_ref, out_ref, send_sems, recv_sems, local_sem):
        x, y, c = lax.axis_index("x"), lax.axis_index("y"), lax.axis_index("c")
        mine = 4 * x + 2 * y + c
        peers = [(x ^ (k >> 2), y ^ ((k >> 1) & 1), c ^ (k & 1)) for k in range(1, N_DEV)]

        def copy(k, peer):
            dest = 4 * peer[0] + 2 * peer[1] + peer[2]
            return pltpu.make_async_remote_copy(
                src_ref=p_ref.at[dest], dst_ref=out_ref.at[mine],
                send_sem=send_sems.at[k], recv_sem=recv_sems.at[k], device_id=peer, device_id_type=MESH)

        own = pltpu.make_async_copy(p_ref.at[mine], out_ref.at[mine], local_sem)
        own.start()
        cps = [copy(k, peer) for k, peer in enumerate(peers)]
        for cp in cps:
            cp.start()
        for k, peer in enumerate(peers):
            src = 4 * peer[0] + 2 * peer[1] + peer[2]
            pltpu.make_async_remote_copy(
                src_ref=p_ref.at[src], dst_ref=out_ref.at[src],
                send_sem=send_sems.at[k], recv_sem=recv_sems.at[k], device_id=peer, device_id_type=MESH).wait_recv()
        for cp in cps:
            cp.wait_send()
        own.wait()

    return pl.pallas_call(
        body, name=name,
        in_specs=[pl.BlockSpec(memory_space=pl.ANY)], out_specs=pl.BlockSpec(memory_space=pl.ANY),
        out_shape=jax.ShapeDtypeStruct(parts.shape, parts.dtype),
        scratch_shapes=[pltpu.SemaphoreType.DMA((7,)), pltpu.SemaphoreType.DMA((7,)), pltpu.SemaphoreType.DMA],
        compiler_params=pltpu.CompilerParams(has_side_effects=True),
    )(parts)


def _peers(x, y, c):
    return [(x ^ (k >> 2), y ^ ((k >> 1) & 1), c ^ (k & 1)) for k in range(1, N_DEV)]


_HBM = pl.BlockSpec(memory_space=pltpu.HBM)
_SEM = pl.BlockSpec(memory_space=pltpu.SEMAPHORE)
_EFFECT = pltpu.SideEffectType.DATAFLOW_SIDE_EFFECTING


def _exchange_start(srcs, *, gather, name):
    n = len(srcs)
    lands = [lax.empty((N_DEV,) + s.shape[-2:], s.dtype) for s in srcs]

    def body(*refs):
        src_refs, land_refs = refs[:n], refs[n:2 * n]
        send_sems, recv_sems = refs[2 * n], refs[2 * n + 1]
        token = refs[-1]
        x, y, c = lax.axis_index("x"), lax.axis_index("y"), lax.axis_index("c")
        mine = 4 * x + 2 * y + c
        for a in range(n):
            for k, peer in enumerate(_peers(x, y, c)):
                dest = 4 * peer[0] + 2 * peer[1] + peer[2]
                j = a * (N_DEV - 1) + k
                pltpu.make_async_remote_copy(
                    src_ref=src_refs[a] if gather else src_refs[a].at[dest], dst_ref=land_refs[a].at[mine],
                    send_sem=send_sems.at[j], recv_sem=recv_sems.at[j], device_id=peer, device_id_type=MESH).start()
        token[...] = jnp.zeros_like(token)

    sems = pltpu.SemaphoreType.DMA((n * (N_DEV - 1),))
    out = pl.pallas_call(
        body, name=name,
        out_shape=(sems, sems) + tuple(pltpu.HBM(a.shape, a.dtype) for a in list(srcs) + lands)
        + (jax.ShapeDtypeStruct((8, 128), F32),),
        in_specs=(_HBM,) * (2 * n), out_specs=(_SEM, _SEM) + (_HBM,) * (2 * n) + (pl.BlockSpec(memory_space=pltpu.VMEM),),
        input_output_aliases={i: 2 + i for i in range(2 * n)},
        compiler_params=pltpu.CompilerParams(has_side_effects=_EFFECT),
    )(*[pltpu.with_memory_space_constraint(a, pltpu.HBM) for a in list(srcs) + lands])
    return out[:-1], out[-1]


def _exchange_wait(state, after, *, gather, name):
    send_sems, recv_sems = state[0], state[1]
    n = (len(state) - 2) // 2
    arrays = state[2:]

    def body(*refs):
        src_refs, land_refs = refs[:n], refs[n:2 * n]
        send_sems, recv_sems = refs[2 * n], refs[2 * n + 1]
        x, y, c = lax.axis_index("x"), lax.axis_index("y"), lax.axis_index("c")
        for a in range(n):
            for k, peer in enumerate(_peers(x, y, c)):
                other = 4 * peer[0] + 2 * peer[1] + peer[2]
                j = a * (N_DEV - 1) + k
                copy = pltpu.make_async_remote_copy(
                    src_ref=src_refs[a] if gather else src_refs[a].at[other], dst_ref=land_refs[a].at[other],
                    send_sem=send_sems.at[j], recv_sem=recv_sems.at[j], device_id=peer, device_id_type=MESH)
                copy.wait_send()
                copy.wait_recv()

    out = pl.pallas_call(
        body, name=name,
        out_shape=tuple(pltpu.HBM(a.shape, a.dtype) for a in arrays),
        in_specs=(_HBM,) * (2 * n) + (_SEM, _SEM, pl.BlockSpec(memory_space=pl.ANY)), out_specs=(_HBM,) * (2 * n),
        input_output_aliases={i: i for i in range(2 * n)},
        compiler_params=pltpu.CompilerParams(has_side_effects=_EFFECT),
    )(*arrays, send_sems, recv_sems, after)
    return out[n:]


def _fill_own(got, own):
    mine = 4 * lax.axis_index("x") + 2 * lax.axis_index("y") + lax.axis_index("c")
    return lax.dynamic_update_slice(got, own[None], (mine, 0, 0))


def _adam_math(w, g, m, v):
    m = ADAM_B1 * m + (1.0 - ADAM_B1) * g
    v = ADAM_B2 * v + (1.0 - ADAM_B2) * (g * g)
    m_hat = m / (1.0 - ADAM_B1 ** ADAM_STEP)
    v_hat = v / (1.0 - ADAM_B2 ** ADAM_STEP)
    delta = -ADAM_LR * (m_hat / (jnp.sqrt(v_hat) + ADAM_EPS) + ADAM_WD * w)
    return delta, m, v


def _sum_parts(parts, *, name):
    _, R, C = parts.shape
    tr = R // 2
    assert tr % 16 == 0

    def body(p_ref, g_ref):
        g = p_ref[0].astype(F32)
        for s in range(1, N_DEV):
            g = g + p_ref[s].astype(F32)
        g_ref[...] = g

    return pl.pallas_call(
        body, name=name, grid=(R // tr,),
        in_specs=[pl.BlockSpec((N_DEV, tr, C), lambda i: (0, i, 0))], out_specs=pl.BlockSpec((tr, C), lambda i: (i, 0)),
        out_shape=jax.ShapeDtypeStruct((R, C), F32), compiler_params=_params(("arbitrary",)),
    )(parts)


def _adamw(parts, w, m, v, *, name):
    R, C = w.shape
    n_parts = parts.shape[0]
    tr = R // 2
    assert tr % 16 == 0

    def body(p_ref, w_ref, m_ref, v_ref, g_ref, d_ref, nm_ref, nv_ref):
        g = p_ref[0].astype(F32)
        for s in range(1, n_parts):
            g = g + p_ref[s].astype(F32)
        d, nm, nv = _adam_math(w_ref[...], g, m_ref[...], v_ref[...])
        g_ref[...] = g
        d_ref[...] = d
        nm_ref[...] = nm
        nv_ref[...] = nv

    blk = pl.BlockSpec((tr, C), lambda i: (i, 0))
    return pl.pallas_call(
        body, name=name, grid=(R // tr,),
        in_specs=[pl.BlockSpec((n_parts, tr, C), lambda i: (0, i, 0)), blk, blk, blk],
        out_specs=[blk] * 4, out_shape=[jax.ShapeDtypeStruct((R, C), F32)] * 4,
        compiler_params=_params(("arbitrary",)),
    )(parts, w, m, v)


def _adamw_small(parts, w, m, v):
    def body(p_ref, w_ref, m_ref, v_ref, g_ref, d_ref, nm_ref, nv_ref):
        g = p_ref[0]
        for s in range(1, N_DEV):
            g = g + p_ref[s]
        d, nm, nv = _adam_math(w_ref[...], g, m_ref[...], v_ref[...])
        g_ref[...] = g
        d_ref[...] = d
        nm_ref[...] = nm
        nv_ref[...] = nv

    vm = pl.BlockSpec(memory_space=pltpu.VMEM)
    return pl.pallas_call(
        body, name="adamw_small", in_specs=[vm] * 4, out_specs=[vm] * 4,
        out_shape=[jax.ShapeDtypeStruct((SMALL_ROWS, 128), F32)] * 4, compiler_params=_params(),
    )(parts, w, m, v)


def _t5_bucket(dist):
    max_exact = REL_BUCKETS // 2
    df = jnp.maximum(dist, 1).astype(F32)
    large = max_exact + (jnp.log(df / max_exact) / math.log(REL_MAX_DISTANCE / max_exact)
                         * (REL_BUCKETS - max_exact)).astype(jnp.int32)
    large = jnp.minimum(large, REL_BUCKETS - 1)
    return jnp.where(dist < max_exact, dist, large)


def _band_tables(rel_table, dil, n_back):
    qi = jnp.arange(BLK)[:, None]
    kj = jnp.arange(2 * BLK)[None, :]
    delta = BLK + qi - kj
    in_band = (delta >= 0) & (delta <= n_back)
    if rel_table is None:
        vals = jnp.zeros((N_HEADS, BLK, 2 * BLK), F32)
        bmap = None
    else:
        bucket = _t5_bucket(jnp.clip(delta, 0, n_back) * dil)
        vals = jnp.zeros((N_HEADS, BLK, 2 * BLK), F32)
        for b in range(REL_BUCKETS):
            vals = jnp.where((bucket == b)[None], rel_table[b][:, None, None], vals)
        bmap = jnp.where(in_band, bucket, -1).astype(jnp.int32)
    later = jnp.where(in_band[None], vals, NEG)
    first = jnp.where((in_band & (kj >= BLK))[None], vals, NEG)
    return jnp.stack([later, first]), bmap


def _rope_tables(T):
    half = HEAD_DIM // 2
    inv_freq = ROPE_THETA ** (-jnp.arange(half, dtype=F32) / half)
    ang = jnp.arange(T, dtype=F32)[:, None] * inv_freq[None, :]
    cos, sin = jnp.cos(ang), jnp.sin(ang)
    return jnp.tile(cos, (1, 4)), jnp.tile(jnp.concatenate([-sin, sin], axis=1), (1, 2))


def _widen_in(a, axis):
    sl = lambda lo, hi: lax.slice_in_dim(a, lo, hi, axis=axis)
    dup = lambda lo: [sl(lo, lo + 64), sl(lo, lo + 64), sl(lo + 64, lo + 128), sl(lo + 64, lo + 128)]
    return jnp.concatenate([sl(0, 512)] + dup(512) + dup(640) + [sl(768, D_IN)], axis=axis)


def _fold_in(a, axis):
    sl = lambda lo, hi: lax.slice_in_dim(a, lo, hi, axis=axis)
    fold = lambda lo: [sl(lo, lo + 64) + sl(lo + 64, lo + 128), sl(lo + 128, lo + 192) + sl(lo + 192, lo + 256)]
    return jnp.concatenate([sl(0, 512)] + fold(512) + fold(768) + [sl(1024, D_INP)], axis=axis)


def _local_step(x, tgt, g_attn, wint, b_in, sinks, rel_table, g_out_a, g_out_b, g_ffn, g_final, token, rest_fn, early_fn):
    T = x.shape[0]
    cos, sin = _rope_tables(T)
    cos = cos + token[0, 0]
    winp = _widen_in(wint, 0)
    binp = _widen_in(b_in, 1)
    g_final2 = g_final.reshape(1, D_MODEL)
    sink8 = sinks.reshape(N_HEADS)

    bias_a, _ = _band_tables(None, 1, BLK - 1)
    tabs = [_band_tables(rel_table, dil, window // dil) for window, dil in BRANCHES]

    h1, qa, ka, va, *qkv_b = _norm_proj(x, g_attn, winp, binp, cos, sin)
    qbs, kbs, vbs = qkv_b[0:3], qkv_b[3:6], qkv_b[6:9]
    oa, lse_a = _attn_fwd(qa, ka, va, bias_a, sink8, dil=1, kv_pairs=2, use_sink=True, name="attn_a_fwd")
    outs = [_attn_fwd(qbs[n], kbs[n], vbs[n], tabs[n][0], sink8, dil=dil, kv_pairs=4, use_sink=False,
                      name=f"attn_b{n}_fwd") for n, (_, dil) in enumerate(BRANCHES)]
    wo, wgt, wut, wd = rest_fn(outs[2][1])
    x2, mixed, h2, *ob_lse = _merge_wo(x, oa, outs[0][0], outs[1][0], outs[2][0], outs[0][1], outs[1][1], outs[2][1],
                                       g_out_a, g_out_b, wo, g_ffn)
    obs, lses = ob_lse[0:3], ob_lse[3:6]
    gate, up, act = _ffn_up(h2, wgt, wut)
    dx3, dx3b, loss, dg_final = _ffn_down_loss(act, wd, x2, tgt, g_final2)

    dgate, dup = _ffn_bwd_act(dx3b, gate, up, wd)
    dx2, dx2b, dg_ffn = _ffn_bwd_in(dgate, dup, wgt, wut, x2, dx3, g_ffn)
    dwd = _matmul_tn(act, dx3b, tk=1408, tn=1024, name="dw_down")
    dwgt = _matmul_tn(dgate, h2, tk=1408, tn=1024, name="dw_gate")
    dwut = _matmul_tn(dup, h2, tk=1408, tn=1024, name="dw_up")
    dwo = _matmul_tn(mixed, dx2b, tk=1024, tn=1024, name="dw_o")
    early, token2 = early_fn(dict(w_o=dwo, w_gate=dwgt, w_up=dwut, w_down=dwd))
    doa, *dobs, dg_out_a, dg_out_b = _wo_bwd(dx2b, wo, oa, obs[0], g_out_a + token2[0, 0], g_out_b)

    dqa, dka, dva, _, dsk = _attn_bwd(qa, ka, va, oa, doa, lse_a, bias_a, sink8, dil=1, kv_pairs=2, use_sink=True,
                                      name="attn_a_bwd")
    res = [_attn_bwd(qbs[n], kbs[n], vbs[n], obs[n], dobs[n], lses[n], tabs[n][0], sink8, dil=dil, kv_pairs=4,
                     use_sink=False, name=f"attn_b{n}_bwd") for n, (_, dil) in enumerate(BRANCHES)]
    dp, dbp = _dproj(dqa, dka, dva, [r[0] for r in res], [r[1] for r in res], [r[2] for r in res], cos, sin)
    grad_x, dg_attn = _inproj_bwd(dp, winp, x, dx2, g_attn)
    dwin = _fold_in(_matmul_tn(dp, h1, tk=1280, tn=1024, out_dtype=F32, name="dw_in"), 0)
    drel, dsink = _bias_sink_grads([r[3] for r in res], [t[1] for t in tabs], dsk)

    small = dict(
        g_attn=dg_attn, b_in=_fold_in(dbp, 1), sinks=dsink[:, ::HEAD_DIM], rel_table=drel[:, :REL_BUCKETS].T,
        g_out_a=dg_out_a, g_out_b=dg_out_b, g_ffn=dg_ffn, g_final=dg_final.reshape(D_MODEL))
    return loss[0, 0], grad_x, dwin, early, small


SMALL_NAMES = ("g_attn", "b_in", "sinks", "rel_table", "g_out_a", "g_out_b", "g_ffn", "g_final")


def _pack_small(vals):
    flat = jnp.concatenate([vals[n].reshape(-1).astype(F32) for n in SMALL_NAMES])
    return jnp.pad(flat, (0, SMALL_ROWS * 128 - flat.shape[0])).reshape(SMALL_ROWS, 128)


def _unpack_small(packed, like):
    flat = packed.reshape(-1)
    out, off = {}, 0
    for n in SMALL_NAMES:
        size = like[n].size
        out[n] = flat[off:off + size].reshape(like[n].shape)
        off += size
    return out


def _cols_to_rows(a):
    return a.reshape(a.shape[1], D_MODEL)


def kernel(x, g_attn, w_in, b_in, sinks, rel_table, g_out_a, g_out_b, w_o, g_ffn, w_gate, w_up, w_down, g_final, loss_target, m_g_attn, m_w_in, m_b_in, m_sinks, m_rel_table, m_g_out_a, m_g_out_b, m_w_o, m_g_ffn, m_w_gate, m_w_up, m_w_down, m_g_final, v_g_attn, v_w_in, v_b_in, v_sinks, v_rel_table, v_g_out_a, v_g_out_b, v_w_o, v_g_ffn, v_w_gate, v_w_up, v_w_down, v_g_final):
    mine = 4 * lax.axis_index("x") + 2 * lax.axis_index("y") + lax.axis_index("c")
    rest_names = ("w_o", "w_gate", "w_up", "w_down")

    rest = [w_o[0].astype(BF16), w_gate[0].astype(BF16).T, w_up[0].astype(BF16).T, w_down[0].astype(BF16)]
    rest_state, token = _exchange_start(rest, gather=True, name="gather_rest_start")
    wint = _all_gather(w_in[0].astype(BF16).T, name="gather_w_in").reshape(D_IN, D_MODEL)

    def rest_fn(after):
        got = _exchange_wait(rest_state, after, gather=True, name="gather_rest_wait")
        return [_fill_own(g, own).reshape(N_DEV * own.shape[0], D_MODEL) for g, own in zip(got, rest)]

    def early_fn(dws):
        parts = [dws[n].reshape(N_DEV, -1, D_MODEL) for n in rest_names]
        own = [lax.dynamic_index_in_dim(p, mine, 0, keepdims=False) for p in parts]
        state, token2 = _exchange_start(parts, gather=False, name="scatter_rest_start")
        return (state, own), token2

    loss_part, grad_x, dwint, (early_state, early_own), small = _local_step(
        x[0], loss_target[0], g_attn, wint, b_in, sinks, rel_table, g_out_a, g_out_b, g_ffn, g_final, token,
        rest_fn, early_fn)
    loss = lax.psum(loss_part, ("x", "y", "c"))

    got_in = _scatter_blocks(dwint.astype(BF16).reshape(N_DEV, D_IN // N_DEV, D_MODEL), name="scatter_w_in")
    got = [_fill_own(g, own) for g, own in
           zip(_exchange_wait(early_state, got_in, gather=False, name="scatter_rest_wait"), early_own)]

    def update(n, parts, w, m, v, transposed):
        if transposed:
            parts = _sum_parts(parts, name="sum_" + n).T[None]
        return [a[None] for a in _adamw(parts, w[0], m[0], v[0], name="adamw_" + n)]

    big = dict(w_in=update("w_in", got_in, w_in, m_w_in, v_w_in, True),
               w_o=update("w_o", got[0], w_o, m_w_o, v_w_o, False),
               w_gate=update("w_gate", got[1], w_gate, m_w_gate, v_w_gate, True),
               w_up=update("w_up", got[2], w_up, m_w_up, v_w_up, True),
               w_down=update("w_down", got[3], w_down, m_w_down, v_w_down, False))

    ws = dict(g_attn=g_attn, b_in=b_in, sinks=sinks, rel_table=rel_table, g_out_a=g_out_a, g_out_b=g_out_b,
              g_ffn=g_ffn, g_final=g_final)
    ms = dict(g_attn=m_g_attn, b_in=m_b_in, sinks=m_sinks, rel_table=m_rel_table, g_out_a=m_g_out_a,
              g_out_b=m_g_out_b, g_ffn=m_g_ffn, g_final=m_g_final)
    vs = dict(g_attn=v_g_attn, b_in=v_b_in, sinks=v_sinks, rel_table=v_rel_table, g_out_a=v_g_out_a,
              g_out_b=v_g_out_b, g_ffn=v_g_ffn, g_final=v_g_final)
    sparts = _all_gather(_pack_small(small), name="gather_small")
    sm = [_unpack_small(a, ws) for a in _adamw_small(sparts, _pack_small(ws), _pack_small(ms), _pack_small(vs))]

    order = ("g_attn", "w_in", "b_in", "sinks", "rel_table", "g_out_a", "g_out_b", "w_o", "g_ffn", "w_gate", "w_up",
             "w_down", "g_final")
    outs = [loss, grad_x[None]]
    for k in range(4):
        outs += [big[n][k] if n in big else sm[k][n] for n in order]
    return tuple(outs)
```

```python
import functools
import math

import jax
import jax.numpy as jnp
from jax import lax
from jax.experimental import pallas as pl
from jax.experimental.pallas import tpu as pltpu

F32 = jnp.float32
BF16 = jnp.bfloat16

N_DEV = 8
D_MODEL = 1024
HEAD_DIM = 64
N_HEADS = 8
PAIR = 2 * HEAD_DIM
WIDTH = N_HEADS * HEAD_DIM
D_IN = 2304
D_INP = 2560
D_FF = 2816
BLK = 128
ROPE_THETA = 150000.0
REL_BUCKETS = 32
REL_MAX_DISTANCE = 2048
EPS = 1e-5
NEG = -1e30
BRANCHES = ((128, 1), (512, 4), (2048, 16))
Q_SCALE = HEAD_DIM ** -0.5

ADAM_LR = 0.001
ADAM_B1 = 0.9
ADAM_B2 = 0.999
ADAM_EPS = 1e-08
ADAM_WD = 0.01
ADAM_STEP = 10

VMEM_LIMIT = 56 * 1024 * 1024
MESH = pl.DeviceIdType.MESH

NT = (((1,), (1,)), ((), ()))
TN = (((0,), (0,)), ((), ()))

SMALL_ROWS = 56


def _params(sem=None):
    return pltpu.CompilerParams(dimension_semantics=sem, vmem_limit_bytes=VMEM_LIMIT)


def _rms_bwd(dh, xh, r, g):
    u = dh * g
    return r * (u - xh * jnp.mean(u * xh, axis=-1, keepdims=True))


def _rope_rot(t, first):
    return jnp.where(first, pltpu.roll(t, 96, 1), pltpu.roll(t, 32, 1))


N_CHUNK = WIDTH // PAIR


def _scr(tm):
    return pltpu.VMEM((N_CHUNK, tm, PAIR), F32)


def _scr_get(scr):
    return jnp.concatenate([scr[j] for j in range(N_CHUNK)], axis=1)


def _scr_put(scr, val):
    for j in range(N_CHUNK):
        scr[j] = val[:, j * PAIR:(j + 1) * PAIR]


def _unstride(view_ref, scr, dil, tm):
    n = tm // dil
    for r in range(dil):
        for j in range(N_CHUNK):
            col = r * WIDTH + j * PAIR
            scr.at[j][pl.ds(r, n, stride=dil), :] = view_ref[:, col:col + PAIR].astype(F32)


def _restride(scr, out_ref, dil, tm):
    n = tm // dil
    for r in range(dil):
        for j in range(N_CHUNK):
            col = r * WIDTH + j * PAIR
            rows = scr[j] if dil == 1 else scr.at[j][pl.ds(r, n, stride=dil), :]
            out_ref[:, col:col + PAIR] = rows.astype(out_ref.dtype)


def _view_specs(tm):
    return [pl.BlockSpec((tm // dil, dil * WIDTH), lambda i: (i, 0)) for _, dil in BRANCHES]


def _view_shapes(T, dtype):
    return [jax.ShapeDtypeStruct((T // dil, dil * WIDTH), dtype) for _, dil in BRANCHES]


def _norm_proj(x, g, w, b, cos, sin, *, tm=512):
    T = x.shape[0]

    def body(x_ref, g_ref, w_ref, b_ref, cos_ref, sin_ref, h_ref, qa_ref, ka_ref, va_ref, *rest):
        outs_b, ys = rest[:9], rest[9]
        xv = x_ref[...]
        r = lax.rsqrt(jnp.mean(xv * xv, axis=-1, keepdims=True) + EPS)
        h = (xv * r * g_ref[...]).astype(BF16)
        h_ref[...] = h
        cosv = cos_ref[...]
        sinv = sin_ref[...]
        lane = lax.broadcasted_iota(jnp.int32, (tm, PAIR), 1)
        first = (lane % HEAD_DIM) < (HEAD_DIM // 2)

        def proj(off):
            return (lax.dot_general(h, w_ref[off:off + 256, :], NT, preferred_element_type=F32)
                    + b_ref[:, off:off + 256])

        for (off, width, rot, scale), o_ref in zip(((0, 512, True, Q_SCALE), (512, 256, True, 1.0), (768, 256, False, 1.0)),
                                                   (qa_ref, ka_ref, va_ref)):
            for c in range(0, width, 256):
                y = proj(off + c)
                for j in range(0, 256, PAIR):
                    t = y[:, j:j + PAIR]
                    if rot:
                        t = t * cosv + _rope_rot(t, first) * sinv
                    if scale != 1.0:
                        t = t * scale
                    o_ref[:, c + j:c + j + PAIR] = t.astype(BF16)
        for n, (off, scale) in enumerate(((1024, Q_SCALE), (1536, 1.0), (2048, 1.0))):
            for c in range(0, WIDTH, 256):
                y = proj(off + c)
                y = y * scale if scale != 1.0 else y
                for j in range(0, 256, PAIR):
                    ys[(c + j) // PAIR] = y[:, j:j + PAIR]
            for (_, dil), o_ref in zip(BRANCHES, outs_b[3 * n:3 * n + 3]):
                _restride(ys, o_ref, dil, tm)

    row = lambda w_: pl.BlockSpec((tm, w_), lambda i: (i, 0))
    full = lambda a: pl.BlockSpec(a.shape, lambda i: (0, 0))
    return pl.pallas_call(
        body, name="norm_proj", grid=(T // tm,),
        in_specs=[row(D_MODEL), full(g), full(w), full(b), row(PAIR), row(PAIR)],
        out_specs=[row(D_MODEL), row(512), row(256), row(256)] + _view_specs(tm) * 3,
        out_shape=[jax.ShapeDtypeStruct((T, n), BF16) for n in (D_MODEL, 512, 256, 256)] + _view_shapes(T, BF16) * 3,
        scratch_shapes=[_scr(tm)],
        compiler_params=_params(("arbitrary",)),
    )(x, g, w, b, cos, sin)


def _attn_specs(dil, kvw, nb, clamp):
    qi = (lambda r, i: (jnp.minimum(i, nb - 1), r)) if clamp else (lambda r, i: (i, r))
    q_spec = pl.BlockSpec((BLK, WIDTH), qi)
    kp_spec = pl.BlockSpec((BLK, kvw), lambda r, i: (jnp.minimum(jnp.maximum(i - 1, 0), nb - 1), r))
    kc_spec = pl.BlockSpec((BLK, kvw), qi)
    b_spec = pl.BlockSpec((1, N_HEADS, BLK, 2 * BLK), lambda r, i: (jnp.where(i == 0, 1, 0), 0, 0, 0))
    return q_spec, kp_spec, kc_spec, b_spec


def _attn_fwd(q, k, v, bias, sinks, *, dil, kv_pairs, use_sink, name):
    L = q.shape[0]
    nb = L // BLK
    kvw = kv_pairs * PAIR
    rep = 4 // kv_pairs

    def body(sink_ref, q_ref, kp_ref, kc_ref, vp_ref, vc_ref, b_ref, o_ref, lse_ref):
        lo = lax.broadcasted_iota(jnp.int32, (1, PAIR), 1) < HEAD_DIM
        for hp in range(4):
            sl = slice(hp * PAIR, (hp + 1) * PAIR)
            ksl = slice((hp // rep) * PAIR, (hp // rep + 1) * PAIR)
            qp = q_ref[:, sl]
            kk = jnp.concatenate([kp_ref[:, ksl], kc_ref[:, ksl]], axis=0)
            vv = jnp.concatenate([vp_ref[:, ksl], vc_ref[:, ksl]], axis=0)
            o_pair = None
            lse_pair = None
            for e in range(2):
                h = 2 * hp + e
                msk = lo if e == 0 else jnp.logical_not(lo)
                qm = jnp.where(msk, qp, jnp.zeros_like(qp))
                s = lax.dot_general(qm, kk, NT, preferred_element_type=F32) + b_ref[0, h]
                m = jnp.max(s, axis=-1, keepdims=True)
                if use_sink:
                    sk = sink_ref[h]
                    m = jnp.maximum(m, sk)
                p = jnp.exp(s - m)
                l = jnp.sum(p, axis=-1, keepdims=True)
                if use_sink:
                    l = l + jnp.exp(sk - m)
                vm = jnp.where(msk, vv, jnp.zeros_like(vv))
                oe = jnp.dot(p.astype(BF16), vm, preferred_element_type=F32) * (1.0 / l)
                ls = m + jnp.log(l)
                if e == 0:
                    o_pair = oe
                    lse_pair = jnp.broadcast_to(ls, (BLK, PAIR))
                else:
                    o_pair = o_pair + oe
                    lse_pair = jnp.where(msk, ls, lse_pair)
            o_ref[:, sl] = o_pair.astype(BF16)
            lse_ref[:, sl] = lse_pair

    q_spec, kp_spec, kc_spec, b_spec = _attn_specs(dil, kvw, nb, False)
    return pl.pallas_call(
        body, name=name, grid=(dil, nb),
        in_specs=[pl.BlockSpec(memory_space=pltpu.SMEM), q_spec, kp_spec, kc_spec, kp_spec, kc_spec, b_spec],
        out_specs=[q_spec, q_spec],
        out_shape=[jax.ShapeDtypeStruct((L, dil * WIDTH), BF16), jax.ShapeDtypeStruct((L, dil * WIDTH), F32)],
        compiler_params=_params(("arbitrary", "arbitrary")),
    )(sinks, q, k, k, v, v, bias)


def _attn_bwd(q, k, v, o, do, lse, bias, sinks, *, dil, kv_pairs, use_sink, name):
    L = q.shape[0]
    nb = L // BLK
    kvw = kv_pairs * PAIR
    rep = 4 // kv_pairs

    def body(sink_ref, q_ref, kp_ref, kc_ref, vp_ref, vc_ref, o_ref, do_ref, lse_ref, b_ref,
             dq_ref, dk_ref, dv_ref, dsum_ref, dsk_ref, ck_ref, cv_ref):
        r = pl.program_id(0)
        i = pl.program_id(1)

        @pl.when((r == 0) & (i == 0))
        def _():
            dsum_ref[...] = jnp.zeros_like(dsum_ref)
            dsk_ref[...] = jnp.zeros_like(dsk_ref)

        @pl.when(i == 0)
        def _():
            ck_ref[...] = jnp.zeros_like(ck_ref)
            cv_ref[...] = jnp.zeros_like(cv_ref)

        @pl.when(i < nb)
        def _():
            lo = lax.broadcasted_iota(jnp.int32, (1, PAIR), 1) < HEAD_DIM
            dks = [None] * kv_pairs
            dvs = [None] * kv_pairs
            for hp in range(4):
                kvp = hp // rep
                sl = slice(hp * PAIR, (hp + 1) * PAIR)
                ksl = slice(kvp * PAIR, (kvp + 1) * PAIR)
                qp = q_ref[:, sl]
                dop = do_ref[:, sl]
                prod = dop.astype(F32) * o_ref[:, sl].astype(F32)
                kk = jnp.concatenate([kp_ref[:, ksl], kc_ref[:, ksl]], axis=0)
                vv = jnp.concatenate([vp_ref[:, ksl], vc_ref[:, ksl]], axis=0)
                dq_pair = None
                c_pair = None
                qms, doms, dsbs, pbs = [], [], [], []
                for e in range(2):
                    h = 2 * hp + e
                    msk = lo if e == 0 else jnp.logical_not(lo)
                    qm = jnp.where(msk, qp, jnp.zeros_like(qp))
                    dom = jnp.where(msk, dop, jnp.zeros_like(dop))
                    km = jnp.where(msk, kk, jnp.zeros_like(kk))
                    s = lax.dot_general(qm, kk, NT, preferred_element_type=F32) + b_ref[0, h]
                    ls = lse_ref[:, h * HEAD_DIM:h * HEAD_DIM + 1]
                    p = jnp.exp(s - ls)
                    dp = lax.dot_general(dom, vv, NT, preferred_element_type=F32)
                    delta = jnp.sum(jnp.where(msk, prod, 0.0), axis=-1, keepdims=True)
                    ds = p * (dp - delta)
                    if use_sink:
                        ce = jnp.exp(sink_ref[h] - ls) * delta
                        c_pair = jnp.broadcast_to(ce, (BLK, PAIR)) if e == 0 else jnp.where(msk, ce, c_pair)
                    else:
                        dsum_ref[h] += ds
                    dsb = ds.astype(BF16)
                    dqe = jnp.dot(dsb, km, preferred_element_type=F32)
                    dq_pair = dqe if e == 0 else dq_pair + dqe
                    qms.append(qm)
                    doms.append(dom)
                    dsbs.append(dsb)
                    pbs.append(p.astype(BF16))
                dke = lax.dot_general(jnp.concatenate(dsbs, axis=0), jnp.concatenate(qms, axis=0), TN,
                                      preferred_element_type=F32)
                dve = lax.dot_general(jnp.concatenate(pbs, axis=0), jnp.concatenate(doms, axis=0), TN,
                                      preferred_element_type=F32)
                dks[kvp] = dke if dks[kvp] is None else dks[kvp] + dke
                dvs[kvp] = dve if dvs[kvp] is None else dvs[kvp] + dve
                dq_ref[:, sl] = (dq_pair * Q_SCALE).astype(BF16)
                if use_sink:
                    dsk_ref[:, sl] += c_pair
            for kvp in range(kv_pairs):
                ksl = slice(kvp * PAIR, (kvp + 1) * PAIR)
                dk_ref[:, ksl] = (ck_ref[:, ksl] + dks[kvp][:BLK]).astype(BF16)
                dv_ref[:, ksl] = (cv_ref[:, ksl] + dvs[kvp][:BLK]).astype(BF16)
                ck_ref[:, ksl] = dks[kvp][BLK:]
                cv_ref[:, ksl] = dvs[kvp][BLK:]

        @pl.when(i == nb)
        def _():
            dk_ref[...] = ck_ref[...].astype(BF16)
            dv_ref[...] = cv_ref[...].astype(BF16)

    q_spec, kp_spec, kc_spec, b_spec = _attn_specs(dil, kvw, nb, True)
    dkv_spec = pl.BlockSpec((BLK, kvw), lambda r, i: (jnp.maximum(i - 1, 0), r))
    return pl.pallas_call(
        body, name=name, grid=(dil, nb + 1),
        in_specs=[pl.BlockSpec(memory_space=pltpu.SMEM), q_spec, kp_spec, kc_spec, kp_spec, kc_spec,
                  q_spec, q_spec, q_spec, b_spec],
        out_specs=[q_spec, dkv_spec, dkv_spec,
                   pl.BlockSpec((N_HEADS, BLK, 2 * BLK), lambda r, i: (0, 0, 0)),
                   pl.BlockSpec((BLK, WIDTH), lambda r, i: (0, 0))],
        out_shape=[jax.ShapeDtypeStruct((L, dil * WIDTH), BF16),
                   jax.ShapeDtypeStruct((L, dil * kvw), BF16),
                   jax.ShapeDtypeStruct((L, dil * kvw), BF16),
                   jax.ShapeDtypeStruct((N_HEADS, BLK, 2 * BLK), F32),
                   jax.ShapeDtypeStruct((BLK, WIDTH), F32)],
        scratch_shapes=[pltpu.VMEM((BLK, kvw), F32), pltpu.VMEM((BLK, kvw), F32)],
        compiler_params=_params(("arbitrary", "arbitrary")),
    )(sinks, q, k, k, v, v, o, do, lse, bias)


def _merge_wo(x, oa, o1, o2, o3, l1, l2, l3, ga, gb, wo, gf, *, tm=512):
    T = x.shape[0]

    def body(x_ref, oa_ref, o1_ref, o2_ref, o3_ref, l1_ref, l2_ref, l3_ref, ga_ref, gb_ref, wo_ref, gf_ref,
             x2_ref, mix_ref, h2_ref, ob1_ref, ob4_ref, ob16_ref, ls1_ref, ls4_ref, ls16_ref, so2, so3, sl2, sl3):
        _unstride(o2_ref, so2, BRANCHES[1][1], tm)
        _unstride(o3_ref, so3, BRANCHES[2][1], tm)
        _unstride(l2_ref, sl2, BRANCHES[1][1], tm)
        _unstride(l3_ref, sl3, BRANCHES[2][1], tm)
        la, lb, lc = l1_ref[...], _scr_get(sl2), _scr_get(sl3)
        m = jnp.maximum(jnp.maximum(la, lb), lc)
        ea, eb, ec = jnp.exp(la - m), jnp.exp(lb - m), jnp.exp(lc - m)
        den = ea + eb + ec
        inv = 1.0 / den
        ob = (ea * o1_ref[...].astype(F32) + eb * _scr_get(so2) + ec * _scr_get(so3)) * inv
        _scr_put(so2, ob)
        _scr_put(sl2, m + jnp.log(den))
        for (_, dil), o_ref, l_ref in zip(BRANCHES, (ob1_ref, ob4_ref, ob16_ref), (ls1_ref, ls4_ref, ls16_ref)):
            _restride(so2, o_ref, dil, tm)
            _restride(sl2, l_ref, dil, tm)
        oav = oa_ref[...].astype(F32)
        ra = lax.rsqrt(jnp.mean(oav * oav, axis=-1, keepdims=True) + EPS)
        rb = lax.rsqrt(jnp.mean(ob * ob, axis=-1, keepdims=True) + EPS)
        mix_ref[:, :WIDTH] = (oav * ra * ga_ref[...]).astype(BF16)
        mix_ref[:, WIDTH:] = (ob * rb * gb_ref[...]).astype(BF16)
        x2 = x_ref[...] + jnp.dot(mix_ref[...], wo_ref[...], preferred_element_type=F32)
        x2_ref[...] = x2
        r2 = lax.rsqrt(jnp.mean(x2 * x2, axis=-1, keepdims=True) + EPS)
        h2_ref[...] = (x2 * r2 * gf_ref[...]).astype(BF16)

    row = lambda w_: pl.BlockSpec((tm, w_), lambda i: (i, 0))
    full = lambda a: pl.BlockSpec(a.shape, lambda i: (0, 0))
    return pl.pallas_call(
        body, name="merge_wo", grid=(T // tm,),
        in_specs=[row(D_MODEL), row(WIDTH)] + _view_specs(tm) * 2 + [full(ga), full(gb), full(wo), full(gf)],
        out_specs=[row(D_MODEL), row(D_MODEL), row(D_MODEL)] + _view_specs(tm) * 2,
        out_shape=[jax.ShapeDtypeStruct((T, D_MODEL), F32), jax.ShapeDtypeStruct((T, D_MODEL), BF16),
                   jax.ShapeDtypeStruct((T, D_MODEL), BF16)] + _view_shapes(T, BF16) + _view_shapes(T, F32),
        scratch_shapes=[_scr(tm)] * 4,
        compiler_params=_params(("arbitrary",)),
    )(x, oa, o1, o2, o3, l1, l2, l3, ga, gb, wo, gf)


def _ffn_up(h2, wgt, wut, *, tm=1024, fc=1408, rc=256):
    T = h2.shape[0]

    def body(h_ref, wg_ref, wu_ref, gate_ref, up_ref, act_ref):
        for s in range(0, tm, rc):
            h = h_ref[s:s + rc, :]
            gt = lax.dot_general(h, wg_ref[...], NT, preferred_element_type=F32)
            u = lax.dot_general(h, wu_ref[...], NT, preferred_element_type=F32)
            gate_ref[s:s + rc, :] = gt.astype(BF16)
            up_ref[s:s + rc, :] = u.astype(BF16)
            act_ref[s:s + rc, :] = (gt * (1.0 / (1.0 + jnp.exp(-gt))) * u).astype(BF16)

    rowd = pl.BlockSpec((tm, D_MODEL), lambda i, c: (i, 0))
    wrow = pl.BlockSpec((fc, D_MODEL), lambda i, c: (c, 0))
    oc = pl.BlockSpec((tm, fc), lambda i, c: (i, c))
    return pl.pallas_call(
        body, name="ffn_up", grid=(T // tm, D_FF // fc),
        in_specs=[rowd, wrow, wrow],
        out_specs=[oc, oc, oc],
        out_shape=[jax.ShapeDtypeStruct((T, D_FF), BF16)] * 3,
        compiler_params=_params(("arbitrary", "arbitrary")),
    )(h2, wgt, wut)


def _ffn_down_loss(act, wd, x2, tgt, g, *, tm=512, rc=256):
    T = x2.shape[0]

    def body(act_ref, wd_ref, x2_ref, tgt_ref, g_ref, dx_ref, dxb_ref, loss_ref, dg_ref):
        @pl.when(pl.program_id(0) == 0)
        def _():
            loss_ref[...] = jnp.zeros_like(loss_ref)
            dg_ref[...] = jnp.zeros_like(dg_ref)

        gv = g_ref[...]
        lsum = jnp.zeros((1, 1), F32)
        dgs = jnp.zeros((1, D_MODEL), F32)
        for c in range(0, tm, rc):
            x3 = x2_ref[c:c + rc, :] + jnp.dot(act_ref[c:c + rc, :], wd_ref[...], preferred_element_type=F32)
            r = lax.rsqrt(jnp.mean(x3 * x3, axis=-1, keepdims=True) + EPS)
            xh = x3 * r
            diff = xh * gv - tgt_ref[c:c + rc, :]
            lsum = lsum + jnp.sum(jnp.sum(diff * diff, axis=-1, keepdims=True), axis=0, keepdims=True)
            dy = diff * (1.0 / D_MODEL)
            dgs = dgs + jnp.sum(dy * xh, axis=0, keepdims=True)
            dx = _rms_bwd(dy, xh, r, gv)
            dx_ref[c:c + rc, :] = dx
            dxb_ref[c:c + rc, :] = dx.astype(BF16)
        loss_ref[...] += lsum * (0.5 / D_MODEL)
        dg_ref[...] += dgs

    rowd = pl.BlockSpec((tm, D_MODEL), lambda i: (i, 0))
    return pl.pallas_call(
        body, name="ffn_down_loss", grid=(T // tm,),
        in_specs=[pl.BlockSpec((tm, D_FF), lambda i: (i, 0)), pl.BlockSpec((D_FF, D_MODEL), lambda i: (0, 0)),
                  rowd, rowd, pl.BlockSpec(g.shape, lambda i: (0, 0))],
        out_specs=[rowd, rowd, pl.BlockSpec((1, 1), lambda i: (0, 0)), pl.BlockSpec((1, D_MODEL), lambda i: (0, 0))],
        out_shape=[jax.ShapeDtypeStruct((T, D_MODEL), F32), jax.ShapeDtypeStruct((T, D_MODEL), BF16),
                   jax.ShapeDtypeStruct((1, 1), F32), jax.ShapeDtypeStruct((1, D_MODEL), F32)],
        compiler_params=_params(("arbitrary",)),
    )(act, wd, x2, tgt, g)


def _ffn_bwd_act(dx3b, gate, up, wd, *, tm=1024, fc=1408, rc=256):
    T = dx3b.shape[0]

    def body(dxb_ref, gate_ref, up_ref, wd_ref, dgate_ref, dup_ref):
        for s in range(0, tm, rc):
            dact = lax.dot_general(dxb_ref[s:s + rc, :], wd_ref[...], NT, preferred_element_type=F32)
            gt = gate_ref[s:s + rc, :].astype(F32)
            u = up_ref[s:s + rc, :].astype(F32)
            sg = 1.0 / (1.0 + jnp.exp(-gt))
            dgate_ref[s:s + rc, :] = (dact * u * sg * (1.0 + gt * (1.0 - sg))).astype(BF16)
            dup_ref[s:s + rc, :] = (dact * gt * sg).astype(BF16)

    rowd = pl.BlockSpec((tm, D_MODEL), lambda i, c: (i, 0))
    oc = pl.BlockSpec((tm, fc), lambda i, c: (i, c))
    return pl.pallas_call(
        body, name="ffn_bwd_act", grid=(T // tm, D_FF // fc),
        in_specs=[rowd, oc, oc, pl.BlockSpec((fc, D_MODEL), lambda i, c: (c, 0))],
        out_specs=[oc, oc],
        out_shape=[jax.ShapeDtypeStruct((T, D_FF), BF16), jax.ShapeDtypeStruct((T, D_FF), BF16)],
        compiler_params=_params(("arbitrary", "arbitrary")),
    )(dx3b, gate, up, wd)


def _ffn_bwd_in(dgate, dup, wgt, wut, x2, dx3, g, *, tm=512, rc=256):
    T = x2.shape[0]

    def body(dgate_ref, dup_ref, wg_ref, wu_ref, x2_ref, dx_ref, g_ref, dx2_ref, dx2b_ref, dg_ref):
        @pl.when(pl.program_id(0) == 0)
        def _():
            dg_ref[...] = jnp.zeros_like(dg_ref)

        gv = g_ref[...]
        dgs = jnp.zeros((1, D_MODEL), F32)
        for s in range(0, tm, rc):
            dh = (jnp.dot(dgate_ref[s:s + rc, :], wg_ref[...], preferred_element_type=F32)
                  + jnp.dot(dup_ref[s:s + rc, :], wu_ref[...], preferred_element_type=F32))
            xv = x2_ref[s:s + rc, :]
            r = lax.rsqrt(jnp.mean(xv * xv, axis=-1, keepdims=True) + EPS)
            xh = xv * r
            dgs = dgs + jnp.sum(dh * xh, axis=0, keepdims=True)
            d = dx_ref[s:s + rc, :] + _rms_bwd(dh, xh, r, gv)
            dx2_ref[s:s + rc, :] = d
            dx2b_ref[s:s + rc, :] = d.astype(BF16)
        dg_ref[...] += dgs

    rowd = pl.BlockSpec((tm, D_MODEL), lambda i: (i, 0))
    rowf = pl.BlockSpec((tm, D_FF), lambda i: (i, 0))
    wfull = pl.BlockSpec((D_FF, D_MODEL), lambda i: (0, 0))
    return pl.pallas_call(
        body, name="ffn_bwd_in", grid=(T // tm,),
        in_specs=[rowf, rowf, wfull, wfull, rowd, rowd, pl.BlockSpec(g.shape, lambda i: (0, 0))],
        out_specs=[rowd, rowd, pl.BlockSpec((1, D_MODEL), lambda i: (0, 0))],
        out_shape=[jax.ShapeDtypeStruct((T, D_MODEL), F32), jax.ShapeDtypeStruct((T, D_MODEL), BF16),
                   jax.ShapeDtypeStruct((1, D_MODEL), F32)],
        compiler_params=_params(("arbitrary",)),
    )(dgate, dup, wgt, wut, x2, dx3, g)


def _matmul_tn(a, b, *, tk, tn, tt=2048, out_dtype=BF16, name):
    T, K = a.shape
    N = b.shape[1]
    nt = T // tt

    def body(a_ref, b_ref, o_ref, acc_ref):
        part = lax.dot_general(a_ref[...], b_ref[...], TN, preferred_element_type=F32)

        @pl.when(pl.program_id(2) == 0)
        def _():
            acc_ref[...] = part

        @pl.when(pl.program_id(2) > 0)
        def _():
            acc_ref[...] += part

        @pl.when(pl.program_id(2) == nt - 1)
        def _():
            o_ref[...] = acc_ref[...].astype(out_dtype)

    return pl.pallas_call(
        body, name=name, grid=(K // tk, N // tn, nt),
        in_specs=[pl.BlockSpec((tt, tk), lambda i, j, t: (t, i)), pl.BlockSpec((tt, tn), lambda i, j, t: (t, j))],
        out_specs=pl.BlockSpec((tk, tn), lambda i, j, t: (i, j)),
        out_shape=jax.ShapeDtypeStruct((K, N), out_dtype),
        scratch_shapes=[pltpu.VMEM((tk, tn), F32)],
        compiler_params=_params(("arbitrary", "arbitrary", "arbitrary")),
    )(a, b)


def _wo_bwd(dx2b, wo, oa, ob, ga, gb, *, tm=512):
    T = dx2b.shape[0]

    def body(dx_ref, wo_ref, oa_ref, ob_ref, ga_ref, gb_ref, doa_ref, dob1_ref, dob4_ref, dob16_ref, dga_ref, dgb_ref, scr):
        @pl.when(pl.program_id(0) == 0)
        def _():
            dga_ref[...] = jnp.zeros_like(dga_ref)
            dgb_ref[...] = jnp.zeros_like(dgb_ref)

        dm = lax.dot_general(dx_ref[...], wo_ref[...], NT, preferred_element_type=F32)
        for o_ref, g_ref, dg_ref, sl in ((oa_ref, ga_ref, dga_ref, slice(0, WIDTH)),
                                         (ob_ref, gb_ref, dgb_ref, slice(WIDTH, 2 * WIDTH))):
            ov = o_ref[...].astype(F32)
            r = lax.rsqrt(jnp.mean(ov * ov, axis=-1, keepdims=True) + EPS)
            xh = ov * r
            d = dm[:, sl]
            dg_ref[...] += jnp.sum(d * xh, axis=0, keepdims=True)
            do = _rms_bwd(d, xh, r, g_ref[...])
            if o_ref is oa_ref:
                doa_ref[...] = do.astype(BF16)
            else:
                _scr_put(scr, do)
                for (_, dil), v_ref in zip(BRANCHES, (dob1_ref, dob4_ref, dob16_ref)):
                    _restride(scr, v_ref, dil, tm)

    row = lambda w_: pl.BlockSpec((tm, w_), lambda i: (i, 0))
    full = lambda a: pl.BlockSpec(a.shape, lambda i: (0, 0))
    return pl.pallas_call(
        body, name="wo_bwd", grid=(T // tm,),
        in_specs=[row(D_MODEL), full(wo), row(WIDTH), row(WIDTH), full(ga), full(gb)],
        out_specs=[row(WIDTH)] + _view_specs(tm)
        + [pl.BlockSpec((1, WIDTH), lambda i: (0, 0)), pl.BlockSpec((1, WIDTH), lambda i: (0, 0))],
        out_shape=[jax.ShapeDtypeStruct((T, WIDTH), BF16)] + _view_shapes(T, BF16)
        + [jax.ShapeDtypeStruct((1, WIDTH), F32), jax.ShapeDtypeStruct((1, WIDTH), F32)],
        scratch_shapes=[_scr(tm)],
        compiler_params=_params(("arbitrary",)),
    )(dx2b, wo, oa, ob, ga, gb)


def _dproj(dqa, dka, dva, dqs, dks, dvs, cos, sin, *, tm=512):
    T = dqa.shape[0]

    def body(dqa_ref, dka_ref, dva_ref, q1, q2, q3, k1, k2, k3, v1, v2, v3, cos_ref, sin_ref, dp_ref, db_ref, acc, tmp):
        @pl.when(pl.program_id(0) == 0)
        def _():
            db_ref[...] = jnp.zeros_like(db_ref)

        cosv = cos_ref[...]
        sinv = sin_ref[...]
        lane = lax.broadcasted_iota(jnp.int32, (tm, PAIR), 1)
        first = (lane % HEAD_DIM) < (HEAD_DIM // 2)

        def put(off, val):
            dp_ref[:, off:off + PAIR] = val.astype(BF16)
            db_ref[:, off:off + PAIR] += jnp.sum(val, axis=0, keepdims=True)

        for src, off, width in ((dqa_ref, 0, 512), (dka_ref, 512, 256)):
            for j in range(0, width, PAIR):
                d = src[:, j:j + PAIR].astype(F32)
                put(off + j, d * cosv - _rope_rot(d, first) * sinv)
        for j in range(0, 256, PAIR):
            put(768 + j, dva_ref[:, j:j + PAIR].astype(F32))
        for (a, b, c), off in (((q1, q2, q3), 1024), ((k1, k2, k3), 1536), ((v1, v2, v3), 2048)):
            _unstride(b, acc, BRANCHES[1][1], tm)
            _unstride(c, tmp, BRANCHES[2][1], tm)
            for j in range(N_CHUNK):
                put(off + j * PAIR, a[:, j * PAIR:(j + 1) * PAIR].astype(F32) + acc[j] + tmp[j])

    row = lambda w_: pl.BlockSpec((tm, w_), lambda i: (i, 0))
    return pl.pallas_call(
        body, name="dproj", grid=(T // tm,),
        in_specs=[row(512), row(256), row(256)] + _view_specs(tm) * 3 + [row(PAIR), row(PAIR)],
        out_specs=[row(D_INP), pl.BlockSpec((1, D_INP), lambda i: (0, 0))],
        out_shape=[jax.ShapeDtypeStruct((T, D_INP), BF16), jax.ShapeDtypeStruct((1, D_INP), F32)],
        scratch_shapes=[_scr(tm)] * 2,
        compiler_params=_params(("arbitrary",)),
    )(dqa, dka, dva, *dqs, *dks, *dvs, cos, sin)


def _inproj_bwd(dp, w, x, dx2, g, *, tm=512):
    T = x.shape[0]

    def body(dp_ref, w_ref, x_ref, dx2_ref, g_ref, gx_ref, dg_ref):
        @pl.when(pl.program_id(0) == 0)
        def _():
            dg_ref[...] = jnp.zeros_like(dg_ref)

        dh = jnp.dot(dp_ref[...], w_ref[...], preferred_element_type=F32)
        xv = x_ref[...]
        r = lax.rsqrt(jnp.mean(xv * xv, axis=-1, keepdims=True) + EPS)
        xh = xv * r
        dg_ref[...] += jnp.sum(dh * xh, axis=0, keepdims=True)
        gx_ref[...] = dx2_ref[...] + _rms_bwd(dh, xh, r, g_ref[...])

    row = lambda w_: pl.BlockSpec((tm, w_), lambda i: (i, 0))
    full = lambda a: pl.BlockSpec(a.shape, lambda i: (0, 0))
    return pl.pallas_call(
        body, name="inproj_bwd", grid=(T // tm,),
        in_specs=[row(D_INP), full(w), row(D_MODEL), row(D_MODEL), full(g)],
        out_specs=[row(D_MODEL), pl.BlockSpec((1, D_MODEL), lambda i: (0, 0))],
        out_shape=[jax.ShapeDtypeStruct((T, D_MODEL), F32), jax.ShapeDtypeStruct((1, D_MODEL), F32)],
        compiler_params=_params(("arbitrary",)),
    )(dp, w, x, dx2, g)


def _bias_sink_grads(dsums, bmaps, dsk):
    def body(s1, s2, s3, m1, m2, m3, dsk_ref, drel_ref, dsink_ref):
        row = lax.broadcasted_iota(jnp.int32, (N_HEADS, 128), 0)
        lane = lax.broadcasted_iota(jnp.int32, (N_HEADS, 128), 1)
        out = jnp.zeros((N_HEADS, 128), F32)
        for s_ref, m_ref in ((s1, m1), (s2, m2), (s3, m3)):
            bm = m_ref[...]
            for h in range(N_HEADS):
                a = s_ref[h]
                for b in range(REL_BUCKETS):
                    v = jnp.sum(jnp.sum(jnp.where(bm == b, a, 0.0), axis=-1, keepdims=True), axis=0, keepdims=True)
                    out = out + jnp.where((row == h) & (lane == b), v, 0.0)
        drel_ref[...] = out
        dsink_ref[...] = -jnp.sum(dsk_ref[...], axis=0, keepdims=True)

    vm = pl.BlockSpec(memory_space=pltpu.VMEM)
    return pl.pallas_call(
        body, name="bias_sink_grads",
        in_specs=[vm] * 7, out_specs=[vm, vm],
        out_shape=[jax.ShapeDtypeStruct((N_HEADS, 128), F32), jax.ShapeDtypeStruct((1, WIDTH), F32)],
        compiler_params=_params(),
    )(*dsums, *bmaps, dsk)


def _all_gather(blk, *, name):
    R, C = blk.shape

    def body(x_ref, out_ref, send_sems, recv_sems, local_sem):
        x, y, c = lax.axis_index("x"), lax.axis_index("y"), lax.axis_index("c")
        me, sibling = (x, y, c), (x, y, 1 - c)
        chips = [(1 - x, y), (x, 1 - y), (1 - x, 1 - y)]

        def slot(px, py, pc):
            return out_ref.at[4 * px + 2 * py + pc]

        def copy(k, block, to, src=None):
            return pltpu.make_async_remote_copy(
                src_ref=slot(*block) if src is None else src, dst_ref=slot(*block),
                send_sem=send_sems.at[k], recv_sem=recv_sems.at[k], device_id=to, device_id_type=MESH)

        mine = pltpu.make_async_copy(x_ref, slot(*me), local_sem)
        mine.start()
        first = [copy(0, me, sibling, src=x_ref)]
        first += [copy(1 + j, me, (*chip, c), src=x_ref) for j, chip in enumerate(chips)]
        for cp in first:
            cp.start()
        passed = [copy(4 + j, (*chip, c), sibling) for j, chip in enumerate(chips)]
        for j, chip in enumerate(chips):
            copy(1 + j, (*chip, c), me).wait_recv()
            passed[j].start()
        copy(0, sibling, me).wait_recv()
        for j, chip in enumerate(chips):
            copy(4 + j, (*chip, 1 - c), me).wait_recv()
        for cp in first + passed:
            cp.wait_send()
        mine.wait()

    return pl.pallas_call(
        body, name=name,
        in_specs=[pl.BlockSpec(memory_space=pl.ANY)], out_specs=pl.BlockSpec(memory_space=pl.ANY),
        out_shape=jax.ShapeDtypeStruct((N_DEV, R, C), blk.dtype),
        scratch_shapes=[pltpu.SemaphoreType.DMA((7,)), pltpu.SemaphoreType.DMA((7,)), pltpu.SemaphoreType.DMA],
        compiler_params=pltpu.CompilerParams(has_side_effects=True),
    )(blk)


def _peers(x, y, c):
    return [(x ^ (k >> 2), y ^ ((k >> 1) & 1), c ^ (k & 1)) for k in range(1, N_DEV)]


_HBM = pl.BlockSpec(memory_space=pltpu.HBM)
_SEM = pl.BlockSpec(memory_space=pltpu.SEMAPHORE)
_EFFECT = pltpu.SideEffectType.DATAFLOW_SIDE_EFFECTING


def _exchange_start(srcs, *, gather, name):
    n = len(srcs)
    lands = [lax.empty((N_DEV,) + s.shape[-2:], s.dtype) for s in srcs]

    def body(*refs):
        src_refs, land_refs = refs[:n], refs[n:2 * n]
        send_sems, recv_sems = refs[2 * n], refs[2 * n + 1]
        token = refs[-1]
        x, y, c = lax.axis_index("x"), lax.axis_index("y"), lax.axis_index("c")
        mine = 4 * x + 2 * y + c
        for a in range(n):
            for k, peer in enumerate(_peers(x, y, c)):
                dest = 4 * peer[0] + 2 * peer[1] + peer[2]
                j = a * (N_DEV - 1) + k
                pltpu.make_async_remote_copy(
                    src_ref=src_refs[a] if gather else src_refs[a].at[dest], dst_ref=land_refs[a].at[mine],
                    send_sem=send_sems.at[j], recv_sem=recv_sems.at[j], device_id=peer, device_id_type=MESH).start()
        token[...] = jnp.zeros_like(token)

    sems = pltpu.SemaphoreType.DMA((n * (N_DEV - 1),))
    out = pl.pallas_call(
        body, name=name,
        out_shape=(sems, sems) + tuple(pltpu.HBM(a.shape, a.dtype) for a in list(srcs) + lands)
        + (jax.ShapeDtypeStruct((8, 128), F32),),
        in_specs=(_HBM,) * (2 * n), out_specs=(_SEM, _SEM) + (_HBM,) * (2 * n) + (pl.BlockSpec(memory_space=pltpu.VMEM),),
        input_output_aliases={i: 2 + i for i in range(2 * n)},
        compiler_params=pltpu.CompilerParams(has_side_effects=_EFFECT),
    )(*[pltpu.with_memory_space_constraint(a, pltpu.HBM) for a in list(srcs) + lands])
    return out[:-1], out[-1]


def _exchange_wait(state, after, *, gather, name):
    send_sems, recv_sems = state[0], state[1]
    n = (len(state) - 2) // 2
    arrays = state[2:]

    def body(*refs):
        src_refs, land_refs = refs[:n], refs[n:2 * n]
        send_sems, recv_sems = refs[2 * n], refs[2 * n + 1]
        x, y, c = lax.axis_index("x"), lax.axis_index("y"), lax.axis_index("c")
        for a in range(n):
            for k, peer in enumerate(_peers(x, y, c)):
                other = 4 * peer[0] + 2 * peer[1] + peer[2]
                j = a * (N_DEV - 1) + k
                copy = pltpu.make_async_remote_copy(
                    src_ref=src_refs[a] if gather else src_refs[a].at[other], dst_ref=land_refs[a].at[other],
                    send_sem=send_sems.at[j], recv_sem=recv_sems.at[j], device_id=peer, device_id_type=MESH)
                copy.wait_send()
                copy.wait_recv()

    out = pl.pallas_call(
        body, name=name,
        out_shape=tuple(pltpu.HBM(a.shape, a.dtype) for a in arrays),
        in_specs=(_HBM,) * (2 * n) + (_SEM, _SEM, pl.BlockSpec(memory_space=pl.ANY)), out_specs=(_HBM,) * (2 * n),
        input_output_aliases={i: i for i in range(2 * n)},
        compiler_params=pltpu.CompilerParams(has_side_effects=_EFFECT),
    )(*arrays, send_sems, recv_sems, after)
    return out[n:]


def _fill_own(got, own):
    mine = 4 * lax.axis_index("x") + 2 * lax.axis_index("y") + lax.axis_index("c")
    return lax.dynamic_update_slice(got, own[None], (mine, 0, 0))


def _adam_math(w, g, m, v):
    m = ADAM_B1 * m + (1.0 - ADAM_B1) * g
    v = ADAM_B2 * v + (1.0 - ADAM_B2) * (g * g)
    m_hat = m / (1.0 - ADAM_B1 ** ADAM_STEP)
    v_hat = v / (1.0 - ADAM_B2 ** ADAM_STEP)
    delta = -ADAM_LR * (m_hat / (jnp.sqrt(v_hat) + ADAM_EPS) + ADAM_WD * w)
    return delta, m, v


def _sum_parts(parts, *, name):
    _, R, C = parts.shape
    tr = R // 2
    assert tr % 16 == 0

    def body(p_ref, g_ref):
        g = p_ref[0].astype(F32)
        for s in range(1, N_DEV):
            g = g + p_ref[s].astype(F32)
        g_ref[...] = g

    return pl.pallas_call(
        body, name=name, grid=(R // tr,),
        in_specs=[pl.BlockSpec((N_DEV, tr, C), lambda i: (0, i, 0))], out_specs=pl.BlockSpec((tr, C), lambda i: (i, 0)),
        out_shape=jax.ShapeDtypeStruct((R, C), F32), compiler_params=_params(("arbitrary",)),
    )(parts)


def _adamw(parts, w, m, v, *, name):
    R, C = w.shape
    n_parts = parts.shape[0]
    tr = R // 2
    assert tr % 16 == 0

    def body(p_ref, w_ref, m_ref, v_ref, g_ref, d_ref, nm_ref, nv_ref):
        g = p_ref[0].astype(F32)
        for s in range(1, n_parts):
            g = g + p_ref[s].astype(F32)
        d, nm, nv = _adam_math(w_ref[...], g, m_ref[...], v_ref[...])
        g_ref[...] = g
        d_ref[...] = d
        nm_ref[...] = nm
        nv_ref[...] = nv

    blk = pl.BlockSpec((tr, C), lambda i: (i, 0))
    return pl.pallas_call(
        body, name=name, grid=(R // tr,),
        in_specs=[pl.BlockSpec((n_parts, tr, C), lambda i: (0, i, 0)), blk, blk, blk],
        out_specs=[blk] * 4, out_shape=[jax.ShapeDtypeStruct((R, C), F32)] * 4,
        compiler_params=_params(("arbitrary",)),
    )(parts, w, m, v)


def _adamw_small(parts, w, m, v):
    def body(p_ref, w_ref, m_ref, v_ref, g_ref, d_ref, nm_ref, nv_ref):
        g = p_ref[0]
        for s in range(1, N_DEV):
            g = g + p_ref[s]
        d, nm, nv = _adam_math(w_ref[...], g, m_ref[...], v_ref[...])
        g_ref[...] = g
        d_ref[...] = d
        nm_ref[...] = nm
        nv_ref[...] = nv

    vm = pl.BlockSpec(memory_space=pltpu.VMEM)
    return pl.pallas_call(
        body, name="adamw_small", in_specs=[vm] * 4, out_specs=[vm] * 4,
        out_shape=[jax.ShapeDtypeStruct((SMALL_ROWS, 128), F32)] * 4, compiler_params=_params(),
    )(parts, w, m, v)


def _t5_bucket(dist):
    max_exact = REL_BUCKETS // 2
    df = jnp.maximum(dist, 1).astype(F32)
    large = max_exact + (jnp.log(df / max_exact) / math.log(REL_MAX_DISTANCE / max_exact)
                         * (REL_BUCKETS - max_exact)).astype(jnp.int32)
    large = jnp.minimum(large, REL_BUCKETS - 1)
    return jnp.where(dist < max_exact, dist, large)


def _band_tables(rel_table, dil, n_back):
    qi = jnp.arange(BLK)[:, None]
    kj = jnp.arange(2 * BLK)[None, :]
    delta = BLK + qi - kj
    in_band = (delta >= 0) & (delta <= n_back)
    if rel_table is None:
        vals = jnp.zeros((N_HEADS, BLK, 2 * BLK), F32)
        bmap = None
    else:
        bucket = _t5_bucket(jnp.clip(delta, 0, n_back) * dil)
        vals = jnp.zeros((N_HEADS, BLK, 2 * BLK), F32)
        for b in range(REL_BUCKETS):
            vals = jnp.where((bucket == b)[None], rel_table[b][:, None, None], vals)
        bmap = jnp.where(in_band, bucket, -1).astype(jnp.int32)
    later = jnp.where(in_band[None], vals, NEG)
    first = jnp.where((in_band & (kj >= BLK))[None], vals, NEG)
    return jnp.stack([later, first]), bmap


def _rope_tables(T):
    half = HEAD_DIM // 2
    inv_freq = ROPE_THETA ** (-jnp.arange(half, dtype=F32) / half)
    ang = jnp.arange(T, dtype=F32)[:, None] * inv_freq[None, :]
    cos, sin = jnp.cos(ang), jnp.sin(ang)
    return jnp.tile(cos, (1, 4)), jnp.tile(jnp.concatenate([-sin, sin], axis=1), (1, 2))


def _widen_in(a, axis):
    sl = lambda lo, hi: lax.slice_in_dim(a, lo, hi, axis=axis)
    dup = lambda lo: [sl(lo, lo + 64), sl(lo, lo + 64), sl(lo + 64, lo + 128), sl(lo + 64, lo + 128)]
    return jnp.concatenate([sl(0, 512)] + dup(512) + dup(640) + [sl(768, D_IN)], axis=axis)


def _fold_in(a, axis):
    sl = lambda lo, hi: lax.slice_in_dim(a, lo, hi, axis=axis)
    fold = lambda lo: [sl(lo, lo + 64) + sl(lo + 64, lo + 128), sl(lo + 128, lo + 192) + sl(lo + 192, lo + 256)]
    return jnp.concatenate([sl(0, 512)] + fold(512) + fold(768) + [sl(1024, D_INP)], axis=axis)


def _local_step(x, tgt, g_attn, wint, b_in, sinks, rel_table, g_out_a, g_out_b, g_ffn, g_final, token, rest_fn, early_fn):
    T = x.shape[0]
    cos, sin = _rope_tables(T)
    cos = cos + token[0, 0]
    winp = _widen_in(wint, 0)
    binp = _widen_in(b_in, 1)
    g_final2 = g_final.reshape(1, D_MODEL)
    sink8 = sinks.reshape(N_HEADS)

    bias_a, _ = _band_tables(None, 1, BLK - 1)
    tabs = [_band_tables(rel_table, dil, window // dil) for window, dil in BRANCHES]

    h1, qa, ka, va, *qkv_b = _norm_proj(x, g_attn, winp, binp, cos, sin)
    qbs, kbs, vbs = qkv_b[0:3], qkv_b[3:6], qkv_b[6:9]
    oa, lse_a = _attn_fwd(qa, ka, va, bias_a, sink8, dil=1, kv_pairs=2, use_sink=True, name="attn_a_fwd")
    outs = [_attn_fwd(qbs[n], kbs[n], vbs[n], tabs[n][0], sink8, dil=dil, kv_pairs=4, use_sink=False,
                      name=f"attn_b{n}_fwd") for n, (_, dil) in enumerate(BRANCHES)]
    wo, wgt, wut, wd = rest_fn(outs[2][1])
    x2, mixed, h2, *ob_lse = _merge_wo(x, oa, outs[0][0], outs[1][0], outs[2][0], outs[0][1], outs[1][1], outs[2][1],
                                       g_out_a, g_out_b, wo, g_ffn)
    obs, lses = ob_lse[0:3], ob_lse[3:6]
    gate, up, act = _ffn_up(h2, wgt, wut)
    dx3, dx3b, loss, dg_final = _ffn_down_loss(act, wd, x2, tgt, g_final2)

    dgate, dup = _ffn_bwd_act(dx3b, gate, up, wd)
    dx2, dx2b, dg_ffn = _ffn_bwd_in(dgate, dup, wgt, wut, x2, dx3, g_ffn)
    dwd = _matmul_tn(act, dx3b, tk=1408, tn=1024, name="dw_down")
    dwgt = _matmul_tn(dgate, h2, tk=1408, tn=1024, name="dw_gate")
    dwut = _matmul_tn(dup, h2, tk=1408, tn=1024, name="dw_up")
    dwo = _matmul_tn(mixed, dx2b, tk=1024, tn=1024, name="dw_o")
    early, token2 = early_fn(dict(w_o=dwo, w_gate=dwgt, w_up=dwut, w_down=dwd))
    doa, *dobs, dg_out_a, dg_out_b = _wo_bwd(dx2b, wo, oa, obs[0], g_out_a + token2[0, 0], g_out_b)

    dqa, dka, dva, _, dsk = _attn_bwd(qa, ka, va, oa, doa, lse_a, bias_a, sink8, dil=1, kv_pairs=2, use_sink=True,
                                      name="attn_a_bwd")
    res = [_attn_bwd(qbs[n], kbs[n], vbs[n], obs[n], dobs[n], lses[n], tabs[n][0], sink8, dil=dil, kv_pairs=4,
                     use_sink=False, name=f"attn_b{n}_bwd") for n, (_, dil) in enumerate(BRANCHES)]
    dp, dbp = _dproj(dqa, dka, dva, [r[0] for r in res], [r[1] for r in res], [r[2] for r in res], cos, sin)
    grad_x, dg_attn = _inproj_bwd(dp, winp, x, dx2, g_attn)
    dwin = _fold_in(_matmul_tn(dp, h1, tk=1280, tn=1024, out_dtype=F32, name="dw_in"), 0)
    drel, dsink = _bias_sink_grads([r[3] for r in res], [t[1] for t in tabs], dsk)

    small = dict(
        g_attn=dg_attn, b_in=_fold_in(dbp, 1), sinks=dsink[:, ::HEAD_DIM], rel_table=drel[:, :REL_BUCKETS].T,
        g_out_a=dg_out_a, g_out_b=dg_out_b, g_ffn=dg_ffn, g_final=dg_final.reshape(D_MODEL))
    return loss[0, 0], grad_x, dwin, early, small


SMALL_NAMES = ("g_attn", "b_in", "sinks", "rel_table", "g_out_a", "g_out_b", "g_ffn", "g_final")


def _pack_small(vals):
    flat = jnp.concatenate([vals[n].reshape(-1).astype(F32) for n in SMALL_NAMES])
    return jnp.pad(flat, (0, SMALL_ROWS * 128 - flat.shape[0])).reshape(SMALL_ROWS, 128)


def _unpack_small(packed, like):
    flat = packed.reshape(-1)
    out, off = {}, 0
    for n in SMALL_NAMES:
        size = like[n].size
        out[n] = flat[off:off + size].reshape(like[n].shape)
        off += size
    return out


def kernel(x, g_attn, w_in, b_in, sinks, rel_table, g_out_a, g_out_b, w_o, g_ffn, w_gate, w_up, w_down, g_final, loss_target, m_g_attn, m_w_in, m_b_in, m_sinks, m_rel_table, m_g_out_a, m_g_out_b, m_w_o, m_g_ffn, m_w_gate, m_w_up, m_w_down, m_g_final, v_g_attn, v_w_in, v_b_in, v_sinks, v_rel_table, v_g_out_a, v_g_out_b, v_w_o, v_g_ffn, v_w_gate, v_w_up, v_w_down, v_g_final):
    mine = 4 * lax.axis_index("x") + 2 * lax.axis_index("y") + lax.axis_index("c")
    rest_names = ("w_o", "w_gate", "w_up", "w_down")

    rest = [w_o[0].astype(BF16), w_gate[0].astype(BF16).T, w_up[0].astype(BF16).T, w_down[0].astype(BF16)]
    wint = _all_gather(w_in[0].astype(BF16).T, name="gather_w_in").reshape(D_IN, D_MODEL)
    wint, rest = lax.optimization_barrier((wint, rest))
    rest_state, token = _exchange_start(rest, gather=True, name="gather_rest_start")

    def rest_fn(after):
        got = _exchange_wait(rest_state, after, gather=True, name="gather_rest_wait")
        return [_fill_own(g, own).reshape(N_DEV * own.shape[0], D_MODEL) for g, own in zip(got, rest)]

    def early_fn(dws):
        parts = [dws[n].reshape(N_DEV, -1, D_MODEL) for n in rest_names]
        own = [lax.dynamic_index_in_dim(p, mine, 0, keepdims=False) for p in parts]
        state, token2 = _exchange_start(parts, gather=False, name="scatter_rest_start")
        return (state, own), token2

    loss_part, grad_x, dwint, (early_state, early_own), small = _local_step(
        x[0], loss_target[0], g_attn, wint, b_in, sinks, rel_table, g_out_a, g_out_b, g_ffn, g_final, token,
        rest_fn, early_fn)
    loss = lax.psum(loss_part, ("x", "y", "c"))

    parts_in = dwint.astype(BF16).reshape(N_DEV, D_IN // N_DEV, D_MODEL)
    own_in = lax.dynamic_index_in_dim(parts_in, mine, 0, keepdims=False)
    in_state, token3 = _exchange_start([parts_in], gather=False, name="scatter_w_in_start")
    got = [_fill_own(g, own) for g, own in
           zip(_exchange_wait(early_state, token3, gather=False, name="scatter_rest_wait"), early_own)]

    def update(n, parts, w, m, v, transposed):
        if transposed:
            parts = _sum_parts(parts, name="sum_" + n).T[None]
        return [a[None] for a in _adamw(parts, w[0], m[0], v[0], name="adamw_" + n)]

    big = dict(w_o=update("w_o", got[0], w_o, m_w_o, v_w_o, False),
               w_gate=update("w_gate", got[1], w_gate, m_w_gate, v_w_gate, True),
               w_up=update("w_up", got[2], w_up, m_w_up, v_w_up, True),
               w_down=update("w_down", got[3], w_down, m_w_down, v_w_down, False))

    ws = dict(g_attn=g_attn, b_in=b_in, sinks=sinks, rel_table=rel_table, g_out_a=g_out_a, g_out_b=g_out_b,
              g_ffn=g_ffn, g_final=g_final)
    ms = dict(g_attn=m_g_attn, b_in=m_b_in, sinks=m_sinks, rel_table=m_rel_table, g_out_a=m_g_out_a,
              g_out_b=m_g_out_b, g_ffn=m_g_ffn, g_final=m_g_final)
    vs = dict(g_attn=v_g_attn, b_in=v_b_in, sinks=v_sinks, rel_table=v_rel_table, g_out_a=v_g_out_a,
              g_out_b=v_g_out_b, g_ffn=v_g_ffn, g_final=v_g_final)
    sparts = _all_gather(_pack_small(small), name="gather_small")
    sm_packed = _adamw_small(sparts, _pack_small(ws), _pack_small(ms), _pack_small(vs))
    sm = [_unpack_small(a, ws) for a in sm_packed]

    done = sm_packed[1][:1, :1] + sum(big[n][1][0, :1, :1] for n in rest_names)
    got_in = _fill_own(_exchange_wait(in_state, done, gather=False, name="scatter_w_in_wait")[0], own_in)
    big["w_in"] = update("w_in", got_in, w_in, m_w_in, v_w_in, True)

    order = ("g_attn", "w_in", "b_in", "sinks", "rel_table", "g_out_a", "g_out_b", "w_o", "g_ffn", "w_gate", "w_up",
             "w_down", "g_final")
    outs = [loss, grad_x[None]]
    for k in range(4):
        outs += [big[n][k] if n in big else sm[k][n] for n in order]
    return tuple(outs)
```

```python
import functools
import math

import jax
import jax.numpy as jnp
from jax import lax
from jax.experimental import pallas as pl
from jax.experimental.pallas import tpu as pltpu

F32 = jnp.float32
BF16 = jnp.bfloat16

N_DEV = 8
D_MODEL = 1024
HEAD_DIM = 64
N_HEADS = 8
PAIR = 2 * HEAD_DIM
WIDTH = N_HEADS * HEAD_DIM
D_IN = 2304
D_INP = 2560
D_FF = 2816
BLK = 128
ROPE_THETA = 150000.0
REL_BUCKETS = 32
REL_MAX_DISTANCE = 2048
EPS = 1e-5
NEG = -1e30
BRANCHES = ((128, 1), (512, 4), (2048, 16))
Q_SCALE = HEAD_DIM ** -0.5

ADAM_LR = 0.001
ADAM_B1 = 0.9
ADAM_B2 = 0.999
ADAM_EPS = 1e-08
ADAM_WD = 0.01
ADAM_STEP = 10

VMEM_LIMIT = 56 * 1024 * 1024
MESH = pl.DeviceIdType.MESH

NT = (((1,), (1,)), ((), ()))
TN = (((0,), (0,)), ((), ()))

SMALL_ROWS = 56


def _params(sem=None):
    return pltpu.CompilerParams(dimension_semantics=sem, vmem_limit_bytes=VMEM_LIMIT)


def _rms_bwd(dh, xh, r, g):
    u = dh * g
    return r * (u - xh * jnp.mean(u * xh, axis=-1, keepdims=True))


def _rope_rot(t, first):
    return jnp.where(first, pltpu.roll(t, 96, 1), pltpu.roll(t, 32, 1))


N_CHUNK = WIDTH // PAIR


def _scr(tm):
    return pltpu.VMEM((N_CHUNK, tm, PAIR), F32)


def _scr_get(scr):
    return jnp.concatenate([scr[j] for j in range(N_CHUNK)], axis=1)


def _scr_put(scr, val):
    for j in range(N_CHUNK):
        scr[j] = val[:, j * PAIR:(j + 1) * PAIR]


def _unstride(view_ref, scr, dil, tm):
    n = tm // dil
    for r in range(dil):
        for j in range(N_CHUNK):
            col = r * WIDTH + j * PAIR
            scr.at[j][pl.ds(r, n, stride=dil), :] = view_ref[:, col:col + PAIR].astype(F32)


def _restride(scr, out_ref, dil, tm):
    n = tm // dil
    for r in range(dil):
        for j in range(N_CHUNK):
            col = r * WIDTH + j * PAIR
            rows = scr[j] if dil == 1 else scr.at[j][pl.ds(r, n, stride=dil), :]
            out_ref[:, col:col + PAIR] = rows.astype(out_ref.dtype)


def _view_specs(tm):
    return [pl.BlockSpec((tm // dil, dil * WIDTH), lambda i: (i, 0)) for _, dil in BRANCHES]


def _view_shapes(T, dtype):
    return [jax.ShapeDtypeStruct((T // dil, dil * WIDTH), dtype) for _, dil in BRANCHES]


def _norm_proj(x, g, w, b, cos, sin, *, tm=512):
    T = x.shape[0]

    def body(x_ref, g_ref, w_ref, b_ref, cos_ref, sin_ref, h_ref, qa_ref, ka_ref, va_ref, *rest):
        outs_b, ys = rest[:9], rest[9]
        xv = x_ref[...]
        r = lax.rsqrt(jnp.mean(xv * xv, axis=-1, keepdims=True) + EPS)
        h = (xv * r * g_ref[...]).astype(BF16)
        h_ref[...] = h
        cosv = cos_ref[...]
        sinv = sin_ref[...]
        lane = lax.broadcasted_iota(jnp.int32, (tm, PAIR), 1)
        first = (lane % HEAD_DIM) < (HEAD_DIM // 2)

        def proj(off):
            return (lax.dot_general(h, w_ref[off:off + 256, :], NT, preferred_element_type=F32)
                    + b_ref[:, off:off + 256])

        for (off, width, rot, scale), o_ref in zip(((0, 512, True, Q_SCALE), (512, 256, True, 1.0), (768, 256, False, 1.0)),
                                                   (qa_ref, ka_ref, va_ref)):
            for c in range(0, width, 256):
                y = proj(off + c)
                for j in range(0, 256, PAIR):
                    t = y[:, j:j + PAIR]
                    if rot:
                        t = t * cosv + _rope_rot(t, first) * sinv
                    if scale != 1.0:
                        t = t * scale
                    o_ref[:, c + j:c + j + PAIR] = t.astype(BF16)
        for n, (off, scale) in enumerate(((1024, Q_SCALE), (1536, 1.0), (2048, 1.0))):
            for c in range(0, WIDTH, 256):
                y = proj(off + c)
                y = y * scale if scale != 1.0 else y
                for j in range(0, 256, PAIR):
                    ys[(c + j) // PAIR] = y[:, j:j + PAIR]
            for (_, dil), o_ref in zip(BRANCHES, outs_b[3 * n:3 * n + 3]):
                _restride(ys, o_ref, dil, tm)

    row = lambda w_: pl.BlockSpec((tm, w_), lambda i: (i, 0))
    full = lambda a: pl.BlockSpec(a.shape, lambda i: (0, 0))
    return pl.pallas_call(
        body, name="norm_proj", grid=(T // tm,),
        in_specs=[row(D_MODEL), full(g), full(w), full(b), row(PAIR), row(PAIR)],
        out_specs=[row(D_MODEL), row(512), row(256), row(256)] + _view_specs(tm) * 3,
        out_shape=[jax.ShapeDtypeStruct((T, n), BF16) for n in (D_MODEL, 512, 256, 256)] + _view_shapes(T, BF16) * 3,
        scratch_shapes=[_scr(tm)],
        compiler_params=_params(("arbitrary",)),
    )(x, g, w, b, cos, sin)


SUB = 2


def _attn_specs(kvw, ns, clamp):
    qi = (lambda r, i: (jnp.minimum(i, ns - 1), r)) if clamp else (lambda r, i: (i, r))
    q_spec = pl.BlockSpec((SUB * BLK, WIDTH), qi)
    kc_spec = pl.BlockSpec((SUB * BLK, kvw), qi)
    kp_spec = pl.BlockSpec((BLK, kvw), lambda r, i: (jnp.maximum(SUB * jnp.minimum(i, ns - 1) - 1, 0), r))
    b_spec = pl.BlockSpec((2, N_HEADS, BLK, 2 * BLK), lambda r, i: (0, 0, 0, 0))
    return q_spec, kp_spec, kc_spec, b_spec


def _window(prev_ref, cur_ref, j, ksl):
    before = prev_ref[:, ksl] if j == 0 else cur_ref[(j - 1) * BLK:j * BLK, ksl]
    return jnp.concatenate([before, cur_ref[j * BLK:(j + 1) * BLK, ksl]], axis=0)


def _attn_fwd(q, k, v, bias, sinks, *, dil, kv_pairs, use_sink, name):
    L = q.shape[0]
    ns = L // (SUB * BLK)
    kvw = kv_pairs * PAIR
    rep = 4 // kv_pairs

    def body(sink_ref, q_ref, kp_ref, kc_ref, vp_ref, vc_ref, b_ref, o_ref, lse_ref):
        lane = lax.broadcasted_iota(jnp.int32, (1, PAIR), 1)
        lo = lane < HEAD_DIM
        first = jnp.where(pl.program_id(1) == 0, 1, 0)
        for j in range(SUB):
            rows = slice(j * BLK, (j + 1) * BLK)
            for hp in range(4):
                sl = slice(hp * PAIR, (hp + 1) * PAIR)
                ksl = slice((hp // rep) * PAIR, (hp // rep + 1) * PAIR)
                qp = q_ref[rows, sl]
                kk = _window(kp_ref, kc_ref, j, ksl)
                vv = _window(vp_ref, vc_ref, j, ksl)
                o_pair = None
                lse_pair = None
                for e in range(2):
                    h = 2 * hp + e
                    msk = lo if e == 0 else jnp.logical_not(lo)
                    qm = jnp.where(msk, qp, jnp.zeros_like(qp))
                    s = lax.dot_general(qm, kk, NT, preferred_element_type=F32) + (b_ref[first, h] if j == 0 else b_ref[0, h])
                    m = jnp.max(s, axis=-1, keepdims=True)
                    if use_sink:
                        sk = sink_ref[h]
                        m = jnp.maximum(m, sk)
                    p = jnp.exp(s - m)
                    l = jnp.sum(p, axis=-1, keepdims=True)
                    if use_sink:
                        l = l + jnp.exp(sk - m)
                    vm = jnp.where(msk, vv, jnp.zeros_like(vv))
                    oe = jnp.dot(p.astype(BF16), vm, preferred_element_type=F32) * (1.0 / l)
                    ls = m + jnp.log(l)
                    if e == 0:
                        o_pair = oe
                        lse_pair = jnp.broadcast_to(ls, (BLK, PAIR))
                    else:
                        o_pair = o_pair + oe
                        lse_pair = jnp.where(lo, lse_pair, ls)
                o_ref[rows, sl] = o_pair.astype(BF16)
                lse_ref[rows, sl] = lse_pair

    q_spec, kp_spec, kc_spec, b_spec = _attn_specs(kvw, ns, False)
    return pl.pallas_call(
        body, name=name, grid=(dil, ns),
        in_specs=[pl.BlockSpec(memory_space=pltpu.SMEM), q_spec, kp_spec, kc_spec, kp_spec, kc_spec, b_spec],
        out_specs=[q_spec, q_spec],
        out_shape=[jax.ShapeDtypeStruct((L, dil * WIDTH), BF16), jax.ShapeDtypeStruct((L, dil * WIDTH), F32)],
        compiler_params=_params(("arbitrary", "arbitrary")),
    )(sinks, q, k, k, v, v, bias)


def _attn_bwd(q, k, v, o, do, lse, bias, sinks, *, dil, kv_pairs, use_sink, name):
    L = q.shape[0]
    ns = L // (SUB * BLK)
    kvw = kv_pairs * PAIR
    rep = 4 // kv_pairs
    last = slice((SUB - 1) * BLK, SUB * BLK)

    def body(sink_ref, q_ref, kp_ref, kc_ref, vp_ref, vc_ref, o_ref, do_ref, lse_ref, b_ref,
             dq_ref, dk_ref, dv_ref, dsum_ref, dsk_ref, pk_ref, pv_ref):
        r = pl.program_id(0)
        i = pl.program_id(1)

        @pl.when((r == 0) & (i == 0))
        def _():
            dsum_ref[...] = jnp.zeros_like(dsum_ref)
            dsk_ref[...] = jnp.zeros_like(dsk_ref)

        @pl.when(i == 0)
        def _():
            pk_ref[...] = jnp.zeros_like(pk_ref)
            pv_ref[...] = jnp.zeros_like(pv_ref)

        @pl.when(i < ns)
        def _():
            lo = lax.broadcasted_iota(jnp.int32, (1, PAIR), 1) < HEAD_DIM
            first = jnp.where(i == 0, 1, 0)
            dks = [[None] * kv_pairs for _ in range(SUB)]
            dvs = [[None] * kv_pairs for _ in range(SUB)]
            for j in range(SUB):
                rows = slice(j * BLK, (j + 1) * BLK)
                for hp in range(4):
                    kvp = hp // rep
                    sl = slice(hp * PAIR, (hp + 1) * PAIR)
                    ksl = slice(kvp * PAIR, (kvp + 1) * PAIR)
                    qp = q_ref[rows, sl]
                    dop = do_ref[rows, sl]
                    prod = dop.astype(F32) * o_ref[rows, sl].astype(F32)
                    kk = _window(kp_ref, kc_ref, j, ksl)
                    vv = _window(vp_ref, vc_ref, j, ksl)
                    dq_pair = None
                    c_pair = None
                    qms, doms, dsbs, pbs = [], [], [], []
                    for e in range(2):
                        h = 2 * hp + e
                        msk = lo if e == 0 else jnp.logical_not(lo)
                        qm = jnp.where(msk, qp, jnp.zeros_like(qp))
                        dom = jnp.where(msk, dop, jnp.zeros_like(dop))
                        km = jnp.where(msk, kk, jnp.zeros_like(kk))
                        s = (lax.dot_general(qm, kk, NT, preferred_element_type=F32)
                             + (b_ref[first, h] if j == 0 else b_ref[0, h]))
                        ls = lse_ref[rows, h * HEAD_DIM:h * HEAD_DIM + 1]
                        p = jnp.exp(s - ls)
                        dp = lax.dot_general(dom, vv, NT, preferred_element_type=F32)
                        delta = jnp.sum(jnp.where(msk, prod, 0.0), axis=-1, keepdims=True)
                        ds = p * (dp - delta)
                        if use_sink:
                            ce = jnp.exp(sink_ref[h] - ls) * delta
                            c_pair = jnp.broadcast_to(ce, (BLK, PAIR)) if e == 0 else jnp.where(msk, ce, c_pair)
                        else:
                            dsum_ref[h] += ds
                        dsb = ds.astype(BF16)
                        dqe = jnp.dot(dsb, km, preferred_element_type=F32)
                        dq_pair = dqe if e == 0 else dq_pair + dqe
                        qms.append(qm)
                        doms.append(dom)
                        dsbs.append(dsb)
                        pbs.append(p.astype(BF16))
                    dke = lax.dot_general(jnp.concatenate(dsbs, axis=0), jnp.concatenate(qms, axis=0), TN,
                                          preferred_element_type=F32)
                    dve = lax.dot_general(jnp.concatenate(pbs, axis=0), jnp.concatenate(doms, axis=0), TN,
                                          preferred_element_type=F32)
                    dks[j][kvp] = dke if dks[j][kvp] is None else dks[j][kvp] + dke
                    dvs[j][kvp] = dve if dvs[j][kvp] is None else dvs[j][kvp] + dve
                    dq_ref[rows, sl] = (dq_pair * Q_SCALE).astype(BF16)
                    if use_sink:
                        dsk_ref[:, sl] += c_pair
            for kvp in range(kv_pairs):
                ksl = slice(kvp * PAIR, (kvp + 1) * PAIR)
                for pend_ref, out_ref, parts in ((pk_ref, dk_ref, [d[kvp] for d in dks]),
                                                 (pv_ref, dv_ref, [d[kvp] for d in dvs])):
                    if SUB > 1:
                        out_ref[:(SUB - 1) * BLK, ksl] = pend_ref[:(SUB - 1) * BLK, ksl].astype(BF16)
                    out_ref[last, ksl] = (pend_ref[last, ksl] + parts[0][:BLK]).astype(BF16)
                    for j in range(SUB):
                        own = parts[j][BLK:]
                        pend_ref[j * BLK:(j + 1) * BLK, ksl] = own + parts[j + 1][:BLK] if j + 1 < SUB else own

        @pl.when(i == ns)
        def _():
            dk_ref[...] = pk_ref[...].astype(BF16)
            dv_ref[...] = pv_ref[...].astype(BF16)

    q_spec, kp_spec, kc_spec, b_spec = _attn_specs(kvw, ns, True)
    dkv_spec = pl.BlockSpec((SUB * BLK, kvw), lambda r, i: (jnp.maximum(i - 1, 0), r))
    return pl.pallas_call(
        body, name=name, grid=(dil, ns + 1),
        in_specs=[pl.BlockSpec(memory_space=pltpu.SMEM), q_spec, kp_spec, kc_spec, kp_spec, kc_spec,
                  q_spec, q_spec, q_spec, b_spec],
        out_specs=[q_spec, dkv_spec, dkv_spec,
                   pl.BlockSpec((N_HEADS, BLK, 2 * BLK), lambda r, i: (0, 0, 0)),
                   pl.BlockSpec((BLK, WIDTH), lambda r, i: (0, 0))],
        out_shape=[jax.ShapeDtypeStruct((L, dil * WIDTH), BF16),
                   jax.ShapeDtypeStruct((L, dil * kvw), BF16),
                   jax.ShapeDtypeStruct((L, dil * kvw), BF16),
                   jax.ShapeDtypeStruct((N_HEADS, BLK, 2 * BLK), F32),
                   jax.ShapeDtypeStruct((BLK, WIDTH), F32)],
        scratch_shapes=[pltpu.VMEM((SUB * BLK, kvw), F32), pltpu.VMEM((SUB * BLK, kvw), F32)],
        compiler_params=_params(("arbitrary", "arbitrary")),
    )(sinks, q, k, k, v, v, o, do, lse, bias)


def _merge_wo(x, oa, o1, o2, o3, l1, l2, l3, ga, gb, wo, gf, *, tm=512):
    T = x.shape[0]

    def body(x_ref, oa_ref, o1_ref, o2_ref, o3_ref, l1_ref, l2_ref, l3_ref, ga_ref, gb_ref, wo_ref, gf_ref,
             x2_ref, mix_ref, h2_ref, ob1_ref, ob4_ref, ob16_ref, ls1_ref, ls4_ref, ls16_ref, so2, so3, sl2, sl3):
        _unstride(o2_ref, so2, BRANCHES[1][1], tm)
        _unstride(o3_ref, so3, BRANCHES[2][1], tm)
        _unstride(l2_ref, sl2, BRANCHES[1][1], tm)
        _unstride(l3_ref, sl3, BRANCHES[2][1], tm)
        la, lb, lc = l1_ref[...], _scr_get(sl2), _scr_get(sl3)
        m = jnp.maximum(jnp.maximum(la, lb), lc)
        ea, eb, ec = jnp.exp(la - m), jnp.exp(lb - m), jnp.exp(lc - m)
        den = ea + eb + ec
        inv = 1.0 / den
        ob = (ea * o1_ref[...].astype(F32) + eb * _scr_get(so2) + ec * _scr_get(so3)) * inv
        _scr_put(so2, ob)
        _scr_put(sl2, m + jnp.log(den))
        for (_, dil), o_ref, l_ref in zip(BRANCHES, (ob1_ref, ob4_ref, ob16_ref), (ls1_ref, ls4_ref, ls16_ref)):
            _restride(so2, o_ref, dil, tm)
            _restride(sl2, l_ref, dil, tm)
        oav = oa_ref[...].astype(F32)
        ra = lax.rsqrt(jnp.mean(oav * oav, axis=-1, keepdims=True) + EPS)
        rb = lax.rsqrt(jnp.mean(ob * ob, axis=-1, keepdims=True) + EPS)
        mix_ref[:, :WIDTH] = (oav * ra * ga_ref[...]).astype(BF16)
        mix_ref[:, WIDTH:] = (ob * rb * gb_ref[...]).astype(BF16)
        x2 = x_ref[...] + jnp.dot(mix_ref[...], wo_ref[...], preferred_element_type=F32)
        x2_ref[...] = x2
        r2 = lax.rsqrt(jnp.mean(x2 * x2, axis=-1, keepdims=True) + EPS)
        h2_ref[...] = (x2 * r2 * gf_ref[...]).astype(BF16)

    row = lambda w_: pl.BlockSpec((tm, w_), lambda i: (i, 0))
    full = lambda a: pl.BlockSpec(a.shape, lambda i: (0, 0))
    return pl.pallas_call(
        body, name="merge_wo", grid=(T // tm,),
        in_specs=[row(D_MODEL), row(WIDTH)] + _view_specs(tm) * 2 + [full(ga), full(gb), full(wo), full(gf)],
        out_specs=[row(D_MODEL), row(D_MODEL), row(D_MODEL)] + _view_specs(tm) * 2,
        out_shape=[jax.ShapeDtypeStruct((T, D_MODEL), F32), jax.ShapeDtypeStruct((T, D_MODEL), BF16),
                   jax.ShapeDtypeStruct((T, D_MODEL), BF16)] + _view_shapes(T, BF16) + _view_shapes(T, F32),
        scratch_shapes=[_scr(tm)] * 4,
        compiler_params=_params(("arbitrary",)),
    )(x, oa, o1, o2, o3, l1, l2, l3, ga, gb, wo, gf)


def _ffn_up(h2, wgt, wut, *, tm=1024, fc=1408, rc=256):
    T = h2.shape[0]

    def body(h_ref, wg_ref, wu_ref, gate_ref, up_ref, act_ref):
        for s in range(0, tm, rc):
            h = h_ref[s:s + rc, :]
            gt = lax.dot_general(h, wg_ref[...], NT, preferred_element_type=F32)
            u = lax.dot_general(h, wu_ref[...], NT, preferred_element_type=F32)
            gate_ref[s:s + rc, :] = gt.astype(BF16)
            up_ref[s:s + rc, :] = u.astype(BF16)
            act_ref[s:s + rc, :] = (gt * (1.0 / (1.0 + jnp.exp(-gt))) * u).astype(BF16)

    rowd = pl.BlockSpec((tm, D_MODEL), lambda i, c: (i, 0))
    wrow = pl.BlockSpec((fc, D_MODEL), lambda i, c: (c, 0))
    oc = pl.BlockSpec((tm, fc), lambda i, c: (i, c))
    return pl.pallas_call(
        body, name="ffn_up", grid=(T // tm, D_FF // fc),
        in_specs=[rowd, wrow, wrow],
        out_specs=[oc, oc, oc],
        out_shape=[jax.ShapeDtypeStruct((T, D_FF), BF16)] * 3,
        compiler_params=_params(("arbitrary", "arbitrary")),
    )(h2, wgt, wut)


def _ffn_down_loss(act, wd, x2, tgt, g, *, tm=512, rc=256):
    T = x2.shape[0]

    def body(act_ref, wd_ref, x2_ref, tgt_ref, g_ref, dx_ref, dxb_ref, loss_ref, dg_ref):
        @pl.when(pl.program_id(0) == 0)
        def _():
            loss_ref[...] = jnp.zeros_like(loss_ref)
            dg_ref[...] = jnp.zeros_like(dg_ref)

        gv = g_ref[...]
        lsum = jnp.zeros((1, 1), F32)
        dgs = jnp.zeros((1, D_MODEL), F32)
        for c in range(0, tm, rc):
            x3 = x2_ref[c:c + rc, :] + jnp.dot(act_ref[c:c + rc, :], wd_ref[...], preferred_element_type=F32)
            r = lax.rsqrt(jnp.mean(x3 * x3, axis=-1, keepdims=True) + EPS)
            xh = x3 * r
            diff = xh * gv - tgt_ref[c:c + rc, :]
            lsum = lsum + jnp.sum(jnp.sum(diff * diff, axis=-1, keepdims=True), axis=0, keepdims=True)
            dy = diff * (1.0 / D_MODEL)
            dgs = dgs + jnp.sum(dy * xh, axis=0, keepdims=True)
            dx = _rms_bwd(dy, xh, r, gv)
            dx_ref[c:c + rc, :] = dx
            dxb_ref[c:c + rc, :] = dx.astype(BF16)
        loss_ref[...] += lsum * (0.5 / D_MODEL)
        dg_ref[...] += dgs

    rowd = pl.BlockSpec((tm, D_MODEL), lambda i: (i, 0))
    return pl.pallas_call(
        body, name="ffn_down_loss", grid=(T // tm,),
        in_specs=[pl.BlockSpec((tm, D_FF), lambda i: (i, 0)), pl.BlockSpec((D_FF, D_MODEL), lambda i: (0, 0)),
                  rowd, rowd, pl.BlockSpec(g.shape, lambda i: (0, 0))],
        out_specs=[rowd, rowd, pl.BlockSpec((1, 1), lambda i: (0, 0)), pl.BlockSpec((1, D_MODEL), lambda i: (0, 0))],
        out_shape=[jax.ShapeDtypeStruct((T, D_MODEL), F32), jax.ShapeDtypeStruct((T, D_MODEL), BF16),
                   jax.ShapeDtypeStruct((1, 1), F32), jax.ShapeDtypeStruct((1, D_MODEL), F32)],
        compiler_params=_params(("arbitrary",)),
    )(act, wd, x2, tgt, g)


def _ffn_bwd_act(dx3b, gate, up, wd, *, tm=1024, fc=1408, rc=256):
    T = dx3b.shape[0]

    def body(dxb_ref, gate_ref, up_ref, wd_ref, dgate_ref, dup_ref):
        for s in range(0, tm, rc):
            dact = lax.dot_general(dxb_ref[s:s + rc, :], wd_ref[...], NT, preferred_element_type=F32)
            gt = gate_ref[s:s + rc, :].astype(F32)
            u = up_ref[s:s + rc, :].astype(F32)
            sg = 1.0 / (1.0 + jnp.exp(-gt))
            dgate_ref[s:s + rc, :] = (dact * u * sg * (1.0 + gt * (1.0 - sg))).astype(BF16)
            dup_ref[s:s + rc, :] = (dact * gt * sg).astype(BF16)

    rowd = pl.BlockSpec((tm, D_MODEL), lambda i, c: (i, 0))
    oc = pl.BlockSpec((tm, fc), lambda i, c: (i, c))
    return pl.pallas_call(
        body, name="ffn_bwd_act", grid=(T // tm, D_FF // fc),
        in_specs=[rowd, oc, oc, pl.BlockSpec((fc, D_MODEL), lambda i, c: (c, 0))],
        out_specs=[oc, oc],
        out_shape=[jax.ShapeDtypeStruct((T, D_FF), BF16), jax.ShapeDtypeStruct((T, D_FF), BF16)],
        compiler_params=_params(("arbitrary", "arbitrary")),
    )(dx3b, gate, up, wd)


def _ffn_bwd_in(dgate, dup, wgt, wut, x2, dx3, g, *, tm=512, rc=256):
    T = x2.shape[0]

    def body(dgate_ref, dup_ref, wg_ref, wu_ref, x2_ref, dx_ref, g_ref, dx2_ref, dx2b_ref, dg_ref):
        @pl.when(pl.program_id(0) == 0)
        def _():
            dg_ref[...] = jnp.zeros_like(dg_ref)

        gv = g_ref[...]
        dgs = jnp.zeros((1, D_MODEL), F32)
        for s in range(0, tm, rc):
            dh = (jnp.dot(dgate_ref[s:s + rc, :], wg_ref[...], preferred_element_type=F32)
                  + jnp.dot(dup_ref[s:s + rc, :], wu_ref[...], preferred_element_type=F32))
            xv = x2_ref[s:s + rc, :]
            r = lax.rsqrt(jnp.mean(xv * xv, axis=-1, keepdims=True) + EPS)
            xh = xv * r
            dgs = dgs + jnp.sum(dh * xh, axis=0, keepdims=True)
            d = dx_ref[s:s + rc, :] + _rms_bwd(dh, xh, r, gv)
            dx2_ref[s:s + rc, :] = d
            dx2b_ref[s:s + rc, :] = d.astype(BF16)
        dg_ref[...] += dgs

    rowd = pl.BlockSpec((tm, D_MODEL), lambda i: (i, 0))
    rowf = pl.BlockSpec((tm, D_FF), lambda i: (i, 0))
    wfull = pl.BlockSpec((D_FF, D_MODEL), lambda i: (0, 0))
    return pl.pallas_call(
        body, name="ffn_bwd_in", grid=(T // tm,),
        in_specs=[rowf, rowf, wfull, wfull, rowd, rowd, pl.BlockSpec(g.shape, lambda i: (0, 0))],
        out_specs=[rowd, rowd, pl.BlockSpec((1, D_MODEL), lambda i: (0, 0))],
        out_shape=[jax.ShapeDtypeStruct((T, D_MODEL), F32), jax.ShapeDtypeStruct((T, D_MODEL), BF16),
                   jax.ShapeDtypeStruct((1, D_MODEL), F32)],
        compiler_params=_params(("arbitrary",)),
    )(dgate, dup, wgt, wut, x2, dx3, g)


def _matmul_tn(a, b, *, tk, tn, tt=2048, out_dtype=BF16, name):
    T, K = a.shape
    N = b.shape[1]
    nt = T // tt

    def body(a_ref, b_ref, o_ref, acc_ref):
        part = lax.dot_general(a_ref[...], b_ref[...], TN, preferred_element_type=F32)

        @pl.when(pl.program_id(2) == 0)
        def _():
            acc_ref[...] = part

        @pl.when(pl.program_id(2) > 0)
        def _():
            acc_ref[...] += part

        @pl.when(pl.program_id(2) == nt - 1)
        def _():
            o_ref[...] = acc_ref[...].astype(out_dtype)

    return pl.pallas_call(
        body, name=name, grid=(K // tk, N // tn, nt),
        in_specs=[pl.BlockSpec((tt, tk), lambda i, j, t: (t, i)), pl.BlockSpec((tt, tn), lambda i, j, t: (t, j))],
        out_specs=pl.BlockSpec((tk, tn), lambda i, j, t: (i, j)),
        out_shape=jax.ShapeDtypeStruct((K, N), out_dtype),
        scratch_shapes=[pltpu.VMEM((tk, tn), F32)],
        compiler_params=_params(("arbitrary", "arbitrary", "arbitrary")),
    )(a, b)


def _wo_bwd(dx2b, wo, oa, ob, ga, gb, *, tm=512):
    T = dx2b.shape[0]

    def body(dx_ref, wo_ref, oa_ref, ob_ref, ga_ref, gb_ref, doa_ref, dob1_ref, dob4_ref, dob16_ref, dga_ref, dgb_ref, scr):
        @pl.when(pl.program_id(0) == 0)
        def _():
            dga_ref[...] = jnp.zeros_like(dga_ref)
            dgb_ref[...] = jnp.zeros_like(dgb_ref)

        dm = lax.dot_general(dx_ref[...], wo_ref[...], NT, preferred_element_type=F32)
        for o_ref, g_ref, dg_ref, sl in ((oa_ref, ga_ref, dga_ref, slice(0, WIDTH)),
                                         (ob_ref, gb_ref, dgb_ref, slice(WIDTH, 2 * WIDTH))):
            ov = o_ref[...].astype(F32)
            r = lax.rsqrt(jnp.mean(ov * ov, axis=-1, keepdims=True) + EPS)
            xh = ov * r
            d = dm[:, sl]
            dg_ref[...] += jnp.sum(d * xh, axis=0, keepdims=True)
            do = _rms_bwd(d, xh, r, g_ref[...])
            if o_ref is oa_ref:
                doa_ref[...] = do.astype(BF16)
            else:
                _scr_put(scr, do)
                for (_, dil), v_ref in zip(BRANCHES, (dob1_ref, dob4_ref, dob16_ref)):
                    _restride(scr, v_ref, dil, tm)

    row = lambda w_: pl.BlockSpec((tm, w_), lambda i: (i, 0))
    full = lambda a: pl.BlockSpec(a.shape, lambda i: (0, 0))
    return pl.pallas_call(
        body, name="wo_bwd", grid=(T // tm,),
        in_specs=[row(D_MODEL), full(wo), row(WIDTH), row(WIDTH), full(ga), full(gb)],
        out_specs=[row(WIDTH)] + _view_specs(tm)
        + [pl.BlockSpec((1, WIDTH), lambda i: (0, 0)), pl.BlockSpec((1, WIDTH), lambda i: (0, 0))],
        out_shape=[jax.ShapeDtypeStruct((T, WIDTH), BF16)] + _view_shapes(T, BF16)
        + [jax.ShapeDtypeStruct((1, WIDTH), F32), jax.ShapeDtypeStruct((1, WIDTH), F32)],
        scratch_shapes=[_scr(tm)],
        compiler_params=_params(("arbitrary",)),
    )(dx2b, wo, oa, ob, ga, gb)


def _dproj(dqa, dka, dva, dqs, dks, dvs, cos, sin, *, tm=512):
    T = dqa.shape[0]

    def body(dqa_ref, dka_ref, dva_ref, q1, q2, q3, k1, k2, k3, v1, v2, v3, cos_ref, sin_ref, dp_ref, db_ref, acc, tmp):
        @pl.when(pl.program_id(0) == 0)
        def _():
            db_ref[...] = jnp.zeros_like(db_ref)

        cosv = cos_ref[...]
        sinv = sin_ref[...]
        lane = lax.broadcasted_iota(jnp.int32, (tm, PAIR), 1)
        first = (lane % HEAD_DIM) < (HEAD_DIM // 2)

        def put(off, val):
            dp_ref[:, off:off + PAIR] = val.astype(BF16)
            db_ref[:, off:off + PAIR] += jnp.sum(val, axis=0, keepdims=True)

        for src, off, width in ((dqa_ref, 0, 512), (dka_ref, 512, 256)):
            for j in range(0, width, PAIR):
                d = src[:, j:j + PAIR].astype(F32)
                put(off + j, d * cosv - _rope_rot(d, first) * sinv)
        for j in range(0, 256, PAIR):
            put(768 + j, dva_ref[:, j:j + PAIR].astype(F32))
        for (a, b, c), off in (((q1, q2, q3), 1024), ((k1, k2, k3), 1536), ((v1, v2, v3), 2048)):
            _unstride(b, acc, BRANCHES[1][1], tm)
            _unstride(c, tmp, BRANCHES[2][1], tm)
            for j in range(N_CHUNK):
                put(off + j * PAIR, a[:, j * PAIR:(j + 1) * PAIR].astype(F32) + acc[j] + tmp[j])

    row = lambda w_: pl.BlockSpec((tm, w_), lambda i: (i, 0))
    return pl.pallas_call(
        body, name="dproj", grid=(T // tm,),
        in_specs=[row(512), row(256), row(256)] + _view_specs(tm) * 3 + [row(PAIR), row(PAIR)],
        out_specs=[row(D_INP), pl.BlockSpec((1, D_INP), lambda i: (0, 0))],
        out_shape=[jax.ShapeDtypeStruct((T, D_INP), BF16), jax.ShapeDtypeStruct((1, D_INP), F32)],
        scratch_shapes=[_scr(tm)] * 2,
        compiler_params=_params(("arbitrary",)),
    )(dqa, dka, dva, *dqs, *dks, *dvs, cos, sin)


def _inproj_bwd(dp, w, x, dx2, g, *, tm=512):
    T = x.shape[0]

    def body(dp_ref, w_ref, x_ref, dx2_ref, g_ref, gx_ref, dg_ref):
        @pl.when(pl.program_id(0) == 0)
        def _():
            dg_ref[...] = jnp.zeros_like(dg_ref)

        dh = jnp.dot(dp_ref[...], w_ref[...], preferred_element_type=F32)
        xv = x_ref[...]
        r = lax.rsqrt(jnp.mean(xv * xv, axis=-1, keepdims=True) + EPS)
        xh = xv * r
        dg_ref[...] += jnp.sum(dh * xh, axis=0, keepdims=True)
        gx_ref[...] = dx2_ref[...] + _rms_bwd(dh, xh, r, g_ref[...])

    row = lambda w_: pl.BlockSpec((tm, w_), lambda i: (i, 0))
    full = lambda a: pl.BlockSpec(a.shape, lambda i: (0, 0))
    return pl.pallas_call(
        body, name="inproj_bwd", grid=(T // tm,),
        in_specs=[row(D_INP), full(w), row(D_MODEL), row(D_MODEL), full(g)],
        out_specs=[row(D_MODEL), pl.BlockSpec((1, D_MODEL), lambda i: (0, 0))],
        out_shape=[jax.ShapeDtypeStruct((T, D_MODEL), F32), jax.ShapeDtypeStruct((1, D_MODEL), F32)],
        compiler_params=_params(("arbitrary",)),
    )(dp, w, x, dx2, g)


def _bias_sink_grads(dsums, bmaps, dsk):
    def body(s1, s2, s3, m1, m2, m3, dsk_ref, drel_ref, dsink_ref):
        row = lax.broadcasted_iota(jnp.int32, (N_HEADS, 128), 0)
        lane = lax.broadcasted_iota(jnp.int32, (N_HEADS, 128), 1)
        out = jnp.zeros((N_HEADS, 128), F32)
        for s_ref, m_ref in ((s1, m1), (s2, m2), (s3, m3)):
            bm = m_ref[...]
            for h in range(N_HEADS):
                a = s_ref[h]
                for b in range(REL_BUCKETS):
                    v = jnp.sum(jnp.sum(jnp.where(bm == b, a, 0.0), axis=-1, keepdims=True), axis=0, keepdims=True)
                    out = out + jnp.where((row == h) & (lane == b), v, 0.0)
        drel_ref[...] = out
        dsink_ref[...] = -jnp.sum(dsk_ref[...], axis=0, keepdims=True)

    vm = pl.BlockSpec(memory_space=pltpu.VMEM)
    return pl.pallas_call(
        body, name="bias_sink_grads",
        in_specs=[vm] * 7, out_specs=[vm, vm],
        out_shape=[jax.ShapeDtypeStruct((N_HEADS, 128), F32), jax.ShapeDtypeStruct((1, WIDTH), F32)],
        compiler_params=_params(),
    )(*dsums, *bmaps, dsk)


def _all_gather(blk, *, name):
    R, C = blk.shape

    def body(x_ref, out_ref, send_sems, recv_sems, local_sem):
        x, y, c = lax.axis_index("x"), lax.axis_index("y"), lax.axis_index("c")
        me, sibling = (x, y, c), (x, y, 1 - c)
        chips = [(1 - x, y), (x, 1 - y), (1 - x, 1 - y)]

        def slot(px, py, pc):
            return out_ref.at[4 * px + 2 * py + pc]

        def copy(k, block, to, src=None):
            return pltpu.make_async_remote_copy(
                src_ref=slot(*block) if src is None else src, dst_ref=slot(*block),
                send_sem=send_sems.at[k], recv_sem=recv_sems.at[k], device_id=to, device_id_type=MESH)

        mine = pltpu.make_async_copy(x_ref, slot(*me), local_sem)
        mine.start()
        first = [copy(0, me, sibling, src=x_ref)]
        first += [copy(1 + j, me, (*chip, c), src=x_ref) for j, chip in enumerate(chips)]
        for cp in first:
            cp.start()
        passed = [copy(4 + j, (*chip, c), sibling) for j, chip in enumerate(chips)]
        for j, chip in enumerate(chips):
            copy(1 + j, (*chip, c), me).wait_recv()
            passed[j].start()
        copy(0, sibling, me).wait_recv()
        for j, chip in enumerate(chips):
            copy(4 + j, (*chip, 1 - c), me).wait_recv()
        for cp in first + passed:
            cp.wait_send()
        mine.wait()

    return pl.pallas_call(
        body, name=name,
        in_specs=[pl.BlockSpec(memory_space=pl.ANY)], out_specs=pl.BlockSpec(memory_space=pl.ANY),
        out_shape=jax.ShapeDtypeStruct((N_DEV, R, C), blk.dtype),
        scratch_shapes=[pltpu.SemaphoreType.DMA((7,)), pltpu.SemaphoreType.DMA((7,)), pltpu.SemaphoreType.DMA],
        compiler_params=pltpu.CompilerParams(has_side_effects=True),
    )(blk)


def _peers(x, y, c):
    return [(x ^ (k >> 2), y ^ ((k >> 1) & 1), c ^ (k & 1)) for k in range(1, N_DEV)]


_HBM = pl.BlockSpec(memory_space=pltpu.HBM)
_SEM = pl.BlockSpec(memory_space=pltpu.SEMAPHORE)
_EFFECT = pltpu.SideEffectType.DATAFLOW_SIDE_EFFECTING


def _exchange_start(srcs, *, gather, name):
    n = len(srcs)
    lands = [lax.empty((N_DEV,) + s.shape[-2:], s.dtype) for s in srcs]

    def body(*refs):
        src_refs, land_refs = refs[:n], refs[n:2 * n]
        send_sems, recv_sems = refs[2 * n], refs[2 * n + 1]
        token = refs[-1]
        x, y, c = lax.axis_index("x"), lax.axis_index("y"), lax.axis_index("c")
        mine = 4 * x + 2 * y + c
        for a in range(n):
            for k, peer in enumerate(_peers(x, y, c)):
                dest = 4 * peer[0] + 2 * peer[1] + peer[2]
                j = a * (N_DEV - 1) + k
                pltpu.make_async_remote_copy(
                    src_ref=src_refs[a] if gather else src_refs[a].at[dest], dst_ref=land_refs[a].at[mine],
                    send_sem=send_sems.at[j], recv_sem=recv_sems.at[j], device_id=peer, device_id_type=MESH).start()
        token[...] = jnp.zeros_like(token)

    sems = pltpu.SemaphoreType.DMA((n * (N_DEV - 1),))
    out = pl.pallas_call(
        body, name=name,
        out_shape=(sems, sems) + tuple(pltpu.HBM(a.shape, a.dtype) for a in list(srcs) + lands)
        + (jax.ShapeDtypeStruct((8, 128), F32),),
        in_specs=(_HBM,) * (2 * n), out_specs=(_SEM, _SEM) + (_HBM,) * (2 * n) + (pl.BlockSpec(memory_space=pltpu.VMEM),),
        input_output_aliases={i: 2 + i for i in range(2 * n)},
        compiler_params=pltpu.CompilerParams(has_side_effects=_EFFECT),
    )(*[pltpu.with_memory_space_constraint(a, pltpu.HBM) for a in list(srcs) + lands])
    return out[:-1], out[-1]


def _exchange_wait(state, after, *, gather, name):
    send_sems, recv_sems = state[0], state[1]
    n = (len(state) - 2) // 2
    arrays = state[2:]

    def body(*refs):
        src_refs, land_refs = refs[:n], refs[n:2 * n]
        send_sems, recv_sems = refs[2 * n], refs[2 * n + 1]
        x, y, c = lax.axis_index("x"), lax.axis_index("y"), lax.axis_index("c")
        for a in range(n):
            for k, peer in enumerate(_peers(x, y, c)):
                other = 4 * peer[0] + 2 * peer[1] + peer[2]
                j = a * (N_DEV - 1) + k
                copy = pltpu.make_async_remote_copy(
                    src_ref=src_refs[a] if gather else src_refs[a].at[other], dst_ref=land_refs[a].at[other],
                    send_sem=send_sems.at[j], recv_sem=recv_sems.at[j], device_id=peer, device_id_type=MESH)
                copy.wait_send()
                copy.wait_recv()

    out = pl.pallas_call(
        body, name=name,
        out_shape=tuple(pltpu.HBM(a.shape, a.dtype) for a in arrays),
        in_specs=(_HBM,) * (2 * n) + (_SEM, _SEM, pl.BlockSpec(memory_space=pl.ANY)), out_specs=(_HBM,) * (2 * n),
        input_output_aliases={i: i for i in range(2 * n)},
        compiler_params=pltpu.CompilerParams(has_side_effects=_EFFECT),
    )(*arrays, send_sems, recv_sems, after)
    return out[n:]


def _fill_own(got, own):
    mine = 4 * lax.axis_index("x") + 2 * lax.axis_index("y") + lax.axis_index("c")
    return lax.dynamic_update_slice(got, own[None], (mine, 0, 0))


def _adam_math(w, g, m, v):
    m = ADAM_B1 * m + (1.0 - ADAM_B1) * g
    v = ADAM_B2 * v + (1.0 - ADAM_B2) * (g * g)
    m_hat = m / (1.0 - ADAM_B1 ** ADAM_STEP)
    v_hat = v / (1.0 - ADAM_B2 ** ADAM_STEP)
    delta = -ADAM_LR * (m_hat / (jnp.sqrt(v_hat) + ADAM_EPS) + ADAM_WD * w)
    return delta, m, v


def _sum_parts(parts, *, name):
    _, R, C = parts.shape
    tr = R // 2
    assert tr % 16 == 0

    def body(p_ref, g_ref):
        g = p_ref[0].astype(F32)
        for s in range(1, N_DEV):
            g = g + p_ref[s].astype(F32)
        g_ref[...] = g

    return pl.pallas_call(
        body, name=name, grid=(R // tr,),
        in_specs=[pl.BlockSpec((N_DEV, tr, C), lambda i: (0, i, 0))], out_specs=pl.BlockSpec((tr, C), lambda i: (i, 0)),
        out_shape=jax.ShapeDtypeStruct((R, C), F32), compiler_params=_params(("arbitrary",)),
    )(parts)


def _adamw(parts, w, m, v, *, name):
    R, C = w.shape
    n_parts = parts.shape[0]
    tr = R // 2
    assert tr % 16 == 0

    def body(p_ref, w_ref, m_ref, v_ref, g_ref, d_ref, nm_ref, nv_ref):
        g = p_ref[0].astype(F32)
        for s in range(1, n_parts):
            g = g + p_ref[s].astype(F32)
        d, nm, nv = _adam_math(w_ref[...], g, m_ref[...], v_ref[...])
        g_ref[...] = g
        d_ref[...] = d
        nm_ref[...] = nm
        nv_ref[...] = nv

    blk = pl.BlockSpec((tr, C), lambda i: (i, 0))
    return pl.pallas_call(
        body, name=name, grid=(R // tr,),
        in_specs=[pl.BlockSpec((n_parts, tr, C), lambda i: (0, i, 0)), blk, blk, blk],
        out_specs=[blk] * 4, out_shape=[jax.ShapeDtypeStruct((R, C), F32)] * 4,
        compiler_params=_params(("arbitrary",)),
    )(parts, w, m, v)


def _adamw_small(parts, w, m, v):
    def body(p_ref, w_ref, m_ref, v_ref, g_ref, d_ref, nm_ref, nv_ref):
        g = p_ref[0]
        for s in range(1, N_DEV):
            g = g + p_ref[s]
        d, nm, nv = _adam_math(w_ref[...], g, m_ref[...], v_ref[...])
        g_ref[...] = g
        d_ref[...] = d
        nm_ref[...] = nm
        nv_ref[...] = nv

    vm = pl.BlockSpec(memory_space=pltpu.VMEM)
    return pl.pallas_call(
        body, name="adamw_small", in_specs=[vm] * 4, out_specs=[vm] * 4,
        out_shape=[jax.ShapeDtypeStruct((SMALL_ROWS, 128), F32)] * 4, compiler_params=_params(),
    )(parts, w, m, v)


def _t5_bucket(dist):
    max_exact = REL_BUCKETS // 2
    df = jnp.maximum(dist, 1).astype(F32)
    large = max_exact + (jnp.log(df / max_exact) / math.log(REL_MAX_DISTANCE / max_exact)
                         * (REL_BUCKETS - max_exact)).astype(jnp.int32)
    large = jnp.minimum(large, REL_BUCKETS - 1)
    return jnp.where(dist < max_exact, dist, large)


def _band_tables(rel_table, dil, n_back):
    qi = jnp.arange(BLK)[:, None]
    kj = jnp.arange(2 * BLK)[None, :]
    delta = BLK + qi - kj
    in_band = (delta >= 0) & (delta <= n_back)
    if rel_table is None:
        vals = jnp.zeros((N_HEADS, BLK, 2 * BLK), F32)
        bmap = None
    else:
        bucket = _t5_bucket(jnp.clip(delta, 0, n_back) * dil)
        vals = jnp.zeros((N_HEADS, BLK, 2 * BLK), F32)
        for b in range(REL_BUCKETS):
            vals = jnp.where((bucket == b)[None], rel_table[b][:, None, None], vals)
        bmap = jnp.where(in_band, bucket, -1).astype(jnp.int32)
    later = jnp.where(in_band[None], vals, NEG)
    first = jnp.where((in_band & (kj >= BLK))[None], vals, NEG)
    return jnp.stack([later, first]), bmap


def _rope_tables(T):
    half = HEAD_DIM // 2
    inv_freq = ROPE_THETA ** (-jnp.arange(half, dtype=F32) / half)
    ang = jnp.arange(T, dtype=F32)[:, None] * inv_freq[None, :]
    cos, sin = jnp.cos(ang), jnp.sin(ang)
    return jnp.tile(cos, (1, 4)), jnp.tile(jnp.concatenate([-sin, sin], axis=1), (1, 2))


def _widen_in(a, axis):
    sl = lambda lo, hi: lax.slice_in_dim(a, lo, hi, axis=axis)
    dup = lambda lo: [sl(lo, lo + 64), sl(lo, lo + 64), sl(lo + 64, lo + 128), sl(lo + 64, lo + 128)]
    return jnp.concatenate([sl(0, 512)] + dup(512) + dup(640) + [sl(768, D_IN)], axis=axis)


def _fold_in(a, axis):
    sl = lambda lo, hi: lax.slice_in_dim(a, lo, hi, axis=axis)
    fold = lambda lo: [sl(lo, lo + 64) + sl(lo + 64, lo + 128), sl(lo + 128, lo + 192) + sl(lo + 192, lo + 256)]
    return jnp.concatenate([sl(0, 512)] + fold(512) + fold(768) + [sl(1024, D_INP)], axis=axis)


def _local_step(x, tgt, g_attn, wint, b_in, sinks, rel_table, g_out_a, g_out_b, g_ffn, g_final, token, rest_fn, early_fn):
    T = x.shape[0]
    cos, sin = _rope_tables(T)
    cos = cos + token[0, 0]
    winp = _widen_in(wint, 0)
    binp = _widen_in(b_in, 1)
    g_final2 = g_final.reshape(1, D_MODEL)
    sink8 = sinks.reshape(N_HEADS)

    bias_a, _ = _band_tables(None, 1, BLK - 1)
    tabs = [_band_tables(rel_table, dil, window // dil) for window, dil in BRANCHES]

    h1, qa, ka, va, *qkv_b = _norm_proj(x, g_attn, winp, binp, cos, sin)
    qbs, kbs, vbs = qkv_b[0:3], qkv_b[3:6], qkv_b[6:9]
    oa, lse_a = _attn_fwd(qa, ka, va, bias_a, sink8, dil=1, kv_pairs=2, use_sink=True, name="attn_a_fwd")
    outs = [_attn_fwd(qbs[n], kbs[n], vbs[n], tabs[n][0], sink8, dil=dil, kv_pairs=4, use_sink=False,
                      name=f"attn_b{n}_fwd") for n, (_, dil) in enumerate(BRANCHES)]
    wo, wgt, wut, wd = rest_fn(outs[2][1])
    x2, mixed, h2, *ob_lse = _merge_wo(x, oa, outs[0][0], outs[1][0], outs[2][0], outs[0][1], outs[1][1], outs[2][1],
                                       g_out_a, g_out_b, wo, g_ffn)
    obs, lses = ob_lse[0:3], ob_lse[3:6]
    gate, up, act = _ffn_up(h2, wgt, wut)
    dx3, dx3b, loss, dg_final = _ffn_down_loss(act, wd, x2, tgt, g_final2)

    dgate, dup = _ffn_bwd_act(dx3b, gate, up, wd)
    dx2, dx2b, dg_ffn = _ffn_bwd_in(dgate, dup, wgt, wut, x2, dx3, g_ffn)
    dwd = _matmul_tn(act, dx3b, tk=1408, tn=1024, name="dw_down")
    dwgt = _matmul_tn(dgate, h2, tk=1408, tn=1024, name="dw_gate")
    dwut = _matmul_tn(dup, h2, tk=1408, tn=1024, name="dw_up")
    dwo = _matmul_tn(mixed, dx2b, tk=1024, tn=1024, name="dw_o")
    early, token2 = early_fn(dict(w_o=dwo, w_gate=dwgt, w_up=dwut, w_down=dwd))
    doa, *dobs, dg_out_a, dg_out_b = _wo_bwd(dx2b, wo, oa, obs[0], g_out_a + token2[0, 0], g_out_b)

    dqa, dka, dva, _, dsk = _attn_bwd(qa, ka, va, oa, doa, lse_a, bias_a, sink8, dil=1, kv_pairs=2, use_sink=True,
                                      name="attn_a_bwd")
    res = [_attn_bwd(qbs[n], kbs[n], vbs[n], obs[n], dobs[n], lses[n], tabs[n][0], sink8, dil=dil, kv_pairs=4,
                     use_sink=False, name=f"attn_b{n}_bwd") for n, (_, dil) in enumerate(BRANCHES)]
    dp, dbp = _dproj(dqa, dka, dva, [r[0] for r in res], [r[1] for r in res], [r[2] for r in res], cos, sin)
    grad_x, dg_attn = _inproj_bwd(dp, winp, x, dx2, g_attn)
    dwin = _fold_in(_matmul_tn(dp, h1, tk=1280, tn=1024, out_dtype=F32, name="dw_in"), 0)
    drel, dsink = _bias_sink_grads([r[3] for r in res], [t[1] for t in tabs], dsk)

    small = dict(
        g_attn=dg_attn, b_in=_fold_in(dbp, 1), sinks=dsink[:, ::HEAD_DIM], rel_table=drel[:, :REL_BUCKETS].T,
        g_out_a=dg_out_a, g_out_b=dg_out_b, g_ffn=dg_ffn, g_final=dg_final.reshape(D_MODEL))
    return loss[0, 0], grad_x, dwin, early, small


SMALL_NAMES = ("g_attn", "b_in", "sinks", "rel_table", "g_out_a", "g_out_b", "g_ffn", "g_final")


def _pack_small(vals):
    flat = jnp.concatenate([vals[n].reshape(-1).astype(F32) for n in SMALL_NAMES])
    return jnp.pad(flat, (0, SMALL_ROWS * 128 - flat.shape[0])).reshape(SMALL_ROWS, 128)


def _unpack_small(packed, like):
    flat = packed.reshape(-1)
    out, off = {}, 0
    for n in SMALL_NAMES:
        size = like[n].size
        out[n] = flat[off:off + size].reshape(like[n].shape)
        off += size
    return out


def kernel(x, g_attn, w_in, b_in, sinks, rel_table, g_out_a, g_out_b, w_o, g_ffn, w_gate, w_up, w_down, g_final, loss_target, m_g_attn, m_w_in, m_b_in, m_sinks, m_rel_table, m_g_out_a, m_g_out_b, m_w_o, m_g_ffn, m_w_gate, m_w_up, m_w_down, m_g_final, v_g_attn, v_w_in, v_b_in, v_sinks, v_rel_table, v_g_out_a, v_g_out_b, v_w_o, v_g_ffn, v_w_gate, v_w_up, v_w_down, v_g_final):
    mine = 4 * lax.axis_index("x") + 2 * lax.axis_index("y") + lax.axis_index("c")
    rest_names = ("w_o", "w_gate", "w_up", "w_down")

    rest = [w_o[0].astype(BF16), w_gate[0].astype(BF16).T, w_up[0].astype(BF16).T, w_down[0].astype(BF16)]
    wint = _all_gather(w_in[0].astype(BF16).T, name="gather_w_in").reshape(D_IN, D_MODEL)
    wint, rest = lax.optimization_barrier((wint, rest))
    rest_state, token = _exchange_start(rest, gather=True, name="gather_rest_start")

    def rest_fn(after):
        got = _exchange_wait(rest_state, after, gather=True, name="gather_rest_wait")
        return [_fill_own(g, own).reshape(N_DEV * own.shape[0], D_MODEL) for g, own in zip(got, rest)]

    def early_fn(dws):
        parts = [dws[n].reshape(N_DEV, -1, D_MODEL) for n in rest_names]
        own = [lax.dynamic_index_in_dim(p, mine, 0, keepdims=False) for p in parts]
        state, token2 = _exchange_start(parts, gather=False, name="scatter_rest_start")
        return (state, own), token2

    loss_part, grad_x, dwint, (early_state, early_own), small = _local_step(
        x[0], loss_target[0], g_attn, wint, b_in, sinks, rel_table, g_out_a, g_out_b, g_ffn, g_final, token,
        rest_fn, early_fn)
    loss = lax.psum(loss_part, ("x", "y", "c"))

    parts_in = dwint.astype(BF16).reshape(N_DEV, D_IN // N_DEV, D_MODEL)
    own_in = lax.dynamic_index_in_dim(parts_in, mine, 0, keepdims=False)
    in_state, token3 = _exchange_start([parts_in], gather=False, name="scatter_w_in_start")
    got = [_fill_own(g, own) for g, own in
           zip(_exchange_wait(early_state, token3, gather=False, name="scatter_rest_wait"), early_own)]

    def update(n, parts, w, m, v, transposed):
        if transposed:
            parts = _sum_parts(parts, name="sum_" + n).T[None]
        return [a[None] for a in _adamw(parts, w[0], m[0], v[0], name="adamw_" + n)]

    big = dict(w_o=update("w_o", got[0], w_o, m_w_o, v_w_o, False),
               w_gate=update("w_gate", got[1], w_gate, m_w_gate, v_w_gate, True),
               w_up=update("w_up", got[2], w_up, m_w_up, v_w_up, True),
               w_down=update("w_down", got[3], w_down, m_w_down, v_w_down, False))

    ws = dict(g_attn=g_attn, b_in=b_in, sinks=sinks, rel_table=rel_table, g_out_a=g_out_a, g_out_b=g_out_b,
              g_ffn=g_ffn, g_final=g_final)
    ms = dict(g_attn=m_g_attn, b_in=m_b_in, sinks=m_sinks, rel_table=m_rel_table, g_out_a=m_g_out_a,
              g_out_b=m_g_out_b, g_ffn=m_g_ffn, g_final=m_g_final)
    vs = dict(g_attn=v_g_attn, b_in=v_b_in, sinks=v_sinks, rel_table=v_rel_table, g_out_a=v_g_out_a,
              g_out_b=v_g_out_b, g_ffn=v_g_ffn, g_final=v_g_final)
    sparts = _all_gather(_pack_small(small), name="gather_small")
    sm_packed = _adamw_small(sparts, _pack_small(ws), _pack_small(ms), _pack_small(vs))
    sm = [_unpack_small(a, ws) for a in sm_packed]

    done = sm_packed[1][:1, :1] + sum(big[n][1][0, :1, :1] for n in rest_names)
    got_in = _fill_own(_exchange_wait(in_state, done, gather=False, name="scatter_w_in_wait")[0], own_in)
    big["w_in"] = update("w_in", got_in, w_in, m_w_in, v_w_in, True)

    order = ("g_attn", "w_in", "b_in", "sinks", "rel_table", "g_out_a", "g_out_b", "w_o", "g_ffn", "w_gate", "w_up",
             "w_down", "g_final")
    outs = [loss, grad_x[None]]
    for k in range(4):
        outs += [big[n][k] if n in big else sm[k][n] for n in order]
    return tuple(outs)
```

```python
import functools
import math

import jax
import jax.numpy as jnp
from jax import lax
from jax.experimental import pallas as pl
from jax.experimental.pallas import tpu as pltpu

F32 = jnp.float32
BF16 = jnp.bfloat16

N_DEV = 8
D_MODEL = 1024
HEAD_DIM = 64
N_HEADS = 8
PAIR = 2 * HEAD_DIM
WIDTH = N_HEADS * HEAD_DIM
D_IN = 2304
D_INP = 2560
D_FF = 2816
BLK = 128
ROPE_THETA = 150000.0
REL_BUCKETS = 32
REL_MAX_DISTANCE = 2048
EPS = 1e-5
NEG = -1e30
BRANCHES = ((128, 1), (512, 4), (2048, 16))
Q_SCALE = HEAD_DIM ** -0.5

ADAM_LR = 0.001
ADAM_B1 = 0.9
ADAM_B2 = 0.999
ADAM_EPS = 1e-08
ADAM_WD = 0.01
ADAM_STEP = 10

VMEM_LIMIT = 56 * 1024 * 1024
MESH = pl.DeviceIdType.MESH

NT = (((1,), (1,)), ((), ()))
TN = (((0,), (0,)), ((), ()))

SMALL_ROWS = 56


def _params(sem=None):
    return pltpu.CompilerParams(dimension_semantics=sem, vmem_limit_bytes=VMEM_LIMIT)


def _rms_bwd(dh, xh, r, g):
    u = dh * g
    return r * (u - xh * jnp.mean(u * xh, axis=-1, keepdims=True))


def _rope_rot(t, first):
    return jnp.where(first, pltpu.roll(t, 96, 1), pltpu.roll(t, 32, 1))


N_CHUNK = WIDTH // PAIR


def _scr(tm):
    return pltpu.VMEM((N_CHUNK, tm, PAIR), F32)


def _scr_get(scr):
    return jnp.concatenate([scr[j] for j in range(N_CHUNK)], axis=1)


def _scr_put(scr, val):
    for j in range(N_CHUNK):
        scr[j] = val[:, j * PAIR:(j + 1) * PAIR]


def _unstride(view_ref, scr, dil, tm):
    n = tm // dil
    for r in range(dil):
        for j in range(N_CHUNK):
            col = r * WIDTH + j * PAIR
            scr.at[j][pl.ds(r, n, stride=dil), :] = view_ref[:, col:col + PAIR].astype(F32)


def _restride(scr, out_ref, dil, tm):
    n = tm // dil
    for r in range(dil):
        for j in range(N_CHUNK):
            col = r * WIDTH + j * PAIR
            rows = scr[j] if dil == 1 else scr.at[j][pl.ds(r, n, stride=dil), :]
            out_ref[:, col:col + PAIR] = rows.astype(out_ref.dtype)


def _view_specs(tm):
    return [pl.BlockSpec((tm // dil, dil * WIDTH), lambda i: (i, 0)) for _, dil in BRANCHES]


def _view_shapes(T, dtype):
    return [jax.ShapeDtypeStruct((T // dil, dil * WIDTH), dtype) for _, dil in BRANCHES]


def _norm_proj(x, g, w, b, cos, sin, *, tm=512):
    T = x.shape[0]

    def body(x_ref, g_ref, w_ref, b_ref, cos_ref, sin_ref, h_ref, qa_ref, ka_ref, va_ref, *rest):
        outs_b, ys = rest[:9], rest[9]
        xv = x_ref[...]
        r = lax.rsqrt(jnp.mean(xv * xv, axis=-1, keepdims=True) + EPS)
        h = (xv * r * g_ref[...]).astype(BF16)
        h_ref[...] = h
        cosv = cos_ref[...]
        sinv = sin_ref[...]
        lane = lax.broadcasted_iota(jnp.int32, (tm, PAIR), 1)
        first = (lane % HEAD_DIM) < (HEAD_DIM // 2)

        def proj(off):
            return (lax.dot_general(h, w_ref[off:off + 256, :], NT, preferred_element_type=F32)
                    + b_ref[:, off:off + 256])

        for (off, width, rot, scale), o_ref in zip(((0, 512, True, Q_SCALE), (512, 256, True, 1.0), (768, 256, False, 1.0)),
                                                   (qa_ref, ka_ref, va_ref)):
            for c in range(0, width, 256):
                y = proj(off + c)
                for j in range(0, 256, PAIR):
                    t = y[:, j:j + PAIR]
                    if rot:
                        t = t * cosv + _rope_rot(t, first) * sinv
                    if scale != 1.0:
                        t = t * scale
                    o_ref[:, c + j:c + j + PAIR] = t.astype(BF16)
        for n, (off, scale) in enumerate(((1024, Q_SCALE), (1536, 1.0), (2048, 1.0))):
            for c in range(0, WIDTH, 256):
                y = proj(off + c)
                y = y * scale if scale != 1.0 else y
                for j in range(0, 256, PAIR):
                    ys[(c + j) // PAIR] = y[:, j:j + PAIR]
            for (_, dil), o_ref in zip(BRANCHES, outs_b[3 * n:3 * n + 3]):
                _restride(ys, o_ref, dil, tm)

    row = lambda w_: pl.BlockSpec((tm, w_), lambda i: (i, 0))
    full = lambda a: pl.BlockSpec(a.shape, lambda i: (0, 0))
    return pl.pallas_call(
        body, name="norm_proj", grid=(T // tm,),
        in_specs=[row(D_MODEL), full(g), full(w), full(b), row(PAIR), row(PAIR)],
        out_specs=[row(D_MODEL), row(512), row(256), row(256)] + _view_specs(tm) * 3,
        out_shape=[jax.ShapeDtypeStruct((T, n), BF16) for n in (D_MODEL, 512, 256, 256)] + _view_shapes(T, BF16) * 3,
        scratch_shapes=[_scr(tm)],
        compiler_params=_params(("arbitrary",)),
    )(x, g, w, b, cos, sin)


SUB = 4


def _attn_specs(kvw, ns, clamp):
    qi = (lambda r, i: (jnp.minimum(i, ns - 1), r)) if clamp else (lambda r, i: (i, r))
    q_spec = pl.BlockSpec((SUB * BLK, WIDTH), qi)
    kc_spec = pl.BlockSpec((SUB * BLK, kvw), qi)
    kp_spec = pl.BlockSpec((BLK, kvw), lambda r, i: (jnp.maximum(SUB * jnp.minimum(i, ns - 1) - 1, 0), r))
    b_spec = pl.BlockSpec((2, N_HEADS, BLK, 2 * BLK), lambda r, i: (0, 0, 0, 0))
    return q_spec, kp_spec, kc_spec, b_spec


def _window(prev_ref, cur_ref, j, ksl):
    before = prev_ref[:, ksl] if j == 0 else cur_ref[(j - 1) * BLK:j * BLK, ksl]
    return jnp.concatenate([before, cur_ref[j * BLK:(j + 1) * BLK, ksl]], axis=0)


def _attn_fwd(q, k, v, bias, sinks, *, dil, kv_pairs, use_sink, name):
    L = q.shape[0]
    ns = L // (SUB * BLK)
    kvw = kv_pairs * PAIR
    rep = 4 // kv_pairs

    def body(sink_ref, q_ref, kp_ref, kc_ref, vp_ref, vc_ref, b_ref, o_ref, lse_ref):
        lane = lax.broadcasted_iota(jnp.int32, (1, PAIR), 1)
        lo = lane < HEAD_DIM
        first = jnp.where(pl.program_id(1) == 0, 1, 0)
        for j in range(SUB):
            rows = slice(j * BLK, (j + 1) * BLK)
            for hp in range(4):
                sl = slice(hp * PAIR, (hp + 1) * PAIR)
                ksl = slice((hp // rep) * PAIR, (hp // rep + 1) * PAIR)
                qp = q_ref[rows, sl]
                kk = _window(kp_ref, kc_ref, j, ksl)
                vv = _window(vp_ref, vc_ref, j, ksl)
                o_pair = None
                lse_pair = None
                for e in range(2):
                    h = 2 * hp + e
                    msk = lo if e == 0 else jnp.logical_not(lo)
                    qm = jnp.where(msk, qp, jnp.zeros_like(qp))
                    s = lax.dot_general(qm, kk, NT, preferred_element_type=F32) + (b_ref[first, h] if j == 0 else b_ref[0, h])
                    m = jnp.max(s, axis=-1, keepdims=True)
                    if use_sink:
                        sk = sink_ref[h]
                        m = jnp.maximum(m, sk)
                    p = jnp.exp(s - m)
                    l = jnp.sum(p, axis=-1, keepdims=True)
                    if use_sink:
                        l = l + jnp.exp(sk - m)
                    vm = jnp.where(msk, vv, jnp.zeros_like(vv))
                    oe = jnp.dot(p.astype(BF16), vm, preferred_element_type=F32) * (1.0 / l)
                    ls = m + jnp.log(l)
                    if e == 0:
                        o_pair = oe
                        lse_pair = jnp.broadcast_to(ls, (BLK, PAIR))
                    else:
                        o_pair = o_pair + oe
                        lse_pair = jnp.where(lo, lse_pair, ls)
                o_ref[rows, sl] = o_pair.astype(BF16)
                lse_ref[rows, sl] = lse_pair

    q_spec, kp_spec, kc_spec, b_spec = _attn_specs(kvw, ns, False)
    return pl.pallas_call(
        body, name=name, grid=(dil, ns),
        in_specs=[pl.BlockSpec(memory_space=pltpu.SMEM), q_spec, kp_spec, kc_spec, kp_spec, kc_spec, b_spec],
        out_specs=[q_spec, q_spec],
        out_shape=[jax.ShapeDtypeStruct((L, dil * WIDTH), BF16), jax.ShapeDtypeStruct((L, dil * WIDTH), F32)],
        compiler_params=_params(("arbitrary", "arbitrary")),
    )(sinks, q, k, k, v, v, bias)


def _attn_bwd(q, k, v, o, do, lse, bias, sinks, *, dil, kv_pairs, use_sink, name):
    L = q.shape[0]
    ns = L // (SUB * BLK)
    kvw = kv_pairs * PAIR
    rep = 4 // kv_pairs
    last = slice((SUB - 1) * BLK, SUB * BLK)

    def body(sink_ref, q_ref, kp_ref, kc_ref, vp_ref, vc_ref, o_ref, do_ref, lse_ref, b_ref,
             dq_ref, dk_ref, dv_ref, dsum_ref, dsk_ref, pk_ref, pv_ref):
        r = pl.program_id(0)
        i = pl.program_id(1)

        @pl.when((r == 0) & (i == 0))
        def _():
            dsum_ref[...] = jnp.zeros_like(dsum_ref)
            dsk_ref[...] = jnp.zeros_like(dsk_ref)

        @pl.when(i == 0)
        def _():
            pk_ref[...] = jnp.zeros_like(pk_ref)
            pv_ref[...] = jnp.zeros_like(pv_ref)

        @pl.when(i < ns)
        def _():
            lo = lax.broadcasted_iota(jnp.int32, (1, PAIR), 1) < HEAD_DIM
            first = jnp.where(i == 0, 1, 0)
            dks = [[None] * kv_pairs for _ in range(SUB)]
            dvs = [[None] * kv_pairs for _ in range(SUB)]
            for j in range(SUB):
                rows = slice(j * BLK, (j + 1) * BLK)
                for hp in range(4):
                    kvp = hp // rep
                    sl = slice(hp * PAIR, (hp + 1) * PAIR)
                    ksl = slice(kvp * PAIR, (kvp + 1) * PAIR)
                    qp = q_ref[rows, sl]
                    dop = do_ref[rows, sl]
                    prod = dop.astype(F32) * o_ref[rows, sl].astype(F32)
                    kk = _window(kp_ref, kc_ref, j, ksl)
                    vv = _window(vp_ref, vc_ref, j, ksl)
                    dq_pair = None
                    c_pair = None
                    qms, doms, dsbs, pbs = [], [], [], []
                    for e in range(2):
                        h = 2 * hp + e
                        msk = lo if e == 0 else jnp.logical_not(lo)
                        qm = jnp.where(msk, qp, jnp.zeros_like(qp))
                        dom = jnp.where(msk, dop, jnp.zeros_like(dop))
                        km = jnp.where(msk, kk, jnp.zeros_like(kk))
                        s = (lax.dot_general(qm, kk, NT, preferred_element_type=F32)
                             + (b_ref[first, h] if j == 0 else b_ref[0, h]))
                        ls = lse_ref[rows, h * HEAD_DIM:h * HEAD_DIM + 1]
                        p = jnp.exp(s - ls)
                        dp = lax.dot_general(dom, vv, NT, preferred_element_type=F32)
                        delta = jnp.sum(jnp.where(msk, prod, 0.0), axis=-1, keepdims=True)
                        ds = p * (dp - delta)
                        if use_sink:
                            ce = jnp.exp(sink_ref[h] - ls) * delta
                            c_pair = jnp.broadcast_to(ce, (BLK, PAIR)) if e == 0 else jnp.where(msk, ce, c_pair)
                        else:
                            dsum_ref[h] += ds
                        dsb = ds.astype(BF16)
                        dqe = jnp.dot(dsb, km, preferred_element_type=F32)
                        dq_pair = dqe if e == 0 else dq_pair + dqe
                        qms.append(qm)
                        doms.append(dom)
                        dsbs.append(dsb)
                        pbs.append(p.astype(BF16))
                    dke = lax.dot_general(jnp.concatenate(dsbs, axis=0), jnp.concatenate(qms, axis=0), TN,
                                          preferred_element_type=F32)
                    dve = lax.dot_general(jnp.concatenate(pbs, axis=0), jnp.concatenate(doms, axis=0), TN,
                                          preferred_element_type=F32)
                    dks[j][kvp] = dke if dks[j][kvp] is None else dks[j][kvp] + dke
                    dvs[j][kvp] = dve if dvs[j][kvp] is None else dvs[j][kvp] + dve
                    dq_ref[rows, sl] = (dq_pair * Q_SCALE).astype(BF16)
                    if use_sink:
                        dsk_ref[:, sl] += c_pair
            for kvp in range(kv_pairs):
                ksl = slice(kvp * PAIR, (kvp + 1) * PAIR)
                for pend_ref, out_ref, parts in ((pk_ref, dk_ref, [d[kvp] for d in dks]),
                                                 (pv_ref, dv_ref, [d[kvp] for d in dvs])):
                    if SUB > 1:
                        out_ref[:(SUB - 1) * BLK, ksl] = pend_ref[:(SUB - 1) * BLK, ksl].astype(BF16)
                    out_ref[last, ksl] = (pend_ref[last, ksl] + parts[0][:BLK]).astype(BF16)
                    for j in range(SUB):
                        own = parts[j][BLK:]
                        pend_ref[j * BLK:(j + 1) * BLK, ksl] = own + parts[j + 1][:BLK] if j + 1 < SUB else own

        @pl.when(i == ns)
        def _():
            dk_ref[...] = pk_ref[...].astype(BF16)
            dv_ref[...] = pv_ref[...].astype(BF16)

    q_spec, kp_spec, kc_spec, b_spec = _attn_specs(kvw, ns, True)
    dkv_spec = pl.BlockSpec((SUB * BLK, kvw), lambda r, i: (jnp.maximum(i - 1, 0), r))
    return pl.pallas_call(
        body, name=name, grid=(dil, ns + 1),
        in_specs=[pl.BlockSpec(memory_space=pltpu.SMEM), q_spec, kp_spec, kc_spec, kp_spec, kc_spec,
                  q_spec, q_spec, q_spec, b_spec],
        out_specs=[q_spec, dkv_spec, dkv_spec,
                   pl.BlockSpec((N_HEADS, BLK, 2 * BLK), lambda r, i: (0, 0, 0)),
                   pl.BlockSpec((BLK, WIDTH), lambda r, i: (0, 0))],
        out_shape=[jax.ShapeDtypeStruct((L, dil * WIDTH), BF16),
                   jax.ShapeDtypeStruct((L, dil * kvw), BF16),
                   jax.ShapeDtypeStruct((L, dil * kvw), BF16),
                   jax.ShapeDtypeStruct((N_HEADS, BLK, 2 * BLK), F32),
                   jax.ShapeDtypeStruct((BLK, WIDTH), F32)],
        scratch_shapes=[pltpu.VMEM((SUB * BLK, kvw), F32), pltpu.VMEM((SUB * BLK, kvw), F32)],
        compiler_params=_params(("arbitrary", "arbitrary")),
    )(sinks, q, k, k, v, v, o, do, lse, bias)


def _merge_wo(x, oa, o1, o2, o3, l1, l2, l3, ga, gb, wo, gf, *, tm=512):
    T = x.shape[0]

    def body(x_ref, oa_ref, o1_ref, o2_ref, o3_ref, l1_ref, l2_ref, l3_ref, ga_ref, gb_ref, wo_ref, gf_ref,
             x2_ref, mix_ref, h2_ref, ob1_ref, ob4_ref, ob16_ref, ls1_ref, ls4_ref, ls16_ref, so2, so3, sl2, sl3):
        _unstride(o2_ref, so2, BRANCHES[1][1], tm)
        _unstride(o3_ref, so3, BRANCHES[2][1], tm)
        _unstride(l2_ref, sl2, BRANCHES[1][1], tm)
        _unstride(l3_ref, sl3, BRANCHES[2][1], tm)
        la, lb, lc = l1_ref[...], _scr_get(sl2), _scr_get(sl3)
        m = jnp.maximum(jnp.maximum(la, lb), lc)
        ea, eb, ec = jnp.exp(la - m), jnp.exp(lb - m), jnp.exp(lc - m)
        den = ea + eb + ec
        inv = 1.0 / den
        ob = (ea * o1_ref[...].astype(F32) + eb * _scr_get(so2) + ec * _scr_get(so3)) * inv
        _scr_put(so2, ob)
        _scr_put(sl2, m + jnp.log(den))
        for (_, dil), o_ref, l_ref in zip(BRANCHES, (ob1_ref, ob4_ref, ob16_ref), (ls1_ref, ls4_ref, ls16_ref)):
            _restride(so2, o_ref, dil, tm)
            _restride(sl2, l_ref, dil, tm)
        oav = oa_ref[...].astype(F32)
        ra = lax.rsqrt(jnp.mean(oav * oav, axis=-1, keepdims=True) + EPS)
        rb = lax.rsqrt(jnp.mean(ob * ob, axis=-1, keepdims=True) + EPS)
        mix_ref[:, :WIDTH] = (oav * ra * ga_ref[...]).astype(BF16)
        mix_ref[:, WIDTH:] = (ob * rb * gb_ref[...]).astype(BF16)
        x2 = x_ref[...] + jnp.dot(mix_ref[...], wo_ref[...], preferred_element_type=F32)
        x2_ref[...] = x2
        r2 = lax.rsqrt(jnp.mean(x2 * x2, axis=-1, keepdims=True) + EPS)
        h2_ref[...] = (x2 * r2 * gf_ref[...]).astype(BF16)

    row = lambda w_: pl.BlockSpec((tm, w_), lambda i: (i, 0))
    full = lambda a: pl.BlockSpec(a.shape, lambda i: (0, 0))
    return pl.pallas_call(
        body, name="merge_wo", grid=(T // tm,),
        in_specs=[row(D_MODEL), row(WIDTH)] + _view_specs(tm) * 2 + [full(ga), full(gb), full(wo), full(gf)],
        out_specs=[row(D_MODEL), row(D_MODEL), row(D_MODEL)] + _view_specs(tm) * 2,
        out_shape=[jax.ShapeDtypeStruct((T, D_MODEL), F32), jax.ShapeDtypeStruct((T, D_MODEL), BF16),
                   jax.ShapeDtypeStruct((T, D_MODEL), BF16)] + _view_shapes(T, BF16) + _view_shapes(T, F32),
        scratch_shapes=[_scr(tm)] * 4,
        compiler_params=_params(("arbitrary",)),
    )(x, oa, o1, o2, o3, l1, l2, l3, ga, gb, wo, gf)


def _ffn_up(h2, wgt, wut, *, tm=1024, fc=1408, rc=256):
    T = h2.shape[0]

    def body(h_ref, wg_ref, wu_ref, dg_ref, du_ref, act_ref):
        for s in range(0, tm, rc):
            h = h_ref[s:s + rc, :]
            gt = lax.dot_general(h, wg_ref[...], NT, preferred_element_type=F32)
            u = lax.dot_general(h, wu_ref[...], NT, preferred_element_type=F32)
            sg = 1.0 / (1.0 + jnp.exp(-gt))
            silu = gt * sg
            dg_ref[s:s + rc, :] = (u * sg * (1.0 + gt * (1.0 - sg))).astype(BF16)
            du_ref[s:s + rc, :] = silu.astype(BF16)
            act_ref[s:s + rc, :] = (silu * u).astype(BF16)

    rowd = pl.BlockSpec((tm, D_MODEL), lambda i, c: (i, 0))
    wrow = pl.BlockSpec((fc, D_MODEL), lambda i, c: (c, 0))
    oc = pl.BlockSpec((tm, fc), lambda i, c: (i, c))
    return pl.pallas_call(
        body, name="ffn_up", grid=(T // tm, D_FF // fc),
        in_specs=[rowd, wrow, wrow],
        out_specs=[oc, oc, oc],
        out_shape=[jax.ShapeDtypeStruct((T, D_FF), BF16)] * 3,
        compiler_params=_params(("arbitrary", "arbitrary")),
    )(h2, wgt, wut)


def _ffn_down_loss(act, wd, x2, tgt, g, *, tm=512, rc=256):
    T = x2.shape[0]

    def body(act_ref, wd_ref, x2_ref, tgt_ref, g_ref, dx_ref, dxb_ref, loss_ref, dg_ref):
        @pl.when(pl.program_id(0) == 0)
        def _():
            loss_ref[...] = jnp.zeros_like(loss_ref)
            dg_ref[...] = jnp.zeros_like(dg_ref)

        gv = g_ref[...]
        lsum = jnp.zeros((1, 1), F32)
        dgs = jnp.zeros((1, D_MODEL), F32)
        for c in range(0, tm, rc):
            x3 = x2_ref[c:c + rc, :] + jnp.dot(act_ref[c:c + rc, :], wd_ref[...], preferred_element_type=F32)
            r = lax.rsqrt(jnp.mean(x3 * x3, axis=-1, keepdims=True) + EPS)
            xh = x3 * r
            diff = xh * gv - tgt_ref[c:c + rc, :]
            lsum = lsum + jnp.sum(jnp.sum(diff * diff, axis=-1, keepdims=True), axis=0, keepdims=True)
            dy = diff * (1.0 / D_MODEL)
            dgs = dgs + jnp.sum(dy * xh, axis=0, keepdims=True)
            dx = _rms_bwd(dy, xh, r, gv)
            dx_ref[c:c + rc, :] = dx
            dxb_ref[c:c + rc, :] = dx.astype(BF16)
        loss_ref[...] += lsum * (0.5 / D_MODEL)
        dg_ref[...] += dgs

    rowd = pl.BlockSpec((tm, D_MODEL), lambda i: (i, 0))
    return pl.pallas_call(
        body, name="ffn_down_loss", grid=(T // tm,),
        in_specs=[pl.BlockSpec((tm, D_FF), lambda i: (i, 0)), pl.BlockSpec((D_FF, D_MODEL), lambda i: (0, 0)),
                  rowd, rowd, pl.BlockSpec(g.shape, lambda i: (0, 0))],
        out_specs=[rowd, rowd, pl.BlockSpec((1, 1), lambda i: (0, 0)), pl.BlockSpec((1, D_MODEL), lambda i: (0, 0))],
        out_shape=[jax.ShapeDtypeStruct((T, D_MODEL), F32), jax.ShapeDtypeStruct((T, D_MODEL), BF16),
                   jax.ShapeDtypeStruct((1, 1), F32), jax.ShapeDtypeStruct((1, D_MODEL), F32)],
        compiler_params=_params(("arbitrary",)),
    )(act, wd, x2, tgt, g)


def _ffn_bwd_act(dx3b, act_dg, act_du, wd, *, tm=1024, fc=1408, rc=256):
    T = dx3b.shape[0]

    def body(dxb_ref, dg_ref, du_ref, wd_ref, dgate_ref, dup_ref):
        for s in range(0, tm, rc):
            dact = lax.dot_general(dxb_ref[s:s + rc, :], wd_ref[...], NT, preferred_element_type=F32)
            dgate_ref[s:s + rc, :] = (dact * dg_ref[s:s + rc, :].astype(F32)).astype(BF16)
            dup_ref[s:s + rc, :] = (dact * du_ref[s:s + rc, :].astype(F32)).astype(BF16)

    rowd = pl.BlockSpec((tm, D_MODEL), lambda i, c: (i, 0))
    oc = pl.BlockSpec((tm, fc), lambda i, c: (i, c))
    return pl.pallas_call(
        body, name="ffn_bwd_act", grid=(T // tm, D_FF // fc),
        in_specs=[rowd, oc, oc, pl.BlockSpec((fc, D_MODEL), lambda i, c: (c, 0))],
        out_specs=[oc, oc],
        out_shape=[jax.ShapeDtypeStruct((T, D_FF), BF16), jax.ShapeDtypeStruct((T, D_FF), BF16)],
        compiler_params=_params(("arbitrary", "arbitrary")),
    )(dx3b, act_dg, act_du, wd)


def _ffn_bwd_in(dgate, dup, wgt, wut, x2, dx3, g, *, tm=512, rc=256):
    T = x2.shape[0]

    def body(dgate_ref, dup_ref, wg_ref, wu_ref, x2_ref, dx_ref, g_ref, dx2_ref, dx2b_ref, dg_ref):
        @pl.when(pl.program_id(0) == 0)
        def _():
            dg_ref[...] = jnp.zeros_like(dg_ref)

        gv = g_ref[...]
        dgs = jnp.zeros((1, D_MODEL), F32)
        for s in range(0, tm, rc):
            dh = (jnp.dot(dgate_ref[s:s + rc, :], wg_ref[...], preferred_element_type=F32)
                  + jnp.dot(dup_ref[s:s + rc, :], wu_ref[...], preferred_element_type=F32))
            xv = x2_ref[s:s + rc, :]
            r = lax.rsqrt(jnp.mean(xv * xv, axis=-1, keepdims=True) + EPS)
            xh = xv * r
            dgs = dgs + jnp.sum(dh * xh, axis=0, keepdims=True)
            d = dx_ref[s:s + rc, :] + _rms_bwd(dh, xh, r, gv)
            dx2_ref[s:s + rc, :] = d
            dx2b_ref[s:s + rc, :] = d.astype(BF16)
        dg_ref[...] += dgs

    rowd = pl.BlockSpec((tm, D_MODEL), lambda i: (i, 0))
    rowf = pl.BlockSpec((tm, D_FF), lambda i: (i, 0))
    wfull = pl.BlockSpec((D_FF, D_MODEL), lambda i: (0, 0))
    return pl.pallas_call(
        body, name="ffn_bwd_in", grid=(T // tm,),
        in_specs=[rowf, rowf, wfull, wfull, rowd, rowd, pl.BlockSpec(g.shape, lambda i: (0, 0))],
        out_specs=[rowd, rowd, pl.BlockSpec((1, D_MODEL), lambda i: (0, 0))],
        out_shape=[jax.ShapeDtypeStruct((T, D_MODEL), F32), jax.ShapeDtypeStruct((T, D_MODEL), BF16),
                   jax.ShapeDtypeStruct((1, D_MODEL), F32)],
        compiler_params=_params(("arbitrary",)),
    )(dgate, dup, wgt, wut, x2, dx3, g)


def _matmul_tn(a, b, *, tk, tn, tt=2048, out_dtype=BF16, name):
    T, K = a.shape
    N = b.shape[1]
    nt = T // tt

    def body(a_ref, b_ref, o_ref, acc_ref):
        part = lax.dot_general(a_ref[...], b_ref[...], TN, preferred_element_type=F32)

        @pl.when(pl.program_id(2) == 0)
        def _():
            acc_ref[...] = part

        @pl.when(pl.program_id(2) > 0)
        def _():
            acc_ref[...] += part

        @pl.when(pl.program_id(2) == nt - 1)
        def _():
            o_ref[...] = acc_ref[...].astype(out_dtype)

    return pl.pallas_call(
        body, name=name, grid=(K // tk, N // tn, nt),
        in_specs=[pl.BlockSpec((tt, tk), lambda i, j, t: (t, i)), pl.BlockSpec((tt, tn), lambda i, j, t: (t, j))],
        out_specs=pl.BlockSpec((tk, tn), lambda i, j, t: (i, j)),
        out_shape=jax.ShapeDtypeStruct((K, N), out_dtype),
        scratch_shapes=[pltpu.VMEM((tk, tn), F32)],
        compiler_params=_params(("arbitrary", "arbitrary", "arbitrary")),
    )(a, b)


def _wo_bwd(dx2b, wo, oa, ob, ga, gb, *, tm=512):
    T = dx2b.shape[0]

    def body(dx_ref, wo_ref, oa_ref, ob_ref, ga_ref, gb_ref, doa_ref, dob1_ref, dob4_ref, dob16_ref, dga_ref, dgb_ref, scr):
        @pl.when(pl.program_id(0) == 0)
        def _():
            dga_ref[...] = jnp.zeros_like(dga_ref)
            dgb_ref[...] = jnp.zeros_like(dgb_ref)

        dm = lax.dot_general(dx_ref[...], wo_ref[...], NT, preferred_element_type=F32)
        for o_ref, g_ref, dg_ref, sl in ((oa_ref, ga_ref, dga_ref, slice(0, WIDTH)),
                                         (ob_ref, gb_ref, dgb_ref, slice(WIDTH, 2 * WIDTH))):
            ov = o_ref[...].astype(F32)
            r = lax.rsqrt(jnp.mean(ov * ov, axis=-1, keepdims=True) + EPS)
            xh = ov * r
            d = dm[:, sl]
            dg_ref[...] += jnp.sum(d * xh, axis=0, keepdims=True)
            do = _rms_bwd(d, xh, r, g_ref[...])
            if o_ref is oa_ref:
                doa_ref[...] = do.astype(BF16)
            else:
                _scr_put(scr, do)
                for (_, dil), v_ref in zip(BRANCHES, (dob1_ref, dob4_ref, dob16_ref)):
                    _restride(scr, v_ref, dil, tm)

    row = lambda w_: pl.BlockSpec((tm, w_), lambda i: (i, 0))
    full = lambda a: pl.BlockSpec(a.shape, lambda i: (0, 0))
    return pl.pallas_call(
        body, name="wo_bwd", grid=(T // tm,),
        in_specs=[row(D_MODEL), full(wo), row(WIDTH), row(WIDTH), full(ga), full(gb)],
        out_specs=[row(WIDTH)] + _view_specs(tm)
        + [pl.BlockSpec((1, WIDTH), lambda i: (0, 0)), pl.BlockSpec((1, WIDTH), lambda i: (0, 0))],
        out_shape=[jax.ShapeDtypeStruct((T, WIDTH), BF16)] + _view_shapes(T, BF16)
        + [jax.ShapeDtypeStruct((1, WIDTH), F32), jax.ShapeDtypeStruct((1, WIDTH), F32)],
        scratch_shapes=[_scr(tm)],
        compiler_params=_params(("arbitrary",)),
    )(dx2b, wo, oa, ob, ga, gb)


def _dproj(dqa, dka, dva, dqs, dks, dvs, cos, sin, *, tm=512):
    T = dqa.shape[0]

    def body(dqa_ref, dka_ref, dva_ref, q1, q2, q3, k1, k2, k3, v1, v2, v3, cos_ref, sin_ref, dp_ref, db_ref, acc, tmp):
        @pl.when(pl.program_id(0) == 0)
        def _():
            db_ref[...] = jnp.zeros_like(db_ref)

        cosv = cos_ref[...]
        sinv = sin_ref[...]
        lane = lax.broadcasted_iota(jnp.int32, (tm, PAIR), 1)
        first = (lane % HEAD_DIM) < (HEAD_DIM // 2)

        def put(off, val):
            dp_ref[:, off:off + PAIR] = val.astype(BF16)
            db_ref[:, off:off + PAIR] += jnp.sum(val, axis=0, keepdims=True)

        for src, off, width in ((dqa_ref, 0, 512), (dka_ref, 512, 256)):
            for j in range(0, width, PAIR):
                d = src[:, j:j + PAIR].astype(F32)
                put(off + j, d * cosv - _rope_rot(d, first) * sinv)
        for j in range(0, 256, PAIR):
            put(768 + j, dva_ref[:, j:j + PAIR].astype(F32))
        for (a, b, c), off in (((q1, q2, q3), 1024), ((k1, k2, k3), 1536), ((v1, v2, v3), 2048)):
            _unstride(b, acc, BRANCHES[1][1], tm)
            _unstride(c, tmp, BRANCHES[2][1], tm)
            for j in range(N_CHUNK):
                put(off + j * PAIR, a[:, j * PAIR:(j + 1) * PAIR].astype(F32) + acc[j] + tmp[j])

    row = lambda w_: pl.BlockSpec((tm, w_), lambda i: (i, 0))
    return pl.pallas_call(
        body, name="dproj", grid=(T // tm,),
        in_specs=[row(512), row(256), row(256)] + _view_specs(tm) * 3 + [row(PAIR), row(PAIR)],
        out_specs=[row(D_INP), pl.BlockSpec((1, D_INP), lambda i: (0, 0))],
        out_shape=[jax.ShapeDtypeStruct((T, D_INP), BF16), jax.ShapeDtypeStruct((1, D_INP), F32)],
        scratch_shapes=[_scr(tm)] * 2,
        compiler_params=_params(("arbitrary",)),
    )(dqa, dka, dva, *dqs, *dks, *dvs, cos, sin)


def _inproj_bwd(dp, w, x, dx2, g, *, tm=512):
    T = x.shape[0]

    def body(dp_ref, w_ref, x_ref, dx2_ref, g_ref, gx_ref, dg_ref):
        @pl.when(pl.program_id(0) == 0)
        def _():
            dg_ref[...] = jnp.zeros_like(dg_ref)

        dh = jnp.dot(dp_ref[...], w_ref[...], preferred_element_type=F32)
        xv = x_ref[...]
        r = lax.rsqrt(jnp.mean(xv * xv, axis=-1, keepdims=True) + EPS)
        xh = xv * r
        dg_ref[...] += jnp.sum(dh * xh, axis=0, keepdims=True)
        gx_ref[...] = dx2_ref[...] + _rms_bwd(dh, xh, r, g_ref[...])

    row = lambda w_: pl.BlockSpec((tm, w_), lambda i: (i, 0))
    full = lambda a: pl.BlockSpec(a.shape, lambda i: (0, 0))
    return pl.pallas_call(
        body, name="inproj_bwd", grid=(T // tm,),
        in_specs=[row(D_INP), full(w), row(D_MODEL), row(D_MODEL), full(g)],
        out_specs=[row(D_MODEL), pl.BlockSpec((1, D_MODEL), lambda i: (0, 0))],
        out_shape=[jax.ShapeDtypeStruct((T, D_MODEL), F32), jax.ShapeDtypeStruct((1, D_MODEL), F32)],
        compiler_params=_params(("arbitrary",)),
    )(dp, w, x, dx2, g)


def _bias_sink_grads(dsums, bmaps, dsk):
    def body(s1, s2, s3, m1, m2, m3, dsk_ref, drel_ref, dsink_ref):
        row = lax.broadcasted_iota(jnp.int32, (N_HEADS, 128), 0)
        lane = lax.broadcasted_iota(jnp.int32, (N_HEADS, 128), 1)
        out = jnp.zeros((N_HEADS, 128), F32)
        for s_ref, m_ref in ((s1, m1), (s2, m2), (s3, m3)):
            bm = m_ref[...]
            for h in range(N_HEADS):
                a = s_ref[h]
                for b in range(REL_BUCKETS):
                    v = jnp.sum(jnp.sum(jnp.where(bm == b, a, 0.0), axis=-1, keepdims=True), axis=0, keepdims=True)
                    out = out + jnp.where((row == h) & (lane == b), v, 0.0)
        drel_ref[...] = out
        dsink_ref[...] = -jnp.sum(dsk_ref[...], axis=0, keepdims=True)

    vm = pl.BlockSpec(memory_space=pltpu.VMEM)
    return pl.pallas_call(
        body, name="bias_sink_grads",
        in_specs=[vm] * 7, out_specs=[vm, vm],
        out_shape=[jax.ShapeDtypeStruct((N_HEADS, 128), F32), jax.ShapeDtypeStruct((1, WIDTH), F32)],
        compiler_params=_params(),
    )(*dsums, *bmaps, dsk)


def _all_gather(blk, *, name):
    R, C = blk.shape

    def body(x_ref, out_ref, send_sems, recv_sems, local_sem):
        x, y, c = lax.axis_index("x"), lax.axis_index("y"), lax.axis_index("c")
        me, sibling = (x, y, c), (x, y, 1 - c)
        chips = [(1 - x, y), (x, 1 - y), (1 - x, 1 - y)]

        def slot(px, py, pc):
            return out_ref.at[4 * px + 2 * py + pc]

        def copy(k, block, to, src=None):
            return pltpu.make_async_remote_copy(
                src_ref=slot(*block) if src is None else src, dst_ref=slot(*block),
                send_sem=send_sems.at[k], recv_sem=recv_sems.at[k], device_id=to, device_id_type=MESH)

        mine = pltpu.make_async_copy(x_ref, slot(*me), local_sem)
        mine.start()
        first = [copy(0, me, sibling, src=x_ref)]
        first += [copy(1 + j, me, (*chip, c), src=x_ref) for j, chip in enumerate(chips)]
        for cp in first:
            cp.start()
        passed = [copy(4 + j, (*chip, c), sibling) for j, chip in enumerate(chips)]
        for j, chip in enumerate(chips):
            copy(1 + j, (*chip, c), me).wait_recv()
            passed[j].start()
        copy(0, sibling, me).wait_recv()
        for j, chip in enumerate(chips):
            copy(4 + j, (*chip, 1 - c), me).wait_recv()
        for cp in first + passed:
            cp.wait_send()
        mine.wait()

    return pl.pallas_call(
        body, name=name,
        in_specs=[pl.BlockSpec(memory_space=pl.ANY)], out_specs=pl.BlockSpec(memory_space=pl.ANY),
        out_shape=jax.ShapeDtypeStruct((N_DEV, R, C), blk.dtype),
        scratch_shapes=[pltpu.SemaphoreType.DMA((7,)), pltpu.SemaphoreType.DMA((7,)), pltpu.SemaphoreType.DMA],
        compiler_params=pltpu.CompilerParams(has_side_effects=True),
    )(blk)


def _peers(x, y, c):
    return [(x ^ (k >> 2), y ^ ((k >> 1) & 1), c ^ (k & 1)) for k in range(1, N_DEV)]


_HBM = pl.BlockSpec(memory_space=pltpu.HBM)
_SEM = pl.BlockSpec(memory_space=pltpu.SEMAPHORE)
_EFFECT = pltpu.SideEffectType.DATAFLOW_SIDE_EFFECTING


def _exchange_start(srcs, *, gather, name):
    n = len(srcs)
    lands = [lax.empty((N_DEV,) + s.shape[-2:], s.dtype) for s in srcs]

    def body(*refs):
        src_refs, land_refs = refs[:n], refs[n:2 * n]
        send_sems, recv_sems = refs[2 * n], refs[2 * n + 1]
        token = refs[-1]
        x, y, c = lax.axis_index("x"), lax.axis_index("y"), lax.axis_index("c")
        mine = 4 * x + 2 * y + c
        for a in range(n):
            for k, peer in enumerate(_peers(x, y, c)):
                dest = 4 * peer[0] + 2 * peer[1] + peer[2]
                j = a * (N_DEV - 1) + k
                pltpu.make_async_remote_copy(
                    src_ref=src_refs[a] if gather else src_refs[a].at[dest], dst_ref=land_refs[a].at[mine],
                    send_sem=send_sems.at[j], recv_sem=recv_sems.at[j], device_id=peer, device_id_type=MESH).start()
        token[...] = jnp.zeros_like(token)

    sems = pltpu.SemaphoreType.DMA((n * (N_DEV - 1),))
    out = pl.pallas_call(
        body, name=name,
        out_shape=(sems, sems) + tuple(pltpu.HBM(a.shape, a.dtype) for a in list(srcs) + lands)
        + (jax.ShapeDtypeStruct((8, 128), F32),),
        in_specs=(_HBM,) * (2 * n), out_specs=(_SEM, _SEM) + (_HBM,) * (2 * n) + (pl.BlockSpec(memory_space=pltpu.VMEM),),
        input_output_aliases={i: 2 + i for i in range(2 * n)},
        compiler_params=pltpu.CompilerParams(has_side_effects=_EFFECT),
    )(*[pltpu.with_memory_space_constraint(a, pltpu.HBM) for a in list(srcs) + lands])
    return out[:-1], out[-1]


def _exchange_wait(state, after, *, gather, name):
    send_sems, recv_sems = state[0], state[1]
    n = (len(state) - 2) // 2
    arrays = state[2:]

    def body(*refs):
        src_refs, land_refs = refs[:n], refs[n:2 * n]
        send_sems, recv_sems = refs[2 * n], refs[2 * n + 1]
        x, y, c = lax.axis_index("x"), lax.axis_index("y"), lax.axis_index("c")
        for a in range(n):
            for k, peer in enumerate(_peers(x, y, c)):
                other = 4 * peer[0] + 2 * peer[1] + peer[2]
                j = a * (N_DEV - 1) + k
                copy = pltpu.make_async_remote_copy(
                    src_ref=src_refs[a] if gather else src_refs[a].at[other], dst_ref=land_refs[a].at[other],
                    send_sem=send_sems.at[j], recv_sem=recv_sems.at[j], device_id=peer, device_id_type=MESH)
                copy.wait_send()
                copy.wait_recv()

    out = pl.pallas_call(
        body, name=name,
        out_shape=tuple(pltpu.HBM(a.shape, a.dtype) for a in arrays),
        in_specs=(_HBM,) * (2 * n) + (_SEM, _SEM, pl.BlockSpec(memory_space=pl.ANY)), out_specs=(_HBM,) * (2 * n),
        input_output_aliases={i: i for i in range(2 * n)},
        compiler_params=pltpu.CompilerParams(has_side_effects=_EFFECT),
    )(*arrays, send_sems, recv_sems, after)
    return out[n:]


def _fill_own(got, own):
    mine = 4 * lax.axis_index("x") + 2 * lax.axis_index("y") + lax.axis_index("c")
    return lax.dynamic_update_slice(got, own[None], (mine, 0, 0))


def _adam_math(w, g, m, v):
    m = ADAM_B1 * m + (1.0 - ADAM_B1) * g
    v = ADAM_B2 * v + (1.0 - ADAM_B2) * (g * g)
    m_hat = m / (1.0 - ADAM_B1 ** ADAM_STEP)
    v_hat = v / (1.0 - ADAM_B2 ** ADAM_STEP)
    delta = -ADAM_LR * (m_hat / (jnp.sqrt(v_hat) + ADAM_EPS) + ADAM_WD * w)
    return delta, m, v


def _sum_parts(parts, *, name):
    _, R, C = parts.shape
    tr = R // 2
    assert tr % 16 == 0

    def body(p_ref, g_ref):
        g = p_ref[0].astype(F32)
        for s in range(1, N_DEV):
            g = g + p_ref[s].astype(F32)
        g_ref[...] = g

    return pl.pallas_call(
        body, name=name, grid=(R // tr,),
        in_specs=[pl.BlockSpec((N_DEV, tr, C), lambda i: (0, i, 0))], out_specs=pl.BlockSpec((tr, C), lambda i: (i, 0)),
        out_shape=jax.ShapeDtypeStruct((R, C), F32), compiler_params=_params(("arbitrary",)),
    )(parts)


def _adamw(parts, w, m, v, *, name):
    R, C = w.shape
    n_parts = parts.shape[0]
    tr = R // 2
    assert tr % 16 == 0

    def body(p_ref, w_ref, m_ref, v_ref, g_ref, d_ref, nm_ref, nv_ref):
        g = p_ref[0].astype(F32)
        for s in range(1, n_parts):
            g = g + p_ref[s].astype(F32)
        d, nm, nv = _adam_math(w_ref[...], g, m_ref[...], v_ref[...])
        g_ref[...] = g
        d_ref[...] = d
        nm_ref[...] = nm
        nv_ref[...] = nv

    blk = pl.BlockSpec((tr, C), lambda i: (i, 0))
    return pl.pallas_call(
        body, name=name, grid=(R // tr,),
        in_specs=[pl.BlockSpec((n_parts, tr, C), lambda i: (0, i, 0)), blk, blk, blk],
        out_specs=[blk] * 4, out_shape=[jax.ShapeDtypeStruct((R, C), F32)] * 4,
        compiler_params=_params(("arbitrary",)),
    )(parts, w, m, v)


def _adamw_small(parts, w, m, v):
    def body(p_ref, w_ref, m_ref, v_ref, g_ref, d_ref, nm_ref, nv_ref):
        g = p_ref[0]
        for s in range(1, N_DEV):
            g = g + p_ref[s]
        d, nm, nv = _adam_math(w_ref[...], g, m_ref[...], v_ref[...])
        g_ref[...] = g
        d_ref[...] = d
        nm_ref[...] = nm
        nv_ref[...] = nv

    vm = pl.BlockSpec(memory_space=pltpu.VMEM)
    return pl.pallas_call(
        body, name="adamw_small", in_specs=[vm] * 4, out_specs=[vm] * 4,
        out_shape=[jax.ShapeDtypeStruct((SMALL_ROWS, 128), F32)] * 4, compiler_params=_params(),
    )(parts, w, m, v)


def _t5_bucket(dist):
    max_exact = REL_BUCKETS // 2
    df = jnp.maximum(dist, 1).astype(F32)
    large = max_exact + (jnp.log(df / max_exact) / math.log(REL_MAX_DISTANCE / max_exact)
                         * (REL_BUCKETS - max_exact)).astype(jnp.int32)
    large = jnp.minimum(large, REL_BUCKETS - 1)
    return jnp.where(dist < max_exact, dist, large)


def _band_tables(rel_table, dil, n_back):
    qi = jnp.arange(BLK)[:, None]
    kj = jnp.arange(2 * BLK)[None, :]
    delta = BLK + qi - kj
    in_band = (delta >= 0) & (delta <= n_back)
    if rel_table is None:
        vals = jnp.zeros((N_HEADS, BLK, 2 * BLK), F32)
        bmap = None
    else:
        bucket = _t5_bucket(jnp.clip(delta, 0, n_back) * dil)
        vals = jnp.zeros((N_HEADS, BLK, 2 * BLK), F32)
        for b in range(REL_BUCKETS):
            vals = jnp.where((bucket == b)[None], rel_table[b][:, None, None], vals)
        bmap = jnp.where(in_band, bucket, -1).astype(jnp.int32)
    later = jnp.where(in_band[None], vals, NEG)
    first = jnp.where((in_band & (kj >= BLK))[None], vals, NEG)
    return jnp.stack([later, first]), bmap


def _rope_tables(T):
    half = HEAD_DIM // 2
    inv_freq = ROPE_THETA ** (-jnp.arange(half, dtype=F32) / half)
    ang = jnp.arange(T, dtype=F32)[:, None] * inv_freq[None, :]
    cos, sin = jnp.cos(ang), jnp.sin(ang)
    return jnp.tile(cos, (1, 4)), jnp.tile(jnp.concatenate([-sin, sin], axis=1), (1, 2))


def _widen_in(a, axis):
    sl = lambda lo, hi: lax.slice_in_dim(a, lo, hi, axis=axis)
    dup = lambda lo: [sl(lo, lo + 64), sl(lo, lo + 64), sl(lo + 64, lo + 128), sl(lo + 64, lo + 128)]
    return jnp.concatenate([sl(0, 512)] + dup(512) + dup(640) + [sl(768, D_IN)], axis=axis)


def _fold_in(a, axis):
    sl = lambda lo, hi: lax.slice_in_dim(a, lo, hi, axis=axis)
    fold = lambda lo: [sl(lo, lo + 64) + sl(lo + 64, lo + 128), sl(lo + 128, lo + 192) + sl(lo + 192, lo + 256)]
    return jnp.concatenate([sl(0, 512)] + fold(512) + fold(768) + [sl(1024, D_INP)], axis=axis)


def _local_step(x, tgt, g_attn, wint, b_in, sinks, rel_table, g_out_a, g_out_b, g_ffn, g_final, token,
                wo_fn, ffn_fn, early_fn):
    T = x.shape[0]
    cos, sin = _rope_tables(T)
    cos = cos + token[0, 0]
    winp = _widen_in(wint, 0)
    binp = _widen_in(b_in, 1)
    g_final2 = g_final.reshape(1, D_MODEL)
    sink8 = sinks.reshape(N_HEADS)

    bias_a, _ = _band_tables(None, 1, BLK - 1)
    tabs = [_band_tables(rel_table, dil, window // dil) for window, dil in BRANCHES]

    h1, qa, ka, va, *qkv_b = _norm_proj(x, g_attn, winp, binp, cos, sin)
    qbs, kbs, vbs = qkv_b[0:3], qkv_b[3:6], qkv_b[6:9]
    oa, lse_a = _attn_fwd(qa, ka, va, bias_a, sink8, dil=1, kv_pairs=2, use_sink=True, name="attn_a_fwd")
    outs = [_attn_fwd(qbs[n], kbs[n], vbs[n], tabs[n][0], sink8, dil=dil, kv_pairs=4, use_sink=False,
                      name=f"attn_b{n}_fwd") for n, (_, dil) in enumerate(BRANCHES)]
    wo = wo_fn(outs[2][1])
    x2, mixed, h2, *ob_lse = _merge_wo(x, oa, outs[0][0], outs[1][0], outs[2][0], outs[0][1], outs[1][1], outs[2][1],
                                       g_out_a, g_out_b, wo, g_ffn)
    obs, lses = ob_lse[0:3], ob_lse[3:6]
    wgt, wut, wd = ffn_fn(h2)
    act_dg, act_du, act = _ffn_up(h2, wgt, wut)
    dx3, dx3b, loss, dg_final = _ffn_down_loss(act, wd, x2, tgt, g_final2)

    dgate, dup = _ffn_bwd_act(dx3b, act_dg, act_du, wd)
    dx2, dx2b, dg_ffn = _ffn_bwd_in(dgate, dup, wgt, wut, x2, dx3, g_ffn)
    dwd = _matmul_tn(act, dx3b, tk=1408, tn=1024, name="dw_down")
    dwgt = _matmul_tn(dgate, h2, tk=1408, tn=1024, name="dw_gate")
    dwut = _matmul_tn(dup, h2, tk=1408, tn=1024, name="dw_up")
    dwo = _matmul_tn(mixed, dx2b, tk=1024, tn=1024, name="dw_o")
    early, token2 = early_fn(dict(w_o=dwo, w_gate=dwgt, w_up=dwut, w_down=dwd))
    doa, *dobs, dg_out_a, dg_out_b = _wo_bwd(dx2b, wo, oa, obs[0], g_out_a + token2[0, 0], g_out_b)

    dqa, dka, dva, _, dsk = _attn_bwd(qa, ka, va, oa, doa, lse_a, bias_a, sink8, dil=1, kv_pairs=2, use_sink=True,
                                      name="attn_a_bwd")
    res = [_attn_bwd(qbs[n], kbs[n], vbs[n], obs[n], dobs[n], lses[n], tabs[n][0], sink8, dil=dil, kv_pairs=4,
                     use_sink=False, name=f"attn_b{n}_bwd") for n, (_, dil) in enumerate(BRANCHES)]
    dp, dbp = _dproj(dqa, dka, dva, [r[0] for r in res], [r[1] for r in res], [r[2] for r in res], cos, sin)
    grad_x, dg_attn = _inproj_bwd(dp, winp, x, dx2, g_attn)
    dwin = _fold_in(_matmul_tn(dp, h1, tk=1280, tn=1024, out_dtype=F32, name="dw_in"), 0)
    drel, dsink = _bias_sink_grads([r[3] for r in res], [t[1] for t in tabs], dsk)

    small = dict(
        g_attn=dg_attn, b_in=_fold_in(dbp, 1), sinks=dsink[:, ::HEAD_DIM], rel_table=drel[:, :REL_BUCKETS].T,
        g_out_a=dg_out_a, g_out_b=dg_out_b, g_ffn=dg_ffn, g_final=dg_final.reshape(D_MODEL))
    return loss[0, 0], grad_x, dwin, early, small


SMALL_NAMES = ("g_attn", "b_in", "sinks", "rel_table", "g_out_a", "g_out_b", "g_ffn", "g_final")


def _pack_small(vals):
    flat = jnp.concatenate([vals[n].reshape(-1).astype(F32) for n in SMALL_NAMES])
    return jnp.pad(flat, (0, SMALL_ROWS * 128 - flat.shape[0])).reshape(SMALL_ROWS, 128)


def _unpack_small(packed, like):
    flat = packed.reshape(-1)
    out, off = {}, 0
    for n in SMALL_NAMES:
        size = like[n].size
        out[n] = flat[off:off + size].reshape(like[n].shape)
        off += size
    return out


def kernel(x, g_attn, w_in, b_in, sinks, rel_table, g_out_a, g_out_b, w_o, g_ffn, w_gate, w_up, w_down, g_final, loss_target, m_g_attn, m_w_in, m_b_in, m_sinks, m_rel_table, m_g_out_a, m_g_out_b, m_w_o, m_g_ffn, m_w_gate, m_w_up, m_w_down, m_g_final, v_g_attn, v_w_in, v_b_in, v_sinks, v_rel_table, v_g_out_a, v_g_out_b, v_w_o, v_g_ffn, v_w_gate, v_w_up, v_w_down, v_g_final):
    mine = 4 * lax.axis_index("x") + 2 * lax.axis_index("y") + lax.axis_index("c")
    rest_names = ("w_o", "w_gate", "w_up", "w_down")

    rest = [w_o[0].astype(BF16), w_gate[0].astype(BF16).T, w_up[0].astype(BF16).T, w_down[0].astype(BF16)]
    wint = _all_gather(w_in[0].astype(BF16).T, name="gather_w_in").reshape(D_IN, D_MODEL)
    wint, rest = lax.optimization_barrier((wint, rest))
    wo_state, token_o = _exchange_start(rest[:1], gather=True, name="gather_w_o_start")
    token_o, ffn_src = lax.optimization_barrier((token_o, rest[1:]))
    ffn_state, token = _exchange_start(ffn_src, gather=True, name="gather_ffn_start")
    token = token + token_o

    def whole(got, own):
        return [_fill_own(g, o).reshape(N_DEV * o.shape[0], D_MODEL) for g, o in zip(got, own)]

    def wo_fn(after):
        return whole(_exchange_wait(wo_state, after, gather=True, name="gather_w_o_wait"), rest[:1])[0]

    def ffn_fn(after):
        return whole(_exchange_wait(ffn_state, after, gather=True, name="gather_ffn_wait"), ffn_src)

    def early_fn(dws):
        parts = [dws[n].reshape(N_DEV, -1, D_MODEL) for n in rest_names]
        own = [lax.dynamic_index_in_dim(p, mine, 0, keepdims=False) for p in parts]
        state, token2 = _exchange_start(parts, gather=False, name="scatter_rest_start")
        return (state, own), token2

    loss_part, grad_x, dwint, (early_state, early_own), small = _local_step(
        x[0], loss_target[0], g_attn, wint, b_in, sinks, rel_table, g_out_a, g_out_b, g_ffn, g_final, token,
        wo_fn, ffn_fn, early_fn)
    loss = lax.psum(loss_part, ("x", "y", "c"))

    parts_in = dwint.astype(BF16).reshape(N_DEV, D_IN // N_DEV, D_MODEL)
    own_in = lax.dynamic_index_in_dim(parts_in, mine, 0, keepdims=False)
    in_state, token3 = _exchange_start([parts_in], gather=False, name="scatter_w_in_start")
    got = [_fill_own(g, own) for g, own in
           zip(_exchange_wait(early_state, token3, gather=False, name="scatter_rest_wait"), early_own)]

    def update(n, parts, w, m, v, transposed):
        if transposed:
            parts = _sum_parts(parts, name="sum_" + n).T[None]
        return [a[None] for a in _adamw(parts, w[0], m[0], v[0], name="adamw_" + n)]

    big = dict(w_o=update("w_o", got[0], w_o, m_w_o, v_w_o, False),
               w_gate=update("w_gate", got[1], w_gate, m_w_gate, v_w_gate, True),
               w_up=update("w_up", got[2], w_up, m_w_up, v_w_up, True),
               w_down=update("w_down", got[3], w_down, m_w_down, v_w_down, False))

    ws = dict(g_attn=g_attn, b_in=b_in, sinks=sinks, rel_table=rel_table, g_out_a=g_out_a, g_out_b=g_out_b,
              g_ffn=g_ffn, g_final=g_final)
    ms = dict(g_attn=m_g_attn, b_in=m_b_in, sinks=m_sinks, rel_table=m_rel_table, g_out_a=m_g_out_a,
              g_out_b=m_g_out_b, g_ffn=m_g_ffn, g_final=m_g_final)
    vs = dict(g_attn=v_g_attn, b_in=v_b_in, sinks=v_sinks, rel_table=v_rel_table, g_out_a=v_g_out_a,
              g_out_b=v_g_out_b, g_ffn=v_g_ffn, g_final=v_g_final)
    sparts = _all_gather(_pack_small(small), name="gather_small")
    sm_packed = _adamw_small(sparts, _pack_small(ws), _pack_small(ms), _pack_small(vs))
    sm = [_unpack_small(a, ws) for a in sm_packed]

    done = sm_packed[1][:1, :1] + sum(big[n][1][0, :1, :1] for n in rest_names)
    got_in = _fill_own(_exchange_wait(in_state, done, gather=False, name="scatter_w_in_wait")[0], own_in)
    big["w_in"] = update("w_in", got_in, w_in, m_w_in, v_w_in, True)

    order = ("g_attn", "w_in", "b_in", "sinks", "rel_table", "g_out_a", "g_out_b", "w_o", "g_ffn", "w_gate", "w_up",
             "w_down", "g_final")
    outs = [loss, grad_x[None]]
    for k in range(4):
        outs += [big[n][k] if n in big else sm[k][n] for n in order]
    return tuple(outs)
```

```python
import functools
import math

import jax
import jax.numpy as jnp
from jax import lax
from jax.experimental import pallas as pl
from jax.experimental.pallas import tpu as pltpu

F32 = jnp.float32
BF16 = jnp.bfloat16

N_DEV = 8
D_MODEL = 1024
HEAD_DIM = 64
N_HEADS = 8
PAIR = 2 * HEAD_DIM
WIDTH = N_HEADS * HEAD_DIM
D_IN = 2304
D_INP = 2560
D_FF = 2816
BLK = 128
ROPE_THETA = 150000.0
REL_BUCKETS = 32
REL_MAX_DISTANCE = 2048
EPS = 1e-5
NEG = -1e30
BRANCHES = ((128, 1), (512, 4), (2048, 16))
Q_SCALE = HEAD_DIM ** -0.5

ADAM_LR = 0.001
ADAM_B1 = 0.9
ADAM_B2 = 0.999
ADAM_EPS = 1e-08
ADAM_WD = 0.01
ADAM_STEP = 10

VMEM_LIMIT = 56 * 1024 * 1024
MESH = pl.DeviceIdType.MESH

NT = (((1,), (1,)), ((), ()))
TN = (((0,), (0,)), ((), ()))

SMALL_ROWS = 56


def _params(sem=None):
    return pltpu.CompilerParams(dimension_semantics=sem, vmem_limit_bytes=VMEM_LIMIT)


def _rms_bwd(dh, xh, r, g):
    u = dh * g
    return r * (u - xh * jnp.mean(u * xh, axis=-1, keepdims=True))


def _rope_rot(t, first):
    return jnp.where(first, pltpu.roll(t, 96, 1), pltpu.roll(t, 32, 1))


N_CHUNK = WIDTH // PAIR


def _scr(tm):
    return pltpu.VMEM((N_CHUNK, tm, PAIR), F32)


def _scr_get(scr):
    return jnp.concatenate([scr[j] for j in range(N_CHUNK)], axis=1)


def _scr_put(scr, val):
    for j in range(N_CHUNK):
        scr[j] = val[:, j * PAIR:(j + 1) * PAIR]


def _unstride(view_ref, scr, dil, tm):
    n = tm // dil
    for r in range(dil):
        for j in range(N_CHUNK):
            col = r * WIDTH + j * PAIR
            scr.at[j][pl.ds(r, n, stride=dil), :] = view_ref[:, col:col + PAIR].astype(F32)


def _restride(scr, out_ref, dil, tm):
    n = tm // dil
    for r in range(dil):
        for j in range(N_CHUNK):
            col = r * WIDTH + j * PAIR
            rows = scr[j] if dil == 1 else scr.at[j][pl.ds(r, n, stride=dil), :]
            out_ref[:, col:col + PAIR] = rows.astype(out_ref.dtype)


def _view_specs(tm):
    return [pl.BlockSpec((tm // dil, dil * WIDTH), lambda i: (i, 0)) for _, dil in BRANCHES]


def _view_shapes(T, dtype):
    return [jax.ShapeDtypeStruct((T // dil, dil * WIDTH), dtype) for _, dil in BRANCHES]


def _norm_proj(x, g, w, b, cos, sin, *, tm=512):
    T = x.shape[0]

    def body(x_ref, g_ref, w_ref, b_ref, cos_ref, sin_ref, h_ref, qa_ref, ka_ref, va_ref, *rest):
        outs_b, ys = rest[:9], rest[9]
        xv = x_ref[...]
        r = lax.rsqrt(jnp.mean(xv * xv, axis=-1, keepdims=True) + EPS)
        h = (xv * r * g_ref[...]).astype(BF16)
        h_ref[...] = h
        cosv = cos_ref[...]
        sinv = sin_ref[...]
        lane = lax.broadcasted_iota(jnp.int32, (tm, PAIR), 1)
        first = (lane % HEAD_DIM) < (HEAD_DIM // 2)

        def proj(off):
            return (lax.dot_general(h, w_ref[off:off + 256, :], NT, preferred_element_type=F32)
                    + b_ref[:, off:off + 256])

        for (off, width, rot, scale), o_ref in zip(((0, 512, True, Q_SCALE), (512, 256, True, 1.0), (768, 256, False, 1.0)),
                                                   (qa_ref, ka_ref, va_ref)):
            for c in range(0, width, 256):
                y = proj(off + c)
                for j in range(0, 256, PAIR):
                    t = y[:, j:j + PAIR]
                    if rot:
                        t = t * cosv + _rope_rot(t, first) * sinv
                    if scale != 1.0:
                        t = t * scale
                    o_ref[:, c + j:c + j + PAIR] = t.astype(BF16)
        for n, (off, scale) in enumerate(((1024, Q_SCALE), (1536, 1.0), (2048, 1.0))):
            for c in range(0, WIDTH, 256):
                y = proj(off + c)
                y = y * scale if scale != 1.0 else y
                for j in range(0, 256, PAIR):
                    ys[(c + j) // PAIR] = y[:, j:j + PAIR]
            for (_, dil), o_ref in zip(BRANCHES, outs_b[3 * n:3 * n + 3]):
                _restride(ys, o_ref, dil, tm)

    row = lambda w_: pl.BlockSpec((tm, w_), lambda i: (i, 0))
    full = lambda a: pl.BlockSpec(a.shape, lambda i: (0, 0))
    return pl.pallas_call(
        body, name="norm_proj", grid=(T // tm,),
        in_specs=[row(D_MODEL), full(g), full(w), full(b), row(PAIR), row(PAIR)],
        out_specs=[row(D_MODEL), row(512), row(256), row(256)] + _view_specs(tm) * 3,
        out_shape=[jax.ShapeDtypeStruct((T, n), BF16) for n in (D_MODEL, 512, 256, 256)] + _view_shapes(T, BF16) * 3,
        scratch_shapes=[_scr(tm)],
        compiler_params=_params(("arbitrary",)),
    )(x, g, w, b, cos, sin)


SUB = 4
AHEAD = 2


def _attn_specs(kvw):
    q_spec = pl.BlockSpec((SUB * BLK, WIDTH), lambda r, i: (i, r))
    kc_spec = pl.BlockSpec((SUB * BLK, kvw), lambda r, i: (i, r))
    kp_spec = pl.BlockSpec((BLK, kvw), lambda r, i: (jnp.maximum(SUB * i - 1, 0), r))
    b_spec = pl.BlockSpec((2, N_HEADS, BLK, 2 * BLK), lambda r, i: (0, 0, 0, 0))
    return q_spec, kp_spec, kc_spec, b_spec


def _window(prev_ref, cur_ref, j, ksl):
    before = prev_ref[:, ksl] if j == 0 else cur_ref[(j - 1) * BLK:j * BLK, ksl]
    return jnp.concatenate([before, cur_ref[j * BLK:(j + 1) * BLK, ksl]], axis=0)


def _attn_fwd(q, k, v, bias, sinks, *, dil, kv_pairs, use_sink, name):
    L = q.shape[0]
    ns = L // (SUB * BLK)
    kvw = kv_pairs * PAIR
    rep = 4 // kv_pairs

    def body(sink_ref, q_ref, kp_ref, kc_ref, vp_ref, vc_ref, b_ref, o_ref, lse_ref):
        lane = lax.broadcasted_iota(jnp.int32, (1, PAIR), 1)
        lo = lane < HEAD_DIM
        first = jnp.where(pl.program_id(1) == 0, 1, 0)
        def scores(j, hp):
            rows = slice(j * BLK, (j + 1) * BLK)
            sl = slice(hp * PAIR, (hp + 1) * PAIR)
            ksl = slice((hp // rep) * PAIR, (hp // rep + 1) * PAIR)
            qp = q_ref[rows, sl]
            kk = _window(kp_ref, kc_ref, j, ksl)
            vv = _window(vp_ref, vc_ref, j, ksl)
            heads = []
            for e in range(2):
                h = 2 * hp + e
                msk = lo if e == 0 else jnp.logical_not(lo)
                qm = jnp.where(msk, qp, jnp.zeros_like(qp))
                s = lax.dot_general(qm, kk, NT, preferred_element_type=F32) + (b_ref[first, h] if j == 0 else b_ref[0, h])
                heads.append((h, msk, s))
            return rows, sl, vv, heads

        def outputs(rows, sl, vv, heads):
            o_pair = None
            lse_pair = None
            for h, msk, s in heads:
                m = jnp.max(s, axis=-1, keepdims=True)
                if use_sink:
                    sk = sink_ref[h]
                    m = jnp.maximum(m, sk)
                p = jnp.exp(s - m)
                l = jnp.sum(p, axis=-1, keepdims=True)
                if use_sink:
                    l = l + jnp.exp(sk - m)
                vm = jnp.where(msk, vv, jnp.zeros_like(vv))
                oe = jnp.dot(p.astype(BF16), vm, preferred_element_type=F32) * (1.0 / l)
                ls = m + jnp.log(l)
                if o_pair is None:
                    o_pair = oe
                    lse_pair = jnp.broadcast_to(ls, (BLK, PAIR))
                else:
                    o_pair = o_pair + oe
                    lse_pair = jnp.where(lo, lse_pair, ls)
            o_ref[rows, sl] = o_pair.astype(BF16)
            lse_ref[rows, sl] = lse_pair

        items = [(j, hp) for j in range(SUB) for hp in range(4)]
        queue = [scores(*it) for it in items[:AHEAD]]
        for n in range(len(items)):
            if n + AHEAD < len(items):
                queue.append(scores(*items[n + AHEAD]))
            outputs(*queue.pop(0))

    q_spec, kp_spec, kc_spec, b_spec = _attn_specs(kvw)
    return pl.pallas_call(
        body, name=name, grid=(dil, ns),
        in_specs=[pl.BlockSpec(memory_space=pltpu.SMEM), q_spec, kp_spec, kc_spec, kp_spec, kc_spec, b_spec],
        out_specs=[q_spec, q_spec],
        out_shape=[jax.ShapeDtypeStruct((L, dil * WIDTH), BF16), jax.ShapeDtypeStruct((L, dil * WIDTH), F32)],
        compiler_params=_params(("arbitrary", "arbitrary")),
    )(sinks, q, k, k, v, v, bias)


def _attn_bwd(q, k, v, o, do, lse, bias, sinks, *, dil, kv_pairs, use_sink, name):
    L = q.shape[0]
    ns = L // (SUB * BLK)
    n_steps = dil * ns
    kvw = kv_pairs * PAIR
    rep = 4 // kv_pairs
    last = slice((SUB - 1) * BLK, SUB * BLK)

    def body(sink_ref, q_ref, kp_ref, kc_ref, vp_ref, vc_ref, o_ref, do_ref, lse_ref, b_ref,
             dq_ref, dk_ref, dv_ref, dsum_ref, dsk_ref, pk_ref, pv_ref):
        t = pl.program_id(0)
        i = t % ns

        @pl.when(t == 0)
        def _():
            dsum_ref[...] = jnp.zeros_like(dsum_ref)
            dsk_ref[...] = jnp.zeros_like(dsk_ref)
            pk_ref[...] = jnp.zeros_like(pk_ref)
            pv_ref[...] = jnp.zeros_like(pv_ref)

        @pl.when(t < n_steps)
        def _():
            lo = lax.broadcasted_iota(jnp.int32, (1, PAIR), 1) < HEAD_DIM
            first = jnp.where(i == 0, 1, 0)
            dks = [[None] * kv_pairs for _ in range(SUB)]
            dvs = [[None] * kv_pairs for _ in range(SUB)]
            def scores(j, hp):
                rows = slice(j * BLK, (j + 1) * BLK)
                kvp = hp // rep
                sl = slice(hp * PAIR, (hp + 1) * PAIR)
                ksl = slice(kvp * PAIR, (kvp + 1) * PAIR)
                qp = q_ref[rows, sl]
                dop = do_ref[rows, sl]
                prod = dop.astype(F32) * o_ref[rows, sl].astype(F32)
                kk = _window(kp_ref, kc_ref, j, ksl)
                vv = _window(vp_ref, vc_ref, j, ksl)
                heads = []
                for e in range(2):
                    h = 2 * hp + e
                    msk = lo if e == 0 else jnp.logical_not(lo)
                    qm = jnp.where(msk, qp, jnp.zeros_like(qp))
                    dom = jnp.where(msk, dop, jnp.zeros_like(dop))
                    km = jnp.where(msk, kk, jnp.zeros_like(kk))
                    s = (lax.dot_general(qm, kk, NT, preferred_element_type=F32)
                         + (b_ref[first, h] if j == 0 else b_ref[0, h]))
                    dp = lax.dot_general(dom, vv, NT, preferred_element_type=F32)
                    heads.append((h, msk, qm, dom, km, s, dp))
                return j, rows, kvp, sl, prod, heads

            def grads(j, rows, kvp, sl, prod, heads):
                dq_pair = None
                c_pair = None
                qms, doms, dsbs, pbs = [], [], [], []
                for h, msk, qm, dom, km, s, dp in heads:
                    ls = lse_ref[rows, h * HEAD_DIM:h * HEAD_DIM + 1]
                    p = jnp.exp(s - ls)
                    delta = jnp.sum(jnp.where(msk, prod, 0.0), axis=-1, keepdims=True)
                    ds = p * (dp - delta)
                    if use_sink:
                        ce = jnp.exp(sink_ref[h] - ls) * delta
                        c_pair = jnp.broadcast_to(ce, (BLK, PAIR)) if c_pair is None else jnp.where(msk, ce, c_pair)
                    else:
                        dsum_ref[h] += ds
                    dsb = ds.astype(BF16)
                    dqe = jnp.dot(dsb, km, preferred_element_type=F32)
                    dq_pair = dqe if dq_pair is None else dq_pair + dqe
                    qms.append(qm)
                    doms.append(dom)
                    dsbs.append(dsb)
                    pbs.append(p.astype(BF16))
                dke = lax.dot_general(jnp.concatenate(dsbs, axis=0), jnp.concatenate(qms, axis=0), TN,
                                      preferred_element_type=F32)
                dve = lax.dot_general(jnp.concatenate(pbs, axis=0), jnp.concatenate(doms, axis=0), TN,
                                      preferred_element_type=F32)
                dks[j][kvp] = dke if dks[j][kvp] is None else dks[j][kvp] + dke
                dvs[j][kvp] = dve if dvs[j][kvp] is None else dvs[j][kvp] + dve
                dq_ref[rows, sl] = (dq_pair * Q_SCALE).astype(BF16)
                if use_sink:
                    dsk_ref[:, sl] += c_pair

            items = [(j, hp) for j in range(SUB) for hp in range(4)]
            ahead = AHEAD + 1 if use_sink else AHEAD
            queue = [scores(*it) for it in items[:ahead]]
            for n in range(len(items)):
                if n + ahead < len(items):
                    queue.append(scores(*items[n + ahead]))
                grads(*queue.pop(0))
            for kvp in range(kv_pairs):
                ksl = slice(kvp * PAIR, (kvp + 1) * PAIR)
                for pend_ref, out_ref, parts in ((pk_ref, dk_ref, [d[kvp] for d in dks]),
                                                 (pv_ref, dv_ref, [d[kvp] for d in dvs])):
                    if SUB > 1:
                        out_ref[:(SUB - 1) * BLK, ksl] = pend_ref[:(SUB - 1) * BLK, ksl].astype(BF16)
                    out_ref[last, ksl] = (pend_ref[last, ksl] + parts[0][:BLK]).astype(BF16)
                    for j in range(SUB):
                        own = parts[j][BLK:]
                        pend_ref[j * BLK:(j + 1) * BLK, ksl] = own + parts[j + 1][:BLK] if j + 1 < SUB else own

        @pl.when(t == n_steps)
        def _():
            dk_ref[...] = pk_ref[...].astype(BF16)
            dv_ref[...] = pv_ref[...].astype(BF16)

    def at(t):
        t = jnp.minimum(t, n_steps - 1)
        return t % ns, t // ns

    def before(t):
        return at(jnp.maximum(t - 1, 0))

    q_spec = pl.BlockSpec((SUB * BLK, WIDTH), at)
    kc_spec = pl.BlockSpec((SUB * BLK, kvw), at)
    kp_spec = pl.BlockSpec((BLK, kvw), lambda t: (jnp.maximum(SUB * at(t)[0] - 1, 0), at(t)[1]))
    b_spec = pl.BlockSpec((2, N_HEADS, BLK, 2 * BLK), lambda t: (0, 0, 0, 0))
    dkv_spec = pl.BlockSpec((SUB * BLK, kvw), before)
    return pl.pallas_call(
        body, name=name, grid=(n_steps + 1,),
        in_specs=[pl.BlockSpec(memory_space=pltpu.SMEM), q_spec, kp_spec, kc_spec, kp_spec, kc_spec,
                  q_spec, q_spec, q_spec, b_spec],
        out_specs=[q_spec, dkv_spec, dkv_spec,
                   pl.BlockSpec((N_HEADS, BLK, 2 * BLK), lambda t: (0, 0, 0)),
                   pl.BlockSpec((BLK, WIDTH), lambda t: (0, 0))],
        out_shape=[jax.ShapeDtypeStruct((L, dil * WIDTH), BF16),
                   jax.ShapeDtypeStruct((L, dil * kvw), BF16),
                   jax.ShapeDtypeStruct((L, dil * kvw), BF16),
                   jax.ShapeDtypeStruct((N_HEADS, BLK, 2 * BLK), F32),
                   jax.ShapeDtypeStruct((BLK, WIDTH), F32)],
        scratch_shapes=[pltpu.VMEM((SUB * BLK, kvw), F32), pltpu.VMEM((SUB * BLK, kvw), F32)],
        compiler_params=_params(("arbitrary",)),
    )(sinks, q, k, k, v, v, o, do, lse, bias)


def _merge_wo(x, oa, o1, o2, o3, l1, l2, l3, ga, gb, wo, gf, *, tm=512):
    T = x.shape[0]

    def body(x_ref, oa_ref, o1_ref, o2_ref, o3_ref, l1_ref, l2_ref, l3_ref, ga_ref, gb_ref, wo_ref, gf_ref,
             x2_ref, mix_ref, h2_ref, ob1_ref, ob4_ref, ob16_ref, ls1_ref, ls4_ref, ls16_ref, so2, so3, sl2, sl3):
        _unstride(o2_ref, so2, BRANCHES[1][1], tm)
        _unstride(o3_ref, so3, BRANCHES[2][1], tm)
        _unstride(l2_ref, sl2, BRANCHES[1][1], tm)
        _unstride(l3_ref, sl3, BRANCHES[2][1], tm)
        la, lb, lc = l1_ref[...], _scr_get(sl2), _scr_get(sl3)
        m = jnp.maximum(jnp.maximum(la, lb), lc)
        ea, eb, ec = jnp.exp(la - m), jnp.exp(lb - m), jnp.exp(lc - m)
        den = ea + eb + ec
        inv = 1.0 / den
        ob = (ea * o1_ref[...].astype(F32) + eb * _scr_get(so2) + ec * _scr_get(so3)) * inv
        _scr_put(so2, ob)
        _scr_put(sl2, m + jnp.log(den))
        for (_, dil), o_ref, l_ref in zip(BRANCHES, (ob1_ref, ob4_ref, ob16_ref), (ls1_ref, ls4_ref, ls16_ref)):
            _restride(so2, o_ref, dil, tm)
            _restride(sl2, l_ref, dil, tm)
        oav = oa_ref[...].astype(F32)
        ra = lax.rsqrt(jnp.mean(oav * oav, axis=-1, keepdims=True) + EPS)
        rb = lax.rsqrt(jnp.mean(ob * ob, axis=-1, keepdims=True) + EPS)
        mix_ref[:, :WIDTH] = (oav * ra * ga_ref[...]).astype(BF16)
        mix_ref[:, WIDTH:] = (ob * rb * gb_ref[...]).astype(BF16)
        x2 = x_ref[...] + jnp.dot(mix_ref[...], wo_ref[...], preferred_element_type=F32)
        x2_ref[...] = x2
        r2 = lax.rsqrt(jnp.mean(x2 * x2, axis=-1, keepdims=True) + EPS)
        h2_ref[...] = (x2 * r2 * gf_ref[...]).astype(BF16)

    row = lambda w_: pl.BlockSpec((tm, w_), lambda i: (i, 0))
    full = lambda a: pl.BlockSpec(a.shape, lambda i: (0, 0))
    return pl.pallas_call(
        body, name="merge_wo", grid=(T // tm,),
        in_specs=[row(D_MODEL), row(WIDTH)] + _view_specs(tm) * 2 + [full(ga), full(gb), full(wo), full(gf)],
        out_specs=[row(D_MODEL), row(D_MODEL), row(D_MODEL)] + _view_specs(tm) * 2,
        out_shape=[jax.ShapeDtypeStruct((T, D_MODEL), F32), jax.ShapeDtypeStruct((T, D_MODEL), BF16),
                   jax.ShapeDtypeStruct((T, D_MODEL), BF16)] + _view_shapes(T, BF16) + _view_shapes(T, F32),
        scratch_shapes=[_scr(tm)] * 4,
        compiler_params=_params(("arbitrary",)),
    )(x, oa, o1, o2, o3, l1, l2, l3, ga, gb, wo, gf)


def _ffn_up(h2, wgt, wut, *, tm=1024, fc=1408, rc=256):
    T = h2.shape[0]

    def body(h_ref, wg_ref, wu_ref, gate_ref, up_ref, act_ref):
        for s in range(0, tm, rc):
            h = h_ref[s:s + rc, :]
            gt = lax.dot_general(h, wg_ref[...], NT, preferred_element_type=F32)
            u = lax.dot_general(h, wu_ref[...], NT, preferred_element_type=F32)
            gate_ref[s:s + rc, :] = gt.astype(BF16)
            up_ref[s:s + rc, :] = u.astype(BF16)
            act_ref[s:s + rc, :] = (gt * (1.0 / (1.0 + jnp.exp(-gt))) * u).astype(BF16)

    rowd = pl.BlockSpec((tm, D_MODEL), lambda i, c: (i, 0))
    wrow = pl.BlockSpec((fc, D_MODEL), lambda i, c: (c, 0))
    oc = pl.BlockSpec((tm, fc), lambda i, c: (i, c))
    return pl.pallas_call(
        body, name="ffn_up", grid=(T // tm, D_FF // fc),
        in_specs=[rowd, wrow, wrow],
        out_specs=[oc, oc, oc],
        out_shape=[jax.ShapeDtypeStruct((T, D_FF), BF16)] * 3,
        compiler_params=_params(("arbitrary", "arbitrary")),
    )(h2, wgt, wut)


def _ffn_down_loss(act, wd, x2, tgt, g, *, tm=512, rc=256):
    T = x2.shape[0]

    def body(act_ref, wd_ref, x2_ref, tgt_ref, g_ref, dx_ref, dxb_ref, loss_ref, dg_ref):
        @pl.when(pl.program_id(0) == 0)
        def _():
            loss_ref[...] = jnp.zeros_like(loss_ref)
            dg_ref[...] = jnp.zeros_like(dg_ref)

        gv = g_ref[...]
        lsum = jnp.zeros((1, 1), F32)
        dgs = jnp.zeros((1, D_MODEL), F32)
        for c in range(0, tm, rc):
            x3 = x2_ref[c:c + rc, :] + jnp.dot(act_ref[c:c + rc, :], wd_ref[...], preferred_element_type=F32)
            r = lax.rsqrt(jnp.mean(x3 * x3, axis=-1, keepdims=True) + EPS)
            xh = x3 * r
            diff = xh * gv - tgt_ref[c:c + rc, :]
            lsum = lsum + jnp.sum(jnp.sum(diff * diff, axis=-1, keepdims=True), axis=0, keepdims=True)
            dy = diff * (1.0 / D_MODEL)
            dgs = dgs + jnp.sum(dy * xh, axis=0, keepdims=True)
            dx = _rms_bwd(dy, xh, r, gv)
            dx_ref[c:c + rc, :] = dx
            dxb_ref[c:c + rc, :] = dx.astype(BF16)
        loss_ref[...] += lsum * (0.5 / D_MODEL)
        dg_ref[...] += dgs

    rowd = pl.BlockSpec((tm, D_MODEL), lambda i: (i, 0))
    return pl.pallas_call(
        body, name="ffn_down_loss", grid=(T // tm,),
        in_specs=[pl.BlockSpec((tm, D_FF), lambda i: (i, 0)), pl.BlockSpec((D_FF, D_MODEL), lambda i: (0, 0)),
                  rowd, rowd, pl.BlockSpec(g.shape, lambda i: (0, 0))],
        out_specs=[rowd, rowd, pl.BlockSpec((1, 1), lambda i: (0, 0)), pl.BlockSpec((1, D_MODEL), lambda i: (0, 0))],
        out_shape=[jax.ShapeDtypeStruct((T, D_MODEL), F32), jax.ShapeDtypeStruct((T, D_MODEL), BF16),
                   jax.ShapeDtypeStruct((1, 1), F32), jax.ShapeDtypeStruct((1, D_MODEL), F32)],
        compiler_params=_params(("arbitrary",)),
    )(act, wd, x2, tgt, g)


def _ffn_bwd_act(dx3b, gate, up, wd, *, tm=1024, fc=1408, rc=256):
    T = dx3b.shape[0]

    def body(dxb_ref, gate_ref, up_ref, wd_ref, dgate_ref, dup_ref):
        for s in range(0, tm, rc):
            dact = lax.dot_general(dxb_ref[s:s + rc, :], wd_ref[...], NT, preferred_element_type=F32)
            gt = gate_ref[s:s + rc, :].astype(F32)
            u = up_ref[s:s + rc, :].astype(F32)
            sg = 1.0 / (1.0 + jnp.exp(-gt))
            dgate_ref[s:s + rc, :] = (dact * u * sg * (1.0 + gt * (1.0 - sg))).astype(BF16)
            dup_ref[s:s + rc, :] = (dact * gt * sg).astype(BF16)

    rowd = pl.BlockSpec((tm, D_MODEL), lambda i, c: (i, 0))
    oc = pl.BlockSpec((tm, fc), lambda i, c: (i, c))
    return pl.pallas_call(
        body, name="ffn_bwd_act", grid=(T // tm, D_FF // fc),
        in_specs=[rowd, oc, oc, pl.BlockSpec((fc, D_MODEL), lambda i, c: (c, 0))],
        out_specs=[oc, oc],
        out_shape=[jax.ShapeDtypeStruct((T, D_FF), BF16), jax.ShapeDtypeStruct((T, D_FF), BF16)],
        compiler_params=_params(("arbitrary", "arbitrary")),
    )(dx3b, gate, up, wd)


def _ffn_bwd_in(dgate, dup, wgt, wut, x2, dx3, g, *, tm=512, rc=256):
    T = x2.shape[0]

    def body(dgate_ref, dup_ref, wg_ref, wu_ref, x2_ref, dx_ref, g_ref, dx2_ref, dx2b_ref, dg_ref):
        @pl.when(pl.program_id(0) == 0)
        def _():
            dg_ref[...] = jnp.zeros_like(dg_ref)

        gv = g_ref[...]
        dgs = jnp.zeros((1, D_MODEL), F32)
        for s in range(0, tm, rc):
            dh = (jnp.dot(dgate_ref[s:s + rc, :], wg_ref[...], preferred_element_type=F32)
                  + jnp.dot(dup_ref[s:s + rc, :], wu_ref[...], preferred_element_type=F32))
            xv = x2_ref[s:s + rc, :]
            r = lax.rsqrt(jnp.mean(xv * xv, axis=-1, keepdims=True) + EPS)
            xh = xv * r
            dgs = dgs + jnp.sum(dh * xh, axis=0, keepdims=True)
            d = dx_ref[s:s + rc, :] + _rms_bwd(dh, xh, r, gv)
            dx2_ref[s:s + rc, :] = d
            dx2b_ref[s:s + rc, :] = d.astype(BF16)
        dg_ref[...] += dgs

    rowd = pl.BlockSpec((tm, D_MODEL), lambda i: (i, 0))
    rowf = pl.BlockSpec((tm, D_FF), lambda i: (i, 0))
    wfull = pl.BlockSpec((D_FF, D_MODEL), lambda i: (0, 0))
    return pl.pallas_call(
        body, name="ffn_bwd_in", grid=(T // tm,),
        in_specs=[rowf, rowf, wfull, wfull, rowd, rowd, pl.BlockSpec(g.shape, lambda i: (0, 0))],
        out_specs=[rowd, rowd, pl.BlockSpec((1, D_MODEL), lambda i: (0, 0))],
        out_shape=[jax.ShapeDtypeStruct((T, D_MODEL), F32), jax.ShapeDtypeStruct((T, D_MODEL), BF16),
                   jax.ShapeDtypeStruct((1, D_MODEL), F32)],
        compiler_params=_params(("arbitrary",)),
    )(dgate, dup, wgt, wut, x2, dx3, g)


def _matmul_tn(a, b, *, tk, tn, tt=2048, out_dtype=BF16, name):
    T, K = a.shape
    N = b.shape[1]
    nt = T // tt

    def body(a_ref, b_ref, o_ref, acc_ref):
        part = lax.dot_general(a_ref[...], b_ref[...], TN, preferred_element_type=F32)

        @pl.when(pl.program_id(2) == 0)
        def _():
            acc_ref[...] = part

        @pl.when(pl.program_id(2) > 0)
        def _():
            acc_ref[...] += part

        @pl.when(pl.program_id(2) == nt - 1)
        def _():
            o_ref[...] = acc_ref[...].astype(out_dtype)

    return pl.pallas_call(
        body, name=name, grid=(K // tk, N // tn, nt),
        in_specs=[pl.BlockSpec((tt, tk), lambda i, j, t: (t, i)), pl.BlockSpec((tt, tn), lambda i, j, t: (t, j))],
        out_specs=pl.BlockSpec((tk, tn), lambda i, j, t: (i, j)),
        out_shape=jax.ShapeDtypeStruct((K, N), out_dtype),
        scratch_shapes=[pltpu.VMEM((tk, tn), F32)],
        compiler_params=_params(("arbitrary", "arbitrary", "arbitrary")),
    )(a, b)


def _wo_bwd(dx2b, wo, oa, ob, ga, gb, *, tm=512):
    T = dx2b.shape[0]

    def body(dx_ref, wo_ref, oa_ref, ob_ref, ga_ref, gb_ref, doa_ref, dob1_ref, dob4_ref, dob16_ref, dga_ref, dgb_ref, scr):
        @pl.when(pl.program_id(0) == 0)
        def _():
            dga_ref[...] = jnp.zeros_like(dga_ref)
            dgb_ref[...] = jnp.zeros_like(dgb_ref)

        dm = lax.dot_general(dx_ref[...], wo_ref[...], NT, preferred_element_type=F32)
        for o_ref, g_ref, dg_ref, sl in ((oa_ref, ga_ref, dga_ref, slice(0, WIDTH)),
                                         (ob_ref, gb_ref, dgb_ref, slice(WIDTH, 2 * WIDTH))):
            ov = o_ref[...].astype(F32)
            r = lax.rsqrt(jnp.mean(ov * ov, axis=-1, keepdims=True) + EPS)
            xh = ov * r
            d = dm[:, sl]
            dg_ref[...] += jnp.sum(d * xh, axis=0, keepdims=True)
            do = _rms_bwd(d, xh, r, g_ref[...])
            if o_ref is oa_ref:
                doa_ref[...] = do.astype(BF16)
            else:
                _scr_put(scr, do)
                for (_, dil), v_ref in zip(BRANCHES, (dob1_ref, dob4_ref, dob16_ref)):
                    _restride(scr, v_ref, dil, tm)

    row = lambda w_: pl.BlockSpec((tm, w_), lambda i: (i, 0))
    full = lambda a: pl.BlockSpec(a.shape, lambda i: (0, 0))
    return pl.pallas_call(
        body, name="wo_bwd", grid=(T // tm,),
        in_specs=[row(D_MODEL), full(wo), row(WIDTH), row(WIDTH), full(ga), full(gb)],
        out_specs=[row(WIDTH)] + _view_specs(tm)
        + [pl.BlockSpec((1, WIDTH), lambda i: (0, 0)), pl.BlockSpec((1, WIDTH), lambda i: (0, 0))],
        out_shape=[jax.ShapeDtypeStruct((T, WIDTH), BF16)] + _view_shapes(T, BF16)
        + [jax.ShapeDtypeStruct((1, WIDTH), F32), jax.ShapeDtypeStruct((1, WIDTH), F32)],
        scratch_shapes=[_scr(tm)],
        compiler_params=_params(("arbitrary",)),
    )(dx2b, wo, oa, ob, ga, gb)


def _dproj(dqa, dka, dva, dqs, dks, dvs, cos, sin, *, tm=512):
    T = dqa.shape[0]

    def body(dqa_ref, dka_ref, dva_ref, q1, q2, q3, k1, k2, k3, v1, v2, v3, cos_ref, sin_ref, dp_ref, db_ref, acc, tmp):
        @pl.when(pl.program_id(0) == 0)
        def _():
            db_ref[...] = jnp.zeros_like(db_ref)

        cosv = cos_ref[...]
        sinv = sin_ref[...]
        lane = lax.broadcasted_iota(jnp.int32, (tm, PAIR), 1)
        first = (lane % HEAD_DIM) < (HEAD_DIM // 2)

        def put(off, val):
            dp_ref[:, off:off + PAIR] = val.astype(BF16)
            db_ref[:, off:off + PAIR] += jnp.sum(val, axis=0, keepdims=True)

        for src, off, width in ((dqa_ref, 0, 512), (dka_ref, 512, 256)):
            for j in range(0, width, PAIR):
                d = src[:, j:j + PAIR].astype(F32)
                put(off + j, d * cosv - _rope_rot(d, first) * sinv)
        for j in range(0, 256, PAIR):
            put(768 + j, dva_ref[:, j:j + PAIR].astype(F32))
        for (a, b, c), off in (((q1, q2, q3), 1024), ((k1, k2, k3), 1536), ((v1, v2, v3), 2048)):
            _unstride(b, acc, BRANCHES[1][1], tm)
            _unstride(c, tmp, BRANCHES[2][1], tm)
            for j in range(N_CHUNK):
                put(off + j * PAIR, a[:, j * PAIR:(j + 1) * PAIR].astype(F32) + acc[j] + tmp[j])

    row = lambda w_: pl.BlockSpec((tm, w_), lambda i: (i, 0))
    return pl.pallas_call(
        body, name="dproj", grid=(T // tm,),
        in_specs=[row(512), row(256), row(256)] + _view_specs(tm) * 3 + [row(PAIR), row(PAIR)],
        out_specs=[row(D_INP), pl.BlockSpec((1, D_INP), lambda i: (0, 0))],
        out_shape=[jax.ShapeDtypeStruct((T, D_INP), BF16), jax.ShapeDtypeStruct((1, D_INP), F32)],
        scratch_shapes=[_scr(tm)] * 2,
        compiler_params=_params(("arbitrary",)),
    )(dqa, dka, dva, *dqs, *dks, *dvs, cos, sin)


def _inproj_bwd(dp, w, x, dx2, g, *, tm=512):
    T = x.shape[0]

    def body(dp_ref, w_ref, x_ref, dx2_ref, g_ref, gx_ref, dg_ref):
        @pl.when(pl.program_id(0) == 0)
        def _():
            dg_ref[...] = jnp.zeros_like(dg_ref)

        dh = jnp.dot(dp_ref[...], w_ref[...], preferred_element_type=F32)
        xv = x_ref[...]
        r = lax.rsqrt(jnp.mean(xv * xv, axis=-1, keepdims=True) + EPS)
        xh = xv * r
        dg_ref[...] += jnp.sum(dh * xh, axis=0, keepdims=True)
        gx_ref[...] = dx2_ref[...] + _rms_bwd(dh, xh, r, g_ref[...])

    row = lambda w_: pl.BlockSpec((tm, w_), lambda i: (i, 0))
    full = lambda a: pl.BlockSpec(a.shape, lambda i: (0, 0))
    return pl.pallas_call(
        body, name="inproj_bwd", grid=(T // tm,),
        in_specs=[row(D_INP), full(w), row(D_MODEL), row(D_MODEL), full(g)],
        out_specs=[row(D_MODEL), pl.BlockSpec((1, D_MODEL), lambda i: (0, 0))],
        out_shape=[jax.ShapeDtypeStruct((T, D_MODEL), F32), jax.ShapeDtypeStruct((1, D_MODEL), F32)],
        compiler_params=_params(("arbitrary",)),
    )(dp, w, x, dx2, g)


def _bias_sink_grads(dsums, bmaps, dsk):
    def body(s1, s2, s3, m1, m2, m3, dsk_ref, drel_ref, dsink_ref):
        row = lax.broadcasted_iota(jnp.int32, (N_HEADS, 128), 0)
        lane = lax.broadcasted_iota(jnp.int32, (N_HEADS, 128), 1)
        out = jnp.zeros((N_HEADS, 128), F32)
        for s_ref, m_ref in ((s1, m1), (s2, m2), (s3, m3)):
            bm = m_ref[...]
            for h in range(N_HEADS):
                a = s_ref[h]
                for b in range(REL_BUCKETS):
                    v = jnp.sum(jnp.sum(jnp.where(bm == b, a, 0.0), axis=-1, keepdims=True), axis=0, keepdims=True)
                    out = out + jnp.where((row == h) & (lane == b), v, 0.0)
        drel_ref[...] = out
        dsink_ref[...] = -jnp.sum(dsk_ref[...], axis=0, keepdims=True)

    vm = pl.BlockSpec(memory_space=pltpu.VMEM)
    return pl.pallas_call(
        body, name="bias_sink_grads",
        in_specs=[vm] * 7, out_specs=[vm, vm],
        out_shape=[jax.ShapeDtypeStruct((N_HEADS, 128), F32), jax.ShapeDtypeStruct((1, WIDTH), F32)],
        compiler_params=_params(),
    )(*dsums, *bmaps, dsk)


def _all_gather(blk, *, name):
    R, C = blk.shape

    def body(x_ref, out_ref, send_sems, recv_sems, local_sem):
        x, y, c = lax.axis_index("x"), lax.axis_index("y"), lax.axis_index("c")
        me, sibling = (x, y, c), (x, y, 1 - c)
        chips = [(1 - x, y), (x, 1 - y), (1 - x, 1 - y)]

        def slot(px, py, pc):
            return out_ref.at[4 * px + 2 * py + pc]

        def copy(k, block, to, src=None):
            return pltpu.make_async_remote_copy(
                src_ref=slot(*block) if src is None else src, dst_ref=slot(*block),
                send_sem=send_sems.at[k], recv_sem=recv_sems.at[k], device_id=to, device_id_type=MESH)

        mine = pltpu.make_async_copy(x_ref, slot(*me), local_sem)
        mine.start()
        first = [copy(0, me, sibling, src=x_ref)]
        first += [copy(1 + j, me, (*chip, c), src=x_ref) for j, chip in enumerate(chips)]
        for cp in first:
            cp.start()
        passed = [copy(4 + j, (*chip, c), sibling) for j, chip in enumerate(chips)]
        for j, chip in enumerate(chips):
            copy(1 + j, (*chip, c), me).wait_recv()
            passed[j].start()
        copy(0, sibling, me).wait_recv()
        for j, chip in enumerate(chips):
            copy(4 + j, (*chip, 1 - c), me).wait_recv()
        for cp in first + passed:
            cp.wait_send()
        mine.wait()

    return pl.pallas_call(
        body, name=name,
        in_specs=[pl.BlockSpec(memory_space=pl.ANY)], out_specs=pl.BlockSpec(memory_space=pl.ANY),
        out_shape=jax.ShapeDtypeStruct((N_DEV, R, C), blk.dtype),
        scratch_shapes=[pltpu.SemaphoreType.DMA((7,)), pltpu.SemaphoreType.DMA((7,)), pltpu.SemaphoreType.DMA],
        compiler_params=pltpu.CompilerParams(has_side_effects=True),
    )(blk)


def _peers(x, y, c):
    return [(x ^ (k >> 2), y ^ ((k >> 1) & 1), c ^ (k & 1)) for k in range(1, N_DEV)]


_HBM = pl.BlockSpec(memory_space=pltpu.HBM)
_SEM = pl.BlockSpec(memory_space=pltpu.SEMAPHORE)
_EFFECT = pltpu.SideEffectType.DATAFLOW_SIDE_EFFECTING


def _exchange_start(srcs, *, gather, name):
    n = len(srcs)
    lands = [lax.empty((N_DEV,) + s.shape[-2:], s.dtype) for s in srcs]

    def body(*refs):
        src_refs, land_refs = refs[:n], refs[n:2 * n]
        send_sems, recv_sems = refs[2 * n], refs[2 * n + 1]
        token = refs[-1]
        x, y, c = lax.axis_index("x"), lax.axis_index("y"), lax.axis_index("c")
        mine = 4 * x + 2 * y + c
        for a in range(n):
            for k, peer in enumerate(_peers(x, y, c)):
                dest = 4 * peer[0] + 2 * peer[1] + peer[2]
                j = a * (N_DEV - 1) + k
                pltpu.make_async_remote_copy(
                    src_ref=src_refs[a] if gather else src_refs[a].at[dest], dst_ref=land_refs[a].at[mine],
                    send_sem=send_sems.at[j], recv_sem=recv_sems.at[j], device_id=peer, device_id_type=MESH).start()
        token[...] = jnp.zeros_like(token)

    sems = pltpu.SemaphoreType.DMA((n * (N_DEV - 1),))
    out = pl.pallas_call(
        body, name=name,
        out_shape=(sems, sems) + tuple(pltpu.HBM(a.shape, a.dtype) for a in list(srcs) + lands)
        + (jax.ShapeDtypeStruct((8, 128), F32),),
        in_specs=(_HBM,) * (2 * n), out_specs=(_SEM, _SEM) + (_HBM,) * (2 * n) + (pl.BlockSpec(memory_space=pltpu.VMEM),),
        input_output_aliases={i: 2 + i for i in range(2 * n)},
        compiler_params=pltpu.CompilerParams(has_side_effects=_EFFECT),
    )(*[pltpu.with_memory_space_constraint(a, pltpu.HBM) for a in list(srcs) + lands])
    return out[:-1], out[-1]


def _exchange_wait(state, after, *, gather, name):
    send_sems, recv_sems = state[0], state[1]
    n = (len(state) - 2) // 2
    arrays = state[2:]

    def body(*refs):
        src_refs, land_refs = refs[:n], refs[n:2 * n]
        send_sems, recv_sems = refs[2 * n], refs[2 * n + 1]
        x, y, c = lax.axis_index("x"), lax.axis_index("y"), lax.axis_index("c")
        for a in range(n):
            for k, peer in enumerate(_peers(x, y, c)):
                other = 4 * peer[0] + 2 * peer[1] + peer[2]
                j = a * (N_DEV - 1) + k
                copy = pltpu.make_async_remote_copy(
                    src_ref=src_refs[a] if gather else src_refs[a].at[other], dst_ref=land_refs[a].at[other],
                    send_sem=send_sems.at[j], recv_sem=recv_sems.at[j], device_id=peer, device_id_type=MESH)
                copy.wait_send()
                copy.wait_recv()

    out = pl.pallas_call(
        body, name=name,
        out_shape=tuple(pltpu.HBM(a.shape, a.dtype) for a in arrays),
        in_specs=(_HBM,) * (2 * n) + (_SEM, _SEM, pl.BlockSpec(memory_space=pl.ANY)), out_specs=(_HBM,) * (2 * n),
        input_output_aliases={i: i for i in range(2 * n)},
        compiler_params=pltpu.CompilerParams(has_side_effects=_EFFECT),
    )(*arrays, send_sems, recv_sems, after)
    return out[n:]


def _fill_own(got, own):
    mine = 4 * lax.axis_index("x") + 2 * lax.axis_index("y") + lax.axis_index("c")
    return lax.dynamic_update_slice(got, own[None], (mine, 0, 0))


def _adam_math(w, g, m, v):
    m = ADAM_B1 * m + (1.0 - ADAM_B1) * g
    v = ADAM_B2 * v + (1.0 - ADAM_B2) * (g * g)
    m_hat = m / (1.0 - ADAM_B1 ** ADAM_STEP)
    v_hat = v / (1.0 - ADAM_B2 ** ADAM_STEP)
    delta = -ADAM_LR * (m_hat / (jnp.sqrt(v_hat) + ADAM_EPS) + ADAM_WD * w)
    return delta, m, v


def _sum_parts(parts, *, name):
    _, R, C = parts.shape
    tr = R // 2
    assert tr % 16 == 0

    def body(p_ref, g_ref):
        g = p_ref[0].astype(F32)
        for s in range(1, N_DEV):
            g = g + p_ref[s].astype(F32)
        g_ref[...] = g

    return pl.pallas_call(
        body, name=name, grid=(R // tr,),
        in_specs=[pl.BlockSpec((N_DEV, tr, C), lambda i: (0, i, 0))], out_specs=pl.BlockSpec((tr, C), lambda i: (i, 0)),
        out_shape=jax.ShapeDtypeStruct((R, C), F32), compiler_params=_params(("arbitrary",)),
    )(parts)


def _adamw(parts, w, m, v, *, name):
    R, C = w.shape
    n_parts = parts.shape[0]
    tr = R // 2
    assert tr % 16 == 0

    def body(p_ref, w_ref, m_ref, v_ref, g_ref, d_ref, nm_ref, nv_ref):
        g = p_ref[0].astype(F32)
        for s in range(1, n_parts):
            g = g + p_ref[s].astype(F32)
        d, nm, nv = _adam_math(w_ref[...], g, m_ref[...], v_ref[...])
        g_ref[...] = g
        d_ref[...] = d
        nm_ref[...] = nm
        nv_ref[...] = nv

    blk = pl.BlockSpec((tr, C), lambda i: (i, 0))
    return pl.pallas_call(
        body, name=name, grid=(R // tr,),
        in_specs=[pl.BlockSpec((n_parts, tr, C), lambda i: (0, i, 0)), blk, blk, blk],
        out_specs=[blk] * 4, out_shape=[jax.ShapeDtypeStruct((R, C), F32)] * 4,
        compiler_params=_params(("arbitrary",)),
    )(parts, w, m, v)


def _adamw_small(parts, w, m, v):
    def body(p_ref, w_ref, m_ref, v_ref, g_ref, d_ref, nm_ref, nv_ref):
        g = p_ref[0]
        for s in range(1, N_DEV):
            g = g + p_ref[s]
        d, nm, nv = _adam_math(w_ref[...], g, m_ref[...], v_ref[...])
        g_ref[...] = g
        d_ref[...] = d
        nm_ref[...] = nm
        nv_ref[...] = nv

    vm = pl.BlockSpec(memory_space=pltpu.VMEM)
    return pl.pallas_call(
        body, name="adamw_small", in_specs=[vm] * 4, out_specs=[vm] * 4,
        out_shape=[jax.ShapeDtypeStruct((SMALL_ROWS, 128), F32)] * 4, compiler_params=_params(),
    )(parts, w, m, v)


def _t5_bucket(dist):
    max_exact = REL_BUCKETS // 2
    df = jnp.maximum(dist, 1).astype(F32)
    large = max_exact + (jnp.log(df / max_exact) / math.log(REL_MAX_DISTANCE / max_exact)
                         * (REL_BUCKETS - max_exact)).astype(jnp.int32)
    large = jnp.minimum(large, REL_BUCKETS - 1)
    return jnp.where(dist < max_exact, dist, large)


def _band_tables(rel_table, dil, n_back):
    qi = jnp.arange(BLK)[:, None]
    kj = jnp.arange(2 * BLK)[None, :]
    delta = BLK + qi - kj
    in_band = (delta >= 0) & (delta <= n_back)
    if rel_table is None:
        vals = jnp.zeros((N_HEADS, BLK, 2 * BLK), F32)
        bmap = None
    else:
        bucket = _t5_bucket(jnp.clip(delta, 0, n_back) * dil)
        vals = jnp.zeros((N_HEADS, BLK, 2 * BLK), F32)
        for b in range(REL_BUCKETS):
            vals = jnp.where((bucket == b)[None], rel_table[b][:, None, None], vals)
        bmap = jnp.where(in_band, bucket, -1).astype(jnp.int32)
    later = jnp.where(in_band[None], vals, NEG)
    first = jnp.where((in_band & (kj >= BLK))[None], vals, NEG)
    return jnp.stack([later, first]), bmap


def _rope_tables(T):
    half = HEAD_DIM // 2
    inv_freq = ROPE_THETA ** (-jnp.arange(half, dtype=F32) / half)
    ang = jnp.arange(T, dtype=F32)[:, None] * inv_freq[None, :]
    cos, sin = jnp.cos(ang), jnp.sin(ang)
    return jnp.tile(cos, (1, 4)), jnp.tile(jnp.concatenate([-sin, sin], axis=1), (1, 2))


def _widen_in(a, axis):
    sl = lambda lo, hi: lax.slice_in_dim(a, lo, hi, axis=axis)
    dup = lambda lo: [sl(lo, lo + 64), sl(lo, lo + 64), sl(lo + 64, lo + 128), sl(lo + 64, lo + 128)]
    return jnp.concatenate([sl(0, 512)] + dup(512) + dup(640) + [sl(768, D_IN)], axis=axis)


def _fold_in(a, axis):
    sl = lambda lo, hi: lax.slice_in_dim(a, lo, hi, axis=axis)
    fold = lambda lo: [sl(lo, lo + 64) + sl(lo + 64, lo + 128), sl(lo + 128, lo + 192) + sl(lo + 192, lo + 256)]
    return jnp.concatenate([sl(0, 512)] + fold(512) + fold(768) + [sl(1024, D_INP)], axis=axis)


def _local_step(x, tgt, g_attn, wint, b_in, sinks, rel_table, g_out_a, g_out_b, g_ffn, g_final, token,
                wo_fn, ffn_fn, early_fn):
    T = x.shape[0]
    cos, sin = _rope_tables(T)
    cos = cos + token[0, 0]
    winp = _widen_in(wint, 0)
    binp = _widen_in(b_in, 1)
    g_final2 = g_final.reshape(1, D_MODEL)
    sink8 = sinks.reshape(N_HEADS)

    bias_a, _ = _band_tables(None, 1, BLK - 1)
    tabs = [_band_tables(rel_table, dil, window // dil) for window, dil in BRANCHES]

    h1, qa, ka, va, *qkv_b = _norm_proj(x, g_attn, winp, binp, cos, sin)
    qbs, kbs, vbs = qkv_b[0:3], qkv_b[3:6], qkv_b[6:9]
    oa, lse_a = _attn_fwd(qa, ka, va, bias_a, sink8, dil=1, kv_pairs=2, use_sink=True, name="attn_a_fwd")
    outs = [_attn_fwd(qbs[n], kbs[n], vbs[n], tabs[n][0], sink8, dil=dil, kv_pairs=4, use_sink=False,
                      name=f"attn_b{n}_fwd") for n, (_, dil) in enumerate(BRANCHES)]
    wo = wo_fn(outs[2][1])
    x2, mixed, h2, *ob_lse = _merge_wo(x, oa, outs[0][0], outs[1][0], outs[2][0], outs[0][1], outs[1][1], outs[2][1],
                                       g_out_a, g_out_b, wo, g_ffn)
    obs, lses = ob_lse[0:3], ob_lse[3:6]
    wgt, wut, wd = ffn_fn(h2)
    gate, up, act = _ffn_up(h2, wgt, wut)
    dx3, dx3b, loss, dg_final = _ffn_down_loss(act, wd, x2, tgt, g_final2)

    dgate, dup = _ffn_bwd_act(dx3b, gate, up, wd)
    dx2, dx2b, dg_ffn = _ffn_bwd_in(dgate, dup, wgt, wut, x2, dx3, g_ffn)
    dwd = _matmul_tn(act, dx3b, tk=1408, tn=1024, name="dw_down")
    dwgt = _matmul_tn(dgate, h2, tk=1408, tn=1024, name="dw_gate")
    dwut = _matmul_tn(dup, h2, tk=1408, tn=1024, name="dw_up")
    dwo = _matmul_tn(mixed, dx2b, tk=1024, tn=1024, name="dw_o")
    early, token2 = early_fn(dict(w_o=dwo, w_gate=dwgt, w_up=dwut, w_down=dwd))
    doa, *dobs, dg_out_a, dg_out_b = _wo_bwd(dx2b, wo, oa, obs[0], g_out_a + token2[0, 0], g_out_b)

    dqa, dka, dva, _, dsk = _attn_bwd(qa, ka, va, oa, doa, lse_a, bias_a, sink8, dil=1, kv_pairs=2, use_sink=True,
                                      name="attn_a_bwd")
    res = [_attn_bwd(qbs[n], kbs[n], vbs[n], obs[n], dobs[n], lses[n], tabs[n][0], sink8, dil=dil, kv_pairs=4,
                     use_sink=False, name=f"attn_b{n}_bwd") for n, (_, dil) in enumerate(BRANCHES)]
    dp, dbp = _dproj(dqa, dka, dva, [r[0] for r in res], [r[1] for r in res], [r[2] for r in res], cos, sin)
    grad_x, dg_attn = _inproj_bwd(dp, winp, x, dx2, g_attn)
    dwin = _fold_in(_matmul_tn(dp, h1, tk=1280, tn=1024, out_dtype=F32, name="dw_in"), 0)
    drel, dsink = _bias_sink_grads([r[3] for r in res], [t[1] for t in tabs], dsk)

    small = dict(
        g_attn=dg_attn, b_in=_fold_in(dbp, 1), sinks=dsink[:, ::HEAD_DIM], rel_table=drel[:, :REL_BUCKETS].T,
        g_out_a=dg_out_a, g_out_b=dg_out_b, g_ffn=dg_ffn, g_final=dg_final.reshape(D_MODEL))
    return loss[0, 0], grad_x, dwin, early, small


SMALL_NAMES = ("g_attn", "b_in", "sinks", "rel_table", "g_out_a", "g_out_b", "g_ffn", "g_final")


def _pack_small(vals):
    flat = jnp.concatenate([vals[n].reshape(-1).astype(F32) for n in SMALL_NAMES])
    return jnp.pad(flat, (0, SMALL_ROWS * 128 - flat.shape[0])).reshape(SMALL_ROWS, 128)


def _unpack_small(packed, like):
    flat = packed.reshape(-1)
    out, off = {}, 0
    for n in SMALL_NAMES:
        size = like[n].size
        out[n] = flat[off:off + size].reshape(like[n].shape)
        off += size
    return out


def kernel(x, g_attn, w_in, b_in, sinks, rel_table, g_out_a, g_out_b, w_o, g_ffn, w_gate, w_up, w_down, g_final, loss_target, m_g_attn, m_w_in, m_b_in, m_sinks, m_rel_table, m_g_out_a, m_g_out_b, m_w_o, m_g_ffn, m_w_gate, m_w_up, m_w_down, m_g_final, v_g_attn, v_w_in, v_b_in, v_sinks, v_rel_table, v_g_out_a, v_g_out_b, v_w_o, v_g_ffn, v_w_gate, v_w_up, v_w_down, v_g_final):
    mine = 4 * lax.axis_index("x") + 2 * lax.axis_index("y") + lax.axis_index("c")
    rest_names = ("w_o", "w_gate", "w_up", "w_down")

    rest = [w_o[0].astype(BF16), w_gate[0].astype(BF16).T, w_up[0].astype(BF16).T, w_down[0].astype(BF16)]
    wint = _all_gather(w_in[0].astype(BF16).T, name="gather_w_in").reshape(D_IN, D_MODEL)
    wint, rest = lax.optimization_barrier((wint, rest))
    wo_state, token_o = _exchange_start(rest[:1], gather=True, name="gather_w_o_start")
    token_o, ffn_src = lax.optimization_barrier((token_o, rest[1:]))
    ffn_state, token = _exchange_start(ffn_src, gather=True, name="gather_ffn_start")
    token = token + token_o

    def whole(got, own):
        return [_fill_own(g, o).reshape(N_DEV * o.shape[0], D_MODEL) for g, o in zip(got, own)]

    def wo_fn(after):
        return whole(_exchange_wait(wo_state, after, gather=True, name="gather_w_o_wait"), rest[:1])[0]

    def ffn_fn(after):
        return whole(_exchange_wait(ffn_state, after, gather=True, name="gather_ffn_wait"), ffn_src)

    def early_fn(dws):
        parts = [dws[n].reshape(N_DEV, -1, D_MODEL) for n in rest_names]
        own = [lax.dynamic_index_in_dim(p, mine, 0, keepdims=False) for p in parts]
        state, token2 = _exchange_start(parts, gather=False, name="scatter_rest_start")
        return (state, own), token2

    loss_part, grad_x, dwint, (early_state, early_own), small = _local_step(
        x[0], loss_target[0], g_attn, wint, b_in, sinks, rel_table, g_out_a, g_out_b, g_ffn, g_final, token,
        wo_fn, ffn_fn, early_fn)
    loss = lax.psum(loss_part, ("x", "y", "c"))

    parts_in = dwint.astype(BF16).reshape(N_DEV, D_IN // N_DEV, D_MODEL)
    own_in = lax.dynamic_index_in_dim(parts_in, mine, 0, keepdims=False)
    in_state, token3 = _exchange_start([parts_in], gather=False, name="scatter_w_in_start")
    got = [_fill_own(g, own) for g, own in
           zip(_exchange_wait(early_state, token3, gather=False, name="scatter_rest_wait"), early_own)]

    def update(n, parts, w, m, v, transposed):
        if transposed:
            parts = _sum_parts(parts, name="sum_" + n).T[None]
        return [a[None] for a in _adamw(parts, w[0], m[0], v[0], name="adamw_" + n)]

    big = dict(w_o=update("w_o", got[0], w_o, m_w_o, v_w_o, False),
               w_gate=update("w_gate", got[1], w_gate, m_w_gate, v_w_gate, True),
               w_up=update("w_up", got[2], w_up, m_w_up, v_w_up, True),
               w_down=update("w_down", got[3], w_down, m_w_down, v_w_down, False))

    ws = dict(g_attn=g_attn, b_in=b_in, sinks=sinks, rel_table=rel_table, g_out_a=g_out_a, g_out_b=g_out_b,
              g_ffn=g_ffn, g_final=g_final)
    ms = dict(g_attn=m_g_attn, b_in=m_b_in, sinks=m_sinks, rel_table=m_rel_table, g_out_a=m_g_out_a,
              g_out_b=m_g_out_b, g_ffn=m_g_ffn, g_final=m_g_final)
    vs = dict(g_attn=v_g_attn, b_in=v_b_in, sinks=v_sinks, rel_table=v_rel_table, g_out_a=v_g_out_a,
              g_out_b=v_g_out_b, g_ffn=v_g_ffn, g_final=v_g_final)
    sparts = _all_gather(_pack_small(small), name="gather_small")
    sm_packed = _adamw_small(sparts, _pack_small(ws), _pack_small(ms), _pack_small(vs))
    sm = [_unpack_small(a, ws) for a in sm_packed]

    done = sm_packed[1][:1, :1] + sum(big[n][1][0, :1, :1] for n in rest_names)
    got_in = _fill_own(_exchange_wait(in_state, done, gather=False, name="scatter_w_in_wait")[0], own_in)
    big["w_in"] = update("w_in", got_in, w_in, m_w_in, v_w_in, True)

    order = ("g_attn", "w_in", "b_in", "sinks", "rel_table", "g_out_a", "g_out_b", "w_o", "g_ffn", "w_gate", "w_up",
             "w_down", "g_final")
    outs = [loss, grad_x[None]]
    for k in range(4):
        outs += [big[n][k] if n in big else sm[k][n] for n in order]
    return tuple(outs)
```

```python
import functools
import math

import jax
import jax.numpy as jnp
from jax import lax
from jax.experimental import pallas as pl
from jax.experimental.pallas import tpu as pltpu

F32 = jnp.float32
BF16 = jnp.bfloat16

N_DEV = 8
D_MODEL = 1024
HEAD_DIM = 64
N_HEADS = 8
PAIR = 2 * HEAD_DIM
WIDTH = N_HEADS * HEAD_DIM
D_IN = 2304
D_INP = 2560
D_FF = 2816
BLK = 128
ROPE_THETA = 150000.0
REL_BUCKETS = 32
REL_MAX_DISTANCE = 2048
EPS = 1e-5
NEG = -1e30
BRANCHES = ((128, 1), (512, 4), (2048, 16))
Q_SCALE = HEAD_DIM ** -0.5

ADAM_LR = 0.001
ADAM_B1 = 0.9
ADAM_B2 = 0.999
ADAM_EPS = 1e-08
ADAM_WD = 0.01
ADAM_STEP = 10

VMEM_LIMIT = 56 * 1024 * 1024
MESH = pl.DeviceIdType.MESH

NT = (((1,), (1,)), ((), ()))
TN = (((0,), (0,)), ((), ()))

SMALL_ROWS = 56


def _params(sem=None):
    return pltpu.CompilerParams(dimension_semantics=sem, vmem_limit_bytes=VMEM_LIMIT)


def _rms_bwd(dh, xh, r, g):
    u = dh * g
    return r * (u - xh * jnp.mean(u * xh, axis=-1, keepdims=True))


def _rope_rot(t, first):
    return jnp.where(first, pltpu.roll(t, 96, 1), pltpu.roll(t, 32, 1))


N_CHUNK = WIDTH // PAIR


def _scr(tm):
    return pltpu.VMEM((N_CHUNK, tm, PAIR), F32)


def _scr_get(scr):
    return jnp.concatenate([scr[j] for j in range(N_CHUNK)], axis=1)


def _scr_put(scr, val):
    for j in range(N_CHUNK):
        scr[j] = val[:, j * PAIR:(j + 1) * PAIR]


def _unstride(view_ref, scr, dil, tm):
    n = tm // dil
    for r in range(dil):
        for j in range(N_CHUNK):
            col = r * WIDTH + j * PAIR
            scr.at[j][pl.ds(r, n, stride=dil), :] = view_ref[:, col:col + PAIR].astype(F32)


def _restride(scr, out_ref, dil, tm):
    n = tm // dil
    for r in range(dil):
        for j in range(N_CHUNK):
            col = r * WIDTH + j * PAIR
            rows = scr[j] if dil == 1 else scr.at[j][pl.ds(r, n, stride=dil), :]
            out_ref[:, col:col + PAIR] = rows.astype(out_ref.dtype)


def _view_specs(tm):
    return [pl.BlockSpec((tm // dil, dil * WIDTH), lambda i: (i, 0)) for _, dil in BRANCHES]


def _view_shapes(T, dtype):
    return [jax.ShapeDtypeStruct((T // dil, dil * WIDTH), dtype) for _, dil in BRANCHES]


def _norm_proj(x, g, w, b, cos, sin, *, tm=512):
    T = x.shape[0]

    def body(x_ref, g_ref, w_ref, b_ref, cos_ref, sin_ref, h_ref, qa_ref, ka_ref, va_ref, *rest):
        outs_b, ys = rest[:9], rest[9]
        xv = x_ref[...]
        r = lax.rsqrt(jnp.mean(xv * xv, axis=-1, keepdims=True) + EPS)
        h = (xv * r * g_ref[...]).astype(BF16)
        h_ref[...] = h
        cosv = cos_ref[...]
        sinv = sin_ref[...]
        lane = lax.broadcasted_iota(jnp.int32, (tm, PAIR), 1)
        first = (lane % HEAD_DIM) < (HEAD_DIM // 2)

        def proj(off):
            return (lax.dot_general(h, w_ref[off:off + 256, :], NT, preferred_element_type=F32)
                    + b_ref[:, off:off + 256])

        for (off, width, rot, scale), o_ref in zip(((0, 512, True, Q_SCALE), (512, 256, True, 1.0), (768, 256, False, 1.0)),
                                                   (qa_ref, ka_ref, va_ref)):
            for c in range(0, width, 256):
                y = proj(off + c)
                for j in range(0, 256, PAIR):
                    t = y[:, j:j + PAIR]
                    if rot:
                        t = t * cosv + _rope_rot(t, first) * sinv
                    if scale != 1.0:
                        t = t * scale
                    o_ref[:, c + j:c + j + PAIR] = t.astype(BF16)
        for n, (off, scale) in enumerate(((1024, Q_SCALE), (1536, 1.0), (2048, 1.0))):
            for c in range(0, WIDTH, 256):
                y = proj(off + c)
                y = y * scale if scale != 1.0 else y
                for j in range(0, 256, PAIR):
                    ys[(c + j) // PAIR] = y[:, j:j + PAIR]
            for (_, dil), o_ref in zip(BRANCHES, outs_b[3 * n:3 * n + 3]):
                _restride(ys, o_ref, dil, tm)

    row = lambda w_: pl.BlockSpec((tm, w_), lambda i: (i, 0))
    full = lambda a: pl.BlockSpec(a.shape, lambda i: (0, 0))
    return pl.pallas_call(
        body, name="norm_proj", grid=(T // tm,),
        in_specs=[row(D_MODEL), full(g), full(w), full(b), row(PAIR), row(PAIR)],
        out_specs=[row(D_MODEL), row(512), row(256), row(256)] + _view_specs(tm) * 3,
        out_shape=[jax.ShapeDtypeStruct((T, n), BF16) for n in (D_MODEL, 512, 256, 256)] + _view_shapes(T, BF16) * 3,
        scratch_shapes=[_scr(tm)],
        compiler_params=_params(("arbitrary",)),
    )(x, g, w, b, cos, sin)


SUB = 4
AHEAD = 2


def _attn_specs(kvw):
    q_spec = pl.BlockSpec((SUB * BLK, WIDTH), lambda r, i: (i, r))
    kc_spec = pl.BlockSpec((SUB * BLK, kvw), lambda r, i: (i, r))
    kp_spec = pl.BlockSpec((BLK, kvw), lambda r, i: (jnp.maximum(SUB * i - 1, 0), r))
    b_spec = pl.BlockSpec((2, N_HEADS, BLK, 2 * BLK), lambda r, i: (0, 0, 0, 0))
    return q_spec, kp_spec, kc_spec, b_spec


def _window(prev_ref, cur_ref, j, ksl):
    before = prev_ref[:, ksl] if j == 0 else cur_ref[(j - 1) * BLK:j * BLK, ksl]
    return jnp.concatenate([before, cur_ref[j * BLK:(j + 1) * BLK, ksl]], axis=0)


def _attn_fwd(q, k, v, bias, sinks, *, dil, kv_pairs, use_sink, name):
    L = q.shape[0]
    ns = L // (SUB * BLK)
    kvw = kv_pairs * PAIR
    rep = 4 // kv_pairs

    def body(sink_ref, q_ref, kp_ref, kc_ref, vp_ref, vc_ref, b_ref, o_ref, lse_ref):
        lane = lax.broadcasted_iota(jnp.int32, (1, PAIR), 1)
        lo = lane < HEAD_DIM
        first = jnp.where(pl.program_id(1) == 0, 1, 0)
        def scores(j, hp):
            rows = slice(j * BLK, (j + 1) * BLK)
            sl = slice(hp * PAIR, (hp + 1) * PAIR)
            ksl = slice((hp // rep) * PAIR, (hp // rep + 1) * PAIR)
            qp = q_ref[rows, sl]
            kk = _window(kp_ref, kc_ref, j, ksl)
            vv = _window(vp_ref, vc_ref, j, ksl)
            heads = []
            for e in range(2):
                h = 2 * hp + e
                msk = lo if e == 0 else jnp.logical_not(lo)
                qm = jnp.where(msk, qp, jnp.zeros_like(qp))
                s = lax.dot_general(qm, kk, NT, preferred_element_type=F32) + (b_ref[first, h] if j == 0 else b_ref[0, h])
                heads.append((h, msk, s))
            return rows, sl, vv, heads

        def outputs(rows, sl, vv, heads):
            o_pair = None
            lse_pair = None
            for h, msk, s in heads:
                m = jnp.max(s, axis=-1, keepdims=True)
                if use_sink:
                    sk = sink_ref[h]
                    m = jnp.maximum(m, sk)
                p = jnp.exp(s - m)
                l = jnp.sum(p, axis=-1, keepdims=True)
                if use_sink:
                    l = l + jnp.exp(sk - m)
                vm = jnp.where(msk, vv, jnp.zeros_like(vv))
                oe = jnp.dot(p.astype(BF16), vm, preferred_element_type=F32) * (1.0 / l)
                ls = m + jnp.log(l)
                if o_pair is None:
                    o_pair = oe
                    lse_pair = jnp.broadcast_to(ls, (BLK, PAIR))
                else:
                    o_pair = o_pair + oe
                    lse_pair = jnp.where(lo, lse_pair, ls)
            o_ref[rows, sl] = o_pair.astype(BF16)
            lse_ref[rows, sl] = lse_pair

        items = [(j, hp) for j in range(SUB) for hp in range(4)]
        queue = [scores(*it) for it in items[:AHEAD]]
        for n in range(len(items)):
            if n + AHEAD < len(items):
                queue.append(scores(*items[n + AHEAD]))
            outputs(*queue.pop(0))

    q_spec, kp_spec, kc_spec, b_spec = _attn_specs(kvw)
    return pl.pallas_call(
        body, name=name, grid=(dil, ns),
        in_specs=[pl.BlockSpec(memory_space=pltpu.SMEM), q_spec, kp_spec, kc_spec, kp_spec, kc_spec, b_spec],
        out_specs=[q_spec, q_spec],
        out_shape=[jax.ShapeDtypeStruct((L, dil * WIDTH), BF16), jax.ShapeDtypeStruct((L, dil * WIDTH), F32)],
        compiler_params=_params(("arbitrary", "arbitrary")),
    )(sinks, q, k, k, v, v, bias)


def _attn_bwd(q, k, v, o, do, lse, bias, sinks, *, dil, kv_pairs, use_sink, name):
    L = q.shape[0]
    ns = L // (SUB * BLK)
    n_steps = dil * ns
    kvw = kv_pairs * PAIR
    rep = 4 // kv_pairs
    last = slice((SUB - 1) * BLK, SUB * BLK)

    def body(sink_ref, q_ref, kp_ref, kc_ref, vp_ref, vc_ref, o_ref, do_ref, lse_ref, b_ref,
             dq_ref, dk_ref, dv_ref, dsum_ref, dsk_ref, pk_ref, pv_ref):
        t = pl.program_id(0)
        i = t % ns

        @pl.when(t == 0)
        def _():
            dsum_ref[...] = jnp.zeros_like(dsum_ref)
            dsk_ref[...] = jnp.zeros_like(dsk_ref)
            pk_ref[...] = jnp.zeros_like(pk_ref)
            pv_ref[...] = jnp.zeros_like(pv_ref)

        @pl.when(t < n_steps)
        def _():
            lo = lax.broadcasted_iota(jnp.int32, (1, PAIR), 1) < HEAD_DIM
            first = jnp.where(i == 0, 1, 0)
            dks = [[None] * kv_pairs for _ in range(SUB)]
            dvs = [[None] * kv_pairs for _ in range(SUB)]
            def scores(j, hp):
                rows = slice(j * BLK, (j + 1) * BLK)
                kvp = hp // rep
                sl = slice(hp * PAIR, (hp + 1) * PAIR)
                ksl = slice(kvp * PAIR, (kvp + 1) * PAIR)
                qp = q_ref[rows, sl]
                dop = do_ref[rows, sl]
                prod = dop.astype(F32) * o_ref[rows, sl].astype(F32)
                kk = _window(kp_ref, kc_ref, j, ksl)
                vv = _window(vp_ref, vc_ref, j, ksl)
                heads = []
                for e in range(2):
                    h = 2 * hp + e
                    msk = lo if e == 0 else jnp.logical_not(lo)
                    qm = jnp.where(msk, qp, jnp.zeros_like(qp))
                    dom = jnp.where(msk, dop, jnp.zeros_like(dop))
                    km = jnp.where(msk, kk, jnp.zeros_like(kk))
                    s = (lax.dot_general(qm, kk, NT, preferred_element_type=F32)
                         + (b_ref[first, h] if j == 0 else b_ref[0, h]))
                    dp = lax.dot_general(dom, vv, NT, preferred_element_type=F32)
                    heads.append((h, msk, qm, dom, km, s, dp))
                return j, rows, kvp, sl, prod, heads

            def grads(j, rows, kvp, sl, prod, heads):
                dq_pair = None
                c_pair = None
                qms, doms, dsbs, pbs = [], [], [], []
                for h, msk, qm, dom, km, s, dp in heads:
                    ls = lse_ref[rows, h * HEAD_DIM:h * HEAD_DIM + 1]
                    p = jnp.exp(s - ls)
                    delta = jnp.sum(jnp.where(msk, prod, 0.0), axis=-1, keepdims=True)
                    ds = p * (dp - delta)
                    if use_sink:
                        ce = jnp.exp(sink_ref[h] - ls) * delta
                        c_pair = jnp.broadcast_to(ce, (BLK, PAIR)) if c_pair is None else jnp.where(msk, ce, c_pair)
                    else:
                        dsum_ref[h] += ds
                    dsb = ds.astype(BF16)
                    dqe = jnp.dot(dsb, km, preferred_element_type=F32)
                    dq_pair = dqe if dq_pair is None else dq_pair + dqe
                    qms.append(qm)
                    doms.append(dom)
                    dsbs.append(dsb)
                    pbs.append(p.astype(BF16))
                dke = lax.dot_general(jnp.concatenate(dsbs, axis=0), jnp.concatenate(qms, axis=0), TN,
                                      preferred_element_type=F32)
                dve = lax.dot_general(jnp.concatenate(pbs, axis=0), jnp.concatenate(doms, axis=0), TN,
                                      preferred_element_type=F32)
                dks[j][kvp] = dke if dks[j][kvp] is None else dks[j][kvp] + dke
                dvs[j][kvp] = dve if dvs[j][kvp] is None else dvs[j][kvp] + dve
                dq_ref[rows, sl] = (dq_pair * Q_SCALE).astype(BF16)
                if use_sink:
                    dsk_ref[:, sl] += c_pair

            items = [(j, hp) for j in range(SUB) for hp in range(4)]
            ahead = AHEAD + 1 if use_sink else AHEAD
            queue = [scores(*it) for it in items[:ahead]]
            for n in range(len(items)):
                if n + ahead < len(items):
                    queue.append(scores(*items[n + ahead]))
                grads(*queue.pop(0))
            for kvp in range(kv_pairs):
                ksl = slice(kvp * PAIR, (kvp + 1) * PAIR)
                for pend_ref, out_ref, parts in ((pk_ref, dk_ref, [d[kvp] for d in dks]),
                                                 (pv_ref, dv_ref, [d[kvp] for d in dvs])):
                    if SUB > 1:
                        out_ref[:(SUB - 1) * BLK, ksl] = pend_ref[:(SUB - 1) * BLK, ksl].astype(BF16)
                    out_ref[last, ksl] = (pend_ref[last, ksl] + parts[0][:BLK]).astype(BF16)
                    for j in range(SUB):
                        own = parts[j][BLK:]
                        pend_ref[j * BLK:(j + 1) * BLK, ksl] = own + parts[j + 1][:BLK] if j + 1 < SUB else own

        @pl.when(t == n_steps)
        def _():
            dk_ref[...] = pk_ref[...].astype(BF16)
            dv_ref[...] = pv_ref[...].astype(BF16)

    def at(t):
        t = jnp.minimum(t, n_steps - 1)
        return t % ns, t // ns

    def before(t):
        return at(jnp.maximum(t - 1, 0))

    q_spec = pl.BlockSpec((SUB * BLK, WIDTH), at)
    kc_spec = pl.BlockSpec((SUB * BLK, kvw), at)
    kp_spec = pl.BlockSpec((BLK, kvw), lambda t: (jnp.maximum(SUB * at(t)[0] - 1, 0), at(t)[1]))
    b_spec = pl.BlockSpec((2, N_HEADS, BLK, 2 * BLK), lambda t: (0, 0, 0, 0))
    dkv_spec = pl.BlockSpec((SUB * BLK, kvw), before)
    return pl.pallas_call(
        body, name=name, grid=(n_steps + 1,),
        in_specs=[pl.BlockSpec(memory_space=pltpu.SMEM), q_spec, kp_spec, kc_spec, kp_spec, kc_spec,
                  q_spec, q_spec, q_spec, b_spec],
        out_specs=[q_spec, dkv_spec, dkv_spec,
                   pl.BlockSpec((N_HEADS, BLK, 2 * BLK), lambda t: (0, 0, 0)),
                   pl.BlockSpec((BLK, WIDTH), lambda t: (0, 0))],
        out_shape=[jax.ShapeDtypeStruct((L, dil * WIDTH), BF16),
                   jax.ShapeDtypeStruct((L, dil * kvw), BF16),
                   jax.ShapeDtypeStruct((L, dil * kvw), BF16),
                   jax.ShapeDtypeStruct((N_HEADS, BLK, 2 * BLK), F32),
                   jax.ShapeDtypeStruct((BLK, WIDTH), F32)],
        scratch_shapes=[pltpu.VMEM((SUB * BLK, kvw), F32), pltpu.VMEM((SUB * BLK, kvw), F32)],
        compiler_params=_params(("arbitrary",)),
    )(sinks, q, k, k, v, v, o, do, lse, bias)


def _merge_wo(x, oa, o1, o2, o3, l1, l2, l3, ga, gb, wo, gf, *, tm=512):
    T = x.shape[0]

    def body(x_ref, oa_ref, o1_ref, o2_ref, o3_ref, l1_ref, l2_ref, l3_ref, ga_ref, gb_ref, wo_ref, gf_ref,
             x2_ref, mix_ref, h2_ref, ob1_ref, ob4_ref, ob16_ref, ls1_ref, ls4_ref, ls16_ref, so2, so3, sl2, sl3):
        _unstride(o2_ref, so2, BRANCHES[1][1], tm)
        _unstride(o3_ref, so3, BRANCHES[2][1], tm)
        _unstride(l2_ref, sl2, BRANCHES[1][1], tm)
        _unstride(l3_ref, sl3, BRANCHES[2][1], tm)
        la, lb, lc = l1_ref[...], _scr_get(sl2), _scr_get(sl3)
        m = jnp.maximum(jnp.maximum(la, lb), lc)
        ea, eb, ec = jnp.exp(la - m), jnp.exp(lb - m), jnp.exp(lc - m)
        den = ea + eb + ec
        inv = 1.0 / den
        ob = (ea * o1_ref[...].astype(F32) + eb * _scr_get(so2) + ec * _scr_get(so3)) * inv
        _scr_put(so2, ob)
        _scr_put(sl2, m + jnp.log(den))
        for (_, dil), o_ref, l_ref in zip(BRANCHES, (ob1_ref, ob4_ref, ob16_ref), (ls1_ref, ls4_ref, ls16_ref)):
            _restride(so2, o_ref, dil, tm)
            _restride(sl2, l_ref, dil, tm)
        oav = oa_ref[...].astype(F32)
        ra = lax.rsqrt(jnp.mean(oav * oav, axis=-1, keepdims=True) + EPS)
        rb = lax.rsqrt(jnp.mean(ob * ob, axis=-1, keepdims=True) + EPS)
        mix_ref[:, :WIDTH] = (oav * ra * ga_ref[...]).astype(BF16)
        mix_ref[:, WIDTH:] = (ob * rb * gb_ref[...]).astype(BF16)
        x2 = x_ref[...] + jnp.dot(mix_ref[...], wo_ref[...], preferred_element_type=F32)
        x2_ref[...] = x2
        r2 = lax.rsqrt(jnp.mean(x2 * x2, axis=-1, keepdims=True) + EPS)
        h2_ref[...] = (x2 * r2 * gf_ref[...]).astype(BF16)

    row = lambda w_: pl.BlockSpec((tm, w_), lambda i: (i, 0))
    full = lambda a: pl.BlockSpec(a.shape, lambda i: (0, 0))
    return pl.pallas_call(
        body, name="merge_wo", grid=(T // tm,),
        in_specs=[row(D_MODEL), row(WIDTH)] + _view_specs(tm) * 2 + [full(ga), full(gb), full(wo), full(gf)],
        out_specs=[row(D_MODEL), row(D_MODEL), row(D_MODEL)] + _view_specs(tm) * 2,
        out_shape=[jax.ShapeDtypeStruct((T, D_MODEL), F32), jax.ShapeDtypeStruct((T, D_MODEL), BF16),
                   jax.ShapeDtypeStruct((T, D_MODEL), BF16)] + _view_shapes(T, BF16) + _view_shapes(T, F32),
        scratch_shapes=[_scr(tm)] * 4,
        compiler_params=_params(("arbitrary",)),
    )(x, oa, o1, o2, o3, l1, l2, l3, ga, gb, wo, gf)


def _ffn_up(h2, wgt, wut, *, tm=1024, fc=1408, rc=256):
    T = h2.shape[0]

    def body(h_ref, wg_ref, wu_ref, gate_ref, up_ref, act_ref):
        for s in range(0, tm, rc):
            h = h_ref[s:s + rc, :]
            gt = lax.dot_general(h, wg_ref[...], NT, preferred_element_type=F32)
            u = lax.dot_general(h, wu_ref[...], NT, preferred_element_type=F32)
            gate_ref[s:s + rc, :] = gt.astype(BF16)
            up_ref[s:s + rc, :] = u.astype(BF16)
            act_ref[s:s + rc, :] = (gt * (1.0 / (1.0 + jnp.exp(-gt))) * u).astype(BF16)

    rowd = pl.BlockSpec((tm, D_MODEL), lambda i, c: (i, 0))
    wrow = pl.BlockSpec((fc, D_MODEL), lambda i, c: (c, 0))
    oc = pl.BlockSpec((tm, fc), lambda i, c: (i, c))
    return pl.pallas_call(
        body, name="ffn_up", grid=(T // tm, D_FF // fc),
        in_specs=[rowd, wrow, wrow],
        out_specs=[oc, oc, oc],
        out_shape=[jax.ShapeDtypeStruct((T, D_FF), BF16)] * 3,
        compiler_params=_params(("arbitrary", "arbitrary")),
    )(h2, wgt, wut)


def _ffn_down_loss(act, wd, x2, tgt, g, *, tm=512, rc=256):
    T = x2.shape[0]

    def body(act_ref, wd_ref, x2_ref, tgt_ref, g_ref, dx_ref, dxb_ref, loss_ref, dg_ref):
        @pl.when(pl.program_id(0) == 0)
        def _():
            loss_ref[...] = jnp.zeros_like(loss_ref)
            dg_ref[...] = jnp.zeros_like(dg_ref)

        gv = g_ref[...]
        lsum = jnp.zeros((1, 1), F32)
        dgs = jnp.zeros((1, D_MODEL), F32)
        for c in range(0, tm, rc):
            x3 = x2_ref[c:c + rc, :] + jnp.dot(act_ref[c:c + rc, :], wd_ref[...], preferred_element_type=F32)
            r = lax.rsqrt(jnp.mean(x3 * x3, axis=-1, keepdims=True) + EPS)
            xh = x3 * r
            diff = xh * gv - tgt_ref[c:c + rc, :]
            lsum = lsum + jnp.sum(jnp.sum(diff * diff, axis=-1, keepdims=True), axis=0, keepdims=True)
            dy = diff * (1.0 / D_MODEL)
            dgs = dgs + jnp.sum(dy * xh, axis=0, keepdims=True)
            dx = _rms_bwd(dy, xh, r, gv)
            dx_ref[c:c + rc, :] = dx
            dxb_ref[c:c + rc, :] = dx.astype(BF16)
        loss_ref[...] += lsum * (0.5 / D_MODEL)
        dg_ref[...] += dgs

    rowd = pl.BlockSpec((tm, D_MODEL), lambda i: (i, 0))
    return pl.pallas_call(
        body, name="ffn_down_loss", grid=(T // tm,),
        in_specs=[pl.BlockSpec((tm, D_FF), lambda i: (i, 0)), pl.BlockSpec((D_FF, D_MODEL), lambda i: (0, 0)),
                  rowd, rowd, pl.BlockSpec(g.shape, lambda i: (0, 0))],
        out_specs=[rowd, rowd, pl.BlockSpec((1, 1), lambda i: (0, 0)), pl.BlockSpec((1, D_MODEL), lambda i: (0, 0))],
        out_shape=[jax.ShapeDtypeStruct((T, D_MODEL), F32), jax.ShapeDtypeStruct((T, D_MODEL), BF16),
                   jax.ShapeDtypeStruct((1, 1), F32), jax.ShapeDtypeStruct((1, D_MODEL), F32)],
        compiler_params=_params(("arbitrary",)),
    )(act, wd, x2, tgt, g)


def _ffn_bwd_act(dx3b, gate, up, wd, *, tm=1024, fc=1408, rc=256):
    T = dx3b.shape[0]

    def body(dxb_ref, gate_ref, up_ref, wd_ref, dgate_ref, dup_ref):
        for s in range(0, tm, rc):
            dact = lax.dot_general(dxb_ref[s:s + rc, :], wd_ref[...], NT, preferred_element_type=F32)
            gt = gate_ref[s:s + rc, :].astype(F32)
            u = up_ref[s:s + rc, :].astype(F32)
            sg = 1.0 / (1.0 + jnp.exp(-gt))
            dgate_ref[s:s + rc, :] = (dact * u * sg * (1.0 + gt * (1.0 - sg))).astype(BF16)
            dup_ref[s:s + rc, :] = (dact * gt * sg).astype(BF16)

    rowd = pl.BlockSpec((tm, D_MODEL), lambda i, c: (i, 0))
    oc = pl.BlockSpec((tm, fc), lambda i, c: (i, c))
    return pl.pallas_call(
        body, name="ffn_bwd_act", grid=(T // tm, D_FF // fc),
        in_specs=[rowd, oc, oc, pl.BlockSpec((fc, D_MODEL), lambda i, c: (c, 0))],
        out_specs=[oc, oc],
        out_shape=[jax.ShapeDtypeStruct((T, D_FF), BF16), jax.ShapeDtypeStruct((T, D_FF), BF16)],
        compiler_params=_params(("arbitrary", "arbitrary")),
    )(dx3b, gate, up, wd)


def _ffn_bwd_in(dgate, dup, wgt, wut, x2, dx3, g, *, tm=512, rc=256):
    T = x2.shape[0]

    def body(dgate_ref, dup_ref, wg_ref, wu_ref, x2_ref, dx_ref, g_ref, dx2_ref, dx2b_ref, dg_ref):
        @pl.when(pl.program_id(0) == 0)
        def _():
            dg_ref[...] = jnp.zeros_like(dg_ref)

        gv = g_ref[...]
        dgs = jnp.zeros((1, D_MODEL), F32)
        for s in range(0, tm, rc):
            dh = (jnp.dot(dgate_ref[s:s + rc, :], wg_ref[...], preferred_element_type=F32)
                  + jnp.dot(dup_ref[s:s + rc, :], wu_ref[...], preferred_element_type=F32))
            xv = x2_ref[s:s + rc, :]
            r = lax.rsqrt(jnp.mean(xv * xv, axis=-1, keepdims=True) + EPS)
            xh = xv * r
            dgs = dgs + jnp.sum(dh * xh, axis=0, keepdims=True)
            d = dx_ref[s:s + rc, :] + _rms_bwd(dh, xh, r, gv)
            dx2_ref[s:s + rc, :] = d
            dx2b_ref[s:s + rc, :] = d.astype(BF16)
        dg_ref[...] += dgs

    rowd = pl.BlockSpec((tm, D_MODEL), lambda i: (i, 0))
    rowf = pl.BlockSpec((tm, D_FF), lambda i: (i, 0))
    wfull = pl.BlockSpec((D_FF, D_MODEL), lambda i: (0, 0))
    return pl.pallas_call(
        body, name="ffn_bwd_in", grid=(T // tm,),
        in_specs=[rowf, rowf, wfull, wfull, rowd, rowd, pl.BlockSpec(g.shape, lambda i: (0, 0))],
        out_specs=[rowd, rowd, pl.BlockSpec((1, D_MODEL), lambda i: (0, 0))],
        out_shape=[jax.ShapeDtypeStruct((T, D_MODEL), F32), jax.ShapeDtypeStruct((T, D_MODEL), BF16),
                   jax.ShapeDtypeStruct((1, D_MODEL), F32)],
        compiler_params=_params(("arbitrary",)),
    )(dgate, dup, wgt, wut, x2, dx3, g)


def _matmul_tn(a, b, *, tk, tn, tt=2048, out_dtype=BF16, name):
    T, K = a.shape
    N = b.shape[1]
    nt = T // tt

    def body(a_ref, b_ref, o_ref, acc_ref):
        part = lax.dot_general(a_ref[...], b_ref[...], TN, preferred_element_type=F32)

        @pl.when(pl.program_id(2) == 0)
        def _():
            acc_ref[...] = part

        @pl.when(pl.program_id(2) > 0)
        def _():
            acc_ref[...] += part

        @pl.when(pl.program_id(2) == nt - 1)
        def _():
            o_ref[...] = acc_ref[...].astype(out_dtype)

    return pl.pallas_call(
        body, name=name, grid=(K // tk, N // tn, nt),
        in_specs=[pl.BlockSpec((tt, tk), lambda i, j, t: (t, i)), pl.BlockSpec((tt, tn), lambda i, j, t: (t, j))],
        out_specs=pl.BlockSpec((tk, tn), lambda i, j, t: (i, j)),
        out_shape=jax.ShapeDtypeStruct((K, N), out_dtype),
        scratch_shapes=[pltpu.VMEM((tk, tn), F32)],
        compiler_params=_params(("arbitrary", "arbitrary", "arbitrary")),
    )(a, b)


def _wo_bwd(dx2b, wo, oa, ob, ga, gb, *, tm=512):
    T = dx2b.shape[0]

    def body(dx_ref, wo_ref, oa_ref, ob_ref, ga_ref, gb_ref, doa_ref, dob1_ref, dob4_ref, dob16_ref, dga_ref, dgb_ref, scr):
        @pl.when(pl.program_id(0) == 0)
        def _():
            dga_ref[...] = jnp.zeros_like(dga_ref)
            dgb_ref[...] = jnp.zeros_like(dgb_ref)

        dm = lax.dot_general(dx_ref[...], wo_ref[...], NT, preferred_element_type=F32)
        for o_ref, g_ref, dg_ref, sl in ((oa_ref, ga_ref, dga_ref, slice(0, WIDTH)),
                                         (ob_ref, gb_ref, dgb_ref, slice(WIDTH, 2 * WIDTH))):
            ov = o_ref[...].astype(F32)
            r = lax.rsqrt(jnp.mean(ov * ov, axis=-1, keepdims=True) + EPS)
            xh = ov * r
            d = dm[:, sl]
            dg_ref[...] += jnp.sum(d * xh, axis=0, keepdims=True)
            do = _rms_bwd(d, xh, r, g_ref[...])
            if o_ref is oa_ref:
                doa_ref[...] = do.astype(BF16)
            else:
                _scr_put(scr, do)
                for (_, dil), v_ref in zip(BRANCHES, (dob1_ref, dob4_ref, dob16_ref)):
                    _restride(scr, v_ref, dil, tm)

    row = lambda w_: pl.BlockSpec((tm, w_), lambda i: (i, 0))
    full = lambda a: pl.BlockSpec(a.shape, lambda i: (0, 0))
    return pl.pallas_call(
        body, name="wo_bwd", grid=(T // tm,),
        in_specs=[row(D_MODEL), full(wo), row(WIDTH), row(WIDTH), full(ga), full(gb)],
        out_specs=[row(WIDTH)] + _view_specs(tm)
        + [pl.BlockSpec((1, WIDTH), lambda i: (0, 0)), pl.BlockSpec((1, WIDTH), lambda i: (0, 0))],
        out_shape=[jax.ShapeDtypeStruct((T, WIDTH), BF16)] + _view_shapes(T, BF16)
        + [jax.ShapeDtypeStruct((1, WIDTH), F32), jax.ShapeDtypeStruct((1, WIDTH), F32)],
        scratch_shapes=[_scr(tm)],
        compiler_params=_params(("arbitrary",)),
    )(dx2b, wo, oa, ob, ga, gb)


def _dproj(dqa, dka, dva, dqs, dks, dvs, cos, sin, *, tm=512):
    T = dqa.shape[0]

    def body(dqa_ref, dka_ref, dva_ref, q1, q2, q3, k1, k2, k3, v1, v2, v3, cos_ref, sin_ref, dp_ref, db_ref, acc, tmp):
        @pl.when(pl.program_id(0) == 0)
        def _():
            db_ref[...] = jnp.zeros_like(db_ref)

        cosv = cos_ref[...]
        sinv = sin_ref[...]
        lane = lax.broadcasted_iota(jnp.int32, (tm, PAIR), 1)
        first = (lane % HEAD_DIM) < (HEAD_DIM // 2)

        def put(off, val):
            dp_ref[:, off:off + PAIR] = val.astype(BF16)
            db_ref[:, off:off + PAIR] += jnp.sum(val, axis=0, keepdims=True)

        for src, off, width in ((dqa_ref, 0, 512), (dka_ref, 512, 256)):
            for j in range(0, width, PAIR):
                d = src[:, j:j + PAIR].astype(F32)
                put(off + j, d * cosv - _rope_rot(d, first) * sinv)
        for j in range(0, 256, PAIR):
            put(768 + j, dva_ref[:, j:j + PAIR].astype(F32))
        for (a, b, c), off in (((q1, q2, q3), 1024), ((k1, k2, k3), 1536), ((v1, v2, v3), 2048)):
            _unstride(b, acc, BRANCHES[1][1], tm)
            _unstride(c, tmp, BRANCHES[2][1], tm)
            for j in range(N_CHUNK):
                put(off + j * PAIR, a[:, j * PAIR:(j + 1) * PAIR].astype(F32) + acc[j] + tmp[j])

    row = lambda w_: pl.BlockSpec((tm, w_), lambda i: (i, 0))
    return pl.pallas_call(
        body, name="dproj", grid=(T // tm,),
        in_specs=[row(512), row(256), row(256)] + _view_specs(tm) * 3 + [row(PAIR), row(PAIR)],
        out_specs=[row(D_INP), pl.BlockSpec((1, D_INP), lambda i: (0, 0))],
        out_shape=[jax.ShapeDtypeStruct((T, D_INP), BF16), jax.ShapeDtypeStruct((1, D_INP), F32)],
        scratch_shapes=[_scr(tm)] * 2,
        compiler_params=_params(("arbitrary",)),
    )(dqa, dka, dva, *dqs, *dks, *dvs, cos, sin)


def _inproj_bwd(dp, w, x, dx2, g, *, tm=512):
    T = x.shape[0]

    def body(dp_ref, w_ref, x_ref, dx2_ref, g_ref, gx_ref, dg_ref):
        @pl.when(pl.program_id(0) == 0)
        def _():
            dg_ref[...] = jnp.zeros_like(dg_ref)

        dh = jnp.dot(dp_ref[...], w_ref[...], preferred_element_type=F32)
        xv = x_ref[...]
        r = lax.rsqrt(jnp.mean(xv * xv, axis=-1, keepdims=True) + EPS)
        xh = xv * r
        dg_ref[...] += jnp.sum(dh * xh, axis=0, keepdims=True)
        gx_ref[...] = dx2_ref[...] + _rms_bwd(dh, xh, r, g_ref[...])

    row = lambda w_: pl.BlockSpec((tm, w_), lambda i: (i, 0))
    full = lambda a: pl.BlockSpec(a.shape, lambda i: (0, 0))
    return pl.pallas_call(
        body, name="inproj_bwd", grid=(T // tm,),
        in_specs=[row(D_INP), full(w), row(D_MODEL), row(D_MODEL), full(g)],
        out_specs=[row(D_MODEL), pl.BlockSpec((1, D_MODEL), lambda i: (0, 0))],
        out_shape=[jax.ShapeDtypeStruct((T, D_MODEL), F32), jax.ShapeDtypeStruct((1, D_MODEL), F32)],
        compiler_params=_params(("arbitrary",)),
    )(dp, w, x, dx2, g)


def _bias_sink_grads(dsums, bmaps, dsk):
    def body(s1, s2, s3, m1, m2, m3, dsk_ref, drel_ref, dsink_ref):
        row = lax.broadcasted_iota(jnp.int32, (N_HEADS, 128), 0)
        lane = lax.broadcasted_iota(jnp.int32, (N_HEADS, 128), 1)
        out = jnp.zeros((N_HEADS, 128), F32)
        for s_ref, m_ref in ((s1, m1), (s2, m2), (s3, m3)):
            bm = m_ref[...]
            for h in range(N_HEADS):
                a = s_ref[h]
                for b in range(REL_BUCKETS):
                    v = jnp.sum(jnp.sum(jnp.where(bm == b, a, 0.0), axis=-1, keepdims=True), axis=0, keepdims=True)
                    out = out + jnp.where((row == h) & (lane == b), v, 0.0)
        drel_ref[...] = out
        dsink_ref[...] = -jnp.sum(dsk_ref[...], axis=0, keepdims=True)

    vm = pl.BlockSpec(memory_space=pltpu.VMEM)
    return pl.pallas_call(
        body, name="bias_sink_grads",
        in_specs=[vm] * 7, out_specs=[vm, vm],
        out_shape=[jax.ShapeDtypeStruct((N_HEADS, 128), F32), jax.ShapeDtypeStruct((1, WIDTH), F32)],
        compiler_params=_params(),
    )(*dsums, *bmaps, dsk)


def _all_gather(blk, *, name):
    R, C = blk.shape

    def body(x_ref, out_ref, send_sems, recv_sems, local_sem):
        x, y, c = lax.axis_index("x"), lax.axis_index("y"), lax.axis_index("c")
        me, sibling = (x, y, c), (x, y, 1 - c)
        chips = [(1 - x, y), (x, 1 - y), (1 - x, 1 - y)]

        def slot(px, py, pc):
            return out_ref.at[4 * px + 2 * py + pc]

        def copy(k, block, to, src=None):
            return pltpu.make_async_remote_copy(
                src_ref=slot(*block) if src is None else src, dst_ref=slot(*block),
                send_sem=send_sems.at[k], recv_sem=recv_sems.at[k], device_id=to, device_id_type=MESH)

        mine = pltpu.make_async_copy(x_ref, slot(*me), local_sem)
        mine.start()
        first = [copy(0, me, sibling, src=x_ref)]
        first += [copy(1 + j, me, (*chip, c), src=x_ref) for j, chip in enumerate(chips)]
        for cp in first:
            cp.start()
        passed = [copy(4 + j, (*chip, c), sibling) for j, chip in enumerate(chips)]
        for j, chip in enumerate(chips):
            copy(1 + j, (*chip, c), me).wait_recv()
            passed[j].start()
        copy(0, sibling, me).wait_recv()
        for j, chip in enumerate(chips):
            copy(4 + j, (*chip, 1 - c), me).wait_recv()
        for cp in first + passed:
            cp.wait_send()
        mine.wait()

    return pl.pallas_call(
        body, name=name,
        in_specs=[pl.BlockSpec(memory_space=pl.ANY)], out_specs=pl.BlockSpec(memory_space=pl.ANY),
        out_shape=jax.ShapeDtypeStruct((N_DEV, R, C), blk.dtype),
        scratch_shapes=[pltpu.SemaphoreType.DMA((7,)), pltpu.SemaphoreType.DMA((7,)), pltpu.SemaphoreType.DMA],
        compiler_params=pltpu.CompilerParams(has_side_effects=True),
    )(blk)


def _peers(x, y, c):
    return [(x ^ (k >> 2), y ^ ((k >> 1) & 1), c ^ (k & 1)) for k in range(1, N_DEV)]


_HBM = pl.BlockSpec(memory_space=pltpu.HBM)
_SEM = pl.BlockSpec(memory_space=pltpu.SEMAPHORE)
_EFFECT = pltpu.SideEffectType.DATAFLOW_SIDE_EFFECTING


def _exchange_start(srcs, *, gather, name):
    n = len(srcs)
    lands = [lax.empty((N_DEV,) + s.shape[-2:], s.dtype) for s in srcs]

    def body(*refs):
        src_refs, land_refs = refs[:n], refs[n:2 * n]
        send_sems, recv_sems = refs[2 * n], refs[2 * n + 1]
        token = refs[-1]
        x, y, c = lax.axis_index("x"), lax.axis_index("y"), lax.axis_index("c")
        mine = 4 * x + 2 * y + c
        for a in range(n):
            for k, peer in enumerate(_peers(x, y, c)):
                dest = 4 * peer[0] + 2 * peer[1] + peer[2]
                j = a * (N_DEV - 1) + k
                pltpu.make_async_remote_copy(
                    src_ref=src_refs[a] if gather else src_refs[a].at[dest], dst_ref=land_refs[a].at[mine],
                    send_sem=send_sems.at[j], recv_sem=recv_sems.at[j], device_id=peer, device_id_type=MESH).start()
        token[...] = jnp.zeros_like(token)

    sems = pltpu.SemaphoreType.DMA((n * (N_DEV - 1),))
    out = pl.pallas_call(
        body, name=name,
        out_shape=(sems, sems) + tuple(pltpu.HBM(a.shape, a.dtype) for a in list(srcs) + lands)
        + (jax.ShapeDtypeStruct((8, 128), F32),),
        in_specs=(_HBM,) * (2 * n), out_specs=(_SEM, _SEM) + (_HBM,) * (2 * n) + (pl.BlockSpec(memory_space=pltpu.VMEM),),
        input_output_aliases={i: 2 + i for i in range(2 * n)},
        compiler_params=pltpu.CompilerParams(has_side_effects=_EFFECT),
    )(*[pltpu.with_memory_space_constraint(a, pltpu.HBM) for a in list(srcs) + lands])
    return out[:-1], out[-1]


def _exchange_wait(state, after, *, gather, name):
    send_sems, recv_sems = state[0], state[1]
    n = (len(state) - 2) // 2
    arrays = state[2:]

    def body(*refs):
        src_refs, land_refs = refs[:n], refs[n:2 * n]
        send_sems, recv_sems = refs[2 * n], refs[2 * n + 1]
        x, y, c = lax.axis_index("x"), lax.axis_index("y"), lax.axis_index("c")
        for a in range(n):
            for k, peer in enumerate(_peers(x, y, c)):
                other = 4 * peer[0] + 2 * peer[1] + peer[2]
                j = a * (N_DEV - 1) + k
                copy = pltpu.make_async_remote_copy(
                    src_ref=src_refs[a] if gather else src_refs[a].at[other], dst_ref=land_refs[a].at[other],
                    send_sem=send_sems.at[j], recv_sem=recv_sems.at[j], device_id=peer, device_id_type=MESH)
                copy.wait_send()
                copy.wait_recv()

    out = pl.pallas_call(
        body, name=name,
        out_shape=tuple(pltpu.HBM(a.shape, a.dtype) for a in arrays),
        in_specs=(_HBM,) * (2 * n) + (_SEM, _SEM, pl.BlockSpec(memory_space=pl.ANY)), out_specs=(_HBM,) * (2 * n),
        input_output_aliases={i: i for i in range(2 * n)},
        compiler_params=pltpu.CompilerParams(has_side_effects=_EFFECT),
    )(*arrays, send_sems, recv_sems, after)
    mine = 4 * lax.axis_index("x") + 2 * lax.axis_index("y") + lax.axis_index("c")
    own = out[:n] if gather else [lax.dynamic_index_in_dim(s, mine, 0, keepdims=False) for s in out[:n]]
    return [lax.dynamic_update_slice(g, o[None], (mine, 0, 0)) for g, o in zip(out[n:], own)]


def _adam_math(w, g, m, v):
    m = ADAM_B1 * m + (1.0 - ADAM_B1) * g
    v = ADAM_B2 * v + (1.0 - ADAM_B2) * (g * g)
    m_hat = m / (1.0 - ADAM_B1 ** ADAM_STEP)
    v_hat = v / (1.0 - ADAM_B2 ** ADAM_STEP)
    delta = -ADAM_LR * (m_hat / (jnp.sqrt(v_hat) + ADAM_EPS) + ADAM_WD * w)
    return delta, m, v


def _adamw(parts, w, m, v, *, name):
    R, C = w.shape
    n_parts = parts.shape[0]
    tr = R // 2
    assert tr % 16 == 0

    def body(p_ref, w_ref, m_ref, v_ref, g_ref, d_ref, nm_ref, nv_ref):
        g = p_ref[0].astype(F32)
        for s in range(1, n_parts):
            g = g + p_ref[s].astype(F32)
        d, nm, nv = _adam_math(w_ref[...], g, m_ref[...], v_ref[...])
        g_ref[...] = g
        d_ref[...] = d
        nm_ref[...] = nm
        nv_ref[...] = nv

    blk = pl.BlockSpec((tr, C), lambda i: (i, 0))
    return pl.pallas_call(
        body, name=name, grid=(R // tr,),
        in_specs=[pl.BlockSpec((n_parts, tr, C), lambda i: (0, i, 0)), blk, blk, blk],
        out_specs=[blk] * 4, out_shape=[jax.ShapeDtypeStruct((R, C), F32)] * 4,
        compiler_params=_params(("arbitrary",)),
    )(parts, w, m, v)


def _adamw_small(parts, w, m, v):
    def body(p_ref, w_ref, m_ref, v_ref, g_ref, d_ref, nm_ref, nv_ref):
        g = p_ref[0]
        for s in range(1, N_DEV):
            g = g + p_ref[s]
        d, nm, nv = _adam_math(w_ref[...], g, m_ref[...], v_ref[...])
        g_ref[...] = g
        d_ref[...] = d
        nm_ref[...] = nm
        nv_ref[...] = nv

    vm = pl.BlockSpec(memory_space=pltpu.VMEM)
    return pl.pallas_call(
        body, name="adamw_small", in_specs=[vm] * 4, out_specs=[vm] * 4,
        out_shape=[jax.ShapeDtypeStruct((SMALL_ROWS, 128), F32)] * 4, compiler_params=_params(),
    )(parts, w, m, v)


def _t5_bucket(dist):
    max_exact = REL_BUCKETS // 2
    df = jnp.maximum(dist, 1).astype(F32)
    large = max_exact + (jnp.log(df / max_exact) / math.log(REL_MAX_DISTANCE / max_exact)
                         * (REL_BUCKETS - max_exact)).astype(jnp.int32)
    large = jnp.minimum(large, REL_BUCKETS - 1)
    return jnp.where(dist < max_exact, dist, large)


def _band_tables(rel_table, dil, n_back):
    qi = jnp.arange(BLK)[:, None]
    kj = jnp.arange(2 * BLK)[None, :]
    delta = BLK + qi - kj
    in_band = (delta >= 0) & (delta <= n_back)
    if rel_table is None:
        vals = jnp.zeros((N_HEADS, BLK, 2 * BLK), F32)
        bmap = None
    else:
        bucket = _t5_bucket(jnp.clip(delta, 0, n_back) * dil)
        vals = jnp.zeros((N_HEADS, BLK, 2 * BLK), F32)
        for b in range(REL_BUCKETS):
            vals = jnp.where((bucket == b)[None], rel_table[b][:, None, None], vals)
        bmap = jnp.where(in_band, bucket, -1).astype(jnp.int32)
    later = jnp.where(in_band[None], vals, NEG)
    first = jnp.where((in_band & (kj >= BLK))[None], vals, NEG)
    return jnp.stack([later, first]), bmap


def _rope_tables(T):
    half = HEAD_DIM // 2
    inv_freq = ROPE_THETA ** (-jnp.arange(half, dtype=F32) / half)
    ang = jnp.arange(T, dtype=F32)[:, None] * inv_freq[None, :]
    cos, sin = jnp.cos(ang), jnp.sin(ang)
    return jnp.tile(cos, (1, 4)), jnp.tile(jnp.concatenate([-sin, sin], axis=1), (1, 2))


def _widen_in(a, axis):
    sl = lambda lo, hi: lax.slice_in_dim(a, lo, hi, axis=axis)
    dup = lambda lo: [sl(lo, lo + 64), sl(lo, lo + 64), sl(lo + 64, lo + 128), sl(lo + 64, lo + 128)]
    return jnp.concatenate([sl(0, 512)] + dup(512) + dup(640) + [sl(768, D_IN)], axis=axis)


def _fold_in(a, axis):
    sl = lambda lo, hi: lax.slice_in_dim(a, lo, hi, axis=axis)
    fold = lambda lo: [sl(lo, lo + 64) + sl(lo + 64, lo + 128), sl(lo + 128, lo + 192) + sl(lo + 192, lo + 256)]
    return jnp.concatenate([sl(0, 512)] + fold(512) + fold(768) + [sl(1024, D_INP)], axis=axis)


def _local_step(x, tgt, g_attn, wint, b_in, sinks, rel_table, g_out_a, g_out_b, g_ffn, g_final, token,
                wo_fn, ffn_fn, early_fn):
    T = x.shape[0]
    cos, sin = _rope_tables(T)
    cos = cos + token[0, 0]
    winp = _widen_in(wint, 0)
    binp = _widen_in(b_in, 1)
    g_final2 = g_final.reshape(1, D_MODEL)
    sink8 = sinks.reshape(N_HEADS)

    bias_a, _ = _band_tables(None, 1, BLK - 1)
    tabs = [_band_tables(rel_table, dil, window // dil) for window, dil in BRANCHES]

    h1, qa, ka, va, *qkv_b = _norm_proj(x, g_attn, winp, binp, cos, sin)
    qbs, kbs, vbs = qkv_b[0:3], qkv_b[3:6], qkv_b[6:9]
    oa, lse_a = _attn_fwd(qa, ka, va, bias_a, sink8, dil=1, kv_pairs=2, use_sink=True, name="attn_a_fwd")
    outs = [_attn_fwd(qbs[n], kbs[n], vbs[n], tabs[n][0], sink8, dil=dil, kv_pairs=4, use_sink=False,
                      name=f"attn_b{n}_fwd") for n, (_, dil) in enumerate(BRANCHES)]
    wo = wo_fn(outs[2][1])
    x2, mixed, h2, *ob_lse = _merge_wo(x, oa, outs[0][0], outs[1][0], outs[2][0], outs[0][1], outs[1][1], outs[2][1],
                                       g_out_a, g_out_b, wo, g_ffn)
    obs, lses = ob_lse[0:3], ob_lse[3:6]
    wgt, wut, wd = ffn_fn(h2)
    gate, up, act = _ffn_up(h2, wgt, wut)
    dx3, dx3b, loss, dg_final = _ffn_down_loss(act, wd, x2, tgt, g_final2)

    dgate, dup = _ffn_bwd_act(dx3b, gate, up, wd)
    dx2, dx2b, dg_ffn = _ffn_bwd_in(dgate, dup, wgt, wut, x2, dx3, g_ffn)
    dwd = _matmul_tn(act, dx3b, tk=1408, tn=1024, name="dw_down")
    dwgt = _matmul_tn(dgate, h2, tk=1408, tn=1024, name="dw_gate")
    dwut = _matmul_tn(dup, h2, tk=1408, tn=1024, name="dw_up")
    dwo = _matmul_tn(mixed, dx2b, tk=1024, tn=1024, name="dw_o")
    early, token2 = early_fn(dict(w_o=dwo, w_gate=dwgt, w_up=dwut, w_down=dwd))
    doa, *dobs, dg_out_a, dg_out_b = _wo_bwd(dx2b, wo, oa, obs[0], g_out_a + token2[0, 0], g_out_b)

    dqa, dka, dva, _, dsk = _attn_bwd(qa, ka, va, oa, doa, lse_a, bias_a, sink8, dil=1, kv_pairs=2, use_sink=True,
                                      name="attn_a_bwd")
    res = [_attn_bwd(qbs[n], kbs[n], vbs[n], obs[n], dobs[n], lses[n], tabs[n][0], sink8, dil=dil, kv_pairs=4,
                     use_sink=False, name=f"attn_b{n}_bwd") for n, (_, dil) in enumerate(BRANCHES)]
    dp, dbp = _dproj(dqa, dka, dva, [r[0] for r in res], [r[1] for r in res], [r[2] for r in res], cos, sin)
    grad_x, dg_attn = _inproj_bwd(dp, winp, x, dx2, g_attn)
    dwin = _fold_in(_matmul_tn(dp, h1, tk=1280, tn=1024, out_dtype=F32, name="dw_in"), 0)
    drel, dsink = _bias_sink_grads([r[3] for r in res], [t[1] for t in tabs], dsk)

    small = dict(
        g_attn=dg_attn, b_in=_fold_in(dbp, 1), sinks=dsink[:, ::HEAD_DIM], rel_table=drel[:, :REL_BUCKETS].T,
        g_out_a=dg_out_a, g_out_b=dg_out_b, g_ffn=dg_ffn, g_final=dg_final.reshape(D_MODEL))
    return loss[0, 0], grad_x, dwin, early, small


SMALL_NAMES = ("g_attn", "b_in", "sinks", "rel_table", "g_out_a", "g_out_b", "g_ffn", "g_final")


def _pack_small(vals):
    flat = jnp.concatenate([vals[n].reshape(-1).astype(F32) for n in SMALL_NAMES])
    return jnp.pad(flat, (0, SMALL_ROWS * 128 - flat.shape[0])).reshape(SMALL_ROWS, 128)


def _unpack_small(packed, like):
    flat = packed.reshape(-1)
    out, off = {}, 0
    for n in SMALL_NAMES:
        size = like[n].size
        out[n] = flat[off:off + size].reshape(like[n].shape)
        off += size
    return out


def kernel(x, g_attn, w_in, b_in, sinks, rel_table, g_out_a, g_out_b, w_o, g_ffn, w_gate, w_up, w_down, g_final, loss_target, m_g_attn, m_w_in, m_b_in, m_sinks, m_rel_table, m_g_out_a, m_g_out_b, m_w_o, m_g_ffn, m_w_gate, m_w_up, m_w_down, m_g_final, v_g_attn, v_w_in, v_b_in, v_sinks, v_rel_table, v_g_out_a, v_g_out_b, v_w_o, v_g_ffn, v_w_gate, v_w_up, v_w_down, v_g_final):
    rest_names = ("w_o", "w_gate", "w_up", "w_down")

    rest = [w_o[0].astype(BF16), w_gate[0].astype(BF16).T, w_up[0].astype(BF16).T, w_down[0].astype(BF16)]
    wint = _all_gather(w_in[0].astype(BF16).T, name="gather_w_in").reshape(D_IN, D_MODEL)
    wint, rest = lax.optimization_barrier((wint, rest))
    wo_state, token_o = _exchange_start(rest[:1], gather=True, name="gather_w_o_start")
    token_o, ffn_src = lax.optimization_barrier((token_o, rest[1:]))
    ffn_state, token = _exchange_start(ffn_src, gather=True, name="gather_ffn_start")
    token = token + token_o

    def whole(got):
        return [g.reshape(N_DEV * g.shape[1], D_MODEL) for g in got]

    def wo_fn(after):
        return whole(_exchange_wait(wo_state, after, gather=True, name="gather_w_o_wait"))[0]

    def ffn_fn(after):
        return whole(_exchange_wait(ffn_state, after, gather=True, name="gather_ffn_wait"))

    def early_fn(dws):
        return _exchange_start([dws[n].reshape(N_DEV, -1, D_MODEL) for n in rest_names], gather=False,
                               name="scatter_rest_start")

    loss_part, grad_x, dwint, early_state, small = _local_step(
        x[0], loss_target[0], g_attn, wint, b_in, sinks, rel_table, g_out_a, g_out_b, g_ffn, g_final, token,
        wo_fn, ffn_fn, early_fn)
    loss = lax.psum(loss_part, ("x", "y", "c"))

    parts_in = dwint.astype(BF16).reshape(N_DEV, D_IN // N_DEV, D_MODEL)
    in_state, token3 = _exchange_start([parts_in], gather=False, name="scatter_w_in_start")
    got = _exchange_wait(early_state, token3, gather=False, name="scatter_rest_wait")

    def update(n, parts, w, m, v, transposed):
        if transposed:
            return [a.T[None] for a in _adamw(parts, w[0].T, m[0].T, v[0].T, name="adamw_" + n)]
        return [a[None] for a in _adamw(parts, w[0], m[0], v[0], name="adamw_" + n)]

    big = dict(w_o=update("w_o", got[0], w_o, m_w_o, v_w_o, False),
               w_gate=update("w_gate", got[1], w_gate, m_w_gate, v_w_gate, True),
               w_up=update("w_up", got[2], w_up, m_w_up, v_w_up, True),
               w_down=update("w_down", got[3], w_down, m_w_down, v_w_down, False))

    ws = dict(g_attn=g_attn, b_in=b_in, sinks=sinks, rel_table=rel_table, g_out_a=g_out_a, g_out_b=g_out_b,
              g_ffn=g_ffn, g_final=g_final)
    ms = dict(g_attn=m_g_attn, b_in=m_b_in, sinks=m_sinks, rel_table=m_rel_table, g_out_a=m_g_out_a,
              g_out_b=m_g_out_b, g_ffn=m_g_ffn, g_final=m_g_final)
    vs = dict(g_attn=v_g_attn, b_in=v_b_in, sinks=v_sinks, rel_table=v_rel_table, g_out_a=v_g_out_a,
              g_out_b=v_g_out_b, g_ffn=v_g_ffn, g_final=v_g_final)
    sparts = _all_gather(_pack_small(small), name="gather_small")
    sm_packed = _adamw_small(sparts, _pack_small(ws), _pack_small(ms), _pack_small(vs))
    sm = [_unpack_small(a, ws) for a in sm_packed]

    done = sm_packed[1][:1, :1] + sum(big[n][1][0, :1, :1] for n in rest_names)
    got_in = _exchange_wait(in_state, done, gather=False, name="scatter_w_in_wait")[0]
    big["w_in"] = update("w_in", got_in, w_in, m_w_in, v_w_in, True)

    order = ("g_attn", "w_in", "b_in", "sinks", "rel_table", "g_out_a", "g_out_b", "w_o", "g_ffn", "w_gate", "w_up",
             "w_down", "g_final")
    outs = [loss, grad_x[None]]
    for k in range(4):
        outs += [big[n][k] if n in big else sm[k][n] for n in order]
    return tuple(outs)
```

```python
import functools
import math

import jax
import jax.numpy as jnp
from jax import lax
from jax.experimental import pallas as pl
from jax.experimental.pallas import tpu as pltpu

F32 = jnp.float32
BF16 = jnp.bfloat16

N_DEV = 8
D_MODEL = 1024
HEAD_DIM = 64
N_HEADS = 8
PAIR = 2 * HEAD_DIM
WIDTH = N_HEADS * HEAD_DIM
D_IN = 2304
D_INP = 2560
D_FF = 2816
BLK = 128
ROPE_THETA = 150000.0
REL_BUCKETS = 32
REL_MAX_DISTANCE = 2048
EPS = 1e-5
NEG = -1e30
BRANCHES = ((128, 1), (512, 4), (2048, 16))
Q_SCALE = HEAD_DIM ** -0.5

ADAM_LR = 0.001
ADAM_B1 = 0.9
ADAM_B2 = 0.999
ADAM_EPS = 1e-08
ADAM_WD = 0.01
ADAM_STEP = 10

VMEM_LIMIT = 56 * 1024 * 1024
MESH = pl.DeviceIdType.MESH

NT = (((1,), (1,)), ((), ()))
TN = (((0,), (0,)), ((), ()))

SMALL_ROWS = 56


def _params(sem=None):
    return pltpu.CompilerParams(dimension_semantics=sem, vmem_limit_bytes=VMEM_LIMIT)


def _rms_bwd(dh, xh, r, g):
    u = dh * g
    return r * (u - xh * jnp.mean(u * xh, axis=-1, keepdims=True))


def _rope_rot(t, first):
    return jnp.where(first, pltpu.roll(t, 96, 1), pltpu.roll(t, 32, 1))


N_CHUNK = WIDTH // PAIR


def _scr(tm):
    return pltpu.VMEM((N_CHUNK, tm, PAIR), F32)


def _scr_get(scr):
    return jnp.concatenate([scr[j] for j in range(N_CHUNK)], axis=1)


def _scr_put(scr, val):
    for j in range(N_CHUNK):
        scr[j] = val[:, j * PAIR:(j + 1) * PAIR]


def _unstride(view_ref, scr, dil, tm):
    n = tm // dil
    for r in range(dil):
        for j in range(N_CHUNK):
            col = r * WIDTH + j * PAIR
            scr.at[j][pl.ds(r, n, stride=dil), :] = view_ref[:, col:col + PAIR].astype(F32)


def _restride(scr, out_ref, dil, tm):
    n = tm // dil
    for r in range(dil):
        for j in range(N_CHUNK):
            col = r * WIDTH + j * PAIR
            rows = scr[j] if dil == 1 else scr.at[j][pl.ds(r, n, stride=dil), :]
            out_ref[:, col:col + PAIR] = rows.astype(out_ref.dtype)


def _view_specs(tm):
    return [pl.BlockSpec((tm // dil, dil * WIDTH), lambda i: (i, 0)) for _, dil in BRANCHES]


def _view_shapes(T, dtype):
    return [jax.ShapeDtypeStruct((T // dil, dil * WIDTH), dtype) for _, dil in BRANCHES]


def _norm_proj(x, g, w, b, cos, sin, *, tm=512):
    T = x.shape[0]

    def body(x_ref, g_ref, w_ref, b_ref, cos_ref, sin_ref, h_ref, qa_ref, ka_ref, va_ref, *rest):
        outs_b, ys = rest[:9], rest[9]
        xv = x_ref[...]
        r = lax.rsqrt(jnp.mean(xv * xv, axis=-1, keepdims=True) + EPS)
        h = (xv * r * g_ref[...]).astype(BF16)
        h_ref[...] = h
        cosv = cos_ref[...]
        sinv = sin_ref[...]
        lane = lax.broadcasted_iota(jnp.int32, (tm, PAIR), 1)
        first = (lane % HEAD_DIM) < (HEAD_DIM // 2)

        def proj(off):
            return (lax.dot_general(h, w_ref[off:off + 256, :], NT, preferred_element_type=F32)
                    + b_ref[:, off:off + 256])

        for (off, width, rot, scale), o_ref in zip(((0, 512, True, Q_SCALE), (512, 256, True, 1.0), (768, 256, False, 1.0)),
                                                   (qa_ref, ka_ref, va_ref)):
            for c in range(0, width, 256):
                y = proj(off + c)
                for j in range(0, 256, PAIR):
                    t = y[:, j:j + PAIR]
                    if rot:
                        t = t * cosv + _rope_rot(t, first) * sinv
                    if scale != 1.0:
                        t = t * scale
                    o_ref[:, c + j:c + j + PAIR] = t.astype(BF16)
        for n, (off, scale) in enumerate(((1024, Q_SCALE), (1536, 1.0), (2048, 1.0))):
            for c in range(0, WIDTH, 256):
                y = proj(off + c)
                y = y * scale if scale != 1.0 else y
                for j in range(0, 256, PAIR):
                    ys[(c + j) // PAIR] = y[:, j:j + PAIR]
            for (_, dil), o_ref in zip(BRANCHES, outs_b[3 * n:3 * n + 3]):
                _restride(ys, o_ref, dil, tm)

    row = lambda w_: pl.BlockSpec((tm, w_), lambda i: (i, 0))
    full = lambda a: pl.BlockSpec(a.shape, lambda i: (0, 0))
    return pl.pallas_call(
        body, name="norm_proj", grid=(T // tm,),
        in_specs=[row(D_MODEL), full(g), full(w), full(b), row(PAIR), row(PAIR)],
        out_specs=[row(D_MODEL), row(512), row(256), row(256)] + _view_specs(tm) * 3,
        out_shape=[jax.ShapeDtypeStruct((T, n), BF16) for n in (D_MODEL, 512, 256, 256)] + _view_shapes(T, BF16) * 3,
        scratch_shapes=[_scr(tm)],
        compiler_params=_params(("arbitrary",)),
    )(x, g, w, b, cos, sin)


SUB = 4
AHEAD = 2


def _attn_specs(kvw):
    q_spec = pl.BlockSpec((SUB * BLK, WIDTH), lambda r, i: (i, r))
    kc_spec = pl.BlockSpec((SUB * BLK, kvw), lambda r, i: (i, r))
    kp_spec = pl.BlockSpec((BLK, kvw), lambda r, i: (jnp.maximum(SUB * i - 1, 0), r))
    b_spec = pl.BlockSpec((2, N_HEADS, BLK, 2 * BLK), lambda r, i: (0, 0, 0, 0))
    return q_spec, kp_spec, kc_spec, b_spec


def _window(prev_ref, cur_ref, j, ksl):
    before = prev_ref[:, ksl] if j == 0 else cur_ref[(j - 1) * BLK:j * BLK, ksl]
    return jnp.concatenate([before, cur_ref[j * BLK:(j + 1) * BLK, ksl]], axis=0)


def _attn_fwd(q, k, v, bias, sinks, *, dil, kv_pairs, use_sink, name):
    L = q.shape[0]
    ns = L // (SUB * BLK)
    kvw = kv_pairs * PAIR
    rep = 4 // kv_pairs

    def body(sink_ref, q_ref, kp_ref, kc_ref, vp_ref, vc_ref, b_ref, o_ref, lse_ref):
        lane = lax.broadcasted_iota(jnp.int32, (1, PAIR), 1)
        lo = lane < HEAD_DIM
        first = jnp.where(pl.program_id(1) == 0, 1, 0)
        def scores(j, hp):
            rows = slice(j * BLK, (j + 1) * BLK)
            sl = slice(hp * PAIR, (hp + 1) * PAIR)
            ksl = slice((hp // rep) * PAIR, (hp // rep + 1) * PAIR)
            qp = q_ref[rows, sl]
            kk = _window(kp_ref, kc_ref, j, ksl)
            vv = _window(vp_ref, vc_ref, j, ksl)
            heads = []
            for e in range(2):
                h = 2 * hp + e
                msk = lo if e == 0 else jnp.logical_not(lo)
                qm = jnp.where(msk, qp, jnp.zeros_like(qp))
                s = lax.dot_general(qm, kk, NT, preferred_element_type=F32) + (b_ref[first, h] if j == 0 else b_ref[0, h])
                heads.append((h, msk, s))
            return rows, sl, vv, heads

        def outputs(rows, sl, vv, heads):
            o_pair = None
            lse_pair = None
            for h, msk, s in heads:
                m = jnp.max(s, axis=-1, keepdims=True)
                if use_sink:
                    sk = sink_ref[h]
                    m = jnp.maximum(m, sk)
                p = jnp.exp(s - m)
                l = jnp.sum(p, axis=-1, keepdims=True)
                if use_sink:
                    l = l + jnp.exp(sk - m)
                vm = jnp.where(msk, vv, jnp.zeros_like(vv))
                oe = jnp.dot(p.astype(BF16), vm, preferred_element_type=F32) * (1.0 / l)
                ls = m + jnp.log(l)
                if o_pair is None:
                    o_pair = oe
                    lse_pair = jnp.broadcast_to(ls, (BLK, PAIR))
                else:
                    o_pair = o_pair + oe
                    lse_pair = jnp.where(lo, lse_pair, ls)
            o_ref[rows, sl] = o_pair.astype(BF16)
            lse_ref[rows, sl] = lse_pair

        items = [(j, hp) for j in range(SUB) for hp in range(4)]
        queue = [scores(*it) for it in items[:AHEAD]]
        for n in range(len(items)):
            if n + AHEAD < len(items):
                queue.append(scores(*items[n + AHEAD]))
            outputs(*queue.pop(0))

    q_spec, kp_spec, kc_spec, b_spec = _attn_specs(kvw)
    return pl.pallas_call(
        body, name=name, grid=(dil, ns),
        in_specs=[pl.BlockSpec(memory_space=pltpu.SMEM), q_spec, kp_spec, kc_spec, kp_spec, kc_spec, b_spec],
        out_specs=[q_spec, q_spec],
        out_shape=[jax.ShapeDtypeStruct((L, dil * WIDTH), BF16), jax.ShapeDtypeStruct((L, dil * WIDTH), F32)],
        compiler_params=_params(("arbitrary", "arbitrary")),
    )(sinks, q, k, k, v, v, bias)


def _attn_bwd(q, k, v, o, do, lse, bias, sinks, *, dil, kv_pairs, use_sink, name):
    L = q.shape[0]
    ns = L // (SUB * BLK)
    n_steps = dil * ns
    kvw = kv_pairs * PAIR
    rep = 4 // kv_pairs
    last = slice((SUB - 1) * BLK, SUB * BLK)

    def body(sink_ref, q_ref, kp_ref, kc_ref, vp_ref, vc_ref, o_ref, do_ref, lse_ref, b_ref,
             dq_ref, dk_ref, dv_ref, dsum_ref, dsk_ref, pk_ref, pv_ref):
        t = pl.program_id(0)
        i = t % ns

        @pl.when(t == 0)
        def _():
            dsum_ref[...] = jnp.zeros_like(dsum_ref)
            dsk_ref[...] = jnp.zeros_like(dsk_ref)
            pk_ref[...] = jnp.zeros_like(pk_ref)
            pv_ref[...] = jnp.zeros_like(pv_ref)

        @pl.when(t < n_steps)
        def _():
            lo = lax.broadcasted_iota(jnp.int32, (1, PAIR), 1) < HEAD_DIM
            first = jnp.where(i == 0, 1, 0)
            dks = [[None] * kv_pairs for _ in range(SUB)]
            dvs = [[None] * kv_pairs for _ in range(SUB)]
            def scores(j, hp):
                rows = slice(j * BLK, (j + 1) * BLK)
                kvp = hp // rep
                sl = slice(hp * PAIR, (hp + 1) * PAIR)
                ksl = slice(kvp * PAIR, (kvp + 1) * PAIR)
                qp = q_ref[rows, sl]
                dop = do_ref[rows, sl]
                prod = dop.astype(F32) * o_ref[rows, sl].astype(F32)
                kk = _window(kp_ref, kc_ref, j, ksl)
                vv = _window(vp_ref, vc_ref, j, ksl)
                heads = []
                for e in range(2):
                    h = 2 * hp + e
                    msk = lo if e == 0 else jnp.logical_not(lo)
                    qm = jnp.where(msk, qp, jnp.zeros_like(qp))
                    dom = jnp.where(msk, dop, jnp.zeros_like(dop))
                    km = jnp.where(msk, kk, jnp.zeros_like(kk))
                    s = (lax.dot_general(qm, kk, NT, preferred_element_type=F32)
                         + (b_ref[first, h] if j == 0 else b_ref[0, h]))
                    dp = lax.dot_general(dom, vv, NT, preferred_element_type=F32)
                    heads.append((h, msk, qm, dom, km, s, dp))
                return j, rows, kvp, sl, prod, heads

            def grads(j, rows, kvp, sl, prod, heads):
                dq_pair = None
                c_pair = None
                qms, doms, dsbs, pbs = [], [], [], []
                for h, msk, qm, dom, km, s, dp in heads:
                    ls = lse_ref[rows, h * HEAD_DIM:h * HEAD_DIM + 1]
                    p = jnp.exp(s - ls)
                    delta = jnp.sum(jnp.where(msk, prod, 0.0), axis=-1, keepdims=True)
                    ds = p * (dp - delta)
                    if use_sink:
                        ce = jnp.exp(sink_ref[h] - ls) * delta
                        c_pair = jnp.broadcast_to(ce, (BLK, PAIR)) if c_pair is None else jnp.where(msk, ce, c_pair)
                    else:
                        dsum_ref[h] += ds
                    dsb = ds.astype(BF16)
                    dqe = jnp.dot(dsb, km, preferred_element_type=F32)
                    dq_pair = dqe if dq_pair is None else dq_pair + dqe
                    qms.append(qm)
                    doms.append(dom)
                    dsbs.append(dsb)
                    pbs.append(p.astype(BF16))
                dke = lax.dot_general(jnp.concatenate(dsbs, axis=0), jnp.concatenate(qms, axis=0), TN,
                                      preferred_element_type=F32)
                dve = lax.dot_general(jnp.concatenate(pbs, axis=0), jnp.concatenate(doms, axis=0), TN,
                                      preferred_element_type=F32)
                dks[j][kvp] = dke if dks[j][kvp] is None else dks[j][kvp] + dke
                dvs[j][kvp] = dve if dvs[j][kvp] is None else dvs[j][kvp] + dve
                dq_ref[rows, sl] = (dq_pair * Q_SCALE).astype(BF16)
                if use_sink:
                    dsk_ref[:, sl] += c_pair

            items = [(j, hp) for j in range(SUB) for hp in range(4)]
            ahead = AHEAD + 1 if use_sink else AHEAD
            queue = [scores(*it) for it in items[:ahead]]
            for n in range(len(items)):
                if n + ahead < len(items):
                    queue.append(scores(*items[n + ahead]))
                grads(*queue.pop(0))
            for kvp in range(kv_pairs):
                ksl = slice(kvp * PAIR, (kvp + 1) * PAIR)
                for pend_ref, out_ref, parts in ((pk_ref, dk_ref, [d[kvp] for d in dks]),
                                                 (pv_ref, dv_ref, [d[kvp] for d in dvs])):
                    if SUB > 1:
                        out_ref[:(SUB - 1) * BLK, ksl] = pend_ref[:(SUB - 1) * BLK, ksl].astype(BF16)
                    out_ref[last, ksl] = (pend_ref[last, ksl] + parts[0][:BLK]).astype(BF16)
                    for j in range(SUB):
                        own = parts[j][BLK:]
                        pend_ref[j * BLK:(j + 1) * BLK, ksl] = own + parts[j + 1][:BLK] if j + 1 < SUB else own

        @pl.when(t == n_steps)
        def _():
            dk_ref[...] = pk_ref[...].astype(BF16)
            dv_ref[...] = pv_ref[...].astype(BF16)

    def at(t):
        t = jnp.minimum(t, n_steps - 1)
        return t % ns, t // ns

    def before(t):
        return at(jnp.maximum(t - 1, 0))

    q_spec = pl.BlockSpec((SUB * BLK, WIDTH), at)
    kc_spec = pl.BlockSpec((SUB * BLK, kvw), at)
    kp_spec = pl.BlockSpec((BLK, kvw), lambda t: (jnp.maximum(SUB * at(t)[0] - 1, 0), at(t)[1]))
    b_spec = pl.BlockSpec((2, N_HEADS, BLK, 2 * BLK), lambda t: (0, 0, 0, 0))
    dkv_spec = pl.BlockSpec((SUB * BLK, kvw), before)
    return pl.pallas_call(
        body, name=name, grid=(n_steps + 1,),
        in_specs=[pl.BlockSpec(memory_space=pltpu.SMEM), q_spec, kp_spec, kc_spec, kp_spec, kc_spec,
                  q_spec, q_spec, q_spec, b_spec],
        out_specs=[q_spec, dkv_spec, dkv_spec,
                   pl.BlockSpec((N_HEADS, BLK, 2 * BLK), lambda t: (0, 0, 0)),
                   pl.BlockSpec((BLK, WIDTH), lambda t: (0, 0))],
        out_shape=[jax.ShapeDtypeStruct((L, dil * WIDTH), BF16),
                   jax.ShapeDtypeStruct((L, dil * kvw), BF16),
                   jax.ShapeDtypeStruct((L, dil * kvw), BF16),
                   jax.ShapeDtypeStruct((N_HEADS, BLK, 2 * BLK), F32),
                   jax.ShapeDtypeStruct((BLK, WIDTH), F32)],
        scratch_shapes=[pltpu.VMEM((SUB * BLK, kvw), F32), pltpu.VMEM((SUB * BLK, kvw), F32)],
        compiler_params=_params(("arbitrary",)),
    )(sinks, q, k, k, v, v, o, do, lse, bias)


def _merge_wo(x, oa, o1, o2, o3, l1, l2, l3, ga, gb, wo, gf, *, tm=512):
    T = x.shape[0]

    def body(x_ref, oa_ref, o1_ref, o2_ref, o3_ref, l1_ref, l2_ref, l3_ref, ga_ref, gb_ref, wo_ref, gf_ref,
             x2_ref, mix_ref, h2_ref, ob1_ref, ob4_ref, ob16_ref, ls1_ref, ls4_ref, ls16_ref, so2, so3, sl2, sl3):
        _unstride(o2_ref, so2, BRANCHES[1][1], tm)
        _unstride(o3_ref, so3, BRANCHES[2][1], tm)
        _unstride(l2_ref, sl2, BRANCHES[1][1], tm)
        _unstride(l3_ref, sl3, BRANCHES[2][1], tm)
        la, lb, lc = l1_ref[...], _scr_get(sl2), _scr_get(sl3)
        m = jnp.maximum(jnp.maximum(la, lb), lc)
        ea, eb, ec = jnp.exp(la - m), jnp.exp(lb - m), jnp.exp(lc - m)
        den = ea + eb + ec
        inv = 1.0 / den
        ob = (ea * o1_ref[...].astype(F32) + eb * _scr_get(so2) + ec * _scr_get(so3)) * inv
        _scr_put(so2, ob)
        _scr_put(sl2, m + jnp.log(den))
        for (_, dil), o_ref, l_ref in zip(BRANCHES, (ob1_ref, ob4_ref, ob16_ref), (ls1_ref, ls4_ref, ls16_ref)):
            _restride(so2, o_ref, dil, tm)
            _restride(sl2, l_ref, dil, tm)
        oav = oa_ref[...].astype(F32)
        ra = lax.rsqrt(jnp.mean(oav * oav, axis=-1, keepdims=True) + EPS)
        rb = lax.rsqrt(jnp.mean(ob * ob, axis=-1, keepdims=True) + EPS)
        mix_ref[:, :WIDTH] = (oav * ra * ga_ref[...]).astype(BF16)
        mix_ref[:, WIDTH:] = (ob * rb * gb_ref[...]).astype(BF16)
        x2 = x_ref[...] + jnp.dot(mix_ref[...], wo_ref[...], preferred_element_type=F32)
        x2_ref[...] = x2
        r2 = lax.rsqrt(jnp.mean(x2 * x2, axis=-1, keepdims=True) + EPS)
        h2_ref[...] = (x2 * r2 * gf_ref[...]).astype(BF16)

    row = lambda w_: pl.BlockSpec((tm, w_), lambda i: (i, 0))
    full = lambda a: pl.BlockSpec(a.shape, lambda i: (0, 0))
    return pl.pallas_call(
        body, name="merge_wo", grid=(T // tm,),
        in_specs=[row(D_MODEL), row(WIDTH)] + _view_specs(tm) * 2 + [full(ga), full(gb), full(wo), full(gf)],
        out_specs=[row(D_MODEL), row(D_MODEL), row(D_MODEL)] + _view_specs(tm) * 2,
        out_shape=[jax.ShapeDtypeStruct((T, D_MODEL), F32), jax.ShapeDtypeStruct((T, D_MODEL), BF16),
                   jax.ShapeDtypeStruct((T, D_MODEL), BF16)] + _view_shapes(T, BF16) + _view_shapes(T, F32),
        scratch_shapes=[_scr(tm)] * 4,
        compiler_params=_params(("arbitrary",)),
    )(x, oa, o1, o2, o3, l1, l2, l3, ga, gb, wo, gf)


def _ffn_up(h2, wgt, wut, *, tm=1024, fc=1408, rc=256):
    T = h2.shape[0]

    def body(h_ref, wg_ref, wu_ref, gate_ref, up_ref, act_ref):
        for s in range(0, tm, rc):
            h = h_ref[s:s + rc, :]
            gt = lax.dot_general(h, wg_ref[...], NT, preferred_element_type=F32)
            u = lax.dot_general(h, wu_ref[...], NT, preferred_element_type=F32)
            gate_ref[s:s + rc, :] = gt.astype(BF16)
            up_ref[s:s + rc, :] = u.astype(BF16)
            act_ref[s:s + rc, :] = (gt * (1.0 / (1.0 + jnp.exp(-gt))) * u).astype(BF16)

    rowd = pl.BlockSpec((tm, D_MODEL), lambda i, c: (i, 0))
    wrow = pl.BlockSpec((fc, D_MODEL), lambda i, c: (c, 0))
    oc = pl.BlockSpec((tm, fc), lambda i, c: (i, c))
    return pl.pallas_call(
        body, name="ffn_up", grid=(T // tm, D_FF // fc),
        in_specs=[rowd, wrow, wrow],
        out_specs=[oc, oc, oc],
        out_shape=[jax.ShapeDtypeStruct((T, D_FF), BF16)] * 3,
        compiler_params=_params(("arbitrary", "arbitrary")),
    )(h2, wgt, wut)


def _ffn_down_loss(act, wd, x2, tgt, g, *, tm=512, rc=256):
    T = x2.shape[0]

    def body(act_ref, wd_ref, x2_ref, tgt_ref, g_ref, dx_ref, dxb_ref, loss_ref, dg_ref):
        @pl.when(pl.program_id(0) == 0)
        def _():
            loss_ref[...] = jnp.zeros_like(loss_ref)
            dg_ref[...] = jnp.zeros_like(dg_ref)

        gv = g_ref[...]
        lsum = jnp.zeros((1, 1), F32)
        dgs = jnp.zeros((1, D_MODEL), F32)
        for c in range(0, tm, rc):
            x3 = x2_ref[c:c + rc, :] + jnp.dot(act_ref[c:c + rc, :], wd_ref[...], preferred_element_type=F32)
            r = lax.rsqrt(jnp.mean(x3 * x3, axis=-1, keepdims=True) + EPS)
            xh = x3 * r
            diff = xh * gv - tgt_ref[c:c + rc, :]
            lsum = lsum + jnp.sum(jnp.sum(diff * diff, axis=-1, keepdims=True), axis=0, keepdims=True)
            dy = diff * (1.0 / D_MODEL)
            dgs = dgs + jnp.sum(dy * xh, axis=0, keepdims=True)
            dx = _rms_bwd(dy, xh, r, gv)
            dx_ref[c:c + rc, :] = dx
            dxb_ref[c:c + rc, :] = dx.astype(BF16)
        loss_ref[...] += lsum * (0.5 / D_MODEL)
        dg_ref[...] += dgs

    rowd = pl.BlockSpec((tm, D_MODEL), lambda i: (i, 0))
    return pl.pallas_call(
        body, name="ffn_down_loss", grid=(T // tm,),
        in_specs=[pl.BlockSpec((tm, D_FF), lambda i: (i, 0)), pl.BlockSpec((D_FF, D_MODEL), lambda i: (0, 0)),
                  rowd, rowd, pl.BlockSpec(g.shape, lambda i: (0, 0))],
        out_specs=[rowd, rowd, pl.BlockSpec((1, 1), lambda i: (0, 0)), pl.BlockSpec((1, D_MODEL), lambda i: (0, 0))],
        out_shape=[jax.ShapeDtypeStruct((T, D_MODEL), F32), jax.ShapeDtypeStruct((T, D_MODEL), BF16),
                   jax.ShapeDtypeStruct((1, 1), F32), jax.ShapeDtypeStruct((1, D_MODEL), F32)],
        compiler_params=_params(("arbitrary",)),
    )(act, wd, x2, tgt, g)


def _ffn_bwd_act(dx3b, gate, up, wd, *, tm=1024, fc=1408, rc=256):
    T = dx3b.shape[0]

    def body(dxb_ref, gate_ref, up_ref, wd_ref, dgate_ref, dup_ref):
        for s in range(0, tm, rc):
            dact = lax.dot_general(dxb_ref[s:s + rc, :], wd_ref[...], NT, preferred_element_type=F32)
            gt = gate_ref[s:s + rc, :].astype(F32)
            u = up_ref[s:s + rc, :].astype(F32)
            sg = 1.0 / (1.0 + jnp.exp(-gt))
            dgate_ref[s:s + rc, :] = (dact * u * sg * (1.0 + gt * (1.0 - sg))).astype(BF16)
            dup_ref[s:s + rc, :] = (dact * gt * sg).astype(BF16)

    rowd = pl.BlockSpec((tm, D_MODEL), lambda i, c: (i, 0))
    oc = pl.BlockSpec((tm, fc), lambda i, c: (i, c))
    return pl.pallas_call(
        body, name="ffn_bwd_act", grid=(T // tm, D_FF // fc),
        in_specs=[rowd, oc, oc, pl.BlockSpec((fc, D_MODEL), lambda i, c: (c, 0))],
        out_specs=[oc, oc],
        out_shape=[jax.ShapeDtypeStruct((T, D_FF), BF16), jax.ShapeDtypeStruct((T, D_FF), BF16)],
        compiler_params=_params(("arbitrary", "arbitrary")),
    )(dx3b, gate, up, wd)


def _ffn_bwd_in(dgate, dup, wgt, wut, x2, dx3, g, *, tm=512, rc=256):
    T = x2.shape[0]

    def body(dgate_ref, dup_ref, wg_ref, wu_ref, x2_ref, dx_ref, g_ref, dx2_ref, dx2b_ref, dg_ref):
        @pl.when(pl.program_id(0) == 0)
        def _():
            dg_ref[...] = jnp.zeros_like(dg_ref)

        gv = g_ref[...]
        dgs = jnp.zeros((1, D_MODEL), F32)
        for s in range(0, tm, rc):
            dh = (jnp.dot(dgate_ref[s:s + rc, :], wg_ref[...], preferred_element_type=F32)
                  + jnp.dot(dup_ref[s:s + rc, :], wu_ref[...], preferred_element_type=F32))
            xv = x2_ref[s:s + rc, :]
            r = lax.rsqrt(jnp.mean(xv * xv, axis=-1, keepdims=True) + EPS)
            xh = xv * r
            dgs = dgs + jnp.sum(dh * xh, axis=0, keepdims=True)
            d = dx_ref[s:s + rc, :] + _rms_bwd(dh, xh, r, gv)
            dx2_ref[s:s + rc, :] = d
            dx2b_ref[s:s + rc, :] = d.astype(BF16)
        dg_ref[...] += dgs

    rowd = pl.BlockSpec((tm, D_MODEL), lambda i: (i, 0))
    rowf = pl.BlockSpec((tm, D_FF), lambda i: (i, 0))
    wfull = pl.BlockSpec((D_FF, D_MODEL), lambda i: (0, 0))
    return pl.pallas_call(
        body, name="ffn_bwd_in", grid=(T // tm,),
        in_specs=[rowf, rowf, wfull, wfull, rowd, rowd, pl.BlockSpec(g.shape, lambda i: (0, 0))],
        out_specs=[rowd, rowd, pl.BlockSpec((1, D_MODEL), lambda i: (0, 0))],
        out_shape=[jax.ShapeDtypeStruct((T, D_MODEL), F32), jax.ShapeDtypeStruct((T, D_MODEL), BF16),
                   jax.ShapeDtypeStruct((1, D_MODEL), F32)],
        compiler_params=_params(("arbitrary",)),
    )(dgate, dup, wgt, wut, x2, dx3, g)


def _matmul_tn(a, b, *, tk, tn, tt=2048, out_dtype=BF16, name):
    T, K = a.shape
    N = b.shape[1]
    nt = T // tt

    def body(a_ref, b_ref, o_ref, acc_ref):
        part = lax.dot_general(a_ref[...], b_ref[...], TN, preferred_element_type=F32)

        @pl.when(pl.program_id(2) == 0)
        def _():
            acc_ref[...] = part

        @pl.when(pl.program_id(2) > 0)
        def _():
            acc_ref[...] += part

        @pl.when(pl.program_id(2) == nt - 1)
        def _():
            o_ref[...] = acc_ref[...].astype(out_dtype)

    return pl.pallas_call(
        body, name=name, grid=(K // tk, N // tn, nt),
        in_specs=[pl.BlockSpec((tt, tk), lambda i, j, t: (t, i)), pl.BlockSpec((tt, tn), lambda i, j, t: (t, j))],
        out_specs=pl.BlockSpec((tk, tn), lambda i, j, t: (i, j)),
        out_shape=jax.ShapeDtypeStruct((K, N), out_dtype),
        scratch_shapes=[pltpu.VMEM((tk, tn), F32)],
        compiler_params=_params(("arbitrary", "arbitrary", "arbitrary")),
    )(a, b)


def _wo_bwd(dx2b, wo, oa, ob, ga, gb, *, tm=512):
    T = dx2b.shape[0]

    def body(dx_ref, wo_ref, oa_ref, ob_ref, ga_ref, gb_ref, doa_ref, dob1_ref, dob4_ref, dob16_ref, dga_ref, dgb_ref, scr):
        @pl.when(pl.program_id(0) == 0)
        def _():
            dga_ref[...] = jnp.zeros_like(dga_ref)
            dgb_ref[...] = jnp.zeros_like(dgb_ref)

        dm = lax.dot_general(dx_ref[...], wo_ref[...], NT, preferred_element_type=F32)
        for o_ref, g_ref, dg_ref, sl in ((oa_ref, ga_ref, dga_ref, slice(0, WIDTH)),
                                         (ob_ref, gb_ref, dgb_ref, slice(WIDTH, 2 * WIDTH))):
            ov = o_ref[...].astype(F32)
            r = lax.rsqrt(jnp.mean(ov * ov, axis=-1, keepdims=True) + EPS)
            xh = ov * r
            d = dm[:, sl]
            dg_ref[...] += jnp.sum(d * xh, axis=0, keepdims=True)
            do = _rms_bwd(d, xh, r, g_ref[...])
            if o_ref is oa_ref:
                doa_ref[...] = do.astype(BF16)
            else:
                _scr_put(scr, do)
                for (_, dil), v_ref in zip(BRANCHES, (dob1_ref, dob4_ref, dob16_ref)):
                    _restride(scr, v_ref, dil, tm)

    row = lambda w_: pl.BlockSpec((tm, w_), lambda i: (i, 0))
    full = lambda a: pl.BlockSpec(a.shape, lambda i: (0, 0))
    return pl.pallas_call(
        body, name="wo_bwd", grid=(T // tm,),
        in_specs=[row(D_MODEL), full(wo), row(WIDTH), row(WIDTH), full(ga), full(gb)],
        out_specs=[row(WIDTH)] + _view_specs(tm)
        + [pl.BlockSpec((1, WIDTH), lambda i: (0, 0)), pl.BlockSpec((1, WIDTH), lambda i: (0, 0))],
        out_shape=[jax.ShapeDtypeStruct((T, WIDTH), BF16)] + _view_shapes(T, BF16)
        + [jax.ShapeDtypeStruct((1, WIDTH), F32), jax.ShapeDtypeStruct((1, WIDTH), F32)],
        scratch_shapes=[_scr(tm)],
        compiler_params=_params(("arbitrary",)),
    )(dx2b, wo, oa, ob, ga, gb)


def _inproj_bwd(dqa, dka, dva, dqs, dks, dvs, cos, sin, w, x, dx2, g, *, tm=512):
    T = dqa.shape[0]

    def body(dqa_ref, dka_ref, dva_ref, q1, q2, q3, k1, k2, k3, v1, v2, v3, cos_ref, sin_ref, w_ref, x_ref, dx2_ref,
             g_ref, dp_ref, db_ref, gx_ref, dg_ref, acc, tmp):
        @pl.when(pl.program_id(0) == 0)
        def _():
            db_ref[...] = jnp.zeros_like(db_ref)
            dg_ref[...] = jnp.zeros_like(dg_ref)

        cosv = cos_ref[...]
        sinv = sin_ref[...]
        lane = lax.broadcasted_iota(jnp.int32, (tm, PAIR), 1)
        first = (lane % HEAD_DIM) < (HEAD_DIM // 2)

        def put(off, val):
            dp_ref[:, off:off + PAIR] = val.astype(BF16)
            db_ref[:, off:off + PAIR] += jnp.sum(val, axis=0, keepdims=True)

        for src, off, width in ((dqa_ref, 0, 512), (dka_ref, 512, 256)):
            for j in range(0, width, PAIR):
                d = src[:, j:j + PAIR].astype(F32)
                put(off + j, d * cosv - _rope_rot(d, first) * sinv)
        for j in range(0, 256, PAIR):
            put(768 + j, dva_ref[:, j:j + PAIR].astype(F32))
        for (a, b, c), off in (((q1, q2, q3), 1024), ((k1, k2, k3), 1536), ((v1, v2, v3), 2048)):
            _unstride(b, acc, BRANCHES[1][1], tm)
            _unstride(c, tmp, BRANCHES[2][1], tm)
            for j in range(N_CHUNK):
                put(off + j * PAIR, a[:, j * PAIR:(j + 1) * PAIR].astype(F32) + acc[j] + tmp[j])

        dh = jnp.dot(dp_ref[...], w_ref[...], preferred_element_type=F32)
        xv = x_ref[...]
        r = lax.rsqrt(jnp.mean(xv * xv, axis=-1, keepdims=True) + EPS)
        xh = xv * r
        dg_ref[...] += jnp.sum(dh * xh, axis=0, keepdims=True)
        gx_ref[...] = dx2_ref[...] + _rms_bwd(dh, xh, r, g_ref[...])

    row = lambda w_: pl.BlockSpec((tm, w_), lambda i: (i, 0))
    full = lambda a: pl.BlockSpec(a.shape, lambda i: (0, 0))
    return pl.pallas_call(
        body, name="inproj_bwd", grid=(T // tm,),
        in_specs=[row(512), row(256), row(256)] + _view_specs(tm) * 3 + [row(PAIR), row(PAIR)]
        + [full(w), row(D_MODEL), row(D_MODEL), full(g)],
        out_specs=[row(D_INP), pl.BlockSpec((1, D_INP), lambda i: (0, 0)), row(D_MODEL),
                   pl.BlockSpec((1, D_MODEL), lambda i: (0, 0))],
        out_shape=[jax.ShapeDtypeStruct((T, D_INP), BF16), jax.ShapeDtypeStruct((1, D_INP), F32),
                   jax.ShapeDtypeStruct((T, D_MODEL), F32), jax.ShapeDtypeStruct((1, D_MODEL), F32)],
        scratch_shapes=[_scr(tm)] * 2,
        compiler_params=_params(("arbitrary",)),
    )(dqa, dka, dva, *dqs, *dks, *dvs, cos, sin, w, x, dx2, g)


def _bias_sink_grads(dsums, bmaps, dsk):
    def body(s1, s2, s3, m1, m2, m3, dsk_ref, drel_ref, dsink_ref):
        row = lax.broadcasted_iota(jnp.int32, (N_HEADS, 128), 0)
        lane = lax.broadcasted_iota(jnp.int32, (N_HEADS, 128), 1)
        out = jnp.zeros((N_HEADS, 128), F32)
        for s_ref, m_ref in ((s1, m1), (s2, m2), (s3, m3)):
            bm = m_ref[...]
            for h in range(N_HEADS):
                a = s_ref[h]
                for b in range(REL_BUCKETS):
                    v = jnp.sum(jnp.sum(jnp.where(bm == b, a, 0.0), axis=-1, keepdims=True), axis=0, keepdims=True)
                    out = out + jnp.where((row == h) & (lane == b), v, 0.0)
        drel_ref[...] = out
        dsink_ref[...] = -jnp.sum(dsk_ref[...], axis=0, keepdims=True)

    vm = pl.BlockSpec(memory_space=pltpu.VMEM)
    return pl.pallas_call(
        body, name="bias_sink_grads",
        in_specs=[vm] * 7, out_specs=[vm, vm],
        out_shape=[jax.ShapeDtypeStruct((N_HEADS, 128), F32), jax.ShapeDtypeStruct((1, WIDTH), F32)],
        compiler_params=_params(),
    )(*dsums, *bmaps, dsk)


def _all_gather(blk, *, name):
    R, C = blk.shape

    def body(x_ref, out_ref, send_sems, recv_sems, local_sem):
        x, y, c = lax.axis_index("x"), lax.axis_index("y"), lax.axis_index("c")
        me, sibling = (x, y, c), (x, y, 1 - c)
        chips = [(1 - x, y), (x, 1 - y), (1 - x, 1 - y)]

        def slot(px, py, pc):
            return out_ref.at[4 * px + 2 * py + pc]

        def copy(k, block, to, src=None):
            return pltpu.make_async_remote_copy(
                src_ref=slot(*block) if src is None else src, dst_ref=slot(*block),
                send_sem=send_sems.at[k], recv_sem=recv_sems.at[k], device_id=to, device_id_type=MESH)

        mine = pltpu.make_async_copy(x_ref, slot(*me), local_sem)
        mine.start()
        first = [copy(0, me, sibling, src=x_ref)]
        first += [copy(1 + j, me, (*chip, c), src=x_ref) for j, chip in enumerate(chips)]
        for cp in first:
            cp.start()
        passed = [copy(4 + j, (*chip, c), sibling) for j, chip in enumerate(chips)]
        for j, chip in enumerate(chips):
            copy(1 + j, (*chip, c), me).wait_recv()
            passed[j].start()
        copy(0, sibling, me).wait_recv()
        for j, chip in enumerate(chips):
            copy(4 + j, (*chip, 1 - c), me).wait_recv()
        for cp in first + passed:
            cp.wait_send()
        mine.wait()

    return pl.pallas_call(
        body, name=name,
        in_specs=[pl.BlockSpec(memory_space=pl.ANY)], out_specs=pl.BlockSpec(memory_space=pl.ANY),
        out_shape=jax.ShapeDtypeStruct((N_DEV, R, C), blk.dtype),
        scratch_shapes=[pltpu.SemaphoreType.DMA((7,)), pltpu.SemaphoreType.DMA((7,)), pltpu.SemaphoreType.DMA],
        compiler_params=pltpu.CompilerParams(has_side_effects=True),
    )(blk)


def _peers(x, y, c):
    return [(x ^ (k >> 2), y ^ ((k >> 1) & 1), c ^ (k & 1)) for k in range(1, N_DEV)]


_HBM = pl.BlockSpec(memory_space=pltpu.HBM)
_SEM = pl.BlockSpec(memory_space=pltpu.SEMAPHORE)
_EFFECT = pltpu.SideEffectType.DATAFLOW_SIDE_EFFECTING


def _exchange_start(srcs, *, gather, name):
    n = len(srcs)
    lands = [lax.empty((N_DEV,) + s.shape[-2:], s.dtype) for s in srcs]

    def body(*refs):
        src_refs, land_refs = refs[:n], refs[n:2 * n]
        send_sems, recv_sems = refs[2 * n], refs[2 * n + 1]
        token = refs[-1]
        x, y, c = lax.axis_index("x"), lax.axis_index("y"), lax.axis_index("c")
        mine = 4 * x + 2 * y + c
        for a in range(n):
            for k, peer in enumerate(_peers(x, y, c)):
                dest = 4 * peer[0] + 2 * peer[1] + peer[2]
                j = a * (N_DEV - 1) + k
                pltpu.make_async_remote_copy(
                    src_ref=src_refs[a] if gather else src_refs[a].at[dest], dst_ref=land_refs[a].at[mine],
                    send_sem=send_sems.at[j], recv_sem=recv_sems.at[j], device_id=peer, device_id_type=MESH).start()
        token[...] = jnp.zeros_like(token)

    sems = pltpu.SemaphoreType.DMA((n * (N_DEV - 1),))
    out = pl.pallas_call(
        body, name=name,
        out_shape=(sems, sems) + tuple(pltpu.HBM(a.shape, a.dtype) for a in list(srcs) + lands)
        + (jax.ShapeDtypeStruct((8, 128), F32),),
        in_specs=(_HBM,) * (2 * n), out_specs=(_SEM, _SEM) + (_HBM,) * (2 * n) + (pl.BlockSpec(memory_space=pltpu.VMEM),),
        input_output_aliases={i: 2 + i for i in range(2 * n)},
        compiler_params=pltpu.CompilerParams(has_side_effects=_EFFECT),
    )(*[pltpu.with_memory_space_constraint(a, pltpu.HBM) for a in list(srcs) + lands])
    return out[:-1], out[-1]


def _exchange_wait(state, after, *, gather, name):
    send_sems, recv_sems = state[0], state[1]
    n = (len(state) - 2) // 2
    arrays = state[2:]

    def body(*refs):
        src_refs, land_refs = refs[:n], refs[n:2 * n]
        send_sems, recv_sems = refs[2 * n], refs[2 * n + 1]
        x, y, c = lax.axis_index("x"), lax.axis_index("y"), lax.axis_index("c")
        for a in range(n):
            for k, peer in enumerate(_peers(x, y, c)):
                other = 4 * peer[0] + 2 * peer[1] + peer[2]
                j = a * (N_DEV - 1) + k
                copy = pltpu.make_async_remote_copy(
                    src_ref=src_refs[a] if gather else src_refs[a].at[other], dst_ref=land_refs[a].at[other],
                    send_sem=send_sems.at[j], recv_sem=recv_sems.at[j], device_id=peer, device_id_type=MESH)
                copy.wait_send()
                copy.wait_recv()

    out = pl.pallas_call(
        body, name=name,
        out_shape=tuple(pltpu.HBM(a.shape, a.dtype) for a in arrays),
        in_specs=(_HBM,) * (2 * n) + (_SEM, _SEM, pl.BlockSpec(memory_space=pl.ANY)), out_specs=(_HBM,) * (2 * n),
        input_output_aliases={i: i for i in range(2 * n)},
        compiler_params=pltpu.CompilerParams(has_side_effects=_EFFECT),
    )(*arrays, send_sems, recv_sems, after)
    mine = 4 * lax.axis_index("x") + 2 * lax.axis_index("y") + lax.axis_index("c")
    own = out[:n] if gather else [lax.dynamic_index_in_dim(s, mine, 0, keepdims=False) for s in out[:n]]
    return [lax.dynamic_update_slice(g, o[None], (mine, 0, 0)) for g, o in zip(out[n:], own)]


def _adam_math(w, g, m, v):
    m = ADAM_B1 * m + (1.0 - ADAM_B1) * g
    v = ADAM_B2 * v + (1.0 - ADAM_B2) * (g * g)
    m_hat = m / (1.0 - ADAM_B1 ** ADAM_STEP)
    v_hat = v / (1.0 - ADAM_B2 ** ADAM_STEP)
    delta = -ADAM_LR * (m_hat / (jnp.sqrt(v_hat) + ADAM_EPS) + ADAM_WD * w)
    return delta, m, v


def _adamw(parts, w, m, v, *, name):
    R, C = w.shape
    n_parts = parts.shape[0]
    tr = R // 2
    assert tr % 16 == 0

    def body(p_ref, w_ref, m_ref, v_ref, g_ref, d_ref, nm_ref, nv_ref):
        g = p_ref[0].astype(F32)
        for s in range(1, n_parts):
            g = g + p_ref[s].astype(F32)
        d, nm, nv = _adam_math(w_ref[...], g, m_ref[...], v_ref[...])
        g_ref[...] = g
        d_ref[...] = d
        nm_ref[...] = nm
        nv_ref[...] = nv

    blk = pl.BlockSpec((tr, C), lambda i: (i, 0))
    return pl.pallas_call(
        body, name=name, grid=(R // tr,),
        in_specs=[pl.BlockSpec((n_parts, tr, C), lambda i: (0, i, 0)), blk, blk, blk],
        out_specs=[blk] * 4, out_shape=[jax.ShapeDtypeStruct((R, C), F32)] * 4,
        compiler_params=_params(("arbitrary",)),
    )(parts, w, m, v)


def _adamw_small(parts, w, m, v):
    def body(p_ref, w_ref, m_ref, v_ref, g_ref, d_ref, nm_ref, nv_ref):
        g = p_ref[0]
        for s in range(1, N_DEV):
            g = g + p_ref[s]
        d, nm, nv = _adam_math(w_ref[...], g, m_ref[...], v_ref[...])
        g_ref[...] = g
        d_ref[...] = d
        nm_ref[...] = nm
        nv_ref[...] = nv

    vm = pl.BlockSpec(memory_space=pltpu.VMEM)
    return pl.pallas_call(
        body, name="adamw_small", in_specs=[vm] * 4, out_specs=[vm] * 4,
        out_shape=[jax.ShapeDtypeStruct((SMALL_ROWS, 128), F32)] * 4, compiler_params=_params(),
    )(parts, w, m, v)


def _t5_bucket(dist):
    max_exact = REL_BUCKETS // 2
    df = jnp.maximum(dist, 1).astype(F32)
    large = max_exact + (jnp.log(df / max_exact) / math.log(REL_MAX_DISTANCE / max_exact)
                         * (REL_BUCKETS - max_exact)).astype(jnp.int32)
    large = jnp.minimum(large, REL_BUCKETS - 1)
    return jnp.where(dist < max_exact, dist, large)


def _band_tables(rel_table, dil, n_back):
    qi = jnp.arange(BLK)[:, None]
    kj = jnp.arange(2 * BLK)[None, :]
    delta = BLK + qi - kj
    in_band = (delta >= 0) & (delta <= n_back)
    if rel_table is None:
        vals = jnp.zeros((N_HEADS, BLK, 2 * BLK), F32)
        bmap = None
    else:
        bucket = _t5_bucket(jnp.clip(delta, 0, n_back) * dil)
        vals = jnp.zeros((N_HEADS, BLK, 2 * BLK), F32)
        for b in range(REL_BUCKETS):
            vals = jnp.where((bucket == b)[None], rel_table[b][:, None, None], vals)
        bmap = jnp.where(in_band, bucket, -1).astype(jnp.int32)
    later = jnp.where(in_band[None], vals, NEG)
    first = jnp.where((in_band & (kj >= BLK))[None], vals, NEG)
    return jnp.stack([later, first]), bmap


def _rope_tables(T):
    half = HEAD_DIM // 2
    inv_freq = ROPE_THETA ** (-jnp.arange(half, dtype=F32) / half)
    ang = jnp.arange(T, dtype=F32)[:, None] * inv_freq[None, :]
    cos, sin = jnp.cos(ang), jnp.sin(ang)
    return jnp.tile(cos, (1, 4)), jnp.tile(jnp.concatenate([-sin, sin], axis=1), (1, 2))


def _widen_in(a, axis):
    sl = lambda lo, hi: lax.slice_in_dim(a, lo, hi, axis=axis)
    dup = lambda lo: [sl(lo, lo + 64), sl(lo, lo + 64), sl(lo + 64, lo + 128), sl(lo + 64, lo + 128)]
    return jnp.concatenate([sl(0, 512)] + dup(512) + dup(640) + [sl(768, D_IN)], axis=axis)


def _fold_in(a, axis):
    sl = lambda lo, hi: lax.slice_in_dim(a, lo, hi, axis=axis)
    fold = lambda lo: [sl(lo, lo + 64) + sl(lo + 64, lo + 128), sl(lo + 128, lo + 192) + sl(lo + 192, lo + 256)]
    return jnp.concatenate([sl(0, 512)] + fold(512) + fold(768) + [sl(1024, D_INP)], axis=axis)


def _local_step(x, tgt, g_attn, wint, b_in, sinks, rel_table, g_out_a, g_out_b, g_ffn, g_final, token,
                wo_fn, ffn_fn, early_fn):
    T = x.shape[0]
    cos, sin = _rope_tables(T)
    cos = cos + token[0, 0]
    winp = _widen_in(wint, 0)
    binp = _widen_in(b_in, 1)
    g_final2 = g_final.reshape(1, D_MODEL)
    sink8 = sinks.reshape(N_HEADS)

    bias_a, _ = _band_tables(None, 1, BLK - 1)
    tabs = [_band_tables(rel_table, dil, window // dil) for window, dil in BRANCHES]

    h1, qa, ka, va, *qkv_b = _norm_proj(x, g_attn, winp, binp, cos, sin)
    qbs, kbs, vbs = qkv_b[0:3], qkv_b[3:6], qkv_b[6:9]
    oa, lse_a = _attn_fwd(qa, ka, va, bias_a, sink8, dil=1, kv_pairs=2, use_sink=True, name="attn_a_fwd")
    outs = [_attn_fwd(qbs[n], kbs[n], vbs[n], tabs[n][0], sink8, dil=dil, kv_pairs=4, use_sink=False,
                      name=f"attn_b{n}_fwd") for n, (_, dil) in enumerate(BRANCHES)]
    wo = wo_fn(outs[2][1])
    x2, mixed, h2, *ob_lse = _merge_wo(x, oa, outs[0][0], outs[1][0], outs[2][0], outs[0][1], outs[1][1], outs[2][1],
                                       g_out_a, g_out_b, wo, g_ffn)
    obs, lses = ob_lse[0:3], ob_lse[3:6]
    wgt, wut, wd = ffn_fn(h2)
    gate, up, act = _ffn_up(h2, wgt, wut)
    dx3, dx3b, loss, dg_final = _ffn_down_loss(act, wd, x2, tgt, g_final2)

    dgate, dup = _ffn_bwd_act(dx3b, gate, up, wd)
    dx2, dx2b, dg_ffn = _ffn_bwd_in(dgate, dup, wgt, wut, x2, dx3, g_ffn)
    dwd = _matmul_tn(act, dx3b, tk=1408, tn=1024, name="dw_down")
    dwgt = _matmul_tn(dgate, h2, tk=1408, tn=1024, name="dw_gate")
    dwut = _matmul_tn(dup, h2, tk=1408, tn=1024, name="dw_up")
    dwo = _matmul_tn(mixed, dx2b, tk=1024, tn=1024, name="dw_o")
    early, token2 = early_fn(dict(w_o=dwo, w_gate=dwgt, w_up=dwut, w_down=dwd))
    doa, *dobs, dg_out_a, dg_out_b = _wo_bwd(dx2b, wo, oa, obs[0], g_out_a + token2[0, 0], g_out_b)

    dqa, dka, dva, _, dsk = _attn_bwd(qa, ka, va, oa, doa, lse_a, bias_a, sink8, dil=1, kv_pairs=2, use_sink=True,
                                      name="attn_a_bwd")
    res = [_attn_bwd(qbs[n], kbs[n], vbs[n], obs[n], dobs[n], lses[n], tabs[n][0], sink8, dil=dil, kv_pairs=4,
                     use_sink=False, name=f"attn_b{n}_bwd") for n, (_, dil) in enumerate(BRANCHES)]
    dp, dbp, grad_x, dg_attn = _inproj_bwd(dqa, dka, dva, [r[0] for r in res], [r[1] for r in res],
                                           [r[2] for r in res], cos, sin, winp, x, dx2, g_attn)
    dwin = _fold_in(_matmul_tn(dp, h1, tk=1280, tn=1024, out_dtype=F32, name="dw_in"), 0)
    drel, dsink = _bias_sink_grads([r[3] for r in res], [t[1] for t in tabs], dsk)

    small = dict(
        g_attn=dg_attn, b_in=_fold_in(dbp, 1), sinks=dsink[:, ::HEAD_DIM], rel_table=drel[:, :REL_BUCKETS].T,
        g_out_a=dg_out_a, g_out_b=dg_out_b, g_ffn=dg_ffn, g_final=dg_final.reshape(D_MODEL))
    return loss[0, 0], grad_x, dwin, early, small


SMALL_NAMES = ("g_attn", "b_in", "sinks", "rel_table", "g_out_a", "g_out_b", "g_ffn", "g_final")


def _pack_small(vals):
    flat = jnp.concatenate([vals[n].reshape(-1).astype(F32) for n in SMALL_NAMES])
    return jnp.pad(flat, (0, SMALL_ROWS * 128 - flat.shape[0])).reshape(SMALL_ROWS, 128)


def _unpack_small(packed, like):
    flat = packed.reshape(-1)
    out, off = {}, 0
    for n in SMALL_NAMES:
        size = like[n].size
        out[n] = flat[off:off + size].reshape(like[n].shape)
        off += size
    return out


def kernel(x, g_attn, w_in, b_in, sinks, rel_table, g_out_a, g_out_b, w_o, g_ffn, w_gate, w_up, w_down, g_final, loss_target, m_g_attn, m_w_in, m_b_in, m_sinks, m_rel_table, m_g_out_a, m_g_out_b, m_w_o, m_g_ffn, m_w_gate, m_w_up, m_w_down, m_g_final, v_g_attn, v_w_in, v_b_in, v_sinks, v_rel_table, v_g_out_a, v_g_out_b, v_w_o, v_g_ffn, v_w_gate, v_w_up, v_w_down, v_g_final):
    rest_names = ("w_o", "w_gate", "w_up", "w_down")

    rest = [w_o[0].astype(BF16), w_gate[0].astype(BF16).T, w_up[0].astype(BF16).T, w_down[0].astype(BF16)]
    wint = _all_gather(w_in[0].astype(BF16).T, name="gather_w_in").reshape(D_IN, D_MODEL)
    wint, rest = lax.optimization_barrier((wint, rest))
    wo_state, token_o = _exchange_start(rest[:1], gather=True, name="gather_w_o_start")
    token_o, ffn_src = lax.optimization_barrier((token_o, rest[1:]))
    ffn_state, token = _exchange_start(ffn_src, gather=True, name="gather_ffn_start")
    token = token + token_o

    def whole(got):
        return [g.reshape(N_DEV * g.shape[1], D_MODEL) for g in got]

    def wo_fn(after):
        return whole(_exchange_wait(wo_state, after, gather=True, name="gather_w_o_wait"))[0]

    def ffn_fn(after):
        return whole(_exchange_wait(ffn_state, after, gather=True, name="gather_ffn_wait"))

    def early_fn(dws):
        return _exchange_start([dws[n].reshape(N_DEV, -1, D_MODEL) for n in rest_names], gather=False,
                               name="scatter_rest_start")

    loss_part, grad_x, dwint, early_state, small = _local_step(
        x[0], loss_target[0], g_attn, wint, b_in, sinks, rel_table, g_out_a, g_out_b, g_ffn, g_final, token,
        wo_fn, ffn_fn, early_fn)
    loss = lax.psum(loss_part, ("x", "y", "c"))

    parts_in = dwint.astype(BF16).reshape(N_DEV, D_IN // N_DEV, D_MODEL)
    in_state, token3 = _exchange_start([parts_in], gather=False, name="scatter_w_in_start")
    got = _exchange_wait(early_state, token3, gather=False, name="scatter_rest_wait")

    def update(n, parts, w, m, v, transposed):
        if transposed:
            return [a.T[None] for a in _adamw(parts, w[0].T, m[0].T, v[0].T, name="adamw_" + n)]
        return [a[None] for a in _adamw(parts, w[0], m[0], v[0], name="adamw_" + n)]

    big = dict(w_o=update("w_o", got[0], w_o, m_w_o, v_w_o, False),
               w_gate=update("w_gate", got[1], w_gate, m_w_gate, v_w_gate, True),
               w_up=update("w_up", got[2], w_up, m_w_up, v_w_up, True),
               w_down=update("w_down", got[3], w_down, m_w_down, v_w_down, False))

    ws = dict(g_attn=g_attn, b_in=b_in, sinks=sinks, rel_table=rel_table, g_out_a=g_out_a, g_out_b=g_out_b,
              g_ffn=g_ffn, g_final=g_final)
    ms = dict(g_attn=m_g_attn, b_in=m_b_in, sinks=m_sinks, rel_table=m_rel_table, g_out_a=m_g_out_a,
              g_out_b=m_g_out_b, g_ffn=m_g_ffn, g_final=m_g_final)
    vs = dict(g_attn=v_g_attn, b_in=v_b_in, sinks=v_sinks, rel_table=v_rel_table, g_out_a=v_g_out_a,
              g_out_b=v_g_out_b, g_ffn=v_g_ffn, g_final=v_g_final)
    sparts = _all_gather(_pack_small(small), name="gather_small")
    sm_packed = _adamw_small(sparts, _pack_small(ws), _pack_small(ms), _pack_small(vs))
    sm = [_unpack_small(a, ws) for a in sm_packed]

    done = sm_packed[1][:1, :1] + sum(big[n][1][0, :1, :1] for n in rest_names)
    got_in = _exchange_wait(in_state, done, gather=False, name="scatter_w_in_wait")[0]
    big["w_in"] = update("w_in", got_in, w_in, m_w_in, v_w_in, True)

    order = ("g_attn", "w_in", "b_in", "sinks", "rel_table", "g_out_a", "g_out_b", "w_o", "g_ffn", "w_gate", "w_up",
             "w_down", "g_final")
    outs = [loss, grad_x[None]]
    for k in range(4):
        outs += [big[n][k] if n in big else sm[k][n] for n in order]
    return tuple(outs)
```

```python
import functools
import math

import jax
import jax.numpy as jnp
from jax import lax
from jax.experimental import pallas as pl
from jax.experimental.pallas import tpu as pltpu

F32 = jnp.float32
BF16 = jnp.bfloat16

N_DEV = 8
D_MODEL = 1024
HEAD_DIM = 64
N_HEADS = 8
PAIR = 2 * HEAD_DIM
WIDTH = N_HEADS * HEAD_DIM
D_IN = 2304
D_INP = 2560
D_FF = 2816
BLK = 128
ROPE_THETA = 150000.0
REL_BUCKETS = 32
REL_MAX_DISTANCE = 2048
EPS = 1e-5
NEG = -1e30
BRANCHES = ((128, 1), (512, 4), (2048, 16))
Q_SCALE = HEAD_DIM ** -0.5

ADAM_LR = 0.001
ADAM_B1 = 0.9
ADAM_B2 = 0.999
ADAM_EPS = 1e-08
ADAM_WD = 0.01
ADAM_STEP = 10

VMEM_LIMIT = 56 * 1024 * 1024
MESH = pl.DeviceIdType.MESH

NT = (((1,), (1,)), ((), ()))
TN = (((0,), (0,)), ((), ()))

SMALL_ROWS = 56


def _params(sem=None):
    return pltpu.CompilerParams(dimension_semantics=sem, vmem_limit_bytes=VMEM_LIMIT)


def _sigmoid(x):
    return 1.0 / (1.0 + jnp.exp2(x * (-1.0 / math.log(2.0))))


def _rms_bwd(dh, xh, r, g):
    u = dh * g
    return r * (u - xh * jnp.mean(u * xh, axis=-1, keepdims=True))


def _rope_rot(t, first):
    return jnp.where(first, pltpu.roll(t, 96, 1), pltpu.roll(t, 32, 1))


N_CHUNK = WIDTH // PAIR


def _scr(tm):
    return pltpu.VMEM((N_CHUNK, tm, PAIR), F32)


def _scr_get(scr):
    return jnp.concatenate([scr[j] for j in range(N_CHUNK)], axis=1)


def _scr_put(scr, val):
    for j in range(N_CHUNK):
        scr[j] = val[:, j * PAIR:(j + 1) * PAIR]


def _unstride(view_ref, scr, dil, tm):
    n = tm // dil
    for r in range(dil):
        for j in range(N_CHUNK):
            col = r * WIDTH + j * PAIR
            scr.at[j][pl.ds(r, n, stride=dil), :] = view_ref[:, col:col + PAIR].astype(F32)


def _restride(scr, out_ref, dil, tm):
    n = tm // dil
    for r in range(dil):
        for j in range(N_CHUNK):
            col = r * WIDTH + j * PAIR
            rows = scr[j] if dil == 1 else scr.at[j][pl.ds(r, n, stride=dil), :]
            out_ref[:, col:col + PAIR] = rows.astype(out_ref.dtype)


def _view_specs(tm):
    return [pl.BlockSpec((tm // dil, dil * WIDTH), lambda i: (i, 0)) for _, dil in BRANCHES]


def _view_shapes(T, dtype):
    return [jax.ShapeDtypeStruct((T // dil, dil * WIDTH), dtype) for _, dil in BRANCHES]


def _norm_proj(x, g, w, b, cos, sin, *, tm=512):
    T = x.shape[0]

    def body(x_ref, g_ref, w_ref, b_ref, cos_ref, sin_ref, h_ref, qa_ref, ka_ref, va_ref, *rest):
        outs_b, ys = rest[:9], rest[9]
        xv = x_ref[...]
        r = lax.rsqrt(jnp.mean(xv * xv, axis=-1, keepdims=True) + EPS)
        h = (xv * r * g_ref[...]).astype(BF16)
        h_ref[...] = h
        cosv = cos_ref[...]
        sinv = sin_ref[...]
        lane = lax.broadcasted_iota(jnp.int32, (tm, PAIR), 1)
        first = (lane % HEAD_DIM) < (HEAD_DIM // 2)

        def proj(off):
            return (lax.dot_general(h, w_ref[off:off + 256, :], NT, preferred_element_type=F32)
                    + b_ref[:, off:off + 256])

        for (off, width, rot, scale), o_ref in zip(((0, 512, True, Q_SCALE), (512, 256, True, 1.0), (768, 256, False, 1.0)),
                                                   (qa_ref, ka_ref, va_ref)):
            for c in range(0, width, 256):
                y = proj(off + c)
                for j in range(0, 256, PAIR):
                    t = y[:, j:j + PAIR]
                    if rot:
                        t = t * cosv + _rope_rot(t, first) * sinv
                    if scale != 1.0:
                        t = t * scale
                    o_ref[:, c + j:c + j + PAIR] = t.astype(BF16)
        for n, (off, scale) in enumerate(((1024, Q_SCALE), (1536, 1.0), (2048, 1.0))):
            for c in range(0, WIDTH, 256):
                y = proj(off + c)
                y = y * scale if scale != 1.0 else y
                for j in range(0, 256, PAIR):
                    ys[(c + j) // PAIR] = y[:, j:j + PAIR]
            for (_, dil), o_ref in zip(BRANCHES, outs_b[3 * n:3 * n + 3]):
                _restride(ys, o_ref, dil, tm)

    row = lambda w_: pl.BlockSpec((tm, w_), lambda i: (i, 0))
    full = lambda a: pl.BlockSpec(a.shape, lambda i: (0, 0))
    return pl.pallas_call(
        body, name="norm_proj", grid=(T // tm,),
        in_specs=[row(D_MODEL), full(g), full(w), full(b), row(PAIR), row(PAIR)],
        out_specs=[row(D_MODEL), row(512), row(256), row(256)] + _view_specs(tm) * 3,
        out_shape=[jax.ShapeDtypeStruct((T, n), BF16) for n in (D_MODEL, 512, 256, 256)] + _view_shapes(T, BF16) * 3,
        scratch_shapes=[_scr(tm)],
        compiler_params=_params(("arbitrary",)),
    )(x, g, w, b, cos, sin)


SUB = 4
AHEAD = 2


def _attn_specs(kvw):
    q_spec = pl.BlockSpec((SUB * BLK, WIDTH), lambda r, i: (i, r))
    kc_spec = pl.BlockSpec((SUB * BLK, kvw), lambda r, i: (i, r))
    kp_spec = pl.BlockSpec((BLK, kvw), lambda r, i: (jnp.maximum(SUB * i - 1, 0), r))
    b_spec = pl.BlockSpec((2, N_HEADS, BLK, 2 * BLK), lambda r, i: (0, 0, 0, 0))
    return q_spec, kp_spec, kc_spec, b_spec


def _window(prev_ref, cur_ref, j, ksl):
    before = prev_ref[:, ksl] if j == 0 else cur_ref[(j - 1) * BLK:j * BLK, ksl]
    return jnp.concatenate([before, cur_ref[j * BLK:(j + 1) * BLK, ksl]], axis=0)


def _attn_fwd(q, k, v, bias, sinks, *, dil, kv_pairs, use_sink, name):
    L = q.shape[0]
    ns = L // (SUB * BLK)
    kvw = kv_pairs * PAIR
    rep = 4 // kv_pairs

    def body(sink_ref, q_ref, kp_ref, kc_ref, vp_ref, vc_ref, b_ref, o_ref, lse_ref):
        lane = lax.broadcasted_iota(jnp.int32, (1, PAIR), 1)
        lo = lane < HEAD_DIM
        first = jnp.where(pl.program_id(1) == 0, 1, 0)
        def scores(j, hp):
            rows = slice(j * BLK, (j + 1) * BLK)
            sl = slice(hp * PAIR, (hp + 1) * PAIR)
            ksl = slice((hp // rep) * PAIR, (hp // rep + 1) * PAIR)
            qp = q_ref[rows, sl]
            kk = _window(kp_ref, kc_ref, j, ksl)
            vv = _window(vp_ref, vc_ref, j, ksl)
            heads = []
            for e in range(2):
                h = 2 * hp + e
                msk = lo if e == 0 else jnp.logical_not(lo)
                qm = jnp.where(msk, qp, jnp.zeros_like(qp))
                s = lax.dot_general(qm, kk, NT, preferred_element_type=F32) + (b_ref[first, h] if j == 0 else b_ref[0, h])
                heads.append((h, msk, s))
            return rows, sl, vv, heads

        def outputs(rows, sl, vv, heads):
            o_pair = None
            lse_pair = None
            for h, msk, s in heads:
                m = jnp.max(s, axis=-1, keepdims=True)
                if use_sink:
                    sk = sink_ref[h]
                    m = jnp.maximum(m, sk)
                p = jnp.exp(s - m)
                l = jnp.sum(p, axis=-1, keepdims=True)
                if use_sink:
                    l = l + jnp.exp(sk - m)
                vm = jnp.where(msk, vv, jnp.zeros_like(vv))
                oe = jnp.dot(p.astype(BF16), vm, preferred_element_type=F32) * (1.0 / l)
                ls = m + jnp.log(l)
                if o_pair is None:
                    o_pair = oe
                    lse_pair = jnp.broadcast_to(ls, (BLK, PAIR))
                else:
                    o_pair = o_pair + oe
                    lse_pair = jnp.where(lo, lse_pair, ls)
            o_ref[rows, sl] = o_pair.astype(BF16)
            lse_ref[rows, sl] = lse_pair

        items = [(j, hp) for j in range(SUB) for hp in range(4)]
        queue = [scores(*it) for it in items[:AHEAD]]
        for n in range(len(items)):
            if n + AHEAD < len(items):
                queue.append(scores(*items[n + AHEAD]))
            outputs(*queue.pop(0))

    q_spec, kp_spec, kc_spec, b_spec = _attn_specs(kvw)
    return pl.pallas_call(
        body, name=name, grid=(dil, ns),
        in_specs=[pl.BlockSpec(memory_space=pltpu.SMEM), q_spec, kp_spec, kc_spec, kp_spec, kc_spec, b_spec],
        out_specs=[q_spec, q_spec],
        out_shape=[jax.ShapeDtypeStruct((L, dil * WIDTH), BF16), jax.ShapeDtypeStruct((L, dil * WIDTH), F32)],
        compiler_params=_params(("arbitrary", "arbitrary")),
    )(sinks, q, k, k, v, v, bias)


def _attn_bwd(q, k, v, o, do, lse, bias, sinks, *, dil, kv_pairs, use_sink, name):
    L = q.shape[0]
    ns = L // (SUB * BLK)
    n_steps = dil * ns
    kvw = kv_pairs * PAIR
    rep = 4 // kv_pairs
    last = slice((SUB - 1) * BLK, SUB * BLK)

    def body(sink_ref, q_ref, kp_ref, kc_ref, vp_ref, vc_ref, o_ref, do_ref, lse_ref, b_ref,
             dq_ref, dk_ref, dv_ref, dsum_ref, dsk_ref, pk_ref, pv_ref):
        t = pl.program_id(0)
        i = t % ns

        @pl.when(t == 0)
        def _():
            dsum_ref[...] = jnp.zeros_like(dsum_ref)
            dsk_ref[...] = jnp.zeros_like(dsk_ref)
            pk_ref[...] = jnp.zeros_like(pk_ref)
            pv_ref[...] = jnp.zeros_like(pv_ref)

        @pl.when(t < n_steps)
        def _():
            lo = lax.broadcasted_iota(jnp.int32, (1, PAIR), 1) < HEAD_DIM
            first = jnp.where(i == 0, 1, 0)
            dks = [[None] * kv_pairs for _ in range(SUB)]
            dvs = [[None] * kv_pairs for _ in range(SUB)]
            def scores(j, hp):
                rows = slice(j * BLK, (j + 1) * BLK)
                kvp = hp // rep
                sl = slice(hp * PAIR, (hp + 1) * PAIR)
                ksl = slice(kvp * PAIR, (kvp + 1) * PAIR)
                qp = q_ref[rows, sl]
                dop = do_ref[rows, sl]
                prod = dop.astype(F32) * o_ref[rows, sl].astype(F32)
                kk = _window(kp_ref, kc_ref, j, ksl)
                vv = _window(vp_ref, vc_ref, j, ksl)
                heads = []
                for e in range(2):
                    h = 2 * hp + e
                    msk = lo if e == 0 else jnp.logical_not(lo)
                    qm = jnp.where(msk, qp, jnp.zeros_like(qp))
                    dom = jnp.where(msk, dop, jnp.zeros_like(dop))
                    km = jnp.where(msk, kk, jnp.zeros_like(kk))
                    s = (lax.dot_general(qm, kk, NT, preferred_element_type=F32)
                         + (b_ref[first, h] if j == 0 else b_ref[0, h]))
                    dp = lax.dot_general(dom, vv, NT, preferred_element_type=F32)
                    heads.append((h, msk, qm, dom, km, s, dp))
                return j, rows, kvp, sl, prod, heads

            def grads(j, rows, kvp, sl, prod, heads):
                dq_pair = None
                c_pair = None
                qms, doms, dsbs, pbs = [], [], [], []
                for h, msk, qm, dom, km, s, dp in heads:
                    ls = lse_ref[rows, h * HEAD_DIM:h * HEAD_DIM + 1]
                    p = jnp.exp(s - ls)
                    delta = jnp.sum(jnp.where(msk, prod, 0.0), axis=-1, keepdims=True)
                    ds = p * (dp - delta)
                    if use_sink:
                        ce = jnp.exp(sink_ref[h] - ls) * delta
                        c_pair = jnp.broadcast_to(ce, (BLK, PAIR)) if c_pair is None else jnp.where(msk, ce, c_pair)
                    else:
                        dsum_ref[h] += ds
                    dsb = ds.astype(BF16)
                    dqe = jnp.dot(dsb, km, preferred_element_type=F32)
                    dq_pair = dqe if dq_pair is None else dq_pair + dqe
                    qms.append(qm)
                    doms.append(dom)
                    dsbs.append(dsb)
                    pbs.append(p.astype(BF16))
                dke = lax.dot_general(jnp.concatenate(dsbs, axis=0), jnp.concatenate(qms, axis=0), TN,
                                      preferred_element_type=F32)
                dve = lax.dot_general(jnp.concatenate(pbs, axis=0), jnp.concatenate(doms, axis=0), TN,
                                      preferred_element_type=F32)
                dks[j][kvp] = dke if dks[j][kvp] is None else dks[j][kvp] + dke
                dvs[j][kvp] = dve if dvs[j][kvp] is None else dvs[j][kvp] + dve
                dq_ref[rows, sl] = (dq_pair * Q_SCALE).astype(BF16)
                if use_sink:
                    dsk_ref[:, sl] += c_pair

            items = [(j, hp) for j in range(SUB) for hp in range(4)]
            ahead = AHEAD + 1 if use_sink else AHEAD
            queue = [scores(*it) for it in items[:ahead]]
            for n in range(len(items)):
                if n + ahead < len(items):
                    queue.append(scores(*items[n + ahead]))
                grads(*queue.pop(0))
            for kvp in range(kv_pairs):
                ksl = slice(kvp * PAIR, (kvp + 1) * PAIR)
                for pend_ref, out_ref, parts in ((pk_ref, dk_ref, [d[kvp] for d in dks]),
                                                 (pv_ref, dv_ref, [d[kvp] for d in dvs])):
                    if SUB > 1:
                        out_ref[:(SUB - 1) * BLK, ksl] = pend_ref[:(SUB - 1) * BLK, ksl].astype(BF16)
                    out_ref[last, ksl] = (pend_ref[last, ksl] + parts[0][:BLK]).astype(BF16)
                    for j in range(SUB):
                        own = parts[j][BLK:]
                        pend_ref[j * BLK:(j + 1) * BLK, ksl] = own + parts[j + 1][:BLK] if j + 1 < SUB else own

        @pl.when(t == n_steps)
        def _():
            dk_ref[...] = pk_ref[...].astype(BF16)
            dv_ref[...] = pv_ref[...].astype(BF16)

    def at(t):
        t = jnp.minimum(t, n_steps - 1)
        return t % ns, t // ns

    def before(t):
        return at(jnp.maximum(t - 1, 0))

    q_spec = pl.BlockSpec((SUB * BLK, WIDTH), at)
    kc_spec = pl.BlockSpec((SUB * BLK, kvw), at)
    kp_spec = pl.BlockSpec((BLK, kvw), lambda t: (jnp.maximum(SUB * at(t)[0] - 1, 0), at(t)[1]))
    b_spec = pl.BlockSpec((2, N_HEADS, BLK, 2 * BLK), lambda t: (0, 0, 0, 0))
    dkv_spec = pl.BlockSpec((SUB * BLK, kvw), before)
    return pl.pallas_call(
        body, name=name, grid=(n_steps + 1,),
        in_specs=[pl.BlockSpec(memory_space=pltpu.SMEM), q_spec, kp_spec, kc_spec, kp_spec, kc_spec,
                  q_spec, q_spec, q_spec, b_spec],
        out_specs=[q_spec, dkv_spec, dkv_spec,
                   pl.BlockSpec((N_HEADS, BLK, 2 * BLK), lambda t: (0, 0, 0)),
                   pl.BlockSpec((BLK, WIDTH), lambda t: (0, 0))],
        out_shape=[jax.ShapeDtypeStruct((L, dil * WIDTH), BF16),
                   jax.ShapeDtypeStruct((L, dil * kvw), BF16),
                   jax.ShapeDtypeStruct((L, dil * kvw), BF16),
                   jax.ShapeDtypeStruct((N_HEADS, BLK, 2 * BLK), F32),
                   jax.ShapeDtypeStruct((BLK, WIDTH), F32)],
        scratch_shapes=[pltpu.VMEM((SUB * BLK, kvw), F32), pltpu.VMEM((SUB * BLK, kvw), F32)],
        compiler_params=_params(("arbitrary",)),
    )(sinks, q, k, k, v, v, o, do, lse, bias)


def _merge_wo(x, oa, o1, o2, o3, l1, l2, l3, ga, gb, wo, gf, *, tm=512):
    T = x.shape[0]

    def body(x_ref, oa_ref, o1_ref, o2_ref, o3_ref, l1_ref, l2_ref, l3_ref, ga_ref, gb_ref, wo_ref, gf_ref,
             x2_ref, mix_ref, h2_ref, ob1_ref, ob4_ref, ob16_ref, ls1_ref, ls4_ref, ls16_ref, so2, so3, sl2, sl3):
        _unstride(o2_ref, so2, BRANCHES[1][1], tm)
        _unstride(o3_ref, so3, BRANCHES[2][1], tm)
        _unstride(l2_ref, sl2, BRANCHES[1][1], tm)
        _unstride(l3_ref, sl3, BRANCHES[2][1], tm)
        la, lb, lc = l1_ref[...], _scr_get(sl2), _scr_get(sl3)
        m = jnp.maximum(jnp.maximum(la, lb), lc)
        ea, eb, ec = jnp.exp(la - m), jnp.exp(lb - m), jnp.exp(lc - m)
        den = ea + eb + ec
        inv = 1.0 / den
        ob = (ea * o1_ref[...].astype(F32) + eb * _scr_get(so2) + ec * _scr_get(so3)) * inv
        _scr_put(so2, ob)
        _scr_put(sl2, m + jnp.log(den))
        for (_, dil), o_ref, l_ref in zip(BRANCHES, (ob1_ref, ob4_ref, ob16_ref), (ls1_ref, ls4_ref, ls16_ref)):
            _restride(so2, o_ref, dil, tm)
            _restride(sl2, l_ref, dil, tm)
        oav = oa_ref[...].astype(F32)
        ra = lax.rsqrt(jnp.mean(oav * oav, axis=-1, keepdims=True) + EPS)
        rb = lax.rsqrt(jnp.mean(ob * ob, axis=-1, keepdims=True) + EPS)
        mix_ref[:, :WIDTH] = (oav * ra * ga_ref[...]).astype(BF16)
        mix_ref[:, WIDTH:] = (ob * rb * gb_ref[...]).astype(BF16)
        x2 = x_ref[...] + jnp.dot(mix_ref[...], wo_ref[...], preferred_element_type=F32)
        x2_ref[...] = x2
        r2 = lax.rsqrt(jnp.mean(x2 * x2, axis=-1, keepdims=True) + EPS)
        h2_ref[...] = (x2 * r2 * gf_ref[...]).astype(BF16)

    row = lambda w_: pl.BlockSpec((tm, w_), lambda i: (i, 0))
    full = lambda a: pl.BlockSpec(a.shape, lambda i: (0, 0))
    return pl.pallas_call(
        body, name="merge_wo", grid=(T // tm,),
        in_specs=[row(D_MODEL), row(WIDTH)] + _view_specs(tm) * 2 + [full(ga), full(gb), full(wo), full(gf)],
        out_specs=[row(D_MODEL), row(D_MODEL), row(D_MODEL)] + _view_specs(tm) * 2,
        out_shape=[jax.ShapeDtypeStruct((T, D_MODEL), F32), jax.ShapeDtypeStruct((T, D_MODEL), BF16),
                   jax.ShapeDtypeStruct((T, D_MODEL), BF16)] + _view_shapes(T, BF16) + _view_shapes(T, F32),
        scratch_shapes=[_scr(tm)] * 4,
        compiler_params=_params(("arbitrary",)),
    )(x, oa, o1, o2, o3, l1, l2, l3, ga, gb, wo, gf)


def _ffn_up(h2, wgt, wut, *, tm=512, fc=D_FF, rc=512, cc=256):
    T = h2.shape[0]

    def body(h_ref, wg_ref, wu_ref, gate_ref, up_ref, act_ref):
        for s in range(0, tm, rc):
            h = h_ref[s:s + rc, :]
            for c in range(0, fc, cc):
                gt = lax.dot_general(h, wg_ref[c:c + cc, :], NT, preferred_element_type=F32)
                u = lax.dot_general(h, wu_ref[c:c + cc, :], NT, preferred_element_type=F32)
                gate_ref[s:s + rc, c:c + cc] = gt.astype(BF16)
                up_ref[s:s + rc, c:c + cc] = u.astype(BF16)
                act_ref[s:s + rc, c:c + cc] = (gt * _sigmoid(gt) * u).astype(BF16)

    rowd = pl.BlockSpec((tm, D_MODEL), lambda i, c: (i, 0))
    wrow = pl.BlockSpec((fc, D_MODEL), lambda i, c: (c, 0))
    oc = pl.BlockSpec((tm, fc), lambda i, c: (i, c))
    return pl.pallas_call(
        body, name="ffn_up", grid=(T // tm, D_FF // fc),
        in_specs=[rowd, wrow, wrow],
        out_specs=[oc, oc, oc],
        out_shape=[jax.ShapeDtypeStruct((T, D_FF), BF16)] * 3,
        compiler_params=_params(("arbitrary", "arbitrary")),
    )(h2, wgt, wut)


def _ffn_down_loss(act, wd, x2, tgt, g, *, tm=512, rc=256):
    T = x2.shape[0]

    def body(act_ref, wd_ref, x2_ref, tgt_ref, g_ref, dx_ref, dxb_ref, loss_ref, dg_ref):
        @pl.when(pl.program_id(0) == 0)
        def _():
            loss_ref[...] = jnp.zeros_like(loss_ref)
            dg_ref[...] = jnp.zeros_like(dg_ref)

        gv = g_ref[...]
        lsum = jnp.zeros((1, 1), F32)
        dgs = jnp.zeros((1, D_MODEL), F32)
        for c in range(0, tm, rc):
            x3 = x2_ref[c:c + rc, :] + jnp.dot(act_ref[c:c + rc, :], wd_ref[...], preferred_element_type=F32)
            r = lax.rsqrt(jnp.mean(x3 * x3, axis=-1, keepdims=True) + EPS)
            xh = x3 * r
            diff = xh * gv - tgt_ref[c:c + rc, :]
            lsum = lsum + jnp.sum(jnp.sum(diff * diff, axis=-1, keepdims=True), axis=0, keepdims=True)
            dy = diff * (1.0 / D_MODEL)
            dgs = dgs + jnp.sum(dy * xh, axis=0, keepdims=True)
            dx = _rms_bwd(dy, xh, r, gv)
            dx_ref[c:c + rc, :] = dx
            dxb_ref[c:c + rc, :] = dx.astype(BF16)
        loss_ref[...] += lsum * (0.5 / D_MODEL)
        dg_ref[...] += dgs

    rowd = pl.BlockSpec((tm, D_MODEL), lambda i: (i, 0))
    return pl.pallas_call(
        body, name="ffn_down_loss", grid=(T // tm,),
        in_specs=[pl.BlockSpec((tm, D_FF), lambda i: (i, 0)), pl.BlockSpec((D_FF, D_MODEL), lambda i: (0, 0)),
                  rowd, rowd, pl.BlockSpec(g.shape, lambda i: (0, 0))],
        out_specs=[rowd, rowd, pl.BlockSpec((1, 1), lambda i: (0, 0)), pl.BlockSpec((1, D_MODEL), lambda i: (0, 0))],
        out_shape=[jax.ShapeDtypeStruct((T, D_MODEL), F32), jax.ShapeDtypeStruct((T, D_MODEL), BF16),
                   jax.ShapeDtypeStruct((1, 1), F32), jax.ShapeDtypeStruct((1, D_MODEL), F32)],
        compiler_params=_params(("arbitrary",)),
    )(act, wd, x2, tgt, g)


def _ffn_bwd_act(dx3b, gate, up, wd, *, tm=512, fc=D_FF, rc=256, cc=256):
    T = dx3b.shape[0]

    def body(dxb_ref, gate_ref, up_ref, wd_ref, dgate_ref, dup_ref):
        for s in range(0, tm, rc):
            for c in range(0, fc, cc):
                dact = lax.dot_general(dxb_ref[s:s + rc, :], wd_ref[c:c + cc, :], NT, preferred_element_type=F32)
                gt = gate_ref[s:s + rc, c:c + cc].astype(F32)
                u = up_ref[s:s + rc, c:c + cc].astype(F32)
                sg = _sigmoid(gt)
                a = dact * sg
                dgate_ref[s:s + rc, c:c + cc] = (a * u * ((1.0 + gt) - gt * sg)).astype(BF16)
                dup_ref[s:s + rc, c:c + cc] = (a * gt).astype(BF16)

    rowd = pl.BlockSpec((tm, D_MODEL), lambda i, c: (i, 0))
    oc = pl.BlockSpec((tm, fc), lambda i, c: (i, c))
    return pl.pallas_call(
        body, name="ffn_bwd_act", grid=(T // tm, D_FF // fc),
        in_specs=[rowd, oc, oc, pl.BlockSpec((fc, D_MODEL), lambda i, c: (c, 0))],
        out_specs=[oc, oc],
        out_shape=[jax.ShapeDtypeStruct((T, D_FF), BF16), jax.ShapeDtypeStruct((T, D_FF), BF16)],
        compiler_params=_params(("arbitrary", "arbitrary")),
    )(dx3b, gate, up, wd)


def _ffn_bwd_in(dgate, dup, wgt, wut, x2, dx3, g, *, tm=512, rc=256):
    T = x2.shape[0]

    def body(dgate_ref, dup_ref, wg_ref, wu_ref, x2_ref, dx_ref, g_ref, dx2_ref, dx2b_ref, dg_ref):
        @pl.when(pl.program_id(0) == 0)
        def _():
            dg_ref[...] = jnp.zeros_like(dg_ref)

        gv = g_ref[...]
        dgs = jnp.zeros((1, D_MODEL), F32)
        for s in range(0, tm, rc):
            dh = (jnp.dot(dgate_ref[s:s + rc, :], wg_ref[...], preferred_element_type=F32)
                  + jnp.dot(dup_ref[s:s + rc, :], wu_ref[...], preferred_element_type=F32))
            xv = x2_ref[s:s + rc, :]
            r = lax.rsqrt(jnp.mean(xv * xv, axis=-1, keepdims=True) + EPS)
            xh = xv * r
            dgs = dgs + jnp.sum(dh * xh, axis=0, keepdims=True)
            d = dx_ref[s:s + rc, :] + _rms_bwd(dh, xh, r, gv)
            dx2_ref[s:s + rc, :] = d
            dx2b_ref[s:s + rc, :] = d.astype(BF16)
        dg_ref[...] += dgs

    rowd = pl.BlockSpec((tm, D_MODEL), lambda i: (i, 0))
    rowf = pl.BlockSpec((tm, D_FF), lambda i: (i, 0))
    wfull = pl.BlockSpec((D_FF, D_MODEL), lambda i: (0, 0))
    return pl.pallas_call(
        body, name="ffn_bwd_in", grid=(T // tm,),
        in_specs=[rowf, rowf, wfull, wfull, rowd, rowd, pl.BlockSpec(g.shape, lambda i: (0, 0))],
        out_specs=[rowd, rowd, pl.BlockSpec((1, D_MODEL), lambda i: (0, 0))],
        out_shape=[jax.ShapeDtypeStruct((T, D_MODEL), F32), jax.ShapeDtypeStruct((T, D_MODEL), BF16),
                   jax.ShapeDtypeStruct((1, D_MODEL), F32)],
        compiler_params=_params(("arbitrary",)),
    )(dgate, dup, wgt, wut, x2, dx3, g)


def _matmul_tn(a, b, *, tk, tn, tt=2048, out_dtype=BF16, name):
    T, K = a.shape
    N = b.shape[1]
    nt = T // tt

    def body(a_ref, b_ref, o_ref, acc_ref):
        part = lax.dot_general(a_ref[...], b_ref[...], TN, preferred_element_type=F32)

        @pl.when(pl.program_id(2) == 0)
        def _():
            acc_ref[...] = part

        @pl.when(pl.program_id(2) > 0)
        def _():
            acc_ref[...] += part

        @pl.when(pl.program_id(2) == nt - 1)
        def _():
            o_ref[...] = acc_ref[...].astype(out_dtype)

    return pl.pallas_call(
        body, name=name, grid=(K // tk, N // tn, nt),
        in_specs=[pl.BlockSpec((tt, tk), lambda i, j, t: (t, i)), pl.BlockSpec((tt, tn), lambda i, j, t: (t, j))],
        out_specs=pl.BlockSpec((tk, tn), lambda i, j, t: (i, j)),
        out_shape=jax.ShapeDtypeStruct((K, N), out_dtype),
        scratch_shapes=[pltpu.VMEM((tk, tn), F32)],
        compiler_params=_params(("arbitrary", "arbitrary", "arbitrary")),
    )(a, b)


def _wo_bwd(dx2b, wo, oa, ob, ga, gb, *, tm=512):
    T = dx2b.shape[0]

    def body(dx_ref, wo_ref, oa_ref, ob_ref, ga_ref, gb_ref, doa_ref, dob1_ref, dob4_ref, dob16_ref, dga_ref, dgb_ref, scr):
        @pl.when(pl.program_id(0) == 0)
        def _():
            dga_ref[...] = jnp.zeros_like(dga_ref)
            dgb_ref[...] = jnp.zeros_like(dgb_ref)

        dm = lax.dot_general(dx_ref[...], wo_ref[...], NT, preferred_element_type=F32)
        for o_ref, g_ref, dg_ref, sl in ((oa_ref, ga_ref, dga_ref, slice(0, WIDTH)),
                                         (ob_ref, gb_ref, dgb_ref, slice(WIDTH, 2 * WIDTH))):
            ov = o_ref[...].astype(F32)
            r = lax.rsqrt(jnp.mean(ov * ov, axis=-1, keepdims=True) + EPS)
            xh = ov * r
            d = dm[:, sl]
            dg_ref[...] += jnp.sum(d * xh, axis=0, keepdims=True)
            do = _rms_bwd(d, xh, r, g_ref[...])
            if o_ref is oa_ref:
                doa_ref[...] = do.astype(BF16)
            else:
                _scr_put(scr, do)
                for (_, dil), v_ref in zip(BRANCHES, (dob1_ref, dob4_ref, dob16_ref)):
                    _restride(scr, v_ref, dil, tm)

    row = lambda w_: pl.BlockSpec((tm, w_), lambda i: (i, 0))
    full = lambda a: pl.BlockSpec(a.shape, lambda i: (0, 0))
    return pl.pallas_call(
        body, name="wo_bwd", grid=(T // tm,),
        in_specs=[row(D_MODEL), full(wo), row(WIDTH), row(WIDTH), full(ga), full(gb)],
        out_specs=[row(WIDTH)] + _view_specs(tm)
        + [pl.BlockSpec((1, WIDTH), lambda i: (0, 0)), pl.BlockSpec((1, WIDTH), lambda i: (0, 0))],
        out_shape=[jax.ShapeDtypeStruct((T, WIDTH), BF16)] + _view_shapes(T, BF16)
        + [jax.ShapeDtypeStruct((1, WIDTH), F32), jax.ShapeDtypeStruct((1, WIDTH), F32)],
        scratch_shapes=[_scr(tm)],
        compiler_params=_params(("arbitrary",)),
    )(dx2b, wo, oa, ob, ga, gb)


def _inproj_bwd(dqa, dka, dva, dqs, dks, dvs, cos, sin, w, x, dx2, g, *, tm=512):
    T = dqa.shape[0]

    def body(dqa_ref, dka_ref, dva_ref, q1, q2, q3, k1, k2, k3, v1, v2, v3, cos_ref, sin_ref, w_ref, x_ref, dx2_ref,
             g_ref, dp_ref, db_ref, gx_ref, dg_ref, acc, tmp):
        @pl.when(pl.program_id(0) == 0)
        def _():
            db_ref[...] = jnp.zeros_like(db_ref)
            dg_ref[...] = jnp.zeros_like(dg_ref)

        cosv = cos_ref[...]
        sinv = sin_ref[...]
        lane = lax.broadcasted_iota(jnp.int32, (tm, PAIR), 1)
        first = (lane % HEAD_DIM) < (HEAD_DIM // 2)

        def put(off, val):
            dp_ref[:, off:off + PAIR] = val.astype(BF16)
            db_ref[:, off:off + PAIR] += jnp.sum(val, axis=0, keepdims=True)

        for src, off, width in ((dqa_ref, 0, 512), (dka_ref, 512, 256)):
            for j in range(0, width, PAIR):
                d = src[:, j:j + PAIR].astype(F32)
                put(off + j, d * cosv - _rope_rot(d, first) * sinv)
        for j in range(0, 256, PAIR):
            put(768 + j, dva_ref[:, j:j + PAIR].astype(F32))
        for (a, b, c), off in (((q1, q2, q3), 1024), ((k1, k2, k3), 1536), ((v1, v2, v3), 2048)):
            _unstride(b, acc, BRANCHES[1][1], tm)
            _unstride(c, tmp, BRANCHES[2][1], tm)
            for j in range(N_CHUNK):
                put(off + j * PAIR, a[:, j * PAIR:(j + 1) * PAIR].astype(F32) + acc[j] + tmp[j])

        dh = jnp.dot(dp_ref[...], w_ref[...], preferred_element_type=F32)
        xv = x_ref[...]
        r = lax.rsqrt(jnp.mean(xv * xv, axis=-1, keepdims=True) + EPS)
        xh = xv * r
        dg_ref[...] += jnp.sum(dh * xh, axis=0, keepdims=True)
        gx_ref[...] = dx2_ref[...] + _rms_bwd(dh, xh, r, g_ref[...])

    row = lambda w_: pl.BlockSpec((tm, w_), lambda i: (i, 0))
    full = lambda a: pl.BlockSpec(a.shape, lambda i: (0, 0))
    return pl.pallas_call(
        body, name="inproj_bwd", grid=(T // tm,),
        in_specs=[row(512), row(256), row(256)] + _view_specs(tm) * 3 + [row(PAIR), row(PAIR)]
        + [full(w), row(D_MODEL), row(D_MODEL), full(g)],
        out_specs=[row(D_INP), pl.BlockSpec((1, D_INP), lambda i: (0, 0)), row(D_MODEL),
                   pl.BlockSpec((1, D_MODEL), lambda i: (0, 0))],
        out_shape=[jax.ShapeDtypeStruct((T, D_INP), BF16), jax.ShapeDtypeStruct((1, D_INP), F32),
                   jax.ShapeDtypeStruct((T, D_MODEL), F32), jax.ShapeDtypeStruct((1, D_MODEL), F32)],
        scratch_shapes=[_scr(tm)] * 2,
        compiler_params=_params(("arbitrary",)),
    )(dqa, dka, dva, *dqs, *dks, *dvs, cos, sin, w, x, dx2, g)


def _bias_sink_grads(dsums, bmaps, dsk):
    def body(s1, s2, s3, m1, m2, m3, dsk_ref, drel_ref, dsink_ref):
        row = lax.broadcasted_iota(jnp.int32, (N_HEADS, 128), 0)
        lane = lax.broadcasted_iota(jnp.int32, (N_HEADS, 128), 1)
        out = jnp.zeros((N_HEADS, 128), F32)
        for s_ref, m_ref in ((s1, m1), (s2, m2), (s3, m3)):
            bm = m_ref[...]
            for h in range(N_HEADS):
                a = s_ref[h]
                for b in range(REL_BUCKETS):
                    v = jnp.sum(jnp.sum(jnp.where(bm == b, a, 0.0), axis=-1, keepdims=True), axis=0, keepdims=True)
                    out = out + jnp.where((row == h) & (lane == b), v, 0.0)
        drel_ref[...] = out
        dsink_ref[...] = -jnp.sum(dsk_ref[...], axis=0, keepdims=True)

    vm = pl.BlockSpec(memory_space=pltpu.VMEM)
    return pl.pallas_call(
        body, name="bias_sink_grads",
        in_specs=[vm] * 7, out_specs=[vm, vm],
        out_shape=[jax.ShapeDtypeStruct((N_HEADS, 128), F32), jax.ShapeDtypeStruct((1, WIDTH), F32)],
        compiler_params=_params(),
    )(*dsums, *bmaps, dsk)


def _all_gather(blk, *, name):
    R, C = blk.shape

    def body(x_ref, out_ref, send_sems, recv_sems, local_sem):
        x, y, c = lax.axis_index("x"), lax.axis_index("y"), lax.axis_index("c")
        me, sibling = (x, y, c), (x, y, 1 - c)
        chips = [(1 - x, y), (x, 1 - y), (1 - x, 1 - y)]

        def slot(px, py, pc):
            return out_ref.at[4 * px + 2 * py + pc]

        def copy(k, block, to, src=None):
            return pltpu.make_async_remote_copy(
                src_ref=slot(*block) if src is None else src, dst_ref=slot(*block),
                send_sem=send_sems.at[k], recv_sem=recv_sems.at[k], device_id=to, device_id_type=MESH)

        mine = pltpu.make_async_copy(x_ref, slot(*me), local_sem)
        mine.start()
        first = [copy(0, me, sibling, src=x_ref)]
        first += [copy(1 + j, me, (*chip, c), src=x_ref) for j, chip in enumerate(chips)]
        for cp in first:
            cp.start()
        passed = [copy(4 + j, (*chip, c), sibling) for j, chip in enumerate(chips)]
        for j, chip in enumerate(chips):
            copy(1 + j, (*chip, c), me).wait_recv()
            passed[j].start()
        copy(0, sibling, me).wait_recv()
        for j, chip in enumerate(chips):
            copy(4 + j, (*chip, 1 - c), me).wait_recv()
        for cp in first + passed:
            cp.wait_send()
        mine.wait()

    return pl.pallas_call(
        body, name=name,
        in_specs=[pl.BlockSpec(memory_space=pl.ANY)], out_specs=pl.BlockSpec(memory_space=pl.ANY),
        out_shape=jax.ShapeDtypeStruct((N_DEV, R, C), blk.dtype),
        scratch_shapes=[pltpu.SemaphoreType.DMA((7,)), pltpu.SemaphoreType.DMA((7,)), pltpu.SemaphoreType.DMA],
        compiler_params=pltpu.CompilerParams(has_side_effects=True),
    )(blk)


def _peers(x, y, c):
    return [(x ^ (k >> 2), y ^ ((k >> 1) & 1), c ^ (k & 1)) for k in range(1, N_DEV)]


_HBM = pl.BlockSpec(memory_space=pltpu.HBM)
_SEM = pl.BlockSpec(memory_space=pltpu.SEMAPHORE)
_EFFECT = pltpu.SideEffectType.DATAFLOW_SIDE_EFFECTING


def _exchange_start(srcs, *, gather, name):
    n = len(srcs)
    lands = [lax.empty((N_DEV,) + s.shape[-2:], s.dtype) for s in srcs]

    def body(*refs):
        src_refs, land_refs = refs[:n], refs[n:2 * n]
        send_sems, recv_sems = refs[2 * n], refs[2 * n + 1]
        token = refs[-1]
        x, y, c = lax.axis_index("x"), lax.axis_index("y"), lax.axis_index("c")
        mine = 4 * x + 2 * y + c
        for a in range(n):
            for k, peer in enumerate(_peers(x, y, c)):
                dest = 4 * peer[0] + 2 * peer[1] + peer[2]
                j = a * (N_DEV - 1) + k
                pltpu.make_async_remote_copy(
                    src_ref=src_refs[a] if gather else src_refs[a].at[dest], dst_ref=land_refs[a].at[mine],
                    send_sem=send_sems.at[j], recv_sem=recv_sems.at[j], device_id=peer, device_id_type=MESH).start()
        token[...] = jnp.zeros_like(token)

    sems = pltpu.SemaphoreType.DMA((n * (N_DEV - 1),))
    out = pl.pallas_call(
        body, name=name,
        out_shape=(sems, sems) + tuple(pltpu.HBM(a.shape, a.dtype) for a in list(srcs) + lands)
        + (jax.ShapeDtypeStruct((8, 128), F32),),
        in_specs=(_HBM,) * (2 * n), out_specs=(_SEM, _SEM) + (_HBM,) * (2 * n) + (pl.BlockSpec(memory_space=pltpu.VMEM),),
        input_output_aliases={i: 2 + i for i in range(2 * n)},
        compiler_params=pltpu.CompilerParams(has_side_effects=_EFFECT),
    )(*[pltpu.with_memory_space_constraint(a, pltpu.HBM) for a in list(srcs) + lands])
    return out[:-1], out[-1]


def _exchange_wait(state, after, *, gather, name):
    send_sems, recv_sems = state[0], state[1]
    n = (len(state) - 2) // 2
    arrays = state[2:]

    def body(*refs):
        src_refs, land_refs = refs[:n], refs[n:2 * n]
        send_sems, recv_sems = refs[2 * n], refs[2 * n + 1]
        x, y, c = lax.axis_index("x"), lax.axis_index("y"), lax.axis_index("c")
        for a in range(n):
            for k, peer in enumerate(_peers(x, y, c)):
                other = 4 * peer[0] + 2 * peer[1] + peer[2]
                j = a * (N_DEV - 1) + k
                copy = pltpu.make_async_remote_copy(
                    src_ref=src_refs[a] if gather else src_refs[a].at[other], dst_ref=land_refs[a].at[other],
                    send_sem=send_sems.at[j], recv_sem=recv_sems.at[j], device_id=peer, device_id_type=MESH)
                copy.wait_send()
                copy.wait_recv()

    out = pl.pallas_call(
        body, name=name,
        out_shape=tuple(pltpu.HBM(a.shape, a.dtype) for a in arrays),
        in_specs=(_HBM,) * (2 * n) + (_SEM, _SEM, pl.BlockSpec(memory_space=pl.ANY)), out_specs=(_HBM,) * (2 * n),
        input_output_aliases={i: i for i in range(2 * n)},
        compiler_params=pltpu.CompilerParams(has_side_effects=_EFFECT),
    )(*arrays, send_sems, recv_sems, after)
    mine = 4 * lax.axis_index("x") + 2 * lax.axis_index("y") + lax.axis_index("c")
    own = out[:n] if gather else [lax.dynamic_index_in_dim(s, mine, 0, keepdims=False) for s in out[:n]]
    return [lax.dynamic_update_slice(g, o[None], (mine, 0, 0)) for g, o in zip(out[n:], own)]


def _adam_math(w, g, m, v):
    m = ADAM_B1 * m + (1.0 - ADAM_B1) * g
    v = ADAM_B2 * v + (1.0 - ADAM_B2) * (g * g)
    m_hat = m / (1.0 - ADAM_B1 ** ADAM_STEP)
    v_hat = v / (1.0 - ADAM_B2 ** ADAM_STEP)
    delta = -ADAM_LR * (m_hat / (jnp.sqrt(v_hat) + ADAM_EPS) + ADAM_WD * w)
    return delta, m, v


def _adamw(parts, w, m, v, *, name):
    R, C = w.shape
    n_parts = parts.shape[0]
    tr = R // 2
    assert tr % 16 == 0

    def body(p_ref, w_ref, m_ref, v_ref, g_ref, d_ref, nm_ref, nv_ref):
        g = p_ref[0].astype(F32)
        for s in range(1, n_parts):
            g = g + p_ref[s].astype(F32)
        d, nm, nv = _adam_math(w_ref[...], g, m_ref[...], v_ref[...])
        g_ref[...] = g
        d_ref[...] = d
        nm_ref[...] = nm
        nv_ref[...] = nv

    blk = pl.BlockSpec((tr, C), lambda i: (i, 0))
    return pl.pallas_call(
        body, name=name, grid=(R // tr,),
        in_specs=[pl.BlockSpec((n_parts, tr, C), lambda i: (0, i, 0)), blk, blk, blk],
        out_specs=[blk] * 4, out_shape=[jax.ShapeDtypeStruct((R, C), F32)] * 4,
        compiler_params=_params(("arbitrary",)),
    )(parts, w, m, v)


def _adamw_small(parts, w, m, v):
    def body(p_ref, w_ref, m_ref, v_ref, g_ref, d_ref, nm_ref, nv_ref):
        g = p_ref[0]
        for s in range(1, N_DEV):
            g = g + p_ref[s]
        d, nm, nv = _adam_math(w_ref[...], g, m_ref[...], v_ref[...])
        g_ref[...] = g
        d_ref[...] = d
        nm_ref[...] = nm
        nv_ref[...] = nv

    vm = pl.BlockSpec(memory_space=pltpu.VMEM)
    return pl.pallas_call(
        body, name="adamw_small", in_specs=[vm] * 4, out_specs=[vm] * 4,
        out_shape=[jax.ShapeDtypeStruct((SMALL_ROWS, 128), F32)] * 4, compiler_params=_params(),
    )(parts, w, m, v)


def _t5_bucket(dist):
    max_exact = REL_BUCKETS // 2
    df = jnp.maximum(dist, 1).astype(F32)
    large = max_exact + (jnp.log(df / max_exact) / math.log(REL_MAX_DISTANCE / max_exact)
                         * (REL_BUCKETS - max_exact)).astype(jnp.int32)
    large = jnp.minimum(large, REL_BUCKETS - 1)
    return jnp.where(dist < max_exact, dist, large)


def _band_tables(rel_table, dil, n_back):
    qi = jnp.arange(BLK)[:, None]
    kj = jnp.arange(2 * BLK)[None, :]
    delta = BLK + qi - kj
    in_band = (delta >= 0) & (delta <= n_back)
    if rel_table is None:
        vals = jnp.zeros((N_HEADS, BLK, 2 * BLK), F32)
        bmap = None
    else:
        bucket = _t5_bucket(jnp.clip(delta, 0, n_back) * dil)
        vals = jnp.zeros((N_HEADS, BLK, 2 * BLK), F32)
        for b in range(REL_BUCKETS):
            vals = jnp.where((bucket == b)[None], rel_table[b][:, None, None], vals)
        bmap = jnp.where(in_band, bucket, -1).astype(jnp.int32)
    later = jnp.where(in_band[None], vals, NEG)
    first = jnp.where((in_band & (kj >= BLK))[None], vals, NEG)
    return jnp.stack([later, first]), bmap


def _rope_tables(T):
    half = HEAD_DIM // 2
    inv_freq = ROPE_THETA ** (-jnp.arange(half, dtype=F32) / half)
    ang = jnp.arange(T, dtype=F32)[:, None] * inv_freq[None, :]
    cos, sin = jnp.cos(ang), jnp.sin(ang)
    return jnp.tile(cos, (1, 4)), jnp.tile(jnp.concatenate([-sin, sin], axis=1), (1, 2))


def _widen_in(a, axis):
    sl = lambda lo, hi: lax.slice_in_dim(a, lo, hi, axis=axis)
    dup = lambda lo: [sl(lo, lo + 64), sl(lo, lo + 64), sl(lo + 64, lo + 128), sl(lo + 64, lo + 128)]
    return jnp.concatenate([sl(0, 512)] + dup(512) + dup(640) + [sl(768, D_IN)], axis=axis)


def _fold_in(a, axis):
    sl = lambda lo, hi: lax.slice_in_dim(a, lo, hi, axis=axis)
    fold = lambda lo: [sl(lo, lo + 64) + sl(lo + 64, lo + 128), sl(lo + 128, lo + 192) + sl(lo + 192, lo + 256)]
    return jnp.concatenate([sl(0, 512)] + fold(512) + fold(768) + [sl(1024, D_INP)], axis=axis)


def _local_step(x, tgt, g_attn, wint, b_in, sinks, rel_table, g_out_a, g_out_b, g_ffn, g_final, token,
                wo_fn, ffn_fn, early_fn):
    T = x.shape[0]
    cos, sin = _rope_tables(T)
    cos = cos + token[0, 0]
    winp = _widen_in(wint, 0)
    binp = _widen_in(b_in, 1)
    g_final2 = g_final.reshape(1, D_MODEL)
    sink8 = sinks.reshape(N_HEADS)

    bias_a, _ = _band_tables(None, 1, BLK - 1)
    tabs = [_band_tables(rel_table, dil, window // dil) for window, dil in BRANCHES]

    h1, qa, ka, va, *qkv_b = _norm_proj(x, g_attn, winp, binp, cos, sin)
    qbs, kbs, vbs = qkv_b[0:3], qkv_b[3:6], qkv_b[6:9]
    oa, lse_a = _attn_fwd(qa, ka, va, bias_a, sink8, dil=1, kv_pairs=2, use_sink=True, name="attn_a_fwd")
    outs = [_attn_fwd(qbs[n], kbs[n], vbs[n], tabs[n][0], sink8, dil=dil, kv_pairs=4, use_sink=False,
                      name=f"attn_b{n}_fwd") for n, (_, dil) in enumerate(BRANCHES)]
    wo = wo_fn(outs[2][1])
    x2, mixed, h2, *ob_lse = _merge_wo(x, oa, outs[0][0], outs[1][0], outs[2][0], outs[0][1], outs[1][1], outs[2][1],
                                       g_out_a, g_out_b, wo, g_ffn)
    obs, lses = ob_lse[0:3], ob_lse[3:6]
    wgt, wut, wd = ffn_fn(h2)
    gate, up, act = _ffn_up(h2, wgt, wut)
    dx3, dx3b, loss, dg_final = _ffn_down_loss(act, wd, x2, tgt, g_final2)

    dgate, dup = _ffn_bwd_act(dx3b, gate, up, wd)
    dx2, dx2b, dg_ffn = _ffn_bwd_in(dgate, dup, wgt, wut, x2, dx3, g_ffn)
    dwd = _matmul_tn(act, dx3b, tk=1408, tn=1024, name="dw_down")
    dwgt = _matmul_tn(dgate, h2, tk=1408, tn=1024, name="dw_gate")
    dwut = _matmul_tn(dup, h2, tk=1408, tn=1024, name="dw_up")
    dwo = _matmul_tn(mixed, dx2b, tk=1024, tn=1024, name="dw_o")
    early, token2 = early_fn(dict(w_o=dwo, w_gate=dwgt, w_up=dwut, w_down=dwd))
    doa, *dobs, dg_out_a, dg_out_b = _wo_bwd(dx2b, wo, oa, obs[0], g_out_a + token2[0, 0], g_out_b)

    dqa, dka, dva, _, dsk = _attn_bwd(qa, ka, va, oa, doa, lse_a, bias_a, sink8, dil=1, kv_pairs=2, use_sink=True,
                                      name="attn_a_bwd")
    res = [_attn_bwd(qbs[n], kbs[n], vbs[n], obs[n], dobs[n], lses[n], tabs[n][0], sink8, dil=dil, kv_pairs=4,
                     use_sink=False, name=f"attn_b{n}_bwd") for n, (_, dil) in enumerate(BRANCHES)]
    dp, dbp, grad_x, dg_attn = _inproj_bwd(dqa, dka, dva, [r[0] for r in res], [r[1] for r in res],
                                           [r[2] for r in res], cos, sin, winp, x, dx2, g_attn)
    dwin = _fold_in(_matmul_tn(dp, h1, tk=1280, tn=1024, out_dtype=F32, name="dw_in"), 0)
    drel, dsink = _bias_sink_grads([r[3] for r in res], [t[1] for t in tabs], dsk)

    small = dict(
        g_attn=dg_attn, b_in=_fold_in(dbp, 1), sinks=dsink[:, ::HEAD_DIM], rel_table=drel[:, :REL_BUCKETS].T,
        g_out_a=dg_out_a, g_out_b=dg_out_b, g_ffn=dg_ffn, g_final=dg_final.reshape(D_MODEL))
    return loss[0, 0], grad_x, dwin, early, small


SMALL_NAMES = ("g_attn", "b_in", "sinks", "rel_table", "g_out_a", "g_out_b", "g_ffn", "g_final")


def _pack_small(vals):
    flat = jnp.concatenate([vals[n].reshape(-1).astype(F32) for n in SMALL_NAMES])
    return jnp.pad(flat, (0, SMALL_ROWS * 128 - flat.shape[0])).reshape(SMALL_ROWS, 128)


def _unpack_small(packed, like):
    flat = packed.reshape(-1)
    out, off = {}, 0
    for n in SMALL_NAMES:
        size = like[n].size
        out[n] = flat[off:off + size].reshape(like[n].shape)
        off += size
    return out


def kernel(x, g_attn, w_in, b_in, sinks, rel_table, g_out_a, g_out_b, w_o, g_ffn, w_gate, w_up, w_down, g_final, loss_target, m_g_attn, m_w_in, m_b_in, m_sinks, m_rel_table, m_g_out_a, m_g_out_b, m_w_o, m_g_ffn, m_w_gate, m_w_up, m_w_down, m_g_final, v_g_attn, v_w_in, v_b_in, v_sinks, v_rel_table, v_g_out_a, v_g_out_b, v_w_o, v_g_ffn, v_w_gate, v_w_up, v_w_down, v_g_final):
    rest_names = ("w_o", "w_gate", "w_up", "w_down")

    rest = [w_o[0].astype(BF16), w_gate[0].astype(BF16).T, w_up[0].astype(BF16).T, w_down[0].astype(BF16)]
    wint = _all_gather(w_in[0].astype(BF16).T, name="gather_w_in").reshape(D_IN, D_MODEL)
    wint, rest = lax.optimization_barrier((wint, rest))
    wo_state, token_o = _exchange_start(rest[:1], gather=True, name="gather_w_o_start")
    token_o, ffn_src = lax.optimization_barrier((token_o, rest[1:]))
    ffn_state, token = _exchange_start(ffn_src, gather=True, name="gather_ffn_start")
    token = token + token_o

    def whole(got):
        return [g.reshape(N_DEV * g.shape[1], D_MODEL) for g in got]

    def wo_fn(after):
        return whole(_exchange_wait(wo_state, after, gather=True, name="gather_w_o_wait"))[0]

    def ffn_fn(after):
        return whole(_exchange_wait(ffn_state, after, gather=True, name="gather_ffn_wait"))

    def early_fn(dws):
        return _exchange_start([dws[n].reshape(N_DEV, -1, D_MODEL) for n in rest_names], gather=False,
                               name="scatter_rest_start")

    loss_part, grad_x, dwint, early_state, small = _local_step(
        x[0], loss_target[0], g_attn, wint, b_in, sinks, rel_table, g_out_a, g_out_b, g_ffn, g_final, token,
        wo_fn, ffn_fn, early_fn)
    loss = lax.psum(loss_part, ("x", "y", "c"))

    parts_in = dwint.astype(BF16).reshape(N_DEV, D_IN // N_DEV, D_MODEL)
    in_state, token3 = _exchange_start([parts_in], gather=False, name="scatter_w_in_start")
    got = _exchange_wait(early_state, token3, gather=False, name="scatter_rest_wait")

    def update(n, parts, w, m, v, transposed):
        if transposed:
            return [a.T[None] for a in _adamw(parts, w[0].T, m[0].T, v[0].T, name="adamw_" + n)]
        return [a[None] for a in _adamw(parts, w[0], m[0], v[0], name="adamw_" + n)]

    big = dict(w_o=update("w_o", got[0], w_o, m_w_o, v_w_o, False),
               w_gate=update("w_gate", got[1], w_gate, m_w_gate, v_w_gate, True),
               w_up=update("w_up", got[2], w_up, m_w_up, v_w_up, True),
               w_down=update("w_down", got[3], w_down, m_w_down, v_w_down, False))

    ws = dict(g_attn=g_attn, b_in=b_in, sinks=sinks, rel_table=rel_table, g_out_a=g_out_a, g_out_b=g_out_b,
              g_ffn=g_ffn, g_final=g_final)
    ms = dict(g_attn=m_g_attn, b_in=m_b_in, sinks=m_sinks, rel_table=m_rel_table, g_out_a=m_g_out_a,
              g_out_b=m_g_out_b, g_ffn=m_g_ffn, g_final=m_g_final)
    vs = dict(g_attn=v_g_attn, b_in=v_b_in, sinks=v_sinks, rel_table=v_rel_table, g_out_a=v_g_out_a,
              g_out_b=v_g_out_b, g_ffn=v_g_ffn, g_final=v_g_final)
    sparts = _all_gather(_pack_small(small), name="gather_small")
    sm_packed = _adamw_small(sparts, _pack_small(ws), _pack_small(ms), _pack_small(vs))
    sm = [_unpack_small(a, ws) for a in sm_packed]

    done = sm_packed[1][:1, :1] + sum(big[n][1][0, :1, :1] for n in rest_names)
    got_in = _exchange_wait(in_state, done, gather=False, name="scatter_w_in_wait")[0]
    big["w_in"] = update("w_in", got_in, w_in, m_w_in, v_w_in, True)

    order = ("g_attn", "w_in", "b_in", "sinks", "rel_table", "g_out_a", "g_out_b", "w_o", "g_ffn", "w_gate", "w_up",
             "w_down", "g_final")
    outs = [loss, grad_x[None]]
    for k in range(4):
        outs += [big[n][k] if n in big else sm[k][n] for n in order]
    return tuple(outs)
```

```python
import functools
import math

import jax
import jax.numpy as jnp
from jax import lax
from jax.experimental import pallas as pl
from jax.experimental.pallas import tpu as pltpu

F32 = jnp.float32
BF16 = jnp.bfloat16

N_DEV = 8
D_MODEL = 1024
HEAD_DIM = 64
N_HEADS = 8
PAIR = 2 * HEAD_DIM
WIDTH = N_HEADS * HEAD_DIM
D_IN = 2304
D_INP = 2560
D_FF = 2816
BLK = 128
ROPE_THETA = 150000.0
REL_BUCKETS = 32
REL_MAX_DISTANCE = 2048
EPS = 1e-5
NEG = -1e30
BRANCHES = ((128, 1), (512, 4), (2048, 16))
Q_SCALE = HEAD_DIM ** -0.5

ADAM_LR = 0.001
ADAM_B1 = 0.9
ADAM_B2 = 0.999
ADAM_EPS = 1e-08
ADAM_WD = 0.01
ADAM_STEP = 10

VMEM_LIMIT = 56 * 1024 * 1024
MESH = pl.DeviceIdType.MESH

NT = (((1,), (1,)), ((), ()))
TN = (((0,), (0,)), ((), ()))

SMALL_ROWS = 56


def _params(sem=None):
    return pltpu.CompilerParams(dimension_semantics=sem, vmem_limit_bytes=VMEM_LIMIT)


def _sigmoid(x):
    return 1.0 / (1.0 + jnp.exp2(x * (-1.0 / math.log(2.0))))


def _rms_bwd(dh, xh, r, g):
    u = dh * g
    return r * (u - xh * jnp.mean(u * xh, axis=-1, keepdims=True))


def _rope_rot(t, first):
    return jnp.where(first, pltpu.roll(t, 96, 1), pltpu.roll(t, 32, 1))


N_CHUNK = WIDTH // PAIR


def _scr(tm):
    return pltpu.VMEM((N_CHUNK, tm, PAIR), F32)


def _scr_get(scr):
    return jnp.concatenate([scr[j] for j in range(N_CHUNK)], axis=1)


def _scr_put(scr, val):
    for j in range(N_CHUNK):
        scr[j] = val[:, j * PAIR:(j + 1) * PAIR]


def _unstride(view_ref, scr, dil, tm):
    n = tm // dil
    for r in range(dil):
        for j in range(N_CHUNK):
            col = r * WIDTH + j * PAIR
            scr.at[j][pl.ds(r, n, stride=dil), :] = view_ref[:, col:col + PAIR].astype(F32)


def _restride(scr, out_ref, dil, tm):
    n = tm // dil
    for r in range(dil):
        for j in range(N_CHUNK):
            col = r * WIDTH + j * PAIR
            rows = scr[j] if dil == 1 else scr.at[j][pl.ds(r, n, stride=dil), :]
            out_ref[:, col:col + PAIR] = rows.astype(out_ref.dtype)


def _view_specs(tm):
    return [pl.BlockSpec((tm // dil, dil * WIDTH), lambda i: (i, 0)) for _, dil in BRANCHES]


def _view_shapes(T, dtype):
    return [jax.ShapeDtypeStruct((T // dil, dil * WIDTH), dtype) for _, dil in BRANCHES]


def _norm_proj(x, g, w, b, cos, sin, *, tm=512):
    T = x.shape[0]

    def body(x_ref, g_ref, w_ref, b_ref, cos_ref, sin_ref, h_ref, qa_ref, ka_ref, va_ref, *rest):
        outs_b, ys = rest[:9], rest[9]
        xv = x_ref[...]
        r = lax.rsqrt(jnp.mean(xv * xv, axis=-1, keepdims=True) + EPS)
        h = (xv * r * g_ref[...]).astype(BF16)
        h_ref[...] = h
        cosv = cos_ref[...]
        sinv = sin_ref[...]
        lane = lax.broadcasted_iota(jnp.int32, (tm, PAIR), 1)
        first = (lane % HEAD_DIM) < (HEAD_DIM // 2)

        def proj(off):
            return (lax.dot_general(h, w_ref[off:off + 256, :], NT, preferred_element_type=F32)
                    + b_ref[:, off:off + 256])

        for (off, width, rot, scale), o_ref in zip(((0, 512, True, Q_SCALE), (512, 256, True, 1.0), (768, 256, False, 1.0)),
                                                   (qa_ref, ka_ref, va_ref)):
            for c in range(0, width, 256):
                y = proj(off + c)
                for j in range(0, 256, PAIR):
                    t = y[:, j:j + PAIR]
                    if rot:
                        t = t * cosv + _rope_rot(t, first) * sinv
                    if scale != 1.0:
                        t = t * scale
                    o_ref[:, c + j:c + j + PAIR] = t.astype(BF16)
        for n, (off, scale) in enumerate(((1024, Q_SCALE), (1536, 1.0), (2048, 1.0))):
            for c in range(0, WIDTH, 256):
                y = proj(off + c)
                y = y * scale if scale != 1.0 else y
                for j in range(0, 256, PAIR):
                    ys[(c + j) // PAIR] = y[:, j:j + PAIR]
            for (_, dil), o_ref in zip(BRANCHES, outs_b[3 * n:3 * n + 3]):
                _restride(ys, o_ref, dil, tm)

    row = lambda w_: pl.BlockSpec((tm, w_), lambda i: (i, 0))
    full = lambda a: pl.BlockSpec(a.shape, lambda i: (0, 0))
    return pl.pallas_call(
        body, name="norm_proj", grid=(T // tm,),
        in_specs=[row(D_MODEL), full(g), full(w), full(b), row(PAIR), row(PAIR)],
        out_specs=[row(D_MODEL), row(512), row(256), row(256)] + _view_specs(tm) * 3,
        out_shape=[jax.ShapeDtypeStruct((T, n), BF16) for n in (D_MODEL, 512, 256, 256)] + _view_shapes(T, BF16) * 3,
        scratch_shapes=[_scr(tm)],
        compiler_params=_params(("arbitrary",)),
    )(x, g, w, b, cos, sin)


SUB = 4
AHEAD = 2


def _attn_specs(kvw):
    q_spec = pl.BlockSpec((SUB * BLK, WIDTH), lambda r, i: (i, r))
    kc_spec = pl.BlockSpec((SUB * BLK, kvw), lambda r, i: (i, r))
    kp_spec = pl.BlockSpec((BLK, kvw), lambda r, i: (jnp.maximum(SUB * i - 1, 0), r))
    b_spec = pl.BlockSpec((2, N_HEADS, BLK, 2 * BLK), lambda r, i: (0, 0, 0, 0))
    return q_spec, kp_spec, kc_spec, b_spec


def _window(prev_ref, cur_ref, j, ksl):
    before = prev_ref[:, ksl] if j == 0 else cur_ref[(j - 1) * BLK:j * BLK, ksl]
    return jnp.concatenate([before, cur_ref[j * BLK:(j + 1) * BLK, ksl]], axis=0)


def _attn_fwd(q, k, v, bias, sinks, *, dil, kv_pairs, use_sink, name):
    L = q.shape[0]
    ns = L // (SUB * BLK)
    kvw = kv_pairs * PAIR
    rep = 4 // kv_pairs

    def body(sink_ref, q_ref, kp_ref, kc_ref, vp_ref, vc_ref, b_ref, o_ref, lse_ref):
        lane = lax.broadcasted_iota(jnp.int32, (1, PAIR), 1)
        lo = lane < HEAD_DIM
        first = jnp.where(pl.program_id(1) == 0, 1, 0)
        def scores(j, hp):
            rows = slice(j * BLK, (j + 1) * BLK)
            sl = slice(hp * PAIR, (hp + 1) * PAIR)
            ksl = slice((hp // rep) * PAIR, (hp // rep + 1) * PAIR)
            qp = q_ref[rows, sl]
            kk = _window(kp_ref, kc_ref, j, ksl)
            vv = _window(vp_ref, vc_ref, j, ksl)
            heads = []
            for e in range(2):
                h = 2 * hp + e
                msk = lo if e == 0 else jnp.logical_not(lo)
                qm = jnp.where(msk, qp, jnp.zeros_like(qp))
                s = lax.dot_general(qm, kk, NT, preferred_element_type=F32) + (b_ref[first, h] if j == 0 else b_ref[0, h])
                heads.append((h, msk, s))
            return rows, sl, vv, heads

        def outputs(rows, sl, vv, heads):
            o_pair = None
            lse_pair = None
            for h, msk, s in heads:
                m = jnp.max(s, axis=-1, keepdims=True)
                if use_sink:
                    sk = sink_ref[h]
                    m = jnp.maximum(m, sk)
                p = jnp.exp(s - m)
                l = jnp.sum(p, axis=-1, keepdims=True)
                if use_sink:
                    l = l + jnp.exp(sk - m)
                vm = jnp.where(msk, vv, jnp.zeros_like(vv))
                oe = jnp.dot(p.astype(BF16), vm, preferred_element_type=F32) * (1.0 / l)
                ls = m + jnp.log(l)
                if o_pair is None:
                    o_pair = oe
                    lse_pair = jnp.broadcast_to(ls, (BLK, PAIR))
                else:
                    o_pair = o_pair + oe
                    lse_pair = jnp.where(lo, lse_pair, ls)
            o_ref[rows, sl] = o_pair.astype(BF16)
            lse_ref[rows, sl] = lse_pair

        items = [(j, hp) for j in range(SUB) for hp in range(4)]
        queue = [scores(*it) for it in items[:AHEAD]]
        for n in range(len(items)):
            if n + AHEAD < len(items):
                queue.append(scores(*items[n + AHEAD]))
            outputs(*queue.pop(0))

    q_spec, kp_spec, kc_spec, b_spec = _attn_specs(kvw)
    return pl.pallas_call(
        body, name=name, grid=(dil, ns),
        in_specs=[pl.BlockSpec(memory_space=pltpu.SMEM), q_spec, kp_spec, kc_spec, kp_spec, kc_spec, b_spec],
        out_specs=[q_spec, q_spec],
        out_shape=[jax.ShapeDtypeStruct((L, dil * WIDTH), BF16), jax.ShapeDtypeStruct((L, dil * WIDTH), F32)],
        compiler_params=_params(("arbitrary", "arbitrary")),
    )(sinks, q, k, k, v, v, bias)


def _attn_bwd(q, k, v, o, do, lse, bias, sinks, *, dil, kv_pairs, use_sink, name):
    L = q.shape[0]
    ns = L // (SUB * BLK)
    n_steps = dil * ns
    kvw = kv_pairs * PAIR
    rep = 4 // kv_pairs
    last = slice((SUB - 1) * BLK, SUB * BLK)

    def body(sink_ref, q_ref, kp_ref, kc_ref, vp_ref, vc_ref, o_ref, do_ref, lse_ref, b_ref,
             dq_ref, dk_ref, dv_ref, dsum_ref, dsk_ref, pk_ref, pv_ref):
        t = pl.program_id(0)
        i = t % ns

        @pl.when(t == 0)
        def _():
            dsum_ref[...] = jnp.zeros_like(dsum_ref)
            dsk_ref[...] = jnp.zeros_like(dsk_ref)
            pk_ref[...] = jnp.zeros_like(pk_ref)
            pv_ref[...] = jnp.zeros_like(pv_ref)

        @pl.when(t < n_steps)
        def _():
            lo = lax.broadcasted_iota(jnp.int32, (1, PAIR), 1) < HEAD_DIM
            first = jnp.where(i == 0, 1, 0)
            dks = [[None] * kv_pairs for _ in range(SUB)]
            dvs = [[None] * kv_pairs for _ in range(SUB)]
            def scores(j, hp):
                rows = slice(j * BLK, (j + 1) * BLK)
                kvp = hp // rep
                sl = slice(hp * PAIR, (hp + 1) * PAIR)
                ksl = slice(kvp * PAIR, (kvp + 1) * PAIR)
                qp = q_ref[rows, sl]
                dop = do_ref[rows, sl]
                prod = dop.astype(F32) * o_ref[rows, sl].astype(F32)
                kk = _window(kp_ref, kc_ref, j, ksl)
                vv = _window(vp_ref, vc_ref, j, ksl)
                heads = []
                for e in range(2):
                    h = 2 * hp + e
                    msk = lo if e == 0 else jnp.logical_not(lo)
                    qm = jnp.where(msk, qp, jnp.zeros_like(qp))
                    dom = jnp.where(msk, dop, jnp.zeros_like(dop))
                    km = jnp.where(msk, kk, jnp.zeros_like(kk))
                    s = (lax.dot_general(qm, kk, NT, preferred_element_type=F32)
                         + (b_ref[first, h] if j == 0 else b_ref[0, h]))
                    dp = lax.dot_general(dom, vv, NT, preferred_element_type=F32)
                    heads.append((h, msk, qm, dom, km, s, dp))
                return j, rows, kvp, sl, prod, heads

            def grads(j, rows, kvp, sl, prod, heads):
                dq_pair = None
                c_pair = None
                qms, doms, dsbs, pbs = [], [], [], []
                for h, msk, qm, dom, km, s, dp in heads:
                    ls = lse_ref[rows, h * HEAD_DIM:h * HEAD_DIM + 1]
                    p = jnp.exp(s - ls)
                    delta = jnp.sum(jnp.where(msk, prod, 0.0), axis=-1, keepdims=True)
                    ds = p * (dp - delta)
                    if use_sink:
                        ce = jnp.exp(sink_ref[h] - ls) * delta
                        c_pair = jnp.broadcast_to(ce, (BLK, PAIR)) if c_pair is None else jnp.where(msk, ce, c_pair)
                    else:
                        dsum_ref[h] += ds
                    dsb = ds.astype(BF16)
                    dqe = jnp.dot(dsb, km, preferred_element_type=F32)
                    dq_pair = dqe if dq_pair is None else dq_pair + dqe
                    qms.append(qm)
                    doms.append(dom)
                    dsbs.append(dsb)
                    pbs.append(p.astype(BF16))
                dke = lax.dot_general(jnp.concatenate(dsbs, axis=0), jnp.concatenate(qms, axis=0), TN,
                                      preferred_element_type=F32)
                dve = lax.dot_general(jnp.concatenate(pbs, axis=0), jnp.concatenate(doms, axis=0), TN,
                                      preferred_element_type=F32)
                dks[j][kvp] = dke if dks[j][kvp] is None else dks[j][kvp] + dke
                dvs[j][kvp] = dve if dvs[j][kvp] is None else dvs[j][kvp] + dve
                dq_ref[rows, sl] = (dq_pair * Q_SCALE).astype(BF16)
                if use_sink:
                    dsk_ref[:, sl] += c_pair

            items = [(j, hp) for j in range(SUB) for hp in range(4)]
            ahead = AHEAD + 1 if use_sink else AHEAD
            queue = [scores(*it) for it in items[:ahead]]
            for n in range(len(items)):
                if n + ahead < len(items):
                    queue.append(scores(*items[n + ahead]))
                grads(*queue.pop(0))
            for kvp in range(kv_pairs):
                ksl = slice(kvp * PAIR, (kvp + 1) * PAIR)
                for pend_ref, out_ref, parts in ((pk_ref, dk_ref, [d[kvp] for d in dks]),
                                                 (pv_ref, dv_ref, [d[kvp] for d in dvs])):
                    if SUB > 1:
                        out_ref[:(SUB - 1) * BLK, ksl] = pend_ref[:(SUB - 1) * BLK, ksl].astype(BF16)
                    out_ref[last, ksl] = (pend_ref[last, ksl] + parts[0][:BLK]).astype(BF16)
                    for j in range(SUB):
                        own = parts[j][BLK:]
                        pend_ref[j * BLK:(j + 1) * BLK, ksl] = own + parts[j + 1][:BLK] if j + 1 < SUB else own

        @pl.when(t == n_steps)
        def _():
            dk_ref[...] = pk_ref[...].astype(BF16)
            dv_ref[...] = pv_ref[...].astype(BF16)

    def at(t):
        t = jnp.minimum(t, n_steps - 1)
        return t % ns, t // ns

    def before(t):
        return at(jnp.maximum(t - 1, 0))

    q_spec = pl.BlockSpec((SUB * BLK, WIDTH), at)
    kc_spec = pl.BlockSpec((SUB * BLK, kvw), at)
    kp_spec = pl.BlockSpec((BLK, kvw), lambda t: (jnp.maximum(SUB * at(t)[0] - 1, 0), at(t)[1]))
    b_spec = pl.BlockSpec((2, N_HEADS, BLK, 2 * BLK), lambda t: (0, 0, 0, 0))
    dkv_spec = pl.BlockSpec((SUB * BLK, kvw), before)
    return pl.pallas_call(
        body, name=name, grid=(n_steps + 1,),
        in_specs=[pl.BlockSpec(memory_space=pltpu.SMEM), q_spec, kp_spec, kc_spec, kp_spec, kc_spec,
                  q_spec, q_spec, q_spec, b_spec],
        out_specs=[q_spec, dkv_spec, dkv_spec,
                   pl.BlockSpec((N_HEADS, BLK, 2 * BLK), lambda t: (0, 0, 0)),
                   pl.BlockSpec((BLK, WIDTH), lambda t: (0, 0))],
        out_shape=[jax.ShapeDtypeStruct((L, dil * WIDTH), BF16),
                   jax.ShapeDtypeStruct((L, dil * kvw), BF16),
                   jax.ShapeDtypeStruct((L, dil * kvw), BF16),
                   jax.ShapeDtypeStruct((N_HEADS, BLK, 2 * BLK), F32),
                   jax.ShapeDtypeStruct((BLK, WIDTH), F32)],
        scratch_shapes=[pltpu.VMEM((SUB * BLK, kvw), F32), pltpu.VMEM((SUB * BLK, kvw), F32)],
        compiler_params=_params(("arbitrary",)),
    )(sinks, q, k, k, v, v, o, do, lse, bias)


def _merge_wo(x, oa, o1, o2, o3, l1, l2, l3, ga, gb, wo, gf, *, tm=512):
    T = x.shape[0]

    def body(x_ref, oa_ref, o1_ref, o2_ref, o3_ref, l1_ref, l2_ref, l3_ref, ga_ref, gb_ref, wo_ref, gf_ref,
             x2_ref, mix_ref, h2_ref, ob1_ref, ob4_ref, ob16_ref, ls1_ref, ls4_ref, ls16_ref, so2, so3, sl2, sl3):
        _unstride(o2_ref, so2, BRANCHES[1][1], tm)
        _unstride(o3_ref, so3, BRANCHES[2][1], tm)
        _unstride(l2_ref, sl2, BRANCHES[1][1], tm)
        _unstride(l3_ref, sl3, BRANCHES[2][1], tm)
        la, lb, lc = l1_ref[...], _scr_get(sl2), _scr_get(sl3)
        m = jnp.maximum(jnp.maximum(la, lb), lc)
        ea, eb, ec = jnp.exp(la - m), jnp.exp(lb - m), jnp.exp(lc - m)
        den = ea + eb + ec
        inv = 1.0 / den
        ob = (ea * o1_ref[...].astype(F32) + eb * _scr_get(so2) + ec * _scr_get(so3)) * inv
        _scr_put(so2, ob)
        _scr_put(sl2, m + jnp.log(den))
        for (_, dil), o_ref, l_ref in zip(BRANCHES, (ob1_ref, ob4_ref, ob16_ref), (ls1_ref, ls4_ref, ls16_ref)):
            _restride(so2, o_ref, dil, tm)
            _restride(sl2, l_ref, dil, tm)
        oav = oa_ref[...].astype(F32)
        ra = lax.rsqrt(jnp.mean(oav * oav, axis=-1, keepdims=True) + EPS)
        rb = lax.rsqrt(jnp.mean(ob * ob, axis=-1, keepdims=True) + EPS)
        mix_ref[:, :WIDTH] = (oav * ra * ga_ref[...]).astype(BF16)
        mix_ref[:, WIDTH:] = (ob * rb * gb_ref[...]).astype(BF16)
        x2 = x_ref[...] + jnp.dot(mix_ref[...], wo_ref[...], preferred_element_type=F32)
        x2_ref[...] = x2
        r2 = lax.rsqrt(jnp.mean(x2 * x2, axis=-1, keepdims=True) + EPS)
        h2_ref[...] = (x2 * r2 * gf_ref[...]).astype(BF16)

    row = lambda w_: pl.BlockSpec((tm, w_), lambda i: (i, 0))
    full = lambda a: pl.BlockSpec(a.shape, lambda i: (0, 0))
    return pl.pallas_call(
        body, name="merge_wo", grid=(T // tm,),
        in_specs=[row(D_MODEL), row(WIDTH)] + _view_specs(tm) * 2 + [full(ga), full(gb), full(wo), full(gf)],
        out_specs=[row(D_MODEL), row(D_MODEL), row(D_MODEL)] + _view_specs(tm) * 2,
        out_shape=[jax.ShapeDtypeStruct((T, D_MODEL), F32), jax.ShapeDtypeStruct((T, D_MODEL), BF16),
                   jax.ShapeDtypeStruct((T, D_MODEL), BF16)] + _view_shapes(T, BF16) + _view_shapes(T, F32),
        scratch_shapes=[_scr(tm)] * 4,
        compiler_params=_params(("arbitrary",)),
    )(x, oa, o1, o2, o3, l1, l2, l3, ga, gb, wo, gf)


def _ffn_up(h2, wgt, wut, *, tm=512, fc=D_FF, rc=512, cc=256):
    T = h2.shape[0]

    def body(h_ref, wg_ref, wu_ref, gate_ref, up_ref, act_ref):
        for s in range(0, tm, rc):
            h = h_ref[s:s + rc, :]
            for c in range(0, fc, cc):
                gt = lax.dot_general(h, wg_ref[c:c + cc, :], NT, preferred_element_type=F32)
                u = lax.dot_general(h, wu_ref[c:c + cc, :], NT, preferred_element_type=F32)
                gate_ref[s:s + rc, c:c + cc] = gt.astype(BF16)
                up_ref[s:s + rc, c:c + cc] = u.astype(BF16)
                act_ref[s:s + rc, c:c + cc] = (gt * _sigmoid(gt) * u).astype(BF16)

    rowd = pl.BlockSpec((tm, D_MODEL), lambda i, c: (i, 0))
    wrow = pl.BlockSpec((fc, D_MODEL), lambda i, c: (c, 0))
    oc = pl.BlockSpec((tm, fc), lambda i, c: (i, c))
    return pl.pallas_call(
        body, name="ffn_up", grid=(T // tm, D_FF // fc),
        in_specs=[rowd, wrow, wrow],
        out_specs=[oc, oc, oc],
        out_shape=[jax.ShapeDtypeStruct((T, D_FF), BF16)] * 3,
        compiler_params=_params(("arbitrary", "arbitrary")),
    )(h2, wgt, wut)


def _ffn_down_loss(act, wd, x2, tgt, g, *, tm=512, rc=256):
    T = x2.shape[0]

    def body(act_ref, wd_ref, x2_ref, tgt_ref, g_ref, dx_ref, dxb_ref, loss_ref, dg_ref):
        @pl.when(pl.program_id(0) == 0)
        def _():
            loss_ref[...] = jnp.zeros_like(loss_ref)
            dg_ref[...] = jnp.zeros_like(dg_ref)

        gv = g_ref[...]
        lsum = jnp.zeros((1, 1), F32)
        dgs = jnp.zeros((1, D_MODEL), F32)
        for c in range(0, tm, rc):
            x3 = x2_ref[c:c + rc, :] + jnp.dot(act_ref[c:c + rc, :], wd_ref[...], preferred_element_type=F32)
            r = lax.rsqrt(jnp.mean(x3 * x3, axis=-1, keepdims=True) + EPS)
            xh = x3 * r
            diff = xh * gv - tgt_ref[c:c + rc, :]
            lsum = lsum + jnp.sum(jnp.sum(diff * diff, axis=-1, keepdims=True), axis=0, keepdims=True)
            dy = diff * (1.0 / D_MODEL)
            dgs = dgs + jnp.sum(dy * xh, axis=0, keepdims=True)
            dx = _rms_bwd(dy, xh, r, gv)
            dx_ref[c:c + rc, :] = dx
            dxb_ref[c:c + rc, :] = dx.astype(BF16)
        loss_ref[...] += lsum * (0.5 / D_MODEL)
        dg_ref[...] += dgs

    rowd = pl.BlockSpec((tm, D_MODEL), lambda i: (i, 0))
    return pl.pallas_call(
        body, name="ffn_down_loss", grid=(T // tm,),
        in_specs=[pl.BlockSpec((tm, D_FF), lambda i: (i, 0)), pl.BlockSpec((D_FF, D_MODEL), lambda i: (0, 0)),
                  rowd, rowd, pl.BlockSpec(g.shape, lambda i: (0, 0))],
        out_specs=[rowd, rowd, pl.BlockSpec((1, 1), lambda i: (0, 0)), pl.BlockSpec((1, D_MODEL), lambda i: (0, 0))],
        out_shape=[jax.ShapeDtypeStruct((T, D_MODEL), F32), jax.ShapeDtypeStruct((T, D_MODEL), BF16),
                   jax.ShapeDtypeStruct((1, 1), F32), jax.ShapeDtypeStruct((1, D_MODEL), F32)],
        compiler_params=_params(("arbitrary",)),
    )(act, wd, x2, tgt, g)


def _ffn_bwd_act(dx3b, gate, up, wd, *, tm=512, fc=D_FF, rc=256, cc=256):
    T = dx3b.shape[0]

    def body(dxb_ref, gate_ref, up_ref, wd_ref, dgate_ref, dup_ref):
        for s in range(0, tm, rc):
            for c in range(0, fc, cc):
                dact = lax.dot_general(dxb_ref[s:s + rc, :], wd_ref[c:c + cc, :], NT, preferred_element_type=F32)
                gt = gate_ref[s:s + rc, c:c + cc].astype(F32)
                u = up_ref[s:s + rc, c:c + cc].astype(F32)
                sg = _sigmoid(gt)
                a = dact * sg
                dgate_ref[s:s + rc, c:c + cc] = (a * u * ((1.0 + gt) - gt * sg)).astype(BF16)
                dup_ref[s:s + rc, c:c + cc] = (a * gt).astype(BF16)

    rowd = pl.BlockSpec((tm, D_MODEL), lambda i, c: (i, 0))
    oc = pl.BlockSpec((tm, fc), lambda i, c: (i, c))
    return pl.pallas_call(
        body, name="ffn_bwd_act", grid=(T // tm, D_FF // fc),
        in_specs=[rowd, oc, oc, pl.BlockSpec((fc, D_MODEL), lambda i, c: (c, 0))],
        out_specs=[oc, oc],
        out_shape=[jax.ShapeDtypeStruct((T, D_FF), BF16), jax.ShapeDtypeStruct((T, D_FF), BF16)],
        compiler_params=_params(("arbitrary", "arbitrary")),
    )(dx3b, gate, up, wd)


def _ffn_bwd(dx3, dx3b, gate, up, wd, wgt, wut, x2, g, *, tm=256, cc=256):
    T = x2.shape[0]

    def body(dx_ref, dxb_ref, gate_ref, up_ref, wd_ref, wg_ref, wu_ref, x2_ref, g_ref,
             dgate_ref, dup_ref, dx2_ref, dx2b_ref, dg_ref):
        @pl.when(pl.program_id(0) == 0)
        def _():
            dg_ref[...] = jnp.zeros_like(dg_ref)

        dxb = dxb_ref[...]
        for c in range(0, D_FF, cc):
            dact = lax.dot_general(dxb, wd_ref[c:c + cc, :], NT, preferred_element_type=F32)
            gt = gate_ref[:, c:c + cc].astype(F32)
            u = up_ref[:, c:c + cc].astype(F32)
            sg = _sigmoid(gt)
            a = dact * sg
            dgate_ref[:, c:c + cc] = (a * u * ((1.0 + gt) - gt * sg)).astype(BF16)
            dup_ref[:, c:c + cc] = (a * gt).astype(BF16)
        dh = (jnp.dot(dgate_ref[...], wg_ref[...], preferred_element_type=F32)
              + jnp.dot(dup_ref[...], wu_ref[...], preferred_element_type=F32))
        xv = x2_ref[...]
        r = lax.rsqrt(jnp.mean(xv * xv, axis=-1, keepdims=True) + EPS)
        xh = xv * r
        dg_ref[...] += jnp.sum(dh * xh, axis=0, keepdims=True)
        d = dx_ref[...] + _rms_bwd(dh, xh, r, g_ref[...])
        dx2_ref[...] = d
        dx2b_ref[...] = d.astype(BF16)

    rowd = pl.BlockSpec((tm, D_MODEL), lambda i: (i, 0))
    rowf = pl.BlockSpec((tm, D_FF), lambda i: (i, 0))
    wfull = pl.BlockSpec((D_FF, D_MODEL), lambda i: (0, 0), pipeline_mode=pl.Buffered(1))
    return pl.pallas_call(
        body, name="ffn_bwd", grid=(T // tm,),
        in_specs=[rowd, rowd, rowf, rowf, wfull, wfull, wfull, rowd, pl.BlockSpec(g.shape, lambda i: (0, 0))],
        out_specs=[rowf, rowf, rowd, rowd, pl.BlockSpec((1, D_MODEL), lambda i: (0, 0))],
        out_shape=[jax.ShapeDtypeStruct((T, D_FF), BF16), jax.ShapeDtypeStruct((T, D_FF), BF16),
                   jax.ShapeDtypeStruct((T, D_MODEL), F32), jax.ShapeDtypeStruct((T, D_MODEL), BF16),
                   jax.ShapeDtypeStruct((1, D_MODEL), F32)],
        compiler_params=_params(("arbitrary",)),
    )(dx3, dx3b, gate, up, wd, wgt, wut, x2, g)


def _ffn_bwd_in(dgate, dup, wgt, wut, x2, dx3, g, *, tm=512, rc=256):
    T = x2.shape[0]

    def body(dgate_ref, dup_ref, wg_ref, wu_ref, x2_ref, dx_ref, g_ref, dx2_ref, dx2b_ref, dg_ref):
        @pl.when(pl.program_id(0) == 0)
        def _():
            dg_ref[...] = jnp.zeros_like(dg_ref)

        gv = g_ref[...]
        dgs = jnp.zeros((1, D_MODEL), F32)
        for s in range(0, tm, rc):
            dh = (jnp.dot(dgate_ref[s:s + rc, :], wg_ref[...], preferred_element_type=F32)
                  + jnp.dot(dup_ref[s:s + rc, :], wu_ref[...], preferred_element_type=F32))
            xv = x2_ref[s:s + rc, :]
            r = lax.rsqrt(jnp.mean(xv * xv, axis=-1, keepdims=True) + EPS)
            xh = xv * r
            dgs = dgs + jnp.sum(dh * xh, axis=0, keepdims=True)
            d = dx_ref[s:s + rc, :] + _rms_bwd(dh, xh, r, gv)
            dx2_ref[s:s + rc, :] = d
            dx2b_ref[s:s + rc, :] = d.astype(BF16)
        dg_ref[...] += dgs

    rowd = pl.BlockSpec((tm, D_MODEL), lambda i: (i, 0))
    rowf = pl.BlockSpec((tm, D_FF), lambda i: (i, 0))
    wfull = pl.BlockSpec((D_FF, D_MODEL), lambda i: (0, 0))
    return pl.pallas_call(
        body, name="ffn_bwd_in", grid=(T // tm,),
        in_specs=[rowf, rowf, wfull, wfull, rowd, rowd, pl.BlockSpec(g.shape, lambda i: (0, 0))],
        out_specs=[rowd, rowd, pl.BlockSpec((1, D_MODEL), lambda i: (0, 0))],
        out_shape=[jax.ShapeDtypeStruct((T, D_MODEL), F32), jax.ShapeDtypeStruct((T, D_MODEL), BF16),
                   jax.ShapeDtypeStruct((1, D_MODEL), F32)],
        compiler_params=_params(("arbitrary",)),
    )(dgate, dup, wgt, wut, x2, dx3, g)


def _matmul_tn(a, b, *, tk, tn, tt=2048, out_dtype=BF16, name):
    T, K = a.shape
    N = b.shape[1]
    nt = T // tt

    def body(a_ref, b_ref, o_ref, acc_ref):
        part = lax.dot_general(a_ref[...], b_ref[...], TN, preferred_element_type=F32)

        @pl.when(pl.program_id(2) == 0)
        def _():
            acc_ref[...] = part

        @pl.when(pl.program_id(2) > 0)
        def _():
            acc_ref[...] += part

        @pl.when(pl.program_id(2) == nt - 1)
        def _():
            o_ref[...] = acc_ref[...].astype(out_dtype)

    return pl.pallas_call(
        body, name=name, grid=(K // tk, N // tn, nt),
        in_specs=[pl.BlockSpec((tt, tk), lambda i, j, t: (t, i)), pl.BlockSpec((tt, tn), lambda i, j, t: (t, j))],
        out_specs=pl.BlockSpec((tk, tn), lambda i, j, t: (i, j)),
        out_shape=jax.ShapeDtypeStruct((K, N), out_dtype),
        scratch_shapes=[pltpu.VMEM((tk, tn), F32)],
        compiler_params=_params(("arbitrary", "arbitrary", "arbitrary")),
    )(a, b)


def _wo_bwd(dx2b, wo, oa, ob, ga, gb, *, tm=512):
    T = dx2b.shape[0]

    def body(dx_ref, wo_ref, oa_ref, ob_ref, ga_ref, gb_ref, doa_ref, dob1_ref, dob4_ref, dob16_ref, dga_ref, dgb_ref, scr):
        @pl.when(pl.program_id(0) == 0)
        def _():
            dga_ref[...] = jnp.zeros_like(dga_ref)
            dgb_ref[...] = jnp.zeros_like(dgb_ref)

        dm = lax.dot_general(dx_ref[...], wo_ref[...], NT, preferred_element_type=F32)
        for o_ref, g_ref, dg_ref, sl in ((oa_ref, ga_ref, dga_ref, slice(0, WIDTH)),
                                         (ob_ref, gb_ref, dgb_ref, slice(WIDTH, 2 * WIDTH))):
            ov = o_ref[...].astype(F32)
            r = lax.rsqrt(jnp.mean(ov * ov, axis=-1, keepdims=True) + EPS)
            xh = ov * r
            d = dm[:, sl]
            dg_ref[...] += jnp.sum(d * xh, axis=0, keepdims=True)
            do = _rms_bwd(d, xh, r, g_ref[...])
            if o_ref is oa_ref:
                doa_ref[...] = do.astype(BF16)
            else:
                _scr_put(scr, do)
                for (_, dil), v_ref in zip(BRANCHES, (dob1_ref, dob4_ref, dob16_ref)):
                    _restride(scr, v_ref, dil, tm)

    row = lambda w_: pl.BlockSpec((tm, w_), lambda i: (i, 0))
    full = lambda a: pl.BlockSpec(a.shape, lambda i: (0, 0))
    return pl.pallas_call(
        body, name="wo_bwd", grid=(T // tm,),
        in_specs=[row(D_MODEL), full(wo), row(WIDTH), row(WIDTH), full(ga), full(gb)],
        out_specs=[row(WIDTH)] + _view_specs(tm)
        + [pl.BlockSpec((1, WIDTH), lambda i: (0, 0)), pl.BlockSpec((1, WIDTH), lambda i: (0, 0))],
        out_shape=[jax.ShapeDtypeStruct((T, WIDTH), BF16)] + _view_shapes(T, BF16)
        + [jax.ShapeDtypeStruct((1, WIDTH), F32), jax.ShapeDtypeStruct((1, WIDTH), F32)],
        scratch_shapes=[_scr(tm)],
        compiler_params=_params(("arbitrary",)),
    )(dx2b, wo, oa, ob, ga, gb)


def _inproj_bwd(dqa, dka, dva, dqs, dks, dvs, cos, sin, w, x, dx2, g, *, tm=512):
    T = dqa.shape[0]

    def body(dqa_ref, dka_ref, dva_ref, q1, q2, q3, k1, k2, k3, v1, v2, v3, cos_ref, sin_ref, w_ref, x_ref, dx2_ref,
             g_ref, dp_ref, db_ref, gx_ref, dg_ref, acc, tmp):
        @pl.when(pl.program_id(0) == 0)
        def _():
            db_ref[...] = jnp.zeros_like(db_ref)
            dg_ref[...] = jnp.zeros_like(dg_ref)

        cosv = cos_ref[...]
        sinv = sin_ref[...]
        lane = lax.broadcasted_iota(jnp.int32, (tm, PAIR), 1)
        first = (lane % HEAD_DIM) < (HEAD_DIM // 2)

        def put(off, val):
            dp_ref[:, off:off + PAIR] = val.astype(BF16)
            db_ref[:, off:off + PAIR] += jnp.sum(val, axis=0, keepdims=True)

        for src, off, width in ((dqa_ref, 0, 512), (dka_ref, 512, 256)):
            for j in range(0, width, PAIR):
                d = src[:, j:j + PAIR].astype(F32)
                put(off + j, d * cosv - _rope_rot(d, first) * sinv)
        for j in range(0, 256, PAIR):
            put(768 + j, dva_ref[:, j:j + PAIR].astype(F32))
        for (a, b, c), off in (((q1, q2, q3), 1024), ((k1, k2, k3), 1536), ((v1, v2, v3), 2048)):
            _unstride(b, acc, BRANCHES[1][1], tm)
            _unstride(c, tmp, BRANCHES[2][1], tm)
            for j in range(N_CHUNK):
                put(off + j * PAIR, a[:, j * PAIR:(j + 1) * PAIR].astype(F32) + acc[j] + tmp[j])

        dh = jnp.dot(dp_ref[...], w_ref[...], preferred_element_type=F32)
        xv = x_ref[...]
        r = lax.rsqrt(jnp.mean(xv * xv, axis=-1, keepdims=True) + EPS)
        xh = xv * r
        dg_ref[...] += jnp.sum(dh * xh, axis=0, keepdims=True)
        gx_ref[...] = dx2_ref[...] + _rms_bwd(dh, xh, r, g_ref[...])

    row = lambda w_: pl.BlockSpec((tm, w_), lambda i: (i, 0))
    full = lambda a: pl.BlockSpec(a.shape, lambda i: (0, 0))
    return pl.pallas_call(
        body, name="inproj_bwd", grid=(T // tm,),
        in_specs=[row(512), row(256), row(256)] + _view_specs(tm) * 3 + [row(PAIR), row(PAIR)]
        + [full(w), row(D_MODEL), row(D_MODEL), full(g)],
        out_specs=[row(D_INP), pl.BlockSpec((1, D_INP), lambda i: (0, 0)), row(D_MODEL),
                   pl.BlockSpec((1, D_MODEL), lambda i: (0, 0))],
        out_shape=[jax.ShapeDtypeStruct((T, D_INP), BF16), jax.ShapeDtypeStruct((1, D_INP), F32),
                   jax.ShapeDtypeStruct((T, D_MODEL), F32), jax.ShapeDtypeStruct((1, D_MODEL), F32)],
        scratch_shapes=[_scr(tm)] * 2,
        compiler_params=_params(("arbitrary",)),
    )(dqa, dka, dva, *dqs, *dks, *dvs, cos, sin, w, x, dx2, g)


def _bias_sink_grads(dsums, bmaps, dsk):
    def body(s1, s2, s3, m1, m2, m3, dsk_ref, drel_ref, dsink_ref):
        row = lax.broadcasted_iota(jnp.int32, (N_HEADS, 128), 0)
        lane = lax.broadcasted_iota(jnp.int32, (N_HEADS, 128), 1)
        out = jnp.zeros((N_HEADS, 128), F32)
        for s_ref, m_ref in ((s1, m1), (s2, m2), (s3, m3)):
            bm = m_ref[...]
            for h in range(N_HEADS):
                a = s_ref[h]
                for b in range(REL_BUCKETS):
                    v = jnp.sum(jnp.sum(jnp.where(bm == b, a, 0.0), axis=-1, keepdims=True), axis=0, keepdims=True)
                    out = out + jnp.where((row == h) & (lane == b), v, 0.0)
        drel_ref[...] = out
        dsink_ref[...] = -jnp.sum(dsk_ref[...], axis=0, keepdims=True)

    vm = pl.BlockSpec(memory_space=pltpu.VMEM)
    return pl.pallas_call(
        body, name="bias_sink_grads",
        in_specs=[vm] * 7, out_specs=[vm, vm],
        out_shape=[jax.ShapeDtypeStruct((N_HEADS, 128), F32), jax.ShapeDtypeStruct((1, WIDTH), F32)],
        compiler_params=_params(),
    )(*dsums, *bmaps, dsk)


def _all_gather(blk, *, name):
    R, C = blk.shape

    def body(x_ref, out_ref, send_sems, recv_sems, local_sem):
        x, y, c = lax.axis_index("x"), lax.axis_index("y"), lax.axis_index("c")
        me, sibling = (x, y, c), (x, y, 1 - c)
        chips = [(1 - x, y), (x, 1 - y), (1 - x, 1 - y)]

        def slot(px, py, pc):
            return out_ref.at[4 * px + 2 * py + pc]

        def copy(k, block, to, src=None):
            return pltpu.make_async_remote_copy(
                src_ref=slot(*block) if src is None else src, dst_ref=slot(*block),
                send_sem=send_sems.at[k], recv_sem=recv_sems.at[k], device_id=to, device_id_type=MESH)

        mine = pltpu.make_async_copy(x_ref, slot(*me), local_sem)
        mine.start()
        first = [copy(0, me, sibling, src=x_ref)]
        first += [copy(1 + j, me, (*chip, c), src=x_ref) for j, chip in enumerate(chips)]
        for cp in first:
            cp.start()
        passed = [copy(4 + j, (*chip, c), sibling) for j, chip in enumerate(chips)]
        for j, chip in enumerate(chips):
            copy(1 + j, (*chip, c), me).wait_recv()
            passed[j].start()
        copy(0, sibling, me).wait_recv()
        for j, chip in enumerate(chips):
            copy(4 + j, (*chip, 1 - c), me).wait_recv()
        for cp in first + passed:
            cp.wait_send()
        mine.wait()

    return pl.pallas_call(
        body, name=name,
        in_specs=[pl.BlockSpec(memory_space=pl.ANY)], out_specs=pl.BlockSpec(memory_space=pl.ANY),
        out_shape=jax.ShapeDtypeStruct((N_DEV, R, C), blk.dtype),
        scratch_shapes=[pltpu.SemaphoreType.DMA((7,)), pltpu.SemaphoreType.DMA((7,)), pltpu.SemaphoreType.DMA],
        compiler_params=pltpu.CompilerParams(has_side_effects=True),
    )(blk)


def _peers(x, y, c):
    return [(x ^ (k >> 2), y ^ ((k >> 1) & 1), c ^ (k & 1)) for k in range(1, N_DEV)]


_HBM = pl.BlockSpec(memory_space=pltpu.HBM)
_SEM = pl.BlockSpec(memory_space=pltpu.SEMAPHORE)
_EFFECT = pltpu.SideEffectType.DATAFLOW_SIDE_EFFECTING


def _exchange_start(srcs, *, gather, name):
    n = len(srcs)
    lands = [lax.empty((N_DEV,) + s.shape[-2:], s.dtype) for s in srcs]

    def body(*refs):
        src_refs, land_refs = refs[:n], refs[n:2 * n]
        send_sems, recv_sems = refs[2 * n], refs[2 * n + 1]
        token = refs[-1]
        x, y, c = lax.axis_index("x"), lax.axis_index("y"), lax.axis_index("c")
        mine = 4 * x + 2 * y + c
        for a in range(n):
            for k, peer in enumerate(_peers(x, y, c)):
                dest = 4 * peer[0] + 2 * peer[1] + peer[2]
                j = a * (N_DEV - 1) + k
                pltpu.make_async_remote_copy(
                    src_ref=src_refs[a] if gather else src_refs[a].at[dest], dst_ref=land_refs[a].at[mine],
                    send_sem=send_sems.at[j], recv_sem=recv_sems.at[j], device_id=peer, device_id_type=MESH).start()
        token[...] = jnp.zeros_like(token)

    sems = pltpu.SemaphoreType.DMA((n * (N_DEV - 1),))
    out = pl.pallas_call(
        body, name=name,
        out_shape=(sems, sems) + tuple(pltpu.HBM(a.shape, a.dtype) for a in list(srcs) + lands)
        + (jax.ShapeDtypeStruct((8, 128), F32),),
        in_specs=(_HBM,) * (2 * n), out_specs=(_SEM, _SEM) + (_HBM,) * (2 * n) + (pl.BlockSpec(memory_space=pltpu.VMEM),),
        input_output_aliases={i: 2 + i for i in range(2 * n)},
        compiler_params=pltpu.CompilerParams(has_side_effects=_EFFECT),
    )(*[pltpu.with_memory_space_constraint(a, pltpu.HBM) for a in list(srcs) + lands])
    return out[:-1], out[-1]


def _exchange_wait(state, after, *, gather, name):
    send_sems, recv_sems = state[0], state[1]
    n = (len(state) - 2) // 2
    arrays = state[2:]

    def body(*refs):
        src_refs, land_refs = refs[:n], refs[n:2 * n]
        send_sems, recv_sems = refs[2 * n], refs[2 * n + 1]
        x, y, c = lax.axis_index("x"), lax.axis_index("y"), lax.axis_index("c")
        for a in range(n):
            for k, peer in enumerate(_peers(x, y, c)):
                other = 4 * peer[0] + 2 * peer[1] + peer[2]
                j = a * (N_DEV - 1) + k
                copy = pltpu.make_async_remote_copy(
                    src_ref=src_refs[a] if gather else src_refs[a].at[other], dst_ref=land_refs[a].at[other],
                    send_sem=send_sems.at[j], recv_sem=recv_sems.at[j], device_id=peer, device_id_type=MESH)
                copy.wait_send()
                copy.wait_recv()

    out = pl.pallas_call(
        body, name=name,
        out_shape=tuple(pltpu.HBM(a.shape, a.dtype) for a in arrays),
        in_specs=(_HBM,) * (2 * n) + (_SEM, _SEM, pl.BlockSpec(memory_space=pl.ANY)), out_specs=(_HBM,) * (2 * n),
        input_output_aliases={i: i for i in range(2 * n)},
        compiler_params=pltpu.CompilerParams(has_side_effects=_EFFECT),
    )(*arrays, send_sems, recv_sems, after)
    mine = 4 * lax.axis_index("x") + 2 * lax.axis_index("y") + lax.axis_index("c")
    own = out[:n] if gather else [lax.dynamic_index_in_dim(s, mine, 0, keepdims=False) for s in out[:n]]
    return [lax.dynamic_update_slice(g, o[None], (mine, 0, 0)) for g, o in zip(out[n:], own)]


def _adam_math(w, g, m, v):
    m = ADAM_B1 * m + (1.0 - ADAM_B1) * g
    v = ADAM_B2 * v + (1.0 - ADAM_B2) * (g * g)
    m_hat = m / (1.0 - ADAM_B1 ** ADAM_STEP)
    v_hat = v / (1.0 - ADAM_B2 ** ADAM_STEP)
    delta = -ADAM_LR * (m_hat / (jnp.sqrt(v_hat) + ADAM_EPS) + ADAM_WD * w)
    return delta, m, v


def _adamw(parts, w, m, v, *, name):
    R, C = w.shape
    n_parts = parts.shape[0]
    tr = R // 2
    assert tr % 16 == 0

    def body(p_ref, w_ref, m_ref, v_ref, g_ref, d_ref, nm_ref, nv_ref):
        g = p_ref[0].astype(F32)
        for s in range(1, n_parts):
            g = g + p_ref[s].astype(F32)
        d, nm, nv = _adam_math(w_ref[...], g, m_ref[...], v_ref[...])
        g_ref[...] = g
        d_ref[...] = d
        nm_ref[...] = nm
        nv_ref[...] = nv

    blk = pl.BlockSpec((tr, C), lambda i: (i, 0))
    return pl.pallas_call(
        body, name=name, grid=(R // tr,),
        in_specs=[pl.BlockSpec((n_parts, tr, C), lambda i: (0, i, 0)), blk, blk, blk],
        out_specs=[blk] * 4, out_shape=[jax.ShapeDtypeStruct((R, C), F32)] * 4,
        compiler_params=_params(("arbitrary",)),
    )(parts, w, m, v)


def _adamw_small(parts, w, m, v):
    def body(p_ref, w_ref, m_ref, v_ref, g_ref, d_ref, nm_ref, nv_ref):
        g = p_ref[0]
        for s in range(1, N_DEV):
            g = g + p_ref[s]
        d, nm, nv = _adam_math(w_ref[...], g, m_ref[...], v_ref[...])
        g_ref[...] = g
        d_ref[...] = d
        nm_ref[...] = nm
        nv_ref[...] = nv

    vm = pl.BlockSpec(memory_space=pltpu.VMEM)
    return pl.pallas_call(
        body, name="adamw_small", in_specs=[vm] * 4, out_specs=[vm] * 4,
        out_shape=[jax.ShapeDtypeStruct((SMALL_ROWS, 128), F32)] * 4, compiler_params=_params(),
    )(parts, w, m, v)


def _t5_bucket(dist):
    max_exact = REL_BUCKETS // 2
    df = jnp.maximum(dist, 1).astype(F32)
    large = max_exact + (jnp.log(df / max_exact) / math.log(REL_MAX_DISTANCE / max_exact)
                         * (REL_BUCKETS - max_exact)).astype(jnp.int32)
    large = jnp.minimum(large, REL_BUCKETS - 1)
    return jnp.where(dist < max_exact, dist, large)


def _band_tables(rel_table, dil, n_back):
    qi = jnp.arange(BLK)[:, None]
    kj = jnp.arange(2 * BLK)[None, :]
    delta = BLK + qi - kj
    in_band = (delta >= 0) & (delta <= n_back)
    if rel_table is None:
        vals = jnp.zeros((N_HEADS, BLK, 2 * BLK), F32)
        bmap = None
    else:
        bucket = _t5_bucket(jnp.clip(delta, 0, n_back) * dil)
        vals = jnp.zeros((N_HEADS, BLK, 2 * BLK), F32)
        for b in range(REL_BUCKETS):
            vals = jnp.where((bucket == b)[None], rel_table[b][:, None, None], vals)
        bmap = jnp.where(in_band, bucket, -1).astype(jnp.int32)
    later = jnp.where(in_band[None], vals, NEG)
    first = jnp.where((in_band & (kj >= BLK))[None], vals, NEG)
    return jnp.stack([later, first]), bmap


def _rope_tables(T):
    half = HEAD_DIM // 2
    inv_freq = ROPE_THETA ** (-jnp.arange(half, dtype=F32) / half)
    ang = jnp.arange(T, dtype=F32)[:, None] * inv_freq[None, :]
    cos, sin = jnp.cos(ang), jnp.sin(ang)
    return jnp.tile(cos, (1, 4)), jnp.tile(jnp.concatenate([-sin, sin], axis=1), (1, 2))


def _widen_in(a, axis):
    sl = lambda lo, hi: lax.slice_in_dim(a, lo, hi, axis=axis)
    dup = lambda lo: [sl(lo, lo + 64), sl(lo, lo + 64), sl(lo + 64, lo + 128), sl(lo + 64, lo + 128)]
    return jnp.concatenate([sl(0, 512)] + dup(512) + dup(640) + [sl(768, D_IN)], axis=axis)


def _fold_in(a, axis):
    sl = lambda lo, hi: lax.slice_in_dim(a, lo, hi, axis=axis)
    fold = lambda lo: [sl(lo, lo + 64) + sl(lo + 64, lo + 128), sl(lo + 128, lo + 192) + sl(lo + 192, lo + 256)]
    return jnp.concatenate([sl(0, 512)] + fold(512) + fold(768) + [sl(1024, D_INP)], axis=axis)


def _local_step(x, tgt, g_attn, wint, b_in, sinks, rel_table, g_out_a, g_out_b, g_ffn, g_final, token,
                wo_fn, ffn_fn, early_fn):
    T = x.shape[0]
    cos, sin = _rope_tables(T)
    cos = cos + token[0, 0]
    winp = _widen_in(wint, 0)
    binp = _widen_in(b_in, 1)
    g_final2 = g_final.reshape(1, D_MODEL)
    sink8 = sinks.reshape(N_HEADS)

    bias_a, _ = _band_tables(None, 1, BLK - 1)
    tabs = [_band_tables(rel_table, dil, window // dil) for window, dil in BRANCHES]

    h1, qa, ka, va, *qkv_b = _norm_proj(x, g_attn, winp, binp, cos, sin)
    qbs, kbs, vbs = qkv_b[0:3], qkv_b[3:6], qkv_b[6:9]
    oa, lse_a = _attn_fwd(qa, ka, va, bias_a, sink8, dil=1, kv_pairs=2, use_sink=True, name="attn_a_fwd")
    outs = [_attn_fwd(qbs[n], kbs[n], vbs[n], tabs[n][0], sink8, dil=dil, kv_pairs=4, use_sink=False,
                      name=f"attn_b{n}_fwd") for n, (_, dil) in enumerate(BRANCHES)]
    wo = wo_fn(outs[2][1])
    x2, mixed, h2, *ob_lse = _merge_wo(x, oa, outs[0][0], outs[1][0], outs[2][0], outs[0][1], outs[1][1], outs[2][1],
                                       g_out_a, g_out_b, wo, g_ffn)
    obs, lses = ob_lse[0:3], ob_lse[3:6]
    wgt, wut, wd = ffn_fn(h2)
    gate, up, act = _ffn_up(h2, wgt, wut)
    dx3, dx3b, loss, dg_final = _ffn_down_loss(act, wd, x2, tgt, g_final2)

    dgate, dup, dx2, dx2b, dg_ffn = _ffn_bwd(dx3, dx3b, gate, up, wd, wgt, wut, x2, g_ffn)
    dwd = _matmul_tn(act, dx3b, tk=1408, tn=1024, name="dw_down")
    dwgt = _matmul_tn(dgate, h2, tk=1408, tn=1024, name="dw_gate")
    dwut = _matmul_tn(dup, h2, tk=1408, tn=1024, name="dw_up")
    dwo = _matmul_tn(mixed, dx2b, tk=1024, tn=1024, name="dw_o")
    early, token2 = early_fn(dict(w_o=dwo, w_gate=dwgt, w_up=dwut, w_down=dwd))
    doa, *dobs, dg_out_a, dg_out_b = _wo_bwd(dx2b, wo, oa, obs[0], g_out_a + token2[0, 0], g_out_b)

    dqa, dka, dva, _, dsk = _attn_bwd(qa, ka, va, oa, doa, lse_a, bias_a, sink8, dil=1, kv_pairs=2, use_sink=True,
                                      name="attn_a_bwd")
    res = [_attn_bwd(qbs[n], kbs[n], vbs[n], obs[n], dobs[n], lses[n], tabs[n][0], sink8, dil=dil, kv_pairs=4,
                     use_sink=False, name=f"attn_b{n}_bwd") for n, (_, dil) in enumerate(BRANCHES)]
    dp, dbp, grad_x, dg_attn = _inproj_bwd(dqa, dka, dva, [r[0] for r in res], [r[1] for r in res],
                                           [r[2] for r in res], cos, sin, winp, x, dx2, g_attn)
    dwin = _fold_in(_matmul_tn(dp, h1, tk=1280, tn=1024, out_dtype=F32, name="dw_in"), 0)
    drel, dsink = _bias_sink_grads([r[3] for r in res], [t[1] for t in tabs], dsk)

    small = dict(
        g_attn=dg_attn, b_in=_fold_in(dbp, 1), sinks=dsink[:, ::HEAD_DIM], rel_table=drel[:, :REL_BUCKETS].T,
        g_out_a=dg_out_a, g_out_b=dg_out_b, g_ffn=dg_ffn, g_final=dg_final.reshape(D_MODEL))
    return loss[0, 0], grad_x, dwin, early, small


SMALL_NAMES = ("g_attn", "b_in", "sinks", "rel_table", "g_out_a", "g_out_b", "g_ffn", "g_final")


def _pack_small(vals):
    flat = jnp.concatenate([vals[n].reshape(-1).astype(F32) for n in SMALL_NAMES])
    return jnp.pad(flat, (0, SMALL_ROWS * 128 - flat.shape[0])).reshape(SMALL_ROWS, 128)


def _unpack_small(packed, like):
    flat = packed.reshape(-1)
    out, off = {}, 0
    for n in SMALL_NAMES:
        size = like[n].size
        out[n] = flat[off:off + size].reshape(like[n].shape)
        off += size
    return out


def kernel(x, g_attn, w_in, b_in, sinks, rel_table, g_out_a, g_out_b, w_o, g_ffn, w_gate, w_up, w_down, g_final, loss_target, m_g_attn, m_w_in, m_b_in, m_sinks, m_rel_table, m_g_out_a, m_g_out_b, m_w_o, m_g_ffn, m_w_gate, m_w_up, m_w_down, m_g_final, v_g_attn, v_w_in, v_b_in, v_sinks, v_rel_table, v_g_out_a, v_g_out_b, v_w_o, v_g_ffn, v_w_gate, v_w_up, v_w_down, v_g_final):
    rest_names = ("w_o", "w_gate", "w_up", "w_down")

    rest = [w_o[0].astype(BF16), w_gate[0].astype(BF16).T, w_up[0].astype(BF16).T, w_down[0].astype(BF16)]
    wint = _all_gather(w_in[0].astype(BF16).T, name="gather_w_in").reshape(D_IN, D_MODEL)
    wint, rest = lax.optimization_barrier((wint, rest))
    wo_state, token_o = _exchange_start(rest[:1], gather=True, name="gather_w_o_start")
    token_o, ffn_src = lax.optimization_barrier((token_o, rest[1:]))
    ffn_state, token = _exchange_start(ffn_src, gather=True, name="gather_ffn_start")
    token = token + token_o

    def whole(got):
        return [g.reshape(N_DEV * g.shape[1], D_MODEL) for g in got]

    def wo_fn(after):
        return whole(_exchange_wait(wo_state, after, gather=True, name="gather_w_o_wait"))[0]

    def ffn_fn(after):
        return whole(_exchange_wait(ffn_state, after, gather=True, name="gather_ffn_wait"))

    def early_fn(dws):
        return _exchange_start([dws[n].reshape(N_DEV, -1, D_MODEL) for n in rest_names], gather=False,
                               name="scatter_rest_start")

    loss_part, grad_x, dwint, early_state, small = _local_step(
        x[0], loss_target[0], g_attn, wint, b_in, sinks, rel_table, g_out_a, g_out_b, g_ffn, g_final, token,
        wo_fn, ffn_fn, early_fn)
    loss = lax.psum(loss_part, ("x", "y", "c"))

    parts_in = dwint.astype(BF16).reshape(N_DEV, D_IN // N_DEV, D_MODEL)
    in_state, token3 = _exchange_start([parts_in], gather=False, name="scatter_w_in_start")
    got = _exchange_wait(early_state, token3, gather=False, name="scatter_rest_wait")

    def update(n, parts, w, m, v, transposed):
        if transposed:
            return [a.T[None] for a in _adamw(parts, w[0].T, m[0].T, v[0].T, name="adamw_" + n)]
        return [a[None] for a in _adamw(parts, w[0], m[0], v[0], name="adamw_" + n)]

    big = dict(w_o=update("w_o", got[0], w_o, m_w_o, v_w_o, False),
               w_gate=update("w_gate", got[1], w_gate, m_w_gate, v_w_gate, True),
               w_up=update("w_up", got[2], w_up, m_w_up, v_w_up, True),
               w_down=update("w_down", got[3], w_down, m_w_down, v_w_down, False))

    ws = dict(g_attn=g_attn, b_in=b_in, sinks=sinks, rel_table=rel_table, g_out_a=g_out_a, g_out_b=g_out_b,
              g_ffn=g_ffn, g_final=g_final)
    ms = dict(g_attn=m_g_attn, b_in=m_b_in, sinks=m_sinks, rel_table=m_rel_table, g_out_a=m_g_out_a,
              g_out_b=m_g_out_b, g_ffn=m_g_ffn, g_final=m_g_final)
    vs = dict(g_attn=v_g_attn, b_in=v_b_in, sinks=v_sinks, rel_table=v_rel_table, g_out_a=v_g_out_a,
              g_out_b=v_g_out_b, g_ffn=v_g_ffn, g_final=v_g_final)
    sparts = _all_gather(_pack_small(small), name="gather_small")
    sm_packed = _adamw_small(sparts, _pack_small(ws), _pack_small(ms), _pack_small(vs))
    sm = [_unpack_small(a, ws) for a in sm_packed]

    done = sm_packed[1][:1, :1] + sum(big[n][1][0, :1, :1] for n in rest_names)
    got_in = _exchange_wait(in_state, done, gather=False, name="scatter_w_in_wait")[0]
    big["w_in"] = update("w_in", got_in, w_in, m_w_in, v_w_in, True)

    order = ("g_attn", "w_in", "b_in", "sinks", "rel_table", "g_out_a", "g_out_b", "w_o", "g_ffn", "w_gate", "w_up",
             "w_down", "g_final")
    outs = [loss, grad_x[None]]
    for k in range(4):
        outs += [big[n][k] if n in big else sm[k][n] for n in order]
    return tuple(outs)
```

```python
import functools
import math

import jax
import jax.numpy as jnp
from jax import lax
from jax.experimental import pallas as pl
from jax.experimental.pallas import tpu as pltpu

F32 = jnp.float32
BF16 = jnp.bfloat16

N_DEV = 8
D_MODEL = 1024
HEAD_DIM = 64
N_HEADS = 8
PAIR = 2 * HEAD_DIM
WIDTH = N_HEADS * HEAD_DIM
D_IN = 2304
D_INP = 2560
D_FF = 2816
BLK = 128
ROPE_THETA = 150000.0
REL_BUCKETS = 32
REL_MAX_DISTANCE = 2048
EPS = 1e-5
NEG = -1e30
BRANCHES = ((128, 1), (512, 4), (2048, 16))
Q_SCALE = HEAD_DIM ** -0.5

ADAM_LR = 0.001
ADAM_B1 = 0.9
ADAM_B2 = 0.999
ADAM_EPS = 1e-08
ADAM_WD = 0.01
ADAM_STEP = 10

VMEM_LIMIT = 56 * 1024 * 1024
MESH = pl.DeviceIdType.MESH

NT = (((1,), (1,)), ((), ()))
TN = (((0,), (0,)), ((), ()))

SMALL_ROWS = 56


def _params(sem=None):
    return pltpu.CompilerParams(dimension_semantics=sem, vmem_limit_bytes=VMEM_LIMIT)


def _sigmoid(x):
    return 1.0 / (1.0 + jnp.exp2(x * (-1.0 / math.log(2.0))))


def _rms_bwd(dh, xh, r, g):
    u = dh * g
    return r * (u - xh * jnp.mean(u * xh, axis=-1, keepdims=True))


def _rope_rot(t, first):
    return jnp.where(first, pltpu.roll(t, 96, 1), pltpu.roll(t, 32, 1))


N_CHUNK = WIDTH // PAIR


def _scr(tm):
    return pltpu.VMEM((N_CHUNK, tm, PAIR), F32)


def _scr_get(scr):
    return jnp.concatenate([scr[j] for j in range(N_CHUNK)], axis=1)


def _scr_put(scr, val):
    for j in range(N_CHUNK):
        scr[j] = val[:, j * PAIR:(j + 1) * PAIR]


def _unstride(view_ref, scr, dil, tm):
    n = tm // dil
    chunks = scr.shape[0]
    for r in range(dil):
        for j in range(chunks):
            col = (r * chunks + j) * PAIR
            scr.at[j][pl.ds(r, n, stride=dil), :] = view_ref[:, col:col + PAIR].astype(F32)


def _restride(scr, out_ref, dil, tm):
    n = tm // dil
    chunks = scr.shape[0]
    for r in range(dil):
        for j in range(chunks):
            col = (r * chunks + j) * PAIR
            rows = scr[j] if dil == 1 else scr.at[j][pl.ds(r, n, stride=dil), :]
            out_ref[:, col:col + PAIR] = rows.astype(out_ref.dtype)


def _view_specs(tm, width=WIDTH):
    return [pl.BlockSpec((tm // dil, dil * width), lambda i: (i, 0)) for _, dil in BRANCHES]


def _view_shapes(T, dtype, width=WIDTH):
    return [jax.ShapeDtypeStruct((T // dil, dil * width), dtype) for _, dil in BRANCHES]


def _norm_proj(x, g, w, b, cos, sin, *, tm=512):
    T = x.shape[0]

    def body(x_ref, g_ref, w_ref, b_ref, cos_ref, sin_ref, h_ref, qa_ref, ka_ref, va_ref, *rest):
        outs_b, ys = rest[:9], rest[9]
        xv = x_ref[...]
        r = lax.rsqrt(jnp.mean(xv * xv, axis=-1, keepdims=True) + EPS)
        h = (xv * r * g_ref[...]).astype(BF16)
        h_ref[...] = h
        cosv = cos_ref[...]
        sinv = sin_ref[...]
        lane = lax.broadcasted_iota(jnp.int32, (tm, PAIR), 1)
        first = (lane % HEAD_DIM) < (HEAD_DIM // 2)

        def proj(off):
            return (lax.dot_general(h, w_ref[off:off + 256, :], NT, preferred_element_type=F32)
                    + b_ref[:, off:off + 256])

        for (off, width, rot, scale), o_ref in zip(((0, 512, True, Q_SCALE), (512, 256, True, 1.0), (768, 256, False, 1.0)),
                                                   (qa_ref, ka_ref, va_ref)):
            for c in range(0, width, 256):
                y = proj(off + c)
                for j in range(0, 256, PAIR):
                    t = y[:, j:j + PAIR]
                    if rot:
                        t = t * cosv + _rope_rot(t, first) * sinv
                    if scale != 1.0:
                        t = t * scale
                    o_ref[:, c + j:c + j + PAIR] = t.astype(BF16)
        for n, (off, scale) in enumerate(((1024, Q_SCALE), (1536, 1.0), (2048, 1.0))):
            for c in range(0, WIDTH, 256):
                y = proj(off + c)
                y = y * scale if scale != 1.0 else y
                for j in range(0, 256, PAIR):
                    ys[(c + j) // PAIR] = y[:, j:j + PAIR]
            for (_, dil), o_ref in zip(BRANCHES, outs_b[3 * n:3 * n + 3]):
                _restride(ys, o_ref, dil, tm)

    row = lambda w_: pl.BlockSpec((tm, w_), lambda i: (i, 0))
    full = lambda a: pl.BlockSpec(a.shape, lambda i: (0, 0))
    return pl.pallas_call(
        body, name="norm_proj", grid=(T // tm,),
        in_specs=[row(D_MODEL), full(g), full(w), full(b), row(PAIR), row(PAIR)],
        out_specs=[row(D_MODEL), row(512), row(256), row(256)] + _view_specs(tm) * 3,
        out_shape=[jax.ShapeDtypeStruct((T, n), BF16) for n in (D_MODEL, 512, 256, 256)] + _view_shapes(T, BF16) * 3,
        scratch_shapes=[_scr(tm)],
        compiler_params=_params(("arbitrary",)),
    )(x, g, w, b, cos, sin)


SUB = 4
AHEAD = 2


def _attn_specs(kvw):
    q_spec = pl.BlockSpec((SUB * BLK, WIDTH), lambda r, i: (i, r))
    kc_spec = pl.BlockSpec((SUB * BLK, kvw), lambda r, i: (i, r))
    kp_spec = pl.BlockSpec((BLK, kvw), lambda r, i: (jnp.maximum(SUB * i - 1, 0), r))
    b_spec = pl.BlockSpec((2, N_HEADS, BLK, 2 * BLK), lambda r, i: (0, 0, 0, 0))
    return q_spec, kp_spec, kc_spec, b_spec


def _window(prev_ref, cur_ref, j, ksl):
    before = prev_ref[:, ksl] if j == 0 else cur_ref[(j - 1) * BLK:j * BLK, ksl]
    return jnp.concatenate([before, cur_ref[j * BLK:(j + 1) * BLK, ksl]], axis=0)


def _attn_fwd(q, k, v, bias, sinks, *, dil, kv_pairs, use_sink, name):
    L = q.shape[0]
    ns = L // (SUB * BLK)
    kvw = kv_pairs * PAIR
    rep = 4 // kv_pairs

    def body(sink_ref, q_ref, kp_ref, kc_ref, vp_ref, vc_ref, b_ref, o_ref, lse_ref):
        lane = lax.broadcasted_iota(jnp.int32, (1, PAIR), 1)
        lo = lane < HEAD_DIM
        first = jnp.where(pl.program_id(1) == 0, 1, 0)
        def scores(j, hp):
            rows = slice(j * BLK, (j + 1) * BLK)
            sl = slice(hp * PAIR, (hp + 1) * PAIR)
            ksl = slice((hp // rep) * PAIR, (hp // rep + 1) * PAIR)
            qp = q_ref[rows, sl]
            kk = _window(kp_ref, kc_ref, j, ksl)
            vv = _window(vp_ref, vc_ref, j, ksl)
            heads = []
            for e in range(2):
                h = 2 * hp + e
                msk = lo if e == 0 else jnp.logical_not(lo)
                qm = jnp.where(msk, qp, jnp.zeros_like(qp))
                s = lax.dot_general(qm, kk, NT, preferred_element_type=F32) + (b_ref[first, h] if j == 0 else b_ref[0, h])
                heads.append((h, msk, s))
            return rows, sl, vv, heads

        lse_tiles = {}

        def outputs(rows, sl, vv, heads):
            o_pair = None
            for h, msk, s in heads:
                m = jnp.max(s, axis=-1, keepdims=True)
                if use_sink:
                    sk = sink_ref[h]
                    m = jnp.maximum(m, sk)
                p = jnp.exp(s - m)
                l = jnp.sum(p, axis=-1, keepdims=True)
                if use_sink:
                    l = l + jnp.exp(sk - m)
                vm = jnp.where(msk, vv, jnp.zeros_like(vv))
                oe = jnp.dot(p.astype(BF16), vm, preferred_element_type=F32) * (1.0 / l)
                ls = m + jnp.log(l)
                o_pair = oe if o_pair is None else o_pair + oe
                tile = lse_tiles.get(rows.start)
                lse_tiles[rows.start] = (jnp.broadcast_to(ls, (BLK, PAIR)) if tile is None
                                         else jnp.where(lane == h, ls, tile))
            o_ref[rows, sl] = o_pair.astype(BF16)
            if heads[-1][0] == N_HEADS - 1:
                lse_ref[rows, :] = lse_tiles.pop(rows.start)

        items = [(j, hp) for j in range(SUB) for hp in range(4)]
        queue = [scores(*it) for it in items[:AHEAD]]
        for n in range(len(items)):
            if n + AHEAD < len(items):
                queue.append(scores(*items[n + AHEAD]))
            outputs(*queue.pop(0))

    q_spec, kp_spec, kc_spec, b_spec = _attn_specs(kvw)
    return pl.pallas_call(
        body, name=name, grid=(dil, ns),
        in_specs=[pl.BlockSpec(memory_space=pltpu.SMEM), q_spec, kp_spec, kc_spec, kp_spec, kc_spec, b_spec],
        out_specs=[q_spec, pl.BlockSpec((SUB * BLK, PAIR), lambda r, i: (i, r))],
        out_shape=[jax.ShapeDtypeStruct((L, dil * WIDTH), BF16), jax.ShapeDtypeStruct((L, dil * PAIR), F32)],
        compiler_params=_params(("arbitrary", "arbitrary")),
    )(sinks, q, k, k, v, v, bias)


def _attn_bwd(q, k, v, o, do, lse, bias, sinks, *, dil, kv_pairs, use_sink, name):
    L = q.shape[0]
    ns = L // (SUB * BLK)
    n_steps = dil * ns
    kvw = kv_pairs * PAIR
    rep = 4 // kv_pairs
    last = slice((SUB - 1) * BLK, SUB * BLK)

    def body(sink_ref, q_ref, kp_ref, kc_ref, vp_ref, vc_ref, o_ref, do_ref, lse_ref, b_ref,
             dq_ref, dk_ref, dv_ref, dsum_ref, dsk_ref, pk_ref, pv_ref):
        t = pl.program_id(0)
        i = t % ns

        @pl.when(t == 0)
        def _():
            dsum_ref[...] = jnp.zeros_like(dsum_ref)
            dsk_ref[...] = jnp.zeros_like(dsk_ref)
            pk_ref[...] = jnp.zeros_like(pk_ref)
            pv_ref[...] = jnp.zeros_like(pv_ref)

        @pl.when(t < n_steps)
        def _():
            lo = lax.broadcasted_iota(jnp.int32, (1, PAIR), 1) < HEAD_DIM
            first = jnp.where(i == 0, 1, 0)
            dks = [[None] * kv_pairs for _ in range(SUB)]
            dvs = [[None] * kv_pairs for _ in range(SUB)]
            def scores(j, hp):
                rows = slice(j * BLK, (j + 1) * BLK)
                kvp = hp // rep
                sl = slice(hp * PAIR, (hp + 1) * PAIR)
                ksl = slice(kvp * PAIR, (kvp + 1) * PAIR)
                qp = q_ref[rows, sl]
                dop = do_ref[rows, sl]
                prod = dop.astype(F32) * o_ref[rows, sl].astype(F32)
                kk = _window(kp_ref, kc_ref, j, ksl)
                vv = _window(vp_ref, vc_ref, j, ksl)
                heads = []
                for e in range(2):
                    h = 2 * hp + e
                    msk = lo if e == 0 else jnp.logical_not(lo)
                    qm = jnp.where(msk, qp, jnp.zeros_like(qp))
                    dom = jnp.where(msk, dop, jnp.zeros_like(dop))
                    km = jnp.where(msk, kk, jnp.zeros_like(kk))
                    s = (lax.dot_general(qm, kk, NT, preferred_element_type=F32)
                         + (b_ref[first, h] if j == 0 else b_ref[0, h]))
                    dp = lax.dot_general(dom, vv, NT, preferred_element_type=F32)
                    heads.append((h, msk, qm, dom, km, s, dp))
                return j, rows, kvp, sl, prod, heads

            def grads(j, rows, kvp, sl, prod, heads):
                dq_pair = None
                c_pair = None
                qms, doms, dsbs, pbs = [], [], [], []
                for h, msk, qm, dom, km, s, dp in heads:
                    ls = lse_ref[rows, h:h + 1]
                    p = jnp.exp(s - ls)
                    delta = jnp.sum(jnp.where(msk, prod, 0.0), axis=-1, keepdims=True)
                    ds = p * (dp - delta)
                    if use_sink:
                        ce = jnp.exp(sink_ref[h] - ls) * delta
                        c_pair = jnp.broadcast_to(ce, (BLK, PAIR)) if c_pair is None else jnp.where(msk, ce, c_pair)
                    else:
                        dsum_ref[h] += ds
                    dsb = ds.astype(BF16)
                    dqe = jnp.dot(dsb, km, preferred_element_type=F32)
                    dq_pair = dqe if dq_pair is None else dq_pair + dqe
                    qms.append(qm)
                    doms.append(dom)
                    dsbs.append(dsb)
                    pbs.append(p.astype(BF16))
                dke = lax.dot_general(jnp.concatenate(dsbs, axis=0), jnp.concatenate(qms, axis=0), TN,
                                      preferred_element_type=F32)
                dve = lax.dot_general(jnp.concatenate(pbs, axis=0), jnp.concatenate(doms, axis=0), TN,
                                      preferred_element_type=F32)
                dks[j][kvp] = dke if dks[j][kvp] is None else dks[j][kvp] + dke
                dvs[j][kvp] = dve if dvs[j][kvp] is None else dvs[j][kvp] + dve
                dq_ref[rows, sl] = (dq_pair * Q_SCALE).astype(BF16)
                if use_sink:
                    dsk_ref[:, sl] += c_pair

            items = [(j, hp) for j in range(SUB) for hp in range(4)]
            ahead = AHEAD + 1 if use_sink else AHEAD
            queue = [scores(*it) for it in items[:ahead]]
            for n in range(len(items)):
                if n + ahead < len(items):
                    queue.append(scores(*items[n + ahead]))
                grads(*queue.pop(0))
            for kvp in range(kv_pairs):
                ksl = slice(kvp * PAIR, (kvp + 1) * PAIR)
                for pend_ref, out_ref, parts in ((pk_ref, dk_ref, [d[kvp] for d in dks]),
                                                 (pv_ref, dv_ref, [d[kvp] for d in dvs])):
                    if SUB > 1:
                        out_ref[:(SUB - 1) * BLK, ksl] = pend_ref[:(SUB - 1) * BLK, ksl].astype(BF16)
                    out_ref[last, ksl] = (pend_ref[last, ksl] + parts[0][:BLK]).astype(BF16)
                    for j in range(SUB):
                        own = parts[j][BLK:]
                        pend_ref[j * BLK:(j + 1) * BLK, ksl] = own + parts[j + 1][:BLK] if j + 1 < SUB else own

        @pl.when(t == n_steps)
        def _():
            dk_ref[...] = pk_ref[...].astype(BF16)
            dv_ref[...] = pv_ref[...].astype(BF16)

    def at(t):
        t = jnp.minimum(t, n_steps - 1)
        return t % ns, t // ns

    def before(t):
        return at(jnp.maximum(t - 1, 0))

    q_spec = pl.BlockSpec((SUB * BLK, WIDTH), at)
    kc_spec = pl.BlockSpec((SUB * BLK, kvw), at)
    kp_spec = pl.BlockSpec((BLK, kvw), lambda t: (jnp.maximum(SUB * at(t)[0] - 1, 0), at(t)[1]))
    b_spec = pl.BlockSpec((2, N_HEADS, BLK, 2 * BLK), lambda t: (0, 0, 0, 0))
    dkv_spec = pl.BlockSpec((SUB * BLK, kvw), before)
    return pl.pallas_call(
        body, name=name, grid=(n_steps + 1,),
        in_specs=[pl.BlockSpec(memory_space=pltpu.SMEM), q_spec, kp_spec, kc_spec, kp_spec, kc_spec,
                  q_spec, q_spec, pl.BlockSpec((SUB * BLK, PAIR), at), b_spec],
        out_specs=[q_spec, dkv_spec, dkv_spec,
                   pl.BlockSpec((N_HEADS, BLK, 2 * BLK), lambda t: (0, 0, 0)),
                   pl.BlockSpec((BLK, WIDTH), lambda t: (0, 0))],
        out_shape=[jax.ShapeDtypeStruct((L, dil * WIDTH), BF16),
                   jax.ShapeDtypeStruct((L, dil * kvw), BF16),
                   jax.ShapeDtypeStruct((L, dil * kvw), BF16),
                   jax.ShapeDtypeStruct((N_HEADS, BLK, 2 * BLK), F32),
                   jax.ShapeDtypeStruct((BLK, WIDTH), F32)],
        scratch_shapes=[pltpu.VMEM((SUB * BLK, kvw), F32), pltpu.VMEM((SUB * BLK, kvw), F32)],
        compiler_params=_params(("arbitrary",)),
    )(sinks, q, k, k, v, v, o, do, lse, bias)


def _merge_wo(x, oa, o1, o2, o3, l1, l2, l3, ga, gb, wo, gf, *, tm=512):
    T = x.shape[0]

    def body(x_ref, oa_ref, o1_ref, o2_ref, o3_ref, l1_ref, l2_ref, l3_ref, ga_ref, gb_ref, wo_ref, gf_ref,
             x2_ref, mix_ref, h2_ref, ob1_ref, ob4_ref, ob16_ref, ls1_ref, ls4_ref, ls16_ref, so2, so3, sl2, sl3):
        _unstride(o2_ref, so2, BRANCHES[1][1], tm)
        _unstride(o3_ref, so3, BRANCHES[2][1], tm)
        _unstride(l2_ref, sl2, BRANCHES[1][1], tm)
        _unstride(l3_ref, sl3, BRANCHES[2][1], tm)
        la, lb, lc = l1_ref[...], sl2[0], sl3[0]
        m = jnp.maximum(jnp.maximum(la, lb), lc)
        ea, eb, ec = jnp.exp(la - m), jnp.exp(lb - m), jnp.exp(lc - m)
        den = ea + eb + ec
        inv = 1.0 / den
        wa, wb, wc = ea * inv, eb * inv, ec * inv
        lo = lax.broadcasted_iota(jnp.int32, (1, PAIR), 1) < HEAD_DIM

        def per_lane(w, j):
            return jnp.where(lo, w[:, 2 * j:2 * j + 1], w[:, 2 * j + 1:2 * j + 2])

        ob = jnp.concatenate(
            [per_lane(wa, j) * o1_ref[:, j * PAIR:(j + 1) * PAIR].astype(F32) + per_lane(wb, j) * so2[j]
             + per_lane(wc, j) * so3[j] for j in range(N_CHUNK)], axis=1)
        _scr_put(so2, ob)
        sl2[0] = m + jnp.log(den)
        for (_, dil), o_ref, l_ref in zip(BRANCHES, (ob1_ref, ob4_ref, ob16_ref), (ls1_ref, ls4_ref, ls16_ref)):
            _restride(so2, o_ref, dil, tm)
            _restride(sl2, l_ref, dil, tm)
        oav = oa_ref[...].astype(F32)
        ra = lax.rsqrt(jnp.mean(oav * oav, axis=-1, keepdims=True) + EPS)
        rb = lax.rsqrt(jnp.mean(ob * ob, axis=-1, keepdims=True) + EPS)
        mix_ref[:, :WIDTH] = (oav * ra * ga_ref[...]).astype(BF16)
        mix_ref[:, WIDTH:] = (ob * rb * gb_ref[...]).astype(BF16)
        x2 = x_ref[...] + jnp.dot(mix_ref[...], wo_ref[...], preferred_element_type=F32)
        x2_ref[...] = x2
        r2 = lax.rsqrt(jnp.mean(x2 * x2, axis=-1, keepdims=True) + EPS)
        h2_ref[...] = (x2 * r2 * gf_ref[...]).astype(BF16)

    row = lambda w_: pl.BlockSpec((tm, w_), lambda i: (i, 0))
    full = lambda a: pl.BlockSpec(a.shape, lambda i: (0, 0))
    return pl.pallas_call(
        body, name="merge_wo", grid=(T // tm,),
        in_specs=[row(D_MODEL), row(WIDTH)] + _view_specs(tm) + _view_specs(tm, PAIR)
        + [full(ga), full(gb), full(wo), full(gf)],
        out_specs=[row(D_MODEL), row(D_MODEL), row(D_MODEL)] + _view_specs(tm) + _view_specs(tm, PAIR),
        out_shape=[jax.ShapeDtypeStruct((T, D_MODEL), F32), jax.ShapeDtypeStruct((T, D_MODEL), BF16),
                   jax.ShapeDtypeStruct((T, D_MODEL), BF16)] + _view_shapes(T, BF16) + _view_shapes(T, F32, PAIR),
        scratch_shapes=[_scr(tm), _scr(tm), pltpu.VMEM((1, tm, PAIR), F32), pltpu.VMEM((1, tm, PAIR), F32)],
        compiler_params=_params(("arbitrary",)),
    )(x, oa, o1, o2, o3, l1, l2, l3, ga, gb, wo, gf)


def _ffn_up(h2, wgt, wut, *, tm=512, fc=D_FF, rc=512, cc=256):
    T = h2.shape[0]

    def body(h_ref, wg_ref, wu_ref, gate_ref, up_ref, act_ref):
        for s in range(0, tm, rc):
            h = h_ref[s:s + rc, :]
            for c in range(0, fc, cc):
                gt = lax.dot_general(h, wg_ref[c:c + cc, :], NT, preferred_element_type=F32)
                u = lax.dot_general(h, wu_ref[c:c + cc, :], NT, preferred_element_type=F32)
                gate_ref[s:s + rc, c:c + cc] = gt.astype(BF16)
                up_ref[s:s + rc, c:c + cc] = u.astype(BF16)
                act_ref[s:s + rc, c:c + cc] = (gt * _sigmoid(gt) * u).astype(BF16)

    rowd = pl.BlockSpec((tm, D_MODEL), lambda i, c: (i, 0))
    wrow = pl.BlockSpec((fc, D_MODEL), lambda i, c: (c, 0))
    oc = pl.BlockSpec((tm, fc), lambda i, c: (i, c))
    return pl.pallas_call(
        body, name="ffn_up", grid=(T // tm, D_FF // fc),
        in_specs=[rowd, wrow, wrow],
        out_specs=[oc, oc, oc],
        out_shape=[jax.ShapeDtypeStruct((T, D_FF), BF16)] * 3,
        compiler_params=_params(("arbitrary", "arbitrary")),
    )(h2, wgt, wut)


def _ffn_down_loss(act, wd, x2, tgt, g, *, tm=512, rc=256):
    T = x2.shape[0]

    def body(act_ref, wd_ref, x2_ref, tgt_ref, g_ref, dx_ref, dxb_ref, loss_ref, dg_ref):
        @pl.when(pl.program_id(0) == 0)
        def _():
            loss_ref[...] = jnp.zeros_like(loss_ref)
            dg_ref[...] = jnp.zeros_like(dg_ref)

        gv = g_ref[...]
        lsum = jnp.zeros((1, 1), F32)
        dgs = jnp.zeros((1, D_MODEL), F32)
        for c in range(0, tm, rc):
            x3 = x2_ref[c:c + rc, :] + jnp.dot(act_ref[c:c + rc, :], wd_ref[...], preferred_element_type=F32)
            r = lax.rsqrt(jnp.mean(x3 * x3, axis=-1, keepdims=True) + EPS)
            xh = x3 * r
            diff = xh * gv - tgt_ref[c:c + rc, :]
            lsum = lsum + jnp.sum(jnp.sum(diff * diff, axis=-1, keepdims=True), axis=0, keepdims=True)
            dy = diff * (1.0 / D_MODEL)
            dgs = dgs + jnp.sum(dy * xh, axis=0, keepdims=True)
            dx = _rms_bwd(dy, xh, r, gv)
            dx_ref[c:c + rc, :] = dx
            dxb_ref[c:c + rc, :] = dx.astype(BF16)
        loss_ref[...] += lsum * (0.5 / D_MODEL)
        dg_ref[...] += dgs

    rowd = pl.BlockSpec((tm, D_MODEL), lambda i: (i, 0))
    return pl.pallas_call(
        body, name="ffn_down_loss", grid=(T // tm,),
        in_specs=[pl.BlockSpec((tm, D_FF), lambda i: (i, 0)), pl.BlockSpec((D_FF, D_MODEL), lambda i: (0, 0)),
                  rowd, rowd, pl.BlockSpec(g.shape, lambda i: (0, 0))],
        out_specs=[rowd, rowd, pl.BlockSpec((1, 1), lambda i: (0, 0)), pl.BlockSpec((1, D_MODEL), lambda i: (0, 0))],
        out_shape=[jax.ShapeDtypeStruct((T, D_MODEL), F32), jax.ShapeDtypeStruct((T, D_MODEL), BF16),
                   jax.ShapeDtypeStruct((1, 1), F32), jax.ShapeDtypeStruct((1, D_MODEL), F32)],
        compiler_params=_params(("arbitrary",)),
    )(act, wd, x2, tgt, g)


def _ffn_bwd_act(dx3b, gate, up, wd, *, tm=512, fc=D_FF, rc=256, cc=256):
    T = dx3b.shape[0]

    def body(dxb_ref, gate_ref, up_ref, wd_ref, dgate_ref, dup_ref):
        for s in range(0, tm, rc):
            for c in range(0, fc, cc):
                dact = lax.dot_general(dxb_ref[s:s + rc, :], wd_ref[c:c + cc, :], NT, preferred_element_type=F32)
                gt = gate_ref[s:s + rc, c:c + cc].astype(F32)
                u = up_ref[s:s + rc, c:c + cc].astype(F32)
                sg = _sigmoid(gt)
                a = dact * sg
                dgate_ref[s:s + rc, c:c + cc] = (a * u * ((1.0 + gt) - gt * sg)).astype(BF16)
                dup_ref[s:s + rc, c:c + cc] = (a * gt).astype(BF16)

    rowd = pl.BlockSpec((tm, D_MODEL), lambda i, c: (i, 0))
    oc = pl.BlockSpec((tm, fc), lambda i, c: (i, c))
    return pl.pallas_call(
        body, name="ffn_bwd_act", grid=(T // tm, D_FF // fc),
        in_specs=[rowd, oc, oc, pl.BlockSpec((fc, D_MODEL), lambda i, c: (c, 0))],
        out_specs=[oc, oc],
        out_shape=[jax.ShapeDtypeStruct((T, D_FF), BF16), jax.ShapeDtypeStruct((T, D_FF), BF16)],
        compiler_params=_params(("arbitrary", "arbitrary")),
    )(dx3b, gate, up, wd)


def _ffn_bwd(dx3, dx3b, gate, up, wd, wgt, wut, x2, g, *, tm=256, cc=256):
    T = x2.shape[0]

    def body(dx_ref, dxb_ref, gate_ref, up_ref, wd_ref, wg_ref, wu_ref, x2_ref, g_ref,
             dgate_ref, dup_ref, dx2_ref, dx2b_ref, dg_ref):
        @pl.when(pl.program_id(0) == 0)
        def _():
            dg_ref[...] = jnp.zeros_like(dg_ref)

        dxb = dxb_ref[...]
        for c in range(0, D_FF, cc):
            dact = lax.dot_general(dxb, wd_ref[c:c + cc, :], NT, preferred_element_type=F32)
            gt = gate_ref[:, c:c + cc].astype(F32)
            u = up_ref[:, c:c + cc].astype(F32)
            sg = _sigmoid(gt)
            a = dact * sg
            dgate_ref[:, c:c + cc] = (a * u * ((1.0 + gt) - gt * sg)).astype(BF16)
            dup_ref[:, c:c + cc] = (a * gt).astype(BF16)
        dh = (jnp.dot(dgate_ref[...], wg_ref[...], preferred_element_type=F32)
              + jnp.dot(dup_ref[...], wu_ref[...], preferred_element_type=F32))
        xv = x2_ref[...]
        r = lax.rsqrt(jnp.mean(xv * xv, axis=-1, keepdims=True) + EPS)
        xh = xv * r
        dg_ref[...] += jnp.sum(dh * xh, axis=0, keepdims=True)
        d = dx_ref[...] + _rms_bwd(dh, xh, r, g_ref[...])
        dx2_ref[...] = d
        dx2b_ref[...] = d.astype(BF16)

    rowd = pl.BlockSpec((tm, D_MODEL), lambda i: (i, 0))
    rowf = pl.BlockSpec((tm, D_FF), lambda i: (i, 0))
    wfull = pl.BlockSpec((D_FF, D_MODEL), lambda i: (0, 0), pipeline_mode=pl.Buffered(1))
    return pl.pallas_call(
        body, name="ffn_bwd", grid=(T // tm,),
        in_specs=[rowd, rowd, rowf, rowf, wfull, wfull, wfull, rowd, pl.BlockSpec(g.shape, lambda i: (0, 0))],
        out_specs=[rowf, rowf, rowd, rowd, pl.BlockSpec((1, D_MODEL), lambda i: (0, 0))],
        out_shape=[jax.ShapeDtypeStruct((T, D_FF), BF16), jax.ShapeDtypeStruct((T, D_FF), BF16),
                   jax.ShapeDtypeStruct((T, D_MODEL), F32), jax.ShapeDtypeStruct((T, D_MODEL), BF16),
                   jax.ShapeDtypeStruct((1, D_MODEL), F32)],
        compiler_params=_params(("arbitrary",)),
    )(dx3, dx3b, gate, up, wd, wgt, wut, x2, g)


def _ffn_bwd_in(dgate, dup, wgt, wut, x2, dx3, g, *, tm=512, rc=256):
    T = x2.shape[0]

    def body(dgate_ref, dup_ref, wg_ref, wu_ref, x2_ref, dx_ref, g_ref, dx2_ref, dx2b_ref, dg_ref):
        @pl.when(pl.program_id(0) == 0)
        def _():
            dg_ref[...] = jnp.zeros_like(dg_ref)

        gv = g_ref[...]
        dgs = jnp.zeros((1, D_MODEL), F32)
        for s in range(0, tm, rc):
            dh = (jnp.dot(dgate_ref[s:s + rc, :], wg_ref[...], preferred_element_type=F32)
                  + jnp.dot(dup_ref[s:s + rc, :], wu_ref[...], preferred_element_type=F32))
            xv = x2_ref[s:s + rc, :]
            r = lax.rsqrt(jnp.mean(xv * xv, axis=-1, keepdims=True) + EPS)
            xh = xv * r
            dgs = dgs + jnp.sum(dh * xh, axis=0, keepdims=True)
            d = dx_ref[s:s + rc, :] + _rms_bwd(dh, xh, r, gv)
            dx2_ref[s:s + rc, :] = d
            dx2b_ref[s:s + rc, :] = d.astype(BF16)
        dg_ref[...] += dgs

    rowd = pl.BlockSpec((tm, D_MODEL), lambda i: (i, 0))
    rowf = pl.BlockSpec((tm, D_FF), lambda i: (i, 0))
    wfull = pl.BlockSpec((D_FF, D_MODEL), lambda i: (0, 0))
    return pl.pallas_call(
        body, name="ffn_bwd_in", grid=(T // tm,),
        in_specs=[rowf, rowf, wfull, wfull, rowd, rowd, pl.BlockSpec(g.shape, lambda i: (0, 0))],
        out_specs=[rowd, rowd, pl.BlockSpec((1, D_MODEL), lambda i: (0, 0))],
        out_shape=[jax.ShapeDtypeStruct((T, D_MODEL), F32), jax.ShapeDtypeStruct((T, D_MODEL), BF16),
                   jax.ShapeDtypeStruct((1, D_MODEL), F32)],
        compiler_params=_params(("arbitrary",)),
    )(dgate, dup, wgt, wut, x2, dx3, g)


def _matmul_tn(a, b, *, tk, tn, tt=2048, out_dtype=BF16, name):
    T, K = a.shape
    N = b.shape[1]
    nt = T // tt

    def body(a_ref, b_ref, o_ref, acc_ref):
        part = lax.dot_general(a_ref[...], b_ref[...], TN, preferred_element_type=F32)

        @pl.when(pl.program_id(2) == 0)
        def _():
            acc_ref[...] = part

        @pl.when(pl.program_id(2) > 0)
        def _():
            acc_ref[...] += part

        @pl.when(pl.program_id(2) == nt - 1)
        def _():
            o_ref[...] = acc_ref[...].astype(out_dtype)

    return pl.pallas_call(
        body, name=name, grid=(K // tk, N // tn, nt),
        in_specs=[pl.BlockSpec((tt, tk), lambda i, j, t: (t, i)), pl.BlockSpec((tt, tn), lambda i, j, t: (t, j))],
        out_specs=pl.BlockSpec((tk, tn), lambda i, j, t: (i, j)),
        out_shape=jax.ShapeDtypeStruct((K, N), out_dtype),
        scratch_shapes=[pltpu.VMEM((tk, tn), F32)],
        compiler_params=_params(("arbitrary", "arbitrary", "arbitrary")),
    )(a, b)


def _wo_bwd(dx2b, wo, oa, ob, ga, gb, *, tm=512):
    T = dx2b.shape[0]

    def body(dx_ref, wo_ref, oa_ref, ob_ref, ga_ref, gb_ref, doa_ref, dob1_ref, dob4_ref, dob16_ref, dga_ref, dgb_ref, scr):
        @pl.when(pl.program_id(0) == 0)
        def _():
            dga_ref[...] = jnp.zeros_like(dga_ref)
            dgb_ref[...] = jnp.zeros_like(dgb_ref)

        dm = lax.dot_general(dx_ref[...], wo_ref[...], NT, preferred_element_type=F32)
        for o_ref, g_ref, dg_ref, sl in ((oa_ref, ga_ref, dga_ref, slice(0, WIDTH)),
                                         (ob_ref, gb_ref, dgb_ref, slice(WIDTH, 2 * WIDTH))):
            ov = o_ref[...].astype(F32)
            r = lax.rsqrt(jnp.mean(ov * ov, axis=-1, keepdims=True) + EPS)
            xh = ov * r
            d = dm[:, sl]
            dg_ref[...] += jnp.sum(d * xh, axis=0, keepdims=True)
            do = _rms_bwd(d, xh, r, g_ref[...])
            if o_ref is oa_ref:
                doa_ref[...] = do.astype(BF16)
            else:
                _scr_put(scr, do)
                for (_, dil), v_ref in zip(BRANCHES, (dob1_ref, dob4_ref, dob16_ref)):
                    _restride(scr, v_ref, dil, tm)

    row = lambda w_: pl.BlockSpec((tm, w_), lambda i: (i, 0))
    full = lambda a: pl.BlockSpec(a.shape, lambda i: (0, 0))
    return pl.pallas_call(
        body, name="wo_bwd", grid=(T // tm,),
        in_specs=[row(D_MODEL), full(wo), row(WIDTH), row(WIDTH), full(ga), full(gb)],
        out_specs=[row(WIDTH)] + _view_specs(tm)
        + [pl.BlockSpec((1, WIDTH), lambda i: (0, 0)), pl.BlockSpec((1, WIDTH), lambda i: (0, 0))],
        out_shape=[jax.ShapeDtypeStruct((T, WIDTH), BF16)] + _view_shapes(T, BF16)
        + [jax.ShapeDtypeStruct((1, WIDTH), F32), jax.ShapeDtypeStruct((1, WIDTH), F32)],
        scratch_shapes=[_scr(tm)],
        compiler_params=_params(("arbitrary",)),
    )(dx2b, wo, oa, ob, ga, gb)


def _inproj_bwd(dqa, dka, dva, dqs, dks, dvs, cos, sin, w, x, dx2, g, *, tm=512):
    T = dqa.shape[0]

    def body(dqa_ref, dka_ref, dva_ref, q1, q2, q3, k1, k2, k3, v1, v2, v3, cos_ref, sin_ref, w_ref, x_ref, dx2_ref,
             g_ref, dp_ref, db_ref, gx_ref, dg_ref, acc, tmp):
        @pl.when(pl.program_id(0) == 0)
        def _():
            db_ref[...] = jnp.zeros_like(db_ref)
            dg_ref[...] = jnp.zeros_like(dg_ref)

        cosv = cos_ref[...]
        sinv = sin_ref[...]
        lane = lax.broadcasted_iota(jnp.int32, (tm, PAIR), 1)
        first = (lane % HEAD_DIM) < (HEAD_DIM // 2)

        def put(off, val):
            dp_ref[:, off:off + PAIR] = val.astype(BF16)
            db_ref[:, off:off + PAIR] += jnp.sum(val, axis=0, keepdims=True)

        for src, off, width in ((dqa_ref, 0, 512), (dka_ref, 512, 256)):
            for j in range(0, width, PAIR):
                d = src[:, j:j + PAIR].astype(F32)
                put(off + j, d * cosv - _rope_rot(d, first) * sinv)
        for j in range(0, 256, PAIR):
            put(768 + j, dva_ref[:, j:j + PAIR].astype(F32))
        for (a, b, c), off in (((q1, q2, q3), 1024), ((k1, k2, k3), 1536), ((v1, v2, v3), 2048)):
            _unstride(b, acc, BRANCHES[1][1], tm)
            _unstride(c, tmp, BRANCHES[2][1], tm)
            for j in range(N_CHUNK):
                put(off + j * PAIR, a[:, j * PAIR:(j + 1) * PAIR].astype(F32) + acc[j] + tmp[j])

        dh = jnp.dot(dp_ref[...], w_ref[...], preferred_element_type=F32)
        xv = x_ref[...]
        r = lax.rsqrt(jnp.mean(xv * xv, axis=-1, keepdims=True) + EPS)
        xh = xv * r
        dg_ref[...] += jnp.sum(dh * xh, axis=0, keepdims=True)
        gx_ref[...] = dx2_ref[...] + _rms_bwd(dh, xh, r, g_ref[...])

    row = lambda w_: pl.BlockSpec((tm, w_), lambda i: (i, 0))
    full = lambda a: pl.BlockSpec(a.shape, lambda i: (0, 0))
    return pl.pallas_call(
        body, name="inproj_bwd", grid=(T // tm,),
        in_specs=[row(512), row(256), row(256)] + _view_specs(tm) * 3 + [row(PAIR), row(PAIR)]
        + [full(w), row(D_MODEL), row(D_MODEL), full(g)],
        out_specs=[row(D_INP), pl.BlockSpec((1, D_INP), lambda i: (0, 0)), row(D_MODEL),
                   pl.BlockSpec((1, D_MODEL), lambda i: (0, 0))],
        out_shape=[jax.ShapeDtypeStruct((T, D_INP), BF16), jax.ShapeDtypeStruct((1, D_INP), F32),
                   jax.ShapeDtypeStruct((T, D_MODEL), F32), jax.ShapeDtypeStruct((1, D_MODEL), F32)],
        scratch_shapes=[_scr(tm)] * 2,
        compiler_params=_params(("arbitrary",)),
    )(dqa, dka, dva, *dqs, *dks, *dvs, cos, sin, w, x, dx2, g)


def _bias_sink_grads(dsums, bmaps, dsk):
    def body(s1, s2, s3, m1, m2, m3, dsk_ref, drel_ref, dsink_ref):
        row = lax.broadcasted_iota(jnp.int32, (N_HEADS, 128), 0)
        lane = lax.broadcasted_iota(jnp.int32, (N_HEADS, 128), 1)
        out = jnp.zeros((N_HEADS, 128), F32)
        for s_ref, m_ref in ((s1, m1), (s2, m2), (s3, m3)):
            bm = m_ref[...]
            for h in range(N_HEADS):
                a = s_ref[h]
                for b in range(REL_BUCKETS):
                    v = jnp.sum(jnp.sum(jnp.where(bm == b, a, 0.0), axis=-1, keepdims=True), axis=0, keepdims=True)
                    out = out + jnp.where((row == h) & (lane == b), v, 0.0)
        drel_ref[...] = out
        dsink_ref[...] = -jnp.sum(dsk_ref[...], axis=0, keepdims=True)

    vm = pl.BlockSpec(memory_space=pltpu.VMEM)
    return pl.pallas_call(
        body, name="bias_sink_grads",
        in_specs=[vm] * 7, out_specs=[vm, vm],
        out_shape=[jax.ShapeDtypeStruct((N_HEADS, 128), F32), jax.ShapeDtypeStruct((1, WIDTH), F32)],
        compiler_params=_params(),
    )(*dsums, *bmaps, dsk)


def _all_gather(blk, *, name):
    R, C = blk.shape

    def body(x_ref, out_ref, send_sems, recv_sems, local_sem):
        x, y, c = lax.axis_index("x"), lax.axis_index("y"), lax.axis_index("c")
        me, sibling = (x, y, c), (x, y, 1 - c)
        chips = [(1 - x, y), (x, 1 - y), (1 - x, 1 - y)]

        def slot(px, py, pc):
            return out_ref.at[4 * px + 2 * py + pc]

        def copy(k, block, to, src=None):
            return pltpu.make_async_remote_copy(
                src_ref=slot(*block) if src is None else src, dst_ref=slot(*block),
                send_sem=send_sems.at[k], recv_sem=recv_sems.at[k], device_id=to, device_id_type=MESH)

        mine = pltpu.make_async_copy(x_ref, slot(*me), local_sem)
        mine.start()
        first = [copy(0, me, sibling, src=x_ref)]
        first += [copy(1 + j, me, (*chip, c), src=x_ref) for j, chip in enumerate(chips)]
        for cp in first:
            cp.start()
        passed = [copy(4 + j, (*chip, c), sibling) for j, chip in enumerate(chips)]
        for j, chip in enumerate(chips):
            copy(1 + j, (*chip, c), me).wait_recv()
            passed[j].start()
        copy(0, sibling, me).wait_recv()
        for j, chip in enumerate(chips):
            copy(4 + j, (*chip, 1 - c), me).wait_recv()
        for cp in first + passed:
            cp.wait_send()
        mine.wait()

    return pl.pallas_call(
        body, name=name,
        in_specs=[pl.BlockSpec(memory_space=pl.ANY)], out_specs=pl.BlockSpec(memory_space=pl.ANY),
        out_shape=jax.ShapeDtypeStruct((N_DEV, R, C), blk.dtype),
        scratch_shapes=[pltpu.SemaphoreType.DMA((7,)), pltpu.SemaphoreType.DMA((7,)), pltpu.SemaphoreType.DMA],
        compiler_params=pltpu.CompilerParams(has_side_effects=True),
    )(blk)


def _peers(x, y, c):
    return [(x ^ (k >> 2), y ^ ((k >> 1) & 1), c ^ (k & 1)) for k in range(1, N_DEV)]


_HBM = pl.BlockSpec(memory_space=pltpu.HBM)
_SEM = pl.BlockSpec(memory_space=pltpu.SEMAPHORE)
_EFFECT = pltpu.SideEffectType.DATAFLOW_SIDE_EFFECTING


def _exchange_start(srcs, *, gather, name):
    n = len(srcs)
    lands = [lax.empty((N_DEV,) + s.shape[-2:], s.dtype) for s in srcs]

    def body(*refs):
        src_refs, land_refs = refs[:n], refs[n:2 * n]
        send_sems, recv_sems = refs[2 * n], refs[2 * n + 1]
        token = refs[-1]
        x, y, c = lax.axis_index("x"), lax.axis_index("y"), lax.axis_index("c")
        mine = 4 * x + 2 * y + c
        for a in range(n):
            for k, peer in enumerate(_peers(x, y, c)):
                dest = 4 * peer[0] + 2 * peer[1] + peer[2]
                j = a * (N_DEV - 1) + k
                pltpu.make_async_remote_copy(
                    src_ref=src_refs[a] if gather else src_refs[a].at[dest], dst_ref=land_refs[a].at[mine],
                    send_sem=send_sems.at[j], recv_sem=recv_sems.at[j], device_id=peer, device_id_type=MESH).start()
        token[...] = jnp.zeros_like(token)

    sems = pltpu.SemaphoreType.DMA((n * (N_DEV - 1),))
    out = pl.pallas_call(
        body, name=name,
        out_shape=(sems, sems) + tuple(pltpu.HBM(a.shape, a.dtype) for a in list(srcs) + lands)
        + (jax.ShapeDtypeStruct((8, 128), F32),),
        in_specs=(_HBM,) * (2 * n), out_specs=(_SEM, _SEM) + (_HBM,) * (2 * n) + (pl.BlockSpec(memory_space=pltpu.VMEM),),
        input_output_aliases={i: 2 + i for i in range(2 * n)},
        compiler_params=pltpu.CompilerParams(has_side_effects=_EFFECT),
    )(*[pltpu.with_memory_space_constraint(a, pltpu.HBM) for a in list(srcs) + lands])
    return out[:-1], out[-1]


def _exchange_wait(state, after, *, gather, name):
    send_sems, recv_sems = state[0], state[1]
    n = (len(state) - 2) // 2
    arrays = state[2:]

    def body(*refs):
        src_refs, land_refs = refs[:n], refs[n:2 * n]
        send_sems, recv_sems = refs[2 * n], refs[2 * n + 1]
        x, y, c = lax.axis_index("x"), lax.axis_index("y"), lax.axis_index("c")
        for a in range(n):
            for k, peer in enumerate(_peers(x, y, c)):
                other = 4 * peer[0] + 2 * peer[1] + peer[2]
                j = a * (N_DEV - 1) + k
                copy = pltpu.make_async_remote_copy(
                    src_ref=src_refs[a] if gather else src_refs[a].at[other], dst_ref=land_refs[a].at[other],
                    send_sem=send_sems.at[j], recv_sem=recv_sems.at[j], device_id=peer, device_id_type=MESH)
                copy.wait_send()
                copy.wait_recv()

    out = pl.pallas_call(
        body, name=name,
        out_shape=tuple(pltpu.HBM(a.shape, a.dtype) for a in arrays),
        in_specs=(_HBM,) * (2 * n) + (_SEM, _SEM, pl.BlockSpec(memory_space=pl.ANY)), out_specs=(_HBM,) * (2 * n),
        input_output_aliases={i: i for i in range(2 * n)},
        compiler_params=pltpu.CompilerParams(has_side_effects=_EFFECT),
    )(*arrays, send_sems, recv_sems, after)
    mine = 4 * lax.axis_index("x") + 2 * lax.axis_index("y") + lax.axis_index("c")
    own = out[:n] if gather else [lax.dynamic_index_in_dim(s, mine, 0, keepdims=False) for s in out[:n]]
    return [lax.dynamic_update_slice(g, o[None], (mine, 0, 0)) for g, o in zip(out[n:], own)]


def _adam_math(w, g, m, v):
    m = ADAM_B1 * m + (1.0 - ADAM_B1) * g
    v = ADAM_B2 * v + (1.0 - ADAM_B2) * (g * g)
    m_hat = m / (1.0 - ADAM_B1 ** ADAM_STEP)
    v_hat = v / (1.0 - ADAM_B2 ** ADAM_STEP)
    delta = -ADAM_LR * (m_hat / (jnp.sqrt(v_hat) + ADAM_EPS) + ADAM_WD * w)
    return delta, m, v


def _adamw(parts, w, m, v, *, name):
    R, C = w.shape
    n_parts = parts.shape[0]
    tr = R // 2
    assert tr % 16 == 0

    def body(p_ref, w_ref, m_ref, v_ref, g_ref, d_ref, nm_ref, nv_ref):
        g = p_ref[0].astype(F32)
        for s in range(1, n_parts):
            g = g + p_ref[s].astype(F32)
        d, nm, nv = _adam_math(w_ref[...], g, m_ref[...], v_ref[...])
        g_ref[...] = g
        d_ref[...] = d
        nm_ref[...] = nm
        nv_ref[...] = nv

    blk = pl.BlockSpec((tr, C), lambda i: (i, 0))
    return pl.pallas_call(
        body, name=name, grid=(R // tr,),
        in_specs=[pl.BlockSpec((n_parts, tr, C), lambda i: (0, i, 0)), blk, blk, blk],
        out_specs=[blk] * 4, out_shape=[jax.ShapeDtypeStruct((R, C), F32)] * 4,
        compiler_params=_params(("arbitrary",)),
    )(parts, w, m, v)


def _adamw_small(parts, w, m, v):
    def body(p_ref, w_ref, m_ref, v_ref, g_ref, d_ref, nm_ref, nv_ref):
        g = p_ref[0]
        for s in range(1, N_DEV):
            g = g + p_ref[s]
        d, nm, nv = _adam_math(w_ref[...], g, m_ref[...], v_ref[...])
        g_ref[...] = g
        d_ref[...] = d
        nm_ref[...] = nm
        nv_ref[...] = nv

    vm = pl.BlockSpec(memory_space=pltpu.VMEM)
    return pl.pallas_call(
        body, name="adamw_small", in_specs=[vm] * 4, out_specs=[vm] * 4,
        out_shape=[jax.ShapeDtypeStruct((SMALL_ROWS, 128), F32)] * 4, compiler_params=_params(),
    )(parts, w, m, v)


def _t5_bucket(dist):
    max_exact = REL_BUCKETS // 2
    df = jnp.maximum(dist, 1).astype(F32)
    large = max_exact + (jnp.log(df / max_exact) / math.log(REL_MAX_DISTANCE / max_exact)
                         * (REL_BUCKETS - max_exact)).astype(jnp.int32)
    large = jnp.minimum(large, REL_BUCKETS - 1)
    return jnp.where(dist < max_exact, dist, large)


def _band_tables(rel_table, dil, n_back):
    qi = jnp.arange(BLK)[:, None]
    kj = jnp.arange(2 * BLK)[None, :]
    delta = BLK + qi - kj
    in_band = (delta >= 0) & (delta <= n_back)
    if rel_table is None:
        vals = jnp.zeros((N_HEADS, BLK, 2 * BLK), F32)
        bmap = None
    else:
        bucket = _t5_bucket(jnp.clip(delta, 0, n_back) * dil)
        vals = jnp.zeros((N_HEADS, BLK, 2 * BLK), F32)
        for b in range(REL_BUCKETS):
            vals = jnp.where((bucket == b)[None], rel_table[b][:, None, None], vals)
        bmap = jnp.where(in_band, bucket, -1).astype(jnp.int32)
    later = jnp.where(in_band[None], vals, NEG)
    first = jnp.where((in_band & (kj >= BLK))[None], vals, NEG)
    return jnp.stack([later, first]), bmap


def _rope_tables(T):
    half = HEAD_DIM // 2
    inv_freq = ROPE_THETA ** (-jnp.arange(half, dtype=F32) / half)
    ang = jnp.arange(T, dtype=F32)[:, None] * inv_freq[None, :]
    cos, sin = jnp.cos(ang), jnp.sin(ang)
    return jnp.tile(cos, (1, 4)), jnp.tile(jnp.concatenate([-sin, sin], axis=1), (1, 2))


def _widen_in(a, axis):
    sl = lambda lo, hi: lax.slice_in_dim(a, lo, hi, axis=axis)
    dup = lambda lo: [sl(lo, lo + 64), sl(lo, lo + 64), sl(lo + 64, lo + 128), sl(lo + 64, lo + 128)]
    return jnp.concatenate([sl(0, 512)] + dup(512) + dup(640) + [sl(768, D_IN)], axis=axis)


def _fold_in(a, axis):
    sl = lambda lo, hi: lax.slice_in_dim(a, lo, hi, axis=axis)
    fold = lambda lo: [sl(lo, lo + 64) + sl(lo + 64, lo + 128), sl(lo + 128, lo + 192) + sl(lo + 192, lo + 256)]
    return jnp.concatenate([sl(0, 512)] + fold(512) + fold(768) + [sl(1024, D_INP)], axis=axis)


def _local_step(x, tgt, g_attn, wint, b_in, sinks, rel_table, g_out_a, g_out_b, g_ffn, g_final, token,
                wo_fn, ffn_fn, early_fn):
    T = x.shape[0]
    cos, sin = _rope_tables(T)
    cos = cos + token[0, 0]
    winp = _widen_in(wint, 0)
    binp = _widen_in(b_in, 1)
    g_final2 = g_final.reshape(1, D_MODEL)
    sink8 = sinks.reshape(N_HEADS)

    bias_a, _ = _band_tables(None, 1, BLK - 1)
    tabs = [_band_tables(rel_table, dil, window // dil) for window, dil in BRANCHES]

    h1, qa, ka, va, *qkv_b = _norm_proj(x, g_attn, winp, binp, cos, sin)
    qbs, kbs, vbs = qkv_b[0:3], qkv_b[3:6], qkv_b[6:9]
    oa, lse_a = _attn_fwd(qa, ka, va, bias_a, sink8, dil=1, kv_pairs=2, use_sink=True, name="attn_a_fwd")
    outs = [_attn_fwd(qbs[n], kbs[n], vbs[n], tabs[n][0], sink8, dil=dil, kv_pairs=4, use_sink=False,
                      name=f"attn_b{n}_fwd") for n, (_, dil) in enumerate(BRANCHES)]
    wo = wo_fn(outs[2][1])
    x2, mixed, h2, *ob_lse = _merge_wo(x, oa, outs[0][0], outs[1][0], outs[2][0], outs[0][1], outs[1][1], outs[2][1],
                                       g_out_a, g_out_b, wo, g_ffn)
    obs, lses = ob_lse[0:3], ob_lse[3:6]
    wgt, wut, wd = ffn_fn(h2)
    gate, up, act = _ffn_up(h2, wgt, wut)
    dx3, dx3b, loss, dg_final = _ffn_down_loss(act, wd, x2, tgt, g_final2)

    dgate, dup, dx2, dx2b, dg_ffn = _ffn_bwd(dx3, dx3b, gate, up, wd, wgt, wut, x2, g_ffn)
    dwd = _matmul_tn(act, dx3b, tk=1408, tn=1024, name="dw_down")
    dwgt = _matmul_tn(dgate, h2, tk=1408, tn=1024, name="dw_gate")
    dwut = _matmul_tn(dup, h2, tk=1408, tn=1024, name="dw_up")
    dwo = _matmul_tn(mixed, dx2b, tk=1024, tn=1024, name="dw_o")
    early, token2 = early_fn(dict(w_o=dwo, w_gate=dwgt, w_up=dwut, w_down=dwd))
    doa, *dobs, dg_out_a, dg_out_b = _wo_bwd(dx2b, wo, oa, obs[0], g_out_a + token2[0, 0], g_out_b)

    dqa, dka, dva, _, dsk = _attn_bwd(qa, ka, va, oa, doa, lse_a, bias_a, sink8, dil=1, kv_pairs=2, use_sink=True,
                                      name="attn_a_bwd")
    res = [_attn_bwd(qbs[n], kbs[n], vbs[n], obs[n], dobs[n], lses[n], tabs[n][0], sink8, dil=dil, kv_pairs=4,
                     use_sink=False, name=f"attn_b{n}_bwd") for n, (_, dil) in enumerate(BRANCHES)]
    dp, dbp, grad_x, dg_attn = _inproj_bwd(dqa, dka, dva, [r[0] for r in res], [r[1] for r in res],
                                           [r[2] for r in res], cos, sin, winp, x, dx2, g_attn)
    dwin = _fold_in(_matmul_tn(dp, h1, tk=1280, tn=1024, out_dtype=F32, name="dw_in"), 0)
    drel, dsink = _bias_sink_grads([r[3] for r in res], [t[1] for t in tabs], dsk)

    small = dict(
        g_attn=dg_attn, b_in=_fold_in(dbp, 1), sinks=dsink[:, ::HEAD_DIM], rel_table=drel[:, :REL_BUCKETS].T,
        g_out_a=dg_out_a, g_out_b=dg_out_b, g_ffn=dg_ffn, g_final=dg_final.reshape(D_MODEL))
    return loss[0, 0], grad_x, dwin, early, small


SMALL_NAMES = ("g_attn", "b_in", "sinks", "rel_table", "g_out_a", "g_out_b", "g_ffn", "g_final")


def _pack_small(vals):
    flat = jnp.concatenate([vals[n].reshape(-1).astype(F32) for n in SMALL_NAMES])
    return jnp.pad(flat, (0, SMALL_ROWS * 128 - flat.shape[0])).reshape(SMALL_ROWS, 128)


def _unpack_small(packed, like):
    flat = packed.reshape(-1)
    out, off = {}, 0
    for n in SMALL_NAMES:
        size = like[n].size
        out[n] = flat[off:off + size].reshape(like[n].shape)
        off += size
    return out


def kernel(x, g_attn, w_in, b_in, sinks, rel_table, g_out_a, g_out_b, w_o, g_ffn, w_gate, w_up, w_down, g_final, loss_target, m_g_attn, m_w_in, m_b_in, m_sinks, m_rel_table, m_g_out_a, m_g_out_b, m_w_o, m_g_ffn, m_w_gate, m_w_up, m_w_down, m_g_final, v_g_attn, v_w_in, v_b_in, v_sinks, v_rel_table, v_g_out_a, v_g_out_b, v_w_o, v_g_ffn, v_w_gate, v_w_up, v_w_down, v_g_final):
    rest_names = ("w_o", "w_gate", "w_up", "w_down")

    rest = [w_o[0].astype(BF16), w_gate[0].astype(BF16).T, w_up[0].astype(BF16).T, w_down[0].astype(BF16)]
    wint = _all_gather(w_in[0].astype(BF16).T, name="gather_w_in").reshape(D_IN, D_MODEL)
    wint, rest = lax.optimization_barrier((wint, rest))
    wo_state, token_o = _exchange_start(rest[:1], gather=True, name="gather_w_o_start")
    token_o, ffn_src = lax.optimization_barrier((token_o, rest[1:]))
    ffn_state, token = _exchange_start(ffn_src, gather=True, name="gather_ffn_start")
    token = token + token_o

    def whole(got):
        return [g.reshape(N_DEV * g.shape[1], D_MODEL) for g in got]

    def wo_fn(after):
        return whole(_exchange_wait(wo_state, after, gather=True, name="gather_w_o_wait"))[0]

    def ffn_fn(after):
        return whole(_exchange_wait(ffn_state, after, gather=True, name="gather_ffn_wait"))

    def early_fn(dws):
        return _exchange_start([dws[n].reshape(N_DEV, -1, D_MODEL) for n in rest_names], gather=False,
                               name="scatter_rest_start")

    loss_part, grad_x, dwint, early_state, small = _local_step(
        x[0], loss_target[0], g_attn, wint, b_in, sinks, rel_table, g_out_a, g_out_b, g_ffn, g_final, token,
        wo_fn, ffn_fn, early_fn)
    loss = lax.psum(loss_part, ("x", "y", "c"))

    parts_in = dwint.astype(BF16).reshape(N_DEV, D_IN // N_DEV, D_MODEL)
    in_state, token3 = _exchange_start([parts_in], gather=False, name="scatter_w_in_start")
    got = _exchange_wait(early_state, token3, gather=False, name="scatter_rest_wait")

    def update(n, parts, w, m, v, transposed):
        if transposed:
            return [a.T[None] for a in _adamw(parts, w[0].T, m[0].T, v[0].T, name="adamw_" + n)]
        return [a[None] for a in _adamw(parts, w[0], m[0], v[0], name="adamw_" + n)]

    big = dict(w_o=update("w_o", got[0], w_o, m_w_o, v_w_o, False),
               w_gate=update("w_gate", got[1], w_gate, m_w_gate, v_w_gate, True),
               w_up=update("w_up", got[2], w_up, m_w_up, v_w_up, True),
               w_down=update("w_down", got[3], w_down, m_w_down, v_w_down, False))

    ws = dict(g_attn=g_attn, b_in=b_in, sinks=sinks, rel_table=rel_table, g_out_a=g_out_a, g_out_b=g_out_b,
              g_ffn=g_ffn, g_final=g_final)
    ms = dict(g_attn=m_g_attn, b_in=m_b_in, sinks=m_sinks, rel_table=m_rel_table, g_out_a=m_g_out_a,
              g_out_b=m_g_out_b, g_ffn=m_g_ffn, g_final=m_g_final)
    vs = dict(g_attn=v_g_attn, b_in=v_b_in, sinks=v_sinks, rel_table=v_rel_table, g_out_a=v_g_out_a,
              g_out_b=v_g_out_b, g_ffn=v_g_ffn, g_final=v_g_final)
    sparts = _all_gather(_pack_small(small), name="gather_small")
    sm_packed = _adamw_small(sparts, _pack_small(ws), _pack_small(ms), _pack_small(vs))
    sm = [_unpack_small(a, ws) for a in sm_packed]

    done = sm_packed[1][:1, :1] + sum(big[n][1][0, :1, :1] for n in rest_names)
    got_in = _exchange_wait(in_state, done, gather=False, name="scatter_w_in_wait")[0]
    big["w_in"] = update("w_in", got_in, w_in, m_w_in, v_w_in, True)

    order = ("g_attn", "w_in", "b_in", "sinks", "rel_table", "g_out_a", "g_out_b", "w_o", "g_ffn", "w_gate", "w_up",
             "w_down", "g_final")
    outs = [loss, grad_x[None]]
    for k in range(4):
        outs += [big[n][k] if n in big else sm[k][n] for n in order]
    return tuple(outs)
```

```python
import functools
import math

import jax
import jax.numpy as jnp
from jax import lax
from jax.experimental import pallas as pl
from jax.experimental.pallas import tpu as pltpu

F32 = jnp.float32
BF16 = jnp.bfloat16

N_DEV = 8
D_MODEL = 1024
HEAD_DIM = 64
N_HEADS = 8
PAIR = 2 * HEAD_DIM
WIDTH = N_HEADS * HEAD_DIM
D_IN = 2304
D_INP = 2560
D_FF = 2816
BLK = 128
ROPE_THETA = 150000.0
REL_BUCKETS = 32
REL_MAX_DISTANCE = 2048
EPS = 1e-5
NEG = -1e30
BRANCHES = ((128, 1), (512, 4), (2048, 16))
Q_SCALE = HEAD_DIM ** -0.5

ADAM_LR = 0.001
ADAM_B1 = 0.9
ADAM_B2 = 0.999
ADAM_EPS = 1e-08
ADAM_WD = 0.01
ADAM_STEP = 10

VMEM_LIMIT = 56 * 1024 * 1024
MESH = pl.DeviceIdType.MESH

NT = (((1,), (1,)), ((), ()))
TN = (((0,), (0,)), ((), ()))

SMALL_ROWS = 56


def _params(sem=None):
    return pltpu.CompilerParams(dimension_semantics=sem, vmem_limit_bytes=VMEM_LIMIT)


def _sigmoid(x):
    return 1.0 / (1.0 + jnp.exp2(x * (-1.0 / math.log(2.0))))


def _rms_bwd(dh, xh, r, g):
    u = dh * g
    return r * (u - xh * jnp.mean(u * xh, axis=-1, keepdims=True))


def _rope_rot(t, first):
    return jnp.where(first, pltpu.roll(t, 96, 1), pltpu.roll(t, 32, 1))


N_CHUNK = WIDTH // PAIR


def _scr(tm):
    return pltpu.VMEM((N_CHUNK, tm, PAIR), F32)


def _scr_get(scr):
    return jnp.concatenate([scr[j] for j in range(N_CHUNK)], axis=1)


def _scr_put(scr, val):
    for j in range(N_CHUNK):
        scr[j] = val[:, j * PAIR:(j + 1) * PAIR]


def _unstride(view_ref, scr, dil, tm):
    n = tm // dil
    chunks = scr.shape[0]
    for r in range(dil):
        for j in range(chunks):
            col = (r * chunks + j) * PAIR
            scr.at[j][pl.ds(r, n, stride=dil), :] = view_ref[:, col:col + PAIR].astype(F32)


def _restride(scr, out_ref, dil, tm):
    n = tm // dil
    chunks = scr.shape[0]
    for r in range(dil):
        for j in range(chunks):
            col = (r * chunks + j) * PAIR
            rows = scr[j] if dil == 1 else scr.at[j][pl.ds(r, n, stride=dil), :]
            out_ref[:, col:col + PAIR] = rows.astype(out_ref.dtype)


def _view_specs(tm, width=WIDTH):
    return [pl.BlockSpec((tm // dil, dil * width), lambda i: (i, 0)) for _, dil in BRANCHES]


def _view_shapes(T, dtype, width=WIDTH):
    return [jax.ShapeDtypeStruct((T // dil, dil * width), dtype) for _, dil in BRANCHES]


def _norm_proj(x, g, w, b, cos, sin, *, tm=512):
    T = x.shape[0]

    def body(x_ref, g_ref, w_ref, b_ref, cos_ref, sin_ref, h_ref, qa_ref, ka_ref, va_ref, *rest):
        outs_b, ys = rest[:9], rest[9]
        xv = x_ref[...]
        r = lax.rsqrt(jnp.mean(xv * xv, axis=-1, keepdims=True) + EPS)
        h = (xv * r * g_ref[...]).astype(BF16)
        h_ref[...] = h
        cosv = cos_ref[...]
        sinv = sin_ref[...]
        lane = lax.broadcasted_iota(jnp.int32, (tm, PAIR), 1)
        first = (lane % HEAD_DIM) < (HEAD_DIM // 2)

        def proj(off):
            return (lax.dot_general(h, w_ref[off:off + 256, :], NT, preferred_element_type=F32)
                    + b_ref[:, off:off + 256])

        for (off, width, rot, scale), o_ref in zip(((0, 512, True, Q_SCALE), (512, 256, True, 1.0), (768, 256, False, 1.0)),
                                                   (qa_ref, ka_ref, va_ref)):
            for c in range(0, width, 256):
                y = proj(off + c)
                for j in range(0, 256, PAIR):
                    t = y[:, j:j + PAIR]
                    if rot:
                        t = t * cosv + _rope_rot(t, first) * sinv
                    if scale != 1.0:
                        t = t * scale
                    o_ref[:, c + j:c + j + PAIR] = t.astype(BF16)
        for n, (off, scale) in enumerate(((1024, Q_SCALE), (1536, 1.0), (2048, 1.0))):
            for c in range(0, WIDTH, 256):
                y = proj(off + c)
                y = y * scale if scale != 1.0 else y
                for j in range(0, 256, PAIR):
                    ys[(c + j) // PAIR] = y[:, j:j + PAIR]
            for (_, dil), o_ref in zip(BRANCHES, outs_b[3 * n:3 * n + 3]):
                _restride(ys, o_ref, dil, tm)

    row = lambda w_: pl.BlockSpec((tm, w_), lambda i: (i, 0))
    full = lambda a: pl.BlockSpec(a.shape, lambda i: (0, 0))
    return pl.pallas_call(
        body, name="norm_proj", grid=(T // tm,),
        in_specs=[row(D_MODEL), full(g), full(w), full(b), row(PAIR), row(PAIR)],
        out_specs=[row(D_MODEL), row(512), row(256), row(256)] + _view_specs(tm) * 3,
        out_shape=[jax.ShapeDtypeStruct((T, n), BF16) for n in (D_MODEL, 512, 256, 256)] + _view_shapes(T, BF16) * 3,
        scratch_shapes=[_scr(tm)],
        compiler_params=_params(("arbitrary",)),
    )(x, g, w, b, cos, sin)


SUB = 4
AHEAD = 2


def _attn_specs(kvw):
    q_spec = pl.BlockSpec((SUB * BLK, WIDTH), lambda r, i: (i, r))
    kc_spec = pl.BlockSpec((SUB * BLK, kvw), lambda r, i: (i, r))
    kp_spec = pl.BlockSpec((BLK, kvw), lambda r, i: (jnp.maximum(SUB * i - 1, 0), r))
    b_spec = pl.BlockSpec((2, N_HEADS, BLK, 2 * BLK), lambda r, i: (0, 0, 0, 0))
    return q_spec, kp_spec, kc_spec, b_spec


def _window(prev_ref, cur_ref, j, ksl):
    before = prev_ref[:, ksl] if j == 0 else cur_ref[(j - 1) * BLK:j * BLK, ksl]
    return jnp.concatenate([before, cur_ref[j * BLK:(j + 1) * BLK, ksl]], axis=0)


def _attn_fwd(q, k, v, bias, sinks, *, dil, kv_pairs, use_sink, name):
    L = q.shape[0]
    ns = L // (SUB * BLK)
    kvw = kv_pairs * PAIR
    rep = 4 // kv_pairs

    def body(sink_ref, q_ref, kp_ref, kc_ref, vp_ref, vc_ref, b_ref, o_ref, lse_ref):
        lane = lax.broadcasted_iota(jnp.int32, (1, PAIR), 1)
        lo = lane < HEAD_DIM
        first = jnp.where(pl.program_id(1) == 0, 1, 0)
        def scores(j, hp):
            rows = slice(j * BLK, (j + 1) * BLK)
            sl = slice(hp * PAIR, (hp + 1) * PAIR)
            ksl = slice((hp // rep) * PAIR, (hp // rep + 1) * PAIR)
            qp = q_ref[rows, sl]
            kk = _window(kp_ref, kc_ref, j, ksl)
            vv = _window(vp_ref, vc_ref, j, ksl)
            heads = []
            for e in range(2):
                h = 2 * hp + e
                msk = lo if e == 0 else jnp.logical_not(lo)
                qm = jnp.where(msk, qp, jnp.zeros_like(qp))
                s = lax.dot_general(qm, kk, NT, preferred_element_type=F32) + (b_ref[first, h] if j == 0 else b_ref[0, h])
                heads.append((h, msk, s))
            return rows, sl, vv, heads

        def outputs(rows, sl, vv, heads):
            o_pair = None
            lse_pair = None
            for h, msk, s in heads:
                m = jnp.max(s, axis=-1, keepdims=True)
                if use_sink:
                    sk = sink_ref[h]
                    m = jnp.maximum(m, sk)
                p = jnp.exp(s - m)
                l = jnp.sum(p, axis=-1, keepdims=True)
                if use_sink:
                    l = l + jnp.exp(sk - m)
                vm = jnp.where(msk, vv, jnp.zeros_like(vv))
                oe = jnp.dot(p.astype(BF16), vm, preferred_element_type=F32) * (1.0 / l)
                ls = m + jnp.log(l)
                if o_pair is None:
                    o_pair = oe
                    lse_pair = jnp.broadcast_to(ls, (BLK, PAIR))
                else:
                    o_pair = o_pair + oe
                    lse_pair = jnp.where(lo, lse_pair, ls)
            o_ref[rows, sl] = o_pair.astype(BF16)
            lse_ref[rows, sl] = lse_pair

        items = [(j, hp) for j in range(SUB) for hp in range(4)]
        queue = [scores(*it) for it in items[:AHEAD]]
        for n in range(len(items)):
            if n + AHEAD < len(items):
                queue.append(scores(*items[n + AHEAD]))
            outputs(*queue.pop(0))

    q_spec, kp_spec, kc_spec, b_spec = _attn_specs(kvw)
    return pl.pallas_call(
        body, name=name, grid=(dil, ns),
        in_specs=[pl.BlockSpec(memory_space=pltpu.SMEM), q_spec, kp_spec, kc_spec, kp_spec, kc_spec, b_spec],
        out_specs=[q_spec, q_spec],
        out_shape=[jax.ShapeDtypeStruct((L, dil * WIDTH), BF16), jax.ShapeDtypeStruct((L, dil * WIDTH), F32)],
        compiler_params=_params(("arbitrary", "arbitrary")),
    )(sinks, q, k, k, v, v, bias)


def _attn_bwd(q, k, v, o, do, lse, bias, sinks, *, dil, kv_pairs, use_sink, name):
    L = q.shape[0]
    ns = L // (SUB * BLK)
    n_steps = dil * ns
    kvw = kv_pairs * PAIR
    rep = 4 // kv_pairs
    last = slice((SUB - 1) * BLK, SUB * BLK)

    def body(sink_ref, q_ref, kp_ref, kc_ref, vp_ref, vc_ref, o_ref, do_ref, lse_ref, b_ref,
             dq_ref, dk_ref, dv_ref, dsum_ref, dsk_ref, pk_ref, pv_ref):
        t = pl.program_id(0)
        i = t % ns

        @pl.when(t == 0)
        def _():
            dsum_ref[...] = jnp.zeros_like(dsum_ref)
            dsk_ref[...] = jnp.zeros_like(dsk_ref)
            pk_ref[...] = jnp.zeros_like(pk_ref)
            pv_ref[...] = jnp.zeros_like(pv_ref)

        @pl.when(t < n_steps)
        def _():
            lo = lax.broadcasted_iota(jnp.int32, (1, PAIR), 1) < HEAD_DIM
            first = jnp.where(i == 0, 1, 0)
            dks = [[None] * kv_pairs for _ in range(SUB)]
            dvs = [[None] * kv_pairs for _ in range(SUB)]
            def scores(j, hp):
                rows = slice(j * BLK, (j + 1) * BLK)
                kvp = hp // rep
                sl = slice(hp * PAIR, (hp + 1) * PAIR)
                ksl = slice(kvp * PAIR, (kvp + 1) * PAIR)
                qp = q_ref[rows, sl]
                dop = do_ref[rows, sl]
                prod = dop.astype(F32) * o_ref[rows, sl].astype(F32)
                kk = _window(kp_ref, kc_ref, j, ksl)
                vv = _window(vp_ref, vc_ref, j, ksl)
                heads = []
                for e in range(2):
                    h = 2 * hp + e
                    msk = lo if e == 0 else jnp.logical_not(lo)
                    qm = jnp.where(msk, qp, jnp.zeros_like(qp))
                    dom = jnp.where(msk, dop, jnp.zeros_like(dop))
                    km = jnp.where(msk, kk, jnp.zeros_like(kk))
                    s = (lax.dot_general(qm, kk, NT, preferred_element_type=F32)
                         + (b_ref[first, h] if j == 0 else b_ref[0, h]))
                    dp = lax.dot_general(dom, vv, NT, preferred_element_type=F32)
                    heads.append((h, msk, qm, dom, km, s, dp))
                return j, rows, kvp, sl, prod, heads

            def grads(j, rows, kvp, sl, prod, heads):
                dq_pair = None
                c_pair = None
                qms, doms, dsbs, pbs = [], [], [], []
                for h, msk, qm, dom, km, s, dp in heads:
                    ls = lse_ref[rows, h * HEAD_DIM:h * HEAD_DIM + 1]
                    p = jnp.exp(s - ls)
                    delta = jnp.sum(jnp.where(msk, prod, 0.0), axis=-1, keepdims=True)
                    ds = p * (dp - delta)
                    if use_sink:
                        ce = jnp.exp(sink_ref[h] - ls) * delta
                        c_pair = jnp.broadcast_to(ce, (BLK, PAIR)) if c_pair is None else jnp.where(msk, ce, c_pair)
                    else:
                        dsum_ref[h] += ds
                    dsb = ds.astype(BF16)
                    dqe = jnp.dot(dsb, km, preferred_element_type=F32)
                    dq_pair = dqe if dq_pair is None else dq_pair + dqe
                    qms.append(qm)
                    doms.append(dom)
                    dsbs.append(dsb)
                    pbs.append(p.astype(BF16))
                dke = lax.dot_general(jnp.concatenate(dsbs, axis=0), jnp.concatenate(qms, axis=0), TN,
                                      preferred_element_type=F32)
                dve = lax.dot_general(jnp.concatenate(pbs, axis=0), jnp.concatenate(doms, axis=0), TN,
                                      preferred_element_type=F32)
                dks[j][kvp] = dke if dks[j][kvp] is None else dks[j][kvp] + dke
                dvs[j][kvp] = dve if dvs[j][kvp] is None else dvs[j][kvp] + dve
                dq_ref[rows, sl] = (dq_pair * Q_SCALE).astype(BF16)
                if use_sink:
                    dsk_ref[:, sl] += c_pair

            items = [(j, hp) for j in range(SUB) for hp in range(4)]
            ahead = AHEAD + 1 if use_sink else AHEAD
            queue = [scores(*it) for it in items[:ahead]]
            for n in range(len(items)):
                if n + ahead < len(items):
                    queue.append(scores(*items[n + ahead]))
                grads(*queue.pop(0))
            for kvp in range(kv_pairs):
                ksl = slice(kvp * PAIR, (kvp + 1) * PAIR)
                for pend_ref, out_ref, parts in ((pk_ref, dk_ref, [d[kvp] for d in dks]),
                                                 (pv_ref, dv_ref, [d[kvp] for d in dvs])):
                    if SUB > 1:
                        out_ref[:(SUB - 1) * BLK, ksl] = pend_ref[:(SUB - 1) * BLK, ksl].astype(BF16)
                    out_ref[last, ksl] = (pend_ref[last, ksl] + parts[0][:BLK]).astype(BF16)
                    for j in range(SUB):
                        own = parts[j][BLK:]
                        pend_ref[j * BLK:(j + 1) * BLK, ksl] = own + parts[j + 1][:BLK] if j + 1 < SUB else own

        @pl.when(t == n_steps)
        def _():
            dk_ref[...] = pk_ref[...].astype(BF16)
            dv_ref[...] = pv_ref[...].astype(BF16)

    def at(t):
        t = jnp.minimum(t, n_steps - 1)
        return t % ns, t // ns

    def before(t):
        return at(jnp.maximum(t - 1, 0))

    q_spec = pl.BlockSpec((SUB * BLK, WIDTH), at)
    kc_spec = pl.BlockSpec((SUB * BLK, kvw), at)
    kp_spec = pl.BlockSpec((BLK, kvw), lambda t: (jnp.maximum(SUB * at(t)[0] - 1, 0), at(t)[1]))
    b_spec = pl.BlockSpec((2, N_HEADS, BLK, 2 * BLK), lambda t: (0, 0, 0, 0))
    dkv_spec = pl.BlockSpec((SUB * BLK, kvw), before)
    return pl.pallas_call(
        body, name=name, grid=(n_steps + 1,),
        in_specs=[pl.BlockSpec(memory_space=pltpu.SMEM), q_spec, kp_spec, kc_spec, kp_spec, kc_spec,
                  q_spec, q_spec, q_spec, b_spec],
        out_specs=[q_spec, dkv_spec, dkv_spec,
                   pl.BlockSpec((N_HEADS, BLK, 2 * BLK), lambda t: (0, 0, 0)),
                   pl.BlockSpec((BLK, WIDTH), lambda t: (0, 0))],
        out_shape=[jax.ShapeDtypeStruct((L, dil * WIDTH), BF16),
                   jax.ShapeDtypeStruct((L, dil * kvw), BF16),
                   jax.ShapeDtypeStruct((L, dil * kvw), BF16),
                   jax.ShapeDtypeStruct((N_HEADS, BLK, 2 * BLK), F32),
                   jax.ShapeDtypeStruct((BLK, WIDTH), F32)],
        scratch_shapes=[pltpu.VMEM((SUB * BLK, kvw), F32), pltpu.VMEM((SUB * BLK, kvw), F32)],
        compiler_params=_params(("arbitrary",)),
    )(sinks, q, k, k, v, v, o, do, lse, bias)


def _merge_wo(x, oa, o1, o2, o3, l1, l2, l3, ga, gb, wo, gf, *, tm=512):
    T = x.shape[0]

    def body(x_ref, oa_ref, o1_ref, o2_ref, o3_ref, l1_ref, l2_ref, l3_ref, ga_ref, gb_ref, wo_ref, gf_ref,
             x2_ref, mix_ref, h2_ref, ob1_ref, ob4_ref, ob16_ref, ls1_ref, ls4_ref, ls16_ref, so2, so3, sl2, sl3):
        _unstride(o2_ref, so2, BRANCHES[1][1], tm)
        _unstride(o3_ref, so3, BRANCHES[2][1], tm)
        _unstride(l2_ref, sl2, BRANCHES[1][1], tm)
        _unstride(l3_ref, sl3, BRANCHES[2][1], tm)
        la, lb, lc = l1_ref[...], _scr_get(sl2), _scr_get(sl3)
        m = jnp.maximum(jnp.maximum(la, lb), lc)
        ea, eb, ec = jnp.exp(la - m), jnp.exp(lb - m), jnp.exp(lc - m)
        den = ea + eb + ec
        inv = 1.0 / den
        ob = (ea * o1_ref[...].astype(F32) + eb * _scr_get(so2) + ec * _scr_get(so3)) * inv
        _scr_put(so2, ob)
        _scr_put(sl2, m + jnp.log(den))
        for (_, dil), o_ref, l_ref in zip(BRANCHES, (ob1_ref, ob4_ref, ob16_ref), (ls1_ref, ls4_ref, ls16_ref)):
            _restride(so2, o_ref, dil, tm)
            _restride(sl2, l_ref, dil, tm)
        oav = oa_ref[...].astype(F32)
        ra = lax.rsqrt(jnp.mean(oav * oav, axis=-1, keepdims=True) + EPS)
        rb = lax.rsqrt(jnp.mean(ob * ob, axis=-1, keepdims=True) + EPS)
        mix_ref[:, :WIDTH] = (oav * ra * ga_ref[...]).astype(BF16)
        mix_ref[:, WIDTH:] = (ob * rb * gb_ref[...]).astype(BF16)
        x2 = x_ref[...] + jnp.dot(mix_ref[...], wo_ref[...], preferred_element_type=F32)
        x2_ref[...] = x2
        r2 = lax.rsqrt(jnp.mean(x2 * x2, axis=-1, keepdims=True) + EPS)
        h2_ref[...] = (x2 * r2 * gf_ref[...]).astype(BF16)

    row = lambda w_: pl.BlockSpec((tm, w_), lambda i: (i, 0))
    full = lambda a: pl.BlockSpec(a.shape, lambda i: (0, 0))
    return pl.pallas_call(
        body, name="merge_wo", grid=(T // tm,),
        in_specs=[row(D_MODEL), row(WIDTH)] + _view_specs(tm) * 2 + [full(ga), full(gb), full(wo), full(gf)],
        out_specs=[row(D_MODEL), row(D_MODEL), row(D_MODEL)] + _view_specs(tm) * 2,
        out_shape=[jax.ShapeDtypeStruct((T, D_MODEL), F32), jax.ShapeDtypeStruct((T, D_MODEL), BF16),
                   jax.ShapeDtypeStruct((T, D_MODEL), BF16)] + _view_shapes(T, BF16) + _view_shapes(T, F32),
        scratch_shapes=[_scr(tm)] * 4,
        compiler_params=_params(("arbitrary",)),
    )(x, oa, o1, o2, o3, l1, l2, l3, ga, gb, wo, gf)


def _ffn_up(h2, wgt, wut, *, tm=512, fc=D_FF, rc=512, cc=256):
    T = h2.shape[0]

    def body(h_ref, wg_ref, wu_ref, gate_ref, up_ref, act_ref):
        for s in range(0, tm, rc):
            h = h_ref[s:s + rc, :]
            for c in range(0, fc, cc):
                gt = lax.dot_general(h, wg_ref[c:c + cc, :], NT, preferred_element_type=F32)
                u = lax.dot_general(h, wu_ref[c:c + cc, :], NT, preferred_element_type=F32)
                gate_ref[s:s + rc, c:c + cc] = gt.astype(BF16)
                up_ref[s:s + rc, c:c + cc] = u.astype(BF16)
                act_ref[s:s + rc, c:c + cc] = (gt * _sigmoid(gt) * u).astype(BF16)

    rowd = pl.BlockSpec((tm, D_MODEL), lambda i, c: (i, 0))
    wrow = pl.BlockSpec((fc, D_MODEL), lambda i, c: (c, 0))
    oc = pl.BlockSpec((tm, fc), lambda i, c: (i, c))
    return pl.pallas_call(
        body, name="ffn_up", grid=(T // tm, D_FF // fc),
        in_specs=[rowd, wrow, wrow],
        out_specs=[oc, oc, oc],
        out_shape=[jax.ShapeDtypeStruct((T, D_FF), BF16)] * 3,
        compiler_params=_params(("arbitrary", "arbitrary")),
    )(h2, wgt, wut)


def _ffn_down_loss(act, wd, x2, tgt, g, *, tm=512, rc=256):
    T = x2.shape[0]

    def body(act_ref, wd_ref, x2_ref, tgt_ref, g_ref, dx_ref, dxb_ref, loss_ref, dg_ref):
        @pl.when(pl.program_id(0) == 0)
        def _():
            loss_ref[...] = jnp.zeros_like(loss_ref)
            dg_ref[...] = jnp.zeros_like(dg_ref)

        gv = g_ref[...]
        lsum = jnp.zeros((1, 1), F32)
        dgs = jnp.zeros((1, D_MODEL), F32)
        for c in range(0, tm, rc):
            x3 = x2_ref[c:c + rc, :] + jnp.dot(act_ref[c:c + rc, :], wd_ref[...], preferred_element_type=F32)
            r = lax.rsqrt(jnp.mean(x3 * x3, axis=-1, keepdims=True) + EPS)
            xh = x3 * r
            diff = xh * gv - tgt_ref[c:c + rc, :]
            lsum = lsum + jnp.sum(jnp.sum(diff * diff, axis=-1, keepdims=True), axis=0, keepdims=True)
            dy = diff * (1.0 / D_MODEL)
            dgs = dgs + jnp.sum(dy * xh, axis=0, keepdims=True)
            dx = _rms_bwd(dy, xh, r, gv)
            dx_ref[c:c + rc, :] = dx
            dxb_ref[c:c + rc, :] = dx.astype(BF16)
        loss_ref[...] += lsum * (0.5 / D_MODEL)
        dg_ref[...] += dgs

    rowd = pl.BlockSpec((tm, D_MODEL), lambda i: (i, 0))
    return pl.pallas_call(
        body, name="ffn_down_loss", grid=(T // tm,),
        in_specs=[pl.BlockSpec((tm, D_FF), lambda i: (i, 0)), pl.BlockSpec((D_FF, D_MODEL), lambda i: (0, 0)),
                  rowd, rowd, pl.BlockSpec(g.shape, lambda i: (0, 0))],
        out_specs=[rowd, rowd, pl.BlockSpec((1, 1), lambda i: (0, 0)), pl.BlockSpec((1, D_MODEL), lambda i: (0, 0))],
        out_shape=[jax.ShapeDtypeStruct((T, D_MODEL), F32), jax.ShapeDtypeStruct((T, D_MODEL), BF16),
                   jax.ShapeDtypeStruct((1, 1), F32), jax.ShapeDtypeStruct((1, D_MODEL), F32)],
        compiler_params=_params(("arbitrary",)),
    )(act, wd, x2, tgt, g)


def _ffn_bwd(dx3, gate, up, wd, wgt, wut, x2, g, *, tm=256, cc=256):
    T = x2.shape[0]

    def body(dx_ref, gate_ref, up_ref, wd_ref, wg_ref, wu_ref, x2_ref, g_ref,
             dgate_ref, dup_ref, dx2_ref, dx2b_ref, dg_ref):
        @pl.when(pl.program_id(0) == 0)
        def _():
            dg_ref[...] = jnp.zeros_like(dg_ref)

        dxb = dx_ref[...].astype(BF16)
        for c in range(0, D_FF, cc):
            dact = lax.dot_general(dxb, wd_ref[c:c + cc, :], NT, preferred_element_type=F32)
            gt = gate_ref[:, c:c + cc].astype(F32)
            u = up_ref[:, c:c + cc].astype(F32)
            sg = _sigmoid(gt)
            a = dact * sg
            dgate_ref[:, c:c + cc] = (a * u * ((1.0 + gt) - gt * sg)).astype(BF16)
            dup_ref[:, c:c + cc] = (a * gt).astype(BF16)
        dh = (jnp.dot(dgate_ref[...], wg_ref[...], preferred_element_type=F32)
              + jnp.dot(dup_ref[...], wu_ref[...], preferred_element_type=F32))
        xv = x2_ref[...]
        r = lax.rsqrt(jnp.mean(xv * xv, axis=-1, keepdims=True) + EPS)
        xh = xv * r
        dg_ref[...] += jnp.sum(dh * xh, axis=0, keepdims=True)
        d = dx_ref[...] + _rms_bwd(dh, xh, r, g_ref[...])
        dx2_ref[...] = d
        dx2b_ref[...] = d.astype(BF16)

    rowd = pl.BlockSpec((tm, D_MODEL), lambda i: (i, 0))
    rowf = pl.BlockSpec((tm, D_FF), lambda i: (i, 0))
    wfull = pl.BlockSpec((D_FF, D_MODEL), lambda i: (0, 0), pipeline_mode=pl.Buffered(1))
    return pl.pallas_call(
        body, name="ffn_bwd", grid=(T // tm,),
        in_specs=[rowd, rowf, rowf, wfull, wfull, wfull, rowd, pl.BlockSpec(g.shape, lambda i: (0, 0))],
        out_specs=[rowf, rowf, rowd, rowd, pl.BlockSpec((1, D_MODEL), lambda i: (0, 0))],
        out_shape=[jax.ShapeDtypeStruct((T, D_FF), BF16), jax.ShapeDtypeStruct((T, D_FF), BF16),
                   jax.ShapeDtypeStruct((T, D_MODEL), F32), jax.ShapeDtypeStruct((T, D_MODEL), BF16),
                   jax.ShapeDtypeStruct((1, D_MODEL), F32)],
        compiler_params=_params(("arbitrary",)),
    )(dx3, gate, up, wd, wgt, wut, x2, g)


def _matmul_tn(a, b, *, tk, tn, tt=2048, out_dtype=BF16, name):
    T, K = a.shape
    N = b.shape[1]
    nt = T // tt

    def body(a_ref, b_ref, o_ref, acc_ref):
        part = lax.dot_general(a_ref[...], b_ref[...], TN, preferred_element_type=F32)

        @pl.when(pl.program_id(2) == 0)
        def _():
            acc_ref[...] = part

        @pl.when(pl.program_id(2) > 0)
        def _():
            acc_ref[...] += part

        @pl.when(pl.program_id(2) == nt - 1)
        def _():
            o_ref[...] = acc_ref[...].astype(out_dtype)

    return pl.pallas_call(
        body, name=name, grid=(K // tk, N // tn, nt),
        in_specs=[pl.BlockSpec((tt, tk), lambda i, j, t: (t, i)), pl.BlockSpec((tt, tn), lambda i, j, t: (t, j))],
        out_specs=pl.BlockSpec((tk, tn), lambda i, j, t: (i, j)),
        out_shape=jax.ShapeDtypeStruct((K, N), out_dtype),
        scratch_shapes=[pltpu.VMEM((tk, tn), F32)],
        compiler_params=_params(("arbitrary", "arbitrary", "arbitrary")),
    )(a, b)


def _wo_bwd(dx2b, wo, oa, ob, ga, gb, *, tm=512):
    T = dx2b.shape[0]

    def body(dx_ref, wo_ref, oa_ref, ob_ref, ga_ref, gb_ref, doa_ref, dob1_ref, dob4_ref, dob16_ref, dga_ref, dgb_ref, scr):
        @pl.when(pl.program_id(0) == 0)
        def _():
            dga_ref[...] = jnp.zeros_like(dga_ref)
            dgb_ref[...] = jnp.zeros_like(dgb_ref)

        dm = lax.dot_general(dx_ref[...], wo_ref[...], NT, preferred_element_type=F32)
        for o_ref, g_ref, dg_ref, sl in ((oa_ref, ga_ref, dga_ref, slice(0, WIDTH)),
                                         (ob_ref, gb_ref, dgb_ref, slice(WIDTH, 2 * WIDTH))):
            ov = o_ref[...].astype(F32)
            r = lax.rsqrt(jnp.mean(ov * ov, axis=-1, keepdims=True) + EPS)
            xh = ov * r
            d = dm[:, sl]
            dg_ref[...] += jnp.sum(d * xh, axis=0, keepdims=True)
            do = _rms_bwd(d, xh, r, g_ref[...])
            if o_ref is oa_ref:
                doa_ref[...] = do.astype(BF16)
            else:
                _scr_put(scr, do)
                for (_, dil), v_ref in zip(BRANCHES, (dob1_ref, dob4_ref, dob16_ref)):
                    _restride(scr, v_ref, dil, tm)

    row = lambda w_: pl.BlockSpec((tm, w_), lambda i: (i, 0))
    full = lambda a: pl.BlockSpec(a.shape, lambda i: (0, 0))
    return pl.pallas_call(
        body, name="wo_bwd", grid=(T // tm,),
        in_specs=[row(D_MODEL), full(wo), row(WIDTH), row(WIDTH), full(ga), full(gb)],
        out_specs=[row(WIDTH)] + _view_specs(tm)
        + [pl.BlockSpec((1, WIDTH), lambda i: (0, 0)), pl.BlockSpec((1, WIDTH), lambda i: (0, 0))],
        out_shape=[jax.ShapeDtypeStruct((T, WIDTH), BF16)] + _view_shapes(T, BF16)
        + [jax.ShapeDtypeStruct((1, WIDTH), F32), jax.ShapeDtypeStruct((1, WIDTH), F32)],
        scratch_shapes=[_scr(tm)],
        compiler_params=_params(("arbitrary",)),
    )(dx2b, wo, oa, ob, ga, gb)


def _inproj_bwd(dqa, dka, dva, dqs, dks, dvs, cos, sin, w, x, dx2, g, *, tm=512):
    T = dqa.shape[0]

    def body(dqa_ref, dka_ref, dva_ref, q1, q2, q3, k1, k2, k3, v1, v2, v3, cos_ref, sin_ref, w_ref, x_ref, dx2_ref,
             g_ref, dp_ref, db_ref, gx_ref, dg_ref, acc, tmp):
        @pl.when(pl.program_id(0) == 0)
        def _():
            db_ref[...] = jnp.zeros_like(db_ref)
            dg_ref[...] = jnp.zeros_like(dg_ref)

        cosv = cos_ref[...]
        sinv = sin_ref[...]
        lane = lax.broadcasted_iota(jnp.int32, (tm, PAIR), 1)
        first = (lane % HEAD_DIM) < (HEAD_DIM // 2)

        def put(off, val):
            dp_ref[:, off:off + PAIR] = val.astype(BF16)
            db_ref[:, off:off + PAIR] += jnp.sum(val, axis=0, keepdims=True)

        for src, off, width in ((dqa_ref, 0, 512), (dka_ref, 512, 256)):
            for j in range(0, width, PAIR):
                d = src[:, j:j + PAIR].astype(F32)
                put(off + j, d * cosv - _rope_rot(d, first) * sinv)
        for j in range(0, 256, PAIR):
            put(768 + j, dva_ref[:, j:j + PAIR].astype(F32))
        for (a, b, c), off in (((q1, q2, q3), 1024), ((k1, k2, k3), 1536), ((v1, v2, v3), 2048)):
            _unstride(b, acc, BRANCHES[1][1], tm)
            _unstride(c, tmp, BRANCHES[2][1], tm)
            for j in range(N_CHUNK):
                put(off + j * PAIR, a[:, j * PAIR:(j + 1) * PAIR].astype(F32) + acc[j] + tmp[j])

        dh = jnp.dot(dp_ref[...], w_ref[...], preferred_element_type=F32)
        xv = x_ref[...]
        r = lax.rsqrt(jnp.mean(xv * xv, axis=-1, keepdims=True) + EPS)
        xh = xv * r
        dg_ref[...] += jnp.sum(dh * xh, axis=0, keepdims=True)
        gx_ref[...] = dx2_ref[...] + _rms_bwd(dh, xh, r, g_ref[...])

    row = lambda w_: pl.BlockSpec((tm, w_), lambda i: (i, 0))
    full = lambda a: pl.BlockSpec(a.shape, lambda i: (0, 0))
    return pl.pallas_call(
        body, name="inproj_bwd", grid=(T // tm,),
        in_specs=[row(512), row(256), row(256)] + _view_specs(tm) * 3 + [row(PAIR), row(PAIR)]
        + [full(w), row(D_MODEL), row(D_MODEL), full(g)],
        out_specs=[row(D_INP), pl.BlockSpec((1, D_INP), lambda i: (0, 0)), row(D_MODEL),
                   pl.BlockSpec((1, D_MODEL), lambda i: (0, 0))],
        out_shape=[jax.ShapeDtypeStruct((T, D_INP), BF16), jax.ShapeDtypeStruct((1, D_INP), F32),
                   jax.ShapeDtypeStruct((T, D_MODEL), F32), jax.ShapeDtypeStruct((1, D_MODEL), F32)],
        scratch_shapes=[_scr(tm)] * 2,
        compiler_params=_params(("arbitrary",)),
    )(dqa, dka, dva, *dqs, *dks, *dvs, cos, sin, w, x, dx2, g)


def _bias_sink_grads(dsums, bmaps, dsk):
    def body(s1, s2, s3, m1, m2, m3, dsk_ref, drel_ref, dsink_ref):
        row = lax.broadcasted_iota(jnp.int32, (N_HEADS, 128), 0)
        lane = lax.broadcasted_iota(jnp.int32, (N_HEADS, 128), 1)
        out = jnp.zeros((N_HEADS, 128), F32)
        for s_ref, m_ref in ((s1, m1), (s2, m2), (s3, m3)):
            bm = m_ref[...]
            for h in range(N_HEADS):
                a = s_ref[h]
                for b in range(REL_BUCKETS):
                    v = jnp.sum(jnp.sum(jnp.where(bm == b, a, 0.0), axis=-1, keepdims=True), axis=0, keepdims=True)
                    out = out + jnp.where((row == h) & (lane == b), v, 0.0)
        drel_ref[...] = out
        dsink_ref[...] = -jnp.sum(dsk_ref[...], axis=0, keepdims=True)

    vm = pl.BlockSpec(memory_space=pltpu.VMEM)
    return pl.pallas_call(
        body, name="bias_sink_grads",
        in_specs=[vm] * 7, out_specs=[vm, vm],
        out_shape=[jax.ShapeDtypeStruct((N_HEADS, 128), F32), jax.ShapeDtypeStruct((1, WIDTH), F32)],
        compiler_params=_params(),
    )(*dsums, *bmaps, dsk)


def _all_gather(blk, *, name):
    R, C = blk.shape

    def body(x_ref, out_ref, send_sems, recv_sems, local_sem):
        x, y, c = lax.axis_index("x"), lax.axis_index("y"), lax.axis_index("c")
        me, sibling = (x, y, c), (x, y, 1 - c)
        chips = [(1 - x, y), (x, 1 - y), (1 - x, 1 - y)]

        def slot(px, py, pc):
            return out_ref.at[4 * px + 2 * py + pc]

        def copy(k, block, to, src=None):
            return pltpu.make_async_remote_copy(
                src_ref=slot(*block) if src is None else src, dst_ref=slot(*block),
                send_sem=send_sems.at[k], recv_sem=recv_sems.at[k], device_id=to, device_id_type=MESH)

        mine = pltpu.make_async_copy(x_ref, slot(*me), local_sem)
        mine.start()
        first = [copy(0, me, sibling, src=x_ref)]
        first += [copy(1 + j, me, (*chip, c), src=x_ref) for j, chip in enumerate(chips)]
        for cp in first:
            cp.start()
        passed = [copy(4 + j, (*chip, c), sibling) for j, chip in enumerate(chips)]
        for j, chip in enumerate(chips):
            copy(1 + j, (*chip, c), me).wait_recv()
            passed[j].start()
        copy(0, sibling, me).wait_recv()
        for j, chip in enumerate(chips):
            copy(4 + j, (*chip, 1 - c), me).wait_recv()
        for cp in first + passed:
            cp.wait_send()
        mine.wait()

    return pl.pallas_call(
        body, name=name,
        in_specs=[pl.BlockSpec(memory_space=pl.ANY)], out_specs=pl.BlockSpec(memory_space=pl.ANY),
        out_shape=jax.ShapeDtypeStruct((N_DEV, R, C), blk.dtype),
        scratch_shapes=[pltpu.SemaphoreType.DMA((7,)), pltpu.SemaphoreType.DMA((7,)), pltpu.SemaphoreType.DMA],
        compiler_params=pltpu.CompilerParams(has_side_effects=True),
    )(blk)


def _peers(x, y, c):
    return [(x ^ (k >> 2), y ^ ((k >> 1) & 1), c ^ (k & 1)) for k in range(1, N_DEV)]


_HBM = pl.BlockSpec(memory_space=pltpu.HBM)
_SEM = pl.BlockSpec(memory_space=pltpu.SEMAPHORE)
_EFFECT = pltpu.SideEffectType.DATAFLOW_SIDE_EFFECTING


def _peer_list(x, y, c, near):
    if near:
        return [(x, y, 1 - c), (1 - x, y, c), (x, 1 - y, c), (1 - x, 1 - y, c)]
    return _peers(x, y, c)


def _exchange_start(srcs, *, gather, name, near=False):
    n = len(srcs)
    n_peers = 4 if near else N_DEV - 1
    lands = [lax.empty((N_DEV,) + s.shape[-2:], s.dtype) for s in srcs]

    def body(*refs):
        src_refs, land_refs = refs[:n], refs[n:2 * n]
        send_sems, recv_sems = refs[2 * n], refs[2 * n + 1]
        token = refs[-1]
        x, y, c = lax.axis_index("x"), lax.axis_index("y"), lax.axis_index("c")
        mine = 4 * x + 2 * y + c
        for a in range(n):
            for k, peer in enumerate(_peer_list(x, y, c, near)):
                dest = 4 * peer[0] + 2 * peer[1] + peer[2]
                j = a * n_peers + k
                pltpu.make_async_remote_copy(
                    src_ref=src_refs[a] if gather else src_refs[a].at[dest], dst_ref=land_refs[a].at[mine],
                    send_sem=send_sems.at[j], recv_sem=recv_sems.at[j], device_id=peer, device_id_type=MESH).start()
        token[...] = jnp.zeros_like(token)

    sems = pltpu.SemaphoreType.DMA((n * n_peers,))
    out = pl.pallas_call(
        body, name=name,
        out_shape=(sems, sems) + tuple(pltpu.HBM(a.shape, a.dtype) for a in list(srcs) + lands)
        + (jax.ShapeDtypeStruct((8, 128), F32),),
        in_specs=(_HBM,) * (2 * n), out_specs=(_SEM, _SEM) + (_HBM,) * (2 * n) + (pl.BlockSpec(memory_space=pltpu.VMEM),),
        input_output_aliases={i: 2 + i for i in range(2 * n)},
        compiler_params=pltpu.CompilerParams(has_side_effects=_EFFECT),
    )(*[pltpu.with_memory_space_constraint(a, pltpu.HBM) for a in list(srcs) + lands])
    return out[:-1], out[-1]


def _exchange_wait(state, after, *, gather, name, near=False):
    send_sems, recv_sems = state[0], state[1]
    n = (len(state) - 2) // 2
    n_peers = 4 if near else N_DEV - 1
    arrays = state[2:]

    def body(*refs):
        src_refs, land_refs = refs[:n], refs[n:2 * n]
        send_sems, recv_sems = refs[2 * n], refs[2 * n + 1]
        x, y, c = lax.axis_index("x"), lax.axis_index("y"), lax.axis_index("c")
        for a in range(n):
            for k, peer in enumerate(_peer_list(x, y, c, near)):
                other = 4 * peer[0] + 2 * peer[1] + peer[2]
                j = a * n_peers + k
                copy = pltpu.make_async_remote_copy(
                    src_ref=src_refs[a] if gather else src_refs[a].at[other], dst_ref=land_refs[a].at[other],
                    send_sem=send_sems.at[j], recv_sem=recv_sems.at[j], device_id=peer, device_id_type=MESH)
                copy.wait_send()
                copy.wait_recv()

    out = pl.pallas_call(
        body, name=name,
        out_shape=tuple(pltpu.HBM(a.shape, a.dtype) for a in arrays),
        in_specs=(_HBM,) * (2 * n) + (_SEM, _SEM, pl.BlockSpec(memory_space=pl.ANY)), out_specs=(_HBM,) * (2 * n),
        input_output_aliases={i: i for i in range(2 * n)},
        compiler_params=pltpu.CompilerParams(has_side_effects=_EFFECT),
    )(*arrays, send_sems, recv_sems, after)
    mine = 4 * lax.axis_index("x") + 2 * lax.axis_index("y") + lax.axis_index("c")
    own = out[:n] if gather else [lax.dynamic_index_in_dim(s, mine, 0, keepdims=False) for s in out[:n]]
    return [lax.dynamic_update_slice(g, o[None], (mine, 0, 0)) for g, o in zip(out[n:], own)]


def _forward_start(lands, *, name):
    n = len(lands)

    def body(*refs):
        land_refs, send_sems, recv_sems, token = refs[:n], refs[n], refs[n + 1], refs[-1]
        x, y, c = lax.axis_index("x"), lax.axis_index("y"), lax.axis_index("c")
        for a in range(n):
            for j, (px, py) in enumerate(((1 - x, y), (x, 1 - y), (1 - x, 1 - y))):
                blk = 4 * px + 2 * py + c
                pltpu.make_async_remote_copy(
                    src_ref=land_refs[a].at[blk], dst_ref=land_refs[a].at[blk], send_sem=send_sems.at[3 * a + j],
                    recv_sem=recv_sems.at[3 * a + j], device_id=(x, y, 1 - c), device_id_type=MESH).start()
        token[...] = jnp.zeros_like(token)

    sems = pltpu.SemaphoreType.DMA((3 * n,))
    out = pl.pallas_call(
        body, name=name,
        out_shape=(sems, sems) + tuple(pltpu.HBM(a.shape, a.dtype) for a in lands) + (jax.ShapeDtypeStruct((8, 128), F32),),
        in_specs=(_HBM,) * n, out_specs=(_SEM, _SEM) + (_HBM,) * n + (pl.BlockSpec(memory_space=pltpu.VMEM),),
        input_output_aliases={i: 2 + i for i in range(n)},
        compiler_params=pltpu.CompilerParams(has_side_effects=_EFFECT),
    )(*[pltpu.with_memory_space_constraint(a, pltpu.HBM) for a in lands])
    return out[:-1], out[-1]


def _forward_wait(state, after, *, name):
    send_sems, recv_sems = state[0], state[1]
    lands = state[2:]
    n = len(lands)

    def body(*refs):
        land_refs, send_sems, recv_sems = refs[:n], refs[n], refs[n + 1]
        x, y, c = lax.axis_index("x"), lax.axis_index("y"), lax.axis_index("c")
        for a in range(n):
            for j, (px, py) in enumerate(((1 - x, y), (x, 1 - y), (1 - x, 1 - y))):
                copy = pltpu.make_async_remote_copy(
                    src_ref=land_refs[a].at[4 * px + 2 * py + c], dst_ref=land_refs[a].at[4 * px + 2 * py + 1 - c],
                    send_sem=send_sems.at[3 * a + j], recv_sem=recv_sems.at[3 * a + j], device_id=(x, y, 1 - c),
                    device_id_type=MESH)
                copy.wait_send()
                copy.wait_recv()

    return pl.pallas_call(
        body, name=name,
        out_shape=tuple(pltpu.HBM(a.shape, a.dtype) for a in lands),
        in_specs=(_HBM,) * n + (_SEM, _SEM, pl.BlockSpec(memory_space=pl.ANY)), out_specs=(_HBM,) * n,
        input_output_aliases={i: i for i in range(n)},
        compiler_params=pltpu.CompilerParams(has_side_effects=_EFFECT),
    )(*lands, send_sems, recv_sems, after)


def _adam_math(w, g, m, v):
    m = ADAM_B1 * m + (1.0 - ADAM_B1) * g
    v = ADAM_B2 * v + (1.0 - ADAM_B2) * (g * g)
    m_hat = m / (1.0 - ADAM_B1 ** ADAM_STEP)
    v_hat = v / (1.0 - ADAM_B2 ** ADAM_STEP)
    delta = -ADAM_LR * (m_hat / (jnp.sqrt(v_hat) + ADAM_EPS) + ADAM_WD * w)
    return delta, m, v


def _adamw(parts, w, m, v, *, name):
    R, C = w.shape
    n_parts = parts.shape[0]
    tr = R // 2
    assert tr % 16 == 0

    def body(p_ref, w_ref, m_ref, v_ref, g_ref, d_ref, nm_ref, nv_ref):
        g = p_ref[0].astype(F32)
        for s in range(1, n_parts):
            g = g + p_ref[s].astype(F32)
        d, nm, nv = _adam_math(w_ref[...], g, m_ref[...], v_ref[...])
        g_ref[...] = g
        d_ref[...] = d
        nm_ref[...] = nm
        nv_ref[...] = nv

    blk = pl.BlockSpec((tr, C), lambda i: (i, 0))
    return pl.pallas_call(
        body, name=name, grid=(R // tr,),
        in_specs=[pl.BlockSpec((n_parts, tr, C), lambda i: (0, i, 0)), blk, blk, blk],
        out_specs=[blk] * 4, out_shape=[jax.ShapeDtypeStruct((R, C), F32)] * 4,
        compiler_params=_params(("arbitrary",)),
    )(parts, w, m, v)


def _adamw_small(parts, w, m, v):
    def body(p_ref, w_ref, m_ref, v_ref, g_ref, d_ref, nm_ref, nv_ref):
        g = p_ref[0]
        for s in range(1, N_DEV):
            g = g + p_ref[s]
        d, nm, nv = _adam_math(w_ref[...], g, m_ref[...], v_ref[...])
        g_ref[...] = g
        d_ref[...] = d
        nm_ref[...] = nm
        nv_ref[...] = nv

    vm = pl.BlockSpec(memory_space=pltpu.VMEM)
    return pl.pallas_call(
        body, name="adamw_small", in_specs=[vm] * 4, out_specs=[vm] * 4,
        out_shape=[jax.ShapeDtypeStruct((SMALL_ROWS, 128), F32)] * 4, compiler_params=_params(),
    )(parts, w, m, v)


def _t5_bucket(dist):
    max_exact = REL_BUCKETS // 2
    df = jnp.maximum(dist, 1).astype(F32)
    large = max_exact + (jnp.log(df / max_exact) / math.log(REL_MAX_DISTANCE / max_exact)
                         * (REL_BUCKETS - max_exact)).astype(jnp.int32)
    large = jnp.minimum(large, REL_BUCKETS - 1)
    return jnp.where(dist < max_exact, dist, large)


def _band_tables(rel_table, dil, n_back):
    qi = jnp.arange(BLK)[:, None]
    kj = jnp.arange(2 * BLK)[None, :]
    delta = BLK + qi - kj
    in_band = (delta >= 0) & (delta <= n_back)
    if rel_table is None:
        vals = jnp.zeros((N_HEADS, BLK, 2 * BLK), F32)
        bmap = None
    else:
        bucket = _t5_bucket(jnp.clip(delta, 0, n_back) * dil)
        vals = jnp.zeros((N_HEADS, BLK, 2 * BLK), F32)
        for b in range(REL_BUCKETS):
            vals = jnp.where((bucket == b)[None], rel_table[b][:, None, None], vals)
        bmap = jnp.where(in_band, bucket, -1).astype(jnp.int32)
    later = jnp.where(in_band[None], vals, NEG)
    first = jnp.where((in_band & (kj >= BLK))[None], vals, NEG)
    return jnp.stack([later, first]), bmap


def _rope_tables(T):
    half = HEAD_DIM // 2
    inv_freq = ROPE_THETA ** (-jnp.arange(half, dtype=F32) / half)
    ang = jnp.arange(T, dtype=F32)[:, None] * inv_freq[None, :]
    cos, sin = jnp.cos(ang), jnp.sin(ang)
    return jnp.tile(cos, (1, 4)), jnp.tile(jnp.concatenate([-sin, sin], axis=1), (1, 2))


def _widen_in(a, axis):
    sl = lambda lo, hi: lax.slice_in_dim(a, lo, hi, axis=axis)
    dup = lambda lo: [sl(lo, lo + 64), sl(lo, lo + 64), sl(lo + 64, lo + 128), sl(lo + 64, lo + 128)]
    return jnp.concatenate([sl(0, 512)] + dup(512) + dup(640) + [sl(768, D_IN)], axis=axis)


def _fold_in(a, axis):
    sl = lambda lo, hi: lax.slice_in_dim(a, lo, hi, axis=axis)
    fold = lambda lo: [sl(lo, lo + 64) + sl(lo + 64, lo + 128), sl(lo + 128, lo + 192) + sl(lo + 192, lo + 256)]
    return jnp.concatenate([sl(0, 512)] + fold(512) + fold(768) + [sl(1024, D_INP)], axis=axis)


def _local_step(x, tgt, g_attn, b_in, sinks, rel_table, g_out_a, g_out_b, g_ffn, g_final,
                win_fn, wo_fn, ffn_fn, early_fn):
    T = x.shape[0]
    cos, sin = _rope_tables(T)
    g_final2 = g_final.reshape(1, D_MODEL)
    sink8 = sinks.reshape(N_HEADS)

    bias_a, _ = _band_tables(None, 1, BLK - 1)
    tabs = [_band_tables(rel_table, dil, window // dil) for window, dil in BRANCHES]
    wint, token = win_fn(tabs[2][0])
    winp = _widen_in(wint, 0)
    binp = _widen_in(b_in, 1) + token[0, 0]

    h1, qa, ka, va, *qkv_b = _norm_proj(x, g_attn, winp, binp, cos, sin)
    qbs, kbs, vbs = qkv_b[0:3], qkv_b[3:6], qkv_b[6:9]
    oa, lse_a = _attn_fwd(qa, ka, va, bias_a, sink8, dil=1, kv_pairs=2, use_sink=True, name="attn_a_fwd")
    outs = [_attn_fwd(qbs[n], kbs[n], vbs[n], tabs[n][0], sink8, dil=dil, kv_pairs=4, use_sink=False,
                      name=f"attn_b{n}_fwd") for n, (_, dil) in enumerate(BRANCHES)]
    wo = wo_fn(outs[2][1])
    x2, mixed, h2, *ob_lse = _merge_wo(x, oa, outs[0][0], outs[1][0], outs[2][0], outs[0][1], outs[1][1], outs[2][1],
                                       g_out_a, g_out_b, wo, g_ffn)
    obs, lses = ob_lse[0:3], ob_lse[3:6]
    wgt, wut, wd = ffn_fn(h2)
    gate, up, act = _ffn_up(h2, wgt, wut)
    dx3, dx3b, loss, dg_final = _ffn_down_loss(act, wd, x2, tgt, g_final2)

    dgate, dup, dx2, dx2b, dg_ffn = _ffn_bwd(dx3, gate, up, wd, wgt, wut, x2, g_ffn)
    dwd = _matmul_tn(act, dx3b, tk=1408, tn=1024, name="dw_down")
    dwgt = _matmul_tn(dgate, h2, tk=1408, tn=1024, name="dw_gate")
    dwut = _matmul_tn(dup, h2, tk=1408, tn=1024, name="dw_up")
    dwo = _matmul_tn(mixed, dx2b, tk=1024, tn=1024, name="dw_o")
    early, token2 = early_fn(dict(w_o=dwo, w_gate=dwgt, w_up=dwut, w_down=dwd))
    doa, *dobs, dg_out_a, dg_out_b = _wo_bwd(dx2b, wo, oa, obs[0], g_out_a + token2[0, 0], g_out_b)

    dqa, dka, dva, _, dsk = _attn_bwd(qa, ka, va, oa, doa, lse_a, bias_a, sink8, dil=1, kv_pairs=2, use_sink=True,
                                      name="attn_a_bwd")
    res = [_attn_bwd(qbs[n], kbs[n], vbs[n], obs[n], dobs[n], lses[n], tabs[n][0], sink8, dil=dil, kv_pairs=4,
                     use_sink=False, name=f"attn_b{n}_bwd") for n, (_, dil) in enumerate(BRANCHES)]
    dp, dbp, grad_x, dg_attn = _inproj_bwd(dqa, dka, dva, [r[0] for r in res], [r[1] for r in res],
                                           [r[2] for r in res], cos, sin, winp, x, dx2, g_attn)
    dwin = _fold_in(_matmul_tn(dp, h1, tk=1280, tn=1024, out_dtype=F32, name="dw_in"), 0)
    drel, dsink = _bias_sink_grads([r[3] for r in res], [t[1] for t in tabs], dsk)

    small = dict(
        g_attn=dg_attn, b_in=_fold_in(dbp, 1), sinks=dsink[:, ::HEAD_DIM], rel_table=drel[:, :REL_BUCKETS].T,
        g_out_a=dg_out_a, g_out_b=dg_out_b, g_ffn=dg_ffn, g_final=dg_final.reshape(D_MODEL))
    return loss[0, 0], grad_x, dwin, early, small


SMALL_NAMES = ("g_attn", "b_in", "sinks", "rel_table", "g_out_a", "g_out_b", "g_ffn", "g_final")


def _pack_small(vals):
    flat = jnp.concatenate([vals[n].reshape(-1).astype(F32) for n in SMALL_NAMES])
    return jnp.pad(flat, (0, SMALL_ROWS * 128 - flat.shape[0])).reshape(SMALL_ROWS, 128)


def _unpack_small(packed, like):
    flat = packed.reshape(-1)
    out, off = {}, 0
    for n in SMALL_NAMES:
        size = like[n].size
        out[n] = flat[off:off + size].reshape(like[n].shape)
        off += size
    return out


def kernel(x, g_attn, w_in, b_in, sinks, rel_table, g_out_a, g_out_b, w_o, g_ffn, w_gate, w_up, w_down, g_final, loss_target, m_g_attn, m_w_in, m_b_in, m_sinks, m_rel_table, m_g_out_a, m_g_out_b, m_w_o, m_g_ffn, m_w_gate, m_w_up, m_w_down, m_g_final, v_g_attn, v_w_in, v_b_in, v_sinks, v_rel_table, v_g_out_a, v_g_out_b, v_w_o, v_g_ffn, v_w_gate, v_w_up, v_w_down, v_g_final):
    rest_names = ("w_o", "w_gate", "w_up", "w_down")

    rest = [w_o[0].astype(BF16), w_gate[0].astype(BF16).T, w_up[0].astype(BF16).T, w_down[0].astype(BF16)]
    in_state, _ = _exchange_start([w_in[0].astype(BF16).T], gather=True, near=True, name="gather_w_in_start")
    later = {}

    def whole(got):
        return [g.reshape(N_DEV * g.shape[1], D_MODEL) for g in got]

    def win_fn(after):
        near = _exchange_wait(in_state, after, gather=True, near=True, name="gather_w_in_near")
        fwd_state, tok = _forward_start(near, name="gather_w_in_forward")
        wint = whole(_forward_wait(fwd_state, tok, name="gather_w_in_wait"))[0]
        wint, src = lax.optimization_barrier((wint, rest))
        later["wo"], token_o = _exchange_start(src[:1], gather=True, name="gather_w_o_start")
        token_o, ffn_src = lax.optimization_barrier((token_o, src[1:]))
        later["ffn"], token = _exchange_start(ffn_src, gather=True, name="gather_ffn_start")
        return wint, token + token_o

    def wo_fn(after):
        return whole(_exchange_wait(later["wo"], after, gather=True, name="gather_w_o_wait"))[0]

    def ffn_fn(after):
        return whole(_exchange_wait(later["ffn"], after, gather=True, name="gather_ffn_wait"))

    def early_fn(dws):
        return _exchange_start([dws[n].reshape(N_DEV, -1, D_MODEL) for n in rest_names], gather=False,
                               name="scatter_rest_start")

    loss_part, grad_x, dwint, early_state, small = _local_step(
        x[0], loss_target[0], g_attn, b_in, sinks, rel_table, g_out_a, g_out_b, g_ffn, g_final,
        win_fn, wo_fn, ffn_fn, early_fn)
    loss = lax.psum(loss_part, ("x", "y", "c"))

    parts_in = dwint.astype(BF16).reshape(N_DEV, D_IN // N_DEV, D_MODEL)
    in_state, token3 = _exchange_start([parts_in], gather=False, name="scatter_w_in_start")
    got = _exchange_wait(early_state, token3, gather=False, name="scatter_rest_wait")

    def update(n, parts, w, m, v, transposed):
        if transposed:
            return [a.T[None] for a in _adamw(parts, w[0].T, m[0].T, v[0].T, name="adamw_" + n)]
        return [a[None] for a in _adamw(parts, w[0], m[0], v[0], name="adamw_" + n)]

    big = dict(w_o=update("w_o", got[0], w_o, m_w_o, v_w_o, False),
               w_gate=update("w_gate", got[1], w_gate, m_w_gate, v_w_gate, True),
               w_up=update("w_up", got[2], w_up, m_w_up, v_w_up, True),
               w_down=update("w_down", got[3], w_down, m_w_down, v_w_down, False))

    ws = dict(g_attn=g_attn, b_in=b_in, sinks=sinks, rel_table=rel_table, g_out_a=g_out_a, g_out_b=g_out_b,
              g_ffn=g_ffn, g_final=g_final)
    ms = dict(g_attn=m_g_attn, b_in=m_b_in, sinks=m_sinks, rel_table=m_rel_table, g_out_a=m_g_out_a,
              g_out_b=m_g_out_b, g_ffn=m_g_ffn, g_final=m_g_final)
    vs = dict(g_attn=v_g_attn, b_in=v_b_in, sinks=v_sinks, rel_table=v_rel_table, g_out_a=v_g_out_a,
              g_out_b=v_g_out_b, g_ffn=v_g_ffn, g_final=v_g_final)
    sparts = _all_gather(_pack_small(small), name="gather_small")
    sm_packed = _adamw_small(sparts, _pack_small(ws), _pack_small(ms), _pack_small(vs))
    sm = [_unpack_small(a, ws) for a in sm_packed]

    done = sm_packed[1][:1, :1] + sum(big[n][1][0, :1, :1] for n in rest_names)
    got_in = _exchange_wait(in_state, done, gather=False, name="scatter_w_in_wait")[0]
    big["w_in"] = update("w_in", got_in, w_in, m_w_in, v_w_in, True)

    order = ("g_attn", "w_in", "b_in", "sinks", "rel_table", "g_out_a", "g_out_b", "w_o", "g_ffn", "w_gate", "w_up",
             "w_down", "g_final")
    outs = [loss, grad_x[None]]
    for k in range(4):
        outs += [big[n][k] if n in big else sm[k][n] for n in order]
    return tuple(outs)
```

```python
import functools
import math

import jax
import jax.numpy as jnp
from jax import lax
from jax.experimental import pallas as pl
from jax.experimental.pallas import tpu as pltpu

F32 = jnp.float32
BF16 = jnp.bfloat16

N_DEV = 8
D_MODEL = 1024
HEAD_DIM = 64
N_HEADS = 8
PAIR = 2 * HEAD_DIM
WIDTH = N_HEADS * HEAD_DIM
D_IN = 2304
D_INP = 2560
D_FF = 2816
BLK = 128
ROPE_THETA = 150000.0
REL_BUCKETS = 32
REL_MAX_DISTANCE = 2048
EPS = 1e-5
NEG = -1e30
BRANCHES = ((128, 1), (512, 4), (2048, 16))
Q_SCALE = HEAD_DIM ** -0.5

ADAM_LR = 0.001
ADAM_B1 = 0.9
ADAM_B2 = 0.999
ADAM_EPS = 1e-08
ADAM_WD = 0.01
ADAM_STEP = 10

VMEM_LIMIT = 56 * 1024 * 1024
MESH = pl.DeviceIdType.MESH

NT = (((1,), (1,)), ((), ()))
TN = (((0,), (0,)), ((), ()))

SMALL_ROWS = 56


def _params(sem=None):
    return pltpu.CompilerParams(dimension_semantics=sem, vmem_limit_bytes=VMEM_LIMIT)


def _sigmoid(x):
    return 1.0 / (1.0 + jnp.exp2(x * (-1.0 / math.log(2.0))))


def _rms_bwd(dh, xh, r, g):
    u = dh * g
    return r * (u - xh * jnp.mean(u * xh, axis=-1, keepdims=True))


def _rope_rot(t, first):
    return jnp.where(first, pltpu.roll(t, 96, 1), pltpu.roll(t, 32, 1))


N_CHUNK = WIDTH // PAIR


def _scr(tm):
    return pltpu.VMEM((N_CHUNK, tm, PAIR), F32)


def _scr_get(scr):
    return jnp.concatenate([scr[j] for j in range(N_CHUNK)], axis=1)


def _scr_put(scr, val):
    for j in range(N_CHUNK):
        scr[j] = val[:, j * PAIR:(j + 1) * PAIR]


def _unstride(view_ref, scr, dil, tm):
    n = tm // dil
    chunks = scr.shape[0]
    for r in range(dil):
        for j in range(chunks):
            col = (r * chunks + j) * PAIR
            scr.at[j][pl.ds(r, n, stride=dil), :] = view_ref[:, col:col + PAIR].astype(F32)


def _restride(scr, out_ref, dil, tm):
    n = tm // dil
    chunks = scr.shape[0]
    for r in range(dil):
        for j in range(chunks):
            col = (r * chunks + j) * PAIR
            rows = scr[j] if dil == 1 else scr.at[j][pl.ds(r, n, stride=dil), :]
            out_ref[:, col:col + PAIR] = rows.astype(out_ref.dtype)


def _view_specs(tm, width=WIDTH):
    return [pl.BlockSpec((tm // dil, dil * width), lambda i: (i, 0)) for _, dil in BRANCHES]


def _view_shapes(T, dtype, width=WIDTH):
    return [jax.ShapeDtypeStruct((T // dil, dil * width), dtype) for _, dil in BRANCHES]


def _norm_proj(x, g, w, b, cos, sin, *, tm=512):
    T = x.shape[0]

    def body(x_ref, g_ref, w_ref, b_ref, cos_ref, sin_ref, h_ref, qa_ref, ka_ref, va_ref, *rest):
        outs_b, ys = rest[:9], rest[9]
        xv = x_ref[...]
        r = lax.rsqrt(jnp.mean(xv * xv, axis=-1, keepdims=True) + EPS)
        h = (xv * r * g_ref[...]).astype(BF16)
        h_ref[...] = h
        cosv = cos_ref[...]
        sinv = sin_ref[...]
        lane = lax.broadcasted_iota(jnp.int32, (tm, PAIR), 1)
        first = (lane % HEAD_DIM) < (HEAD_DIM // 2)

        def proj(off):
            return (lax.dot_general(h, w_ref[off:off + 256, :], NT, preferred_element_type=F32)
                    + b_ref[:, off:off + 256])

        for (off, width, rot, scale), o_ref in zip(((0, 512, True, Q_SCALE), (512, 256, True, 1.0), (768, 256, False, 1.0)),
                                                   (qa_ref, ka_ref, va_ref)):
            for c in range(0, width, 256):
                y = proj(off + c)
                for j in range(0, 256, PAIR):
                    t = y[:, j:j + PAIR]
                    if rot:
                        t = t * cosv + _rope_rot(t, first) * sinv
                    if scale != 1.0:
                        t = t * scale
                    o_ref[:, c + j:c + j + PAIR] = t.astype(BF16)
        for n, (off, scale) in enumerate(((1024, Q_SCALE), (1536, 1.0), (2048, 1.0))):
            for c in range(0, WIDTH, 256):
                y = proj(off + c)
                y = y * scale if scale != 1.0 else y
                for j in range(0, 256, PAIR):
                    ys[(c + j) // PAIR] = y[:, j:j + PAIR]
            for (_, dil), o_ref in zip(BRANCHES, outs_b[3 * n:3 * n + 3]):
                _restride(ys, o_ref, dil, tm)

    row = lambda w_: pl.BlockSpec((tm, w_), lambda i: (i, 0))
    full = lambda a: pl.BlockSpec(a.shape, lambda i: (0, 0))
    return pl.pallas_call(
        body, name="norm_proj", grid=(T // tm,),
        in_specs=[row(D_MODEL), full(g), full(w), full(b), row(PAIR), row(PAIR)],
        out_specs=[row(D_MODEL), row(512), row(256), row(256)] + _view_specs(tm) * 3,
        out_shape=[jax.ShapeDtypeStruct((T, n), BF16) for n in (D_MODEL, 512, 256, 256)] + _view_shapes(T, BF16) * 3,
        scratch_shapes=[_scr(tm)],
        compiler_params=_params(("arbitrary",)),
    )(x, g, w, b, cos, sin)


SUB = 4
AHEAD = 2


def _attn_specs(kvw):
    q_spec = pl.BlockSpec((SUB * BLK, WIDTH), lambda r, i: (i, r))
    kc_spec = pl.BlockSpec((SUB * BLK, kvw), lambda r, i: (i, r))
    kp_spec = pl.BlockSpec((BLK, kvw), lambda r, i: (jnp.maximum(SUB * i - 1, 0), r))
    b_spec = pl.BlockSpec((2, N_HEADS, BLK, 2 * BLK), lambda r, i: (0, 0, 0, 0))
    return q_spec, kp_spec, kc_spec, b_spec


def _window(prev_ref, cur_ref, j, ksl):
    before = prev_ref[:, ksl] if j == 0 else cur_ref[(j - 1) * BLK:j * BLK, ksl]
    return jnp.concatenate([before, cur_ref[j * BLK:(j + 1) * BLK, ksl]], axis=0)


def _attn_fwd(q, k, v, bias, sinks, *, dil, kv_pairs, use_sink, name):
    L = q.shape[0]
    ns = L // (SUB * BLK)
    kvw = kv_pairs * PAIR
    rep = 4 // kv_pairs

    def body(sink_ref, q_ref, kp_ref, kc_ref, vp_ref, vc_ref, b_ref, o_ref, lse_ref):
        lane = lax.broadcasted_iota(jnp.int32, (1, PAIR), 1)
        lo = lane < HEAD_DIM
        first = jnp.where(pl.program_id(1) == 0, 1, 0)
        def scores(j, hp):
            rows = slice(j * BLK, (j + 1) * BLK)
            sl = slice(hp * PAIR, (hp + 1) * PAIR)
            ksl = slice((hp // rep) * PAIR, (hp // rep + 1) * PAIR)
            qp = q_ref[rows, sl]
            kk = _window(kp_ref, kc_ref, j, ksl)
            vv = _window(vp_ref, vc_ref, j, ksl)
            heads = []
            for e in range(2):
                h = 2 * hp + e
                msk = lo if e == 0 else jnp.logical_not(lo)
                qm = jnp.where(msk, qp, jnp.zeros_like(qp))
                s = lax.dot_general(qm, kk, NT, preferred_element_type=F32) + (b_ref[first, h] if j == 0 else b_ref[0, h])
                heads.append((h, msk, s))
            return rows, sl, vv, heads

        def outputs(rows, sl, vv, heads):
            o_pair = None
            lse_pair = None
            for h, msk, s in heads:
                m = jnp.max(s, axis=-1, keepdims=True)
                if use_sink:
                    sk = sink_ref[h]
                    m = jnp.maximum(m, sk)
                p = jnp.exp(s - m)
                l = jnp.sum(p, axis=-1, keepdims=True)
                if use_sink:
                    l = l + jnp.exp(sk - m)
                vm = jnp.where(msk, vv, jnp.zeros_like(vv))
                oe = jnp.dot(p.astype(BF16), vm, preferred_element_type=F32) * (1.0 / l)
                ls = m + jnp.log(l)
                if o_pair is None:
                    o_pair = oe
                    lse_pair = jnp.broadcast_to(ls, (BLK, PAIR))
                else:
                    o_pair = o_pair + oe
                    lse_pair = jnp.where(lo, lse_pair, ls)
            o_ref[rows, sl] = o_pair.astype(BF16)
            lse_ref[rows, sl] = lse_pair

        items = [(j, hp) for j in range(SUB) for hp in range(4)]
        queue = [scores(*it) for it in items[:AHEAD]]
        for n in range(len(items)):
            if n + AHEAD < len(items):
                queue.append(scores(*items[n + AHEAD]))
            outputs(*queue.pop(0))

    q_spec, kp_spec, kc_spec, b_spec = _attn_specs(kvw)
    return pl.pallas_call(
        body, name=name, grid=(dil, ns),
        in_specs=[pl.BlockSpec(memory_space=pltpu.SMEM), q_spec, kp_spec, kc_spec, kp_spec, kc_spec, b_spec],
        out_specs=[q_spec, q_spec],
        out_shape=[jax.ShapeDtypeStruct((L, dil * WIDTH), BF16), jax.ShapeDtypeStruct((L, dil * WIDTH), F32)],
        compiler_params=_params(("arbitrary", "arbitrary")),
    )(sinks, q, k, k, v, v, bias)


def _attn_bwd(q, k, v, o, do, lse, bias, sinks, *, dil, kv_pairs, use_sink, name):
    L = q.shape[0]
    ns = L // (SUB * BLK)
    n_steps = dil * ns
    kvw = kv_pairs * PAIR
    rep = 4 // kv_pairs
    last = slice((SUB - 1) * BLK, SUB * BLK)

    def body(sink_ref, q_ref, kp_ref, kc_ref, vp_ref, vc_ref, o_ref, do_ref, lse_ref, b_ref,
             dq_ref, dk_ref, dv_ref, dsum_ref, dsk_ref, pk_ref, pv_ref):
        t = pl.program_id(0)
        i = t % ns

        @pl.when(t == 0)
        def _():
            dsum_ref[...] = jnp.zeros_like(dsum_ref)
            dsk_ref[...] = jnp.zeros_like(dsk_ref)
            pk_ref[...] = jnp.zeros_like(pk_ref)
            pv_ref[...] = jnp.zeros_like(pv_ref)

        @pl.when(t < n_steps)
        def _():
            lo = lax.broadcasted_iota(jnp.int32, (1, PAIR), 1) < HEAD_DIM
            first = jnp.where(i == 0, 1, 0)
            dks = [[None] * kv_pairs for _ in range(SUB)]
            dvs = [[None] * kv_pairs for _ in range(SUB)]
            def scores(j, hp):
                rows = slice(j * BLK, (j + 1) * BLK)
                kvp = hp // rep
                sl = slice(hp * PAIR, (hp + 1) * PAIR)
                ksl = slice(kvp * PAIR, (kvp + 1) * PAIR)
                qp = q_ref[rows, sl]
                dop = do_ref[rows, sl]
                prod = dop.astype(F32) * o_ref[rows, sl].astype(F32)
                kk = _window(kp_ref, kc_ref, j, ksl)
                vv = _window(vp_ref, vc_ref, j, ksl)
                heads = []
                for e in range(2):
                    h = 2 * hp + e
                    msk = lo if e == 0 else jnp.logical_not(lo)
                    qm = jnp.where(msk, qp, jnp.zeros_like(qp))
                    dom = jnp.where(msk, dop, jnp.zeros_like(dop))
                    km = jnp.where(msk, kk, jnp.zeros_like(kk))
                    s = (lax.dot_general(qm, kk, NT, preferred_element_type=F32)
                         + (b_ref[first, h] if j == 0 else b_ref[0, h]))
                    dp = lax.dot_general(dom, vv, NT, preferred_element_type=F32)
                    heads.append((h, msk, qm, dom, km, s, dp))
                return j, rows, kvp, sl, prod, heads

            def grads(j, rows, kvp, sl, prod, heads):
                dq_pair = None
                c_pair = None
                qms, doms, dsbs, pbs = [], [], [], []
                for h, msk, qm, dom, km, s, dp in heads:
                    ls = lse_ref[rows, h * HEAD_DIM:h * HEAD_DIM + 1]
                    p = jnp.exp(s - ls)
                    delta = jnp.sum(jnp.where(msk, prod, 0.0), axis=-1, keepdims=True)
                    ds = p * (dp - delta)
                    if use_sink:
                        ce = jnp.exp(sink_ref[h] - ls) * delta
                        c_pair = jnp.broadcast_to(ce, (BLK, PAIR)) if c_pair is None else jnp.where(msk, ce, c_pair)
                    else:
                        dsum_ref[h] += ds
                    dsb = ds.astype(BF16)
                    dqe = jnp.dot(dsb, km, preferred_element_type=F32)
                    dq_pair = dqe if dq_pair is None else dq_pair + dqe
                    qms.append(qm)
                    doms.append(dom)
                    dsbs.append(dsb)
                    pbs.append(p.astype(BF16))
                dke = lax.dot_general(jnp.concatenate(dsbs, axis=0), jnp.concatenate(qms, axis=0), TN,
                                      preferred_element_type=F32)
                dve = lax.dot_general(jnp.concatenate(pbs, axis=0), jnp.concatenate(doms, axis=0), TN,
                                      preferred_element_type=F32)
                dks[j][kvp] = dke if dks[j][kvp] is None else dks[j][kvp] + dke
                dvs[j][kvp] = dve if dvs[j][kvp] is None else dvs[j][kvp] + dve
                dq_ref[rows, sl] = (dq_pair * Q_SCALE).astype(BF16)
                if use_sink:
                    dsk_ref[:, sl] += c_pair

            items = [(j, hp) for j in range(SUB) for hp in range(4)]
            ahead = AHEAD + 1 if use_sink else AHEAD
            queue = [scores(*it) for it in items[:ahead]]
            for n in range(len(items)):
                if n + ahead < len(items):
                    queue.append(scores(*items[n + ahead]))
                grads(*queue.pop(0))
            for kvp in range(kv_pairs):
                ksl = slice(kvp * PAIR, (kvp + 1) * PAIR)
                for pend_ref, out_ref, parts in ((pk_ref, dk_ref, [d[kvp] for d in dks]),
                                                 (pv_ref, dv_ref, [d[kvp] for d in dvs])):
                    if SUB > 1:
                        out_ref[:(SUB - 1) * BLK, ksl] = pend_ref[:(SUB - 1) * BLK, ksl].astype(BF16)
                    out_ref[last, ksl] = (pend_ref[last, ksl] + parts[0][:BLK]).astype(BF16)
                    for j in range(SUB):
                        own = parts[j][BLK:]
                        pend_ref[j * BLK:(j + 1) * BLK, ksl] = own + parts[j + 1][:BLK] if j + 1 < SUB else own

        @pl.when(t == n_steps)
        def _():
            dk_ref[...] = pk_ref[...].astype(BF16)
            dv_ref[...] = pv_ref[...].astype(BF16)

    def at(t):
        t = jnp.minimum(t, n_steps - 1)
        return t % ns, t // ns

    def before(t):
        return at(jnp.maximum(t - 1, 0))

    q_spec = pl.BlockSpec((SUB * BLK, WIDTH), at)
    kc_spec = pl.BlockSpec((SUB * BLK, kvw), at)
    kp_spec = pl.BlockSpec((BLK, kvw), lambda t: (jnp.maximum(SUB * at(t)[0] - 1, 0), at(t)[1]))
    b_spec = pl.BlockSpec((2, N_HEADS, BLK, 2 * BLK), lambda t: (0, 0, 0, 0))
    dkv_spec = pl.BlockSpec((SUB * BLK, kvw), before)
    return pl.pallas_call(
        body, name=name, grid=(n_steps + 1,),
        in_specs=[pl.BlockSpec(memory_space=pltpu.SMEM), q_spec, kp_spec, kc_spec, kp_spec, kc_spec,
                  q_spec, q_spec, q_spec, b_spec],
        out_specs=[q_spec, dkv_spec, dkv_spec,
                   pl.BlockSpec((N_HEADS, BLK, 2 * BLK), lambda t: (0, 0, 0)),
                   pl.BlockSpec((BLK, WIDTH), lambda t: (0, 0))],
        out_shape=[jax.ShapeDtypeStruct((L, dil * WIDTH), BF16),
                   jax.ShapeDtypeStruct((L, dil * kvw), BF16),
                   jax.ShapeDtypeStruct((L, dil * kvw), BF16),
                   jax.ShapeDtypeStruct((N_HEADS, BLK, 2 * BLK), F32),
                   jax.ShapeDtypeStruct((BLK, WIDTH), F32)],
        scratch_shapes=[pltpu.VMEM((SUB * BLK, kvw), F32), pltpu.VMEM((SUB * BLK, kvw), F32)],
        compiler_params=_params(("arbitrary",)),
    )(sinks, q, k, k, v, v, o, do, lse, bias)


def _merge_wo(x, oa, o1, o2, o3, l1, l2, l3, ga, gb, wo, gf, *, tm=512):
    T = x.shape[0]

    def body(x_ref, oa_ref, o1_ref, o2_ref, o3_ref, l1_ref, l2_ref, l3_ref, ga_ref, gb_ref, wo_ref, gf_ref,
             x2_ref, mix_ref, h2_ref, ob1_ref, ob4_ref, ob16_ref, ls1_ref, ls4_ref, ls16_ref, so2, so3, sl2, sl3):
        _unstride(o2_ref, so2, BRANCHES[1][1], tm)
        _unstride(o3_ref, so3, BRANCHES[2][1], tm)
        _unstride(l2_ref, sl2, BRANCHES[1][1], tm)
        _unstride(l3_ref, sl3, BRANCHES[2][1], tm)
        la, lb, lc = l1_ref[...], _scr_get(sl2), _scr_get(sl3)
        m = jnp.maximum(jnp.maximum(la, lb), lc)
        ea, eb, ec = jnp.exp(la - m), jnp.exp(lb - m), jnp.exp(lc - m)
        den = ea + eb + ec
        inv = 1.0 / den
        ob = (ea * o1_ref[...].astype(F32) + eb * _scr_get(so2) + ec * _scr_get(so3)) * inv
        _scr_put(so2, ob)
        _scr_put(sl2, m + jnp.log(den))
        for (_, dil), o_ref, l_ref in zip(BRANCHES, (ob1_ref, ob4_ref, ob16_ref), (ls1_ref, ls4_ref, ls16_ref)):
            _restride(so2, o_ref, dil, tm)
            _restride(sl2, l_ref, dil, tm)
        oav = oa_ref[...].astype(F32)
        ra = lax.rsqrt(jnp.mean(oav * oav, axis=-1, keepdims=True) + EPS)
        rb = lax.rsqrt(jnp.mean(ob * ob, axis=-1, keepdims=True) + EPS)
        mix_ref[:, :WIDTH] = (oav * ra * ga_ref[...]).astype(BF16)
        mix_ref[:, WIDTH:] = (ob * rb * gb_ref[...]).astype(BF16)
        x2 = x_ref[...] + jnp.dot(mix_ref[...], wo_ref[...], preferred_element_type=F32)
        x2_ref[...] = x2
        r2 = lax.rsqrt(jnp.mean(x2 * x2, axis=-1, keepdims=True) + EPS)
        h2_ref[...] = (x2 * r2 * gf_ref[...]).astype(BF16)

    row = lambda w_: pl.BlockSpec((tm, w_), lambda i: (i, 0))
    full = lambda a: pl.BlockSpec(a.shape, lambda i: (0, 0))
    return pl.pallas_call(
        body, name="merge_wo", grid=(T // tm,),
        in_specs=[row(D_MODEL), row(WIDTH)] + _view_specs(tm) * 2 + [full(ga), full(gb), full(wo), full(gf)],
        out_specs=[row(D_MODEL), row(D_MODEL), row(D_MODEL)] + _view_specs(tm) * 2,
        out_shape=[jax.ShapeDtypeStruct((T, D_MODEL), F32), jax.ShapeDtypeStruct((T, D_MODEL), BF16),
                   jax.ShapeDtypeStruct((T, D_MODEL), BF16)] + _view_shapes(T, BF16) + _view_shapes(T, F32),
        scratch_shapes=[_scr(tm)] * 4,
        compiler_params=_params(("arbitrary",)),
    )(x, oa, o1, o2, o3, l1, l2, l3, ga, gb, wo, gf)


def _ffn_up(h2, wgt, wut, *, tm=512, fc=D_FF, rc=512, cc=256):
    T = h2.shape[0]

    def body(h_ref, wg_ref, wu_ref, gate_ref, up_ref, act_ref):
        for s in range(0, tm, rc):
            h = h_ref[s:s + rc, :]
            for c in range(0, fc, cc):
                gt = lax.dot_general(h, wg_ref[c:c + cc, :], NT, preferred_element_type=F32)
                u = lax.dot_general(h, wu_ref[c:c + cc, :], NT, preferred_element_type=F32)
                gate_ref[s:s + rc, c:c + cc] = gt.astype(BF16)
                up_ref[s:s + rc, c:c + cc] = u.astype(BF16)
                act_ref[s:s + rc, c:c + cc] = (gt * _sigmoid(gt) * u).astype(BF16)

    rowd = pl.BlockSpec((tm, D_MODEL), lambda i, c: (i, 0))
    wrow = pl.BlockSpec((fc, D_MODEL), lambda i, c: (c, 0))
    oc = pl.BlockSpec((tm, fc), lambda i, c: (i, c))
    return pl.pallas_call(
        body, name="ffn_up", grid=(T // tm, D_FF // fc),
        in_specs=[rowd, wrow, wrow],
        out_specs=[oc, oc, oc],
        out_shape=[jax.ShapeDtypeStruct((T, D_FF), BF16)] * 3,
        compiler_params=_params(("arbitrary", "arbitrary")),
    )(h2, wgt, wut)


def _ffn_down_loss(act, wd, x2, tgt, g, *, tm=512, rc=256):
    T = x2.shape[0]

    def body(act_ref, wd_ref, x2_ref, tgt_ref, g_ref, dx_ref, dxb_ref, loss_ref, dg_ref):
        @pl.when(pl.program_id(0) == 0)
        def _():
            loss_ref[...] = jnp.zeros_like(loss_ref)
            dg_ref[...] = jnp.zeros_like(dg_ref)

        gv = g_ref[...]
        lsum = jnp.zeros((1, 1), F32)
        dgs = jnp.zeros((1, D_MODEL), F32)
        for c in range(0, tm, rc):
            x3 = x2_ref[c:c + rc, :] + jnp.dot(act_ref[c:c + rc, :], wd_ref[...], preferred_element_type=F32)
            r = lax.rsqrt(jnp.mean(x3 * x3, axis=-1, keepdims=True) + EPS)
            xh = x3 * r
            diff = xh * gv - tgt_ref[c:c + rc, :]
            lsum = lsum + jnp.sum(jnp.sum(diff * diff, axis=-1, keepdims=True), axis=0, keepdims=True)
            dy = diff * (1.0 / D_MODEL)
            dgs = dgs + jnp.sum(dy * xh, axis=0, keepdims=True)
            dx = _rms_bwd(dy, xh, r, gv)
            dx_ref[c:c + rc, :] = dx
            dxb_ref[c:c + rc, :] = dx.astype(BF16)
        loss_ref[...] += lsum * (0.5 / D_MODEL)
        dg_ref[...] += dgs

    rowd = pl.BlockSpec((tm, D_MODEL), lambda i: (i, 0))
    return pl.pallas_call(
        body, name="ffn_down_loss", grid=(T // tm,),
        in_specs=[pl.BlockSpec((tm, D_FF), lambda i: (i, 0)), pl.BlockSpec((D_FF, D_MODEL), lambda i: (0, 0)),
                  rowd, rowd, pl.BlockSpec(g.shape, lambda i: (0, 0))],
        out_specs=[rowd, rowd, pl.BlockSpec((1, 1), lambda i: (0, 0)), pl.BlockSpec((1, D_MODEL), lambda i: (0, 0))],
        out_shape=[jax.ShapeDtypeStruct((T, D_MODEL), F32), jax.ShapeDtypeStruct((T, D_MODEL), BF16),
                   jax.ShapeDtypeStruct((1, 1), F32), jax.ShapeDtypeStruct((1, D_MODEL), F32)],
        compiler_params=_params(("arbitrary",)),
    )(act, wd, x2, tgt, g)


def _ffn_bwd(dx3, gate, up, wd, wgt, wut, x2, g, *, tm=256, cc=256):
    T = x2.shape[0]

    def body(dx_ref, gate_ref, up_ref, wd_ref, wg_ref, wu_ref, x2_ref, g_ref,
             dgate_ref, dup_ref, dx2_ref, dx2b_ref, dg_ref):
        @pl.when(pl.program_id(0) == 0)
        def _():
            dg_ref[...] = jnp.zeros_like(dg_ref)

        dxb = dx_ref[...].astype(BF16)
        for c in range(0, D_FF, cc):
            dact = lax.dot_general(dxb, wd_ref[c:c + cc, :], NT, preferred_element_type=F32)
            gt = gate_ref[:, c:c + cc].astype(F32)
            u = up_ref[:, c:c + cc].astype(F32)
            sg = _sigmoid(gt)
            a = dact * sg
            dgate_ref[:, c:c + cc] = (a * u * ((1.0 + gt) - gt * sg)).astype(BF16)
            dup_ref[:, c:c + cc] = (a * gt).astype(BF16)
        dh = (jnp.dot(dgate_ref[...], wg_ref[...], preferred_element_type=F32)
              + jnp.dot(dup_ref[...], wu_ref[...], preferred_element_type=F32))
        xv = x2_ref[...]
        r = lax.rsqrt(jnp.mean(xv * xv, axis=-1, keepdims=True) + EPS)
        xh = xv * r
        dg_ref[...] += jnp.sum(dh * xh, axis=0, keepdims=True)
        d = dx_ref[...] + _rms_bwd(dh, xh, r, g_ref[...])
        dx2_ref[...] = d
        dx2b_ref[...] = d.astype(BF16)

    rowd = pl.BlockSpec((tm, D_MODEL), lambda i: (i, 0))
    rowf = pl.BlockSpec((tm, D_FF), lambda i: (i, 0))
    wfull = pl.BlockSpec((D_FF, D_MODEL), lambda i: (0, 0), pipeline_mode=pl.Buffered(1))
    return pl.pallas_call(
        body, name="ffn_bwd", grid=(T // tm,),
        in_specs=[rowd, rowf, rowf, wfull, wfull, wfull, rowd, pl.BlockSpec(g.shape, lambda i: (0, 0))],
        out_specs=[rowf, rowf, rowd, rowd, pl.BlockSpec((1, D_MODEL), lambda i: (0, 0))],
        out_shape=[jax.ShapeDtypeStruct((T, D_FF), BF16), jax.ShapeDtypeStruct((T, D_FF), BF16),
                   jax.ShapeDtypeStruct((T, D_MODEL), F32), jax.ShapeDtypeStruct((T, D_MODEL), BF16),
                   jax.ShapeDtypeStruct((1, D_MODEL), F32)],
        compiler_params=_params(("arbitrary",)),
    )(dx3, gate, up, wd, wgt, wut, x2, g)


def _matmul_tn(a, b, *, tk, tn, tt=2048, out_dtype=BF16, name):
    T, K = a.shape
    N = b.shape[1]
    nt = T // tt

    def body(a_ref, b_ref, o_ref, acc_ref):
        part = lax.dot_general(a_ref[...], b_ref[...], TN, preferred_element_type=F32)

        @pl.when(pl.program_id(2) == 0)
        def _():
            acc_ref[...] = part

        @pl.when(pl.program_id(2) > 0)
        def _():
            acc_ref[...] += part

        @pl.when(pl.program_id(2) == nt - 1)
        def _():
            o_ref[...] = acc_ref[...].astype(out_dtype)

    return pl.pallas_call(
        body, name=name, grid=(K // tk, N // tn, nt),
        in_specs=[pl.BlockSpec((tt, tk), lambda i, j, t: (t, i)), pl.BlockSpec((tt, tn), lambda i, j, t: (t, j))],
        out_specs=pl.BlockSpec((tk, tn), lambda i, j, t: (i, j)),
        out_shape=jax.ShapeDtypeStruct((K, N), out_dtype),
        scratch_shapes=[pltpu.VMEM((tk, tn), F32)],
        compiler_params=_params(("arbitrary", "arbitrary", "arbitrary")),
    )(a, b)


def _wo_bwd(dx2b, wo, oa, ob, ga, gb, *, tm=512):
    T = dx2b.shape[0]

    def body(dx_ref, wo_ref, oa_ref, ob_ref, ga_ref, gb_ref, doa_ref, dob1_ref, dob4_ref, dob16_ref, dga_ref, dgb_ref, scr):
        @pl.when(pl.program_id(0) == 0)
        def _():
            dga_ref[...] = jnp.zeros_like(dga_ref)
            dgb_ref[...] = jnp.zeros_like(dgb_ref)

        dm = lax.dot_general(dx_ref[...], wo_ref[...], NT, preferred_element_type=F32)
        for o_ref, g_ref, dg_ref, sl in ((oa_ref, ga_ref, dga_ref, slice(0, WIDTH)),
                                         (ob_ref, gb_ref, dgb_ref, slice(WIDTH, 2 * WIDTH))):
            ov = o_ref[...].astype(F32)
            r = lax.rsqrt(jnp.mean(ov * ov, axis=-1, keepdims=True) + EPS)
            xh = ov * r
            d = dm[:, sl]
            dg_ref[...] += jnp.sum(d * xh, axis=0, keepdims=True)
            do = _rms_bwd(d, xh, r, g_ref[...])
            if o_ref is oa_ref:
                doa_ref[...] = do.astype(BF16)
            else:
                _scr_put(scr, do)
                for (_, dil), v_ref in zip(BRANCHES, (dob1_ref, dob4_ref, dob16_ref)):
                    _restride(scr, v_ref, dil, tm)

    row = lambda w_: pl.BlockSpec((tm, w_), lambda i: (i, 0))
    full = lambda a: pl.BlockSpec(a.shape, lambda i: (0, 0))
    return pl.pallas_call(
        body, name="wo_bwd", grid=(T // tm,),
        in_specs=[row(D_MODEL), full(wo), row(WIDTH), row(WIDTH), full(ga), full(gb)],
        out_specs=[row(WIDTH)] + _view_specs(tm)
        + [pl.BlockSpec((1, WIDTH), lambda i: (0, 0)), pl.BlockSpec((1, WIDTH), lambda i: (0, 0))],
        out_shape=[jax.ShapeDtypeStruct((T, WIDTH), BF16)] + _view_shapes(T, BF16)
        + [jax.ShapeDtypeStruct((1, WIDTH), F32), jax.ShapeDtypeStruct((1, WIDTH), F32)],
        scratch_shapes=[_scr(tm)],
        compiler_params=_params(("arbitrary",)),
    )(dx2b, wo, oa, ob, ga, gb)


def _inproj_bwd(dqa, dka, dva, dqs, dks, dvs, cos, sin, w, x, dx2, g, *, tm=512):
    T = dqa.shape[0]

    def body(dqa_ref, dka_ref, dva_ref, q1, q2, q3, k1, k2, k3, v1, v2, v3, cos_ref, sin_ref, w_ref, x_ref, dx2_ref,
             g_ref, dp_ref, db_ref, gx_ref, dg_ref, acc, tmp):
        @pl.when(pl.program_id(0) == 0)
        def _():
            db_ref[...] = jnp.zeros_like(db_ref)
            dg_ref[...] = jnp.zeros_like(dg_ref)

        cosv = cos_ref[...]
        sinv = sin_ref[...]
        lane = lax.broadcasted_iota(jnp.int32, (tm, PAIR), 1)
        first = (lane % HEAD_DIM) < (HEAD_DIM // 2)

        def put(off, val):
            dp_ref[:, off:off + PAIR] = val.astype(BF16)
            db_ref[:, off:off + PAIR] += jnp.sum(val, axis=0, keepdims=True)

        for src, off, width in ((dqa_ref, 0, 512), (dka_ref, 512, 256)):
            for j in range(0, width, PAIR):
                d = src[:, j:j + PAIR].astype(F32)
                put(off + j, d * cosv - _rope_rot(d, first) * sinv)
        for j in range(0, 256, PAIR):
            put(768 + j, dva_ref[:, j:j + PAIR].astype(F32))
        for (a, b, c), off in (((q1, q2, q3), 1024), ((k1, k2, k3), 1536), ((v1, v2, v3), 2048)):
            _unstride(b, acc, BRANCHES[1][1], tm)
            _unstride(c, tmp, BRANCHES[2][1], tm)
            for j in range(N_CHUNK):
                put(off + j * PAIR, a[:, j * PAIR:(j + 1) * PAIR].astype(F32) + acc[j] + tmp[j])

        dh = jnp.dot(dp_ref[...], w_ref[...], preferred_element_type=F32)
        xv = x_ref[...]
        r = lax.rsqrt(jnp.mean(xv * xv, axis=-1, keepdims=True) + EPS)
        xh = xv * r
        dg_ref[...] += jnp.sum(dh * xh, axis=0, keepdims=True)
        gx_ref[...] = dx2_ref[...] + _rms_bwd(dh, xh, r, g_ref[...])

    row = lambda w_: pl.BlockSpec((tm, w_), lambda i: (i, 0))
    full = lambda a: pl.BlockSpec(a.shape, lambda i: (0, 0))
    return pl.pallas_call(
        body, name="inproj_bwd", grid=(T // tm,),
        in_specs=[row(512), row(256), row(256)] + _view_specs(tm) * 3 + [row(PAIR), row(PAIR)]
        + [full(w), row(D_MODEL), row(D_MODEL), full(g)],
        out_specs=[row(D_INP), pl.BlockSpec((1, D_INP), lambda i: (0, 0)), row(D_MODEL),
                   pl.BlockSpec((1, D_MODEL), lambda i: (0, 0))],
        out_shape=[jax.ShapeDtypeStruct((T, D_INP), BF16), jax.ShapeDtypeStruct((1, D_INP), F32),
                   jax.ShapeDtypeStruct((T, D_MODEL), F32), jax.ShapeDtypeStruct((1, D_MODEL), F32)],
        scratch_shapes=[_scr(tm)] * 2,
        compiler_params=_params(("arbitrary",)),
    )(dqa, dka, dva, *dqs, *dks, *dvs, cos, sin, w, x, dx2, g)


def _bias_sink_grads(dsums, bmaps, dsk):
    def body(s1, s2, s3, m1, m2, m3, dsk_ref, drel_ref, dsink_ref):
        row = lax.broadcasted_iota(jnp.int32, (N_HEADS, 128), 0)
        lane = lax.broadcasted_iota(jnp.int32, (N_HEADS, 128), 1)
        out = jnp.zeros((N_HEADS, 128), F32)
        for s_ref, m_ref in ((s1, m1), (s2, m2), (s3, m3)):
            bm = m_ref[...]
            for h in range(N_HEADS):
                a = s_ref[h]
                for b in range(REL_BUCKETS):
                    v = jnp.sum(jnp.sum(jnp.where(bm == b, a, 0.0), axis=-1, keepdims=True), axis=0, keepdims=True)
                    out = out + jnp.where((row == h) & (lane == b), v, 0.0)
        drel_ref[...] = out
        dsink_ref[...] = -jnp.sum(dsk_ref[...], axis=0, keepdims=True)

    vm = pl.BlockSpec(memory_space=pltpu.VMEM)
    return pl.pallas_call(
        body, name="bias_sink_grads",
        in_specs=[vm] * 7, out_specs=[vm, vm],
        out_shape=[jax.ShapeDtypeStruct((N_HEADS, 128), F32), jax.ShapeDtypeStruct((1, WIDTH), F32)],
        compiler_params=_params(),
    )(*dsums, *bmaps, dsk)


def _all_gather(blk, *, name):
    R, C = blk.shape

    def body(x_ref, out_ref, send_sems, recv_sems, local_sem):
        x, y, c = lax.axis_index("x"), lax.axis_index("y"), lax.axis_index("c")
        me, sibling = (x, y, c), (x, y, 1 - c)
        chips = [(1 - x, y), (x, 1 - y), (1 - x, 1 - y)]

        def slot(px, py, pc):
            return out_ref.at[4 * px + 2 * py + pc]

        def copy(k, block, to, src=None):
            return pltpu.make_async_remote_copy(
                src_ref=slot(*block) if src is None else src, dst_ref=slot(*block),
                send_sem=send_sems.at[k], recv_sem=recv_sems.at[k], device_id=to, device_id_type=MESH)

        mine = pltpu.make_async_copy(x_ref, slot(*me), local_sem)
        mine.start()
        first = [copy(0, me, sibling, src=x_ref)]
        first += [copy(1 + j, me, (*chip, c), src=x_ref) for j, chip in enumerate(chips)]
        for cp in first:
            cp.start()
        passed = [copy(4 + j, (*chip, c), sibling) for j, chip in enumerate(chips)]
        for j, chip in enumerate(chips):
            copy(1 + j, (*chip, c), me).wait_recv()
            passed[j].start()
        copy(0, sibling, me).wait_recv()
        for j, chip in enumerate(chips):
            copy(4 + j, (*chip, 1 - c), me).wait_recv()
        for cp in first + passed:
            cp.wait_send()
        mine.wait()

    return pl.pallas_call(
        body, name=name,
        in_specs=[pl.BlockSpec(memory_space=pl.ANY)], out_specs=pl.BlockSpec(memory_space=pl.ANY),
        out_shape=jax.ShapeDtypeStruct((N_DEV, R, C), blk.dtype),
        scratch_shapes=[pltpu.SemaphoreType.DMA((7,)), pltpu.SemaphoreType.DMA((7,)), pltpu.SemaphoreType.DMA],
        compiler_params=pltpu.CompilerParams(has_side_effects=True),
    )(blk)


def _peers(x, y, c):
    return [(x ^ (k >> 2), y ^ ((k >> 1) & 1), c ^ (k & 1)) for k in range(1, N_DEV)]


_HBM = pl.BlockSpec(memory_space=pltpu.HBM)
_SEM = pl.BlockSpec(memory_space=pltpu.SEMAPHORE)
_EFFECT = pltpu.SideEffectType.DATAFLOW_SIDE_EFFECTING


def _peer_list(x, y, c, near):
    if near:
        return [(x, y, 1 - c), (1 - x, y, c), (x, 1 - y, c), (1 - x, 1 - y, c)]
    return _peers(x, y, c)


def _exchange_start(srcs, *, gather, name, near=False):
    n = len(srcs)
    n_peers = 4 if near else N_DEV - 1
    lands = [lax.empty((N_DEV,) + s.shape[-2:], s.dtype) for s in srcs]

    def body(*refs):
        src_refs, land_refs = refs[:n], refs[n:2 * n]
        send_sems, recv_sems = refs[2 * n], refs[2 * n + 1]
        token = refs[-1]
        x, y, c = lax.axis_index("x"), lax.axis_index("y"), lax.axis_index("c")
        mine = 4 * x + 2 * y + c
        for a in range(n):
            for k, peer in enumerate(_peer_list(x, y, c, near)):
                dest = 4 * peer[0] + 2 * peer[1] + peer[2]
                j = a * n_peers + k
                pltpu.make_async_remote_copy(
                    src_ref=src_refs[a] if gather else src_refs[a].at[dest], dst_ref=land_refs[a].at[mine],
                    send_sem=send_sems.at[j], recv_sem=recv_sems.at[j], device_id=peer, device_id_type=MESH).start()
        token[...] = jnp.zeros_like(token)

    sems = pltpu.SemaphoreType.DMA((n * n_peers,))
    out = pl.pallas_call(
        body, name=name,
        out_shape=(sems, sems) + tuple(pltpu.HBM(a.shape, a.dtype) for a in list(srcs) + lands)
        + (jax.ShapeDtypeStruct((8, 128), F32),),
        in_specs=(_HBM,) * (2 * n), out_specs=(_SEM, _SEM) + (_HBM,) * (2 * n) + (pl.BlockSpec(memory_space=pltpu.VMEM),),
        input_output_aliases={i: 2 + i for i in range(2 * n)},
        compiler_params=pltpu.CompilerParams(has_side_effects=_EFFECT),
    )(*[pltpu.with_memory_space_constraint(a, pltpu.HBM) for a in list(srcs) + lands])
    return out[:-1], out[-1]


def _exchange_wait(state, after, *, gather, name, near=False):
    send_sems, recv_sems = state[0], state[1]
    n = (len(state) - 2) // 2
    n_peers = 4 if near else N_DEV - 1
    arrays = state[2:]

    def body(*refs):
        src_refs, land_refs = refs[:n], refs[n:2 * n]
        send_sems, recv_sems = refs[2 * n], refs[2 * n + 1]
        x, y, c = lax.axis_index("x"), lax.axis_index("y"), lax.axis_index("c")
        for a in range(n):
            for k, peer in enumerate(_peer_list(x, y, c, near)):
                other = 4 * peer[0] + 2 * peer[1] + peer[2]
                j = a * n_peers + k
                copy = pltpu.make_async_remote_copy(
                    src_ref=src_refs[a] if gather else src_refs[a].at[other], dst_ref=land_refs[a].at[other],
                    send_sem=send_sems.at[j], recv_sem=recv_sems.at[j], device_id=peer, device_id_type=MESH)
                copy.wait_send()
                copy.wait_recv()

    out = pl.pallas_call(
        body, name=name,
        out_shape=tuple(pltpu.HBM(a.shape, a.dtype) for a in arrays),
        in_specs=(_HBM,) * (2 * n) + (_SEM, _SEM, pl.BlockSpec(memory_space=pl.ANY)), out_specs=(_HBM,) * (2 * n),
        input_output_aliases={i: i for i in range(2 * n)},
        compiler_params=pltpu.CompilerParams(has_side_effects=_EFFECT),
    )(*arrays, send_sems, recv_sems, after)
    mine = 4 * lax.axis_index("x") + 2 * lax.axis_index("y") + lax.axis_index("c")
    own = out[:n] if gather else [lax.dynamic_index_in_dim(s, mine, 0, keepdims=False) for s in out[:n]]
    return [lax.dynamic_update_slice(g, o[None], (mine, 0, 0)) for g, o in zip(out[n:], own)]


def _forward_start(lands, *, name):
    n = len(lands)

    def body(*refs):
        land_refs, send_sems, recv_sems, token = refs[:n], refs[n], refs[n + 1], refs[-1]
        x, y, c = lax.axis_index("x"), lax.axis_index("y"), lax.axis_index("c")
        for a in range(n):
            for j, (px, py) in enumerate(((1 - x, y), (x, 1 - y), (1 - x, 1 - y))):
                blk = 4 * px + 2 * py + c
                pltpu.make_async_remote_copy(
                    src_ref=land_refs[a].at[blk], dst_ref=land_refs[a].at[blk], send_sem=send_sems.at[3 * a + j],
                    recv_sem=recv_sems.at[3 * a + j], device_id=(x, y, 1 - c), device_id_type=MESH).start()
        token[...] = jnp.zeros_like(token)

    sems = pltpu.SemaphoreType.DMA((3 * n,))
    out = pl.pallas_call(
        body, name=name,
        out_shape=(sems, sems) + tuple(pltpu.HBM(a.shape, a.dtype) for a in lands) + (jax.ShapeDtypeStruct((8, 128), F32),),
        in_specs=(_HBM,) * n, out_specs=(_SEM, _SEM) + (_HBM,) * n + (pl.BlockSpec(memory_space=pltpu.VMEM),),
        input_output_aliases={i: 2 + i for i in range(n)},
        compiler_params=pltpu.CompilerParams(has_side_effects=_EFFECT),
    )(*[pltpu.with_memory_space_constraint(a, pltpu.HBM) for a in lands])
    return out[:-1], out[-1]


def _forward_wait(state, after, *, name):
    send_sems, recv_sems = state[0], state[1]
    lands = state[2:]
    n = len(lands)

    def body(*refs):
        land_refs, send_sems, recv_sems = refs[:n], refs[n], refs[n + 1]
        x, y, c = lax.axis_index("x"), lax.axis_index("y"), lax.axis_index("c")
        for a in range(n):
            for j, (px, py) in enumerate(((1 - x, y), (x, 1 - y), (1 - x, 1 - y))):
                copy = pltpu.make_async_remote_copy(
                    src_ref=land_refs[a].at[4 * px + 2 * py + c], dst_ref=land_refs[a].at[4 * px + 2 * py + 1 - c],
                    send_sem=send_sems.at[3 * a + j], recv_sem=recv_sems.at[3 * a + j], device_id=(x, y, 1 - c),
                    device_id_type=MESH)
                copy.wait_send()
                copy.wait_recv()

    return pl.pallas_call(
        body, name=name,
        out_shape=tuple(pltpu.HBM(a.shape, a.dtype) for a in lands),
        in_specs=(_HBM,) * n + (_SEM, _SEM, pl.BlockSpec(memory_space=pl.ANY)), out_specs=(_HBM,) * n,
        input_output_aliases={i: i for i in range(n)},
        compiler_params=pltpu.CompilerParams(has_side_effects=_EFFECT),
    )(*lands, send_sems, recv_sems, after)


def _adam_math(w, g, m, v):
    m = ADAM_B1 * m + (1.0 - ADAM_B1) * g
    v = ADAM_B2 * v + (1.0 - ADAM_B2) * (g * g)
    m_hat = m / (1.0 - ADAM_B1 ** ADAM_STEP)
    v_hat = v / (1.0 - ADAM_B2 ** ADAM_STEP)
    delta = -ADAM_LR * (m_hat / (jnp.sqrt(v_hat) + ADAM_EPS) + ADAM_WD * w)
    return delta, m, v


def _adamw(parts, w, m, v, *, name):
    R, C = w.shape
    n_parts = parts.shape[0]
    tr = R // 2
    assert tr % 16 == 0

    def body(p_ref, w_ref, m_ref, v_ref, g_ref, d_ref, nm_ref, nv_ref):
        g = p_ref[0].astype(F32)
        for s in range(1, n_parts):
            g = g + p_ref[s].astype(F32)
        d, nm, nv = _adam_math(w_ref[...], g, m_ref[...], v_ref[...])
        g_ref[...] = g
        d_ref[...] = d
        nm_ref[...] = nm
        nv_ref[...] = nv

    blk = pl.BlockSpec((tr, C), lambda i: (i, 0))
    return pl.pallas_call(
        body, name=name, grid=(R // tr,),
        in_specs=[pl.BlockSpec((n_parts, tr, C), lambda i: (0, i, 0)), blk, blk, blk],
        out_specs=[blk] * 4, out_shape=[jax.ShapeDtypeStruct((R, C), F32)] * 4,
        compiler_params=_params(("arbitrary",)),
    )(parts, w, m, v)


def _adamw_small(parts, w, m, v):
    def body(p_ref, w_ref, m_ref, v_ref, g_ref, d_ref, nm_ref, nv_ref):
        g = p_ref[0]
        for s in range(1, N_DEV):
            g = g + p_ref[s]
        d, nm, nv = _adam_math(w_ref[...], g, m_ref[...], v_ref[...])
        g_ref[...] = g
        d_ref[...] = d
        nm_ref[...] = nm
        nv_ref[...] = nv

    vm = pl.BlockSpec(memory_space=pltpu.VMEM)
    return pl.pallas_call(
        body, name="adamw_small", in_specs=[vm] * 4, out_specs=[vm] * 4,
        out_shape=[jax.ShapeDtypeStruct((SMALL_ROWS, 128), F32)] * 4, compiler_params=_params(),
    )(parts, w, m, v)


def _t5_bucket(dist):
    max_exact = REL_BUCKETS // 2
    df = jnp.maximum(dist, 1).astype(F32)
    large = max_exact + (jnp.log(df / max_exact) / math.log(REL_MAX_DISTANCE / max_exact)
                         * (REL_BUCKETS - max_exact)).astype(jnp.int32)
    large = jnp.minimum(large, REL_BUCKETS - 1)
    return jnp.where(dist < max_exact, dist, large)


def _band_tables(rel_table, dil, n_back):
    qi = jnp.arange(BLK)[:, None]
    kj = jnp.arange(2 * BLK)[None, :]
    delta = BLK + qi - kj
    in_band = (delta >= 0) & (delta <= n_back)
    if rel_table is None:
        vals = jnp.zeros((N_HEADS, BLK, 2 * BLK), F32)
        bmap = None
    else:
        bucket = _t5_bucket(jnp.clip(delta, 0, n_back) * dil)
        vals = jnp.zeros((N_HEADS, BLK, 2 * BLK), F32)
        for b in range(REL_BUCKETS):
            vals = jnp.where((bucket == b)[None], rel_table[b][:, None, None], vals)
        bmap = jnp.where(in_band, bucket, -1).astype(jnp.int32)
    later = jnp.where(in_band[None], vals, NEG)
    first = jnp.where((in_band & (kj >= BLK))[None], vals, NEG)
    return jnp.stack([later, first]), bmap


def _rope_tables(T):
    half = HEAD_DIM // 2
    inv_freq = ROPE_THETA ** (-jnp.arange(half, dtype=F32) / half)
    ang = jnp.arange(T, dtype=F32)[:, None] * inv_freq[None, :]
    cos, sin = jnp.cos(ang), jnp.sin(ang)
    return jnp.tile(cos, (1, 4)), jnp.tile(jnp.concatenate([-sin, sin], axis=1), (1, 2))


def _widen_in(a, axis):
    sl = lambda lo, hi: lax.slice_in_dim(a, lo, hi, axis=axis)
    dup = lambda lo: [sl(lo, lo + 64), sl(lo, lo + 64), sl(lo + 64, lo + 128), sl(lo + 64, lo + 128)]
    return jnp.concatenate([sl(0, 512)] + dup(512) + dup(640) + [sl(768, D_IN)], axis=axis)


def _fold_in(a, axis):
    sl = lambda lo, hi: lax.slice_in_dim(a, lo, hi, axis=axis)
    fold = lambda lo: [sl(lo, lo + 64) + sl(lo + 64, lo + 128), sl(lo + 128, lo + 192) + sl(lo + 192, lo + 256)]
    return jnp.concatenate([sl(0, 512)] + fold(512) + fold(768) + [sl(1024, D_INP)], axis=axis)


def _local_step(x, tgt, g_attn, b_in, sinks, rel_table, g_out_a, g_out_b, g_ffn, g_final,
                win_fn, wo_fn, ffn_fn, early_fn):
    T = x.shape[0]
    cos, sin = _rope_tables(T)
    g_final2 = g_final.reshape(1, D_MODEL)
    sink8 = sinks.reshape(N_HEADS)

    bias_a, _ = _band_tables(None, 1, BLK - 1)
    tabs = [_band_tables(rel_table, dil, window // dil) for window, dil in BRANCHES]
    built = cos[:1, :1] + sin[:1, :1] + bias_a[0, 0, :1, :1] + sum(t[0][0, 0, :1, :1] for t in tabs)
    wint, token = win_fn(built)
    winp = _widen_in(wint, 0)
    binp = _widen_in(b_in, 1) + token[0, 0]

    h1, qa, ka, va, *qkv_b = _norm_proj(x, g_attn, winp, binp, cos, sin)
    qbs, kbs, vbs = qkv_b[0:3], qkv_b[3:6], qkv_b[6:9]
    oa, lse_a = _attn_fwd(qa, ka, va, bias_a, sink8, dil=1, kv_pairs=2, use_sink=True, name="attn_a_fwd")
    outs = [_attn_fwd(qbs[n], kbs[n], vbs[n], tabs[n][0], sink8, dil=dil, kv_pairs=4, use_sink=False,
                      name=f"attn_b{n}_fwd") for n, (_, dil) in enumerate(BRANCHES)]
    wo = wo_fn(outs[2][1])
    x2, mixed, h2, *ob_lse = _merge_wo(x, oa, outs[0][0], outs[1][0], outs[2][0], outs[0][1], outs[1][1], outs[2][1],
                                       g_out_a, g_out_b, wo, g_ffn)
    obs, lses = ob_lse[0:3], ob_lse[3:6]
    wgt, wut, wd = ffn_fn(h2)
    gate, up, act = _ffn_up(h2, wgt, wut)
    dx3, dx3b, loss, dg_final = _ffn_down_loss(act, wd, x2, tgt, g_final2)

    dgate, dup, dx2, dx2b, dg_ffn = _ffn_bwd(dx3, gate, up, wd, wgt, wut, x2, g_ffn)
    dwd = _matmul_tn(act, dx3b, tk=1408, tn=1024, name="dw_down")
    dwgt = _matmul_tn(dgate, h2, tk=1408, tn=1024, name="dw_gate")
    dwut = _matmul_tn(dup, h2, tk=1408, tn=1024, name="dw_up")
    dwo = _matmul_tn(mixed, dx2b, tk=1024, tn=1024, name="dw_o")
    early, token2 = early_fn(dict(w_o=dwo, w_gate=dwgt, w_up=dwut, w_down=dwd))
    doa, *dobs, dg_out_a, dg_out_b = _wo_bwd(dx2b, wo, oa, obs[0], g_out_a + token2[0, 0], g_out_b)

    dqa, dka, dva, _, dsk = _attn_bwd(qa, ka, va, oa, doa, lse_a, bias_a, sink8, dil=1, kv_pairs=2, use_sink=True,
                                      name="attn_a_bwd")
    res = [_attn_bwd(qbs[n], kbs[n], vbs[n], obs[n], dobs[n], lses[n], tabs[n][0], sink8, dil=dil, kv_pairs=4,
                     use_sink=False, name=f"attn_b{n}_bwd") for n, (_, dil) in enumerate(BRANCHES)]
    dp, dbp, grad_x, dg_attn = _inproj_bwd(dqa, dka, dva, [r[0] for r in res], [r[1] for r in res],
                                           [r[2] for r in res], cos, sin, winp, x, dx2, g_attn)
    dwin = _fold_in(_matmul_tn(dp, h1, tk=1280, tn=1024, out_dtype=F32, name="dw_in"), 0)
    drel, dsink = _bias_sink_grads([r[3] for r in res], [t[1] for t in tabs], dsk)

    small = dict(
        g_attn=dg_attn, b_in=_fold_in(dbp, 1), sinks=dsink[:, ::HEAD_DIM], rel_table=drel[:, :REL_BUCKETS].T,
        g_out_a=dg_out_a, g_out_b=dg_out_b, g_ffn=dg_ffn, g_final=dg_final.reshape(D_MODEL))
    return loss[0, 0], grad_x, dwin, early, small


SMALL_NAMES = ("g_attn", "b_in", "sinks", "rel_table", "g_out_a", "g_out_b", "g_ffn", "g_final")


def _pack_small(vals):
    flat = jnp.concatenate([vals[n].reshape(-1).astype(F32) for n in SMALL_NAMES])
    return jnp.pad(flat, (0, SMALL_ROWS * 128 - flat.shape[0])).reshape(SMALL_ROWS, 128)


def _unpack_small(packed, like):
    flat = packed.reshape(-1)
    out, off = {}, 0
    for n in SMALL_NAMES:
        size = like[n].size
        out[n] = flat[off:off + size].reshape(like[n].shape)
        off += size
    return out


def kernel(x, g_attn, w_in, b_in, sinks, rel_table, g_out_a, g_out_b, w_o, g_ffn, w_gate, w_up, w_down, g_final, loss_target, m_g_attn, m_w_in, m_b_in, m_sinks, m_rel_table, m_g_out_a, m_g_out_b, m_w_o, m_g_ffn, m_w_gate, m_w_up, m_w_down, m_g_final, v_g_attn, v_w_in, v_b_in, v_sinks, v_rel_table, v_g_out_a, v_g_out_b, v_w_o, v_g_ffn, v_w_gate, v_w_up, v_w_down, v_g_final):
    rest_names = ("w_o", "w_gate", "w_up", "w_down")

    rest = [w_o[0].astype(BF16), w_gate[0].astype(BF16).T, w_up[0].astype(BF16).T, w_down[0].astype(BF16)]
    in_state, _ = _exchange_start([w_in[0].astype(BF16).T], gather=True, near=True, name="gather_w_in_start")
    later = {}

    def whole(got):
        return [g.reshape(N_DEV * g.shape[1], D_MODEL) for g in got]

    def win_fn(after):
        after = after + sum(r[:1, :1].astype(F32) for r in rest)
        near = _exchange_wait(in_state, after, gather=True, near=True, name="gather_w_in_near")
        fwd_state, tok = _forward_start(near, name="gather_w_in_forward")
        wint = whole(_forward_wait(fwd_state, tok, name="gather_w_in_wait"))[0]
        wint, src = lax.optimization_barrier((wint, rest))
        later["wo"], token_o = _exchange_start(src[:1], gather=True, name="gather_w_o_start")
        token_o, ffn_src = lax.optimization_barrier((token_o, src[1:]))
        later["ffn"], token = _exchange_start(ffn_src, gather=True, name="gather_ffn_start")
        return wint, token + token_o

    def wo_fn(after):
        return whole(_exchange_wait(later["wo"], after, gather=True, name="gather_w_o_wait"))[0]

    def ffn_fn(after):
        return whole(_exchange_wait(later["ffn"], after, gather=True, name="gather_ffn_wait"))

    def early_fn(dws):
        return _exchange_start([dws[n].reshape(N_DEV, -1, D_MODEL) for n in rest_names], gather=False,
                               name="scatter_rest_start")

    loss_part, grad_x, dwint, early_state, small = _local_step(
        x[0], loss_target[0], g_attn, b_in, sinks, rel_table, g_out_a, g_out_b, g_ffn, g_final,
        win_fn, wo_fn, ffn_fn, early_fn)
    loss = lax.psum(loss_part, ("x", "y", "c"))

    parts_in = dwint.astype(BF16).reshape(N_DEV, D_IN // N_DEV, D_MODEL)
    in_state, token3 = _exchange_start([parts_in], gather=False, name="scatter_w_in_start")
    got = _exchange_wait(early_state, token3, gather=False, name="scatter_rest_wait")

    def update(n, parts, w, m, v, transposed):
        if transposed:
            return [a.T[None] for a in _adamw(parts, w[0].T, m[0].T, v[0].T, name="adamw_" + n)]
        return [a[None] for a in _adamw(parts, w[0], m[0], v[0], name="adamw_" + n)]

    big = dict(w_o=update("w_o", got[0], w_o, m_w_o, v_w_o, False),
               w_gate=update("w_gate", got[1], w_gate, m_w_gate, v_w_gate, True),
               w_up=update("w_up", got[2], w_up, m_w_up, v_w_up, True),
               w_down=update("w_down", got[3], w_down, m_w_down, v_w_down, False))

    ws = dict(g_attn=g_attn, b_in=b_in, sinks=sinks, rel_table=rel_table, g_out_a=g_out_a, g_out_b=g_out_b,
              g_ffn=g_ffn, g_final=g_final)
    ms = dict(g_attn=m_g_attn, b_in=m_b_in, sinks=m_sinks, rel_table=m_rel_table, g_out_a=m_g_out_a,
              g_out_b=m_g_out_b, g_ffn=m_g_ffn, g_final=m_g_final)
    vs = dict(g_attn=v_g_attn, b_in=v_b_in, sinks=v_sinks, rel_table=v_rel_table, g_out_a=v_g_out_a,
              g_out_b=v_g_out_b, g_ffn=v_g_ffn, g_final=v_g_final)
    sparts = _all_gather(_pack_small(small), name="gather_small")
    sm_packed = _adamw_small(sparts, _pack_small(ws), _pack_small(ms), _pack_small(vs))
    sm = [_unpack_small(a, ws) for a in sm_packed]

    done = sm_packed[1][:1, :1] + sum(big[n][1][0, :1, :1] for n in rest_names)
    got_in = _exchange_wait(in_state, done, gather=False, name="scatter_w_in_wait")[0]
    big["w_in"] = update("w_in", got_in, w_in, m_w_in, v_w_in, True)

    order = ("g_attn", "w_in", "b_in", "sinks", "rel_table", "g_out_a", "g_out_b", "w_o", "g_ffn", "w_gate", "w_up",
             "w_down", "g_final")
    outs = [loss, grad_x[None]]
    for k in range(4):
        outs += [big[n][k] if n in big else sm[k][n] for n in order]
    return tuple(outs)
```

```python
import functools
import math

import jax
import jax.numpy as jnp
from jax import lax
from jax.experimental import pallas as pl
from jax.experimental.pallas import tpu as pltpu

F32 = jnp.float32
BF16 = jnp.bfloat16

N_DEV = 8
D_MODEL = 1024
HEAD_DIM = 64
N_HEADS = 8
PAIR = 2 * HEAD_DIM
WIDTH = N_HEADS * HEAD_DIM
D_IN = 2304
D_INP = 2560
D_FF = 2816
BLK = 128
ROPE_THETA = 150000.0
REL_BUCKETS = 32
REL_MAX_DISTANCE = 2048
EPS = 1e-5
NEG = -1e30
BRANCHES = ((128, 1), (512, 4), (2048, 16))
Q_SCALE = HEAD_DIM ** -0.5

ADAM_LR = 0.001
ADAM_B1 = 0.9
ADAM_B2 = 0.999
ADAM_EPS = 1e-08
ADAM_WD = 0.01
ADAM_STEP = 10

VMEM_LIMIT = 56 * 1024 * 1024
MESH = pl.DeviceIdType.MESH

NT = (((1,), (1,)), ((), ()))
TN = (((0,), (0,)), ((), ()))

SMALL_ROWS = 56


def _params(sem=None):
    return pltpu.CompilerParams(dimension_semantics=sem, vmem_limit_bytes=VMEM_LIMIT)


def _sigmoid(x):
    return 1.0 / (1.0 + jnp.exp2(x * (-1.0 / math.log(2.0))))


def _rms_bwd(dh, xh, r, g):
    u = dh * g
    return r * (u - xh * jnp.mean(u * xh, axis=-1, keepdims=True))


def _rope_rot(t, first):
    return jnp.where(first, pltpu.roll(t, 96, 1), pltpu.roll(t, 32, 1))


N_CHUNK = WIDTH // PAIR


def _scr(tm):
    return pltpu.VMEM((N_CHUNK, tm, PAIR), F32)


def _scr_get(scr):
    return jnp.concatenate([scr[j] for j in range(N_CHUNK)], axis=1)


def _scr_put(scr, val):
    for j in range(N_CHUNK):
        scr[j] = val[:, j * PAIR:(j + 1) * PAIR]


def _unstride(view_ref, scr, dil, tm):
    n = tm // dil
    chunks = scr.shape[0]
    for r in range(dil):
        for j in range(chunks):
            col = (r * chunks + j) * PAIR
            scr.at[j][pl.ds(r, n, stride=dil), :] = view_ref[:, col:col + PAIR].astype(F32)


def _restride(scr, out_ref, dil, tm):
    n = tm // dil
    chunks = scr.shape[0]
    for r in range(dil):
        for j in range(chunks):
            col = (r * chunks + j) * PAIR
            rows = scr[j] if dil == 1 else scr.at[j][pl.ds(r, n, stride=dil), :]
            out_ref[:, col:col + PAIR] = rows.astype(out_ref.dtype)


def _view_specs(tm, width=WIDTH):
    return [pl.BlockSpec((tm // dil, dil * width), lambda i: (i, 0)) for _, dil in BRANCHES]


def _view_shapes(T, dtype, width=WIDTH):
    return [jax.ShapeDtypeStruct((T // dil, dil * width), dtype) for _, dil in BRANCHES]


def _norm_proj(x, g, w, b, cos, sin, *, tm=512):
    T = x.shape[0]

    def body(x_ref, g_ref, w_ref, b_ref, cos_ref, sin_ref, h_ref, qa_ref, ka_ref, va_ref, *rest):
        outs_b, ys = rest[:9], rest[9]
        xv = x_ref[...]
        r = lax.rsqrt(jnp.mean(xv * xv, axis=-1, keepdims=True) + EPS)
        h = (xv * r * g_ref[...]).astype(BF16)
        h_ref[...] = h
        cosv = cos_ref[...]
        sinv = sin_ref[...]
        lane = lax.broadcasted_iota(jnp.int32, (tm, PAIR), 1)
        first = (lane % HEAD_DIM) < (HEAD_DIM // 2)

        def proj(off):
            return (lax.dot_general(h, w_ref[off:off + 256, :], NT, preferred_element_type=F32)
                    + b_ref[:, off:off + 256])

        for (off, width, rot, scale), o_ref in zip(((0, 512, True, Q_SCALE), (512, 256, True, 1.0), (768, 256, False, 1.0)),
                                                   (qa_ref, ka_ref, va_ref)):
            for c in range(0, width, 256):
                y = proj(off + c)
                for j in range(0, 256, PAIR):
                    t = y[:, j:j + PAIR]
                    if rot:
                        t = t * cosv + _rope_rot(t, first) * sinv
                    if scale != 1.0:
                        t = t * scale
                    o_ref[:, c + j:c + j + PAIR] = t.astype(BF16)
        for n, (off, scale) in enumerate(((1024, Q_SCALE), (1536, 1.0), (2048, 1.0))):
            for c in range(0, WIDTH, 256):
                y = proj(off + c)
                y = y * scale if scale != 1.0 else y
                for j in range(0, 256, PAIR):
                    ys[(c + j) // PAIR] = y[:, j:j + PAIR]
            for (_, dil), o_ref in zip(BRANCHES, outs_b[3 * n:3 * n + 3]):
                _restride(ys, o_ref, dil, tm)

    row = lambda w_: pl.BlockSpec((tm, w_), lambda i: (i, 0))
    full = lambda a: pl.BlockSpec(a.shape, lambda i: (0, 0))
    return pl.pallas_call(
        body, name="norm_proj", grid=(T // tm,),
        in_specs=[row(D_MODEL), full(g), full(w), full(b), row(PAIR), row(PAIR)],
        out_specs=[row(D_MODEL), row(512), row(256), row(256)] + _view_specs(tm) * 3,
        out_shape=[jax.ShapeDtypeStruct((T, n), BF16) for n in (D_MODEL, 512, 256, 256)] + _view_shapes(T, BF16) * 3,
        scratch_shapes=[_scr(tm)],
        compiler_params=_params(("arbitrary",)),
    )(x, g, w, b, cos, sin)


SUB = 4
AHEAD = 2


def _attn_specs(kvw):
    q_spec = pl.BlockSpec((SUB * BLK, WIDTH), lambda r, i: (i, r))
    kc_spec = pl.BlockSpec((SUB * BLK, kvw), lambda r, i: (i, r))
    kp_spec = pl.BlockSpec((BLK, kvw), lambda r, i: (jnp.maximum(SUB * i - 1, 0), r))
    b_spec = pl.BlockSpec((2, N_HEADS, BLK, 2 * BLK), lambda r, i: (0, 0, 0, 0))
    return q_spec, kp_spec, kc_spec, b_spec


def _window(prev_ref, cur_ref, j, ksl):
    before = prev_ref[:, ksl] if j == 0 else cur_ref[(j - 1) * BLK:j * BLK, ksl]
    return jnp.concatenate([before, cur_ref[j * BLK:(j + 1) * BLK, ksl]], axis=0)


def _attn_fwd(q, k, v, bias, sinks, *, dil, kv_pairs, use_sink, name):
    L = q.shape[0]
    ns = L // (SUB * BLK)
    kvw = kv_pairs * PAIR
    rep = 4 // kv_pairs

    def body(sink_ref, q_ref, kp_ref, kc_ref, vp_ref, vc_ref, b_ref, o_ref, lse_ref):
        lane = lax.broadcasted_iota(jnp.int32, (1, PAIR), 1)
        lo = lane < HEAD_DIM
        first = jnp.where(pl.program_id(1) == 0, 1, 0)
        def scores(j, hp):
            rows = slice(j * BLK, (j + 1) * BLK)
            sl = slice(hp * PAIR, (hp + 1) * PAIR)
            ksl = slice((hp // rep) * PAIR, (hp // rep + 1) * PAIR)
            qp = q_ref[rows, sl]
            kk = _window(kp_ref, kc_ref, j, ksl)
            vv = _window(vp_ref, vc_ref, j, ksl)
            heads = []
            for e in range(2):
                h = 2 * hp + e
                msk = lo if e == 0 else jnp.logical_not(lo)
                qm = jnp.where(msk, qp, jnp.zeros_like(qp))
                s = lax.dot_general(qm, kk, NT, preferred_element_type=F32) + (b_ref[first, h] if j == 0 else b_ref[0, h])
                heads.append((h, msk, s))
            return rows, sl, vv, heads

        def outputs(rows, sl, vv, heads):
            o_pair = None
            lse_pair = None
            for h, msk, s in heads:
                m = jnp.max(s, axis=-1, keepdims=True)
                if use_sink:
                    sk = sink_ref[h]
                    m = jnp.maximum(m, sk)
                p = jnp.exp(s - m)
                l = jnp.sum(p, axis=-1, keepdims=True)
                if use_sink:
                    l = l + jnp.exp(sk - m)
                vm = jnp.where(msk, vv, jnp.zeros_like(vv))
                oe = jnp.dot(p.astype(BF16), vm, preferred_element_type=F32) * (1.0 / l)
                ls = m + jnp.log(l)
                if o_pair is None:
                    o_pair = oe
                    lse_pair = jnp.broadcast_to(ls, (BLK, PAIR))
                else:
                    o_pair = o_pair + oe
                    lse_pair = jnp.where(lo, lse_pair, ls)
            o_ref[rows, sl] = o_pair.astype(BF16)
            lse_ref[rows, sl] = lse_pair

        items = [(j, hp) for j in range(SUB) for hp in range(4)]
        queue = [scores(*it) for it in items[:AHEAD]]
        for n in range(len(items)):
            if n + AHEAD < len(items):
                queue.append(scores(*items[n + AHEAD]))
            outputs(*queue.pop(0))

    q_spec, kp_spec, kc_spec, b_spec = _attn_specs(kvw)
    return pl.pallas_call(
        body, name=name, grid=(dil, ns),
        in_specs=[pl.BlockSpec(memory_space=pltpu.SMEM), q_spec, kp_spec, kc_spec, kp_spec, kc_spec, b_spec],
        out_specs=[q_spec, q_spec],
        out_shape=[jax.ShapeDtypeStruct((L, dil * WIDTH), BF16), jax.ShapeDtypeStruct((L, dil * WIDTH), F32)],
        compiler_params=_params(("arbitrary", "arbitrary")),
    )(sinks, q, k, k, v, v, bias)


def _attn_bwd(q, k, v, o, do, lse, bias, sinks, *, dil, kv_pairs, use_sink, name):
    L = q.shape[0]
    ns = L // (SUB * BLK)
    n_steps = dil * ns
    kvw = kv_pairs * PAIR
    rep = 4 // kv_pairs
    last = slice((SUB - 1) * BLK, SUB * BLK)

    def body(sink_ref, q_ref, kp_ref, kc_ref, vp_ref, vc_ref, o_ref, do_ref, lse_ref, b_ref,
             dq_ref, dk_ref, dv_ref, dsum_ref, dsk_ref, pk_ref, pv_ref):
        t = pl.program_id(0)
        i = t % ns

        @pl.when(t == 0)
        def _():
            dsum_ref[...] = jnp.zeros_like(dsum_ref)
            dsk_ref[...] = jnp.zeros_like(dsk_ref)
            pk_ref[...] = jnp.zeros_like(pk_ref)
            pv_ref[...] = jnp.zeros_like(pv_ref)

        @pl.when(t < n_steps)
        def _():
            lo = lax.broadcasted_iota(jnp.int32, (1, PAIR), 1) < HEAD_DIM
            first = jnp.where(i == 0, 1, 0)
            dks = [[None] * kv_pairs for _ in range(SUB)]
            dvs = [[None] * kv_pairs for _ in range(SUB)]
            def scores(j, hp):
                rows = slice(j * BLK, (j + 1) * BLK)
                kvp = hp // rep
                sl = slice(hp * PAIR, (hp + 1) * PAIR)
                ksl = slice(kvp * PAIR, (kvp + 1) * PAIR)
                qp = q_ref[rows, sl]
                dop = do_ref[rows, sl]
                prod = dop.astype(F32) * o_ref[rows, sl].astype(F32)
                kk = _window(kp_ref, kc_ref, j, ksl)
                vv = _window(vp_ref, vc_ref, j, ksl)
                heads = []
                for e in range(2):
                    h = 2 * hp + e
                    msk = lo if e == 0 else jnp.logical_not(lo)
                    qm = jnp.where(msk, qp, jnp.zeros_like(qp))
                    dom = jnp.where(msk, dop, jnp.zeros_like(dop))
                    km = jnp.where(msk, kk, jnp.zeros_like(kk))
                    s = (lax.dot_general(qm, kk, NT, preferred_element_type=F32)
                         + (b_ref[first, h] if j == 0 else b_ref[0, h]))
                    dp = lax.dot_general(dom, vv, NT, preferred_element_type=F32)
                    heads.append((h, msk, qm, dom, km, s, dp))
                return j, rows, kvp, sl, prod, heads

            def grads(j, rows, kvp, sl, prod, heads):
                dq_pair = None
                c_pair = None
                qms, doms, dsbs, pbs = [], [], [], []
                for h, msk, qm, dom, km, s, dp in heads:
                    ls = lse_ref[rows, h * HEAD_DIM:h * HEAD_DIM + 1]
                    p = jnp.exp(s - ls)
                    delta = jnp.sum(jnp.where(msk, prod, 0.0), axis=-1, keepdims=True)
                    ds = p * (dp - delta)
                    if use_sink:
                        ce = jnp.exp(sink_ref[h] - ls) * delta
                        c_pair = jnp.broadcast_to(ce, (BLK, PAIR)) if c_pair is None else jnp.where(msk, ce, c_pair)
                    else:
                        dsum_ref[h] += ds
                    dsb = ds.astype(BF16)
                    dqe = jnp.dot(dsb, km, preferred_element_type=F32)
                    dq_pair = dqe if dq_pair is None else dq_pair + dqe
                    qms.append(qm)
                    doms.append(dom)
                    dsbs.append(dsb)
                    pbs.append(p.astype(BF16))
                dke = lax.dot_general(jnp.concatenate(dsbs, axis=0), jnp.concatenate(qms, axis=0), TN,
                                      preferred_element_type=F32)
                dve = lax.dot_general(jnp.concatenate(pbs, axis=0), jnp.concatenate(doms, axis=0), TN,
                                      preferred_element_type=F32)
                dks[j][kvp] = dke if dks[j][kvp] is None else dks[j][kvp] + dke
                dvs[j][kvp] = dve if dvs[j][kvp] is None else dvs[j][kvp] + dve
                dq_ref[rows, sl] = (dq_pair * Q_SCALE).astype(BF16)
                if use_sink:
                    dsk_ref[:, sl] += c_pair

            items = [(j, hp) for j in range(SUB) for hp in range(4)]
            ahead = AHEAD + 1 if use_sink else AHEAD
            queue = [scores(*it) for it in items[:ahead]]
            for n in range(len(items)):
                if n + ahead < len(items):
                    queue.append(scores(*items[n + ahead]))
                grads(*queue.pop(0))
            for kvp in range(kv_pairs):
                ksl = slice(kvp * PAIR, (kvp + 1) * PAIR)
                for pend_ref, out_ref, parts in ((pk_ref, dk_ref, [d[kvp] for d in dks]),
                                                 (pv_ref, dv_ref, [d[kvp] for d in dvs])):
                    if SUB > 1:
                        out_ref[:(SUB - 1) * BLK, ksl] = pend_ref[:(SUB - 1) * BLK, ksl].astype(BF16)
                    out_ref[last, ksl] = (pend_ref[last, ksl] + parts[0][:BLK]).astype(BF16)
                    for j in range(SUB):
                        own = parts[j][BLK:]
                        pend_ref[j * BLK:(j + 1) * BLK, ksl] = own + parts[j + 1][:BLK] if j + 1 < SUB else own

        @pl.when(t == n_steps)
        def _():
            dk_ref[...] = pk_ref[...].astype(BF16)
            dv_ref[...] = pv_ref[...].astype(BF16)

    def at(t):
        t = jnp.minimum(t, n_steps - 1)
        return t % ns, t // ns

    def before(t):
        return at(jnp.maximum(t - 1, 0))

    q_spec = pl.BlockSpec((SUB * BLK, WIDTH), at)
    kc_spec = pl.BlockSpec((SUB * BLK, kvw), at)
    kp_spec = pl.BlockSpec((BLK, kvw), lambda t: (jnp.maximum(SUB * at(t)[0] - 1, 0), at(t)[1]))
    b_spec = pl.BlockSpec((2, N_HEADS, BLK, 2 * BLK), lambda t: (0, 0, 0, 0))
    dkv_spec = pl.BlockSpec((SUB * BLK, kvw), before)
    return pl.pallas_call(
        body, name=name, grid=(n_steps + 1,),
        in_specs=[pl.BlockSpec(memory_space=pltpu.SMEM), q_spec, kp_spec, kc_spec, kp_spec, kc_spec,
                  q_spec, q_spec, q_spec, b_spec],
        out_specs=[q_spec, dkv_spec, dkv_spec,
                   pl.BlockSpec((N_HEADS, BLK, 2 * BLK), lambda t: (0, 0, 0)),
                   pl.BlockSpec((BLK, WIDTH), lambda t: (0, 0))],
        out_shape=[jax.ShapeDtypeStruct((L, dil * WIDTH), BF16),
                   jax.ShapeDtypeStruct((L, dil * kvw), BF16),
                   jax.ShapeDtypeStruct((L, dil * kvw), BF16),
                   jax.ShapeDtypeStruct((N_HEADS, BLK, 2 * BLK), F32),
                   jax.ShapeDtypeStruct((BLK, WIDTH), F32)],
        scratch_shapes=[pltpu.VMEM((SUB * BLK, kvw), F32), pltpu.VMEM((SUB * BLK, kvw), F32)],
        compiler_params=_params(("arbitrary",)),
    )(sinks, q, k, k, v, v, o, do, lse, bias)


def _merge_wo(x, oa, o1, o2, o3, l1, l2, l3, ga, gb, wo, gf, *, tm=512):
    T = x.shape[0]

    def body(x_ref, oa_ref, o1_ref, o2_ref, o3_ref, l1_ref, l2_ref, l3_ref, ga_ref, gb_ref, wo_ref, gf_ref,
             x2_ref, mix_ref, h2_ref, ob1_ref, ob4_ref, ob16_ref, ls1_ref, ls4_ref, ls16_ref, so2, so3, sl2, sl3):
        _unstride(o2_ref, so2, BRANCHES[1][1], tm)
        _unstride(o3_ref, so3, BRANCHES[2][1], tm)
        _unstride(l2_ref, sl2, BRANCHES[1][1], tm)
        _unstride(l3_ref, sl3, BRANCHES[2][1], tm)
        la, lb, lc = l1_ref[...], _scr_get(sl2), _scr_get(sl3)
        m = jnp.maximum(jnp.maximum(la, lb), lc)
        ea, eb, ec = jnp.exp(la - m), jnp.exp(lb - m), jnp.exp(lc - m)
        den = ea + eb + ec
        inv = 1.0 / den
        ob = (ea * o1_ref[...].astype(F32) + eb * _scr_get(so2) + ec * _scr_get(so3)) * inv
        _scr_put(so2, ob)
        _scr_put(sl2, m + jnp.log(den))
        for (_, dil), o_ref, l_ref in zip(BRANCHES, (ob1_ref, ob4_ref, ob16_ref), (ls1_ref, ls4_ref, ls16_ref)):
            _restride(so2, o_ref, dil, tm)
            _restride(sl2, l_ref, dil, tm)
        oav = oa_ref[...].astype(F32)
        ra = lax.rsqrt(jnp.mean(oav * oav, axis=-1, keepdims=True) + EPS)
        rb = lax.rsqrt(jnp.mean(ob * ob, axis=-1, keepdims=True) + EPS)
        mix_ref[:, :WIDTH] = (oav * ra * ga_ref[...]).astype(BF16)
        mix_ref[:, WIDTH:] = (ob * rb * gb_ref[...]).astype(BF16)
        x2 = x_ref[...] + jnp.dot(mix_ref[...], wo_ref[...], preferred_element_type=F32)
        x2_ref[...] = x2
        r2 = lax.rsqrt(jnp.mean(x2 * x2, axis=-1, keepdims=True) + EPS)
        h2_ref[...] = (x2 * r2 * gf_ref[...]).astype(BF16)

    row = lambda w_: pl.BlockSpec((tm, w_), lambda i: (i, 0))
    full = lambda a: pl.BlockSpec(a.shape, lambda i: (0, 0))
    return pl.pallas_call(
        body, name="merge_wo", grid=(T // tm,),
        in_specs=[row(D_MODEL), row(WIDTH)] + _view_specs(tm) * 2 + [full(ga), full(gb), full(wo), full(gf)],
        out_specs=[row(D_MODEL), row(D_MODEL), row(D_MODEL)] + _view_specs(tm) * 2,
        out_shape=[jax.ShapeDtypeStruct((T, D_MODEL), F32), jax.ShapeDtypeStruct((T, D_MODEL), BF16),
                   jax.ShapeDtypeStruct((T, D_MODEL), BF16)] + _view_shapes(T, BF16) + _view_shapes(T, F32),
        scratch_shapes=[_scr(tm)] * 4,
        compiler_params=_params(("arbitrary",)),
    )(x, oa, o1, o2, o3, l1, l2, l3, ga, gb, wo, gf)


def _ffn_up(h2, wgt, wut, *, tm=512, fc=D_FF, rc=512, cc=256):
    T = h2.shape[0]

    def body(h_ref, wg_ref, wu_ref, gate_ref, up_ref, act_ref):
        for s in range(0, tm, rc):
            h = h_ref[s:s + rc, :]
            for c in range(0, fc, cc):
                gt = lax.dot_general(h, wg_ref[c:c + cc, :], NT, preferred_element_type=F32)
                u = lax.dot_general(h, wu_ref[c:c + cc, :], NT, preferred_element_type=F32)
                gate_ref[s:s + rc, c:c + cc] = gt.astype(BF16)
                up_ref[s:s + rc, c:c + cc] = u.astype(BF16)
                act_ref[s:s + rc, c:c + cc] = (gt * _sigmoid(gt) * u).astype(BF16)

    rowd = pl.BlockSpec((tm, D_MODEL), lambda i, c: (i, 0))
    wrow = pl.BlockSpec((fc, D_MODEL), lambda i, c: (c, 0))
    oc = pl.BlockSpec((tm, fc), lambda i, c: (i, c))
    return pl.pallas_call(
        body, name="ffn_up", grid=(T // tm, D_FF // fc),
        in_specs=[rowd, wrow, wrow],
        out_specs=[oc, oc, oc],
        out_shape=[jax.ShapeDtypeStruct((T, D_FF), BF16)] * 3,
        compiler_params=_params(("arbitrary", "arbitrary")),
    )(h2, wgt, wut)


def _ffn_down_loss(act, wd, x2, tgt, g, *, tm=512, rc=256):
    T = x2.shape[0]

    def body(act_ref, wd_ref, x2_ref, tgt_ref, g_ref, dx_ref, dxb_ref, loss_ref, dg_ref):
        @pl.when(pl.program_id(0) == 0)
        def _():
            loss_ref[...] = jnp.zeros_like(loss_ref)
            dg_ref[...] = jnp.zeros_like(dg_ref)

        gv = g_ref[...]
        lsum = jnp.zeros((1, 1), F32)
        dgs = jnp.zeros((1, D_MODEL), F32)
        for c in range(0, tm, rc):
            x3 = x2_ref[c:c + rc, :] + jnp.dot(act_ref[c:c + rc, :], wd_ref[...], preferred_element_type=F32)
            r = lax.rsqrt(jnp.mean(x3 * x3, axis=-1, keepdims=True) + EPS)
            xh = x3 * r
            diff = xh * gv - tgt_ref[c:c + rc, :]
            lsum = lsum + jnp.sum(jnp.sum(diff * diff, axis=-1, keepdims=True), axis=0, keepdims=True)
            dy = diff * (1.0 / D_MODEL)
            dgs = dgs + jnp.sum(dy * xh, axis=0, keepdims=True)
            dx = _rms_bwd(dy, xh, r, gv)
            dx_ref[c:c + rc, :] = dx
            dxb_ref[c:c + rc, :] = dx.astype(BF16)
        loss_ref[...] += lsum * (0.5 / D_MODEL)
        dg_ref[...] += dgs

    rowd = pl.BlockSpec((tm, D_MODEL), lambda i: (i, 0))
    return pl.pallas_call(
        body, name="ffn_down_loss", grid=(T // tm,),
        in_specs=[pl.BlockSpec((tm, D_FF), lambda i: (i, 0)), pl.BlockSpec((D_FF, D_MODEL), lambda i: (0, 0)),
                  rowd, rowd, pl.BlockSpec(g.shape, lambda i: (0, 0))],
        out_specs=[rowd, rowd, pl.BlockSpec((1, 1), lambda i: (0, 0)), pl.BlockSpec((1, D_MODEL), lambda i: (0, 0))],
        out_shape=[jax.ShapeDtypeStruct((T, D_MODEL), F32), jax.ShapeDtypeStruct((T, D_MODEL), BF16),
                   jax.ShapeDtypeStruct((1, 1), F32), jax.ShapeDtypeStruct((1, D_MODEL), F32)],
        compiler_params=_params(("arbitrary",)),
    )(act, wd, x2, tgt, g)


def _ffn_bwd(dx3, gate, up, wd, wgt, wut, x2, g, *, tm=256, cc=256):
    T = x2.shape[0]

    def body(dx_ref, gate_ref, up_ref, wd_ref, wg_ref, wu_ref, x2_ref, g_ref,
             dgate_ref, dup_ref, dx2_ref, dx2b_ref, dg_ref):
        @pl.when(pl.program_id(0) == 0)
        def _():
            dg_ref[...] = jnp.zeros_like(dg_ref)

        dxb = dx_ref[...].astype(BF16)
        for c in range(0, D_FF, cc):
            dact = lax.dot_general(dxb, wd_ref[c:c + cc, :], NT, preferred_element_type=F32)
            gt = gate_ref[:, c:c + cc].astype(F32)
            u = up_ref[:, c:c + cc].astype(F32)
            sg = _sigmoid(gt)
            a = dact * sg
            dgate_ref[:, c:c + cc] = (a * u * ((1.0 + gt) - gt * sg)).astype(BF16)
            dup_ref[:, c:c + cc] = (a * gt).astype(BF16)
        dh = (jnp.dot(dgate_ref[...], wg_ref[...], preferred_element_type=F32)
              + jnp.dot(dup_ref[...], wu_ref[...], preferred_element_type=F32))
        xv = x2_ref[...]
        r = lax.rsqrt(jnp.mean(xv * xv, axis=-1, keepdims=True) + EPS)
        xh = xv * r
        dg_ref[...] += jnp.sum(dh * xh, axis=0, keepdims=True)
        d = dx_ref[...] + _rms_bwd(dh, xh, r, g_ref[...])
        dx2_ref[...] = d
        dx2b_ref[...] = d.astype(BF16)

    rowd = pl.BlockSpec((tm, D_MODEL), lambda i: (i, 0))
    rowf = pl.BlockSpec((tm, D_FF), lambda i: (i, 0))
    wfull = pl.BlockSpec((D_FF, D_MODEL), lambda i: (0, 0), pipeline_mode=pl.Buffered(1))
    return pl.pallas_call(
        body, name="ffn_bwd", grid=(T // tm,),
        in_specs=[rowd, rowf, rowf, wfull, wfull, wfull, rowd, pl.BlockSpec(g.shape, lambda i: (0, 0))],
        out_specs=[rowf, rowf, rowd, rowd, pl.BlockSpec((1, D_MODEL), lambda i: (0, 0))],
        out_shape=[jax.ShapeDtypeStruct((T, D_FF), BF16), jax.ShapeDtypeStruct((T, D_FF), BF16),
                   jax.ShapeDtypeStruct((T, D_MODEL), F32), jax.ShapeDtypeStruct((T, D_MODEL), BF16),
                   jax.ShapeDtypeStruct((1, D_MODEL), F32)],
        compiler_params=_params(("arbitrary",)),
    )(dx3, gate, up, wd, wgt, wut, x2, g)


def _matmul_tn(a, b, *, tk, tn, tt=2048, out_dtype=BF16, name):
    T, K = a.shape
    N = b.shape[1]
    nt = T // tt

    def body(a_ref, b_ref, o_ref, acc_ref):
        part = lax.dot_general(a_ref[...], b_ref[...], TN, preferred_element_type=F32)

        @pl.when(pl.program_id(2) == 0)
        def _():
            acc_ref[...] = part

        @pl.when(pl.program_id(2) > 0)
        def _():
            acc_ref[...] += part

        @pl.when(pl.program_id(2) == nt - 1)
        def _():
            o_ref[...] = acc_ref[...].astype(out_dtype)

    return pl.pallas_call(
        body, name=name, grid=(K // tk, N // tn, nt),
        in_specs=[pl.BlockSpec((tt, tk), lambda i, j, t: (t, i)), pl.BlockSpec((tt, tn), lambda i, j, t: (t, j))],
        out_specs=pl.BlockSpec((tk, tn), lambda i, j, t: (i, j)),
        out_shape=jax.ShapeDtypeStruct((K, N), out_dtype),
        scratch_shapes=[pltpu.VMEM((tk, tn), F32)],
        compiler_params=_params(("arbitrary", "arbitrary", "arbitrary")),
    )(a, b)


def _wo_bwd(dx2b, wo, oa, ob, ga, gb, *, tm=512):
    T = dx2b.shape[0]

    def body(dx_ref, wo_ref, oa_ref, ob_ref, ga_ref, gb_ref, doa_ref, dob1_ref, dob4_ref, dob16_ref, dga_ref, dgb_ref, scr):
        @pl.when(pl.program_id(0) == 0)
        def _():
            dga_ref[...] = jnp.zeros_like(dga_ref)
            dgb_ref[...] = jnp.zeros_like(dgb_ref)

        dm = lax.dot_general(dx_ref[...], wo_ref[...], NT, preferred_element_type=F32)
        for o_ref, g_ref, dg_ref, sl in ((oa_ref, ga_ref, dga_ref, slice(0, WIDTH)),
                                         (ob_ref, gb_ref, dgb_ref, slice(WIDTH, 2 * WIDTH))):
            ov = o_ref[...].astype(F32)
            r = lax.rsqrt(jnp.mean(ov * ov, axis=-1, keepdims=True) + EPS)
            xh = ov * r
            d = dm[:, sl]
            dg_ref[...] += jnp.sum(d * xh, axis=0, keepdims=True)
            do = _rms_bwd(d, xh, r, g_ref[...])
            if o_ref is oa_ref:
                doa_ref[...] = do.astype(BF16)
            else:
                _scr_put(scr, do)
                for (_, dil), v_ref in zip(BRANCHES, (dob1_ref, dob4_ref, dob16_ref)):
                    _restride(scr, v_ref, dil, tm)

    row = lambda w_: pl.BlockSpec((tm, w_), lambda i: (i, 0))
    full = lambda a: pl.BlockSpec(a.shape, lambda i: (0, 0))
    return pl.pallas_call(
        body, name="wo_bwd", grid=(T // tm,),
        in_specs=[row(D_MODEL), full(wo), row(WIDTH), row(WIDTH), full(ga), full(gb)],
        out_specs=[row(WIDTH)] + _view_specs(tm)
        + [pl.BlockSpec((1, WIDTH), lambda i: (0, 0)), pl.BlockSpec((1, WIDTH), lambda i: (0, 0))],
        out_shape=[jax.ShapeDtypeStruct((T, WIDTH), BF16)] + _view_shapes(T, BF16)
        + [jax.ShapeDtypeStruct((1, WIDTH), F32), jax.ShapeDtypeStruct((1, WIDTH), F32)],
        scratch_shapes=[_scr(tm)],
        compiler_params=_params(("arbitrary",)),
    )(dx2b, wo, oa, ob, ga, gb)


def _inproj_bwd(dqa, dka, dva, dqs, dks, dvs, cos, sin, w, x, dx2, g, *, tm=512):
    T = dqa.shape[0]

    def body(dqa_ref, dka_ref, dva_ref, q1, q2, q3, k1, k2, k3, v1, v2, v3, cos_ref, sin_ref, w_ref, x_ref, dx2_ref,
             g_ref, dp_ref, db_ref, gx_ref, dg_ref, acc, tmp):
        @pl.when(pl.program_id(0) == 0)
        def _():
            db_ref[...] = jnp.zeros_like(db_ref)
            dg_ref[...] = jnp.zeros_like(dg_ref)

        cosv = cos_ref[...]
        sinv = sin_ref[...]
        lane = lax.broadcasted_iota(jnp.int32, (tm, PAIR), 1)
        first = (lane % HEAD_DIM) < (HEAD_DIM // 2)

        def put(off, val):
            dp_ref[:, off:off + PAIR] = val.astype(BF16)
            db_ref[:, off:off + PAIR] += jnp.sum(val, axis=0, keepdims=True)

        for src, off, width in ((dqa_ref, 0, 512), (dka_ref, 512, 256)):
            for j in range(0, width, PAIR):
                d = src[:, j:j + PAIR].astype(F32)
                put(off + j, d * cosv - _rope_rot(d, first) * sinv)
        for j in range(0, 256, PAIR):
            put(768 + j, dva_ref[:, j:j + PAIR].astype(F32))
        for (a, b, c), off in (((q1, q2, q3), 1024), ((k1, k2, k3), 1536), ((v1, v2, v3), 2048)):
            _unstride(b, acc, BRANCHES[1][1], tm)
            _unstride(c, tmp, BRANCHES[2][1], tm)
            for j in range(N_CHUNK):
                put(off + j * PAIR, a[:, j * PAIR:(j + 1) * PAIR].astype(F32) + acc[j] + tmp[j])

        dh = jnp.dot(dp_ref[...], w_ref[...], preferred_element_type=F32)
        xv = x_ref[...]
        r = lax.rsqrt(jnp.mean(xv * xv, axis=-1, keepdims=True) + EPS)
        xh = xv * r
        dg_ref[...] += jnp.sum(dh * xh, axis=0, keepdims=True)
        gx_ref[...] = dx2_ref[...] + _rms_bwd(dh, xh, r, g_ref[...])

    row = lambda w_: pl.BlockSpec((tm, w_), lambda i: (i, 0))
    full = lambda a: pl.BlockSpec(a.shape, lambda i: (0, 0))
    return pl.pallas_call(
        body, name="inproj_bwd", grid=(T // tm,),
        in_specs=[row(512), row(256), row(256)] + _view_specs(tm) * 3 + [row(PAIR), row(PAIR)]
        + [full(w), row(D_MODEL), row(D_MODEL), full(g)],
        out_specs=[row(D_INP), pl.BlockSpec((1, D_INP), lambda i: (0, 0)), row(D_MODEL),
                   pl.BlockSpec((1, D_MODEL), lambda i: (0, 0))],
        out_shape=[jax.ShapeDtypeStruct((T, D_INP), BF16), jax.ShapeDtypeStruct((1, D_INP), F32),
                   jax.ShapeDtypeStruct((T, D_MODEL), F32), jax.ShapeDtypeStruct((1, D_MODEL), F32)],
        scratch_shapes=[_scr(tm)] * 2,
        compiler_params=_params(("arbitrary",)),
    )(dqa, dka, dva, *dqs, *dks, *dvs, cos, sin, w, x, dx2, g)


def _bias_sink_grads(dsums, bmaps, dsk):
    def body(s1, s2, s3, m1, m2, m3, dsk_ref, drel_ref, dsink_ref):
        row = lax.broadcasted_iota(jnp.int32, (N_HEADS, 128), 0)
        lane = lax.broadcasted_iota(jnp.int32, (N_HEADS, 128), 1)
        out = jnp.zeros((N_HEADS, 128), F32)
        for s_ref, m_ref in ((s1, m1), (s2, m2), (s3, m3)):
            bm = m_ref[...]
            for h in range(N_HEADS):
                a = s_ref[h]
                for b in range(REL_BUCKETS):
                    v = jnp.sum(jnp.sum(jnp.where(bm == b, a, 0.0), axis=-1, keepdims=True), axis=0, keepdims=True)
                    out = out + jnp.where((row == h) & (lane == b), v, 0.0)
        drel_ref[...] = out
        dsink_ref[...] = -jnp.sum(dsk_ref[...], axis=0, keepdims=True)

    vm = pl.BlockSpec(memory_space=pltpu.VMEM)
    return pl.pallas_call(
        body, name="bias_sink_grads",
        in_specs=[vm] * 7, out_specs=[vm, vm],
        out_shape=[jax.ShapeDtypeStruct((N_HEADS, 128), F32), jax.ShapeDtypeStruct((1, WIDTH), F32)],
        compiler_params=_params(),
    )(*dsums, *bmaps, dsk)


def _all_gather(blk, *, name):
    R, C = blk.shape

    def body(x_ref, out_ref, send_sems, recv_sems, local_sem):
        x, y, c = lax.axis_index("x"), lax.axis_index("y"), lax.axis_index("c")
        me, sibling = (x, y, c), (x, y, 1 - c)
        chips = [(1 - x, y), (x, 1 - y), (1 - x, 1 - y)]

        def slot(px, py, pc):
            return out_ref.at[4 * px + 2 * py + pc]

        def copy(k, block, to, src=None):
            return pltpu.make_async_remote_copy(
                src_ref=slot(*block) if src is None else src, dst_ref=slot(*block),
                send_sem=send_sems.at[k], recv_sem=recv_sems.at[k], device_id=to, device_id_type=MESH)

        mine = pltpu.make_async_copy(x_ref, slot(*me), local_sem)
        mine.start()
        first = [copy(0, me, sibling, src=x_ref)]
        first += [copy(1 + j, me, (*chip, c), src=x_ref) for j, chip in enumerate(chips)]
        for cp in first:
            cp.start()
        passed = [copy(4 + j, (*chip, c), sibling) for j, chip in enumerate(chips)]
        for j, chip in enumerate(chips):
            copy(1 + j, (*chip, c), me).wait_recv()
            passed[j].start()
        copy(0, sibling, me).wait_recv()
        for j, chip in enumerate(chips):
            copy(4 + j, (*chip, 1 - c), me).wait_recv()
        for cp in first + passed:
            cp.wait_send()
        mine.wait()

    return pl.pallas_call(
        body, name=name,
        in_specs=[pl.BlockSpec(memory_space=pl.ANY)], out_specs=pl.BlockSpec(memory_space=pl.ANY),
        out_shape=jax.ShapeDtypeStruct((N_DEV, R, C), blk.dtype),
        scratch_shapes=[pltpu.SemaphoreType.DMA((7,)), pltpu.SemaphoreType.DMA((7,)), pltpu.SemaphoreType.DMA],
        compiler_params=pltpu.CompilerParams(has_side_effects=True),
    )(blk)


def _peers(x, y, c):
    return [(x ^ (k >> 2), y ^ ((k >> 1) & 1), c ^ (k & 1)) for k in range(1, N_DEV)]


_HBM = pl.BlockSpec(memory_space=pltpu.HBM)
_SEM = pl.BlockSpec(memory_space=pltpu.SEMAPHORE)
_EFFECT = pltpu.SideEffectType.DATAFLOW_SIDE_EFFECTING


def _peer_list(x, y, c, near):
    if near:
        return [(x, y, 1 - c), (1 - x, y, c), (x, 1 - y, c), (1 - x, 1 - y, c)]
    return _peers(x, y, c)


def _exchange_start(srcs, *, gather, name, near=False):
    n = len(srcs)
    n_peers = 4 if near else N_DEV - 1
    lands = [lax.empty((N_DEV,) + s.shape[-2:], s.dtype) for s in srcs]

    def body(*refs):
        src_refs, land_refs = refs[:n], refs[n:2 * n]
        send_sems, recv_sems = refs[2 * n], refs[2 * n + 1]
        token = refs[-1]
        x, y, c = lax.axis_index("x"), lax.axis_index("y"), lax.axis_index("c")
        mine = 4 * x + 2 * y + c
        for a in range(n):
            for k, peer in enumerate(_peer_list(x, y, c, near)):
                dest = 4 * peer[0] + 2 * peer[1] + peer[2]
                j = a * n_peers + k
                pltpu.make_async_remote_copy(
                    src_ref=src_refs[a] if gather else src_refs[a].at[dest], dst_ref=land_refs[a].at[mine],
                    send_sem=send_sems.at[j], recv_sem=recv_sems.at[j], device_id=peer, device_id_type=MESH).start()
        token[...] = jnp.zeros_like(token)

    sems = pltpu.SemaphoreType.DMA((n * n_peers,))
    out = pl.pallas_call(
        body, name=name,
        out_shape=(sems, sems) + tuple(pltpu.HBM(a.shape, a.dtype) for a in list(srcs) + lands)
        + (jax.ShapeDtypeStruct((8, 128), F32),),
        in_specs=(_HBM,) * (2 * n), out_specs=(_SEM, _SEM) + (_HBM,) * (2 * n) + (pl.BlockSpec(memory_space=pltpu.VMEM),),
        input_output_aliases={i: 2 + i for i in range(2 * n)},
        compiler_params=pltpu.CompilerParams(has_side_effects=_EFFECT),
    )(*[pltpu.with_memory_space_constraint(a, pltpu.HBM) for a in list(srcs) + lands])
    return out[:-1], out[-1]


def _exchange_wait(state, after, *, gather, name, near=False):
    send_sems, recv_sems = state[0], state[1]
    n = (len(state) - 2) // 2
    n_peers = 4 if near else N_DEV - 1
    arrays = state[2:]

    def body(*refs):
        src_refs, land_refs = refs[:n], refs[n:2 * n]
        send_sems, recv_sems = refs[2 * n], refs[2 * n + 1]
        x, y, c = lax.axis_index("x"), lax.axis_index("y"), lax.axis_index("c")
        for a in range(n):
            for k, peer in enumerate(_peer_list(x, y, c, near)):
                other = 4 * peer[0] + 2 * peer[1] + peer[2]
                j = a * n_peers + k
                copy = pltpu.make_async_remote_copy(
                    src_ref=src_refs[a] if gather else src_refs[a].at[other], dst_ref=land_refs[a].at[other],
                    send_sem=send_sems.at[j], recv_sem=recv_sems.at[j], device_id=peer, device_id_type=MESH)
                copy.wait_send()
                copy.wait_recv()

    out = pl.pallas_call(
        body, name=name,
        out_shape=tuple(pltpu.HBM(a.shape, a.dtype) for a in arrays),
        in_specs=(_HBM,) * (2 * n) + (_SEM, _SEM, pl.BlockSpec(memory_space=pl.ANY)), out_specs=(_HBM,) * (2 * n),
        input_output_aliases={i: i for i in range(2 * n)},
        compiler_params=pltpu.CompilerParams(has_side_effects=_EFFECT),
    )(*arrays, send_sems, recv_sems, after)
    mine = 4 * lax.axis_index("x") + 2 * lax.axis_index("y") + lax.axis_index("c")
    own = out[:n] if gather else [lax.dynamic_index_in_dim(s, mine, 0, keepdims=False) for s in out[:n]]
    return [lax.dynamic_update_slice(g, o[None], (mine, 0, 0)) for g, o in zip(out[n:], own)]


def _forward_start(lands, *, name):
    n = len(lands)

    def body(*refs):
        land_refs, send_sems, recv_sems, token = refs[:n], refs[n], refs[n + 1], refs[-1]
        x, y, c = lax.axis_index("x"), lax.axis_index("y"), lax.axis_index("c")
        for a in range(n):
            for j, (px, py) in enumerate(((1 - x, y), (x, 1 - y), (1 - x, 1 - y))):
                blk = 4 * px + 2 * py + c
                pltpu.make_async_remote_copy(
                    src_ref=land_refs[a].at[blk], dst_ref=land_refs[a].at[blk], send_sem=send_sems.at[3 * a + j],
                    recv_sem=recv_sems.at[3 * a + j], device_id=(x, y, 1 - c), device_id_type=MESH).start()
        token[...] = jnp.zeros_like(token)

    sems = pltpu.SemaphoreType.DMA((3 * n,))
    out = pl.pallas_call(
        body, name=name,
        out_shape=(sems, sems) + tuple(pltpu.HBM(a.shape, a.dtype) for a in lands) + (jax.ShapeDtypeStruct((8, 128), F32),),
        in_specs=(_HBM,) * n, out_specs=(_SEM, _SEM) + (_HBM,) * n + (pl.BlockSpec(memory_space=pltpu.VMEM),),
        input_output_aliases={i: 2 + i for i in range(n)},
        compiler_params=pltpu.CompilerParams(has_side_effects=_EFFECT),
    )(*[pltpu.with_memory_space_constraint(a, pltpu.HBM) for a in lands])
    return out[:-1], out[-1]


def _forward_wait(state, after, *, name):
    send_sems, recv_sems = state[0], state[1]
    lands = state[2:]
    n = len(lands)

    def body(*refs):
        land_refs, send_sems, recv_sems = refs[:n], refs[n], refs[n + 1]
        x, y, c = lax.axis_index("x"), lax.axis_index("y"), lax.axis_index("c")
        for a in range(n):
            for j, (px, py) in enumerate(((1 - x, y), (x, 1 - y), (1 - x, 1 - y))):
                copy = pltpu.make_async_remote_copy(
                    src_ref=land_refs[a].at[4 * px + 2 * py + c], dst_ref=land_refs[a].at[4 * px + 2 * py + 1 - c],
                    send_sem=send_sems.at[3 * a + j], recv_sem=recv_sems.at[3 * a + j], device_id=(x, y, 1 - c),
                    device_id_type=MESH)
                copy.wait_send()
                copy.wait_recv()

    return pl.pallas_call(
        body, name=name,
        out_shape=tuple(pltpu.HBM(a.shape, a.dtype) for a in lands),
        in_specs=(_HBM,) * n + (_SEM, _SEM, pl.BlockSpec(memory_space=pl.ANY)), out_specs=(_HBM,) * n,
        input_output_aliases={i: i for i in range(n)},
        compiler_params=pltpu.CompilerParams(has_side_effects=_EFFECT),
    )(*lands, send_sems, recv_sems, after)


def _adam_math(w, g, m, v):
    m = ADAM_B1 * m + (1.0 - ADAM_B1) * g
    v = ADAM_B2 * v + (1.0 - ADAM_B2) * (g * g)
    m_hat = m / (1.0 - ADAM_B1 ** ADAM_STEP)
    v_hat = v / (1.0 - ADAM_B2 ** ADAM_STEP)
    delta = -ADAM_LR * (m_hat / (jnp.sqrt(v_hat) + ADAM_EPS) + ADAM_WD * w)
    return delta, m, v


def _adamw(parts, w, m, v, *, name):
    R, C = w.shape
    n_parts = parts.shape[0]
    tr = R // 2
    assert tr % 16 == 0

    def body(p_ref, w_ref, m_ref, v_ref, g_ref, d_ref, nm_ref, nv_ref):
        g = p_ref[0].astype(F32)
        for s in range(1, n_parts):
            g = g + p_ref[s].astype(F32)
        d, nm, nv = _adam_math(w_ref[...], g, m_ref[...], v_ref[...])
        g_ref[...] = g
        d_ref[...] = d
        nm_ref[...] = nm
        nv_ref[...] = nv

    blk = pl.BlockSpec((tr, C), lambda i: (i, 0))
    return pl.pallas_call(
        body, name=name, grid=(R // tr,),
        in_specs=[pl.BlockSpec((n_parts, tr, C), lambda i: (0, i, 0)), blk, blk, blk],
        out_specs=[blk] * 4, out_shape=[jax.ShapeDtypeStruct((R, C), F32)] * 4,
        compiler_params=_params(("arbitrary",)),
    )(parts, w, m, v)


def _adamw_small(parts, w, m, v):
    def body(p_ref, w_ref, m_ref, v_ref, g_ref, d_ref, nm_ref, nv_ref):
        g = p_ref[0]
        for s in range(1, N_DEV):
            g = g + p_ref[s]
        d, nm, nv = _adam_math(w_ref[...], g, m_ref[...], v_ref[...])
        g_ref[...] = g
        d_ref[...] = d
        nm_ref[...] = nm
        nv_ref[...] = nv

    vm = pl.BlockSpec(memory_space=pltpu.VMEM)
    return pl.pallas_call(
        body, name="adamw_small", in_specs=[vm] * 4, out_specs=[vm] * 4,
        out_shape=[jax.ShapeDtypeStruct((SMALL_ROWS, 128), F32)] * 4, compiler_params=_params(),
    )(parts, w, m, v)


def _t5_bucket(dist):
    max_exact = REL_BUCKETS // 2
    df = jnp.maximum(dist, 1).astype(F32)
    large = max_exact + (jnp.log(df / max_exact) / math.log(REL_MAX_DISTANCE / max_exact)
                         * (REL_BUCKETS - max_exact)).astype(jnp.int32)
    large = jnp.minimum(large, REL_BUCKETS - 1)
    return jnp.where(dist < max_exact, dist, large)


def _band_tables(rel_table, dil, n_back):
    qi = jnp.arange(BLK)[:, None]
    kj = jnp.arange(2 * BLK)[None, :]
    delta = BLK + qi - kj
    in_band = (delta >= 0) & (delta <= n_back)
    if rel_table is None:
        vals = jnp.zeros((N_HEADS, BLK, 2 * BLK), F32)
        bmap = None
    else:
        bucket = _t5_bucket(jnp.clip(delta, 0, n_back) * dil)
        vals = jnp.zeros((N_HEADS, BLK, 2 * BLK), F32)
        for b in range(REL_BUCKETS):
            vals = jnp.where((bucket == b)[None], rel_table[b][:, None, None], vals)
        bmap = jnp.where(in_band, bucket, -1).astype(jnp.int32)
    later = jnp.where(in_band[None], vals, NEG)
    first = jnp.where((in_band & (kj >= BLK))[None], vals, NEG)
    return jnp.stack([later, first]), bmap


def _rope_tables(T):
    half = HEAD_DIM // 2
    inv_freq = ROPE_THETA ** (-jnp.arange(half, dtype=F32) / half)
    ang = jnp.arange(T, dtype=F32)[:, None] * inv_freq[None, :]
    cos, sin = jnp.cos(ang), jnp.sin(ang)
    return jnp.tile(cos, (1, 4)), jnp.tile(jnp.concatenate([-sin, sin], axis=1), (1, 2))


def _widen_in(a, axis):
    sl = lambda lo, hi: lax.slice_in_dim(a, lo, hi, axis=axis)
    dup = lambda lo: [sl(lo, lo + 64), sl(lo, lo + 64), sl(lo + 64, lo + 128), sl(lo + 64, lo + 128)]
    return jnp.concatenate([sl(0, 512)] + dup(512) + dup(640) + [sl(768, D_IN)], axis=axis)


def _fold_in(a, axis):
    sl = lambda lo, hi: lax.slice_in_dim(a, lo, hi, axis=axis)
    fold = lambda lo: [sl(lo, lo + 64) + sl(lo + 64, lo + 128), sl(lo + 128, lo + 192) + sl(lo + 192, lo + 256)]
    return jnp.concatenate([sl(0, 512)] + fold(512) + fold(768) + [sl(1024, D_INP)], axis=axis)


def _local_step(x, tgt, g_attn, b_in, sinks, rel_table, g_out_a, g_out_b, g_ffn, g_final,
                win_fn, wo_fn, ffn_fn, early_fn):
    T = x.shape[0]
    cos, sin = _rope_tables(T)
    g_final2 = g_final.reshape(1, D_MODEL)
    sink8 = sinks.reshape(N_HEADS)

    bias_a, _ = _band_tables(None, 1, BLK - 1)
    tabs = [_band_tables(rel_table, dil, window // dil) for window, dil in BRANCHES]
    wint, token = win_fn(tabs[2][0])
    winp = _widen_in(wint, 0)
    binp = _widen_in(b_in, 1) + token[0, 0]

    h1, qa, ka, va, *qkv_b = _norm_proj(x, g_attn, winp, binp, cos, sin)
    qbs, kbs, vbs = qkv_b[0:3], qkv_b[3:6], qkv_b[6:9]
    oa, lse_a = _attn_fwd(qa, ka, va, bias_a, sink8, dil=1, kv_pairs=2, use_sink=True, name="attn_a_fwd")
    outs = [_attn_fwd(qbs[n], kbs[n], vbs[n], tabs[n][0], sink8, dil=dil, kv_pairs=4, use_sink=False,
                      name=f"attn_b{n}_fwd") for n, (_, dil) in enumerate(BRANCHES)]
    wo = wo_fn(outs[2][1])
    x2, mixed, h2, *ob_lse = _merge_wo(x, oa, outs[0][0], outs[1][0], outs[2][0], outs[0][1], outs[1][1], outs[2][1],
                                       g_out_a, g_out_b, wo, g_ffn)
    obs, lses = ob_lse[0:3], ob_lse[3:6]
    wgt, wut, wd = ffn_fn(h2)
    gate, up, act = _ffn_up(h2, wgt, wut)
    dx3, dx3b, loss, dg_final = _ffn_down_loss(act, wd, x2, tgt, g_final2)

    dgate, dup, dx2, dx2b, dg_ffn = _ffn_bwd(dx3, gate, up, wd, wgt, wut, x2, g_ffn)
    dwd = _matmul_tn(act, dx3b, tk=1408, tn=1024, name="dw_down")
    dwgt = _matmul_tn(dgate, h2, tk=1408, tn=1024, name="dw_gate")
    dwut = _matmul_tn(dup, h2, tk=1408, tn=1024, name="dw_up")
    dwo = _matmul_tn(mixed, dx2b, tk=1024, tn=1024, name="dw_o")
    early, token2 = early_fn(dict(w_o=dwo, w_gate=dwgt, w_up=dwut, w_down=dwd))
    doa, *dobs, dg_out_a, dg_out_b = _wo_bwd(dx2b, wo, oa, obs[0], g_out_a + token2[0, 0], g_out_b)

    dqa, dka, dva, _, dsk = _attn_bwd(qa, ka, va, oa, doa, lse_a, bias_a, sink8, dil=1, kv_pairs=2, use_sink=True,
                                      name="attn_a_bwd")
    res = [_attn_bwd(qbs[n], kbs[n], vbs[n], obs[n], dobs[n], lses[n], tabs[n][0], sink8, dil=dil, kv_pairs=4,
                     use_sink=False, name=f"attn_b{n}_bwd") for n, (_, dil) in enumerate(BRANCHES)]
    dp, dbp, grad_x, dg_attn = _inproj_bwd(dqa, dka, dva, [r[0] for r in res], [r[1] for r in res],
                                           [r[2] for r in res], cos, sin, winp, x, dx2, g_attn)
    dwin = _fold_in(_matmul_tn(dp, h1, tk=1280, tn=1024, out_dtype=F32, name="dw_in"), 0)
    drel, dsink = _bias_sink_grads([r[3] for r in res], [t[1] for t in tabs], dsk)

    small = dict(
        g_attn=dg_attn, b_in=_fold_in(dbp, 1), sinks=dsink[:, ::HEAD_DIM], rel_table=drel[:, :REL_BUCKETS].T,
        g_out_a=dg_out_a, g_out_b=dg_out_b, g_ffn=dg_ffn, g_final=dg_final.reshape(D_MODEL))
    return loss[0, 0], grad_x, dwin, early, small


SMALL_NAMES = ("g_attn", "b_in", "sinks", "rel_table", "g_out_a", "g_out_b", "g_ffn", "g_final", "loss")


def _pack_small(vals):
    flat = jnp.concatenate([vals[n].reshape(-1).astype(F32) for n in SMALL_NAMES])
    return jnp.pad(flat, (0, SMALL_ROWS * 128 - flat.shape[0])).reshape(SMALL_ROWS, 128)


def _unpack_small(packed, like):
    flat = packed.reshape(-1)
    out, off = {}, 0
    for n in SMALL_NAMES:
        size = like[n].size
        out[n] = flat[off:off + size].reshape(like[n].shape)
        off += size
    return out


def kernel(x, g_attn, w_in, b_in, sinks, rel_table, g_out_a, g_out_b, w_o, g_ffn, w_gate, w_up, w_down, g_final, loss_target, m_g_attn, m_w_in, m_b_in, m_sinks, m_rel_table, m_g_out_a, m_g_out_b, m_w_o, m_g_ffn, m_w_gate, m_w_up, m_w_down, m_g_final, v_g_attn, v_w_in, v_b_in, v_sinks, v_rel_table, v_g_out_a, v_g_out_b, v_w_o, v_g_ffn, v_w_gate, v_w_up, v_w_down, v_g_final):
    rest_names = ("w_o", "w_gate", "w_up", "w_down")

    rest = [w_o[0].astype(BF16), w_gate[0].astype(BF16).T, w_up[0].astype(BF16).T, w_down[0].astype(BF16)]
    in_state, _ = _exchange_start([w_in[0].astype(BF16).T], gather=True, near=True, name="gather_w_in_start")
    later = {}

    def whole(got):
        return [g.reshape(N_DEV * g.shape[1], D_MODEL) for g in got]

    def win_fn(after):
        near = _exchange_wait(in_state, after, gather=True, near=True, name="gather_w_in_near")
        fwd_state, tok = _forward_start(near, name="gather_w_in_forward")
        wint = whole(_forward_wait(fwd_state, tok, name="gather_w_in_wait"))[0]
        wint, src = lax.optimization_barrier((wint, rest))
        later["wo"], token_o = _exchange_start(src[:1], gather=True, name="gather_w_o_start")
        token_o, ffn_src = lax.optimization_barrier((token_o, src[1:]))
        later["ffn"], token = _exchange_start(ffn_src, gather=True, name="gather_ffn_start")
        return wint, token + token_o

    def wo_fn(after):
        return whole(_exchange_wait(later["wo"], after, gather=True, name="gather_w_o_wait"))[0]

    def ffn_fn(after):
        return whole(_exchange_wait(later["ffn"], after, gather=True, name="gather_ffn_wait"))

    def early_fn(dws):
        return _exchange_start([dws[n].reshape(N_DEV, -1, D_MODEL) for n in rest_names], gather=False,
                               name="scatter_rest_start")

    loss_part, grad_x, dwint, early_state, small = _local_step(
        x[0], loss_target[0], g_attn, b_in, sinks, rel_table, g_out_a, g_out_b, g_ffn, g_final,
        win_fn, wo_fn, ffn_fn, early_fn)
    parts_in = dwint.astype(BF16).reshape(N_DEV, D_IN // N_DEV, D_MODEL)
    in_state, token3 = _exchange_start([parts_in], gather=False, name="scatter_w_in_start")
    got = _exchange_wait(early_state, token3, gather=False, name="scatter_rest_wait")

    def update(n, parts, w, m, v, transposed):
        if transposed:
            return [a.T[None] for a in _adamw(parts, w[0].T, m[0].T, v[0].T, name="adamw_" + n)]
        return [a[None] for a in _adamw(parts, w[0], m[0], v[0], name="adamw_" + n)]

    big = dict(w_o=update("w_o", got[0], w_o, m_w_o, v_w_o, False),
               w_gate=update("w_gate", got[1], w_gate, m_w_gate, v_w_gate, True),
               w_up=update("w_up", got[2], w_up, m_w_up, v_w_up, True),
               w_down=update("w_down", got[3], w_down, m_w_down, v_w_down, False))

    unused = jnp.zeros((1,), F32)
    ws = dict(g_attn=g_attn, b_in=b_in, sinks=sinks, rel_table=rel_table, g_out_a=g_out_a, g_out_b=g_out_b,
              g_ffn=g_ffn, g_final=g_final, loss=unused)
    ms = dict(g_attn=m_g_attn, b_in=m_b_in, sinks=m_sinks, rel_table=m_rel_table, g_out_a=m_g_out_a,
              g_out_b=m_g_out_b, g_ffn=m_g_ffn, g_final=m_g_final, loss=unused)
    vs = dict(g_attn=v_g_attn, b_in=v_b_in, sinks=v_sinks, rel_table=v_rel_table, g_out_a=v_g_out_a,
              g_out_b=v_g_out_b, g_ffn=v_g_ffn, g_final=v_g_final, loss=unused)
    sparts = _all_gather(_pack_small(dict(small, loss=loss_part)), name="gather_small")
    sm_packed = _adamw_small(sparts, _pack_small(ws), _pack_small(ms), _pack_small(vs))
    sm = [_unpack_small(a, ws) for a in sm_packed]
    loss = sm[0]["loss"][0]

    done = sm_packed[1][:1, :1] + sum(big[n][1][0, :1, :1] for n in rest_names)
    got_in = _exchange_wait(in_state, done, gather=False, name="scatter_w_in_wait")[0]
    big["w_in"] = update("w_in", got_in, w_in, m_w_in, v_w_in, True)

    order = ("g_attn", "w_in", "b_in", "sinks", "rel_table", "g_out_a", "g_out_b", "w_o", "g_ffn", "w_gate", "w_up",
             "w_down", "g_final")
    outs = [loss, grad_x[None]]
    for k in range(4):
        outs += [big[n][k] if n in big else sm[k][n] for n in order]
    return tuple(outs)
```

```python
import functools
import math

import jax
import jax.numpy as jnp
from jax import lax
from jax.experimental import pallas as pl
from jax.experimental.pallas import tpu as pltpu

F32 = jnp.float32
BF16 = jnp.bfloat16

N_DEV = 8
D_MODEL = 1024
HEAD_DIM = 64
N_HEADS = 8
PAIR = 2 * HEAD_DIM
WIDTH = N_HEADS * HEAD_DIM
D_IN = 2304
D_INP = 2560
D_FF = 2816
BLK = 128
ROPE_THETA = 150000.0
REL_BUCKETS = 32
REL_MAX_DISTANCE = 2048
EPS = 1e-5
NEG = -1e30
BRANCHES = ((128, 1), (512, 4), (2048, 16))
Q_SCALE = HEAD_DIM ** -0.5

ADAM_LR = 0.001
ADAM_B1 = 0.9
ADAM_B2 = 0.999
ADAM_EPS = 1e-08
ADAM_WD = 0.01
ADAM_STEP = 10

VMEM_LIMIT = 56 * 1024 * 1024
MESH = pl.DeviceIdType.MESH

NT = (((1,), (1,)), ((), ()))
TN = (((0,), (0,)), ((), ()))

SMALL_ROWS = 56


def _params(sem=None):
    return pltpu.CompilerParams(dimension_semantics=sem, vmem_limit_bytes=VMEM_LIMIT)


def _sigmoid(x):
    return 1.0 / (1.0 + jnp.exp2(x * (-1.0 / math.log(2.0))))


def _rms_bwd(dh, xh, r, g):
    u = dh * g
    return r * (u - xh * jnp.mean(u * xh, axis=-1, keepdims=True))


def _rope_rot(t, first):
    return jnp.where(first, pltpu.roll(t, 96, 1), pltpu.roll(t, 32, 1))


N_CHUNK = WIDTH // PAIR


def _scr(tm):
    return pltpu.VMEM((N_CHUNK, tm, PAIR), F32)


def _scr_get(scr):
    return jnp.concatenate([scr[j] for j in range(N_CHUNK)], axis=1)


def _scr_put(scr, val):
    for j in range(N_CHUNK):
        scr[j] = val[:, j * PAIR:(j + 1) * PAIR]


def _unstride(view_ref, scr, dil, tm):
    n = tm // dil
    chunks = scr.shape[0]
    for r in range(dil):
        for j in range(chunks):
            col = (r * chunks + j) * PAIR
            scr.at[j][pl.ds(r, n, stride=dil), :] = view_ref[:, col:col + PAIR].astype(F32)


def _restride(scr, out_ref, dil, tm):
    n = tm // dil
    chunks = scr.shape[0]
    for r in range(dil):
        for j in range(chunks):
            col = (r * chunks + j) * PAIR
            rows = scr[j] if dil == 1 else scr.at[j][pl.ds(r, n, stride=dil), :]
            out_ref[:, col:col + PAIR] = rows.astype(out_ref.dtype)


def _view_specs(tm, width=WIDTH):
    return [pl.BlockSpec((tm // dil, dil * width), lambda i: (i, 0)) for _, dil in BRANCHES]


def _view_shapes(T, dtype, width=WIDTH):
    return [jax.ShapeDtypeStruct((T // dil, dil * width), dtype) for _, dil in BRANCHES]


def _norm_proj(x, g, w, b, cos, sin, *, tm=512):
    T = x.shape[0]

    def body(x_ref, g_ref, w_ref, b_ref, cos_ref, sin_ref, h_ref, qa_ref, ka_ref, va_ref, *rest):
        outs_b, ys = rest[:9], rest[9]
        xv = x_ref[...]
        r = lax.rsqrt(jnp.mean(xv * xv, axis=-1, keepdims=True) + EPS)
        h = (xv * r * g_ref[...]).astype(BF16)
        h_ref[...] = h
        cosv = cos_ref[...]
        sinv = sin_ref[...]
        lane = lax.broadcasted_iota(jnp.int32, (tm, PAIR), 1)
        first = (lane % HEAD_DIM) < (HEAD_DIM // 2)

        def proj(off):
            return (lax.dot_general(h, w_ref[off:off + 256, :], NT, preferred_element_type=F32)
                    + b_ref[:, off:off + 256])

        for (off, width, rot, scale), o_ref in zip(((0, 512, True, Q_SCALE), (512, 256, True, 1.0), (768, 256, False, 1.0)),
                                                   (qa_ref, ka_ref, va_ref)):
            for c in range(0, width, 256):
                y = proj(off + c)
                for j in range(0, 256, PAIR):
                    t = y[:, j:j + PAIR]
                    if rot:
                        t = t * cosv + _rope_rot(t, first) * sinv
                    if scale != 1.0:
                        t = t * scale
                    o_ref[:, c + j:c + j + PAIR] = t.astype(BF16)
        for n, (off, scale) in enumerate(((1024, Q_SCALE), (1536, 1.0), (2048, 1.0))):
            for c in range(0, WIDTH, 256):
                y = proj(off + c)
                y = y * scale if scale != 1.0 else y
                for j in range(0, 256, PAIR):
                    ys[(c + j) // PAIR] = y[:, j:j + PAIR]
            for (_, dil), o_ref in zip(BRANCHES, outs_b[3 * n:3 * n + 3]):
                _restride(ys, o_ref, dil, tm)

    row = lambda w_: pl.BlockSpec((tm, w_), lambda i: (i, 0))
    full = lambda a: pl.BlockSpec(a.shape, lambda i: (0, 0))
    return pl.pallas_call(
        body, name="norm_proj", grid=(T // tm,),
        in_specs=[row(D_MODEL), full(g), full(w), full(b), row(PAIR), row(PAIR)],
        out_specs=[row(D_MODEL), row(512), row(256), row(256)] + _view_specs(tm) * 3,
        out_shape=[jax.ShapeDtypeStruct((T, n), BF16) for n in (D_MODEL, 512, 256, 256)] + _view_shapes(T, BF16) * 3,
        scratch_shapes=[_scr(tm)],
        compiler_params=_params(("arbitrary",)),
    )(x, g, w, b, cos, sin)


SUB = 4
AHEAD = 2


def _attn_specs(kvw):
    q_spec = pl.BlockSpec((SUB * BLK, WIDTH), lambda r, i: (i, r))
    kc_spec = pl.BlockSpec((SUB * BLK, kvw), lambda r, i: (i, r))
    kp_spec = pl.BlockSpec((BLK, kvw), lambda r, i: (jnp.maximum(SUB * i - 1, 0), r))
    b_spec = pl.BlockSpec((2, N_HEADS, BLK, 2 * BLK), lambda r, i: (0, 0, 0, 0))
    return q_spec, kp_spec, kc_spec, b_spec


def _window(prev_ref, cur_ref, j, ksl):
    before = prev_ref[:, ksl] if j == 0 else cur_ref[(j - 1) * BLK:j * BLK, ksl]
    return jnp.concatenate([before, cur_ref[j * BLK:(j + 1) * BLK, ksl]], axis=0)


def _attn_fwd(q, k, v, bias, sinks, *, dil, kv_pairs, use_sink, name):
    L = q.shape[0]
    ns = L // (SUB * BLK)
    kvw = kv_pairs * PAIR
    rep = 4 // kv_pairs

    def body(sink_ref, q_ref, kp_ref, kc_ref, vp_ref, vc_ref, b_ref, o_ref, lse_ref):
        lane = lax.broadcasted_iota(jnp.int32, (1, PAIR), 1)
        lo = lane < HEAD_DIM
        first = jnp.where(pl.program_id(1) == 0, 1, 0)
        def scores(j, hp):
            rows = slice(j * BLK, (j + 1) * BLK)
            sl = slice(hp * PAIR, (hp + 1) * PAIR)
            ksl = slice((hp // rep) * PAIR, (hp // rep + 1) * PAIR)
            qp = q_ref[rows, sl]
            kk = _window(kp_ref, kc_ref, j, ksl)
            vv = _window(vp_ref, vc_ref, j, ksl)
            heads = []
            for e in range(2):
                h = 2 * hp + e
                msk = lo if e == 0 else jnp.logical_not(lo)
                qm = jnp.where(msk, qp, jnp.zeros_like(qp))
                s = lax.dot_general(qm, kk, NT, preferred_element_type=F32) + (b_ref[first, h] if j == 0 else b_ref[0, h])
                heads.append((h, msk, s))
            return rows, sl, vv, heads

        def outputs(rows, sl, vv, heads):
            o_pair = None
            lse_pair = None
            for h, msk, s in heads:
                m = jnp.max(s, axis=-1, keepdims=True)
                if use_sink:
                    sk = sink_ref[h]
                    m = jnp.maximum(m, sk)
                p = jnp.exp(s - m)
                l = jnp.sum(p, axis=-1, keepdims=True)
                if use_sink:
                    l = l + jnp.exp(sk - m)
                vm = jnp.where(msk, vv, jnp.zeros_like(vv))
                oe = jnp.dot(p.astype(BF16), vm, preferred_element_type=F32) * (1.0 / l)
                ls = m + jnp.log(l)
                if o_pair is None:
                    o_pair = oe
                    lse_pair = jnp.broadcast_to(ls, (BLK, PAIR))
                else:
                    o_pair = o_pair + oe
                    lse_pair = jnp.where(lo, lse_pair, ls)
            o_ref[rows, sl] = o_pair.astype(BF16)
            lse_ref[rows, sl] = lse_pair

        items = [(j, hp) for j in range(SUB) for hp in range(4)]
        queue = [scores(*it) for it in items[:AHEAD]]
        for n in range(len(items)):
            if n + AHEAD < len(items):
                queue.append(scores(*items[n + AHEAD]))
            outputs(*queue.pop(0))

    q_spec, kp_spec, kc_spec, b_spec = _attn_specs(kvw)
    return pl.pallas_call(
        body, name=name, grid=(dil, ns),
        in_specs=[pl.BlockSpec(memory_space=pltpu.SMEM), q_spec, kp_spec, kc_spec, kp_spec, kc_spec, b_spec],
        out_specs=[q_spec, q_spec],
        out_shape=[jax.ShapeDtypeStruct((L, dil * WIDTH), BF16), jax.ShapeDtypeStruct((L, dil * WIDTH), F32)],
        compiler_params=_params(("arbitrary", "arbitrary")),
    )(sinks, q, k, k, v, v, bias)


def _attn_bwd(q, k, v, o, do, lse, bias, sinks, *, dil, kv_pairs, use_sink, name):
    L = q.shape[0]
    ns = L // (SUB * BLK)
    n_steps = dil * ns
    kvw = kv_pairs * PAIR
    rep = 4 // kv_pairs
    last = slice((SUB - 1) * BLK, SUB * BLK)

    def body(sink_ref, q_ref, kp_ref, kc_ref, vp_ref, vc_ref, o_ref, do_ref, lse_ref, b_ref,
             dq_ref, dk_ref, dv_ref, dsum_ref, dsk_ref, pk_ref, pv_ref):
        t = pl.program_id(0)
        i = t % ns

        @pl.when(t == 0)
        def _():
            dsum_ref[...] = jnp.zeros_like(dsum_ref)
            dsk_ref[...] = jnp.zeros_like(dsk_ref)
            pk_ref[...] = jnp.zeros_like(pk_ref)
            pv_ref[...] = jnp.zeros_like(pv_ref)

        @pl.when(t < n_steps)
        def _():
            lo = lax.broadcasted_iota(jnp.int32, (1, PAIR), 1) < HEAD_DIM
            first = jnp.where(i == 0, 1, 0)
            dks = [[None] * kv_pairs for _ in range(SUB)]
            dvs = [[None] * kv_pairs for _ in range(SUB)]
            def scores(j, hp):
                rows = slice(j * BLK, (j + 1) * BLK)
                kvp = hp // rep
                sl = slice(hp * PAIR, (hp + 1) * PAIR)
                ksl = slice(kvp * PAIR, (kvp + 1) * PAIR)
                qp = q_ref[rows, sl]
                dop = do_ref[rows, sl]
                prod = dop.astype(F32) * o_ref[rows, sl].astype(F32)
                kk = _window(kp_ref, kc_ref, j, ksl)
                vv = _window(vp_ref, vc_ref, j, ksl)
                heads = []
                for e in range(2):
                    h = 2 * hp + e
                    msk = lo if e == 0 else jnp.logical_not(lo)
                    qm = jnp.where(msk, qp, jnp.zeros_like(qp))
                    dom = jnp.where(msk, dop, jnp.zeros_like(dop))
                    km = jnp.where(msk, kk, jnp.zeros_like(kk))
                    s = (lax.dot_general(qm, kk, NT, preferred_element_type=F32)
                         + (b_ref[first, h] if j == 0 else b_ref[0, h]))
                    dp = lax.dot_general(dom, vv, NT, preferred_element_type=F32)
                    heads.append((h, msk, qm, dom, km, s, dp))
                return j, rows, kvp, sl, prod, heads

            def grads(j, rows, kvp, sl, prod, heads):
                dq_pair = None
                c_pair = None
                qms, doms, dsbs, pbs = [], [], [], []
                for h, msk, qm, dom, km, s, dp in heads:
                    ls = lse_ref[rows, h * HEAD_DIM:h * HEAD_DIM + 1]
                    p = jnp.exp(s - ls)
                    delta = jnp.sum(jnp.where(msk, prod, 0.0), axis=-1, keepdims=True)
                    ds = p * (dp - delta)
                    if use_sink:
                        ce = jnp.exp(sink_ref[h] - ls) * delta
                        c_pair = jnp.broadcast_to(ce, (BLK, PAIR)) if c_pair is None else jnp.where(msk, ce, c_pair)
                    else:
                        dsum_ref[h] += ds
                    dsb = ds.astype(BF16)
                    dqe = jnp.dot(dsb, km, preferred_element_type=F32)
                    dq_pair = dqe if dq_pair is None else dq_pair + dqe
                    qms.append(qm)
                    doms.append(dom)
                    dsbs.append(dsb)
                    pbs.append(p.astype(BF16))
                dke = lax.dot_general(jnp.concatenate(dsbs, axis=0), jnp.concatenate(qms, axis=0), TN,
                                      preferred_element_type=F32)
                dve = lax.dot_general(jnp.concatenate(pbs, axis=0), jnp.concatenate(doms, axis=0), TN,
                                      preferred_element_type=F32)
                dks[j][kvp] = dke if dks[j][kvp] is None else dks[j][kvp] + dke
                dvs[j][kvp] = dve if dvs[j][kvp] is None else dvs[j][kvp] + dve
                dq_ref[rows, sl] = (dq_pair * Q_SCALE).astype(BF16)
                if use_sink:
                    dsk_ref[:, sl] += c_pair

            items = [(j, hp) for j in range(SUB) for hp in range(4)]
            ahead = AHEAD + 1 if use_sink else AHEAD
            queue = [scores(*it) for it in items[:ahead]]
            for n in range(len(items)):
                if n + ahead < len(items):
                    queue.append(scores(*items[n + ahead]))
                grads(*queue.pop(0))
            for kvp in range(kv_pairs):
                ksl = slice(kvp * PAIR, (kvp + 1) * PAIR)
                for pend_ref, out_ref, parts in ((pk_ref, dk_ref, [d[kvp] for d in dks]),
                                                 (pv_ref, dv_ref, [d[kvp] for d in dvs])):
                    if SUB > 1:
                        out_ref[:(SUB - 1) * BLK, ksl] = pend_ref[:(SUB - 1) * BLK, ksl].astype(BF16)
                    out_ref[last, ksl] = (pend_ref[last, ksl] + parts[0][:BLK]).astype(BF16)
                    for j in range(SUB):
                        own = parts[j][BLK:]
                        pend_ref[j * BLK:(j + 1) * BLK, ksl] = own + parts[j + 1][:BLK] if j + 1 < SUB else own

        @pl.when(t == n_steps)
        def _():
            dk_ref[...] = pk_ref[...].astype(BF16)
            dv_ref[...] = pv_ref[...].astype(BF16)

    def at(t):
        t = jnp.minimum(t, n_steps - 1)
        return t % ns, t // ns

    def before(t):
        return at(jnp.maximum(t - 1, 0))

    q_spec = pl.BlockSpec((SUB * BLK, WIDTH), at)
    kc_spec = pl.BlockSpec((SUB * BLK, kvw), at)
    kp_spec = pl.BlockSpec((BLK, kvw), lambda t: (jnp.maximum(SUB * at(t)[0] - 1, 0), at(t)[1]))
    b_spec = pl.BlockSpec((2, N_HEADS, BLK, 2 * BLK), lambda t: (0, 0, 0, 0))
    dkv_spec = pl.BlockSpec((SUB * BLK, kvw), before)
    return pl.pallas_call(
        body, name=name, grid=(n_steps + 1,),
        in_specs=[pl.BlockSpec(memory_space=pltpu.SMEM), q_spec, kp_spec, kc_spec, kp_spec, kc_spec,
                  q_spec, q_spec, q_spec, b_spec],
        out_specs=[q_spec, dkv_spec, dkv_spec,
                   pl.BlockSpec((N_HEADS, BLK, 2 * BLK), lambda t: (0, 0, 0)),
                   pl.BlockSpec((BLK, WIDTH), lambda t: (0, 0))],
        out_shape=[jax.ShapeDtypeStruct((L, dil * WIDTH), BF16),
                   jax.ShapeDtypeStruct((L, dil * kvw), BF16),
                   jax.ShapeDtypeStruct((L, dil * kvw), BF16),
                   jax.ShapeDtypeStruct((N_HEADS, BLK, 2 * BLK), F32),
                   jax.ShapeDtypeStruct((BLK, WIDTH), F32)],
        scratch_shapes=[pltpu.VMEM((SUB * BLK, kvw), F32), pltpu.VMEM((SUB * BLK, kvw), F32)],
        compiler_params=_params(("arbitrary",)),
    )(sinks, q, k, k, v, v, o, do, lse, bias)


def _merge_wo(x, oa, o1, o2, o3, l1, l2, l3, ga, gb, wo, gf, *, tm=512):
    T = x.shape[0]

    def body(x_ref, oa_ref, o1_ref, o2_ref, o3_ref, l1_ref, l2_ref, l3_ref, ga_ref, gb_ref, wo_ref, gf_ref,
             x2_ref, mix_ref, h2_ref, ob1_ref, ob4_ref, ob16_ref, ls1_ref, ls4_ref, ls16_ref, so2, so3, sl2, sl3):
        _unstride(o2_ref, so2, BRANCHES[1][1], tm)
        _unstride(o3_ref, so3, BRANCHES[2][1], tm)
        _unstride(l2_ref, sl2, BRANCHES[1][1], tm)
        _unstride(l3_ref, sl3, BRANCHES[2][1], tm)
        la, lb, lc = l1_ref[...], _scr_get(sl2), _scr_get(sl3)
        m = jnp.maximum(jnp.maximum(la, lb), lc)
        ea, eb, ec = jnp.exp(la - m), jnp.exp(lb - m), jnp.exp(lc - m)
        den = ea + eb + ec
        inv = 1.0 / den
        ob = (ea * o1_ref[...].astype(F32) + eb * _scr_get(so2) + ec * _scr_get(so3)) * inv
        _scr_put(so2, ob)
        _scr_put(sl2, m + jnp.log(den))
        for (_, dil), o_ref, l_ref in zip(BRANCHES, (ob1_ref, ob4_ref, ob16_ref), (ls1_ref, ls4_ref, ls16_ref)):
            _restride(so2, o_ref, dil, tm)
            _restride(sl2, l_ref, dil, tm)
        oav = oa_ref[...].astype(F32)
        ra = lax.rsqrt(jnp.mean(oav * oav, axis=-1, keepdims=True) + EPS)
        rb = lax.rsqrt(jnp.mean(ob * ob, axis=-1, keepdims=True) + EPS)
        mix_ref[:, :WIDTH] = (oav * ra * ga_ref[...]).astype(BF16)
        mix_ref[:, WIDTH:] = (ob * rb * gb_ref[...]).astype(BF16)
        x2 = x_ref[...] + jnp.dot(mix_ref[...], wo_ref[...], preferred_element_type=F32)
        x2_ref[...] = x2
        r2 = lax.rsqrt(jnp.mean(x2 * x2, axis=-1, keepdims=True) + EPS)
        h2_ref[...] = (x2 * r2 * gf_ref[...]).astype(BF16)

    row = lambda w_: pl.BlockSpec((tm, w_), lambda i: (i, 0))
    full = lambda a: pl.BlockSpec(a.shape, lambda i: (0, 0))
    return pl.pallas_call(
        body, name="merge_wo", grid=(T // tm,),
        in_specs=[row(D_MODEL), row(WIDTH)] + _view_specs(tm) * 2 + [full(ga), full(gb), full(wo), full(gf)],
        out_specs=[row(D_MODEL), row(D_MODEL), row(D_MODEL)] + _view_specs(tm) * 2,
        out_shape=[jax.ShapeDtypeStruct((T, D_MODEL), F32), jax.ShapeDtypeStruct((T, D_MODEL), BF16),
                   jax.ShapeDtypeStruct((T, D_MODEL), BF16)] + _view_shapes(T, BF16) + _view_shapes(T, F32),
        scratch_shapes=[_scr(tm)] * 4,
        compiler_params=_params(("arbitrary",)),
    )(x, oa, o1, o2, o3, l1, l2, l3, ga, gb, wo, gf)


def _ffn_up(h2, wgt, wut, *, tm=512, fc=D_FF, rc=512, cc=256):
    T = h2.shape[0]

    def body(h_ref, wg_ref, wu_ref, gate_ref, up_ref, act_ref):
        for s in range(0, tm, rc):
            h = h_ref[s:s + rc, :]
            for c in range(0, fc, cc):
                gt = lax.dot_general(h, wg_ref[c:c + cc, :], NT, preferred_element_type=F32)
                u = lax.dot_general(h, wu_ref[c:c + cc, :], NT, preferred_element_type=F32)
                gate_ref[s:s + rc, c:c + cc] = gt.astype(BF16)
                up_ref[s:s + rc, c:c + cc] = u.astype(BF16)
                act_ref[s:s + rc, c:c + cc] = (gt * _sigmoid(gt) * u).astype(BF16)

    rowd = pl.BlockSpec((tm, D_MODEL), lambda i, c: (i, 0))
    wrow = pl.BlockSpec((fc, D_MODEL), lambda i, c: (c, 0))
    oc = pl.BlockSpec((tm, fc), lambda i, c: (i, c))
    return pl.pallas_call(
        body, name="ffn_up", grid=(T // tm, D_FF // fc),
        in_specs=[rowd, wrow, wrow],
        out_specs=[oc, oc, oc],
        out_shape=[jax.ShapeDtypeStruct((T, D_FF), BF16)] * 3,
        compiler_params=_params(("arbitrary", "arbitrary")),
    )(h2, wgt, wut)


def _ffn_down_loss(act, wd, x2, tgt, g, *, tm=512, rc=256):
    T = x2.shape[0]

    def body(act_ref, wd_ref, x2_ref, tgt_ref, g_ref, dx_ref, dxb_ref, loss_ref, dg_ref):
        @pl.when(pl.program_id(0) == 0)
        def _():
            loss_ref[...] = jnp.zeros_like(loss_ref)
            dg_ref[...] = jnp.zeros_like(dg_ref)

        gv = g_ref[...]
        lsum = jnp.zeros((1, 1), F32)
        dgs = jnp.zeros((1, D_MODEL), F32)
        for c in range(0, tm, rc):
            x3 = x2_ref[c:c + rc, :] + jnp.dot(act_ref[c:c + rc, :], wd_ref[...], preferred_element_type=F32)
            r = lax.rsqrt(jnp.mean(x3 * x3, axis=-1, keepdims=True) + EPS)
            xh = x3 * r
            diff = xh * gv - tgt_ref[c:c + rc, :]
            lsum = lsum + jnp.sum(jnp.sum(diff * diff, axis=-1, keepdims=True), axis=0, keepdims=True)
            dy = diff * (1.0 / D_MODEL)
            dgs = dgs + jnp.sum(dy * xh, axis=0, keepdims=True)
            dx = _rms_bwd(dy, xh, r, gv)
            dx_ref[c:c + rc, :] = dx
            dxb_ref[c:c + rc, :] = dx.astype(BF16)
        loss_ref[...] += lsum * (0.5 / D_MODEL)
        dg_ref[...] += dgs

    rowd = pl.BlockSpec((tm, D_MODEL), lambda i: (i, 0))
    return pl.pallas_call(
        body, name="ffn_down_loss", grid=(T // tm,),
        in_specs=[pl.BlockSpec((tm, D_FF), lambda i: (i, 0)), pl.BlockSpec((D_FF, D_MODEL), lambda i: (0, 0)),
                  rowd, rowd, pl.BlockSpec(g.shape, lambda i: (0, 0))],
        out_specs=[rowd, rowd, pl.BlockSpec((1, 1), lambda i: (0, 0)), pl.BlockSpec((1, D_MODEL), lambda i: (0, 0))],
        out_shape=[jax.ShapeDtypeStruct((T, D_MODEL), F32), jax.ShapeDtypeStruct((T, D_MODEL), BF16),
                   jax.ShapeDtypeStruct((1, 1), F32), jax.ShapeDtypeStruct((1, D_MODEL), F32)],
        compiler_params=_params(("arbitrary",)),
    )(act, wd, x2, tgt, g)


def _ffn_bwd(dx3, gate, up, wd, wgt, wut, x2, g, *, tm=256, cc=256):
    T = x2.shape[0]

    def body(dx_ref, gate_ref, up_ref, wd_ref, wg_ref, wu_ref, x2_ref, g_ref,
             dgate_ref, dup_ref, dx2_ref, dx2b_ref, dg_ref):
        @pl.when(pl.program_id(0) == 0)
        def _():
            dg_ref[...] = jnp.zeros_like(dg_ref)

        dxb = dx_ref[...].astype(BF16)
        for c in range(0, D_FF, cc):
            dact = lax.dot_general(dxb, wd_ref[c:c + cc, :], NT, preferred_element_type=F32)
            gt = gate_ref[:, c:c + cc].astype(F32)
            u = up_ref[:, c:c + cc].astype(F32)
            sg = _sigmoid(gt)
            a = dact * sg
            dgate_ref[:, c:c + cc] = (a * u * ((1.0 + gt) - gt * sg)).astype(BF16)
            dup_ref[:, c:c + cc] = (a * gt).astype(BF16)
        dh = (jnp.dot(dgate_ref[...], wg_ref[...], preferred_element_type=F32)
              + jnp.dot(dup_ref[...], wu_ref[...], preferred_element_type=F32))
        xv = x2_ref[...]
        r = lax.rsqrt(jnp.mean(xv * xv, axis=-1, keepdims=True) + EPS)
        xh = xv * r
        dg_ref[...] += jnp.sum(dh * xh, axis=0, keepdims=True)
        d = dx_ref[...] + _rms_bwd(dh, xh, r, g_ref[...])
        dx2_ref[...] = d
        dx2b_ref[...] = d.astype(BF16)

    rowd = pl.BlockSpec((tm, D_MODEL), lambda i: (i, 0))
    rowf = pl.BlockSpec((tm, D_FF), lambda i: (i, 0))
    wfull = pl.BlockSpec((D_FF, D_MODEL), lambda i: (0, 0), pipeline_mode=pl.Buffered(1))
    return pl.pallas_call(
        body, name="ffn_bwd", grid=(T // tm,),
        in_specs=[rowd, rowf, rowf, wfull, wfull, wfull, rowd, pl.BlockSpec(g.shape, lambda i: (0, 0))],
        out_specs=[rowf, rowf, rowd, rowd, pl.BlockSpec((1, D_MODEL), lambda i: (0, 0))],
        out_shape=[jax.ShapeDtypeStruct((T, D_FF), BF16), jax.ShapeDtypeStruct((T, D_FF), BF16),
                   jax.ShapeDtypeStruct((T, D_MODEL), F32), jax.ShapeDtypeStruct((T, D_MODEL), BF16),
                   jax.ShapeDtypeStruct((1, D_MODEL), F32)],
        compiler_params=_params(("arbitrary",)),
    )(dx3, gate, up, wd, wgt, wut, x2, g)


def _matmul_tn(a, b, *, tk, tn, tt=2048, out_dtype=BF16, name):
    T, K = a.shape
    N = b.shape[1]
    nt = T // tt

    def body(a_ref, b_ref, o_ref, acc_ref):
        part = lax.dot_general(a_ref[...], b_ref[...], TN, preferred_element_type=F32)

        @pl.when(pl.program_id(2) == 0)
        def _():
            acc_ref[...] = part

        @pl.when(pl.program_id(2) > 0)
        def _():
            acc_ref[...] += part

        @pl.when(pl.program_id(2) == nt - 1)
        def _():
            o_ref[...] = acc_ref[...].astype(out_dtype)

    return pl.pallas_call(
        body, name=name, grid=(K // tk, N // tn, nt),
        in_specs=[pl.BlockSpec((tt, tk), lambda i, j, t: (t, i)), pl.BlockSpec((tt, tn), lambda i, j, t: (t, j))],
        out_specs=pl.BlockSpec((tk, tn), lambda i, j, t: (i, j)),
        out_shape=jax.ShapeDtypeStruct((K, N), out_dtype),
        scratch_shapes=[pltpu.VMEM((tk, tn), F32)],
        compiler_params=_params(("arbitrary", "arbitrary", "arbitrary")),
    )(a, b)


def _wo_bwd(dx2b, wo, oa, ob, ga, gb, mixed, *, tm=512, wc=256):
    T = dx2b.shape[0]
    n_tiles = T // tm

    def body(dx_ref, wo_ref, oa_ref, ob_ref, ga_ref, gb_ref, mix_ref,
             doa_ref, dob1_ref, dob4_ref, dob16_ref, dga_ref, dgb_ref, dwo_ref, scr, dw_acc):
        @pl.when(pl.program_id(0) == 0)
        def _():
            dga_ref[...] = jnp.zeros_like(dga_ref)
            dgb_ref[...] = jnp.zeros_like(dgb_ref)
            dw_acc[...] = jnp.zeros_like(dw_acc)

        dxv = dx_ref[...]
        for c in range(0, D_MODEL, wc):
            dw_acc[c:c + wc, :] += lax.dot_general(mix_ref[:, c:c + wc], dxv, TN, preferred_element_type=F32)

        @pl.when(pl.program_id(0) == n_tiles - 1)
        def _():
            dwo_ref[...] = dw_acc[...].astype(BF16)

        dm = lax.dot_general(dxv, wo_ref[...], NT, preferred_element_type=F32)
        for o_ref, g_ref, dg_ref, sl in ((oa_ref, ga_ref, dga_ref, slice(0, WIDTH)),
                                         (ob_ref, gb_ref, dgb_ref, slice(WIDTH, 2 * WIDTH))):
            ov = o_ref[...].astype(F32)
            r = lax.rsqrt(jnp.mean(ov * ov, axis=-1, keepdims=True) + EPS)
            xh = ov * r
            d = dm[:, sl]
            dg_ref[...] += jnp.sum(d * xh, axis=0, keepdims=True)
            do = _rms_bwd(d, xh, r, g_ref[...])
            if o_ref is oa_ref:
                doa_ref[...] = do.astype(BF16)
            else:
                _scr_put(scr, do)
                for (_, dil), v_ref in zip(BRANCHES, (dob1_ref, dob4_ref, dob16_ref)):
                    _restride(scr, v_ref, dil, tm)

    row = lambda w_: pl.BlockSpec((tm, w_), lambda i: (i, 0))
    full = lambda a: pl.BlockSpec(a.shape, lambda i: (0, 0))
    return pl.pallas_call(
        body, name="wo_bwd", grid=(T // tm,),
        in_specs=[row(D_MODEL), full(wo), row(WIDTH), row(WIDTH), full(ga), full(gb), row(D_MODEL)],
        out_specs=[row(WIDTH)] + _view_specs(tm)
        + [pl.BlockSpec((1, WIDTH), lambda i: (0, 0)), pl.BlockSpec((1, WIDTH), lambda i: (0, 0)), full(wo)],
        out_shape=[jax.ShapeDtypeStruct((T, WIDTH), BF16)] + _view_shapes(T, BF16)
        + [jax.ShapeDtypeStruct((1, WIDTH), F32), jax.ShapeDtypeStruct((1, WIDTH), F32),
           jax.ShapeDtypeStruct((D_MODEL, D_MODEL), BF16)],
        scratch_shapes=[_scr(tm), pltpu.VMEM((D_MODEL, D_MODEL), F32)],
        compiler_params=_params(("arbitrary",)),
    )(dx2b, wo, oa, ob, ga, gb, mixed)


def _inproj_bwd(dqa, dka, dva, dqs, dks, dvs, cos, sin, w, x, dx2, g, h1, *, tm=256, wc=256):
    T = dqa.shape[0]
    n_tiles = T // tm

    def body(dqa_ref, dka_ref, dva_ref, q1, q2, q3, k1, k2, k3, v1, v2, v3, cos_ref, sin_ref, w_ref, x_ref, dx2_ref,
             g_ref, h1_ref, db_ref, gx_ref, dg_ref, dw_ref, dp_ref, dw_acc, acc, tmp):
        @pl.when(pl.program_id(0) == 0)
        def _():
            db_ref[...] = jnp.zeros_like(db_ref)
            dg_ref[...] = jnp.zeros_like(dg_ref)
            dw_acc[...] = jnp.zeros_like(dw_acc)

        cosv = cos_ref[...]
        sinv = sin_ref[...]
        lane = lax.broadcasted_iota(jnp.int32, (tm, PAIR), 1)
        first = (lane % HEAD_DIM) < (HEAD_DIM // 2)

        def put(off, val):
            dp_ref[:, off:off + PAIR] = val.astype(BF16)
            db_ref[:, off:off + PAIR] += jnp.sum(val, axis=0, keepdims=True)

        for src, off, width in ((dqa_ref, 0, 512), (dka_ref, 512, 256)):
            for j in range(0, width, PAIR):
                d = src[:, j:j + PAIR].astype(F32)
                put(off + j, d * cosv - _rope_rot(d, first) * sinv)
        for j in range(0, 256, PAIR):
            put(768 + j, dva_ref[:, j:j + PAIR].astype(F32))
        for (a, b, c), off in (((q1, q2, q3), 1024), ((k1, k2, k3), 1536), ((v1, v2, v3), 2048)):
            _unstride(b, acc, BRANCHES[1][1], tm)
            _unstride(c, tmp, BRANCHES[2][1], tm)
            for j in range(N_CHUNK):
                put(off + j * PAIR, a[:, j * PAIR:(j + 1) * PAIR].astype(F32) + acc[j] + tmp[j])

        dh = jnp.dot(dp_ref[...], w_ref[...], preferred_element_type=F32)
        xv = x_ref[...]
        r = lax.rsqrt(jnp.mean(xv * xv, axis=-1, keepdims=True) + EPS)
        xh = xv * r
        dg_ref[...] += jnp.sum(dh * xh, axis=0, keepdims=True)
        gx_ref[...] = dx2_ref[...] + _rms_bwd(dh, xh, r, g_ref[...])

        h1v = h1_ref[...]
        for c in range(0, D_INP, wc):
            dw_acc[c:c + wc, :] += lax.dot_general(dp_ref[:, c:c + wc], h1v, TN, preferred_element_type=F32)

        @pl.when(pl.program_id(0) == n_tiles - 1)
        def _():
            pltpu.sync_copy(dw_acc, dw_ref)

    row = lambda w_: pl.BlockSpec((tm, w_), lambda i: (i, 0))
    full = lambda a: pl.BlockSpec(a.shape, lambda i: (0, 0))
    return pl.pallas_call(
        body, name="inproj_bwd", grid=(n_tiles,),
        in_specs=[row(512), row(256), row(256)] + _view_specs(tm) * 3 + [row(PAIR), row(PAIR)]
        + [pl.BlockSpec(w.shape, lambda i: (0, 0), pipeline_mode=pl.Buffered(1)), row(D_MODEL), row(D_MODEL), full(g),
           row(D_MODEL)],
        out_specs=[pl.BlockSpec((1, D_INP), lambda i: (0, 0)), row(D_MODEL),
                   pl.BlockSpec((1, D_MODEL), lambda i: (0, 0)), pl.BlockSpec(memory_space=pl.ANY)],
        out_shape=[jax.ShapeDtypeStruct((1, D_INP), F32), jax.ShapeDtypeStruct((T, D_MODEL), F32),
                   jax.ShapeDtypeStruct((1, D_MODEL), F32), jax.ShapeDtypeStruct((D_INP, D_MODEL), F32)],
        scratch_shapes=[pltpu.VMEM((tm, D_INP), BF16), pltpu.VMEM((D_INP, D_MODEL), F32), _scr(tm), _scr(tm)],
        compiler_params=_params(("arbitrary",)),
    )(dqa, dka, dva, *dqs, *dks, *dvs, cos, sin, w, x, dx2, g, h1)


def _bias_sink_grads(dsums, bmaps, dsk):
    def body(s1, s2, s3, m1, m2, m3, dsk_ref, drel_ref, dsink_ref):
        row = lax.broadcasted_iota(jnp.int32, (N_HEADS, 128), 0)
        lane = lax.broadcasted_iota(jnp.int32, (N_HEADS, 128), 1)
        out = jnp.zeros((N_HEADS, 128), F32)
        for s_ref, m_ref in ((s1, m1), (s2, m2), (s3, m3)):
            bm = m_ref[...]
            for h in range(N_HEADS):
                a = s_ref[h]
                for b in range(REL_BUCKETS):
                    v = jnp.sum(jnp.sum(jnp.where(bm == b, a, 0.0), axis=-1, keepdims=True), axis=0, keepdims=True)
                    out = out + jnp.where((row == h) & (lane == b), v, 0.0)
        drel_ref[...] = out
        dsink_ref[...] = -jnp.sum(dsk_ref[...], axis=0, keepdims=True)

    vm = pl.BlockSpec(memory_space=pltpu.VMEM)
    return pl.pallas_call(
        body, name="bias_sink_grads",
        in_specs=[vm] * 7, out_specs=[vm, vm],
        out_shape=[jax.ShapeDtypeStruct((N_HEADS, 128), F32), jax.ShapeDtypeStruct((1, WIDTH), F32)],
        compiler_params=_params(),
    )(*dsums, *bmaps, dsk)


def _all_gather(blk, *, name):
    R, C = blk.shape

    def body(x_ref, out_ref, send_sems, recv_sems, local_sem):
        x, y, c = lax.axis_index("x"), lax.axis_index("y"), lax.axis_index("c")
        me, sibling = (x, y, c), (x, y, 1 - c)
        chips = [(1 - x, y), (x, 1 - y), (1 - x, 1 - y)]

        def slot(px, py, pc):
            return out_ref.at[4 * px + 2 * py + pc]

        def copy(k, block, to, src=None):
            return pltpu.make_async_remote_copy(
                src_ref=slot(*block) if src is None else src, dst_ref=slot(*block),
                send_sem=send_sems.at[k], recv_sem=recv_sems.at[k], device_id=to, device_id_type=MESH)

        mine = pltpu.make_async_copy(x_ref, slot(*me), local_sem)
        mine.start()
        first = [copy(0, me, sibling, src=x_ref)]
        first += [copy(1 + j, me, (*chip, c), src=x_ref) for j, chip in enumerate(chips)]
        for cp in first:
            cp.start()
        passed = [copy(4 + j, (*chip, c), sibling) for j, chip in enumerate(chips)]
        for j, chip in enumerate(chips):
            copy(1 + j, (*chip, c), me).wait_recv()
            passed[j].start()
        copy(0, sibling, me).wait_recv()
        for j, chip in enumerate(chips):
            copy(4 + j, (*chip, 1 - c), me).wait_recv()
        for cp in first + passed:
            cp.wait_send()
        mine.wait()

    return pl.pallas_call(
        body, name=name,
        in_specs=[pl.BlockSpec(memory_space=pl.ANY)], out_specs=pl.BlockSpec(memory_space=pl.ANY),
        out_shape=jax.ShapeDtypeStruct((N_DEV, R, C), blk.dtype),
        scratch_shapes=[pltpu.SemaphoreType.DMA((7,)), pltpu.SemaphoreType.DMA((7,)), pltpu.SemaphoreType.DMA],
        compiler_params=pltpu.CompilerParams(has_side_effects=True),
    )(blk)


def _peers(x, y, c):
    return [(x ^ (k >> 2), y ^ ((k >> 1) & 1), c ^ (k & 1)) for k in range(1, N_DEV)]


_HBM = pl.BlockSpec(memory_space=pltpu.HBM)
_SEM = pl.BlockSpec(memory_space=pltpu.SEMAPHORE)
_EFFECT = pltpu.SideEffectType.DATAFLOW_SIDE_EFFECTING


def _peer_list(x, y, c, near):
    if near:
        return [(x, y, 1 - c), (1 - x, y, c), (x, 1 - y, c), (1 - x, 1 - y, c)]
    return _peers(x, y, c)


def _exchange_start(srcs, *, gather, name, near=False):
    n = len(srcs)
    n_peers = 4 if near else N_DEV - 1
    lands = [lax.empty((N_DEV,) + s.shape[-2:], s.dtype) for s in srcs]

    def body(*refs):
        src_refs, land_refs = refs[:n], refs[n:2 * n]
        send_sems, recv_sems = refs[2 * n], refs[2 * n + 1]
        token = refs[-1]
        x, y, c = lax.axis_index("x"), lax.axis_index("y"), lax.axis_index("c")
        mine = 4 * x + 2 * y + c
        for a in range(n):
            for k, peer in enumerate(_peer_list(x, y, c, near)):
                dest = 4 * peer[0] + 2 * peer[1] + peer[2]
                j = a * n_peers + k
                pltpu.make_async_remote_copy(
                    src_ref=src_refs[a] if gather else src_refs[a].at[dest], dst_ref=land_refs[a].at[mine],
                    send_sem=send_sems.at[j], recv_sem=recv_sems.at[j], device_id=peer, device_id_type=MESH).start()
        token[...] = jnp.zeros_like(token)

    sems = pltpu.SemaphoreType.DMA((n * n_peers,))
    out = pl.pallas_call(
        body, name=name,
        out_shape=(sems, sems) + tuple(pltpu.HBM(a.shape, a.dtype) for a in list(srcs) + lands)
        + (jax.ShapeDtypeStruct((8, 128), F32),),
        in_specs=(_HBM,) * (2 * n), out_specs=(_SEM, _SEM) + (_HBM,) * (2 * n) + (pl.BlockSpec(memory_space=pltpu.VMEM),),
        input_output_aliases={i: 2 + i for i in range(2 * n)},
        compiler_params=pltpu.CompilerParams(has_side_effects=_EFFECT),
    )(*[pltpu.with_memory_space_constraint(a, pltpu.HBM) for a in list(srcs) + lands])
    return out[:-1], out[-1]


def _exchange_wait(state, after, *, gather, name, near=False):
    send_sems, recv_sems = state[0], state[1]
    n = (len(state) - 2) // 2
    n_peers = 4 if near else N_DEV - 1
    arrays = state[2:]

    def body(*refs):
        src_refs, land_refs = refs[:n], refs[n:2 * n]
        send_sems, recv_sems = refs[2 * n], refs[2 * n + 1]
        x, y, c = lax.axis_index("x"), lax.axis_index("y"), lax.axis_index("c")
        for a in range(n):
            for k, peer in enumerate(_peer_list(x, y, c, near)):
                other = 4 * peer[0] + 2 * peer[1] + peer[2]
                j = a * n_peers + k
                copy = pltpu.make_async_remote_copy(
                    src_ref=src_refs[a] if gather else src_refs[a].at[other], dst_ref=land_refs[a].at[other],
                    send_sem=send_sems.at[j], recv_sem=recv_sems.at[j], device_id=peer, device_id_type=MESH)
                copy.wait_send()
                copy.wait_recv()

    out = pl.pallas_call(
        body, name=name,
        out_shape=tuple(pltpu.HBM(a.shape, a.dtype) for a in arrays),
        in_specs=(_HBM,) * (2 * n) + (_SEM, _SEM, pl.BlockSpec(memory_space=pl.ANY)), out_specs=(_HBM,) * (2 * n),
        input_output_aliases={i: i for i in range(2 * n)},
        compiler_params=pltpu.CompilerParams(has_side_effects=_EFFECT),
    )(*arrays, send_sems, recv_sems, after)
    mine = 4 * lax.axis_index("x") + 2 * lax.axis_index("y") + lax.axis_index("c")
    own = out[:n] if gather else [lax.dynamic_index_in_dim(s, mine, 0, keepdims=False) for s in out[:n]]
    return [lax.dynamic_update_slice(g, o[None], (mine, 0, 0)) for g, o in zip(out[n:], own)]


def _forward_start(lands, *, name):
    n = len(lands)

    def body(*refs):
        land_refs, send_sems, recv_sems, token = refs[:n], refs[n], refs[n + 1], refs[-1]
        x, y, c = lax.axis_index("x"), lax.axis_index("y"), lax.axis_index("c")
        for a in range(n):
            for j, (px, py) in enumerate(((1 - x, y), (x, 1 - y), (1 - x, 1 - y))):
                blk = 4 * px + 2 * py + c
                pltpu.make_async_remote_copy(
                    src_ref=land_refs[a].at[blk], dst_ref=land_refs[a].at[blk], send_sem=send_sems.at[3 * a + j],
                    recv_sem=recv_sems.at[3 * a + j], device_id=(x, y, 1 - c), device_id_type=MESH).start()
        token[...] = jnp.zeros_like(token)

    sems = pltpu.SemaphoreType.DMA((3 * n,))
    out = pl.pallas_call(
        body, name=name,
        out_shape=(sems, sems) + tuple(pltpu.HBM(a.shape, a.dtype) for a in lands) + (jax.ShapeDtypeStruct((8, 128), F32),),
        in_specs=(_HBM,) * n, out_specs=(_SEM, _SEM) + (_HBM,) * n + (pl.BlockSpec(memory_space=pltpu.VMEM),),
        input_output_aliases={i: 2 + i for i in range(n)},
        compiler_params=pltpu.CompilerParams(has_side_effects=_EFFECT),
    )(*[pltpu.with_memory_space_constraint(a, pltpu.HBM) for a in lands])
    return out[:-1], out[-1]


def _forward_wait(state, after, *, name):
    send_sems, recv_sems = state[0], state[1]
    lands = state[2:]
    n = len(lands)

    def body(*refs):
        land_refs, send_sems, recv_sems = refs[:n], refs[n], refs[n + 1]
        x, y, c = lax.axis_index("x"), lax.axis_index("y"), lax.axis_index("c")
        for a in range(n):
            for j, (px, py) in enumerate(((1 - x, y), (x, 1 - y), (1 - x, 1 - y))):
                copy = pltpu.make_async_remote_copy(
                    src_ref=land_refs[a].at[4 * px + 2 * py + c], dst_ref=land_refs[a].at[4 * px + 2 * py + 1 - c],
                    send_sem=send_sems.at[3 * a + j], recv_sem=recv_sems.at[3 * a + j], device_id=(x, y, 1 - c),
                    device_id_type=MESH)
                copy.wait_send()
                copy.wait_recv()

    return pl.pallas_call(
        body, name=name,
        out_shape=tuple(pltpu.HBM(a.shape, a.dtype) for a in lands),
        in_specs=(_HBM,) * n + (_SEM, _SEM, pl.BlockSpec(memory_space=pl.ANY)), out_specs=(_HBM,) * n,
        input_output_aliases={i: i for i in range(n)},
        compiler_params=pltpu.CompilerParams(has_side_effects=_EFFECT),
    )(*lands, send_sems, recv_sems, after)


def _adam_math(w, g, m, v):
    m = ADAM_B1 * m + (1.0 - ADAM_B1) * g
    v = ADAM_B2 * v + (1.0 - ADAM_B2) * (g * g)
    m_hat = m / (1.0 - ADAM_B1 ** ADAM_STEP)
    v_hat = v / (1.0 - ADAM_B2 ** ADAM_STEP)
    delta = -ADAM_LR * (m_hat / (jnp.sqrt(v_hat) + ADAM_EPS) + ADAM_WD * w)
    return delta, m, v


def _adamw(parts, w, m, v, *, name):
    R, C = w.shape
    n_parts = parts.shape[0]
    tr = R // 2
    assert tr % 16 == 0

    def body(p_ref, w_ref, m_ref, v_ref, g_ref, d_ref, nm_ref, nv_ref):
        g = p_ref[0].astype(F32)
        for s in range(1, n_parts):
            g = g + p_ref[s].astype(F32)
        d, nm, nv = _adam_math(w_ref[...], g, m_ref[...], v_ref[...])
        g_ref[...] = g
        d_ref[...] = d
        nm_ref[...] = nm
        nv_ref[...] = nv

    blk = pl.BlockSpec((tr, C), lambda i: (i, 0))
    return pl.pallas_call(
        body, name=name, grid=(R // tr,),
        in_specs=[pl.BlockSpec((n_parts, tr, C), lambda i: (0, i, 0)), blk, blk, blk],
        out_specs=[blk] * 4, out_shape=[jax.ShapeDtypeStruct((R, C), F32)] * 4,
        compiler_params=_params(("arbitrary",)),
    )(parts, w, m, v)


def _adamw_small(parts, w, m, v):
    def body(p_ref, w_ref, m_ref, v_ref, g_ref, d_ref, nm_ref, nv_ref):
        g = p_ref[0]
        for s in range(1, N_DEV):
            g = g + p_ref[s]
        d, nm, nv = _adam_math(w_ref[...], g, m_ref[...], v_ref[...])
        g_ref[...] = g
        d_ref[...] = d
        nm_ref[...] = nm
        nv_ref[...] = nv

    vm = pl.BlockSpec(memory_space=pltpu.VMEM)
    return pl.pallas_call(
        body, name="adamw_small", in_specs=[vm] * 4, out_specs=[vm] * 4,
        out_shape=[jax.ShapeDtypeStruct((SMALL_ROWS, 128), F32)] * 4, compiler_params=_params(),
    )(parts, w, m, v)


def _t5_bucket(dist):
    max_exact = REL_BUCKETS // 2
    df = jnp.maximum(dist, 1).astype(F32)
    large = max_exact + (jnp.log(df / max_exact) / math.log(REL_MAX_DISTANCE / max_exact)
                         * (REL_BUCKETS - max_exact)).astype(jnp.int32)
    large = jnp.minimum(large, REL_BUCKETS - 1)
    return jnp.where(dist < max_exact, dist, large)


def _band_tables(rel_table, dil, n_back):
    qi = jnp.arange(BLK)[:, None]
    kj = jnp.arange(2 * BLK)[None, :]
    delta = BLK + qi - kj
    in_band = (delta >= 0) & (delta <= n_back)
    if rel_table is None:
        vals = jnp.zeros((N_HEADS, BLK, 2 * BLK), F32)
        bmap = None
    else:
        bucket = _t5_bucket(jnp.clip(delta, 0, n_back) * dil)
        vals = jnp.zeros((N_HEADS, BLK, 2 * BLK), F32)
        for b in range(REL_BUCKETS):
            vals = jnp.where((bucket == b)[None], rel_table[b][:, None, None], vals)
        bmap = jnp.where(in_band, bucket, -1).astype(jnp.int32)
    later = jnp.where(in_band[None], vals, NEG)
    first = jnp.where((in_band & (kj >= BLK))[None], vals, NEG)
    return jnp.stack([later, first]), bmap


def _rope_tables(T):
    half = HEAD_DIM // 2
    inv_freq = ROPE_THETA ** (-jnp.arange(half, dtype=F32) / half)
    ang = jnp.arange(T, dtype=F32)[:, None] * inv_freq[None, :]
    cos, sin = jnp.cos(ang), jnp.sin(ang)
    return jnp.tile(cos, (1, 4)), jnp.tile(jnp.concatenate([-sin, sin], axis=1), (1, 2))


def _widen_in(a, axis):
    sl = lambda lo, hi: lax.slice_in_dim(a, lo, hi, axis=axis)
    dup = lambda lo: [sl(lo, lo + 64), sl(lo, lo + 64), sl(lo + 64, lo + 128), sl(lo + 64, lo + 128)]
    return jnp.concatenate([sl(0, 512)] + dup(512) + dup(640) + [sl(768, D_IN)], axis=axis)


def _fold_in(a, axis):
    sl = lambda lo, hi: lax.slice_in_dim(a, lo, hi, axis=axis)
    fold = lambda lo: [sl(lo, lo + 64) + sl(lo + 64, lo + 128), sl(lo + 128, lo + 192) + sl(lo + 192, lo + 256)]
    return jnp.concatenate([sl(0, 512)] + fold(512) + fold(768) + [sl(1024, D_INP)], axis=axis)


def _local_step(x, tgt, g_attn, b_in, sinks, rel_table, g_out_a, g_out_b, g_ffn, g_final,
                win_fn, wo_fn, ffn_fn, early_fn):
    T = x.shape[0]
    cos, sin = _rope_tables(T)
    g_final2 = g_final.reshape(1, D_MODEL)
    sink8 = sinks.reshape(N_HEADS)

    bias_a, _ = _band_tables(None, 1, BLK - 1)
    tabs = [_band_tables(rel_table, dil, window // dil) for window, dil in BRANCHES]
    wint, token = win_fn(tabs[2][0])
    winp = _widen_in(wint, 0)
    binp = _widen_in(b_in, 1) + token[0, 0]

    h1, qa, ka, va, *qkv_b = _norm_proj(x, g_attn, winp, binp, cos, sin)
    qbs, kbs, vbs = qkv_b[0:3], qkv_b[3:6], qkv_b[6:9]
    oa, lse_a = _attn_fwd(qa, ka, va, bias_a, sink8, dil=1, kv_pairs=2, use_sink=True, name="attn_a_fwd")
    outs = [_attn_fwd(qbs[n], kbs[n], vbs[n], tabs[n][0], sink8, dil=dil, kv_pairs=4, use_sink=False,
                      name=f"attn_b{n}_fwd") for n, (_, dil) in enumerate(BRANCHES)]
    wo = wo_fn(outs[2][1])
    x2, mixed, h2, *ob_lse = _merge_wo(x, oa, outs[0][0], outs[1][0], outs[2][0], outs[0][1], outs[1][1], outs[2][1],
                                       g_out_a, g_out_b, wo, g_ffn)
    obs, lses = ob_lse[0:3], ob_lse[3:6]
    wgt, wut, wd = ffn_fn(h2)
    gate, up, act = _ffn_up(h2, wgt, wut)
    dx3, dx3b, loss, dg_final = _ffn_down_loss(act, wd, x2, tgt, g_final2)

    dgate, dup, dx2, dx2b, dg_ffn = _ffn_bwd(dx3, gate, up, wd, wgt, wut, x2, g_ffn)
    dwd = _matmul_tn(act, dx3b, tk=1408, tn=1024, name="dw_down")
    dwgt = _matmul_tn(dgate, h2, tk=1408, tn=1024, name="dw_gate")
    dwut = _matmul_tn(dup, h2, tk=1408, tn=1024, name="dw_up")
    doa, *dobs, dg_out_a, dg_out_b, dwo = _wo_bwd(dx2b, wo, oa, obs[0], g_out_a, g_out_b, mixed)
    early, token2 = early_fn(dict(w_o=dwo, w_gate=dwgt, w_up=dwut, w_down=dwd))
    sink8b = sink8 + token2[0, 0]

    dqa, dka, dva, _, dsk = _attn_bwd(qa, ka, va, oa, doa, lse_a, bias_a, sink8b, dil=1, kv_pairs=2, use_sink=True,
                                      name="attn_a_bwd")
    res = [_attn_bwd(qbs[n], kbs[n], vbs[n], obs[n], dobs[n], lses[n], tabs[n][0], sink8b, dil=dil, kv_pairs=4,
                     use_sink=False, name=f"attn_b{n}_bwd") for n, (_, dil) in enumerate(BRANCHES)]
    dbp, grad_x, dg_attn, dwinp = _inproj_bwd(dqa, dka, dva, [r[0] for r in res], [r[1] for r in res],
                                              [r[2] for r in res], cos, sin, winp, x, dx2, g_attn, h1)
    dwin = _fold_in(dwinp, 0)
    drel, dsink = _bias_sink_grads([r[3] for r in res], [t[1] for t in tabs], dsk)

    small = dict(
        g_attn=dg_attn, b_in=_fold_in(dbp, 1), sinks=dsink[:, ::HEAD_DIM], rel_table=drel[:, :REL_BUCKETS].T,
        g_out_a=dg_out_a, g_out_b=dg_out_b, g_ffn=dg_ffn, g_final=dg_final.reshape(D_MODEL))
    return loss[0, 0], grad_x, dwin, early, small


SMALL_NAMES = ("g_attn", "b_in", "sinks", "rel_table", "g_out_a", "g_out_b", "g_ffn", "g_final", "loss")


def _pack_small(vals):
    flat = jnp.concatenate([vals[n].reshape(-1).astype(F32) for n in SMALL_NAMES])
    return jnp.pad(flat, (0, SMALL_ROWS * 128 - flat.shape[0])).reshape(SMALL_ROWS, 128)


def _unpack_small(packed, like):
    flat = packed.reshape(-1)
    out, off = {}, 0
    for n in SMALL_NAMES:
        size = like[n].size
        out[n] = flat[off:off + size].reshape(like[n].shape)
        off += size
    return out


def kernel(x, g_attn, w_in, b_in, sinks, rel_table, g_out_a, g_out_b, w_o, g_ffn, w_gate, w_up, w_down, g_final, loss_target, m_g_attn, m_w_in, m_b_in, m_sinks, m_rel_table, m_g_out_a, m_g_out_b, m_w_o, m_g_ffn, m_w_gate, m_w_up, m_w_down, m_g_final, v_g_attn, v_w_in, v_b_in, v_sinks, v_rel_table, v_g_out_a, v_g_out_b, v_w_o, v_g_ffn, v_w_gate, v_w_up, v_w_down, v_g_final):
    rest_names = ("w_o", "w_gate", "w_up", "w_down")

    rest = [w_o[0].astype(BF16), w_gate[0].astype(BF16).T, w_up[0].astype(BF16).T, w_down[0].astype(BF16)]
    in_state, _ = _exchange_start([w_in[0].astype(BF16).T], gather=True, near=True, name="gather_w_in_start")
    later = {}

    def whole(got):
        return [g.reshape(N_DEV * g.shape[1], D_MODEL) for g in got]

    def win_fn(after):
        near = _exchange_wait(in_state, after, gather=True, near=True, name="gather_w_in_near")
        fwd_state, tok = _forward_start(near, name="gather_w_in_forward")
        wint = whole(_forward_wait(fwd_state, tok, name="gather_w_in_wait"))[0]
        wint, src = lax.optimization_barrier((wint, rest))
        later["wo"], token_o = _exchange_start(src[:1], gather=True, name="gather_w_o_start")
        token_o, ffn_src = lax.optimization_barrier((token_o, src[1:]))
        later["ffn"], token = _exchange_start(ffn_src, gather=True, name="gather_ffn_start")
        return wint, token + token_o

    def wo_fn(after):
        return whole(_exchange_wait(later["wo"], after, gather=True, name="gather_w_o_wait"))[0]

    def ffn_fn(after):
        return whole(_exchange_wait(later["ffn"], after, gather=True, name="gather_ffn_wait"))

    def early_fn(dws):
        return _exchange_start([dws[n].reshape(N_DEV, -1, D_MODEL) for n in rest_names], gather=False,
                               name="scatter_rest_start")

    loss_part, grad_x, dwint, early_state, small = _local_step(
        x[0], loss_target[0], g_attn, b_in, sinks, rel_table, g_out_a, g_out_b, g_ffn, g_final,
        win_fn, wo_fn, ffn_fn, early_fn)
    parts_in = dwint.astype(BF16).reshape(N_DEV, D_IN // N_DEV, D_MODEL)
    in_state, token3 = _exchange_start([parts_in], gather=False, name="scatter_w_in_start")
    got = _exchange_wait(early_state, token3, gather=False, name="scatter_rest_wait")

    def update(n, parts, w, m, v, transposed):
        if transposed:
            return [a.T[None] for a in _adamw(parts, w[0].T, m[0].T, v[0].T, name="adamw_" + n)]
        return [a[None] for a in _adamw(parts, w[0], m[0], v[0], name="adamw_" + n)]

    big = dict(w_o=update("w_o", got[0], w_o, m_w_o, v_w_o, False),
               w_gate=update("w_gate", got[1], w_gate, m_w_gate, v_w_gate, True),
               w_up=update("w_up", got[2], w_up, m_w_up, v_w_up, True),
               w_down=update("w_down", got[3], w_down, m_w_down, v_w_down, False))

    unused = jnp.zeros((1,), F32)
    ws = dict(g_attn=g_attn, b_in=b_in, sinks=sinks, rel_table=rel_table, g_out_a=g_out_a, g_out_b=g_out_b,
              g_ffn=g_ffn, g_final=g_final, loss=unused)
    ms = dict(g_attn=m_g_attn, b_in=m_b_in, sinks=m_sinks, rel_table=m_rel_table, g_out_a=m_g_out_a,
              g_out_b=m_g_out_b, g_ffn=m_g_ffn, g_final=m_g_final, loss=unused)
    vs = dict(g_attn=v_g_attn, b_in=v_b_in, sinks=v_sinks, rel_table=v_rel_table, g_out_a=v_g_out_a,
              g_out_b=v_g_out_b, g_ffn=v_g_ffn, g_final=v_g_final, loss=unused)
    sparts = _all_gather(_pack_small(dict(small, loss=loss_part)), name="gather_small")
    sm_packed = _adamw_small(sparts, _pack_small(ws), _pack_small(ms), _pack_small(vs))
    sm = [_unpack_small(a, ws) for a in sm_packed]
    loss = sm[0]["loss"][0]

    done = sm_packed[1][:1, :1] + sum(big[n][1][0, :1, :1] for n in rest_names)
    got_in = _exchange_wait(in_state, done, gather=False, name="scatter_w_in_wait")[0]
    big["w_in"] = update("w_in", got_in, w_in, m_w_in, v_w_in, True)

    order = ("g_attn", "w_in", "b_in", "sinks", "rel_table", "g_out_a", "g_out_b", "w_o", "g_ffn", "w_gate", "w_up",
             "w_down", "g_final")
    outs = [loss, grad_x[None]]
    for k in range(4):
        outs += [big[n][k] if n in big else sm[k][n] for n in order]
    return tuple(outs)
```

```python
import functools
import math

import jax
import jax.numpy as jnp
from jax import lax
from jax.experimental import pallas as pl
from jax.experimental.pallas import tpu as pltpu

F32 = jnp.float32
BF16 = jnp.bfloat16

N_DEV = 8
D_MODEL = 1024
HEAD_DIM = 64
N_HEADS = 8
PAIR = 2 * HEAD_DIM
WIDTH = N_HEADS * HEAD_DIM
D_IN = 2304
D_INP = 2560
D_FF = 2816
BLK = 128
ROPE_THETA = 150000.0
REL_BUCKETS = 32
REL_MAX_DISTANCE = 2048
EPS = 1e-5
NEG = -1e30
BRANCHES = ((128, 1), (512, 4), (2048, 16))
Q_SCALE = HEAD_DIM ** -0.5

ADAM_LR = 0.001
ADAM_B1 = 0.9
ADAM_B2 = 0.999
ADAM_EPS = 1e-08
ADAM_WD = 0.01
ADAM_STEP = 10

VMEM_LIMIT = 56 * 1024 * 1024
MESH = pl.DeviceIdType.MESH

NT = (((1,), (1,)), ((), ()))
TN = (((0,), (0,)), ((), ()))

SMALL_ROWS = 56


def _params(sem=None):
    return pltpu.CompilerParams(dimension_semantics=sem, vmem_limit_bytes=VMEM_LIMIT)


def _sigmoid(x):
    return 1.0 / (1.0 + jnp.exp2(x * (-1.0 / math.log(2.0))))


def _rms_bwd(dh, xh, r, g):
    u = dh * g
    return r * (u - xh * jnp.mean(u * xh, axis=-1, keepdims=True))


def _rope_rot(t, first):
    return jnp.where(first, pltpu.roll(t, 96, 1), pltpu.roll(t, 32, 1))


N_CHUNK = WIDTH // PAIR


def _scr(tm):
    return pltpu.VMEM((N_CHUNK, tm, PAIR), F32)


def _scr_get(scr):
    return jnp.concatenate([scr[j] for j in range(N_CHUNK)], axis=1)


def _scr_put(scr, val):
    for j in range(N_CHUNK):
        scr[j] = val[:, j * PAIR:(j + 1) * PAIR]


def _unstride(view_ref, scr, dil, tm):
    n = tm // dil
    chunks = scr.shape[0]
    for r in range(dil):
        for j in range(chunks):
            col = (r * chunks + j) * PAIR
            scr.at[j][pl.ds(r, n, stride=dil), :] = view_ref[:, col:col + PAIR].astype(F32)


def _restride(scr, out_ref, dil, tm):
    n = tm // dil
    chunks = scr.shape[0]
    for r in range(dil):
        for j in range(chunks):
            col = (r * chunks + j) * PAIR
            rows = scr[j] if dil == 1 else scr.at[j][pl.ds(r, n, stride=dil), :]
            out_ref[:, col:col + PAIR] = rows.astype(out_ref.dtype)


def _view_specs(tm, width=WIDTH):
    return [pl.BlockSpec((tm // dil, dil * width), lambda i: (i, 0)) for _, dil in BRANCHES]


def _view_shapes(T, dtype, width=WIDTH):
    return [jax.ShapeDtypeStruct((T // dil, dil * width), dtype) for _, dil in BRANCHES]


def _norm_proj(x, g, w, b, cos, sin, *, tm=512):
    T = x.shape[0]

    def body(x_ref, g_ref, w_ref, b_ref, cos_ref, sin_ref, h_ref, qa_ref, ka_ref, va_ref, *rest):
        outs_b, ys = rest[:9], rest[9]
        xv = x_ref[...]
        r = lax.rsqrt(jnp.mean(xv * xv, axis=-1, keepdims=True) + EPS)
        h = (xv * r * g_ref[...]).astype(BF16)
        h_ref[...] = h
        cosv = cos_ref[...]
        sinv = sin_ref[...]
        lane = lax.broadcasted_iota(jnp.int32, (tm, PAIR), 1)
        first = (lane % HEAD_DIM) < (HEAD_DIM // 2)

        def proj(off):
            return (lax.dot_general(h, w_ref[off:off + 256, :], NT, preferred_element_type=F32)
                    + b_ref[:, off:off + 256])

        for (off, width, rot, scale), o_ref in zip(((0, 512, True, Q_SCALE), (512, 256, True, 1.0), (768, 256, False, 1.0)),
                                                   (qa_ref, ka_ref, va_ref)):
            for c in range(0, width, 256):
                y = proj(off + c)
                for j in range(0, 256, PAIR):
                    t = y[:, j:j + PAIR]
                    if rot:
                        t = t * cosv + _rope_rot(t, first) * sinv
                    if scale != 1.0:
                        t = t * scale
                    o_ref[:, c + j:c + j + PAIR] = t.astype(BF16)
        for n, (off, scale) in enumerate(((1024, Q_SCALE), (1536, 1.0), (2048, 1.0))):
            for c in range(0, WIDTH, 256):
                y = proj(off + c)
                y = y * scale if scale != 1.0 else y
                for j in range(0, 256, PAIR):
                    ys[(c + j) // PAIR] = y[:, j:j + PAIR]
            for (_, dil), o_ref in zip(BRANCHES, outs_b[3 * n:3 * n + 3]):
                _restride(ys, o_ref, dil, tm)

    row = lambda w_: pl.BlockSpec((tm, w_), lambda i: (i, 0))
    full = lambda a: pl.BlockSpec(a.shape, lambda i: (0, 0))
    return pl.pallas_call(
        body, name="norm_proj", grid=(T // tm,),
        in_specs=[row(D_MODEL), full(g), full(w), full(b), row(PAIR), row(PAIR)],
        out_specs=[row(D_MODEL), row(512), row(256), row(256)] + _view_specs(tm) * 3,
        out_shape=[jax.ShapeDtypeStruct((T, n), BF16) for n in (D_MODEL, 512, 256, 256)] + _view_shapes(T, BF16) * 3,
        scratch_shapes=[_scr(tm)],
        compiler_params=_params(("arbitrary",)),
    )(x, g, w, b, cos, sin)


SUB = 4
AHEAD = 2


def _attn_specs(kvw):
    q_spec = pl.BlockSpec((SUB * BLK, WIDTH), lambda r, i: (i, r))
    kc_spec = pl.BlockSpec((SUB * BLK, kvw), lambda r, i: (i, r))
    kp_spec = pl.BlockSpec((BLK, kvw), lambda r, i: (jnp.maximum(SUB * i - 1, 0), r))
    b_spec = pl.BlockSpec((2, N_HEADS, BLK, 2 * BLK), lambda r, i: (0, 0, 0, 0))
    return q_spec, kp_spec, kc_spec, b_spec


def _window(prev_ref, cur_ref, j, ksl):
    before = prev_ref[:, ksl] if j == 0 else cur_ref[(j - 1) * BLK:j * BLK, ksl]
    return jnp.concatenate([before, cur_ref[j * BLK:(j + 1) * BLK, ksl]], axis=0)


def _attn_fwd(q, k, v, bias, sinks, *, dil, kv_pairs, use_sink, name):
    L = q.shape[0]
    ns = L // (SUB * BLK)
    kvw = kv_pairs * PAIR
    rep = 4 // kv_pairs

    def body(sink_ref, q_ref, kp_ref, kc_ref, vp_ref, vc_ref, b_ref, o_ref, lse_ref):
        lane = lax.broadcasted_iota(jnp.int32, (1, PAIR), 1)
        lo = lane < HEAD_DIM
        first = jnp.where(pl.program_id(1) == 0, 1, 0)
        def scores(j, hp):
            rows = slice(j * BLK, (j + 1) * BLK)
            sl = slice(hp * PAIR, (hp + 1) * PAIR)
            ksl = slice((hp // rep) * PAIR, (hp // rep + 1) * PAIR)
            qp = q_ref[rows, sl]
            kk = _window(kp_ref, kc_ref, j, ksl)
            vv = _window(vp_ref, vc_ref, j, ksl)
            heads = []
            for e in range(2):
                h = 2 * hp + e
                msk = lo if e == 0 else jnp.logical_not(lo)
                qm = jnp.where(msk, qp, jnp.zeros_like(qp))
                s = lax.dot_general(qm, kk, NT, preferred_element_type=F32) + (b_ref[first, h] if j == 0 else b_ref[0, h])
                heads.append((h, msk, s))
            return rows, sl, vv, heads

        def outputs(rows, sl, vv, heads):
            o_pair = None
            lse_pair = None
            for h, msk, s in heads:
                m = jnp.max(s, axis=-1, keepdims=True)
                if use_sink:
                    sk = sink_ref[h]
                    m = jnp.maximum(m, sk)
                p = jnp.exp(s - m)
                l = jnp.sum(p, axis=-1, keepdims=True)
                if use_sink:
                    l = l + jnp.exp(sk - m)
                vm = jnp.where(msk, vv, jnp.zeros_like(vv))
                oe = jnp.dot(p.astype(BF16), vm, preferred_element_type=F32) * (1.0 / l)
                ls = m + jnp.log(l)
                if o_pair is None:
                    o_pair = oe
                    lse_pair = jnp.broadcast_to(ls, (BLK, PAIR))
                else:
                    o_pair = o_pair + oe
                    lse_pair = jnp.where(lo, lse_pair, ls)
            o_ref[rows, sl] = o_pair.astype(BF16)
            lse_ref[rows, sl] = lse_pair

        items = [(j, hp) for j in range(SUB) for hp in range(4)]
        queue = [scores(*it) for it in items[:AHEAD]]
        for n in range(len(items)):
            if n + AHEAD < len(items):
                queue.append(scores(*items[n + AHEAD]))
            outputs(*queue.pop(0))

    q_spec, kp_spec, kc_spec, b_spec = _attn_specs(kvw)
    return pl.pallas_call(
        body, name=name, grid=(dil, ns),
        in_specs=[pl.BlockSpec(memory_space=pltpu.SMEM), q_spec, kp_spec, kc_spec, kp_spec, kc_spec, b_spec],
        out_specs=[q_spec, q_spec],
        out_shape=[jax.ShapeDtypeStruct((L, dil * WIDTH), BF16), jax.ShapeDtypeStruct((L, dil * WIDTH), F32)],
        compiler_params=_params(("arbitrary", "arbitrary")),
    )(sinks, q, k, k, v, v, bias)


def _attn_bwd(q, k, v, o, do, lse, bias, sinks, *, dil, kv_pairs, use_sink, name):
    L = q.shape[0]
    ns = L // (SUB * BLK)
    n_steps = dil * ns
    kvw = kv_pairs * PAIR
    rep = 4 // kv_pairs
    last = slice((SUB - 1) * BLK, SUB * BLK)

    def body(sink_ref, q_ref, kp_ref, kc_ref, vp_ref, vc_ref, o_ref, do_ref, lse_ref, b_ref,
             dq_ref, dk_ref, dv_ref, dsum_ref, dsk_ref, pk_ref, pv_ref):
        t = pl.program_id(0)
        i = t % ns

        @pl.when(t == 0)
        def _():
            dsum_ref[...] = jnp.zeros_like(dsum_ref)
            dsk_ref[...] = jnp.zeros_like(dsk_ref)
            pk_ref[...] = jnp.zeros_like(pk_ref)
            pv_ref[...] = jnp.zeros_like(pv_ref)

        @pl.when(t < n_steps)
        def _():
            lo = lax.broadcasted_iota(jnp.int32, (1, PAIR), 1) < HEAD_DIM
            first = jnp.where(i == 0, 1, 0)
            dks = [[None] * kv_pairs for _ in range(SUB)]
            dvs = [[None] * kv_pairs for _ in range(SUB)]
            def scores(j, hp):
                rows = slice(j * BLK, (j + 1) * BLK)
                kvp = hp // rep
                sl = slice(hp * PAIR, (hp + 1) * PAIR)
                ksl = slice(kvp * PAIR, (kvp + 1) * PAIR)
                qp = q_ref[rows, sl]
                dop = do_ref[rows, sl]
                prod = dop.astype(F32) * o_ref[rows, sl].astype(F32)
                kk = _window(kp_ref, kc_ref, j, ksl)
                vv = _window(vp_ref, vc_ref, j, ksl)
                heads = []
                for e in range(2):
                    h = 2 * hp + e
                    msk = lo if e == 0 else jnp.logical_not(lo)
                    qm = jnp.where(msk, qp, jnp.zeros_like(qp))
                    dom = jnp.where(msk, dop, jnp.zeros_like(dop))
                    km = jnp.where(msk, kk, jnp.zeros_like(kk))
                    s = (lax.dot_general(qm, kk, NT, preferred_element_type=F32)
                         + (b_ref[first, h] if j == 0 else b_ref[0, h]))
                    dp = lax.dot_general(dom, vv, NT, preferred_element_type=F32)
                    heads.append((h, msk, qm, dom, km, s, dp))
                return j, rows, kvp, sl, prod, heads

            def grads(j, rows, kvp, sl, prod, heads):
                dq_pair = None
                c_pair = None
                qms, doms, dsbs, pbs = [], [], [], []
                for h, msk, qm, dom, km, s, dp in heads:
                    ls = lse_ref[rows, h * HEAD_DIM:h * HEAD_DIM + 1]
                    p = jnp.exp(s - ls)
                    delta = jnp.sum(jnp.where(msk, prod, 0.0), axis=-1, keepdims=True)
                    ds = p * (dp - delta)
                    if use_sink:
                        ce = jnp.exp(sink_ref[h] - ls) * delta
                        c_pair = jnp.broadcast_to(ce, (BLK, PAIR)) if c_pair is None else jnp.where(msk, ce, c_pair)
                    else:
                        dsum_ref[h] += ds
                    dsb = ds.astype(BF16)
                    dqe = jnp.dot(dsb, km, preferred_element_type=F32)
                    dq_pair = dqe if dq_pair is None else dq_pair + dqe
                    qms.append(qm)
                    doms.append(dom)
                    dsbs.append(dsb)
                    pbs.append(p.astype(BF16))
                dke = lax.dot_general(jnp.concatenate(dsbs, axis=0), jnp.concatenate(qms, axis=0), TN,
                                      preferred_element_type=F32)
                dve = lax.dot_general(jnp.concatenate(pbs, axis=0), jnp.concatenate(doms, axis=0), TN,
                                      preferred_element_type=F32)
                dks[j][kvp] = dke if dks[j][kvp] is None else dks[j][kvp] + dke
                dvs[j][kvp] = dve if dvs[j][kvp] is None else dvs[j][kvp] + dve
                dq_ref[rows, sl] = (dq_pair * Q_SCALE).astype(BF16)
                if use_sink:
                    dsk_ref[:, sl] += c_pair

            items = [(j, hp) for j in range(SUB) for hp in range(4)]
            ahead = AHEAD + 1 if use_sink else AHEAD
            queue = [scores(*it) for it in items[:ahead]]
            for n in range(len(items)):
                if n + ahead < len(items):
                    queue.append(scores(*items[n + ahead]))
                grads(*queue.pop(0))
            for kvp in range(kv_pairs):
                ksl = slice(kvp * PAIR, (kvp + 1) * PAIR)
                for pend_ref, out_ref, parts in ((pk_ref, dk_ref, [d[kvp] for d in dks]),
                                                 (pv_ref, dv_ref, [d[kvp] for d in dvs])):
                    if SUB > 1:
                        out_ref[:(SUB - 1) * BLK, ksl] = pend_ref[:(SUB - 1) * BLK, ksl].astype(BF16)
                    out_ref[last, ksl] = (pend_ref[last, ksl] + parts[0][:BLK]).astype(BF16)
                    for j in range(SUB):
                        own = parts[j][BLK:]
                        pend_ref[j * BLK:(j + 1) * BLK, ksl] = own + parts[j + 1][:BLK] if j + 1 < SUB else own

        @pl.when(t == n_steps)
        def _():
            dk_ref[...] = pk_ref[...].astype(BF16)
            dv_ref[...] = pv_ref[...].astype(BF16)

    def at(t):
        t = jnp.minimum(t, n_steps - 1)
        return t % ns, t // ns

    def before(t):
        return at(jnp.maximum(t - 1, 0))

    q_spec = pl.BlockSpec((SUB * BLK, WIDTH), at)
    kc_spec = pl.BlockSpec((SUB * BLK, kvw), at)
    kp_spec = pl.BlockSpec((BLK, kvw), lambda t: (jnp.maximum(SUB * at(t)[0] - 1, 0), at(t)[1]))
    b_spec = pl.BlockSpec((2, N_HEADS, BLK, 2 * BLK), lambda t: (0, 0, 0, 0))
    dkv_spec = pl.BlockSpec((SUB * BLK, kvw), before)
    return pl.pallas_call(
        body, name=name, grid=(n_steps + 1,),
        in_specs=[pl.BlockSpec(memory_space=pltpu.SMEM), q_spec, kp_spec, kc_spec, kp_spec, kc_spec,
                  q_spec, q_spec, q_spec, b_spec],
        out_specs=[q_spec, dkv_spec, dkv_spec,
                   pl.BlockSpec((N_HEADS, BLK, 2 * BLK), lambda t: (0, 0, 0)),
                   pl.BlockSpec((BLK, WIDTH), lambda t: (0, 0))],
        out_shape=[jax.ShapeDtypeStruct((L, dil * WIDTH), BF16),
                   jax.ShapeDtypeStruct((L, dil * kvw), BF16),
                   jax.ShapeDtypeStruct((L, dil * kvw), BF16),
                   jax.ShapeDtypeStruct((N_HEADS, BLK, 2 * BLK), F32),
                   jax.ShapeDtypeStruct((BLK, WIDTH), F32)],
        scratch_shapes=[pltpu.VMEM((SUB * BLK, kvw), F32), pltpu.VMEM((SUB * BLK, kvw), F32)],
        compiler_params=_params(("arbitrary",)),
    )(sinks, q, k, k, v, v, o, do, lse, bias)


def _merge_wo(x, oa, o1, o2, o3, l1, l2, l3, ga, gb, wo, gf, *, tm=512):
    T = x.shape[0]

    def body(x_ref, oa_ref, o1_ref, o2_ref, o3_ref, l1_ref, l2_ref, l3_ref, ga_ref, gb_ref, wo_ref, gf_ref,
             x2_ref, mix_ref, h2_ref, ob1_ref, ob4_ref, ob16_ref, ls1_ref, ls4_ref, ls16_ref, so2, so3, sl2, sl3):
        _unstride(o2_ref, so2, BRANCHES[1][1], tm)
        _unstride(o3_ref, so3, BRANCHES[2][1], tm)
        _unstride(l2_ref, sl2, BRANCHES[1][1], tm)
        _unstride(l3_ref, sl3, BRANCHES[2][1], tm)
        la, lb, lc = l1_ref[...], _scr_get(sl2), _scr_get(sl3)
        m = jnp.maximum(jnp.maximum(la, lb), lc)
        ea, eb, ec = jnp.exp(la - m), jnp.exp(lb - m), jnp.exp(lc - m)
        den = ea + eb + ec
        inv = 1.0 / den
        ob = (ea * o1_ref[...].astype(F32) + eb * _scr_get(so2) + ec * _scr_get(so3)) * inv
        _scr_put(so2, ob)
        _scr_put(sl2, m + jnp.log(den))
        for (_, dil), o_ref, l_ref in zip(BRANCHES, (ob1_ref, ob4_ref, ob16_ref), (ls1_ref, ls4_ref, ls16_ref)):
            _restride(so2, o_ref, dil, tm)
            _restride(sl2, l_ref, dil, tm)
        oav = oa_ref[...].astype(F32)
        ra = lax.rsqrt(jnp.mean(oav * oav, axis=-1, keepdims=True) + EPS)
        rb = lax.rsqrt(jnp.mean(ob * ob, axis=-1, keepdims=True) + EPS)
        mix_ref[:, :WIDTH] = (oav * ra * ga_ref[...]).astype(BF16)
        mix_ref[:, WIDTH:] = (ob * rb * gb_ref[...]).astype(BF16)
        x2 = x_ref[...] + jnp.dot(mix_ref[...], wo_ref[...], preferred_element_type=F32)
        x2_ref[...] = x2
        r2 = lax.rsqrt(jnp.mean(x2 * x2, axis=-1, keepdims=True) + EPS)
        h2_ref[...] = (x2 * r2 * gf_ref[...]).astype(BF16)

    row = lambda w_: pl.BlockSpec((tm, w_), lambda i: (i, 0))
    full = lambda a: pl.BlockSpec(a.shape, lambda i: (0, 0))
    return pl.pallas_call(
        body, name="merge_wo", grid=(T // tm,),
        in_specs=[row(D_MODEL), row(WIDTH)] + _view_specs(tm) * 2 + [full(ga), full(gb), full(wo), full(gf)],
        out_specs=[row(D_MODEL), row(D_MODEL), row(D_MODEL)] + _view_specs(tm) * 2,
        out_shape=[jax.ShapeDtypeStruct((T, D_MODEL), F32), jax.ShapeDtypeStruct((T, D_MODEL), BF16),
                   jax.ShapeDtypeStruct((T, D_MODEL), BF16)] + _view_shapes(T, BF16) + _view_shapes(T, F32),
        scratch_shapes=[_scr(tm)] * 4,
        compiler_params=_params(("arbitrary",)),
    )(x, oa, o1, o2, o3, l1, l2, l3, ga, gb, wo, gf)


def _ffn_up(h2, wgt, wut, *, tm=512, fc=D_FF, rc=512, cc=256):
    T = h2.shape[0]

    def body(h_ref, wg_ref, wu_ref, gate_ref, up_ref, act_ref):
        for s in range(0, tm, rc):
            h = h_ref[s:s + rc, :]
            for c in range(0, fc, cc):
                gt = lax.dot_general(h, wg_ref[c:c + cc, :], NT, preferred_element_type=F32)
                u = lax.dot_general(h, wu_ref[c:c + cc, :], NT, preferred_element_type=F32)
                gate_ref[s:s + rc, c:c + cc] = gt.astype(BF16)
                up_ref[s:s + rc, c:c + cc] = u.astype(BF16)
                act_ref[s:s + rc, c:c + cc] = (gt * _sigmoid(gt) * u).astype(BF16)

    rowd = pl.BlockSpec((tm, D_MODEL), lambda i, c: (i, 0))
    wrow = pl.BlockSpec((fc, D_MODEL), lambda i, c: (c, 0))
    oc = pl.BlockSpec((tm, fc), lambda i, c: (i, c))
    return pl.pallas_call(
        body, name="ffn_up", grid=(T // tm, D_FF // fc),
        in_specs=[rowd, wrow, wrow],
        out_specs=[oc, oc, oc],
        out_shape=[jax.ShapeDtypeStruct((T, D_FF), BF16)] * 3,
        compiler_params=_params(("arbitrary", "arbitrary")),
    )(h2, wgt, wut)


def _ffn_down_loss(act, wd, x2, tgt, g, *, tm=512, rc=256):
    T = x2.shape[0]

    def body(act_ref, wd_ref, x2_ref, tgt_ref, g_ref, dx_ref, dxb_ref, loss_ref, dg_ref):
        @pl.when(pl.program_id(0) == 0)
        def _():
            loss_ref[...] = jnp.zeros_like(loss_ref)
            dg_ref[...] = jnp.zeros_like(dg_ref)

        gv = g_ref[...]
        lsum = jnp.zeros((1, 1), F32)
        dgs = jnp.zeros((1, D_MODEL), F32)
        for c in range(0, tm, rc):
            x3 = x2_ref[c:c + rc, :] + jnp.dot(act_ref[c:c + rc, :], wd_ref[...], preferred_element_type=F32)
            r = lax.rsqrt(jnp.mean(x3 * x3, axis=-1, keepdims=True) + EPS)
            xh = x3 * r
            diff = xh * gv - tgt_ref[c:c + rc, :]
            lsum = lsum + jnp.sum(jnp.sum(diff * diff, axis=-1, keepdims=True), axis=0, keepdims=True)
            dy = diff * (1.0 / D_MODEL)
            dgs = dgs + jnp.sum(dy * xh, axis=0, keepdims=True)
            dx = _rms_bwd(dy, xh, r, gv)
            dx_ref[c:c + rc, :] = dx
            dxb_ref[c:c + rc, :] = dx.astype(BF16)
        loss_ref[...] += lsum * (0.5 / D_MODEL)
        dg_ref[...] += dgs

    rowd = pl.BlockSpec((tm, D_MODEL), lambda i: (i, 0))
    return pl.pallas_call(
        body, name="ffn_down_loss", grid=(T // tm,),
        in_specs=[pl.BlockSpec((tm, D_FF), lambda i: (i, 0)), pl.BlockSpec((D_FF, D_MODEL), lambda i: (0, 0)),
                  rowd, rowd, pl.BlockSpec(g.shape, lambda i: (0, 0))],
        out_specs=[rowd, rowd, pl.BlockSpec((1, 1), lambda i: (0, 0)), pl.BlockSpec((1, D_MODEL), lambda i: (0, 0))],
        out_shape=[jax.ShapeDtypeStruct((T, D_MODEL), F32), jax.ShapeDtypeStruct((T, D_MODEL), BF16),
                   jax.ShapeDtypeStruct((1, 1), F32), jax.ShapeDtypeStruct((1, D_MODEL), F32)],
        compiler_params=_params(("arbitrary",)),
    )(act, wd, x2, tgt, g)


def _ffn_bwd(dx3, gate, up, wd, wgt, wut, x2, g, *, tm=256, cc=256):
    T = x2.shape[0]

    def body(dx_ref, gate_ref, up_ref, wd_ref, wg_ref, wu_ref, x2_ref, g_ref,
             dgate_ref, dup_ref, dx2_ref, dx2b_ref, dg_ref):
        @pl.when(pl.program_id(0) == 0)
        def _():
            dg_ref[...] = jnp.zeros_like(dg_ref)

        dxb = dx_ref[...].astype(BF16)
        for c in range(0, D_FF, cc):
            dact = lax.dot_general(dxb, wd_ref[c:c + cc, :], NT, preferred_element_type=F32)
            gt = gate_ref[:, c:c + cc].astype(F32)
            u = up_ref[:, c:c + cc].astype(F32)
            sg = _sigmoid(gt)
            a = dact * sg
            dgate_ref[:, c:c + cc] = (a * u * ((1.0 + gt) - gt * sg)).astype(BF16)
            dup_ref[:, c:c + cc] = (a * gt).astype(BF16)
        dh = (jnp.dot(dgate_ref[...], wg_ref[...], preferred_element_type=F32)
              + jnp.dot(dup_ref[...], wu_ref[...], preferred_element_type=F32))
        xv = x2_ref[...]
        r = lax.rsqrt(jnp.mean(xv * xv, axis=-1, keepdims=True) + EPS)
        xh = xv * r
        dg_ref[...] += jnp.sum(dh * xh, axis=0, keepdims=True)
        d = dx_ref[...] + _rms_bwd(dh, xh, r, g_ref[...])
        dx2_ref[...] = d
        dx2b_ref[...] = d.astype(BF16)

    rowd = pl.BlockSpec((tm, D_MODEL), lambda i: (i, 0))
    rowf = pl.BlockSpec((tm, D_FF), lambda i: (i, 0))
    wfull = pl.BlockSpec((D_FF, D_MODEL), lambda i: (0, 0), pipeline_mode=pl.Buffered(1))
    return pl.pallas_call(
        body, name="ffn_bwd", grid=(T // tm,),
        in_specs=[rowd, rowf, rowf, wfull, wfull, wfull, rowd, pl.BlockSpec(g.shape, lambda i: (0, 0))],
        out_specs=[rowf, rowf, rowd, rowd, pl.BlockSpec((1, D_MODEL), lambda i: (0, 0))],
        out_shape=[jax.ShapeDtypeStruct((T, D_FF), BF16), jax.ShapeDtypeStruct((T, D_FF), BF16),
                   jax.ShapeDtypeStruct((T, D_MODEL), F32), jax.ShapeDtypeStruct((T, D_MODEL), BF16),
                   jax.ShapeDtypeStruct((1, D_MODEL), F32)],
        compiler_params=_params(("arbitrary",)),
    )(dx3, gate, up, wd, wgt, wut, x2, g)


def _matmul_tn(a, b, *, tk, tn, tt=2048, out_dtype=BF16, name):
    T, K = a.shape
    N = b.shape[1]
    nt = T // tt

    def body(a_ref, b_ref, o_ref, acc_ref):
        part = lax.dot_general(a_ref[...], b_ref[...], TN, preferred_element_type=F32)

        @pl.when(pl.program_id(2) == 0)
        def _():
            acc_ref[...] = part

        @pl.when(pl.program_id(2) > 0)
        def _():
            acc_ref[...] += part

        @pl.when(pl.program_id(2) == nt - 1)
        def _():
            o_ref[...] = acc_ref[...].astype(out_dtype)

    return pl.pallas_call(
        body, name=name, grid=(K // tk, N // tn, nt),
        in_specs=[pl.BlockSpec((tt, tk), lambda i, j, t: (t, i)), pl.BlockSpec((tt, tn), lambda i, j, t: (t, j))],
        out_specs=pl.BlockSpec((tk, tn), lambda i, j, t: (i, j)),
        out_shape=jax.ShapeDtypeStruct((K, N), out_dtype),
        scratch_shapes=[pltpu.VMEM((tk, tn), F32)],
        compiler_params=_params(("arbitrary", "arbitrary", "arbitrary")),
    )(a, b)


def _wo_bwd(dx2b, wo, oa, ob, ga, gb, mixed, *, tm=512, wc=256):
    T = dx2b.shape[0]
    n_tiles = T // tm

    def body(dx_ref, wo_ref, oa_ref, ob_ref, ga_ref, gb_ref, mix_ref,
             doa_ref, dob1_ref, dob4_ref, dob16_ref, dga_ref, dgb_ref, dwo_ref, scr, dw_acc):
        @pl.when(pl.program_id(0) == 0)
        def _():
            dga_ref[...] = jnp.zeros_like(dga_ref)
            dgb_ref[...] = jnp.zeros_like(dgb_ref)
            dw_acc[...] = jnp.zeros_like(dw_acc)

        dxv = dx_ref[...]
        for c in range(0, D_MODEL, wc):
            dw_acc[c:c + wc, :] += lax.dot_general(mix_ref[:, c:c + wc], dxv, TN, preferred_element_type=F32)

        @pl.when(pl.program_id(0) == n_tiles - 1)
        def _():
            dwo_ref[...] = dw_acc[...].astype(BF16)

        dm = lax.dot_general(dxv, wo_ref[...], NT, preferred_element_type=F32)
        for o_ref, g_ref, dg_ref, sl in ((oa_ref, ga_ref, dga_ref, slice(0, WIDTH)),
                                         (ob_ref, gb_ref, dgb_ref, slice(WIDTH, 2 * WIDTH))):
            ov = o_ref[...].astype(F32)
            r = lax.rsqrt(jnp.mean(ov * ov, axis=-1, keepdims=True) + EPS)
            xh = ov * r
            d = dm[:, sl]
            dg_ref[...] += jnp.sum(d * xh, axis=0, keepdims=True)
            do = _rms_bwd(d, xh, r, g_ref[...])
            if o_ref is oa_ref:
                doa_ref[...] = do.astype(BF16)
            else:
                _scr_put(scr, do)
                for (_, dil), v_ref in zip(BRANCHES, (dob1_ref, dob4_ref, dob16_ref)):
                    _restride(scr, v_ref, dil, tm)

    row = lambda w_: pl.BlockSpec((tm, w_), lambda i: (i, 0))
    full = lambda a: pl.BlockSpec(a.shape, lambda i: (0, 0))
    return pl.pallas_call(
        body, name="wo_bwd", grid=(T // tm,),
        in_specs=[row(D_MODEL), full(wo), row(WIDTH), row(WIDTH), full(ga), full(gb), row(D_MODEL)],
        out_specs=[row(WIDTH)] + _view_specs(tm)
        + [pl.BlockSpec((1, WIDTH), lambda i: (0, 0)), pl.BlockSpec((1, WIDTH), lambda i: (0, 0)), full(wo)],
        out_shape=[jax.ShapeDtypeStruct((T, WIDTH), BF16)] + _view_shapes(T, BF16)
        + [jax.ShapeDtypeStruct((1, WIDTH), F32), jax.ShapeDtypeStruct((1, WIDTH), F32),
           jax.ShapeDtypeStruct((D_MODEL, D_MODEL), BF16)],
        scratch_shapes=[_scr(tm), pltpu.VMEM((D_MODEL, D_MODEL), F32)],
        compiler_params=_params(("arbitrary",)),
    )(dx2b, wo, oa, ob, ga, gb, mixed)


def _inproj_bwd(dqa, dka, dva, dqs, dks, dvs, cos, sin, w, x, dx2, g, h1, *, tm=512, wc=256):
    T = dqa.shape[0]
    n_tiles = T // tm

    def body(dqa_ref, dka_ref, dva_ref, q1, q2, q3, k1, k2, k3, v1, v2, v3, cos_ref, sin_ref, w_ref, x_ref, dx2_ref,
             g_ref, h1_ref, db_ref, gx_ref, dg_ref, dw_ref, dp_ref, dw_acc, acc, tmp):
        @pl.when(pl.program_id(0) == 0)
        def _():
            db_ref[...] = jnp.zeros_like(db_ref)
            dg_ref[...] = jnp.zeros_like(dg_ref)
            dw_acc[...] = jnp.zeros_like(dw_acc)

        cosv = cos_ref[...]
        sinv = sin_ref[...]
        lane = lax.broadcasted_iota(jnp.int32, (tm, PAIR), 1)
        first = (lane % HEAD_DIM) < (HEAD_DIM // 2)

        def put(off, val):
            dp_ref[:, off:off + PAIR] = val.astype(BF16)
            db_ref[:, off:off + PAIR] += jnp.sum(val, axis=0, keepdims=True)

        for src, off, width in ((dqa_ref, 0, 512), (dka_ref, 512, 256)):
            for j in range(0, width, PAIR):
                d = src[:, j:j + PAIR].astype(F32)
                put(off + j, d * cosv - _rope_rot(d, first) * sinv)
        for j in range(0, 256, PAIR):
            put(768 + j, dva_ref[:, j:j + PAIR].astype(F32))
        for (a, b, c), off in (((q1, q2, q3), 1024), ((k1, k2, k3), 1536), ((v1, v2, v3), 2048)):
            _unstride(b, acc, BRANCHES[1][1], tm)
            _unstride(c, tmp, BRANCHES[2][1], tm)
            for j in range(N_CHUNK):
                put(off + j * PAIR, a[:, j * PAIR:(j + 1) * PAIR].astype(F32) + acc[j] + tmp[j])

        dh = jnp.dot(dp_ref[...], w_ref[...], preferred_element_type=F32)
        xv = x_ref[...]
        r = lax.rsqrt(jnp.mean(xv * xv, axis=-1, keepdims=True) + EPS)
        xh = xv * r
        dg_ref[...] += jnp.sum(dh * xh, axis=0, keepdims=True)
        gx_ref[...] = dx2_ref[...] + _rms_bwd(dh, xh, r, g_ref[...])

        h1v = h1_ref[...]
        for c in range(0, D_INP, wc):
            dw_acc[c:c + wc, :] += lax.dot_general(dp_ref[:, c:c + wc], h1v, TN, preferred_element_type=F32)

        @pl.when(pl.program_id(0) == n_tiles - 1)
        def _():
            pltpu.sync_copy(dw_acc, dw_ref)

    row = lambda w_: pl.BlockSpec((tm, w_), lambda i: (i, 0))
    full = lambda a: pl.BlockSpec(a.shape, lambda i: (0, 0))
    return pl.pallas_call(
        body, name="inproj_bwd", grid=(n_tiles,),
        in_specs=[row(512), row(256), row(256)] + _view_specs(tm) * 3 + [row(PAIR), row(PAIR)]
        + [pl.BlockSpec(w.shape, lambda i: (0, 0), pipeline_mode=pl.Buffered(1)), row(D_MODEL), row(D_MODEL), full(g),
           row(D_MODEL)],
        out_specs=[pl.BlockSpec((1, D_INP), lambda i: (0, 0)), row(D_MODEL),
                   pl.BlockSpec((1, D_MODEL), lambda i: (0, 0)), pl.BlockSpec(memory_space=pl.ANY)],
        out_shape=[jax.ShapeDtypeStruct((1, D_INP), F32), jax.ShapeDtypeStruct((T, D_MODEL), F32),
                   jax.ShapeDtypeStruct((1, D_MODEL), F32), jax.ShapeDtypeStruct((D_INP, D_MODEL), F32)],
        scratch_shapes=[pltpu.VMEM((tm, D_INP), BF16), pltpu.VMEM((D_INP, D_MODEL), F32), _scr(tm), _scr(tm)],
        compiler_params=_params(("arbitrary",)),
    )(dqa, dka, dva, *dqs, *dks, *dvs, cos, sin, w, x, dx2, g, h1)


def _bias_sink_grads(dsums, bmaps, dsk):
    def body(s1, s2, s3, m1, m2, m3, dsk_ref, drel_ref, dsink_ref):
        row = lax.broadcasted_iota(jnp.int32, (N_HEADS, 128), 0)
        lane = lax.broadcasted_iota(jnp.int32, (N_HEADS, 128), 1)
        out = jnp.zeros((N_HEADS, 128), F32)
        for s_ref, m_ref in ((s1, m1), (s2, m2), (s3, m3)):
            bm = m_ref[...]
            for h in range(N_HEADS):
                a = s_ref[h]
                for b in range(REL_BUCKETS):
                    v = jnp.sum(jnp.sum(jnp.where(bm == b, a, 0.0), axis=-1, keepdims=True), axis=0, keepdims=True)
                    out = out + jnp.where((row == h) & (lane == b), v, 0.0)
        drel_ref[...] = out
        dsink_ref[...] = -jnp.sum(dsk_ref[...], axis=0, keepdims=True)

    vm = pl.BlockSpec(memory_space=pltpu.VMEM)
    return pl.pallas_call(
        body, name="bias_sink_grads",
        in_specs=[vm] * 7, out_specs=[vm, vm],
        out_shape=[jax.ShapeDtypeStruct((N_HEADS, 128), F32), jax.ShapeDtypeStruct((1, WIDTH), F32)],
        compiler_params=_params(),
    )(*dsums, *bmaps, dsk)


def _all_gather(blk, *, name):
    R, C = blk.shape

    def body(x_ref, out_ref, send_sems, recv_sems, local_sem):
        x, y, c = lax.axis_index("x"), lax.axis_index("y"), lax.axis_index("c")
        me, sibling = (x, y, c), (x, y, 1 - c)
        chips = [(1 - x, y), (x, 1 - y), (1 - x, 1 - y)]

        def slot(px, py, pc):
            return out_ref.at[4 * px + 2 * py + pc]

        def copy(k, block, to, src=None):
            return pltpu.make_async_remote_copy(
                src_ref=slot(*block) if src is None else src, dst_ref=slot(*block),
                send_sem=send_sems.at[k], recv_sem=recv_sems.at[k], device_id=to, device_id_type=MESH)

        mine = pltpu.make_async_copy(x_ref, slot(*me), local_sem)
        mine.start()
        first = [copy(0, me, sibling, src=x_ref)]
        first += [copy(1 + j, me, (*chip, c), src=x_ref) for j, chip in enumerate(chips)]
        for cp in first:
            cp.start()
        passed = [copy(4 + j, (*chip, c), sibling) for j, chip in enumerate(chips)]
        for j, chip in enumerate(chips):
            copy(1 + j, (*chip, c), me).wait_recv()
            passed[j].start()
        copy(0, sibling, me).wait_recv()
        for j, chip in enumerate(chips):
            copy(4 + j, (*chip, 1 - c), me).wait_recv()
        for cp in first + passed:
            cp.wait_send()
        mine.wait()

    return pl.pallas_call(
        body, name=name,
        in_specs=[pl.BlockSpec(memory_space=pl.ANY)], out_specs=pl.BlockSpec(memory_space=pl.ANY),
        out_shape=jax.ShapeDtypeStruct((N_DEV, R, C), blk.dtype),
        scratch_shapes=[pltpu.SemaphoreType.DMA((7,)), pltpu.SemaphoreType.DMA((7,)), pltpu.SemaphoreType.DMA],
        compiler_params=pltpu.CompilerParams(has_side_effects=True),
    )(blk)


def _peers(x, y, c):
    return [(x ^ (k >> 2), y ^ ((k >> 1) & 1), c ^ (k & 1)) for k in range(1, N_DEV)]


_HBM = pl.BlockSpec(memory_space=pltpu.HBM)
_SEM = pl.BlockSpec(memory_space=pltpu.SEMAPHORE)
_EFFECT = pltpu.SideEffectType.DATAFLOW_SIDE_EFFECTING


def _peer_list(x, y, c, near):
    if near:
        return [(x, y, 1 - c), (1 - x, y, c), (x, 1 - y, c), (1 - x, 1 - y, c)]
    return _peers(x, y, c)


def _exchange_start(srcs, *, gather, name, near=False):
    n = len(srcs)
    n_peers = 4 if near else N_DEV - 1
    lands = [lax.empty((N_DEV,) + s.shape[-2:], s.dtype) for s in srcs]

    def body(*refs):
        src_refs, land_refs = refs[:n], refs[n:2 * n]
        send_sems, recv_sems = refs[2 * n], refs[2 * n + 1]
        token = refs[-1]
        x, y, c = lax.axis_index("x"), lax.axis_index("y"), lax.axis_index("c")
        mine = 4 * x + 2 * y + c
        for a in range(n):
            for k, peer in enumerate(_peer_list(x, y, c, near)):
                dest = 4 * peer[0] + 2 * peer[1] + peer[2]
                j = a * n_peers + k
                pltpu.make_async_remote_copy(
                    src_ref=src_refs[a] if gather else src_refs[a].at[dest], dst_ref=land_refs[a].at[mine],
                    send_sem=send_sems.at[j], recv_sem=recv_sems.at[j], device_id=peer, device_id_type=MESH).start()
        token[...] = jnp.zeros_like(token)

    sems = pltpu.SemaphoreType.DMA((n * n_peers,))
    out = pl.pallas_call(
        body, name=name,
        out_shape=(sems, sems) + tuple(pltpu.HBM(a.shape, a.dtype) for a in list(srcs) + lands)
        + (jax.ShapeDtypeStruct((8, 128), F32),),
        in_specs=(_HBM,) * (2 * n), out_specs=(_SEM, _SEM) + (_HBM,) * (2 * n) + (pl.BlockSpec(memory_space=pltpu.VMEM),),
        input_output_aliases={i: 2 + i for i in range(2 * n)},
        compiler_params=pltpu.CompilerParams(has_side_effects=_EFFECT),
    )(*[pltpu.with_memory_space_constraint(a, pltpu.HBM) for a in list(srcs) + lands])
    return out[:-1], out[-1]


def _exchange_wait(state, after, *, gather, name, near=False):
    send_sems, recv_sems = state[0], state[1]
    n = (len(state) - 2) // 2
    n_peers = 4 if near else N_DEV - 1
    arrays = state[2:]

    def body(*refs):
        src_refs, land_refs = refs[:n], refs[n:2 * n]
        send_sems, recv_sems = refs[2 * n], refs[2 * n + 1]
        x, y, c = lax.axis_index("x"), lax.axis_index("y"), lax.axis_index("c")
        for a in range(n):
            for k, peer in enumerate(_peer_list(x, y, c, near)):
                other = 4 * peer[0] + 2 * peer[1] + peer[2]
                j = a * n_peers + k
                copy = pltpu.make_async_remote_copy(
                    src_ref=src_refs[a] if gather else src_refs[a].at[other], dst_ref=land_refs[a].at[other],
                    send_sem=send_sems.at[j], recv_sem=recv_sems.at[j], device_id=peer, device_id_type=MESH)
                copy.wait_send()
                copy.wait_recv()

    out = pl.pallas_call(
        body, name=name,
        out_shape=tuple(pltpu.HBM(a.shape, a.dtype) for a in arrays),
        in_specs=(_HBM,) * (2 * n) + (_SEM, _SEM, pl.BlockSpec(memory_space=pl.ANY)), out_specs=(_HBM,) * (2 * n),
        input_output_aliases={i: i for i in range(2 * n)},
        compiler_params=pltpu.CompilerParams(has_side_effects=_EFFECT),
    )(*arrays, send_sems, recv_sems, after)
    mine = 4 * lax.axis_index("x") + 2 * lax.axis_index("y") + lax.axis_index("c")
    own = out[:n] if gather else [lax.dynamic_index_in_dim(s, mine, 0, keepdims=False) for s in out[:n]]
    return [lax.dynamic_update_slice(g, o[None], (mine, 0, 0)) for g, o in zip(out[n:], own)]


def _forward_start(lands, *, name):
    n = len(lands)

    def body(*refs):
        land_refs, send_sems, recv_sems, token = refs[:n], refs[n], refs[n + 1], refs[-1]
        x, y, c = lax.axis_index("x"), lax.axis_index("y"), lax.axis_index("c")
        for a in range(n):
            for j, (px, py) in enumerate(((1 - x, y), (x, 1 - y), (1 - x, 1 - y))):
                blk = 4 * px + 2 * py + c
                pltpu.make_async_remote_copy(
                    src_ref=land_refs[a].at[blk], dst_ref=land_refs[a].at[blk], send_sem=send_sems.at[3 * a + j],
                    recv_sem=recv_sems.at[3 * a + j], device_id=(x, y, 1 - c), device_id_type=MESH).start()
        token[...] = jnp.zeros_like(token)

    sems = pltpu.SemaphoreType.DMA((3 * n,))
    out = pl.pallas_call(
        body, name=name,
        out_shape=(sems, sems) + tuple(pltpu.HBM(a.shape, a.dtype) for a in lands) + (jax.ShapeDtypeStruct((8, 128), F32),),
        in_specs=(_HBM,) * n, out_specs=(_SEM, _SEM) + (_HBM,) * n + (pl.BlockSpec(memory_space=pltpu.VMEM),),
        input_output_aliases={i: 2 + i for i in range(n)},
        compiler_params=pltpu.CompilerParams(has_side_effects=_EFFECT),
    )(*[pltpu.with_memory_space_constraint(a, pltpu.HBM) for a in lands])
    return out[:-1], out[-1]


def _forward_wait(state, after, *, name):
    send_sems, recv_sems = state[0], state[1]
    lands = state[2:]
    n = len(lands)

    def body(*refs):
        land_refs, send_sems, recv_sems = refs[:n], refs[n], refs[n + 1]
        x, y, c = lax.axis_index("x"), lax.axis_index("y"), lax.axis_index("c")
        for a in range(n):
            for j, (px, py) in enumerate(((1 - x, y), (x, 1 - y), (1 - x, 1 - y))):
                copy = pltpu.make_async_remote_copy(
                    src_ref=land_refs[a].at[4 * px + 2 * py + c], dst_ref=land_refs[a].at[4 * px + 2 * py + 1 - c],
                    send_sem=send_sems.at[3 * a + j], recv_sem=recv_sems.at[3 * a + j], device_id=(x, y, 1 - c),
                    device_id_type=MESH)
                copy.wait_send()
                copy.wait_recv()

    return pl.pallas_call(
        body, name=name,
        out_shape=tuple(pltpu.HBM(a.shape, a.dtype) for a in lands),
        in_specs=(_HBM,) * n + (_SEM, _SEM, pl.BlockSpec(memory_space=pl.ANY)), out_specs=(_HBM,) * n,
        input_output_aliases={i: i for i in range(n)},
        compiler_params=pltpu.CompilerParams(has_side_effects=_EFFECT),
    )(*lands, send_sems, recv_sems, after)


def _adam_math(w, g, m, v):
    m = ADAM_B1 * m + (1.0 - ADAM_B1) * g
    v = ADAM_B2 * v + (1.0 - ADAM_B2) * (g * g)
    m_hat = m / (1.0 - ADAM_B1 ** ADAM_STEP)
    v_hat = v / (1.0 - ADAM_B2 ** ADAM_STEP)
    delta = -ADAM_LR * (m_hat / (jnp.sqrt(v_hat) + ADAM_EPS) + ADAM_WD * w)
    return delta, m, v


def _adamw(parts, w, m, v, *, name):
    R, C = w.shape
    n_parts = parts.shape[0]
    tr = R // 2
    assert tr % 16 == 0

    def body(p_ref, w_ref, m_ref, v_ref, g_ref, d_ref, nm_ref, nv_ref):
        g = p_ref[0].astype(F32)
        for s in range(1, n_parts):
            g = g + p_ref[s].astype(F32)
        d, nm, nv = _adam_math(w_ref[...], g, m_ref[...], v_ref[...])
        g_ref[...] = g
        d_ref[...] = d
        nm_ref[...] = nm
        nv_ref[...] = nv

    blk = pl.BlockSpec((tr, C), lambda i: (i, 0))
    return pl.pallas_call(
        body, name=name, grid=(R // tr,),
        in_specs=[pl.BlockSpec((n_parts, tr, C), lambda i: (0, i, 0)), blk, blk, blk],
        out_specs=[blk] * 4, out_shape=[jax.ShapeDtypeStruct((R, C), F32)] * 4,
        compiler_params=_params(("arbitrary",)),
    )(parts, w, m, v)


def _adamw_small(parts, w, m, v):
    def body(p_ref, w_ref, m_ref, v_ref, g_ref, d_ref, nm_ref, nv_ref):
        g = p_ref[0]
        for s in range(1, N_DEV):
            g = g + p_ref[s]
        d, nm, nv = _adam_math(w_ref[...], g, m_ref[...], v_ref[...])
        g_ref[...] = g
        d_ref[...] = d
        nm_ref[...] = nm
        nv_ref[...] = nv

    vm = pl.BlockSpec(memory_space=pltpu.VMEM)
    return pl.pallas_call(
        body, name="adamw_small", in_specs=[vm] * 4, out_specs=[vm] * 4,
        out_shape=[jax.ShapeDtypeStruct((SMALL_ROWS, 128), F32)] * 4, compiler_params=_params(),
    )(parts, w, m, v)


def _t5_bucket(dist):
    max_exact = REL_BUCKETS // 2
    df = jnp.maximum(dist, 1).astype(F32)
    large = max_exact + (jnp.log(df / max_exact) / math.log(REL_MAX_DISTANCE / max_exact)
                         * (REL_BUCKETS - max_exact)).astype(jnp.int32)
    large = jnp.minimum(large, REL_BUCKETS - 1)
    return jnp.where(dist < max_exact, dist, large)


def _band_tables(rel_table, dil, n_back):
    qi = jnp.arange(BLK)[:, None]
    kj = jnp.arange(2 * BLK)[None, :]
    delta = BLK + qi - kj
    in_band = (delta >= 0) & (delta <= n_back)
    if rel_table is None:
        vals = jnp.zeros((N_HEADS, BLK, 2 * BLK), F32)
        bmap = None
    else:
        bucket = _t5_bucket(jnp.clip(delta, 0, n_back) * dil)
        vals = jnp.zeros((N_HEADS, BLK, 2 * BLK), F32)
        for b in range(REL_BUCKETS):
            vals = jnp.where((bucket == b)[None], rel_table[b][:, None, None], vals)
        bmap = jnp.where(in_band, bucket, -1).astype(jnp.int32)
    later = jnp.where(in_band[None], vals, NEG)
    first = jnp.where((in_band & (kj >= BLK))[None], vals, NEG)
    return jnp.stack([later, first]), bmap


def _rope_tables(T):
    half = HEAD_DIM // 2
    inv_freq = ROPE_THETA ** (-jnp.arange(half, dtype=F32) / half)
    ang = jnp.arange(T, dtype=F32)[:, None] * inv_freq[None, :]
    cos, sin = jnp.cos(ang), jnp.sin(ang)
    return jnp.tile(cos, (1, 4)), jnp.tile(jnp.concatenate([-sin, sin], axis=1), (1, 2))


def _widen_in(a, axis):
    sl = lambda lo, hi: lax.slice_in_dim(a, lo, hi, axis=axis)
    dup = lambda lo: [sl(lo, lo + 64), sl(lo, lo + 64), sl(lo + 64, lo + 128), sl(lo + 64, lo + 128)]
    return jnp.concatenate([sl(0, 512)] + dup(512) + dup(640) + [sl(768, D_IN)], axis=axis)


def _fold_in(a, axis):
    sl = lambda lo, hi: lax.slice_in_dim(a, lo, hi, axis=axis)
    fold = lambda lo: [sl(lo, lo + 64) + sl(lo + 64, lo + 128), sl(lo + 128, lo + 192) + sl(lo + 192, lo + 256)]
    return jnp.concatenate([sl(0, 512)] + fold(512) + fold(768) + [sl(1024, D_INP)], axis=axis)


def _local_step(x, tgt, g_attn, b_in, sinks, rel_table, g_out_a, g_out_b, g_ffn, g_final,
                win_fn, wo_fn, ffn_fn, early_fn):
    T = x.shape[0]
    cos, sin = _rope_tables(T)
    g_final2 = g_final.reshape(1, D_MODEL)
    sink8 = sinks.reshape(N_HEADS)

    bias_a, _ = _band_tables(None, 1, BLK - 1)
    tabs = [_band_tables(rel_table, dil, window // dil) for window, dil in BRANCHES]
    wint, token = win_fn(tabs[2][0])
    winp = _widen_in(wint, 0)
    binp = _widen_in(b_in, 1) + token[0, 0]

    h1, qa, ka, va, *qkv_b = _norm_proj(x, g_attn, winp, binp, cos, sin)
    qbs, kbs, vbs = qkv_b[0:3], qkv_b[3:6], qkv_b[6:9]
    oa, lse_a = _attn_fwd(qa, ka, va, bias_a, sink8, dil=1, kv_pairs=2, use_sink=True, name="attn_a_fwd")
    outs = [_attn_fwd(qbs[n], kbs[n], vbs[n], tabs[n][0], sink8, dil=dil, kv_pairs=4, use_sink=False,
                      name=f"attn_b{n}_fwd") for n, (_, dil) in enumerate(BRANCHES)]
    wo = wo_fn(outs[2][1])
    x2, mixed, h2, *ob_lse = _merge_wo(x, oa, outs[0][0], outs[1][0], outs[2][0], outs[0][1], outs[1][1], outs[2][1],
                                       g_out_a, g_out_b, wo, g_ffn)
    obs, lses = ob_lse[0:3], ob_lse[3:6]
    wgt, wut, wd = ffn_fn(h2)
    gate, up, act = _ffn_up(h2, wgt, wut)
    dx3, dx3b, loss, dg_final = _ffn_down_loss(act, wd, x2, tgt, g_final2)

    dgate, dup, dx2, dx2b, dg_ffn = _ffn_bwd(dx3, gate, up, wd, wgt, wut, x2, g_ffn)
    dwd = _matmul_tn(act, dx3b, tk=1408, tn=1024, name="dw_down")
    dwgt = _matmul_tn(dgate, h2, tk=1408, tn=1024, name="dw_gate")
    dwut = _matmul_tn(dup, h2, tk=1408, tn=1024, name="dw_up")
    doa, *dobs, dg_out_a, dg_out_b, dwo = _wo_bwd(dx2b, wo, oa, obs[0], g_out_a, g_out_b, mixed)
    early, token2 = early_fn(dict(w_o=dwo, w_gate=dwgt, w_up=dwut, w_down=dwd))
    sink8b = sink8 + token2[0, 0]

    dqa, dka, dva, _, dsk = _attn_bwd(qa, ka, va, oa, doa, lse_a, bias_a, sink8b, dil=1, kv_pairs=2, use_sink=True,
                                      name="attn_a_bwd")
    res = [_attn_bwd(qbs[n], kbs[n], vbs[n], obs[n], dobs[n], lses[n], tabs[n][0], sink8b, dil=dil, kv_pairs=4,
                     use_sink=False, name=f"attn_b{n}_bwd") for n, (_, dil) in enumerate(BRANCHES)]
    dbp, grad_x, dg_attn, dwinp = _inproj_bwd(dqa, dka, dva, [r[0] for r in res], [r[1] for r in res],
                                              [r[2] for r in res], cos, sin, winp, x, dx2, g_attn, h1)
    dwin = _fold_in(dwinp, 0)
    drel, dsink = _bias_sink_grads([r[3] for r in res], [t[1] for t in tabs], dsk)

    small = dict(
        g_attn=dg_attn, b_in=_fold_in(dbp, 1), sinks=dsink[:, ::HEAD_DIM], rel_table=drel[:, :REL_BUCKETS].T,
        g_out_a=dg_out_a, g_out_b=dg_out_b, g_ffn=dg_ffn, g_final=dg_final.reshape(D_MODEL))
    return loss[0, 0], grad_x, dwin, early, small


SMALL_NAMES = ("g_attn", "b_in", "sinks", "rel_table", "g_out_a", "g_out_b", "g_ffn", "g_final", "loss")


def _pack_small(vals):
    flat = jnp.concatenate([vals[n].reshape(-1).astype(F32) for n in SMALL_NAMES])
    return jnp.pad(flat, (0, SMALL_ROWS * 128 - flat.shape[0])).reshape(SMALL_ROWS, 128)


def _unpack_small(packed, like):
    flat = packed.reshape(-1)
    out, off = {}, 0
    for n in SMALL_NAMES:
        size = like[n].size
        out[n] = flat[off:off + size].reshape(like[n].shape)
        off += size
    return out


def kernel(x, g_attn, w_in, b_in, sinks, rel_table, g_out_a, g_out_b, w_o, g_ffn, w_gate, w_up, w_down, g_final, loss_target, m_g_attn, m_w_in, m_b_in, m_sinks, m_rel_table, m_g_out_a, m_g_out_b, m_w_o, m_g_ffn, m_w_gate, m_w_up, m_w_down, m_g_final, v_g_attn, v_w_in, v_b_in, v_sinks, v_rel_table, v_g_out_a, v_g_out_b, v_w_o, v_g_ffn, v_w_gate, v_w_up, v_w_down, v_g_final):
    rest_names = ("w_o", "w_gate", "w_up", "w_down")

    rest = [w_o[0].astype(BF16), w_gate[0].astype(BF16).T, w_up[0].astype(BF16).T, w_down[0].astype(BF16)]
    in_state, _ = _exchange_start([w_in[0].astype(BF16).T], gather=True, near=True, name="gather_w_in_start")
    later = {}

    def whole(got):
        return [g.reshape(N_DEV * g.shape[1], D_MODEL) for g in got]

    def win_fn(after):
        near = _exchange_wait(in_state, after, gather=True, near=True, name="gather_w_in_near")
        fwd_state, tok = _forward_start(near, name="gather_w_in_forward")
        wint = whole(_forward_wait(fwd_state, tok, name="gather_w_in_wait"))[0]
        wint, src = lax.optimization_barrier((wint, rest))
        later["wo"], token_o = _exchange_start(src[:1], gather=True, name="gather_w_o_start")
        token_o, ffn_src = lax.optimization_barrier((token_o, src[1:]))
        later["ffn"], token = _exchange_start(ffn_src, gather=True, name="gather_ffn_start")
        return wint, token + token_o

    def wo_fn(after):
        return whole(_exchange_wait(later["wo"], after, gather=True, name="gather_w_o_wait"))[0]

    def ffn_fn(after):
        return whole(_exchange_wait(later["ffn"], after, gather=True, name="gather_ffn_wait"))

    def early_fn(dws):
        return _exchange_start([dws[n].reshape(N_DEV, -1, D_MODEL) for n in rest_names], gather=False,
                               name="scatter_rest_start")

    loss_part, grad_x, dwint, early_state, small = _local_step(
        x[0], loss_target[0], g_attn, b_in, sinks, rel_table, g_out_a, g_out_b, g_ffn, g_final,
        win_fn, wo_fn, ffn_fn, early_fn)
    parts_in = dwint.astype(BF16).reshape(N_DEV, D_IN // N_DEV, D_MODEL)
    in_state, token3 = _exchange_start([parts_in], gather=False, name="scatter_w_in_start")
    got = _exchange_wait(early_state, token3, gather=False, name="scatter_rest_wait")

    def update(n, parts, w, m, v, transposed):
        if transposed:
            return [a.T[None] for a in _adamw(parts, w[0].T, m[0].T, v[0].T, name="adamw_" + n)]
        return [a[None] for a in _adamw(parts, w[0], m[0], v[0], name="adamw_" + n)]

    big = dict(w_o=update("w_o", got[0], w_o, m_w_o, v_w_o, False),
               w_gate=update("w_gate", got[1], w_gate, m_w_gate, v_w_gate, True),
               w_up=update("w_up", got[2], w_up, m_w_up, v_w_up, True),
               w_down=update("w_down", got[3], w_down, m_w_down, v_w_down, False))

    unused = jnp.zeros((1,), F32)
    ws = dict(g_attn=g_attn, b_in=b_in, sinks=sinks, rel_table=rel_table, g_out_a=g_out_a, g_out_b=g_out_b,
              g_ffn=g_ffn, g_final=g_final, loss=unused)
    ms = dict(g_attn=m_g_attn, b_in=m_b_in, sinks=m_sinks, rel_table=m_rel_table, g_out_a=m_g_out_a,
              g_out_b=m_g_out_b, g_ffn=m_g_ffn, g_final=m_g_final, loss=unused)
    vs = dict(g_attn=v_g_attn, b_in=v_b_in, sinks=v_sinks, rel_table=v_rel_table, g_out_a=v_g_out_a,
              g_out_b=v_g_out_b, g_ffn=v_g_ffn, g_final=v_g_final, loss=unused)
    sparts = _all_gather(_pack_small(dict(small, loss=loss_part)), name="gather_small")
    sm_packed = _adamw_small(sparts, _pack_small(ws), _pack_small(ms), _pack_small(vs))
    sm = [_unpack_small(a, ws) for a in sm_packed]
    loss = sm[0]["loss"][0]

    done = sm_packed[1][:1, :1] + sum(big[n][1][0, :1, :1] for n in rest_names)
    got_in = _exchange_wait(in_state, done, gather=False, name="scatter_w_in_wait")[0]
    big["w_in"] = update("w_in", got_in, w_in, m_w_in, v_w_in, True)

    order = ("g_attn", "w_in", "b_in", "sinks", "rel_table", "g_out_a", "g_out_b", "w_o", "g_ffn", "w_gate", "w_up",
             "w_down", "g_final")
    outs = [loss, grad_x[None]]
    for k in range(4):
        outs += [big[n][k] if n in big else sm[k][n] for n in order]
    return tuple(outs)
```

```python
import functools
import math

import jax
import jax.numpy as jnp
from jax import lax
from jax.experimental import pallas as pl
from jax.experimental.pallas import tpu as pltpu

F32 = jnp.float32
BF16 = jnp.bfloat16

N_DEV = 8
D_MODEL = 1024
HEAD_DIM = 64
N_HEADS = 8
PAIR = 2 * HEAD_DIM
WIDTH = N_HEADS * HEAD_DIM
D_IN = 2304
D_INP = 2560
D_FF = 2816
BLK = 128
ROPE_THETA = 150000.0
REL_BUCKETS = 32
REL_MAX_DISTANCE = 2048
EPS = 1e-5
NEG = -1e30
BRANCHES = ((128, 1), (512, 4), (2048, 16))
Q_SCALE = HEAD_DIM ** -0.5

ADAM_LR = 0.001
ADAM_B1 = 0.9
ADAM_B2 = 0.999
ADAM_EPS = 1e-08
ADAM_WD = 0.01
ADAM_STEP = 10

VMEM_LIMIT = 56 * 1024 * 1024
MESH = pl.DeviceIdType.MESH

NT = (((1,), (1,)), ((), ()))
TN = (((0,), (0,)), ((), ()))

SMALL_ROWS = 56


def _params(sem=None):
    return pltpu.CompilerParams(dimension_semantics=sem, vmem_limit_bytes=VMEM_LIMIT)


def _sigmoid(x):
    return 1.0 / (1.0 + jnp.exp2(x * (-1.0 / math.log(2.0))))


def _rms_bwd(dh, xh, r, g):
    u = dh * g
    return r * (u - xh * jnp.mean(u * xh, axis=-1, keepdims=True))


def _rope_rot(t, first):
    return jnp.where(first, pltpu.roll(t, 96, 1), pltpu.roll(t, 32, 1))


N_CHUNK = WIDTH // PAIR


def _scr(tm):
    return pltpu.VMEM((N_CHUNK, tm, PAIR), F32)


def _scr_get(scr):
    return jnp.concatenate([scr[j] for j in range(N_CHUNK)], axis=1)


def _scr_put(scr, val):
    for j in range(N_CHUNK):
        scr[j] = val[:, j * PAIR:(j + 1) * PAIR]


def _unstride(view_ref, scr, dil, tm):
    n = tm // dil
    chunks = scr.shape[0]
    for r in range(dil):
        for j in range(chunks):
            col = (r * chunks + j) * PAIR
            scr.at[j][pl.ds(r, n, stride=dil), :] = view_ref[:, col:col + PAIR].astype(F32)


def _restride(scr, out_ref, dil, tm):
    n = tm // dil
    chunks = scr.shape[0]
    for r in range(dil):
        for j in range(chunks):
            col = (r * chunks + j) * PAIR
            rows = scr[j] if dil == 1 else scr.at[j][pl.ds(r, n, stride=dil), :]
            out_ref[:, col:col + PAIR] = rows.astype(out_ref.dtype)


def _view_specs(tm, width=WIDTH):
    return [pl.BlockSpec((tm // dil, dil * width), lambda i: (i, 0)) for _, dil in BRANCHES]


def _view_shapes(T, dtype, width=WIDTH):
    return [jax.ShapeDtypeStruct((T // dil, dil * width), dtype) for _, dil in BRANCHES]


def _norm_proj(x, g, w, b, cos, sin, *, tm=512):
    T = x.shape[0]

    def body(x_ref, g_ref, w_ref, b_ref, cos_ref, sin_ref, h_ref, qa_ref, ka_ref, va_ref, *rest):
        outs_b, ys = rest[:9], rest[9]
        xv = x_ref[...]
        r = lax.rsqrt(jnp.mean(xv * xv, axis=-1, keepdims=True) + EPS)
        h = (xv * r * g_ref[...]).astype(BF16)
        h_ref[...] = h
        cosv = cos_ref[...]
        sinv = sin_ref[...]
        lane = lax.broadcasted_iota(jnp.int32, (tm, PAIR), 1)
        first = (lane % HEAD_DIM) < (HEAD_DIM // 2)

        def proj(off):
            return (lax.dot_general(h, w_ref[off:off + 256, :], NT, preferred_element_type=F32)
                    + b_ref[:, off:off + 256])

        for (off, width, rot, scale), o_ref in zip(((0, 512, True, Q_SCALE), (512, 256, True, 1.0), (768, 256, False, 1.0)),
                                                   (qa_ref, ka_ref, va_ref)):
            for c in range(0, width, 256):
                y = proj(off + c)
                for j in range(0, 256, PAIR):
                    t = y[:, j:j + PAIR]
                    if rot:
                        t = t * cosv + _rope_rot(t, first) * sinv
                    if scale != 1.0:
                        t = t * scale
                    o_ref[:, c + j:c + j + PAIR] = t.astype(BF16)
        for n, (off, scale) in enumerate(((1024, Q_SCALE), (1536, 1.0), (2048, 1.0))):
            for c in range(0, WIDTH, 256):
                y = proj(off + c)
                y = y * scale if scale != 1.0 else y
                for j in range(0, 256, PAIR):
                    ys[(c + j) // PAIR] = y[:, j:j + PAIR]
            for (_, dil), o_ref in zip(BRANCHES, outs_b[3 * n:3 * n + 3]):
                _restride(ys, o_ref, dil, tm)

    row = lambda w_: pl.BlockSpec((tm, w_), lambda i: (i, 0))
    full = lambda a: pl.BlockSpec(a.shape, lambda i: (0, 0))
    return pl.pallas_call(
        body, name="norm_proj", grid=(T // tm,),
        in_specs=[row(D_MODEL), full(g), full(w), full(b), row(PAIR), row(PAIR)],
        out_specs=[row(D_MODEL), row(512), row(256), row(256)] + _view_specs(tm) * 3,
        out_shape=[jax.ShapeDtypeStruct((T, n), BF16) for n in (D_MODEL, 512, 256, 256)] + _view_shapes(T, BF16) * 3,
        scratch_shapes=[_scr(tm)],
        compiler_params=_params(("arbitrary",)),
    )(x, g, w, b, cos, sin)


MAX_SUB = 8
AHEAD = 2


def _attn_specs(kvw, sub):
    q_spec = pl.BlockSpec((sub * BLK, WIDTH), lambda r, i: (i, r))
    kc_spec = pl.BlockSpec((sub * BLK, kvw), lambda r, i: (i, r))
    kp_spec = pl.BlockSpec((BLK, kvw), lambda r, i: (jnp.maximum(sub * i - 1, 0), r))
    b_spec = pl.BlockSpec((2, N_HEADS, BLK, 2 * BLK), lambda r, i: (0, 0, 0, 0))
    return q_spec, kp_spec, kc_spec, b_spec


def _window(prev_ref, cur_ref, j, ksl):
    before = prev_ref[:, ksl] if j == 0 else cur_ref[(j - 1) * BLK:j * BLK, ksl]
    return jnp.concatenate([before, cur_ref[j * BLK:(j + 1) * BLK, ksl]], axis=0)


def _attn_fwd(q, k, v, bias, sinks, *, dil, kv_pairs, use_sink, name):
    L = q.shape[0]
    sub = min(MAX_SUB, L // BLK)
    ns = L // (sub * BLK)
    kvw = kv_pairs * PAIR
    rep = 4 // kv_pairs

    def body(sink_ref, q_ref, kp_ref, kc_ref, vp_ref, vc_ref, b_ref, o_ref, lse_ref):
        lane = lax.broadcasted_iota(jnp.int32, (1, PAIR), 1)
        lo = lane < HEAD_DIM
        first = jnp.where(pl.program_id(1) == 0, 1, 0)
        def scores(j, hp):
            rows = slice(j * BLK, (j + 1) * BLK)
            sl = slice(hp * PAIR, (hp + 1) * PAIR)
            ksl = slice((hp // rep) * PAIR, (hp // rep + 1) * PAIR)
            qp = q_ref[rows, sl]
            kk = _window(kp_ref, kc_ref, j, ksl)
            vv = _window(vp_ref, vc_ref, j, ksl)
            heads = []
            for e in range(2):
                h = 2 * hp + e
                msk = lo if e == 0 else jnp.logical_not(lo)
                qm = jnp.where(msk, qp, jnp.zeros_like(qp))
                s = lax.dot_general(qm, kk, NT, preferred_element_type=F32) + (b_ref[first, h] if j == 0 else b_ref[0, h])
                heads.append((h, msk, s))
            return rows, sl, vv, heads

        def outputs(rows, sl, vv, heads):
            o_pair = None
            lse_pair = None
            for h, msk, s in heads:
                m = jnp.max(s, axis=-1, keepdims=True)
                if use_sink:
                    sk = sink_ref[h]
                    m = jnp.maximum(m, sk)
                p = jnp.exp(s - m)
                l = jnp.sum(p, axis=-1, keepdims=True)
                if use_sink:
                    l = l + jnp.exp(sk - m)
                vm = jnp.where(msk, vv, jnp.zeros_like(vv))
                oe = jnp.dot(p.astype(BF16), vm, preferred_element_type=F32) * (1.0 / l)
                ls = m + jnp.log(l)
                if o_pair is None:
                    o_pair = oe
                    lse_pair = jnp.broadcast_to(ls, (BLK, PAIR))
                else:
                    o_pair = o_pair + oe
                    lse_pair = jnp.where(lo, lse_pair, ls)
            o_ref[rows, sl] = o_pair.astype(BF16)
            lse_ref[rows, sl] = lse_pair

        items = [(j, hp) for j in range(sub) for hp in range(4)]
        queue = [scores(*it) for it in items[:AHEAD]]
        for n in range(len(items)):
            if n + AHEAD < len(items):
                queue.append(scores(*items[n + AHEAD]))
            outputs(*queue.pop(0))

    q_spec, kp_spec, kc_spec, b_spec = _attn_specs(kvw, sub)
    return pl.pallas_call(
        body, name=name, grid=(dil, ns),
        in_specs=[pl.BlockSpec(memory_space=pltpu.SMEM), q_spec, kp_spec, kc_spec, kp_spec, kc_spec, b_spec],
        out_specs=[q_spec, q_spec],
        out_shape=[jax.ShapeDtypeStruct((L, dil * WIDTH), BF16), jax.ShapeDtypeStruct((L, dil * WIDTH), F32)],
        compiler_params=_params(("arbitrary", "arbitrary")),
    )(sinks, q, k, k, v, v, bias)


def _attn_bwd(q, k, v, o, do, lse, bias, sinks, *, dil, kv_pairs, use_sink, name, max_sub=MAX_SUB):
    L = q.shape[0]
    sub = min(max_sub, L // BLK)
    ns = L // (sub * BLK)
    n_steps = dil * ns
    kvw = kv_pairs * PAIR
    rep = 4 // kv_pairs
    last = slice((sub - 1) * BLK, sub * BLK)

    def body(sink_ref, q_ref, kp_ref, kc_ref, vp_ref, vc_ref, o_ref, do_ref, lse_ref, b_ref,
             dq_ref, dk_ref, dv_ref, dsum_ref, dsk_ref, pk_ref, pv_ref):
        t = pl.program_id(0)
        i = t % ns

        @pl.when(t == 0)
        def _():
            dsum_ref[...] = jnp.zeros_like(dsum_ref)
            dsk_ref[...] = jnp.zeros_like(dsk_ref)
            pk_ref[...] = jnp.zeros_like(pk_ref)
            pv_ref[...] = jnp.zeros_like(pv_ref)

        @pl.when(t < n_steps)
        def _():
            lo = lax.broadcasted_iota(jnp.int32, (1, PAIR), 1) < HEAD_DIM
            first = jnp.where(i == 0, 1, 0)
            dks = [[None] * kv_pairs for _ in range(sub)]
            dvs = [[None] * kv_pairs for _ in range(sub)]
            def scores(j, hp):
                rows = slice(j * BLK, (j + 1) * BLK)
                kvp = hp // rep
                sl = slice(hp * PAIR, (hp + 1) * PAIR)
                ksl = slice(kvp * PAIR, (kvp + 1) * PAIR)
                qp = q_ref[rows, sl]
                dop = do_ref[rows, sl]
                prod = dop.astype(F32) * o_ref[rows, sl].astype(F32)
                kk = _window(kp_ref, kc_ref, j, ksl)
                vv = _window(vp_ref, vc_ref, j, ksl)
                heads = []
                for e in range(2):
                    h = 2 * hp + e
                    msk = lo if e == 0 else jnp.logical_not(lo)
                    qm = jnp.where(msk, qp, jnp.zeros_like(qp))
                    dom = jnp.where(msk, dop, jnp.zeros_like(dop))
                    km = jnp.where(msk, kk, jnp.zeros_like(kk))
                    s = (lax.dot_general(qm, kk, NT, preferred_element_type=F32)
                         + (b_ref[first, h] if j == 0 else b_ref[0, h]))
                    dp = lax.dot_general(dom, vv, NT, preferred_element_type=F32)
                    heads.append((h, msk, qm, dom, km, s, dp))
                return j, rows, kvp, sl, prod, heads

            def grads(j, rows, kvp, sl, prod, heads):
                dq_pair = None
                c_pair = None
                qms, doms, dsbs, pbs = [], [], [], []
                for h, msk, qm, dom, km, s, dp in heads:
                    ls = lse_ref[rows, h * HEAD_DIM:h * HEAD_DIM + 1]
                    p = jnp.exp(s - ls)
                    delta = jnp.sum(jnp.where(msk, prod, 0.0), axis=-1, keepdims=True)
                    ds = p * (dp - delta)
                    if use_sink:
                        ce = jnp.exp(sink_ref[h] - ls) * delta
                        c_pair = jnp.broadcast_to(ce, (BLK, PAIR)) if c_pair is None else jnp.where(msk, ce, c_pair)
                    else:
                        dsum_ref[h] += ds
                    dsb = ds.astype(BF16)
                    dqe = jnp.dot(dsb, km, preferred_element_type=F32)
                    dq_pair = dqe if dq_pair is None else dq_pair + dqe
                    qms.append(qm)
                    doms.append(dom)
                    dsbs.append(dsb)
                    pbs.append(p.astype(BF16))
                dke = lax.dot_general(jnp.concatenate(dsbs, axis=0), jnp.concatenate(qms, axis=0), TN,
                                      preferred_element_type=F32)
                dve = lax.dot_general(jnp.concatenate(pbs, axis=0), jnp.concatenate(doms, axis=0), TN,
                                      preferred_element_type=F32)
                dks[j][kvp] = dke if dks[j][kvp] is None else dks[j][kvp] + dke
                dvs[j][kvp] = dve if dvs[j][kvp] is None else dvs[j][kvp] + dve
                dq_ref[rows, sl] = (dq_pair * Q_SCALE).astype(BF16)
                if use_sink:
                    dsk_ref[:, sl] += c_pair

            items = [(j, hp) for j in range(sub) for hp in range(4)]
            ahead = AHEAD + 1 if use_sink else AHEAD
            queue = [scores(*it) for it in items[:ahead]]
            for n in range(len(items)):
                if n + ahead < len(items):
                    queue.append(scores(*items[n + ahead]))
                grads(*queue.pop(0))
            for kvp in range(kv_pairs):
                ksl = slice(kvp * PAIR, (kvp + 1) * PAIR)
                for pend_ref, out_ref, parts in ((pk_ref, dk_ref, [d[kvp] for d in dks]),
                                                 (pv_ref, dv_ref, [d[kvp] for d in dvs])):
                    if sub > 1:
                        out_ref[:(sub - 1) * BLK, ksl] = pend_ref[:(sub - 1) * BLK, ksl].astype(BF16)
                    out_ref[last, ksl] = (pend_ref[last, ksl] + parts[0][:BLK]).astype(BF16)
                    for j in range(sub):
                        own = parts[j][BLK:]
                        pend_ref[j * BLK:(j + 1) * BLK, ksl] = own + parts[j + 1][:BLK] if j + 1 < sub else own

        @pl.when(t == n_steps)
        def _():
            dk_ref[...] = pk_ref[...].astype(BF16)
            dv_ref[...] = pv_ref[...].astype(BF16)

    def at(t):
        t = jnp.minimum(t, n_steps - 1)
        return t % ns, t // ns

    def before(t):
        return at(jnp.maximum(t - 1, 0))

    q_spec = pl.BlockSpec((sub * BLK, WIDTH), at)
    kc_spec = pl.BlockSpec((sub * BLK, kvw), at)
    kp_spec = pl.BlockSpec((BLK, kvw), lambda t: (jnp.maximum(sub * at(t)[0] - 1, 0), at(t)[1]))
    b_spec = pl.BlockSpec((2, N_HEADS, BLK, 2 * BLK), lambda t: (0, 0, 0, 0))
    dkv_spec = pl.BlockSpec((sub * BLK, kvw), before)
    return pl.pallas_call(
        body, name=name, grid=(n_steps + 1,),
        in_specs=[pl.BlockSpec(memory_space=pltpu.SMEM), q_spec, kp_spec, kc_spec, kp_spec, kc_spec,
                  q_spec, q_spec, q_spec, b_spec],
        out_specs=[q_spec, dkv_spec, dkv_spec,
                   pl.BlockSpec((N_HEADS, BLK, 2 * BLK), lambda t: (0, 0, 0)),
                   pl.BlockSpec((BLK, WIDTH), lambda t: (0, 0))],
        out_shape=[jax.ShapeDtypeStruct((L, dil * WIDTH), BF16),
                   jax.ShapeDtypeStruct((L, dil * kvw), BF16),
                   jax.ShapeDtypeStruct((L, dil * kvw), BF16),
                   jax.ShapeDtypeStruct((N_HEADS, BLK, 2 * BLK), F32),
                   jax.ShapeDtypeStruct((BLK, WIDTH), F32)],
        scratch_shapes=[pltpu.VMEM((sub * BLK, kvw), F32), pltpu.VMEM((sub * BLK, kvw), F32)],
        compiler_params=_params(("arbitrary",)),
    )(sinks, q, k, k, v, v, o, do, lse, bias)


def _merge_wo(x, oa, o1, o2, o3, l1, l2, l3, ga, gb, wo, gf, *, tm=512):
    T = x.shape[0]

    def body(x_ref, oa_ref, o1_ref, o2_ref, o3_ref, l1_ref, l2_ref, l3_ref, ga_ref, gb_ref, wo_ref, gf_ref,
             x2_ref, mix_ref, h2_ref, ob1_ref, ob4_ref, ob16_ref, ls1_ref, ls4_ref, ls16_ref, so2, so3, sl2, sl3):
        _unstride(o2_ref, so2, BRANCHES[1][1], tm)
        _unstride(o3_ref, so3, BRANCHES[2][1], tm)
        _unstride(l2_ref, sl2, BRANCHES[1][1], tm)
        _unstride(l3_ref, sl3, BRANCHES[2][1], tm)
        la, lb, lc = l1_ref[...], _scr_get(sl2), _scr_get(sl3)
        m = jnp.maximum(jnp.maximum(la, lb), lc)
        ea, eb, ec = jnp.exp(la - m), jnp.exp(lb - m), jnp.exp(lc - m)
        den = ea + eb + ec
        inv = 1.0 / den
        ob = (ea * o1_ref[...].astype(F32) + eb * _scr_get(so2) + ec * _scr_get(so3)) * inv
        _scr_put(so2, ob)
        _scr_put(sl2, m + jnp.log(den))
        for (_, dil), o_ref, l_ref in zip(BRANCHES, (ob1_ref, ob4_ref, ob16_ref), (ls1_ref, ls4_ref, ls16_ref)):
            _restride(so2, o_ref, dil, tm)
            _restride(sl2, l_ref, dil, tm)
        oav = oa_ref[...].astype(F32)
        ra = lax.rsqrt(jnp.mean(oav * oav, axis=-1, keepdims=True) + EPS)
        rb = lax.rsqrt(jnp.mean(ob * ob, axis=-1, keepdims=True) + EPS)
        mix_ref[:, :WIDTH] = (oav * ra * ga_ref[...]).astype(BF16)
        mix_ref[:, WIDTH:] = (ob * rb * gb_ref[...]).astype(BF16)
        x2 = x_ref[...] + jnp.dot(mix_ref[...], wo_ref[...], preferred_element_type=F32)
        x2_ref[...] = x2
        r2 = lax.rsqrt(jnp.mean(x2 * x2, axis=-1, keepdims=True) + EPS)
        h2_ref[...] = (x2 * r2 * gf_ref[...]).astype(BF16)

    row = lambda w_: pl.BlockSpec((tm, w_), lambda i: (i, 0))
    full = lambda a: pl.BlockSpec(a.shape, lambda i: (0, 0))
    return pl.pallas_call(
        body, name="merge_wo", grid=(T // tm,),
        in_specs=[row(D_MODEL), row(WIDTH)] + _view_specs(tm) * 2 + [full(ga), full(gb), full(wo), full(gf)],
        out_specs=[row(D_MODEL), row(D_MODEL), row(D_MODEL)] + _view_specs(tm) * 2,
        out_shape=[jax.ShapeDtypeStruct((T, D_MODEL), F32), jax.ShapeDtypeStruct((T, D_MODEL), BF16),
                   jax.ShapeDtypeStruct((T, D_MODEL), BF16)] + _view_shapes(T, BF16) + _view_shapes(T, F32),
        scratch_shapes=[_scr(tm)] * 4,
        compiler_params=_params(("arbitrary",)),
    )(x, oa, o1, o2, o3, l1, l2, l3, ga, gb, wo, gf)


def _ffn_up(h2, wgt, wut, *, tm=512, fc=D_FF, rc=512, cc=256):
    T = h2.shape[0]

    def body(h_ref, wg_ref, wu_ref, gate_ref, up_ref, act_ref):
        for s in range(0, tm, rc):
            h = h_ref[s:s + rc, :]
            for c in range(0, fc, cc):
                gt = lax.dot_general(h, wg_ref[c:c + cc, :], NT, preferred_element_type=F32)
                u = lax.dot_general(h, wu_ref[c:c + cc, :], NT, preferred_element_type=F32)
                gate_ref[s:s + rc, c:c + cc] = gt.astype(BF16)
                up_ref[s:s + rc, c:c + cc] = u.astype(BF16)
                act_ref[s:s + rc, c:c + cc] = (gt * _sigmoid(gt) * u).astype(BF16)

    rowd = pl.BlockSpec((tm, D_MODEL), lambda i, c: (i, 0))
    wrow = pl.BlockSpec((fc, D_MODEL), lambda i, c: (c, 0))
    oc = pl.BlockSpec((tm, fc), lambda i, c: (i, c))
    return pl.pallas_call(
        body, name="ffn_up", grid=(T // tm, D_FF // fc),
        in_specs=[rowd, wrow, wrow],
        out_specs=[oc, oc, oc],
        out_shape=[jax.ShapeDtypeStruct((T, D_FF), BF16)] * 3,
        compiler_params=_params(("arbitrary", "arbitrary")),
    )(h2, wgt, wut)


def _ffn_down_loss(act, wd, x2, tgt, g, *, tm=512, rc=256):
    T = x2.shape[0]

    def body(act_ref, wd_ref, x2_ref, tgt_ref, g_ref, dx_ref, dxb_ref, loss_ref, dg_ref):
        @pl.when(pl.program_id(0) == 0)
        def _():
            loss_ref[...] = jnp.zeros_like(loss_ref)
            dg_ref[...] = jnp.zeros_like(dg_ref)

        gv = g_ref[...]
        lsum = jnp.zeros((1, 1), F32)
        dgs = jnp.zeros((1, D_MODEL), F32)
        for c in range(0, tm, rc):
            x3 = x2_ref[c:c + rc, :] + jnp.dot(act_ref[c:c + rc, :], wd_ref[...], preferred_element_type=F32)
            r = lax.rsqrt(jnp.mean(x3 * x3, axis=-1, keepdims=True) + EPS)
            xh = x3 * r
            diff = xh * gv - tgt_ref[c:c + rc, :]
            lsum = lsum + jnp.sum(jnp.sum(diff * diff, axis=-1, keepdims=True), axis=0, keepdims=True)
            dy = diff * (1.0 / D_MODEL)
            dgs = dgs + jnp.sum(dy * xh, axis=0, keepdims=True)
            dx = _rms_bwd(dy, xh, r, gv)
            dx_ref[c:c + rc, :] = dx
            dxb_ref[c:c + rc, :] = dx.astype(BF16)
        loss_ref[...] += lsum * (0.5 / D_MODEL)
        dg_ref[...] += dgs

    rowd = pl.BlockSpec((tm, D_MODEL), lambda i: (i, 0))
    return pl.pallas_call(
        body, name="ffn_down_loss", grid=(T // tm,),
        in_specs=[pl.BlockSpec((tm, D_FF), lambda i: (i, 0)), pl.BlockSpec((D_FF, D_MODEL), lambda i: (0, 0)),
                  rowd, rowd, pl.BlockSpec(g.shape, lambda i: (0, 0))],
        out_specs=[rowd, rowd, pl.BlockSpec((1, 1), lambda i: (0, 0)), pl.BlockSpec((1, D_MODEL), lambda i: (0, 0))],
        out_shape=[jax.ShapeDtypeStruct((T, D_MODEL), F32), jax.ShapeDtypeStruct((T, D_MODEL), BF16),
                   jax.ShapeDtypeStruct((1, 1), F32), jax.ShapeDtypeStruct((1, D_MODEL), F32)],
        compiler_params=_params(("arbitrary",)),
    )(act, wd, x2, tgt, g)


def _ffn_bwd(dx3, gate, up, wd, wgt, wut, x2, g, *, tm=256, cc=256):
    T = x2.shape[0]

    def body(dx_ref, gate_ref, up_ref, wd_ref, wg_ref, wu_ref, x2_ref, g_ref,
             dgate_ref, dup_ref, dx2_ref, dx2b_ref, dg_ref):
        @pl.when(pl.program_id(0) == 0)
        def _():
            dg_ref[...] = jnp.zeros_like(dg_ref)

        dxb = dx_ref[...].astype(BF16)
        for c in range(0, D_FF, cc):
            dact = lax.dot_general(dxb, wd_ref[c:c + cc, :], NT, preferred_element_type=F32)
            gt = gate_ref[:, c:c + cc].astype(F32)
            u = up_ref[:, c:c + cc].astype(F32)
            sg = _sigmoid(gt)
            a = dact * sg
            dgate_ref[:, c:c + cc] = (a * u * ((1.0 + gt) - gt * sg)).astype(BF16)
            dup_ref[:, c:c + cc] = (a * gt).astype(BF16)
        dh = (jnp.dot(dgate_ref[...], wg_ref[...], preferred_element_type=F32)
              + jnp.dot(dup_ref[...], wu_ref[...], preferred_element_type=F32))
        xv = x2_ref[...]
        r = lax.rsqrt(jnp.mean(xv * xv, axis=-1, keepdims=True) + EPS)
        xh = xv * r
        dg_ref[...] += jnp.sum(dh * xh, axis=0, keepdims=True)
        d = dx_ref[...] + _rms_bwd(dh, xh, r, g_ref[...])
        dx2_ref[...] = d
        dx2b_ref[...] = d.astype(BF16)

    rowd = pl.BlockSpec((tm, D_MODEL), lambda i: (i, 0))
    rowf = pl.BlockSpec((tm, D_FF), lambda i: (i, 0))
    wfull = pl.BlockSpec((D_FF, D_MODEL), lambda i: (0, 0), pipeline_mode=pl.Buffered(1))
    return pl.pallas_call(
        body, name="ffn_bwd", grid=(T // tm,),
        in_specs=[rowd, rowf, rowf, wfull, wfull, wfull, rowd, pl.BlockSpec(g.shape, lambda i: (0, 0))],
        out_specs=[rowf, rowf, rowd, rowd, pl.BlockSpec((1, D_MODEL), lambda i: (0, 0))],
        out_shape=[jax.ShapeDtypeStruct((T, D_FF), BF16), jax.ShapeDtypeStruct((T, D_FF), BF16),
                   jax.ShapeDtypeStruct((T, D_MODEL), F32), jax.ShapeDtypeStruct((T, D_MODEL), BF16),
                   jax.ShapeDtypeStruct((1, D_MODEL), F32)],
        compiler_params=_params(("arbitrary",)),
    )(dx3, gate, up, wd, wgt, wut, x2, g)


def _matmul_tn(a, b, *, tk, tn, tt=2048, out_dtype=BF16, name):
    T, K = a.shape
    N = b.shape[1]
    nt = T // tt

    def body(a_ref, b_ref, o_ref, acc_ref):
        part = lax.dot_general(a_ref[...], b_ref[...], TN, preferred_element_type=F32)

        @pl.when(pl.program_id(2) == 0)
        def _():
            acc_ref[...] = part

        @pl.when(pl.program_id(2) > 0)
        def _():
            acc_ref[...] += part

        @pl.when(pl.program_id(2) == nt - 1)
        def _():
            o_ref[...] = acc_ref[...].astype(out_dtype)

    return pl.pallas_call(
        body, name=name, grid=(K // tk, N // tn, nt),
        in_specs=[pl.BlockSpec((tt, tk), lambda i, j, t: (t, i)), pl.BlockSpec((tt, tn), lambda i, j, t: (t, j))],
        out_specs=pl.BlockSpec((tk, tn), lambda i, j, t: (i, j)),
        out_shape=jax.ShapeDtypeStruct((K, N), out_dtype),
        scratch_shapes=[pltpu.VMEM((tk, tn), F32)],
        compiler_params=_params(("arbitrary", "arbitrary", "arbitrary")),
    )(a, b)


def _wo_bwd(dx2b, wo, oa, ob, ga, gb, mixed, *, tm=512, wc=256):
    T = dx2b.shape[0]
    n_tiles = T // tm

    def body(dx_ref, wo_ref, oa_ref, ob_ref, ga_ref, gb_ref, mix_ref,
             doa_ref, dob1_ref, dob4_ref, dob16_ref, dga_ref, dgb_ref, dwo_ref, scr, dw_acc):
        @pl.when(pl.program_id(0) == 0)
        def _():
            dga_ref[...] = jnp.zeros_like(dga_ref)
            dgb_ref[...] = jnp.zeros_like(dgb_ref)
            dw_acc[...] = jnp.zeros_like(dw_acc)

        dxv = dx_ref[...]
        for c in range(0, D_MODEL, wc):
            dw_acc[c:c + wc, :] += lax.dot_general(mix_ref[:, c:c + wc], dxv, TN, preferred_element_type=F32)

        @pl.when(pl.program_id(0) == n_tiles - 1)
        def _():
            dwo_ref[...] = dw_acc[...].astype(BF16)

        dm = lax.dot_general(dxv, wo_ref[...], NT, preferred_element_type=F32)
        for o_ref, g_ref, dg_ref, sl in ((oa_ref, ga_ref, dga_ref, slice(0, WIDTH)),
                                         (ob_ref, gb_ref, dgb_ref, slice(WIDTH, 2 * WIDTH))):
            ov = o_ref[...].astype(F32)
            r = lax.rsqrt(jnp.mean(ov * ov, axis=-1, keepdims=True) + EPS)
            xh = ov * r
            d = dm[:, sl]
            dg_ref[...] += jnp.sum(d * xh, axis=0, keepdims=True)
            do = _rms_bwd(d, xh, r, g_ref[...])
            if o_ref is oa_ref:
                doa_ref[...] = do.astype(BF16)
            else:
                _scr_put(scr, do)
                for (_, dil), v_ref in zip(BRANCHES, (dob1_ref, dob4_ref, dob16_ref)):
                    _restride(scr, v_ref, dil, tm)

    row = lambda w_: pl.BlockSpec((tm, w_), lambda i: (i, 0))
    full = lambda a: pl.BlockSpec(a.shape, lambda i: (0, 0))
    return pl.pallas_call(
        body, name="wo_bwd", grid=(T // tm,),
        in_specs=[row(D_MODEL), full(wo), row(WIDTH), row(WIDTH), full(ga), full(gb), row(D_MODEL)],
        out_specs=[row(WIDTH)] + _view_specs(tm)
        + [pl.BlockSpec((1, WIDTH), lambda i: (0, 0)), pl.BlockSpec((1, WIDTH), lambda i: (0, 0)), full(wo)],
        out_shape=[jax.ShapeDtypeStruct((T, WIDTH), BF16)] + _view_shapes(T, BF16)
        + [jax.ShapeDtypeStruct((1, WIDTH), F32), jax.ShapeDtypeStruct((1, WIDTH), F32),
           jax.ShapeDtypeStruct((D_MODEL, D_MODEL), BF16)],
        scratch_shapes=[_scr(tm), pltpu.VMEM((D_MODEL, D_MODEL), F32)],
        compiler_params=_params(("arbitrary",)),
    )(dx2b, wo, oa, ob, ga, gb, mixed)


def _inproj_bwd(dqa, dka, dva, dqs, dks, dvs, cos, sin, w, x, dx2, g, h1, *, tm=512, wc=256):
    T = dqa.shape[0]
    n_tiles = T // tm

    def body(dqa_ref, dka_ref, dva_ref, q1, q2, q3, k1, k2, k3, v1, v2, v3, cos_ref, sin_ref, w_ref, x_ref, dx2_ref,
             g_ref, h1_ref, db_ref, gx_ref, dg_ref, dw_ref, dp_ref, dw_acc, acc, tmp):
        @pl.when(pl.program_id(0) == 0)
        def _():
            db_ref[...] = jnp.zeros_like(db_ref)
            dg_ref[...] = jnp.zeros_like(dg_ref)
            dw_acc[...] = jnp.zeros_like(dw_acc)

        cosv = cos_ref[...]
        sinv = sin_ref[...]
        lane = lax.broadcasted_iota(jnp.int32, (tm, PAIR), 1)
        first = (lane % HEAD_DIM) < (HEAD_DIM // 2)

        def put(off, val):
            dp_ref[:, off:off + PAIR] = val.astype(BF16)
            db_ref[:, off:off + PAIR] += jnp.sum(val, axis=0, keepdims=True)

        for src, off, width in ((dqa_ref, 0, 512), (dka_ref, 512, 256)):
            for j in range(0, width, PAIR):
                d = src[:, j:j + PAIR].astype(F32)
                put(off + j, d * cosv - _rope_rot(d, first) * sinv)
        for j in range(0, 256, PAIR):
            put(768 + j, dva_ref[:, j:j + PAIR].astype(F32))
        for (a, b, c), off in (((q1, q2, q3), 1024), ((k1, k2, k3), 1536), ((v1, v2, v3), 2048)):
            _unstride(b, acc, BRANCHES[1][1], tm)
            _unstride(c, tmp, BRANCHES[2][1], tm)
            for j in range(N_CHUNK):
                put(off + j * PAIR, a[:, j * PAIR:(j + 1) * PAIR].astype(F32) + acc[j] + tmp[j])

        dh = jnp.dot(dp_ref[...], w_ref[...], preferred_element_type=F32)
        xv = x_ref[...]
        r = lax.rsqrt(jnp.mean(xv * xv, axis=-1, keepdims=True) + EPS)
        xh = xv * r
        dg_ref[...] += jnp.sum(dh * xh, axis=0, keepdims=True)
        gx_ref[...] = dx2_ref[...] + _rms_bwd(dh, xh, r, g_ref[...])

        h1v = h1_ref[...]
        for c in range(0, D_INP, wc):
            dw_acc[c:c + wc, :] += lax.dot_general(dp_ref[:, c:c + wc], h1v, TN, preferred_element_type=F32)

        @pl.when(pl.program_id(0) == n_tiles - 1)
        def _():
            pltpu.sync_copy(dw_acc, dw_ref)

    row = lambda w_: pl.BlockSpec((tm, w_), lambda i: (i, 0))
    full = lambda a: pl.BlockSpec(a.shape, lambda i: (0, 0))
    return pl.pallas_call(
        body, name="inproj_bwd", grid=(n_tiles,),
        in_specs=[row(512), row(256), row(256)] + _view_specs(tm) * 3 + [row(PAIR), row(PAIR)]
        + [pl.BlockSpec(w.shape, lambda i: (0, 0), pipeline_mode=pl.Buffered(1)), row(D_MODEL), row(D_MODEL), full(g),
           row(D_MODEL)],
        out_specs=[pl.BlockSpec((1, D_INP), lambda i: (0, 0)), row(D_MODEL),
                   pl.BlockSpec((1, D_MODEL), lambda i: (0, 0)), pl.BlockSpec(memory_space=pl.ANY)],
        out_shape=[jax.ShapeDtypeStruct((1, D_INP), F32), jax.ShapeDtypeStruct((T, D_MODEL), F32),
                   jax.ShapeDtypeStruct((1, D_MODEL), F32), jax.ShapeDtypeStruct((D_INP, D_MODEL), F32)],
        scratch_shapes=[pltpu.VMEM((tm, D_INP), BF16), pltpu.VMEM((D_INP, D_MODEL), F32), _scr(tm), _scr(tm)],
        compiler_params=_params(("arbitrary",)),
    )(dqa, dka, dva, *dqs, *dks, *dvs, cos, sin, w, x, dx2, g, h1)


def _bias_sink_grads(dsums, bmaps, dsk):
    def body(s1, s2, s3, m1, m2, m3, dsk_ref, drel_ref, dsink_ref):
        row = lax.broadcasted_iota(jnp.int32, (N_HEADS, 128), 0)
        lane = lax.broadcasted_iota(jnp.int32, (N_HEADS, 128), 1)
        out = jnp.zeros((N_HEADS, 128), F32)
        for s_ref, m_ref in ((s1, m1), (s2, m2), (s3, m3)):
            bm = m_ref[...]
            for h in range(N_HEADS):
                a = s_ref[h]
                for b in range(REL_BUCKETS):
                    v = jnp.sum(jnp.sum(jnp.where(bm == b, a, 0.0), axis=-1, keepdims=True), axis=0, keepdims=True)
                    out = out + jnp.where((row == h) & (lane == b), v, 0.0)
        drel_ref[...] = out
        dsink_ref[...] = -jnp.sum(dsk_ref[...], axis=0, keepdims=True)

    vm = pl.BlockSpec(memory_space=pltpu.VMEM)
    return pl.pallas_call(
        body, name="bias_sink_grads",
        in_specs=[vm] * 7, out_specs=[vm, vm],
        out_shape=[jax.ShapeDtypeStruct((N_HEADS, 128), F32), jax.ShapeDtypeStruct((1, WIDTH), F32)],
        compiler_params=_params(),
    )(*dsums, *bmaps, dsk)


def _all_gather(blk, *, name):
    R, C = blk.shape

    def body(x_ref, out_ref, send_sems, recv_sems, local_sem):
        x, y, c = lax.axis_index("x"), lax.axis_index("y"), lax.axis_index("c")
        me, sibling = (x, y, c), (x, y, 1 - c)
        chips = [(1 - x, y), (x, 1 - y), (1 - x, 1 - y)]

        def slot(px, py, pc):
            return out_ref.at[4 * px + 2 * py + pc]

        def copy(k, block, to, src=None):
            return pltpu.make_async_remote_copy(
                src_ref=slot(*block) if src is None else src, dst_ref=slot(*block),
                send_sem=send_sems.at[k], recv_sem=recv_sems.at[k], device_id=to, device_id_type=MESH)

        mine = pltpu.make_async_copy(x_ref, slot(*me), local_sem)
        mine.start()
        first = [copy(0, me, sibling, src=x_ref)]
        first += [copy(1 + j, me, (*chip, c), src=x_ref) for j, chip in enumerate(chips)]
        for cp in first:
            cp.start()
        passed = [copy(4 + j, (*chip, c), sibling) for j, chip in enumerate(chips)]
        for j, chip in enumerate(chips):
            copy(1 + j, (*chip, c), me).wait_recv()
            passed[j].start()
        copy(0, sibling, me).wait_recv()
        for j, chip in enumerate(chips):
            copy(4 + j, (*chip, 1 - c), me).wait_recv()
        for cp in first + passed:
            cp.wait_send()
        mine.wait()

    return pl.pallas_call(
        body, name=name,
        in_specs=[pl.BlockSpec(memory_space=pl.ANY)], out_specs=pl.BlockSpec(memory_space=pl.ANY),
        out_shape=jax.ShapeDtypeStruct((N_DEV, R, C), blk.dtype),
        scratch_shapes=[pltpu.SemaphoreType.DMA((7,)), pltpu.SemaphoreType.DMA((7,)), pltpu.SemaphoreType.DMA],
        compiler_params=pltpu.CompilerParams(has_side_effects=True),
    )(blk)


def _peers(x, y, c):
    return [(x ^ (k >> 2), y ^ ((k >> 1) & 1), c ^ (k & 1)) for k in range(1, N_DEV)]


_HBM = pl.BlockSpec(memory_space=pltpu.HBM)
_SEM = pl.BlockSpec(memory_space=pltpu.SEMAPHORE)
_EFFECT = pltpu.SideEffectType.DATAFLOW_SIDE_EFFECTING


def _peer_list(x, y, c, near):
    if near:
        return [(x, y, 1 - c), (1 - x, y, c), (x, 1 - y, c), (1 - x, 1 - y, c)]
    return _peers(x, y, c)


def _exchange_start(srcs, *, gather, name, near=False):
    n = len(srcs)
    n_peers = 4 if near else N_DEV - 1
    lands = [lax.empty((N_DEV,) + s.shape[-2:], s.dtype) for s in srcs]

    def body(*refs):
        src_refs, land_refs = refs[:n], refs[n:2 * n]
        send_sems, recv_sems = refs[2 * n], refs[2 * n + 1]
        token = refs[-1]
        x, y, c = lax.axis_index("x"), lax.axis_index("y"), lax.axis_index("c")
        mine = 4 * x + 2 * y + c
        for a in range(n):
            for k, peer in enumerate(_peer_list(x, y, c, near)):
                dest = 4 * peer[0] + 2 * peer[1] + peer[2]
                j = a * n_peers + k
                pltpu.make_async_remote_copy(
                    src_ref=src_refs[a] if gather else src_refs[a].at[dest], dst_ref=land_refs[a].at[mine],
                    send_sem=send_sems.at[j], recv_sem=recv_sems.at[j], device_id=peer, device_id_type=MESH).start()
        token[...] = jnp.zeros_like(token)

    sems = pltpu.SemaphoreType.DMA((n * n_peers,))
    out = pl.pallas_call(
        body, name=name,
        out_shape=(sems, sems) + tuple(pltpu.HBM(a.shape, a.dtype) for a in list(srcs) + lands)
        + (jax.ShapeDtypeStruct((8, 128), F32),),
        in_specs=(_HBM,) * (2 * n), out_specs=(_SEM, _SEM) + (_HBM,) * (2 * n) + (pl.BlockSpec(memory_space=pltpu.VMEM),),
        input_output_aliases={i: 2 + i for i in range(2 * n)},
        compiler_params=pltpu.CompilerParams(has_side_effects=_EFFECT),
    )(*[pltpu.with_memory_space_constraint(a, pltpu.HBM) for a in list(srcs) + lands])
    return out[:-1], out[-1]


def _exchange_wait(state, after, *, gather, name, near=False):
    send_sems, recv_sems = state[0], state[1]
    n = (len(state) - 2) // 2
    n_peers = 4 if near else N_DEV - 1
    arrays = state[2:]

    def body(*refs):
        src_refs, land_refs = refs[:n], refs[n:2 * n]
        send_sems, recv_sems = refs[2 * n], refs[2 * n + 1]
        x, y, c = lax.axis_index("x"), lax.axis_index("y"), lax.axis_index("c")
        for a in range(n):
            for k, peer in enumerate(_peer_list(x, y, c, near)):
                other = 4 * peer[0] + 2 * peer[1] + peer[2]
                j = a * n_peers + k
                copy = pltpu.make_async_remote_copy(
                    src_ref=src_refs[a] if gather else src_refs[a].at[other], dst_ref=land_refs[a].at[other],
                    send_sem=send_sems.at[j], recv_sem=recv_sems.at[j], device_id=peer, device_id_type=MESH)
                copy.wait_send()
                copy.wait_recv()

    out = pl.pallas_call(
        body, name=name,
        out_shape=tuple(pltpu.HBM(a.shape, a.dtype) for a in arrays),
        in_specs=(_HBM,) * (2 * n) + (_SEM, _SEM, pl.BlockSpec(memory_space=pl.ANY)), out_specs=(_HBM,) * (2 * n),
        input_output_aliases={i: i for i in range(2 * n)},
        compiler_params=pltpu.CompilerParams(has_side_effects=_EFFECT),
    )(*arrays, send_sems, recv_sems, after)
    mine = 4 * lax.axis_index("x") + 2 * lax.axis_index("y") + lax.axis_index("c")
    own = out[:n] if gather else [lax.dynamic_index_in_dim(s, mine, 0, keepdims=False) for s in out[:n]]
    return [lax.dynamic_update_slice(g, o[None], (mine, 0, 0)) for g, o in zip(out[n:], own)]


def _forward_start(lands, *, name):
    n = len(lands)

    def body(*refs):
        land_refs, send_sems, recv_sems, token = refs[:n], refs[n], refs[n + 1], refs[-1]
        x, y, c = lax.axis_index("x"), lax.axis_index("y"), lax.axis_index("c")
        for a in range(n):
            for j, (px, py) in enumerate(((1 - x, y), (x, 1 - y), (1 - x, 1 - y))):
                blk = 4 * px + 2 * py + c
                pltpu.make_async_remote_copy(
                    src_ref=land_refs[a].at[blk], dst_ref=land_refs[a].at[blk], send_sem=send_sems.at[3 * a + j],
                    recv_sem=recv_sems.at[3 * a + j], device_id=(x, y, 1 - c), device_id_type=MESH).start()
        token[...] = jnp.zeros_like(token)

    sems = pltpu.SemaphoreType.DMA((3 * n,))
    out = pl.pallas_call(
        body, name=name,
        out_shape=(sems, sems) + tuple(pltpu.HBM(a.shape, a.dtype) for a in lands) + (jax.ShapeDtypeStruct((8, 128), F32),),
        in_specs=(_HBM,) * n, out_specs=(_SEM, _SEM) + (_HBM,) * n + (pl.BlockSpec(memory_space=pltpu.VMEM),),
        input_output_aliases={i: 2 + i for i in range(n)},
        compiler_params=pltpu.CompilerParams(has_side_effects=_EFFECT),
    )(*[pltpu.with_memory_space_constraint(a, pltpu.HBM) for a in lands])
    return out[:-1], out[-1]


def _forward_wait(state, after, *, name):
    send_sems, recv_sems = state[0], state[1]
    lands = state[2:]
    n = len(lands)

    def body(*refs):
        land_refs, send_sems, recv_sems = refs[:n], refs[n], refs[n + 1]
        x, y, c = lax.axis_index("x"), lax.axis_index("y"), lax.axis_index("c")
        for a in range(n):
            for j, (px, py) in enumerate(((1 - x, y), (x, 1 - y), (1 - x, 1 - y))):
                copy = pltpu.make_async_remote_copy(
                    src_ref=land_refs[a].at[4 * px + 2 * py + c], dst_ref=land_refs[a].at[4 * px + 2 * py + 1 - c],
                    send_sem=send_sems.at[3 * a + j], recv_sem=recv_sems.at[3 * a + j], device_id=(x, y, 1 - c),
                    device_id_type=MESH)
                copy.wait_send()
                copy.wait_recv()

    return pl.pallas_call(
        body, name=name,
        out_shape=tuple(pltpu.HBM(a.shape, a.dtype) for a in lands),
        in_specs=(_HBM,) * n + (_SEM, _SEM, pl.BlockSpec(memory_space=pl.ANY)), out_specs=(_HBM,) * n,
        input_output_aliases={i: i for i in range(n)},
        compiler_params=pltpu.CompilerParams(has_side_effects=_EFFECT),
    )(*lands, send_sems, recv_sems, after)


def _adam_math(w, g, m, v):
    m = ADAM_B1 * m + (1.0 - ADAM_B1) * g
    v = ADAM_B2 * v + (1.0 - ADAM_B2) * (g * g)
    m_hat = m / (1.0 - ADAM_B1 ** ADAM_STEP)
    v_hat = v / (1.0 - ADAM_B2 ** ADAM_STEP)
    delta = -ADAM_LR * (m_hat / (jnp.sqrt(v_hat) + ADAM_EPS) + ADAM_WD * w)
    return delta, m, v


def _adamw(parts, w, m, v, *, name):
    R, C = w.shape
    n_parts = parts.shape[0]
    tr = R // 2
    assert tr % 16 == 0

    def body(p_ref, w_ref, m_ref, v_ref, g_ref, d_ref, nm_ref, nv_ref):
        g = p_ref[0].astype(F32)
        for s in range(1, n_parts):
            g = g + p_ref[s].astype(F32)
        d, nm, nv = _adam_math(w_ref[...], g, m_ref[...], v_ref[...])
        g_ref[...] = g
        d_ref[...] = d
        nm_ref[...] = nm
        nv_ref[...] = nv

    blk = pl.BlockSpec((tr, C), lambda i: (i, 0))
    return pl.pallas_call(
        body, name=name, grid=(R // tr,),
        in_specs=[pl.BlockSpec((n_parts, tr, C), lambda i: (0, i, 0)), blk, blk, blk],
        out_specs=[blk] * 4, out_shape=[jax.ShapeDtypeStruct((R, C), F32)] * 4,
        compiler_params=_params(("arbitrary",)),
    )(parts, w, m, v)


def _adamw_small(parts, w, m, v):
    def body(p_ref, w_ref, m_ref, v_ref, g_ref, d_ref, nm_ref, nv_ref):
        g = p_ref[0]
        for s in range(1, N_DEV):
            g = g + p_ref[s]
        d, nm, nv = _adam_math(w_ref[...], g, m_ref[...], v_ref[...])
        g_ref[...] = g
        d_ref[...] = d
        nm_ref[...] = nm
        nv_ref[...] = nv

    vm = pl.BlockSpec(memory_space=pltpu.VMEM)
    return pl.pallas_call(
        body, name="adamw_small", in_specs=[vm] * 4, out_specs=[vm] * 4,
        out_shape=[jax.ShapeDtypeStruct((SMALL_ROWS, 128), F32)] * 4, compiler_params=_params(),
    )(parts, w, m, v)


def _t5_bucket(dist):
    max_exact = REL_BUCKETS // 2
    df = jnp.maximum(dist, 1).astype(F32)
    large = max_exact + (jnp.log(df / max_exact) / math.log(REL_MAX_DISTANCE / max_exact)
                         * (REL_BUCKETS - max_exact)).astype(jnp.int32)
    large = jnp.minimum(large, REL_BUCKETS - 1)
    return jnp.where(dist < max_exact, dist, large)


def _band_tables(rel_table, dil, n_back):
    qi = jnp.arange(BLK)[:, None]
    kj = jnp.arange(2 * BLK)[None, :]
    delta = BLK + qi - kj
    in_band = (delta >= 0) & (delta <= n_back)
    if rel_table is None:
        vals = jnp.zeros((N_HEADS, BLK, 2 * BLK), F32)
        bmap = None
    else:
        bucket = _t5_bucket(jnp.clip(delta, 0, n_back) * dil)
        vals = jnp.zeros((N_HEADS, BLK, 2 * BLK), F32)
        for b in range(REL_BUCKETS):
            vals = jnp.where((bucket == b)[None], rel_table[b][:, None, None], vals)
        bmap = jnp.where(in_band, bucket, -1).astype(jnp.int32)
    later = jnp.where(in_band[None], vals, NEG)
    first = jnp.where((in_band & (kj >= BLK))[None], vals, NEG)
    return jnp.stack([later, first]), bmap


def _rope_tables(T):
    half = HEAD_DIM // 2
    inv_freq = ROPE_THETA ** (-jnp.arange(half, dtype=F32) / half)
    ang = jnp.arange(T, dtype=F32)[:, None] * inv_freq[None, :]
    cos, sin = jnp.cos(ang), jnp.sin(ang)
    return jnp.tile(cos, (1, 4)), jnp.tile(jnp.concatenate([-sin, sin], axis=1), (1, 2))


def _widen_in(a, axis):
    sl = lambda lo, hi: lax.slice_in_dim(a, lo, hi, axis=axis)
    dup = lambda lo: [sl(lo, lo + 64), sl(lo, lo + 64), sl(lo + 64, lo + 128), sl(lo + 64, lo + 128)]
    return jnp.concatenate([sl(0, 512)] + dup(512) + dup(640) + [sl(768, D_IN)], axis=axis)


def _fold_in(a, axis):
    sl = lambda lo, hi: lax.slice_in_dim(a, lo, hi, axis=axis)
    fold = lambda lo: [sl(lo, lo + 64) + sl(lo + 64, lo + 128), sl(lo + 128, lo + 192) + sl(lo + 192, lo + 256)]
    return jnp.concatenate([sl(0, 512)] + fold(512) + fold(768) + [sl(1024, D_INP)], axis=axis)


def _local_step(x, tgt, g_attn, b_in, sinks, rel_table, g_out_a, g_out_b, g_ffn, g_final,
                win_fn, wo_fn, ffn_fn, early_fn):
    T = x.shape[0]
    cos, sin = _rope_tables(T)
    g_final2 = g_final.reshape(1, D_MODEL)
    sink8 = sinks.reshape(N_HEADS)

    bias_a, _ = _band_tables(None, 1, BLK - 1)
    tabs = [_band_tables(rel_table, dil, window // dil) for window, dil in BRANCHES]
    wint, token = win_fn(tabs[2][0])
    winp = _widen_in(wint, 0)
    binp = _widen_in(b_in, 1) + token[0, 0]

    h1, qa, ka, va, *qkv_b = _norm_proj(x, g_attn, winp, binp, cos, sin)
    qbs, kbs, vbs = qkv_b[0:3], qkv_b[3:6], qkv_b[6:9]
    oa, lse_a = _attn_fwd(qa, ka, va, bias_a, sink8, dil=1, kv_pairs=2, use_sink=True, name="attn_a_fwd")
    outs = [_attn_fwd(qbs[n], kbs[n], vbs[n], tabs[n][0], sink8, dil=dil, kv_pairs=4, use_sink=False,
                      name=f"attn_b{n}_fwd") for n, (_, dil) in enumerate(BRANCHES)]
    wo = wo_fn(outs[2][1])
    x2, mixed, h2, *ob_lse = _merge_wo(x, oa, outs[0][0], outs[1][0], outs[2][0], outs[0][1], outs[1][1], outs[2][1],
                                       g_out_a, g_out_b, wo, g_ffn)
    obs, lses = ob_lse[0:3], ob_lse[3:6]
    wgt, wut, wd = ffn_fn(h2)
    gate, up, act = _ffn_up(h2, wgt, wut)
    dx3, dx3b, loss, dg_final = _ffn_down_loss(act, wd, x2, tgt, g_final2)

    dgate, dup, dx2, dx2b, dg_ffn = _ffn_bwd(dx3, gate, up, wd, wgt, wut, x2, g_ffn)
    dwd = _matmul_tn(act, dx3b, tk=1408, tn=1024, name="dw_down")
    dwgt = _matmul_tn(dgate, h2, tk=1408, tn=1024, name="dw_gate")
    dwut = _matmul_tn(dup, h2, tk=1408, tn=1024, name="dw_up")
    doa, *dobs, dg_out_a, dg_out_b, dwo = _wo_bwd(dx2b, wo, oa, obs[0], g_out_a, g_out_b, mixed)
    early, token2 = early_fn(dict(w_o=dwo, w_gate=dwgt, w_up=dwut, w_down=dwd))
    sink8b = sink8 + token2[0, 0]

    dqa, dka, dva, _, dsk = _attn_bwd(qa, ka, va, oa, doa, lse_a, bias_a, sink8b, dil=1, kv_pairs=2, use_sink=True,
                                      name="attn_a_bwd", max_sub=4)
    res = [_attn_bwd(qbs[n], kbs[n], vbs[n], obs[n], dobs[n], lses[n], tabs[n][0], sink8b, dil=dil, kv_pairs=4,
                     use_sink=False, name=f"attn_b{n}_bwd") for n, (_, dil) in enumerate(BRANCHES)]
    dbp, grad_x, dg_attn, dwinp = _inproj_bwd(dqa, dka, dva, [r[0] for r in res], [r[1] for r in res],
                                              [r[2] for r in res], cos, sin, winp, x, dx2, g_attn, h1)
    dwin = _fold_in(dwinp, 0)
    drel, dsink = _bias_sink_grads([r[3] for r in res], [t[1] for t in tabs], dsk)

    small = dict(
        g_attn=dg_attn, b_in=_fold_in(dbp, 1), sinks=dsink[:, ::HEAD_DIM], rel_table=drel[:, :REL_BUCKETS].T,
        g_out_a=dg_out_a, g_out_b=dg_out_b, g_ffn=dg_ffn, g_final=dg_final.reshape(D_MODEL))
    return loss[0, 0], grad_x, dwin, early, small


SMALL_NAMES = ("g_attn", "b_in", "sinks", "rel_table", "g_out_a", "g_out_b", "g_ffn", "g_final", "loss")


def _pack_small(vals):
    flat = jnp.concatenate([vals[n].reshape(-1).astype(F32) for n in SMALL_NAMES])
    return jnp.pad(flat, (0, SMALL_ROWS * 128 - flat.shape[0])).reshape(SMALL_ROWS, 128)


def _unpack_small(packed, like):
    flat = packed.reshape(-1)
    out, off = {}, 0
    for n in SMALL_NAMES:
        size = like[n].size
        out[n] = flat[off:off + size].reshape(like[n].shape)
        off += size
    return out


def kernel(x, g_attn, w_in, b_in, sinks, rel_table, g_out_a, g_out_b, w_o, g_ffn, w_gate, w_up, w_down, g_final, loss_target, m_g_attn, m_w_in, m_b_in, m_sinks, m_rel_table, m_g_out_a, m_g_out_b, m_w_o, m_g_ffn, m_w_gate, m_w_up, m_w_down, m_g_final, v_g_attn, v_w_in, v_b_in, v_sinks, v_rel_table, v_g_out_a, v_g_out_b, v_w_o, v_g_ffn, v_w_gate, v_w_up, v_w_down, v_g_final):
    rest_names = ("w_o", "w_gate", "w_up", "w_down")

    rest = [w_o[0].astype(BF16), w_gate[0].astype(BF16).T, w_up[0].astype(BF16).T, w_down[0].astype(BF16)]
    in_state, _ = _exchange_start([w_in[0].astype(BF16).T], gather=True, near=True, name="gather_w_in_start")
    later = {}

    def whole(got):
        return [g.reshape(N_DEV * g.shape[1], D_MODEL) for g in got]

    def win_fn(after):
        near = _exchange_wait(in_state, after, gather=True, near=True, name="gather_w_in_near")
        fwd_state, tok = _forward_start(near, name="gather_w_in_forward")
        wint = whole(_forward_wait(fwd_state, tok, name="gather_w_in_wait"))[0]
        wint, src = lax.optimization_barrier((wint, rest))
        later["wo"], token_o = _exchange_start(src[:1], gather=True, name="gather_w_o_start")
        token_o, ffn_src = lax.optimization_barrier((token_o, src[1:]))
        later["ffn"], token = _exchange_start(ffn_src, gather=True, name="gather_ffn_start")
        return wint, token + token_o

    def wo_fn(after):
        return whole(_exchange_wait(later["wo"], after, gather=True, name="gather_w_o_wait"))[0]

    def ffn_fn(after):
        return whole(_exchange_wait(later["ffn"], after, gather=True, name="gather_ffn_wait"))

    def early_fn(dws):
        return _exchange_start([dws[n].reshape(N_DEV, -1, D_MODEL) for n in rest_names], gather=False,
                               name="scatter_rest_start")

    loss_part, grad_x, dwint, early_state, small = _local_step(
        x[0], loss_target[0], g_attn, b_in, sinks, rel_table, g_out_a, g_out_b, g_ffn, g_final,
        win_fn, wo_fn, ffn_fn, early_fn)
    parts_in = dwint.astype(BF16).reshape(N_DEV, D_IN // N_DEV, D_MODEL)
    in_state, token3 = _exchange_start([parts_in], gather=False, name="scatter_w_in_start")
    got = _exchange_wait(early_state, token3, gather=False, name="scatter_rest_wait")

    def update(n, parts, w, m, v, transposed):
        if transposed:
            return [a.T[None] for a in _adamw(parts, w[0].T, m[0].T, v[0].T, name="adamw_" + n)]
        return [a[None] for a in _adamw(parts, w[0], m[0], v[0], name="adamw_" + n)]

    big = dict(w_o=update("w_o", got[0], w_o, m_w_o, v_w_o, False),
               w_gate=update("w_gate", got[1], w_gate, m_w_gate, v_w_gate, True),
               w_up=update("w_up", got[2], w_up, m_w_up, v_w_up, True),
               w_down=update("w_down", got[3], w_down, m_w_down, v_w_down, False))

    unused = jnp.zeros((1,), F32)
    ws = dict(g_attn=g_attn, b_in=b_in, sinks=sinks, rel_table=rel_table, g_out_a=g_out_a, g_out_b=g_out_b,
              g_ffn=g_ffn, g_final=g_final, loss=unused)
    ms = dict(g_attn=m_g_attn, b_in=m_b_in, sinks=m_sinks, rel_table=m_rel_table, g_out_a=m_g_out_a,
              g_out_b=m_g_out_b, g_ffn=m_g_ffn, g_final=m_g_final, loss=unused)
    vs = dict(g_attn=v_g_attn, b_in=v_b_in, sinks=v_sinks, rel_table=v_rel_table, g_out_a=v_g_out_a,
              g_out_b=v_g_out_b, g_ffn=v_g_ffn, g_final=v_g_final, loss=unused)
    sparts = _all_gather(_pack_small(dict(small, loss=loss_part)), name="gather_small")
    sm_packed = _adamw_small(sparts, _pack_small(ws), _pack_small(ms), _pack_small(vs))
    sm = [_unpack_small(a, ws) for a in sm_packed]
    loss = sm[0]["loss"][0]

    done = sm_packed[1][:1, :1] + sum(big[n][1][0, :1, :1] for n in rest_names)
    got_in = _exchange_wait(in_state, done, gather=False, name="scatter_w_in_wait")[0]
    big["w_in"] = update("w_in", got_in, w_in, m_w_in, v_w_in, True)

    order = ("g_attn", "w_in", "b_in", "sinks", "rel_table", "g_out_a", "g_out_b", "w_o", "g_ffn", "w_gate", "w_up",
             "w_down", "g_final")
    outs = [loss, grad_x[None]]
    for k in range(4):
        outs += [big[n][k] if n in big else sm[k][n] for n in order]
    return tuple(outs)
```

```python
import functools
import math

import jax
import jax.numpy as jnp
from jax import lax
from jax.experimental import pallas as pl
from jax.experimental.pallas import tpu as pltpu

F32 = jnp.float32
BF16 = jnp.bfloat16

N_DEV = 8
D_MODEL = 1024
HEAD_DIM = 64
N_HEADS = 8
PAIR = 2 * HEAD_DIM
WIDTH = N_HEADS * HEAD_DIM
D_IN = 2304
D_INP = 2560
D_FF = 2816
BLK = 128
ROPE_THETA = 150000.0
REL_BUCKETS = 32
REL_MAX_DISTANCE = 2048
EPS = 1e-5
NEG = -1e30
BRANCHES = ((128, 1), (512, 4), (2048, 16))
Q_SCALE = HEAD_DIM ** -0.5

ADAM_LR = 0.001
ADAM_B1 = 0.9
ADAM_B2 = 0.999
ADAM_EPS = 1e-08
ADAM_WD = 0.01
ADAM_STEP = 10

VMEM_LIMIT = 56 * 1024 * 1024
MESH = pl.DeviceIdType.MESH

NT = (((1,), (1,)), ((), ()))
TN = (((0,), (0,)), ((), ()))

SMALL_ROWS = 56


def _params(sem=None):
    return pltpu.CompilerParams(dimension_semantics=sem, vmem_limit_bytes=VMEM_LIMIT)


def _sigmoid(x):
    return 1.0 / (1.0 + jnp.exp2(x * (-1.0 / math.log(2.0))))


def _rms_bwd(dh, xh, r, g):
    u = dh * g
    return r * (u - xh * jnp.mean(u * xh, axis=-1, keepdims=True))


def _rope_rot(t, first):
    return jnp.where(first, pltpu.roll(t, 96, 1), pltpu.roll(t, 32, 1))


N_CHUNK = WIDTH // PAIR


def _scr(tm):
    return pltpu.VMEM((N_CHUNK, tm, PAIR), F32)


def _scr_get(scr):
    return jnp.concatenate([scr[j] for j in range(N_CHUNK)], axis=1)


def _scr_put(scr, val):
    for j in range(N_CHUNK):
        scr[j] = val[:, j * PAIR:(j + 1) * PAIR]


def _unstride(view_ref, scr, dil, tm):
    n = tm // dil
    chunks = scr.shape[0]
    for r in range(dil):
        for j in range(chunks):
            col = (r * chunks + j) * PAIR
            scr.at[j][pl.ds(r, n, stride=dil), :] = view_ref[:, col:col + PAIR].astype(F32)


def _restride(scr, out_ref, dil, tm):
    n = tm // dil
    chunks = scr.shape[0]
    for r in range(dil):
        for j in range(chunks):
            col = (r * chunks + j) * PAIR
            rows = scr[j] if dil == 1 else scr.at[j][pl.ds(r, n, stride=dil), :]
            out_ref[:, col:col + PAIR] = rows.astype(out_ref.dtype)


def _view_specs(tm, width=WIDTH):
    return [pl.BlockSpec((tm // dil, dil * width), lambda i: (i, 0)) for _, dil in BRANCHES]


def _view_shapes(T, dtype, width=WIDTH):
    return [jax.ShapeDtypeStruct((T // dil, dil * width), dtype) for _, dil in BRANCHES]


def _norm_proj(x, g, w, b, cos, sin, *, tm=512):
    T = x.shape[0]

    def body(x_ref, g_ref, w_ref, b_ref, cos_ref, sin_ref, h_ref, qa_ref, ka_ref, va_ref, *rest):
        outs_b, ys = rest[:9], rest[9]
        xv = x_ref[...]
        r = lax.rsqrt(jnp.mean(xv * xv, axis=-1, keepdims=True) + EPS)
        h = (xv * r * g_ref[...]).astype(BF16)
        h_ref[...] = h
        cosv = cos_ref[...]
        sinv = sin_ref[...]
        lane = lax.broadcasted_iota(jnp.int32, (tm, PAIR), 1)
        first = (lane % HEAD_DIM) < (HEAD_DIM // 2)

        def proj(off):
            return (lax.dot_general(h, w_ref[off:off + 256, :], NT, preferred_element_type=F32)
                    + b_ref[:, off:off + 256])

        for (off, width, rot, scale), o_ref in zip(((0, 512, True, Q_SCALE), (512, 256, True, 1.0), (768, 256, False, 1.0)),
                                                   (qa_ref, ka_ref, va_ref)):
            for c in range(0, width, 256):
                y = proj(off + c)
                for j in range(0, 256, PAIR):
                    t = y[:, j:j + PAIR]
                    if rot:
                        t = t * cosv + _rope_rot(t, first) * sinv
                    if scale != 1.0:
                        t = t * scale
                    o_ref[:, c + j:c + j + PAIR] = t.astype(BF16)
        for n, (off, scale) in enumerate(((1024, Q_SCALE), (1536, 1.0), (2048, 1.0))):
            for c in range(0, WIDTH, 256):
                y = proj(off + c)
                y = y * scale if scale != 1.0 else y
                for j in range(0, 256, PAIR):
                    ys[(c + j) // PAIR] = y[:, j:j + PAIR]
            for (_, dil), o_ref in zip(BRANCHES, outs_b[3 * n:3 * n + 3]):
                _restride(ys, o_ref, dil, tm)

    row = lambda w_: pl.BlockSpec((tm, w_), lambda i: (i, 0))
    full = lambda a: pl.BlockSpec(a.shape, lambda i: (0, 0))
    return pl.pallas_call(
        body, name="norm_proj", grid=(T // tm,),
        in_specs=[row(D_MODEL), full(g), full(w), full(b), row(PAIR), row(PAIR)],
        out_specs=[row(D_MODEL), row(512), row(256), row(256)] + _view_specs(tm) * 3,
        out_shape=[jax.ShapeDtypeStruct((T, n), BF16) for n in (D_MODEL, 512, 256, 256)] + _view_shapes(T, BF16) * 3,
        scratch_shapes=[_scr(tm)],
        compiler_params=_params(("arbitrary",)),
    )(x, g, w, b, cos, sin)


MAX_SUB = 8
AHEAD = 2


def _attn_specs(kvw, sub):
    q_spec = pl.BlockSpec((sub * BLK, WIDTH), lambda r, i: (i, r))
    kc_spec = pl.BlockSpec((sub * BLK, kvw), lambda r, i: (i, r))
    kp_spec = pl.BlockSpec((BLK, kvw), lambda r, i: (jnp.maximum(sub * i - 1, 0), r))
    b_spec = pl.BlockSpec((2, N_HEADS, BLK, 2 * BLK), lambda r, i: (0, 0, 0, 0))
    return q_spec, kp_spec, kc_spec, b_spec


def _window(prev_ref, cur_ref, j, ksl):
    before = prev_ref[:, ksl] if j == 0 else cur_ref[(j - 1) * BLK:j * BLK, ksl]
    return jnp.concatenate([before, cur_ref[j * BLK:(j + 1) * BLK, ksl]], axis=0)


def _attn_fwd(q, k, v, bias, sinks, *, dil, kv_pairs, use_sink, name):
    L = q.shape[0]
    sub = min(MAX_SUB, L // BLK)
    ns = L // (sub * BLK)
    kvw = kv_pairs * PAIR
    rep = 4 // kv_pairs

    def body(sink_ref, q_ref, kp_ref, kc_ref, vp_ref, vc_ref, b_ref, o_ref, lse_ref):
        lane = lax.broadcasted_iota(jnp.int32, (1, PAIR), 1)
        lo = lane < HEAD_DIM
        first = jnp.where(pl.program_id(1) == 0, 1, 0)
        def scores(j, hp):
            rows = slice(j * BLK, (j + 1) * BLK)
            sl = slice(hp * PAIR, (hp + 1) * PAIR)
            ksl = slice((hp // rep) * PAIR, (hp // rep + 1) * PAIR)
            qp = q_ref[rows, sl]
            kk = _window(kp_ref, kc_ref, j, ksl)
            vv = _window(vp_ref, vc_ref, j, ksl)
            heads = []
            for e in range(2):
                h = 2 * hp + e
                msk = lo if e == 0 else jnp.logical_not(lo)
                qm = jnp.where(msk, qp, jnp.zeros_like(qp))
                s = lax.dot_general(qm, kk, NT, preferred_element_type=F32) + (b_ref[first, h] if j == 0 else b_ref[0, h])
                heads.append((h, msk, s))
            return rows, sl, vv, heads

        def outputs(rows, sl, vv, heads):
            o_pair = None
            lse_pair = None
            for h, msk, s in heads:
                m = jnp.max(s, axis=-1, keepdims=True)
                if use_sink:
                    sk = sink_ref[h]
                    m = jnp.maximum(m, sk)
                p = jnp.exp(s - m)
                l = jnp.sum(p, axis=-1, keepdims=True)
                if use_sink:
                    l = l + jnp.exp(sk - m)
                vm = jnp.where(msk, vv, jnp.zeros_like(vv))
                oe = jnp.dot(p.astype(BF16), vm, preferred_element_type=F32) * (1.0 / l)
                ls = m + jnp.log(l)
                if o_pair is None:
                    o_pair = oe
                    lse_pair = jnp.broadcast_to(ls, (BLK, PAIR))
                else:
                    o_pair = o_pair + oe
                    lse_pair = jnp.where(lo, lse_pair, ls)
            o_ref[rows, sl] = o_pair.astype(BF16)
            lse_ref[rows, sl] = lse_pair

        items = [(j, hp) for j in range(sub) for hp in range(4)]
        queue = [scores(*it) for it in items[:AHEAD]]
        for n in range(len(items)):
            if n + AHEAD < len(items):
                queue.append(scores(*items[n + AHEAD]))
            outputs(*queue.pop(0))

    q_spec, kp_spec, kc_spec, b_spec = _attn_specs(kvw, sub)
    return pl.pallas_call(
        body, name=name, grid=(dil, ns),
        in_specs=[pl.BlockSpec(memory_space=pltpu.SMEM), q_spec, kp_spec, kc_spec, kp_spec, kc_spec, b_spec],
        out_specs=[q_spec, q_spec],
        out_shape=[jax.ShapeDtypeStruct((L, dil * WIDTH), BF16), jax.ShapeDtypeStruct((L, dil * WIDTH), F32)],
        compiler_params=_params(("arbitrary", "arbitrary")),
    )(sinks, q, k, k, v, v, bias)


def _attn_bwd(q, k, v, o, do, lse, bias, sinks, *, dil, kv_pairs, use_sink, name, max_sub=MAX_SUB):
    L = q.shape[0]
    sub = min(max_sub, L // BLK)
    ns = L // (sub * BLK)
    n_steps = dil * ns
    kvw = kv_pairs * PAIR
    rep = 4 // kv_pairs
    last = slice((sub - 1) * BLK, sub * BLK)

    def body(sink_ref, q_ref, kp_ref, kc_ref, vp_ref, vc_ref, o_ref, do_ref, lse_ref, b_ref,
             dq_ref, dk_ref, dv_ref, dsum_ref, dsk_ref, pk_ref, pv_ref):
        t = pl.program_id(0)
        i = t % ns

        @pl.when(t == 0)
        def _():
            dsum_ref[...] = jnp.zeros_like(dsum_ref)
            dsk_ref[...] = jnp.zeros_like(dsk_ref)
            pk_ref[...] = jnp.zeros_like(pk_ref)
            pv_ref[...] = jnp.zeros_like(pv_ref)

        @pl.when(t < n_steps)
        def _():
            lo = lax.broadcasted_iota(jnp.int32, (1, PAIR), 1) < HEAD_DIM
            first = jnp.where(i == 0, 1, 0)
            dks = [[None] * kv_pairs for _ in range(sub)]
            dvs = [[None] * kv_pairs for _ in range(sub)]
            def scores(j, hp):
                rows = slice(j * BLK, (j + 1) * BLK)
                kvp = hp // rep
                sl = slice(hp * PAIR, (hp + 1) * PAIR)
                ksl = slice(kvp * PAIR, (kvp + 1) * PAIR)
                qp = q_ref[rows, sl]
                dop = do_ref[rows, sl]
                prod = dop.astype(F32) * o_ref[rows, sl].astype(F32)
                kk = _window(kp_ref, kc_ref, j, ksl)
                vv = _window(vp_ref, vc_ref, j, ksl)
                heads = []
                for e in range(2):
                    h = 2 * hp + e
                    msk = lo if e == 0 else jnp.logical_not(lo)
                    qm = jnp.where(msk, qp, jnp.zeros_like(qp))
                    dom = jnp.where(msk, dop, jnp.zeros_like(dop))
                    km = jnp.where(msk, kk, jnp.zeros_like(kk))
                    s = (lax.dot_general(qm, kk, NT, preferred_element_type=F32)
                         + (b_ref[first, h] if j == 0 else b_ref[0, h]))
                    dp = lax.dot_general(dom, vv, NT, preferred_element_type=F32)
                    heads.append((h, msk, qm, dom, km, s, dp))
                return j, rows, kvp, sl, prod, heads

            def grads(j, rows, kvp, sl, prod, heads):
                dq_pair = None
                c_pair = None
                qms, doms, dsbs, pbs = [], [], [], []
                for h, msk, qm, dom, km, s, dp in heads:
                    ls = lse_ref[rows, h * HEAD_DIM:h * HEAD_DIM + 1]
                    p = jnp.exp(s - ls)
                    delta = jnp.sum(jnp.where(msk, prod, 0.0), axis=-1, keepdims=True)
                    ds = p * (dp - delta)
                    if use_sink:
                        ce = jnp.exp(sink_ref[h] - ls) * delta
                        c_pair = jnp.broadcast_to(ce, (BLK, PAIR)) if c_pair is None else jnp.where(msk, ce, c_pair)
                    else:
                        dsum_ref[h] += ds
                    dsb = ds.astype(BF16)
                    dqe = jnp.dot(dsb, km, preferred_element_type=F32)
                    dq_pair = dqe if dq_pair is None else dq_pair + dqe
                    qms.append(qm)
                    doms.append(dom)
                    dsbs.append(dsb)
                    pbs.append(p.astype(BF16))
                dke = lax.dot_general(jnp.concatenate(dsbs, axis=0), jnp.concatenate(qms, axis=0), TN,
                                      preferred_element_type=F32)
                dve = lax.dot_general(jnp.concatenate(pbs, axis=0), jnp.concatenate(doms, axis=0), TN,
                                      preferred_element_type=F32)
                dks[j][kvp] = dke if dks[j][kvp] is None else dks[j][kvp] + dke
                dvs[j][kvp] = dve if dvs[j][kvp] is None else dvs[j][kvp] + dve
                dq_ref[rows, sl] = (dq_pair * Q_SCALE).astype(BF16)
                if use_sink:
                    dsk_ref[:, sl] += c_pair

            items = [(j, hp) for j in range(sub) for hp in range(4)]
            ahead = AHEAD + 1 if use_sink else AHEAD
            queue = [scores(*it) for it in items[:ahead]]
            for n in range(len(items)):
                if n + ahead < len(items):
                    queue.append(scores(*items[n + ahead]))
                grads(*queue.pop(0))
            for kvp in range(kv_pairs):
                ksl = slice(kvp * PAIR, (kvp + 1) * PAIR)
                for pend_ref, out_ref, parts in ((pk_ref, dk_ref, [d[kvp] for d in dks]),
                                                 (pv_ref, dv_ref, [d[kvp] for d in dvs])):
                    if sub > 1:
                        out_ref[:(sub - 1) * BLK, ksl] = pend_ref[:(sub - 1) * BLK, ksl].astype(BF16)
                    out_ref[last, ksl] = (pend_ref[last, ksl] + parts[0][:BLK]).astype(BF16)
                    for j in range(sub):
                        own = parts[j][BLK:]
                        pend_ref[j * BLK:(j + 1) * BLK, ksl] = own + parts[j + 1][:BLK] if j + 1 < sub else own

        @pl.when(t == n_steps)
        def _():
            dk_ref[...] = pk_ref[...].astype(BF16)
            dv_ref[...] = pv_ref[...].astype(BF16)

    def at(t):
        t = jnp.minimum(t, n_steps - 1)
        return t % ns, t // ns

    def before(t):
        return at(jnp.maximum(t - 1, 0))

    q_spec = pl.BlockSpec((sub * BLK, WIDTH), at)
    kc_spec = pl.BlockSpec((sub * BLK, kvw), at)
    kp_spec = pl.BlockSpec((BLK, kvw), lambda t: (jnp.maximum(sub * at(t)[0] - 1, 0), at(t)[1]))
    b_spec = pl.BlockSpec((2, N_HEADS, BLK, 2 * BLK), lambda t: (0, 0, 0, 0))
    dkv_spec = pl.BlockSpec((sub * BLK, kvw), before)
    return pl.pallas_call(
        body, name=name, grid=(n_steps + 1,),
        in_specs=[pl.BlockSpec(memory_space=pltpu.SMEM), q_spec, kp_spec, kc_spec, kp_spec, kc_spec,
                  q_spec, q_spec, q_spec, b_spec],
        out_specs=[q_spec, dkv_spec, dkv_spec,
                   pl.BlockSpec((N_HEADS, BLK, 2 * BLK), lambda t: (0, 0, 0)),
                   pl.BlockSpec((BLK, WIDTH), lambda t: (0, 0))],
        out_shape=[jax.ShapeDtypeStruct((L, dil * WIDTH), BF16),
                   jax.ShapeDtypeStruct((L, dil * kvw), BF16),
                   jax.ShapeDtypeStruct((L, dil * kvw), BF16),
                   jax.ShapeDtypeStruct((N_HEADS, BLK, 2 * BLK), F32),
                   jax.ShapeDtypeStruct((BLK, WIDTH), F32)],
        scratch_shapes=[pltpu.VMEM((sub * BLK, kvw), F32), pltpu.VMEM((sub * BLK, kvw), F32)],
        compiler_params=_params(("arbitrary",)),
    )(sinks, q, k, k, v, v, o, do, lse, bias)


def _merge_wo(x, oa, o1, o2, o3, l1, l2, l3, ga, gb, wo, gf, *, tm=512):
    T = x.shape[0]

    def body(x_ref, oa_ref, o1_ref, o2_ref, o3_ref, l1_ref, l2_ref, l3_ref, ga_ref, gb_ref, wo_ref, gf_ref,
             x2_ref, mix_ref, h2_ref, ob1_ref, ob4_ref, ob16_ref, ls1_ref, ls4_ref, ls16_ref, so2, so3, sl2, sl3):
        _unstride(o2_ref, so2, BRANCHES[1][1], tm)
        _unstride(o3_ref, so3, BRANCHES[2][1], tm)
        _unstride(l2_ref, sl2, BRANCHES[1][1], tm)
        _unstride(l3_ref, sl3, BRANCHES[2][1], tm)
        la, lb, lc = l1_ref[...], _scr_get(sl2), _scr_get(sl3)
        m = jnp.maximum(jnp.maximum(la, lb), lc)
        ea, eb, ec = jnp.exp(la - m), jnp.exp(lb - m), jnp.exp(lc - m)
        den = ea + eb + ec
        inv = 1.0 / den
        ob = (ea * o1_ref[...].astype(F32) + eb * _scr_get(so2) + ec * _scr_get(so3)) * inv
        _scr_put(so2, ob)
        _scr_put(sl2, m + jnp.log(den))
        for (_, dil), o_ref, l_ref in zip(BRANCHES, (ob1_ref, ob4_ref, ob16_ref), (ls1_ref, ls4_ref, ls16_ref)):
            _restride(so2, o_ref, dil, tm)
            _restride(sl2, l_ref, dil, tm)
        oav = oa_ref[...].astype(F32)
        ra = lax.rsqrt(jnp.mean(oav * oav, axis=-1, keepdims=True) + EPS)
        rb = lax.rsqrt(jnp.mean(ob * ob, axis=-1, keepdims=True) + EPS)
        mix_ref[:, :WIDTH] = (oav * ra * ga_ref[...]).astype(BF16)
        mix_ref[:, WIDTH:] = (ob * rb * gb_ref[...]).astype(BF16)
        x2 = x_ref[...] + jnp.dot(mix_ref[...], wo_ref[...], preferred_element_type=F32)
        x2_ref[...] = x2
        r2 = lax.rsqrt(jnp.mean(x2 * x2, axis=-1, keepdims=True) + EPS)
        h2_ref[...] = (x2 * r2 * gf_ref[...]).astype(BF16)

    row = lambda w_: pl.BlockSpec((tm, w_), lambda i: (i, 0))
    full = lambda a: pl.BlockSpec(a.shape, lambda i: (0, 0))
    return pl.pallas_call(
        body, name="merge_wo", grid=(T // tm,),
        in_specs=[row(D_MODEL), row(WIDTH)] + _view_specs(tm) * 2 + [full(ga), full(gb), full(wo), full(gf)],
        out_specs=[row(D_MODEL), row(D_MODEL), row(D_MODEL)] + _view_specs(tm) * 2,
        out_shape=[jax.ShapeDtypeStruct((T, D_MODEL), F32), jax.ShapeDtypeStruct((T, D_MODEL), BF16),
                   jax.ShapeDtypeStruct((T, D_MODEL), BF16)] + _view_shapes(T, BF16) + _view_shapes(T, F32),
        scratch_shapes=[_scr(tm)] * 4,
        compiler_params=_params(("arbitrary",)),
    )(x, oa, o1, o2, o3, l1, l2, l3, ga, gb, wo, gf)


def _ffn_up(h2, wgt, wut, *, tm=512, fc=D_FF, rc=512, cc=256):
    T = h2.shape[0]

    def body(h_ref, wg_ref, wu_ref, gate_ref, up_ref, act_ref):
        for s in range(0, tm, rc):
            h = h_ref[s:s + rc, :]
            for c in range(0, fc, cc):
                gt = lax.dot_general(h, wg_ref[c:c + cc, :], NT, preferred_element_type=F32)
                u = lax.dot_general(h, wu_ref[c:c + cc, :], NT, preferred_element_type=F32)
                gate_ref[s:s + rc, c:c + cc] = gt.astype(BF16)
                up_ref[s:s + rc, c:c + cc] = u.astype(BF16)
                act_ref[s:s + rc, c:c + cc] = (gt * _sigmoid(gt) * u).astype(BF16)

    rowd = pl.BlockSpec((tm, D_MODEL), lambda i, c: (i, 0))
    wrow = pl.BlockSpec((fc, D_MODEL), lambda i, c: (c, 0))
    oc = pl.BlockSpec((tm, fc), lambda i, c: (i, c))
    return pl.pallas_call(
        body, name="ffn_up", grid=(T // tm, D_FF // fc),
        in_specs=[rowd, wrow, wrow],
        out_specs=[oc, oc, oc],
        out_shape=[jax.ShapeDtypeStruct((T, D_FF), BF16)] * 3,
        compiler_params=_params(("arbitrary", "arbitrary")),
    )(h2, wgt, wut)


def _ffn_down_loss(act, wd, x2, tgt, g, *, tm=1024, rc=256):
    T = x2.shape[0]

    def body(act_ref, wd_ref, x2_ref, tgt_ref, g_ref, dx_ref, dxb_ref, loss_ref, dg_ref):
        @pl.when(pl.program_id(0) == 0)
        def _():
            loss_ref[...] = jnp.zeros_like(loss_ref)
            dg_ref[...] = jnp.zeros_like(dg_ref)

        gv = g_ref[...]
        lsum = jnp.zeros((1, 1), F32)
        dgs = jnp.zeros((1, D_MODEL), F32)
        for c in range(0, tm, rc):
            x3 = x2_ref[c:c + rc, :] + jnp.dot(act_ref[c:c + rc, :], wd_ref[...], preferred_element_type=F32)
            r = lax.rsqrt(jnp.mean(x3 * x3, axis=-1, keepdims=True) + EPS)
            xh = x3 * r
            diff = xh * gv - tgt_ref[c:c + rc, :]
            lsum = lsum + jnp.sum(jnp.sum(diff * diff, axis=-1, keepdims=True), axis=0, keepdims=True)
            dy = diff * (1.0 / D_MODEL)
            dgs = dgs + jnp.sum(dy * xh, axis=0, keepdims=True)
            dx = _rms_bwd(dy, xh, r, gv)
            dx_ref[c:c + rc, :] = dx
            dxb_ref[c:c + rc, :] = dx.astype(BF16)
        loss_ref[...] += lsum * (0.5 / D_MODEL)
        dg_ref[...] += dgs

    rowd = pl.BlockSpec((tm, D_MODEL), lambda i: (i, 0))
    return pl.pallas_call(
        body, name="ffn_down_loss", grid=(T // tm,),
        in_specs=[pl.BlockSpec((tm, D_FF), lambda i: (i, 0)),
                  pl.BlockSpec((D_FF, D_MODEL), lambda i: (0, 0), pipeline_mode=pl.Buffered(1)),
                  rowd, rowd, pl.BlockSpec(g.shape, lambda i: (0, 0))],
        out_specs=[rowd, rowd, pl.BlockSpec((1, 1), lambda i: (0, 0)), pl.BlockSpec((1, D_MODEL), lambda i: (0, 0))],
        out_shape=[jax.ShapeDtypeStruct((T, D_MODEL), F32), jax.ShapeDtypeStruct((T, D_MODEL), BF16),
                   jax.ShapeDtypeStruct((1, 1), F32), jax.ShapeDtypeStruct((1, D_MODEL), F32)],
        compiler_params=_params(("arbitrary",)),
    )(act, wd, x2, tgt, g)


def _ffn_bwd(dx3, gate, up, wd, wgt, wut, x2, g, *, tm=256, cc=256):
    T = x2.shape[0]

    def body(dx_ref, gate_ref, up_ref, wd_ref, wg_ref, wu_ref, x2_ref, g_ref,
             dgate_ref, dup_ref, dx2_ref, dx2b_ref, dg_ref):
        @pl.when(pl.program_id(0) == 0)
        def _():
            dg_ref[...] = jnp.zeros_like(dg_ref)

        dxb = dx_ref[...].astype(BF16)
        for c in range(0, D_FF, cc):
            dact = lax.dot_general(dxb, wd_ref[c:c + cc, :], NT, preferred_element_type=F32)
            gt = gate_ref[:, c:c + cc].astype(F32)
            u = up_ref[:, c:c + cc].astype(F32)
            sg = _sigmoid(gt)
            a = dact * sg
            dgate_ref[:, c:c + cc] = (a * u * ((1.0 + gt) - gt * sg)).astype(BF16)
            dup_ref[:, c:c + cc] = (a * gt).astype(BF16)
        dh = (jnp.dot(dgate_ref[...], wg_ref[...], preferred_element_type=F32)
              + jnp.dot(dup_ref[...], wu_ref[...], preferred_element_type=F32))
        xv = x2_ref[...]
        r = lax.rsqrt(jnp.mean(xv * xv, axis=-1, keepdims=True) + EPS)
        xh = xv * r
        dg_ref[...] += jnp.sum(dh * xh, axis=0, keepdims=True)
        d = dx_ref[...] + _rms_bwd(dh, xh, r, g_ref[...])
        dx2_ref[...] = d
        dx2b_ref[...] = d.astype(BF16)

    rowd = pl.BlockSpec((tm, D_MODEL), lambda i: (i, 0))
    rowf = pl.BlockSpec((tm, D_FF), lambda i: (i, 0))
    wfull = pl.BlockSpec((D_FF, D_MODEL), lambda i: (0, 0), pipeline_mode=pl.Buffered(1))
    return pl.pallas_call(
        body, name="ffn_bwd", grid=(T // tm,),
        in_specs=[rowd, rowf, rowf, wfull, wfull, wfull, rowd, pl.BlockSpec(g.shape, lambda i: (0, 0))],
        out_specs=[rowf, rowf, rowd, rowd, pl.BlockSpec((1, D_MODEL), lambda i: (0, 0))],
        out_shape=[jax.ShapeDtypeStruct((T, D_FF), BF16), jax.ShapeDtypeStruct((T, D_FF), BF16),
                   jax.ShapeDtypeStruct((T, D_MODEL), F32), jax.ShapeDtypeStruct((T, D_MODEL), BF16),
                   jax.ShapeDtypeStruct((1, D_MODEL), F32)],
        compiler_params=_params(("arbitrary",)),
    )(dx3, gate, up, wd, wgt, wut, x2, g)


def _matmul_tn(a, b, *, tk, tn, tt=2048, out_dtype=BF16, name):
    T, K = a.shape
    N = b.shape[1]
    nt = T // tt

    def body(a_ref, b_ref, o_ref, acc_ref):
        part = lax.dot_general(a_ref[...], b_ref[...], TN, preferred_element_type=F32)

        @pl.when(pl.program_id(2) == 0)
        def _():
            acc_ref[...] = part

        @pl.when(pl.program_id(2) > 0)
        def _():
            acc_ref[...] += part

        @pl.when(pl.program_id(2) == nt - 1)
        def _():
            o_ref[...] = acc_ref[...].astype(out_dtype)

    return pl.pallas_call(
        body, name=name, grid=(K // tk, N // tn, nt),
        in_specs=[pl.BlockSpec((tt, tk), lambda i, j, t: (t, i)), pl.BlockSpec((tt, tn), lambda i, j, t: (t, j))],
        out_specs=pl.BlockSpec((tk, tn), lambda i, j, t: (i, j)),
        out_shape=jax.ShapeDtypeStruct((K, N), out_dtype),
        scratch_shapes=[pltpu.VMEM((tk, tn), F32)],
        compiler_params=_params(("arbitrary", "arbitrary", "arbitrary")),
    )(a, b)


def _wo_bwd(dx2b, wo, oa, ob, ga, gb, mixed, *, tm=512, wc=256):
    T = dx2b.shape[0]
    n_tiles = T // tm

    def body(dx_ref, wo_ref, oa_ref, ob_ref, ga_ref, gb_ref, mix_ref,
             doa_ref, dob1_ref, dob4_ref, dob16_ref, dga_ref, dgb_ref, dwo_ref, scr, dw_acc):
        @pl.when(pl.program_id(0) == 0)
        def _():
            dga_ref[...] = jnp.zeros_like(dga_ref)
            dgb_ref[...] = jnp.zeros_like(dgb_ref)
            dw_acc[...] = jnp.zeros_like(dw_acc)

        dxv = dx_ref[...]
        for c in range(0, D_MODEL, wc):
            dw_acc[c:c + wc, :] += lax.dot_general(mix_ref[:, c:c + wc], dxv, TN, preferred_element_type=F32)

        @pl.when(pl.program_id(0) == n_tiles - 1)
        def _():
            dwo_ref[...] = dw_acc[...].astype(BF16)

        dm = lax.dot_general(dxv, wo_ref[...], NT, preferred_element_type=F32)
        for o_ref, g_ref, dg_ref, sl in ((oa_ref, ga_ref, dga_ref, slice(0, WIDTH)),
                                         (ob_ref, gb_ref, dgb_ref, slice(WIDTH, 2 * WIDTH))):
            ov = o_ref[...].astype(F32)
            r = lax.rsqrt(jnp.mean(ov * ov, axis=-1, keepdims=True) + EPS)
            xh = ov * r
            d = dm[:, sl]
            dg_ref[...] += jnp.sum(d * xh, axis=0, keepdims=True)
            do = _rms_bwd(d, xh, r, g_ref[...])
            if o_ref is oa_ref:
                doa_ref[...] = do.astype(BF16)
            else:
                _scr_put(scr, do)
                for (_, dil), v_ref in zip(BRANCHES, (dob1_ref, dob4_ref, dob16_ref)):
                    _restride(scr, v_ref, dil, tm)

    row = lambda w_: pl.BlockSpec((tm, w_), lambda i: (i, 0))
    full = lambda a: pl.BlockSpec(a.shape, lambda i: (0, 0))
    return pl.pallas_call(
        body, name="wo_bwd", grid=(T // tm,),
        in_specs=[row(D_MODEL), full(wo), row(WIDTH), row(WIDTH), full(ga), full(gb), row(D_MODEL)],
        out_specs=[row(WIDTH)] + _view_specs(tm)
        + [pl.BlockSpec((1, WIDTH), lambda i: (0, 0)), pl.BlockSpec((1, WIDTH), lambda i: (0, 0)), full(wo)],
        out_shape=[jax.ShapeDtypeStruct((T, WIDTH), BF16)] + _view_shapes(T, BF16)
        + [jax.ShapeDtypeStruct((1, WIDTH), F32), jax.ShapeDtypeStruct((1, WIDTH), F32),
           jax.ShapeDtypeStruct((D_MODEL, D_MODEL), BF16)],
        scratch_shapes=[_scr(tm), pltpu.VMEM((D_MODEL, D_MODEL), F32)],
        compiler_params=_params(("arbitrary",)),
    )(dx2b, wo, oa, ob, ga, gb, mixed)


def _inproj_bwd(dqa, dka, dva, dqs, dks, dvs, cos, sin, w, x, dx2, g, h1, *, tm=512, wc=256):
    T = dqa.shape[0]
    n_tiles = T // tm

    def body(dqa_ref, dka_ref, dva_ref, q1, q2, q3, k1, k2, k3, v1, v2, v3, cos_ref, sin_ref, w_ref, x_ref, dx2_ref,
             g_ref, h1_ref, db_ref, gx_ref, dg_ref, dw_ref, dp_ref, dw_acc, acc, tmp):
        @pl.when(pl.program_id(0) == 0)
        def _():
            db_ref[...] = jnp.zeros_like(db_ref)
            dg_ref[...] = jnp.zeros_like(dg_ref)
            dw_acc[...] = jnp.zeros_like(dw_acc)

        cosv = cos_ref[...]
        sinv = sin_ref[...]
        lane = lax.broadcasted_iota(jnp.int32, (tm, PAIR), 1)
        first = (lane % HEAD_DIM) < (HEAD_DIM // 2)

        def put(off, val):
            dp_ref[:, off:off + PAIR] = val.astype(BF16)
            db_ref[:, off:off + PAIR] += jnp.sum(val, axis=0, keepdims=True)

        for src, off, width in ((dqa_ref, 0, 512), (dka_ref, 512, 256)):
            for j in range(0, width, PAIR):
                d = src[:, j:j + PAIR].astype(F32)
                put(off + j, d * cosv - _rope_rot(d, first) * sinv)
        for j in range(0, 256, PAIR):
            put(768 + j, dva_ref[:, j:j + PAIR].astype(F32))
        for (a, b, c), off in (((q1, q2, q3), 1024), ((k1, k2, k3), 1536), ((v1, v2, v3), 2048)):
            _unstride(b, acc, BRANCHES[1][1], tm)
            _unstride(c, tmp, BRANCHES[2][1], tm)
            for j in range(N_CHUNK):
                put(off + j * PAIR, a[:, j * PAIR:(j + 1) * PAIR].astype(F32) + acc[j] + tmp[j])

        dh = jnp.dot(dp_ref[...], w_ref[...], preferred_element_type=F32)
        xv = x_ref[...]
        r = lax.rsqrt(jnp.mean(xv * xv, axis=-1, keepdims=True) + EPS)
        xh = xv * r
        dg_ref[...] += jnp.sum(dh * xh, axis=0, keepdims=True)
        gx_ref[...] = dx2_ref[...] + _rms_bwd(dh, xh, r, g_ref[...])

        h1v = h1_ref[...]
        for c in range(0, D_INP, wc):
            dw_acc[c:c + wc, :] += lax.dot_general(dp_ref[:, c:c + wc], h1v, TN, preferred_element_type=F32)

        @pl.when(pl.program_id(0) == n_tiles - 1)
        def _():
            pltpu.sync_copy(dw_acc, dw_ref)

    row = lambda w_: pl.BlockSpec((tm, w_), lambda i: (i, 0))
    full = lambda a: pl.BlockSpec(a.shape, lambda i: (0, 0))
    return pl.pallas_call(
        body, name="inproj_bwd", grid=(n_tiles,),
        in_specs=[row(512), row(256), row(256)] + _view_specs(tm) * 3 + [row(PAIR), row(PAIR)]
        + [pl.BlockSpec(w.shape, lambda i: (0, 0), pipeline_mode=pl.Buffered(1)), row(D_MODEL), row(D_MODEL), full(g),
           row(D_MODEL)],
        out_specs=[pl.BlockSpec((1, D_INP), lambda i: (0, 0)), row(D_MODEL),
                   pl.BlockSpec((1, D_MODEL), lambda i: (0, 0)), pl.BlockSpec(memory_space=pl.ANY)],
        out_shape=[jax.ShapeDtypeStruct((1, D_INP), F32), jax.ShapeDtypeStruct((T, D_MODEL), F32),
                   jax.ShapeDtypeStruct((1, D_MODEL), F32), jax.ShapeDtypeStruct((D_INP, D_MODEL), F32)],
        scratch_shapes=[pltpu.VMEM((tm, D_INP), BF16), pltpu.VMEM((D_INP, D_MODEL), F32), _scr(tm), _scr(tm)],
        compiler_params=_params(("arbitrary",)),
    )(dqa, dka, dva, *dqs, *dks, *dvs, cos, sin, w, x, dx2, g, h1)


def _bias_sink_grads(dsums, bmaps, dsk):
    def body(s1, s2, s3, m1, m2, m3, dsk_ref, drel_ref, dsink_ref):
        row = lax.broadcasted_iota(jnp.int32, (N_HEADS, 128), 0)
        lane = lax.broadcasted_iota(jnp.int32, (N_HEADS, 128), 1)
        out = jnp.zeros((N_HEADS, 128), F32)
        for s_ref, m_ref in ((s1, m1), (s2, m2), (s3, m3)):
            bm = m_ref[...]
            for h in range(N_HEADS):
                a = s_ref[h]
                for b in range(REL_BUCKETS):
                    v = jnp.sum(jnp.sum(jnp.where(bm == b, a, 0.0), axis=-1, keepdims=True), axis=0, keepdims=True)
                    out = out + jnp.where((row == h) & (lane == b), v, 0.0)
        drel_ref[...] = out
        dsink_ref[...] = -jnp.sum(dsk_ref[...], axis=0, keepdims=True)

    vm = pl.BlockSpec(memory_space=pltpu.VMEM)
    return pl.pallas_call(
        body, name="bias_sink_grads",
        in_specs=[vm] * 7, out_specs=[vm, vm],
        out_shape=[jax.ShapeDtypeStruct((N_HEADS, 128), F32), jax.ShapeDtypeStruct((1, WIDTH), F32)],
        compiler_params=_params(),
    )(*dsums, *bmaps, dsk)


def _all_gather(blk, *, name):
    R, C = blk.shape

    def body(x_ref, out_ref, send_sems, recv_sems, local_sem):
        x, y, c = lax.axis_index("x"), lax.axis_index("y"), lax.axis_index("c")
        me, sibling = (x, y, c), (x, y, 1 - c)
        chips = [(1 - x, y), (x, 1 - y), (1 - x, 1 - y)]

        def slot(px, py, pc):
            return out_ref.at[4 * px + 2 * py + pc]

        def copy(k, block, to, src=None):
            return pltpu.make_async_remote_copy(
                src_ref=slot(*block) if src is None else src, dst_ref=slot(*block),
                send_sem=send_sems.at[k], recv_sem=recv_sems.at[k], device_id=to, device_id_type=MESH)

        mine = pltpu.make_async_copy(x_ref, slot(*me), local_sem)
        mine.start()
        first = [copy(0, me, sibling, src=x_ref)]
        first += [copy(1 + j, me, (*chip, c), src=x_ref) for j, chip in enumerate(chips)]
        for cp in first:
            cp.start()
        passed = [copy(4 + j, (*chip, c), sibling) for j, chip in enumerate(chips)]
        for j, chip in enumerate(chips):
            copy(1 + j, (*chip, c), me).wait_recv()
            passed[j].start()
        copy(0, sibling, me).wait_recv()
        for j, chip in enumerate(chips):
            copy(4 + j, (*chip, 1 - c), me).wait_recv()
        for cp in first + passed:
            cp.wait_send()
        mine.wait()

    return pl.pallas_call(
        body, name=name,
        in_specs=[pl.BlockSpec(memory_space=pl.ANY)], out_specs=pl.BlockSpec(memory_space=pl.ANY),
        out_shape=jax.ShapeDtypeStruct((N_DEV, R, C), blk.dtype),
        scratch_shapes=[pltpu.SemaphoreType.DMA((7,)), pltpu.SemaphoreType.DMA((7,)), pltpu.SemaphoreType.DMA],
        compiler_params=pltpu.CompilerParams(has_side_effects=True),
    )(blk)


def _peers(x, y, c):
    return [(x ^ (k >> 2), y ^ ((k >> 1) & 1), c ^ (k & 1)) for k in range(1, N_DEV)]


_HBM = pl.BlockSpec(memory_space=pltpu.HBM)
_SEM = pl.BlockSpec(memory_space=pltpu.SEMAPHORE)
_EFFECT = pltpu.SideEffectType.DATAFLOW_SIDE_EFFECTING


def _peer_list(x, y, c, near):
    if near:
        return [(x, y, 1 - c), (1 - x, y, c), (x, 1 - y, c), (1 - x, 1 - y, c)]
    return _peers(x, y, c)


def _exchange_start(srcs, *, gather, name, near=False):
    n = len(srcs)
    n_peers = 4 if near else N_DEV - 1
    lands = [lax.empty((N_DEV,) + s.shape[-2:], s.dtype) for s in srcs]

    def body(*refs):
        src_refs, land_refs = refs[:n], refs[n:2 * n]
        send_sems, recv_sems = refs[2 * n], refs[2 * n + 1]
        token = refs[-1]
        x, y, c = lax.axis_index("x"), lax.axis_index("y"), lax.axis_index("c")
        mine = 4 * x + 2 * y + c
        for a in range(n):
            for k, peer in enumerate(_peer_list(x, y, c, near)):
                dest = 4 * peer[0] + 2 * peer[1] + peer[2]
                j = a * n_peers + k
                pltpu.make_async_remote_copy(
                    src_ref=src_refs[a] if gather else src_refs[a].at[dest], dst_ref=land_refs[a].at[mine],
                    send_sem=send_sems.at[j], recv_sem=recv_sems.at[j], device_id=peer, device_id_type=MESH).start()
        token[...] = jnp.zeros_like(token)

    sems = pltpu.SemaphoreType.DMA((n * n_peers,))
    out = pl.pallas_call(
        body, name=name,
        out_shape=(sems, sems) + tuple(pltpu.HBM(a.shape, a.dtype) for a in list(srcs) + lands)
        + (jax.ShapeDtypeStruct((8, 128), F32),),
        in_specs=(_HBM,) * (2 * n), out_specs=(_SEM, _SEM) + (_HBM,) * (2 * n) + (pl.BlockSpec(memory_space=pltpu.VMEM),),
        input_output_aliases={i: 2 + i for i in range(2 * n)},
        compiler_params=pltpu.CompilerParams(has_side_effects=_EFFECT),
    )(*[pltpu.with_memory_space_constraint(a, pltpu.HBM) for a in list(srcs) + lands])
    return out[:-1], out[-1]


def _exchange_wait(state, after, *, gather, name, near=False):
    send_sems, recv_sems = state[0], state[1]
    n = (len(state) - 2) // 2
    n_peers = 4 if near else N_DEV - 1
    arrays = state[2:]

    def body(*refs):
        src_refs, land_refs = refs[:n], refs[n:2 * n]
        send_sems, recv_sems = refs[2 * n], refs[2 * n + 1]
        x, y, c = lax.axis_index("x"), lax.axis_index("y"), lax.axis_index("c")
        for a in range(n):
            for k, peer in enumerate(_peer_list(x, y, c, near)):
                other = 4 * peer[0] + 2 * peer[1] + peer[2]
                j = a * n_peers + k
                copy = pltpu.make_async_remote_copy(
                    src_ref=src_refs[a] if gather else src_refs[a].at[other], dst_ref=land_refs[a].at[other],
                    send_sem=send_sems.at[j], recv_sem=recv_sems.at[j], device_id=peer, device_id_type=MESH)
                copy.wait_send()
                copy.wait_recv()

    out = pl.pallas_call(
        body, name=name,
        out_shape=tuple(pltpu.HBM(a.shape, a.dtype) for a in arrays),
        in_specs=(_HBM,) * (2 * n) + (_SEM, _SEM, pl.BlockSpec(memory_space=pl.ANY)), out_specs=(_HBM,) * (2 * n),
        input_output_aliases={i: i for i in range(2 * n)},
        compiler_params=pltpu.CompilerParams(has_side_effects=_EFFECT),
    )(*arrays, send_sems, recv_sems, after)
    mine = 4 * lax.axis_index("x") + 2 * lax.axis_index("y") + lax.axis_index("c")
    own = out[:n] if gather else [lax.dynamic_index_in_dim(s, mine, 0, keepdims=False) for s in out[:n]]
    return [lax.dynamic_update_slice(g, o[None], (mine, 0, 0)) for g, o in zip(out[n:], own)]


def _forward_start(lands, *, name):
    n = len(lands)

    def body(*refs):
        land_refs, send_sems, recv_sems, token = refs[:n], refs[n], refs[n + 1], refs[-1]
        x, y, c = lax.axis_index("x"), lax.axis_index("y"), lax.axis_index("c")
        for a in range(n):
            for j, (px, py) in enumerate(((1 - x, y), (x, 1 - y), (1 - x, 1 - y))):
                blk = 4 * px + 2 * py + c
                pltpu.make_async_remote_copy(
                    src_ref=land_refs[a].at[blk], dst_ref=land_refs[a].at[blk], send_sem=send_sems.at[3 * a + j],
                    recv_sem=recv_sems.at[3 * a + j], device_id=(x, y, 1 - c), device_id_type=MESH).start()
        token[...] = jnp.zeros_like(token)

    sems = pltpu.SemaphoreType.DMA((3 * n,))
    out = pl.pallas_call(
        body, name=name,
        out_shape=(sems, sems) + tuple(pltpu.HBM(a.shape, a.dtype) for a in lands) + (jax.ShapeDtypeStruct((8, 128), F32),),
        in_specs=(_HBM,) * n, out_specs=(_SEM, _SEM) + (_HBM,) * n + (pl.BlockSpec(memory_space=pltpu.VMEM),),
        input_output_aliases={i: 2 + i for i in range(n)},
        compiler_params=pltpu.CompilerParams(has_side_effects=_EFFECT),
    )(*[pltpu.with_memory_space_constraint(a, pltpu.HBM) for a in lands])
    return out[:-1], out[-1]


def _forward_wait(state, after, *, name):
    send_sems, recv_sems = state[0], state[1]
    lands = state[2:]
    n = len(lands)

    def body(*refs):
        land_refs, send_sems, recv_sems = refs[:n], refs[n], refs[n + 1]
        x, y, c = lax.axis_index("x"), lax.axis_index("y"), lax.axis_index("c")
        for a in range(n):
            for j, (px, py) in enumerate(((1 - x, y), (x, 1 - y), (1 - x, 1 - y))):
                copy = pltpu.make_async_remote_copy(
                    src_ref=land_refs[a].at[4 * px + 2 * py + c], dst_ref=land_refs[a].at[4 * px + 2 * py + 1 - c],
                    send_sem=send_sems.at[3 * a + j], recv_sem=recv_sems.at[3 * a + j], device_id=(x, y, 1 - c),
                    device_id_type=MESH)
                copy.wait_send()
                copy.wait_recv()

    return pl.pallas_call(
        body, name=name,
        out_shape=tuple(pltpu.HBM(a.shape, a.dtype) for a in lands),
        in_specs=(_HBM,) * n + (_SEM, _SEM, pl.BlockSpec(memory_space=pl.ANY)), out_specs=(_HBM,) * n,
        input_output_aliases={i: i for i in range(n)},
        compiler_params=pltpu.CompilerParams(has_side_effects=_EFFECT),
    )(*lands, send_sems, recv_sems, after)


def _adam_math(w, g, m, v):
    m = ADAM_B1 * m + (1.0 - ADAM_B1) * g
    v = ADAM_B2 * v + (1.0 - ADAM_B2) * (g * g)
    m_hat = m / (1.0 - ADAM_B1 ** ADAM_STEP)
    v_hat = v / (1.0 - ADAM_B2 ** ADAM_STEP)
    delta = -ADAM_LR * (m_hat / (jnp.sqrt(v_hat) + ADAM_EPS) + ADAM_WD * w)
    return delta, m, v


def _adamw(parts, w, m, v, *, name):
    R, C = w.shape
    n_parts = parts.shape[0]
    tr = R // 2
    assert tr % 16 == 0

    def body(p_ref, w_ref, m_ref, v_ref, g_ref, d_ref, nm_ref, nv_ref):
        g = p_ref[0].astype(F32)
        for s in range(1, n_parts):
            g = g + p_ref[s].astype(F32)
        d, nm, nv = _adam_math(w_ref[...], g, m_ref[...], v_ref[...])
        g_ref[...] = g
        d_ref[...] = d
        nm_ref[...] = nm
        nv_ref[...] = nv

    blk = pl.BlockSpec((tr, C), lambda i: (i, 0))
    return pl.pallas_call(
        body, name=name, grid=(R // tr,),
        in_specs=[pl.BlockSpec((n_parts, tr, C), lambda i: (0, i, 0)), blk, blk, blk],
        out_specs=[blk] * 4, out_shape=[jax.ShapeDtypeStruct((R, C), F32)] * 4,
        compiler_params=_params(("arbitrary",)),
    )(parts, w, m, v)


def _adamw_small(parts, w, m, v):
    def body(p_ref, w_ref, m_ref, v_ref, g_ref, d_ref, nm_ref, nv_ref):
        g = p_ref[0]
        for s in range(1, N_DEV):
            g = g + p_ref[s]
        d, nm, nv = _adam_math(w_ref[...], g, m_ref[...], v_ref[...])
        g_ref[...] = g
        d_ref[...] = d
        nm_ref[...] = nm
        nv_ref[...] = nv

    vm = pl.BlockSpec(memory_space=pltpu.VMEM)
    return pl.pallas_call(
        body, name="adamw_small", in_specs=[vm] * 4, out_specs=[vm] * 4,
        out_shape=[jax.ShapeDtypeStruct((SMALL_ROWS, 128), F32)] * 4, compiler_params=_params(),
    )(parts, w, m, v)


def _t5_bucket(dist):
    max_exact = REL_BUCKETS // 2
    df = jnp.maximum(dist, 1).astype(F32)
    large = max_exact + (jnp.log(df / max_exact) / math.log(REL_MAX_DISTANCE / max_exact)
                         * (REL_BUCKETS - max_exact)).astype(jnp.int32)
    large = jnp.minimum(large, REL_BUCKETS - 1)
    return jnp.where(dist < max_exact, dist, large)


def _band_tables(rel_table, dil, n_back):
    qi = jnp.arange(BLK)[:, None]
    kj = jnp.arange(2 * BLK)[None, :]
    delta = BLK + qi - kj
    in_band = (delta >= 0) & (delta <= n_back)
    if rel_table is None:
        vals = jnp.zeros((N_HEADS, BLK, 2 * BLK), F32)
        bmap = None
    else:
        bucket = _t5_bucket(jnp.clip(delta, 0, n_back) * dil)
        vals = jnp.zeros((N_HEADS, BLK, 2 * BLK), F32)
        for b in range(REL_BUCKETS):
            vals = jnp.where((bucket == b)[None], rel_table[b][:, None, None], vals)
        bmap = jnp.where(in_band, bucket, -1).astype(jnp.int32)
    later = jnp.where(in_band[None], vals, NEG)
    first = jnp.where((in_band & (kj >= BLK))[None], vals, NEG)
    return jnp.stack([later, first]), bmap


def _rope_tables(T):
    half = HEAD_DIM // 2
    inv_freq = ROPE_THETA ** (-jnp.arange(half, dtype=F32) / half)
    ang = jnp.arange(T, dtype=F32)[:, None] * inv_freq[None, :]
    cos, sin = jnp.cos(ang), jnp.sin(ang)
    return jnp.tile(cos, (1, 4)), jnp.tile(jnp.concatenate([-sin, sin], axis=1), (1, 2))


def _widen_in(a, axis):
    sl = lambda lo, hi: lax.slice_in_dim(a, lo, hi, axis=axis)
    dup = lambda lo: [sl(lo, lo + 64), sl(lo, lo + 64), sl(lo + 64, lo + 128), sl(lo + 64, lo + 128)]
    return jnp.concatenate([sl(0, 512)] + dup(512) + dup(640) + [sl(768, D_IN)], axis=axis)


def _fold_in(a, axis):
    sl = lambda lo, hi: lax.slice_in_dim(a, lo, hi, axis=axis)
    fold = lambda lo: [sl(lo, lo + 64) + sl(lo + 64, lo + 128), sl(lo + 128, lo + 192) + sl(lo + 192, lo + 256)]
    return jnp.concatenate([sl(0, 512)] + fold(512) + fold(768) + [sl(1024, D_INP)], axis=axis)


def _local_step(x, tgt, g_attn, b_in, sinks, rel_table, g_out_a, g_out_b, g_ffn, g_final,
                win_fn, wo_fn, ffn_fn, early_fn):
    T = x.shape[0]
    cos, sin = _rope_tables(T)
    g_final2 = g_final.reshape(1, D_MODEL)
    sink8 = sinks.reshape(N_HEADS)

    bias_a, _ = _band_tables(None, 1, BLK - 1)
    tabs = [_band_tables(rel_table, dil, window // dil) for window, dil in BRANCHES]
    wint, token = win_fn(tabs[2][0])
    winp = _widen_in(wint, 0)
    binp = _widen_in(b_in, 1) + token[0, 0]

    h1, qa, ka, va, *qkv_b = _norm_proj(x, g_attn, winp, binp, cos, sin)
    qbs, kbs, vbs = qkv_b[0:3], qkv_b[3:6], qkv_b[6:9]
    oa, lse_a = _attn_fwd(qa, ka, va, bias_a, sink8, dil=1, kv_pairs=2, use_sink=True, name="attn_a_fwd")
    outs = [_attn_fwd(qbs[n], kbs[n], vbs[n], tabs[n][0], sink8, dil=dil, kv_pairs=4, use_sink=False,
                      name=f"attn_b{n}_fwd") for n, (_, dil) in enumerate(BRANCHES)]
    wo = wo_fn(outs[2][1])
    x2, mixed, h2, *ob_lse = _merge_wo(x, oa, outs[0][0], outs[1][0], outs[2][0], outs[0][1], outs[1][1], outs[2][1],
                                       g_out_a, g_out_b, wo, g_ffn)
    obs, lses = ob_lse[0:3], ob_lse[3:6]
    wgt, wut, wd = ffn_fn(h2)
    gate, up, act = _ffn_up(h2, wgt, wut)
    dx3, dx3b, loss, dg_final = _ffn_down_loss(act, wd, x2, tgt, g_final2)

    dgate, dup, dx2, dx2b, dg_ffn = _ffn_bwd(dx3, gate, up, wd, wgt, wut, x2, g_ffn)
    dwd = _matmul_tn(act, dx3b, tk=1408, tn=1024, name="dw_down")
    dwgt = _matmul_tn(dgate, h2, tk=1408, tn=1024, name="dw_gate")
    dwut = _matmul_tn(dup, h2, tk=1408, tn=1024, name="dw_up")
    doa, *dobs, dg_out_a, dg_out_b, dwo = _wo_bwd(dx2b, wo, oa, obs[0], g_out_a, g_out_b, mixed)
    early, token2 = early_fn(dict(w_o=dwo, w_gate=dwgt, w_up=dwut, w_down=dwd))
    sink8b = sink8 + token2[0, 0]

    dqa, dka, dva, _, dsk = _attn_bwd(qa, ka, va, oa, doa, lse_a, bias_a, sink8b, dil=1, kv_pairs=2, use_sink=True,
                                      name="attn_a_bwd", max_sub=4)
    res = [_attn_bwd(qbs[n], kbs[n], vbs[n], obs[n], dobs[n], lses[n], tabs[n][0], sink8b, dil=dil, kv_pairs=4,
                     use_sink=False, name=f"attn_b{n}_bwd") for n, (_, dil) in enumerate(BRANCHES)]
    dbp, grad_x, dg_attn, dwinp = _inproj_bwd(dqa, dka, dva, [r[0] for r in res], [r[1] for r in res],
                                              [r[2] for r in res], cos, sin, winp, x, dx2, g_attn, h1)
    dwin = _fold_in(dwinp, 0)
    drel, dsink = _bias_sink_grads([r[3] for r in res], [t[1] for t in tabs], dsk)

    small = dict(
        g_attn=dg_attn, b_in=_fold_in(dbp, 1), sinks=dsink[:, ::HEAD_DIM], rel_table=drel[:, :REL_BUCKETS].T,
        g_out_a=dg_out_a, g_out_b=dg_out_b, g_ffn=dg_ffn, g_final=dg_final.reshape(D_MODEL))
    return loss[0, 0], grad_x, dwin, early, small


SMALL_NAMES = ("g_attn", "b_in", "sinks", "rel_table", "g_out_a", "g_out_b", "g_ffn", "g_final", "loss")


def _pack_small(vals):
    flat = jnp.concatenate([vals[n].reshape(-1).astype(F32) for n in SMALL_NAMES])
    return jnp.pad(flat, (0, SMALL_ROWS * 128 - flat.shape[0])).reshape(SMALL_ROWS, 128)


def _unpack_small(packed, like):
    flat = packed.reshape(-1)
    out, off = {}, 0
    for n in SMALL_NAMES:
        size = like[n].size
        out[n] = flat[off:off + size].reshape(like[n].shape)
        off += size
    return out


def kernel(x, g_attn, w_in, b_in, sinks, rel_table, g_out_a, g_out_b, w_o, g_ffn, w_gate, w_up, w_down, g_final, loss_target, m_g_attn, m_w_in, m_b_in, m_sinks, m_rel_table, m_g_out_a, m_g_out_b, m_w_o, m_g_ffn, m_w_gate, m_w_up, m_w_down, m_g_final, v_g_attn, v_w_in, v_b_in, v_sinks, v_rel_table, v_g_out_a, v_g_out_b, v_w_o, v_g_ffn, v_w_gate, v_w_up, v_w_down, v_g_final):
    rest_names = ("w_o", "w_gate", "w_up", "w_down")

    rest = [w_o[0].astype(BF16), w_gate[0].astype(BF16).T, w_up[0].astype(BF16).T, w_down[0].astype(BF16)]
    in_state, _ = _exchange_start([w_in[0].astype(BF16).T], gather=True, near=True, name="gather_w_in_start")
    later = {}

    def whole(got):
        return [g.reshape(N_DEV * g.shape[1], D_MODEL) for g in got]

    def win_fn(after):
        near = _exchange_wait(in_state, after, gather=True, near=True, name="gather_w_in_near")
        fwd_state, tok = _forward_start(near, name="gather_w_in_forward")
        wint = whole(_forward_wait(fwd_state, tok, name="gather_w_in_wait"))[0]
        wint, src = lax.optimization_barrier((wint, rest))
        later["wo"], token_o = _exchange_start(src[:1], gather=True, name="gather_w_o_start")
        token_o, ffn_src = lax.optimization_barrier((token_o, src[1:]))
        later["ffn"], token = _exchange_start(ffn_src, gather=True, name="gather_ffn_start")
        return wint, token + token_o

    def wo_fn(after):
        return whole(_exchange_wait(later["wo"], after, gather=True, name="gather_w_o_wait"))[0]

    def ffn_fn(after):
        return whole(_exchange_wait(later["ffn"], after, gather=True, name="gather_ffn_wait"))

    def early_fn(dws):
        return _exchange_start([dws[n].reshape(N_DEV, -1, D_MODEL) for n in rest_names], gather=False,
                               name="scatter_rest_start")

    loss_part, grad_x, dwint, early_state, small = _local_step(
        x[0], loss_target[0], g_attn, b_in, sinks, rel_table, g_out_a, g_out_b, g_ffn, g_final,
        win_fn, wo_fn, ffn_fn, early_fn)
    parts_in = dwint.astype(BF16).reshape(N_DEV, D_IN // N_DEV, D_MODEL)
    in_state, token3 = _exchange_start([parts_in], gather=False, name="scatter_w_in_start")
    got = _exchange_wait(early_state, token3, gather=False, name="scatter_rest_wait")

    def update(n, parts, w, m, v, transposed):
        if transposed:
            return [a.T[None] for a in _adamw(parts, w[0].T, m[0].T, v[0].T, name="adamw_" + n)]
        return [a[None] for a in _adamw(parts, w[0], m[0], v[0], name="adamw_" + n)]

    big = dict(w_o=update("w_o", got[0], w_o, m_w_o, v_w_o, False),
               w_gate=update("w_gate", got[1], w_gate, m_w_gate, v_w_gate, True),
               w_up=update("w_up", got[2], w_up, m_w_up, v_w_up, True),
               w_down=update("w_down", got[3], w_down, m_w_down, v_w_down, False))

    unused = jnp.zeros((1,), F32)
    ws = dict(g_attn=g_attn, b_in=b_in, sinks=sinks, rel_table=rel_table, g_out_a=g_out_a, g_out_b=g_out_b,
              g_ffn=g_ffn, g_final=g_final, loss=unused)
    ms = dict(g_attn=m_g_attn, b_in=m_b_in, sinks=m_sinks, rel_table=m_rel_table, g_out_a=m_g_out_a,
              g_out_b=m_g_out_b, g_ffn=m_g_ffn, g_final=m_g_final, loss=unused)
    vs = dict(g_attn=v_g_attn, b_in=v_b_in, sinks=v_sinks, rel_table=v_rel_table, g_out_a=v_g_out_a,
              g_out_b=v_g_out_b, g_ffn=v_g_ffn, g_final=v_g_final, loss=unused)
    sparts = _all_gather(_pack_small(dict(small, loss=loss_part)), name="gather_small")
    sm_packed = _adamw_small(sparts, _pack_small(ws), _pack_small(ms), _pack_small(vs))
    sm = [_unpack_small(a, ws) for a in sm_packed]
    loss = sm[0]["loss"][0]

    done = sm_packed[1][:1, :1] + sum(big[n][1][0, :1, :1] for n in rest_names)
    got_in = _exchange_wait(in_state, done, gather=False, name="scatter_w_in_wait")[0]
    big["w_in"] = update("w_in", got_in, w_in, m_w_in, v_w_in, True)

    order = ("g_attn", "w_in", "b_in", "sinks", "rel_table", "g_out_a", "g_out_b", "w_o", "g_ffn", "w_gate", "w_up",
             "w_down", "g_final")
    outs = [loss, grad_x[None]]
    for k in range(4):
        outs += [big[n][k] if n in big else sm[k][n] for n in order]
    return tuple(outs)
```

```python
import math

import jax
import jax.numpy as jnp
from jax import lax
from jax.experimental import pallas as pl
from jax.experimental.pallas import tpu as pltpu

F32 = jnp.float32
BF16 = jnp.bfloat16

N_DEV = 8
D_MODEL = 1024
HEAD_DIM = 64
N_HEADS = 8
PAIR = 2 * HEAD_DIM
WIDTH = N_HEADS * HEAD_DIM
D_IN = 2304
D_INP = 2560
D_FF = 2816
BLK = 128
ROPE_THETA = 150000.0
REL_BUCKETS = 32
REL_MAX_DISTANCE = 2048
EPS = 1e-5
NEG = -1e30
BRANCHES = ((128, 1), (512, 4), (2048, 16))
Q_SCALE = HEAD_DIM ** -0.5

ADAM_LR = 0.001
ADAM_B1 = 0.9
ADAM_B2 = 0.999
ADAM_EPS = 1e-08
ADAM_WD = 0.01
ADAM_STEP = 10

VMEM_LIMIT = 56 * 1024 * 1024
MESH = pl.DeviceIdType.MESH

NT = (((1,), (1,)), ((), ()))
TN = (((0,), (0,)), ((), ()))

SMALL_ROWS = 56


def _params(sem=None):
    return pltpu.CompilerParams(dimension_semantics=sem, vmem_limit_bytes=VMEM_LIMIT)


def _sigmoid(x):
    return 1.0 / (1.0 + jnp.exp2(x * (-1.0 / math.log(2.0))))


def _rms_bwd(dh, xh, r, g):
    u = dh * g
    return r * (u - xh * jnp.mean(u * xh, axis=-1, keepdims=True))


def _rope_rot(t, first):
    return jnp.where(first, pltpu.roll(t, 96, 1), pltpu.roll(t, 32, 1))


N_CHUNK = WIDTH // PAIR


def _scr(tm):
    return pltpu.VMEM((N_CHUNK, tm, PAIR), F32)


def _scr_get(scr):
    return jnp.concatenate([scr[j] for j in range(N_CHUNK)], axis=1)


def _scr_put(scr, val):
    for j in range(N_CHUNK):
        scr[j] = val[:, j * PAIR:(j + 1) * PAIR]


def _unstride(view_ref, scr, dil, tm):
    n = tm // dil
    chunks = scr.shape[0]
    for r in range(dil):
        for j in range(chunks):
            col = (r * chunks + j) * PAIR
            scr.at[j][pl.ds(r, n, stride=dil), :] = view_ref[:, col:col + PAIR].astype(F32)


def _restride(scr, out_ref, dil, tm):
    n = tm // dil
    chunks = scr.shape[0]
    for r in range(dil):
        for j in range(chunks):
            col = (r * chunks + j) * PAIR
            rows = scr[j] if dil == 1 else scr.at[j][pl.ds(r, n, stride=dil), :]
            out_ref[:, col:col + PAIR] = rows.astype(out_ref.dtype)


def _view_specs(tm, width=WIDTH):
    return [pl.BlockSpec((tm // dil, dil * width), lambda i: (i, 0)) for _, dil in BRANCHES]


def _view_shapes(T, dtype, width=WIDTH):
    return [jax.ShapeDtypeStruct((T // dil, dil * width), dtype) for _, dil in BRANCHES]


def _norm_proj(x, g, w, b, cos, sin, *, tm=512):
    T = x.shape[0]

    def body(x_ref, g_ref, w_ref, b_ref, cos_ref, sin_ref, h_ref, qa_ref, ka_ref, va_ref, *rest):
        outs_b, ys = rest[:9], rest[9]
        xv = x_ref[...]
        r = lax.rsqrt(jnp.mean(xv * xv, axis=-1, keepdims=True) + EPS)
        h = (xv * r * g_ref[...]).astype(BF16)
        h_ref[...] = h
        cosv = cos_ref[...]
        sinv = sin_ref[...]
        lane = lax.broadcasted_iota(jnp.int32, (tm, PAIR), 1)
        first = (lane % HEAD_DIM) < (HEAD_DIM // 2)

        def proj(off):
            return (lax.dot_general(h, w_ref[off:off + 256, :], NT, preferred_element_type=F32)
                    + b_ref[:, off:off + 256])

        for (off, width, rot, scale), o_ref in zip(((0, 512, True, Q_SCALE), (512, 256, True, 1.0), (768, 256, False, 1.0)),
                                                   (qa_ref, ka_ref, va_ref)):
            for c in range(0, width, 256):
                y = proj(off + c)
                for j in range(0, 256, PAIR):
                    t = y[:, j:j + PAIR]
                    if rot:
                        t = t * cosv + _rope_rot(t, first) * sinv
                    if scale != 1.0:
                        t = t * scale
                    o_ref[:, c + j:c + j + PAIR] = t.astype(BF16)
        for n, (off, scale) in enumerate(((1024, Q_SCALE), (1536, 1.0), (2048, 1.0))):
            for c in range(0, WIDTH, 256):
                y = proj(off + c)
                y = y * scale if scale != 1.0 else y
                for j in range(0, 256, PAIR):
                    ys[(c + j) // PAIR] = y[:, j:j + PAIR]
            for (_, dil), o_ref in zip(BRANCHES, outs_b[3 * n:3 * n + 3]):
                _restride(ys, o_ref, dil, tm)

    row = lambda w_: pl.BlockSpec((tm, w_), lambda i: (i, 0))
    full = lambda a: pl.BlockSpec(a.shape, lambda i: (0, 0))
    return pl.pallas_call(
        body, name="norm_proj", grid=(T // tm,),
        in_specs=[row(D_MODEL), full(g), full(w), full(b), row(PAIR), row(PAIR)],
        out_specs=[row(D_MODEL), row(512), row(256), row(256)] + _view_specs(tm) * 3,
        out_shape=[jax.ShapeDtypeStruct((T, n), BF16) for n in (D_MODEL, 512, 256, 256)] + _view_shapes(T, BF16) * 3,
        scratch_shapes=[_scr(tm)],
        compiler_params=_params(("arbitrary",)),
    )(x, g, w, b, cos, sin)


MAX_SUB = 8
AHEAD = 2


def _attn_specs(kvw, sub):
    q_spec = pl.BlockSpec((sub * BLK, WIDTH), lambda r, i: (i, r))
    kc_spec = pl.BlockSpec((sub * BLK, kvw), lambda r, i: (i, r))
    kp_spec = pl.BlockSpec((BLK, kvw), lambda r, i: (jnp.maximum(sub * i - 1, 0), r))
    b_spec = pl.BlockSpec((2, N_HEADS, BLK, 2 * BLK), lambda r, i: (0, 0, 0, 0))
    return q_spec, kp_spec, kc_spec, b_spec


def _window(prev_ref, cur_ref, j, ksl):
    before = prev_ref[:, ksl] if j == 0 else cur_ref[(j - 1) * BLK:j * BLK, ksl]
    return jnp.concatenate([before, cur_ref[j * BLK:(j + 1) * BLK, ksl]], axis=0)


def _attn_fwd(q, k, v, bias, sinks, *, dil, kv_pairs, use_sink, name):
    L = q.shape[0]
    sub = min(MAX_SUB, L // BLK)
    ns = L // (sub * BLK)
    kvw = kv_pairs * PAIR
    rep = 4 // kv_pairs

    def body(sink_ref, q_ref, kp_ref, kc_ref, vp_ref, vc_ref, b_ref, o_ref, lse_ref):
        lane = lax.broadcasted_iota(jnp.int32, (1, PAIR), 1)
        lo = lane < HEAD_DIM
        first = jnp.where(pl.program_id(1) == 0, 1, 0)
        def scores(j, hp):
            rows = slice(j * BLK, (j + 1) * BLK)
            sl = slice(hp * PAIR, (hp + 1) * PAIR)
            ksl = slice((hp // rep) * PAIR, (hp // rep + 1) * PAIR)
            qp = q_ref[rows, sl]
            kk = _window(kp_ref, kc_ref, j, ksl)
            vv = _window(vp_ref, vc_ref, j, ksl)
            heads = []
            for e in range(2):
                h = 2 * hp + e
                msk = lo if e == 0 else jnp.logical_not(lo)
                qm = jnp.where(msk, qp, jnp.zeros_like(qp))
                s = lax.dot_general(qm, kk, NT, preferred_element_type=F32) + (b_ref[first, h] if j == 0 else b_ref[0, h])
                heads.append((h, msk, s))
            return rows, sl, vv, heads

        def outputs(rows, sl, vv, heads):
            o_pair = None
            lse_pair = None
            for h, msk, s in heads:
                m = jnp.max(s, axis=-1, keepdims=True)
                if use_sink:
                    sk = sink_ref[h]
                    m = jnp.maximum(m, sk)
                p = jnp.exp(s - m)
                l = jnp.sum(p, axis=-1, keepdims=True)
                if use_sink:
                    l = l + jnp.exp(sk - m)
                vm = jnp.where(msk, vv, jnp.zeros_like(vv))
                oe = jnp.dot(p.astype(BF16), vm, preferred_element_type=F32) * (1.0 / l)
                ls = m + jnp.log(l)
                if o_pair is None:
                    o_pair = oe
                    lse_pair = jnp.broadcast_to(ls, (BLK, PAIR))
                else:
                    o_pair = o_pair + oe
                    lse_pair = jnp.where(lo, lse_pair, ls)
            o_ref[rows, sl] = o_pair.astype(BF16)
            lse_ref[rows, sl] = lse_pair

        items = [(j, hp) for j in range(sub) for hp in range(4)]
        queue = [scores(*it) for it in items[:AHEAD]]
        for n in range(len(items)):
            if n + AHEAD < len(items):
                queue.append(scores(*items[n + AHEAD]))
            outputs(*queue.pop(0))

    q_spec, kp_spec, kc_spec, b_spec = _attn_specs(kvw, sub)
    return pl.pallas_call(
        body, name=name, grid=(dil, ns),
        in_specs=[pl.BlockSpec(memory_space=pltpu.SMEM), q_spec, kp_spec, kc_spec, kp_spec, kc_spec, b_spec],
        out_specs=[q_spec, q_spec],
        out_shape=[jax.ShapeDtypeStruct((L, dil * WIDTH), BF16), jax.ShapeDtypeStruct((L, dil * WIDTH), F32)],
        compiler_params=_params(("arbitrary", "arbitrary")),
    )(sinks, q, k, k, v, v, bias)


def _attn_bwd(q, k, v, o, do, lse, bias, sinks, *, dil, kv_pairs, use_sink, name, max_sub=MAX_SUB):
    L = q.shape[0]
    sub = min(max_sub, L // BLK)
    ns = L // (sub * BLK)
    n_steps = dil * ns
    kvw = kv_pairs * PAIR
    rep = 4 // kv_pairs
    last = slice((sub - 1) * BLK, sub * BLK)

    def body(sink_ref, q_ref, kp_ref, kc_ref, vp_ref, vc_ref, o_ref, do_ref, lse_ref, b_ref,
             dq_ref, dk_ref, dv_ref, dsum_ref, dsk_ref, pk_ref, pv_ref):
        t = pl.program_id(0)
        i = t % ns

        @pl.when(t == 0)
        def _():
            dsum_ref[...] = jnp.zeros_like(dsum_ref)
            dsk_ref[...] = jnp.zeros_like(dsk_ref)
            pk_ref[...] = jnp.zeros_like(pk_ref)
            pv_ref[...] = jnp.zeros_like(pv_ref)

        @pl.when(t < n_steps)
        def _():
            lo = lax.broadcasted_iota(jnp.int32, (1, PAIR), 1) < HEAD_DIM
            first = jnp.where(i == 0, 1, 0)
            dks = [[None] * kv_pairs for _ in range(sub)]
            dvs = [[None] * kv_pairs for _ in range(sub)]
            def scores(j, hp):
                rows = slice(j * BLK, (j + 1) * BLK)
                kvp = hp // rep
                sl = slice(hp * PAIR, (hp + 1) * PAIR)
                ksl = slice(kvp * PAIR, (kvp + 1) * PAIR)
                qp = q_ref[rows, sl]
                dop = do_ref[rows, sl]
                prod = dop.astype(F32) * o_ref[rows, sl].astype(F32)
                kk = _window(kp_ref, kc_ref, j, ksl)
                vv = _window(vp_ref, vc_ref, j, ksl)
                heads = []
                for e in range(2):
                    h = 2 * hp + e
                    msk = lo if e == 0 else jnp.logical_not(lo)
                    qm = jnp.where(msk, qp, jnp.zeros_like(qp))
                    dom = jnp.where(msk, dop, jnp.zeros_like(dop))
                    km = jnp.where(msk, kk, jnp.zeros_like(kk))
                    s = (lax.dot_general(qm, kk, NT, preferred_element_type=F32)
                         + (b_ref[first, h] if j == 0 else b_ref[0, h]))
                    dp = lax.dot_general(dom, vv, NT, preferred_element_type=F32)
                    heads.append((h, msk, qm, dom, km, s, dp))
                return j, rows, kvp, sl, prod, heads

            def grads(j, rows, kvp, sl, prod, heads):
                dq_pair = None
                c_pair = None
                qms, doms, dsbs, pbs = [], [], [], []
                for h, msk, qm, dom, km, s, dp in heads:
                    ls = lse_ref[rows, h * HEAD_DIM:h * HEAD_DIM + 1]
                    p = jnp.exp(s - ls)
                    delta = jnp.sum(jnp.where(msk, prod, 0.0), axis=-1, keepdims=True)
                    ds = p * (dp - delta)
                    if use_sink:
                        ce = jnp.exp(sink_ref[h] - ls) * delta
                        c_pair = jnp.broadcast_to(ce, (BLK, PAIR)) if c_pair is None else jnp.where(msk, ce, c_pair)
                    else:
                        dsum_ref[h] += ds
                    dsb = ds.astype(BF16)
                    dqe = jnp.dot(dsb, km, preferred_element_type=F32)
                    dq_pair = dqe if dq_pair is None else dq_pair + dqe
                    qms.append(qm)
                    doms.append(dom)
                    dsbs.append(dsb)
                    pbs.append(p.astype(BF16))
                dke = lax.dot_general(jnp.concatenate(dsbs, axis=0), jnp.concatenate(qms, axis=0), TN,
                                      preferred_element_type=F32)
                dve = lax.dot_general(jnp.concatenate(pbs, axis=0), jnp.concatenate(doms, axis=0), TN,
                                      preferred_element_type=F32)
                dks[j][kvp] = dke if dks[j][kvp] is None else dks[j][kvp] + dke
                dvs[j][kvp] = dve if dvs[j][kvp] is None else dvs[j][kvp] + dve
                dq_ref[rows, sl] = (dq_pair * Q_SCALE).astype(BF16)
                if use_sink:
                    dsk_ref[:, sl] += c_pair

            items = [(j, hp) for j in range(sub) for hp in range(4)]
            ahead = AHEAD + 1 if use_sink else AHEAD
            queue = [scores(*it) for it in items[:ahead]]
            for n in range(len(items)):
                if n + ahead < len(items):
                    queue.append(scores(*items[n + ahead]))
                grads(*queue.pop(0))
            for kvp in range(kv_pairs):
                ksl = slice(kvp * PAIR, (kvp + 1) * PAIR)
                for pend_ref, out_ref, parts in ((pk_ref, dk_ref, [d[kvp] for d in dks]),
                                                 (pv_ref, dv_ref, [d[kvp] for d in dvs])):
                    if sub > 1:
                        out_ref[:(sub - 1) * BLK, ksl] = pend_ref[:(sub - 1) * BLK, ksl].astype(BF16)
                    out_ref[last, ksl] = (pend_ref[last, ksl] + parts[0][:BLK]).astype(BF16)
                    for j in range(sub):
                        own = parts[j][BLK:]
                        pend_ref[j * BLK:(j + 1) * BLK, ksl] = own + parts[j + 1][:BLK] if j + 1 < sub else own

        @pl.when(t == n_steps)
        def _():
            dk_ref[...] = pk_ref[...].astype(BF16)
            dv_ref[...] = pv_ref[...].astype(BF16)

    def at(t):
        t = jnp.minimum(t, n_steps - 1)
        return t % ns, t // ns

    def before(t):
        return at(jnp.maximum(t - 1, 0))

    q_spec = pl.BlockSpec((sub * BLK, WIDTH), at)
    kc_spec = pl.BlockSpec((sub * BLK, kvw), at)
    kp_spec = pl.BlockSpec((BLK, kvw), lambda t: (jnp.maximum(sub * at(t)[0] - 1, 0), at(t)[1]))
    b_spec = pl.BlockSpec((2, N_HEADS, BLK, 2 * BLK), lambda t: (0, 0, 0, 0))
    dkv_spec = pl.BlockSpec((sub * BLK, kvw), before)
    return pl.pallas_call(
        body, name=name, grid=(n_steps + 1,),
        in_specs=[pl.BlockSpec(memory_space=pltpu.SMEM), q_spec, kp_spec, kc_spec, kp_spec, kc_spec,
                  q_spec, q_spec, q_spec, b_spec],
        out_specs=[q_spec, dkv_spec, dkv_spec,
                   pl.BlockSpec((N_HEADS, BLK, 2 * BLK), lambda t: (0, 0, 0)),
                   pl.BlockSpec((BLK, WIDTH), lambda t: (0, 0))],
        out_shape=[jax.ShapeDtypeStruct((L, dil * WIDTH), BF16),
                   jax.ShapeDtypeStruct((L, dil * kvw), BF16),
                   jax.ShapeDtypeStruct((L, dil * kvw), BF16),
                   jax.ShapeDtypeStruct((N_HEADS, BLK, 2 * BLK), F32),
                   jax.ShapeDtypeStruct((BLK, WIDTH), F32)],
        scratch_shapes=[pltpu.VMEM((sub * BLK, kvw), F32), pltpu.VMEM((sub * BLK, kvw), F32)],
        compiler_params=_params(("arbitrary",)),
    )(sinks, q, k, k, v, v, o, do, lse, bias)


def _merge_wo(x, oa, o1, o2, o3, l1, l2, l3, ga, gb, wo, gf, *, tm=512):
    T = x.shape[0]

    def body(x_ref, oa_ref, o1_ref, o2_ref, o3_ref, l1_ref, l2_ref, l3_ref, ga_ref, gb_ref, wo_ref, gf_ref,
             x2_ref, mix_ref, h2_ref, ob1_ref, ob4_ref, ob16_ref, ls1_ref, ls4_ref, ls16_ref, so2, so3, sl2, sl3):
        _unstride(o2_ref, so2, BRANCHES[1][1], tm)
        _unstride(o3_ref, so3, BRANCHES[2][1], tm)
        _unstride(l2_ref, sl2, BRANCHES[1][1], tm)
        _unstride(l3_ref, sl3, BRANCHES[2][1], tm)
        la, lb, lc = l1_ref[...], _scr_get(sl2), _scr_get(sl3)
        m = jnp.maximum(jnp.maximum(la, lb), lc)
        ea, eb, ec = jnp.exp(la - m), jnp.exp(lb - m), jnp.exp(lc - m)
        den = ea + eb + ec
        inv = 1.0 / den
        ob = (ea * o1_ref[...].astype(F32) + eb * _scr_get(so2) + ec * _scr_get(so3)) * inv
        _scr_put(so2, ob)
        _scr_put(sl2, m + jnp.log(den))
        for (_, dil), o_ref, l_ref in zip(BRANCHES, (ob1_ref, ob4_ref, ob16_ref), (ls1_ref, ls4_ref, ls16_ref)):
            _restride(so2, o_ref, dil, tm)
            _restride(sl2, l_ref, dil, tm)
        oav = oa_ref[...].astype(F32)
        ra = lax.rsqrt(jnp.mean(oav * oav, axis=-1, keepdims=True) + EPS)
        rb = lax.rsqrt(jnp.mean(ob * ob, axis=-1, keepdims=True) + EPS)
        mix_ref[:, :WIDTH] = (oav * ra * ga_ref[...]).astype(BF16)
        mix_ref[:, WIDTH:] = (ob * rb * gb_ref[...]).astype(BF16)
        x2 = x_ref[...] + jnp.dot(mix_ref[...], wo_ref[...], preferred_element_type=F32)
        x2_ref[...] = x2
        r2 = lax.rsqrt(jnp.mean(x2 * x2, axis=-1, keepdims=True) + EPS)
        h2_ref[...] = (x2 * r2 * gf_ref[...]).astype(BF16)

    row = lambda w_: pl.BlockSpec((tm, w_), lambda i: (i, 0))
    full = lambda a: pl.BlockSpec(a.shape, lambda i: (0, 0))
    return pl.pallas_call(
        body, name="merge_wo", grid=(T // tm,),
        in_specs=[row(D_MODEL), row(WIDTH)] + _view_specs(tm) * 2 + [full(ga), full(gb), full(wo), full(gf)],
        out_specs=[row(D_MODEL), row(D_MODEL), row(D_MODEL)] + _view_specs(tm) * 2,
        out_shape=[jax.ShapeDtypeStruct((T, D_MODEL), F32), jax.ShapeDtypeStruct((T, D_MODEL), BF16),
                   jax.ShapeDtypeStruct((T, D_MODEL), BF16)] + _view_shapes(T, BF16) + _view_shapes(T, F32),
        scratch_shapes=[_scr(tm)] * 4,
        compiler_params=_params(("arbitrary",)),
    )(x, oa, o1, o2, o3, l1, l2, l3, ga, gb, wo, gf)


def _ffn_up(h2, wgt, wut, *, tm=512, fc=D_FF, rc=512, cc=256):
    T = h2.shape[0]

    def body(h_ref, wg_ref, wu_ref, gate_ref, up_ref, act_ref):
        for s in range(0, tm, rc):
            h = h_ref[s:s + rc, :]
            for c in range(0, fc, cc):
                gt = lax.dot_general(h, wg_ref[c:c + cc, :], NT, preferred_element_type=F32)
                u = lax.dot_general(h, wu_ref[c:c + cc, :], NT, preferred_element_type=F32)
                gate_ref[s:s + rc, c:c + cc] = gt.astype(BF16)
                up_ref[s:s + rc, c:c + cc] = u.astype(BF16)
                act_ref[s:s + rc, c:c + cc] = (gt * _sigmoid(gt) * u).astype(BF16)

    rowd = pl.BlockSpec((tm, D_MODEL), lambda i, c: (i, 0))
    wrow = pl.BlockSpec((fc, D_MODEL), lambda i, c: (c, 0))
    oc = pl.BlockSpec((tm, fc), lambda i, c: (i, c))
    return pl.pallas_call(
        body, name="ffn_up", grid=(T // tm, D_FF // fc),
        in_specs=[rowd, wrow, wrow],
        out_specs=[oc, oc, oc],
        out_shape=[jax.ShapeDtypeStruct((T, D_FF), BF16)] * 3,
        compiler_params=_params(("arbitrary", "arbitrary")),
    )(h2, wgt, wut)


def _ffn_down_loss(act, wd, x2, tgt, g, *, tm=1024, rc=256):
    T = x2.shape[0]

    def body(act_ref, wd_ref, x2_ref, tgt_ref, g_ref, dx_ref, dxb_ref, loss_ref, dg_ref):
        @pl.when(pl.program_id(0) == 0)
        def _():
            loss_ref[...] = jnp.zeros_like(loss_ref)
            dg_ref[...] = jnp.zeros_like(dg_ref)

        gv = g_ref[...]
        lsum = jnp.zeros((1, 1), F32)
        dgs = jnp.zeros((1, D_MODEL), F32)
        for c in range(0, tm, rc):
            x3 = x2_ref[c:c + rc, :] + jnp.dot(act_ref[c:c + rc, :], wd_ref[...], preferred_element_type=F32)
            r = lax.rsqrt(jnp.mean(x3 * x3, axis=-1, keepdims=True) + EPS)
            xh = x3 * r
            diff = xh * gv - tgt_ref[c:c + rc, :]
            lsum = lsum + jnp.sum(jnp.sum(diff * diff, axis=-1, keepdims=True), axis=0, keepdims=True)
            dy = diff * (1.0 / D_MODEL)
            dgs = dgs + jnp.sum(dy * xh, axis=0, keepdims=True)
            dx = _rms_bwd(dy, xh, r, gv)
            dx_ref[c:c + rc, :] = dx
            dxb_ref[c:c + rc, :] = dx.astype(BF16)
        loss_ref[...] += lsum * (0.5 / D_MODEL)
        dg_ref[...] += dgs

    rowd = pl.BlockSpec((tm, D_MODEL), lambda i: (i, 0))
    return pl.pallas_call(
        body, name="ffn_down_loss", grid=(T // tm,),
        in_specs=[pl.BlockSpec((tm, D_FF), lambda i: (i, 0)),
                  pl.BlockSpec((D_FF, D_MODEL), lambda i: (0, 0), pipeline_mode=pl.Buffered(1)),
                  rowd, rowd, pl.BlockSpec(g.shape, lambda i: (0, 0))],
        out_specs=[rowd, rowd, pl.BlockSpec((1, 1), lambda i: (0, 0)), pl.BlockSpec((1, D_MODEL), lambda i: (0, 0))],
        out_shape=[jax.ShapeDtypeStruct((T, D_MODEL), F32), jax.ShapeDtypeStruct((T, D_MODEL), BF16),
                   jax.ShapeDtypeStruct((1, 1), F32), jax.ShapeDtypeStruct((1, D_MODEL), F32)],
        compiler_params=_params(("arbitrary",)),
    )(act, wd, x2, tgt, g)


def _ffn_bwd(dx3, gate, up, wd, wgt, wut, x2, g, *, tm=256, cc=256):
    T = x2.shape[0]

    def body(dx_ref, gate_ref, up_ref, wd_ref, wg_ref, wu_ref, x2_ref, g_ref,
             dgate_ref, dup_ref, dx2_ref, dx2b_ref, dg_ref):
        @pl.when(pl.program_id(0) == 0)
        def _():
            dg_ref[...] = jnp.zeros_like(dg_ref)

        dxb = dx_ref[...].astype(BF16)
        for c in range(0, D_FF, cc):
            dact = lax.dot_general(dxb, wd_ref[c:c + cc, :], NT, preferred_element_type=F32)
            gt = gate_ref[:, c:c + cc].astype(F32)
            u = up_ref[:, c:c + cc].astype(F32)
            sg = _sigmoid(gt)
            a = dact * sg
            dgate_ref[:, c:c + cc] = (a * u * ((1.0 + gt) - gt * sg)).astype(BF16)
            dup_ref[:, c:c + cc] = (a * gt).astype(BF16)
        dh = (jnp.dot(dgate_ref[...], wg_ref[...], preferred_element_type=F32)
              + jnp.dot(dup_ref[...], wu_ref[...], preferred_element_type=F32))
        xv = x2_ref[...]
        r = lax.rsqrt(jnp.mean(xv * xv, axis=-1, keepdims=True) + EPS)
        xh = xv * r
        dg_ref[...] += jnp.sum(dh * xh, axis=0, keepdims=True)
        d = dx_ref[...] + _rms_bwd(dh, xh, r, g_ref[...])
        dx2_ref[...] = d
        dx2b_ref[...] = d.astype(BF16)

    rowd = pl.BlockSpec((tm, D_MODEL), lambda i: (i, 0))
    rowf = pl.BlockSpec((tm, D_FF), lambda i: (i, 0))
    wfull = pl.BlockSpec((D_FF, D_MODEL), lambda i: (0, 0), pipeline_mode=pl.Buffered(1))
    return pl.pallas_call(
        body, name="ffn_bwd", grid=(T // tm,),
        in_specs=[rowd, rowf, rowf, wfull, wfull, wfull, rowd, pl.BlockSpec(g.shape, lambda i: (0, 0))],
        out_specs=[rowf, rowf, rowd, rowd, pl.BlockSpec((1, D_MODEL), lambda i: (0, 0))],
        out_shape=[jax.ShapeDtypeStruct((T, D_FF), BF16), jax.ShapeDtypeStruct((T, D_FF), BF16),
                   jax.ShapeDtypeStruct((T, D_MODEL), F32), jax.ShapeDtypeStruct((T, D_MODEL), BF16),
                   jax.ShapeDtypeStruct((1, D_MODEL), F32)],
        compiler_params=_params(("arbitrary",)),
    )(dx3, gate, up, wd, wgt, wut, x2, g)


def _matmul_tn(a, b, *, tk, tn, tt=2048, out_dtype=BF16, name):
    T, K = a.shape
    N = b.shape[1]
    nt = T // tt

    def body(a_ref, b_ref, o_ref, acc_ref):
        part = lax.dot_general(a_ref[...], b_ref[...], TN, preferred_element_type=F32)

        @pl.when(pl.program_id(2) == 0)
        def _():
            acc_ref[...] = part

        @pl.when(pl.program_id(2) > 0)
        def _():
            acc_ref[...] += part

        @pl.when(pl.program_id(2) == nt - 1)
        def _():
            o_ref[...] = acc_ref[...].astype(out_dtype)

    return pl.pallas_call(
        body, name=name, grid=(K // tk, N // tn, nt),
        in_specs=[pl.BlockSpec((tt, tk), lambda i, j, t: (t, i)), pl.BlockSpec((tt, tn), lambda i, j, t: (t, j))],
        out_specs=pl.BlockSpec((tk, tn), lambda i, j, t: (i, j)),
        out_shape=jax.ShapeDtypeStruct((K, N), out_dtype),
        scratch_shapes=[pltpu.VMEM((tk, tn), F32)],
        compiler_params=_params(("arbitrary", "arbitrary", "arbitrary")),
    )(a, b)


def _wo_bwd(dx2b, wo, oa, ob, ga, gb, mixed, *, tm=1024, wc=256):
    T = dx2b.shape[0]
    n_tiles = T // tm

    def body(dx_ref, wo_ref, oa_ref, ob_ref, ga_ref, gb_ref, mix_ref,
             doa_ref, dob1_ref, dob4_ref, dob16_ref, dga_ref, dgb_ref, dwo_ref, scr, dw_acc):
        @pl.when(pl.program_id(0) == 0)
        def _():
            dga_ref[...] = jnp.zeros_like(dga_ref)
            dgb_ref[...] = jnp.zeros_like(dgb_ref)
            dw_acc[...] = jnp.zeros_like(dw_acc)

        dxv = dx_ref[...]
        for c in range(0, D_MODEL, wc):
            dw_acc[c:c + wc, :] += lax.dot_general(mix_ref[:, c:c + wc], dxv, TN, preferred_element_type=F32)

        @pl.when(pl.program_id(0) == n_tiles - 1)
        def _():
            dwo_ref[...] = dw_acc[...].astype(BF16)

        dm = lax.dot_general(dxv, wo_ref[...], NT, preferred_element_type=F32)
        for o_ref, g_ref, dg_ref, sl in ((oa_ref, ga_ref, dga_ref, slice(0, WIDTH)),
                                         (ob_ref, gb_ref, dgb_ref, slice(WIDTH, 2 * WIDTH))):
            ov = o_ref[...].astype(F32)
            r = lax.rsqrt(jnp.mean(ov * ov, axis=-1, keepdims=True) + EPS)
            xh = ov * r
            d = dm[:, sl]
            dg_ref[...] += jnp.sum(d * xh, axis=0, keepdims=True)
            do = _rms_bwd(d, xh, r, g_ref[...])
            if o_ref is oa_ref:
                doa_ref[...] = do.astype(BF16)
            else:
                _scr_put(scr, do)
                for (_, dil), v_ref in zip(BRANCHES, (dob1_ref, dob4_ref, dob16_ref)):
                    _restride(scr, v_ref, dil, tm)

    row = lambda w_: pl.BlockSpec((tm, w_), lambda i: (i, 0))
    full = lambda a: pl.BlockSpec(a.shape, lambda i: (0, 0))
    return pl.pallas_call(
        body, name="wo_bwd", grid=(T // tm,),
        in_specs=[row(D_MODEL), full(wo), row(WIDTH), row(WIDTH), full(ga), full(gb), row(D_MODEL)],
        out_specs=[row(WIDTH)] + _view_specs(tm)
        + [pl.BlockSpec((1, WIDTH), lambda i: (0, 0)), pl.BlockSpec((1, WIDTH), lambda i: (0, 0)), full(wo)],
        out_shape=[jax.ShapeDtypeStruct((T, WIDTH), BF16)] + _view_shapes(T, BF16)
        + [jax.ShapeDtypeStruct((1, WIDTH), F32), jax.ShapeDtypeStruct((1, WIDTH), F32),
           jax.ShapeDtypeStruct((D_MODEL, D_MODEL), BF16)],
        scratch_shapes=[_scr(tm), pltpu.VMEM((D_MODEL, D_MODEL), F32)],
        compiler_params=_params(("arbitrary",)),
    )(dx2b, wo, oa, ob, ga, gb, mixed)


def _inproj_bwd(dqa, dka, dva, dqs, dks, dvs, cos, sin, w, x, dx2, g, h1, *, tm=512, wc=256):
    T = dqa.shape[0]
    n_tiles = T // tm

    def body(dqa_ref, dka_ref, dva_ref, q1, q2, q3, k1, k2, k3, v1, v2, v3, cos_ref, sin_ref, w_ref, x_ref, dx2_ref,
             g_ref, h1_ref, db_ref, gx_ref, dg_ref, dw_ref, dp_ref, dw_acc, acc, tmp):
        @pl.when(pl.program_id(0) == 0)
        def _():
            db_ref[...] = jnp.zeros_like(db_ref)
            dg_ref[...] = jnp.zeros_like(dg_ref)
            dw_acc[...] = jnp.zeros_like(dw_acc)

        cosv = cos_ref[...]
        sinv = sin_ref[...]
        lane = lax.broadcasted_iota(jnp.int32, (tm, PAIR), 1)
        first = (lane % HEAD_DIM) < (HEAD_DIM // 2)

        def put(off, val):
            dp_ref[:, off:off + PAIR] = val.astype(BF16)
            db_ref[:, off:off + PAIR] += jnp.sum(val, axis=0, keepdims=True)

        for src, off, width in ((dqa_ref, 0, 512), (dka_ref, 512, 256)):
            for j in range(0, width, PAIR):
                d = src[:, j:j + PAIR].astype(F32)
                put(off + j, d * cosv - _rope_rot(d, first) * sinv)
        for j in range(0, 256, PAIR):
            put(768 + j, dva_ref[:, j:j + PAIR].astype(F32))
        for (a, b, c), off in (((q1, q2, q3), 1024), ((k1, k2, k3), 1536), ((v1, v2, v3), 2048)):
            _unstride(b, acc, BRANCHES[1][1], tm)
            _unstride(c, tmp, BRANCHES[2][1], tm)
            for j in range(N_CHUNK):
                put(off + j * PAIR, a[:, j * PAIR:(j + 1) * PAIR].astype(F32) + acc[j] + tmp[j])

        dh = jnp.dot(dp_ref[...], w_ref[...], preferred_element_type=F32)
        xv = x_ref[...]
        r = lax.rsqrt(jnp.mean(xv * xv, axis=-1, keepdims=True) + EPS)
        xh = xv * r
        dg_ref[...] += jnp.sum(dh * xh, axis=0, keepdims=True)
        gx_ref[...] = dx2_ref[...] + _rms_bwd(dh, xh, r, g_ref[...])

        h1v = h1_ref[...]
        for c in range(0, D_INP, wc):
            dw_acc[c:c + wc, :] += lax.dot_general(dp_ref[:, c:c + wc], h1v, TN, preferred_element_type=F32)

        @pl.when(pl.program_id(0) == n_tiles - 1)
        def _():
            pltpu.sync_copy(dw_acc, dw_ref)

    row = lambda w_: pl.BlockSpec((tm, w_), lambda i: (i, 0))
    full = lambda a: pl.BlockSpec(a.shape, lambda i: (0, 0))
    return pl.pallas_call(
        body, name="inproj_bwd", grid=(n_tiles,),
        in_specs=[row(512), row(256), row(256)] + _view_specs(tm) * 3 + [row(PAIR), row(PAIR)]
        + [pl.BlockSpec(w.shape, lambda i: (0, 0), pipeline_mode=pl.Buffered(1)), row(D_MODEL), row(D_MODEL), full(g),
           row(D_MODEL)],
        out_specs=[pl.BlockSpec((1, D_INP), lambda i: (0, 0)), row(D_MODEL),
                   pl.BlockSpec((1, D_MODEL), lambda i: (0, 0)), pl.BlockSpec(memory_space=pl.ANY)],
        out_shape=[jax.ShapeDtypeStruct((1, D_INP), F32), jax.ShapeDtypeStruct((T, D_MODEL), F32),
                   jax.ShapeDtypeStruct((1, D_MODEL), F32), jax.ShapeDtypeStruct((D_INP, D_MODEL), F32)],
        scratch_shapes=[pltpu.VMEM((tm, D_INP), BF16), pltpu.VMEM((D_INP, D_MODEL), F32), _scr(tm), _scr(tm)],
        compiler_params=_params(("arbitrary",)),
    )(dqa, dka, dva, *dqs, *dks, *dvs, cos, sin, w, x, dx2, g, h1)


def _bias_sink_grads(dsums, bmaps, dsk):
    def body(s1, s2, s3, m1, m2, m3, dsk_ref, drel_ref, dsink_ref):
        row = lax.broadcasted_iota(jnp.int32, (N_HEADS, 128), 0)
        lane = lax.broadcasted_iota(jnp.int32, (N_HEADS, 128), 1)
        out = jnp.zeros((N_HEADS, 128), F32)
        for s_ref, m_ref in ((s1, m1), (s2, m2), (s3, m3)):
            bm = m_ref[...]
            for h in range(N_HEADS):
                a = s_ref[h]
                for b in range(REL_BUCKETS):
                    v = jnp.sum(jnp.sum(jnp.where(bm == b, a, 0.0), axis=-1, keepdims=True), axis=0, keepdims=True)
                    out = out + jnp.where((row == h) & (lane == b), v, 0.0)
        drel_ref[...] = out
        dsink_ref[...] = -jnp.sum(dsk_ref[...], axis=0, keepdims=True)

    vm = pl.BlockSpec(memory_space=pltpu.VMEM)
    return pl.pallas_call(
        body, name="bias_sink_grads",
        in_specs=[vm] * 7, out_specs=[vm, vm],
        out_shape=[jax.ShapeDtypeStruct((N_HEADS, 128), F32), jax.ShapeDtypeStruct((1, WIDTH), F32)],
        compiler_params=_params(),
    )(*dsums, *bmaps, dsk)


def _all_gather(blk, *, name):
    R, C = blk.shape

    def body(x_ref, out_ref, send_sems, recv_sems, local_sem):
        x, y, c = lax.axis_index("x"), lax.axis_index("y"), lax.axis_index("c")
        me, sibling = (x, y, c), (x, y, 1 - c)
        chips = [(1 - x, y), (x, 1 - y), (1 - x, 1 - y)]

        def slot(px, py, pc):
            return out_ref.at[4 * px + 2 * py + pc]

        def copy(k, block, to, src=None):
            return pltpu.make_async_remote_copy(
                src_ref=slot(*block) if src is None else src, dst_ref=slot(*block),
                send_sem=send_sems.at[k], recv_sem=recv_sems.at[k], device_id=to, device_id_type=MESH)

        mine = pltpu.make_async_copy(x_ref, slot(*me), local_sem)
        mine.start()
        first = [copy(0, me, sibling, src=x_ref)]
        first += [copy(1 + j, me, (*chip, c), src=x_ref) for j, chip in enumerate(chips)]
        for cp in first:
            cp.start()
        passed = [copy(4 + j, (*chip, c), sibling) for j, chip in enumerate(chips)]
        for j, chip in enumerate(chips):
            copy(1 + j, (*chip, c), me).wait_recv()
            passed[j].start()
        copy(0, sibling, me).wait_recv()
        for j, chip in enumerate(chips):
            copy(4 + j, (*chip, 1 - c), me).wait_recv()
        for cp in first + passed:
            cp.wait_send()
        mine.wait()

    return pl.pallas_call(
        body, name=name,
        in_specs=[pl.BlockSpec(memory_space=pl.ANY)], out_specs=pl.BlockSpec(memory_space=pl.ANY),
        out_shape=jax.ShapeDtypeStruct((N_DEV, R, C), blk.dtype),
        scratch_shapes=[pltpu.SemaphoreType.DMA((7,)), pltpu.SemaphoreType.DMA((7,)), pltpu.SemaphoreType.DMA],
        compiler_params=pltpu.CompilerParams(has_side_effects=True),
    )(blk)


def _peers(x, y, c):
    return [(x ^ (k >> 2), y ^ ((k >> 1) & 1), c ^ (k & 1)) for k in range(1, N_DEV)]


_HBM = pl.BlockSpec(memory_space=pltpu.HBM)
_SEM = pl.BlockSpec(memory_space=pltpu.SEMAPHORE)
_EFFECT = pltpu.SideEffectType.DATAFLOW_SIDE_EFFECTING


def _peer_list(x, y, c, near):
    if near:
        return [(x, y, 1 - c), (1 - x, y, c), (x, 1 - y, c), (1 - x, 1 - y, c)]
    return _peers(x, y, c)


def _exchange_start(srcs, *, gather, name, near=False):
    n = len(srcs)
    n_peers = 4 if near else N_DEV - 1
    lands = [lax.empty((N_DEV,) + s.shape[-2:], s.dtype) for s in srcs]

    def body(*refs):
        src_refs, land_refs = refs[:n], refs[n:2 * n]
        send_sems, recv_sems = refs[2 * n], refs[2 * n + 1]
        token = refs[-1]
        x, y, c = lax.axis_index("x"), lax.axis_index("y"), lax.axis_index("c")
        mine = 4 * x + 2 * y + c
        for a in range(n):
            for k, peer in enumerate(_peer_list(x, y, c, near)):
                dest = 4 * peer[0] + 2 * peer[1] + peer[2]
                j = a * n_peers + k
                pltpu.make_async_remote_copy(
                    src_ref=src_refs[a] if gather else src_refs[a].at[dest], dst_ref=land_refs[a].at[mine],
                    send_sem=send_sems.at[j], recv_sem=recv_sems.at[j], device_id=peer, device_id_type=MESH).start()
        token[...] = jnp.zeros_like(token)

    sems = pltpu.SemaphoreType.DMA((n * n_peers,))
    out = pl.pallas_call(
        body, name=name,
        out_shape=(sems, sems) + tuple(pltpu.HBM(a.shape, a.dtype) for a in list(srcs) + lands)
        + (jax.ShapeDtypeStruct((8, 128), F32),),
        in_specs=(_HBM,) * (2 * n), out_specs=(_SEM, _SEM) + (_HBM,) * (2 * n) + (pl.BlockSpec(memory_space=pltpu.VMEM),),
        input_output_aliases={i: 2 + i for i in range(2 * n)},
        compiler_params=pltpu.CompilerParams(has_side_effects=_EFFECT),
    )(*[pltpu.with_memory_space_constraint(a, pltpu.HBM) for a in list(srcs) + lands])
    return out[:-1], out[-1]


def _exchange_wait(state, after, *, gather, name, near=False):
    send_sems, recv_sems = state[0], state[1]
    n = (len(state) - 2) // 2
    n_peers = 4 if near else N_DEV - 1
    arrays = state[2:]

    def body(*refs):
        src_refs, land_refs = refs[:n], refs[n:2 * n]
        send_sems, recv_sems = refs[2 * n], refs[2 * n + 1]
        x, y, c = lax.axis_index("x"), lax.axis_index("y"), lax.axis_index("c")
        for a in range(n):
            for k, peer in enumerate(_peer_list(x, y, c, near)):
                other = 4 * peer[0] + 2 * peer[1] + peer[2]
                j = a * n_peers + k
                copy = pltpu.make_async_remote_copy(
                    src_ref=src_refs[a] if gather else src_refs[a].at[other], dst_ref=land_refs[a].at[other],
                    send_sem=send_sems.at[j], recv_sem=recv_sems.at[j], device_id=peer, device_id_type=MESH)
                copy.wait_send()
                copy.wait_recv()

    out = pl.pallas_call(
        body, name=name,
        out_shape=tuple(pltpu.HBM(a.shape, a.dtype) for a in arrays),
        in_specs=(_HBM,) * (2 * n) + (_SEM, _SEM, pl.BlockSpec(memory_space=pl.ANY)), out_specs=(_HBM,) * (2 * n),
        input_output_aliases={i: i for i in range(2 * n)},
        compiler_params=pltpu.CompilerParams(has_side_effects=_EFFECT),
    )(*arrays, send_sems, recv_sems, after)
    mine = 4 * lax.axis_index("x") + 2 * lax.axis_index("y") + lax.axis_index("c")
    own = out[:n] if gather else [lax.dynamic_index_in_dim(s, mine, 0, keepdims=False) for s in out[:n]]
    return [lax.dynamic_update_slice(g, o[None], (mine, 0, 0)) for g, o in zip(out[n:], own)]


def _forward_start(lands, *, name):
    n = len(lands)

    def body(*refs):
        land_refs, send_sems, recv_sems, token = refs[:n], refs[n], refs[n + 1], refs[-1]
        x, y, c = lax.axis_index("x"), lax.axis_index("y"), lax.axis_index("c")
        for a in range(n):
            for j, (px, py) in enumerate(((1 - x, y), (x, 1 - y), (1 - x, 1 - y))):
                blk = 4 * px + 2 * py + c
                pltpu.make_async_remote_copy(
                    src_ref=land_refs[a].at[blk], dst_ref=land_refs[a].at[blk], send_sem=send_sems.at[3 * a + j],
                    recv_sem=recv_sems.at[3 * a + j], device_id=(x, y, 1 - c), device_id_type=MESH).start()
        token[...] = jnp.zeros_like(token)

    sems = pltpu.SemaphoreType.DMA((3 * n,))
    out = pl.pallas_call(
        body, name=name,
        out_shape=(sems, sems) + tuple(pltpu.HBM(a.shape, a.dtype) for a in lands) + (jax.ShapeDtypeStruct((8, 128), F32),),
        in_specs=(_HBM,) * n, out_specs=(_SEM, _SEM) + (_HBM,) * n + (pl.BlockSpec(memory_space=pltpu.VMEM),),
        input_output_aliases={i: 2 + i for i in range(n)},
        compiler_params=pltpu.CompilerParams(has_side_effects=_EFFECT),
    )(*[pltpu.with_memory_space_constraint(a, pltpu.HBM) for a in lands])
    return out[:-1], out[-1]


def _forward_wait(state, after, *, name):
    send_sems, recv_sems = state[0], state[1]
    lands = state[2:]
    n = len(lands)

    def body(*refs):
        land_refs, send_sems, recv_sems = refs[:n], refs[n], refs[n + 1]
        x, y, c = lax.axis_index("x"), lax.axis_index("y"), lax.axis_index("c")
        for a in range(n):
            for j, (px, py) in enumerate(((1 - x, y), (x, 1 - y), (1 - x, 1 - y))):
                copy = pltpu.make_async_remote_copy(
                    src_ref=land_refs[a].at[4 * px + 2 * py + c], dst_ref=land_refs[a].at[4 * px + 2 * py + 1 - c],
                    send_sem=send_sems.at[3 * a + j], recv_sem=recv_sems.at[3 * a + j], device_id=(x, y, 1 - c),
                    device_id_type=MESH)
                copy.wait_send()
                copy.wait_recv()

    return pl.pallas_call(
        body, name=name,
        out_shape=tuple(pltpu.HBM(a.shape, a.dtype) for a in lands),
        in_specs=(_HBM,) * n + (_SEM, _SEM, pl.BlockSpec(memory_space=pl.ANY)), out_specs=(_HBM,) * n,
        input_output_aliases={i: i for i in range(n)},
        compiler_params=pltpu.CompilerParams(has_side_effects=_EFFECT),
    )(*lands, send_sems, recv_sems, after)


def _adam_math(w, g, m, v):
    m = ADAM_B1 * m + (1.0 - ADAM_B1) * g
    v = ADAM_B2 * v + (1.0 - ADAM_B2) * (g * g)
    m_hat = m / (1.0 - ADAM_B1 ** ADAM_STEP)
    v_hat = v / (1.0 - ADAM_B2 ** ADAM_STEP)
    delta = -ADAM_LR * (m_hat / (jnp.sqrt(v_hat) + ADAM_EPS) + ADAM_WD * w)
    return delta, m, v


def _adamw(parts, w, m, v, *, name):
    R, C = w.shape
    n_parts = parts.shape[0]
    tr = R // 2
    assert tr % 16 == 0

    def body(p_ref, w_ref, m_ref, v_ref, g_ref, d_ref, nm_ref, nv_ref):
        g = p_ref[0].astype(F32)
        for s in range(1, n_parts):
            g = g + p_ref[s].astype(F32)
        d, nm, nv = _adam_math(w_ref[...], g, m_ref[...], v_ref[...])
        g_ref[...] = g
        d_ref[...] = d
        nm_ref[...] = nm
        nv_ref[...] = nv

    blk = pl.BlockSpec((tr, C), lambda i: (i, 0))
    return pl.pallas_call(
        body, name=name, grid=(R // tr,),
        in_specs=[pl.BlockSpec((n_parts, tr, C), lambda i: (0, i, 0)), blk, blk, blk],
        out_specs=[blk] * 4, out_shape=[jax.ShapeDtypeStruct((R, C), F32)] * 4,
        compiler_params=_params(("arbitrary",)),
    )(parts, w, m, v)


def _adamw_small(parts, w, m, v):
    def body(p_ref, w_ref, m_ref, v_ref, g_ref, d_ref, nm_ref, nv_ref):
        g = p_ref[0]
        for s in range(1, N_DEV):
            g = g + p_ref[s]
        d, nm, nv = _adam_math(w_ref[...], g, m_ref[...], v_ref[...])
        g_ref[...] = g
        d_ref[...] = d
        nm_ref[...] = nm
        nv_ref[...] = nv

    vm = pl.BlockSpec(memory_space=pltpu.VMEM)
    return pl.pallas_call(
        body, name="adamw_small", in_specs=[vm] * 4, out_specs=[vm] * 4,
        out_shape=[jax.ShapeDtypeStruct((SMALL_ROWS, 128), F32)] * 4, compiler_params=_params(),
    )(parts, w, m, v)


def _t5_bucket(dist):
    max_exact = REL_BUCKETS // 2
    df = jnp.maximum(dist, 1).astype(F32)
    large = max_exact + (jnp.log(df / max_exact) / math.log(REL_MAX_DISTANCE / max_exact)
                         * (REL_BUCKETS - max_exact)).astype(jnp.int32)
    large = jnp.minimum(large, REL_BUCKETS - 1)
    return jnp.where(dist < max_exact, dist, large)


def _band_tables(rel_table, dil, n_back):
    qi = jnp.arange(BLK)[:, None]
    kj = jnp.arange(2 * BLK)[None, :]
    delta = BLK + qi - kj
    in_band = (delta >= 0) & (delta <= n_back)
    if rel_table is None:
        vals = jnp.zeros((N_HEADS, BLK, 2 * BLK), F32)
        bmap = None
    else:
        bucket = _t5_bucket(jnp.clip(delta, 0, n_back) * dil)
        vals = jnp.zeros((N_HEADS, BLK, 2 * BLK), F32)
        for b in range(REL_BUCKETS):
            vals = jnp.where((bucket == b)[None], rel_table[b][:, None, None], vals)
        bmap = jnp.where(in_band, bucket, -1).astype(jnp.int32)
    later = jnp.where(in_band[None], vals, NEG)
    first = jnp.where((in_band & (kj >= BLK))[None], vals, NEG)
    return jnp.stack([later, first]), bmap


def _rope_tables(T):
    half = HEAD_DIM // 2
    inv_freq = ROPE_THETA ** (-jnp.arange(half, dtype=F32) / half)
    ang = jnp.arange(T, dtype=F32)[:, None] * inv_freq[None, :]
    cos, sin = jnp.cos(ang), jnp.sin(ang)
    return jnp.tile(cos, (1, 4)), jnp.tile(jnp.concatenate([-sin, sin], axis=1), (1, 2))


def _widen_in(a, axis):
    sl = lambda lo, hi: lax.slice_in_dim(a, lo, hi, axis=axis)
    dup = lambda lo: [sl(lo, lo + 64), sl(lo, lo + 64), sl(lo + 64, lo + 128), sl(lo + 64, lo + 128)]
    return jnp.concatenate([sl(0, 512)] + dup(512) + dup(640) + [sl(768, D_IN)], axis=axis)


def _fold_in(a, axis):
    sl = lambda lo, hi: lax.slice_in_dim(a, lo, hi, axis=axis)
    fold = lambda lo: [sl(lo, lo + 64) + sl(lo + 64, lo + 128), sl(lo + 128, lo + 192) + sl(lo + 192, lo + 256)]
    return jnp.concatenate([sl(0, 512)] + fold(512) + fold(768) + [sl(1024, D_INP)], axis=axis)


def _local_step(x, tgt, g_attn, b_in, sinks, rel_table, g_out_a, g_out_b, g_ffn, g_final,
                win_fn, wo_fn, ffn_fn, early_fn):
    T = x.shape[0]
    cos, sin = _rope_tables(T)
    g_final2 = g_final.reshape(1, D_MODEL)
    sink8 = sinks.reshape(N_HEADS)

    bias_a, _ = _band_tables(None, 1, BLK - 1)
    tabs = [_band_tables(rel_table, dil, window // dil) for window, dil in BRANCHES]
    wint, token = win_fn(tabs[2][0])
    winp = _widen_in(wint, 0)
    binp = _widen_in(b_in, 1) + token[0, 0]

    h1, qa, ka, va, *qkv_b = _norm_proj(x, g_attn, winp, binp, cos, sin)
    qbs, kbs, vbs = qkv_b[0:3], qkv_b[3:6], qkv_b[6:9]
    oa, lse_a = _attn_fwd(qa, ka, va, bias_a, sink8, dil=1, kv_pairs=2, use_sink=True, name="attn_a_fwd")
    outs = [_attn_fwd(qbs[n], kbs[n], vbs[n], tabs[n][0], sink8, dil=dil, kv_pairs=4, use_sink=False,
                      name=f"attn_b{n}_fwd") for n, (_, dil) in enumerate(BRANCHES)]
    wo = wo_fn(outs[2][1])
    x2, mixed, h2, *ob_lse = _merge_wo(x, oa, outs[0][0], outs[1][0], outs[2][0], outs[0][1], outs[1][1], outs[2][1],
                                       g_out_a, g_out_b, wo, g_ffn)
    obs, lses = ob_lse[0:3], ob_lse[3:6]
    wgt, wut, wd = ffn_fn(h2)
    gate, up, act = _ffn_up(h2, wgt, wut)
    dx3, dx3b, loss, dg_final = _ffn_down_loss(act, wd, x2, tgt, g_final2)

    dgate, dup, dx2, dx2b, dg_ffn = _ffn_bwd(dx3, gate, up, wd, wgt, wut, x2, g_ffn)
    dwd = _matmul_tn(act, dx3b, tk=1408, tn=1024, name="dw_down")
    dwgt = _matmul_tn(dgate, h2, tk=1408, tn=1024, name="dw_gate")
    dwut = _matmul_tn(dup, h2, tk=1408, tn=1024, name="dw_up")
    doa, *dobs, dg_out_a, dg_out_b, dwo = _wo_bwd(dx2b, wo, oa, obs[0], g_out_a, g_out_b, mixed)
    early, token2 = early_fn(dict(w_o=dwo, w_gate=dwgt, w_up=dwut, w_down=dwd))
    sink8b = sink8 + token2[0, 0]

    dqa, dka, dva, _, dsk = _attn_bwd(qa, ka, va, oa, doa, lse_a, bias_a, sink8b, dil=1, kv_pairs=2, use_sink=True,
                                      name="attn_a_bwd", max_sub=4)
    res = [_attn_bwd(qbs[n], kbs[n], vbs[n], obs[n], dobs[n], lses[n], tabs[n][0], sink8b, dil=dil, kv_pairs=4,
                     use_sink=False, name=f"attn_b{n}_bwd") for n, (_, dil) in enumerate(BRANCHES)]
    dbp, grad_x, dg_attn, dwinp = _inproj_bwd(dqa, dka, dva, [r[0] for r in res], [r[1] for r in res],
                                              [r[2] for r in res], cos, sin, winp, x, dx2, g_attn, h1)
    dwin = _fold_in(dwinp, 0)
    drel, dsink = _bias_sink_grads([r[3] for r in res], [t[1] for t in tabs], dsk)

    small = dict(
        g_attn=dg_attn, b_in=_fold_in(dbp, 1), sinks=dsink[:, ::HEAD_DIM], rel_table=drel[:, :REL_BUCKETS].T,
        g_out_a=dg_out_a, g_out_b=dg_out_b, g_ffn=dg_ffn, g_final=dg_final.reshape(D_MODEL))
    return loss[0, 0], grad_x, dwin, early, small


SMALL_NAMES = ("g_attn", "b_in", "sinks", "rel_table", "g_out_a", "g_out_b", "g_ffn", "g_final", "loss")


def _pack_small(vals):
    flat = jnp.concatenate([vals[n].reshape(-1).astype(F32) for n in SMALL_NAMES])
    return jnp.pad(flat, (0, SMALL_ROWS * 128 - flat.shape[0])).reshape(SMALL_ROWS, 128)


def _unpack_small(packed, like):
    flat = packed.reshape(-1)
    out, off = {}, 0
    for n in SMALL_NAMES:
        size = like[n].size
        out[n] = flat[off:off + size].reshape(like[n].shape)
        off += size
    return out


def kernel(x, g_attn, w_in, b_in, sinks, rel_table, g_out_a, g_out_b, w_o, g_ffn, w_gate, w_up, w_down, g_final, loss_target, m_g_attn, m_w_in, m_b_in, m_sinks, m_rel_table, m_g_out_a, m_g_out_b, m_w_o, m_g_ffn, m_w_gate, m_w_up, m_w_down, m_g_final, v_g_attn, v_w_in, v_b_in, v_sinks, v_rel_table, v_g_out_a, v_g_out_b, v_w_o, v_g_ffn, v_w_gate, v_w_up, v_w_down, v_g_final):
    rest_names = ("w_o", "w_gate", "w_up", "w_down")

    rest = [w_o[0].astype(BF16), w_gate[0].astype(BF16).T, w_up[0].astype(BF16).T, w_down[0].astype(BF16)]
    in_state, _ = _exchange_start([w_in[0].astype(BF16).T], gather=True, near=True, name="gather_w_in_start")
    later = {}

    def whole(got):
        return [g.reshape(N_DEV * g.shape[1], D_MODEL) for g in got]

    def win_fn(after):
        near = _exchange_wait(in_state, after, gather=True, near=True, name="gather_w_in_near")
        fwd_state, tok = _forward_start(near, name="gather_w_in_forward")
        wint = whole(_forward_wait(fwd_state, tok, name="gather_w_in_wait"))[0]
        wint, src = lax.optimization_barrier((wint, rest))
        later["wo"], token_o = _exchange_start(src[:1], gather=True, name="gather_w_o_start")
        token_o, ffn_src = lax.optimization_barrier((token_o, src[1:]))
        later["ffn"], token = _exchange_start(ffn_src, gather=True, name="gather_ffn_start")
        return wint, token + token_o

    def wo_fn(after):
        return whole(_exchange_wait(later["wo"], after, gather=True, name="gather_w_o_wait"))[0]

    def ffn_fn(after):
        return whole(_exchange_wait(later["ffn"], after, gather=True, name="gather_ffn_wait"))

    def early_fn(dws):
        return _exchange_start([dws[n].reshape(N_DEV, -1, D_MODEL) for n in rest_names], gather=False,
                               name="scatter_rest_start")

    loss_part, grad_x, dwint, early_state, small = _local_step(
        x[0], loss_target[0], g_attn, b_in, sinks, rel_table, g_out_a, g_out_b, g_ffn, g_final,
        win_fn, wo_fn, ffn_fn, early_fn)
    parts_in = dwint.astype(BF16).reshape(N_DEV, D_IN // N_DEV, D_MODEL)
    in_state, token3 = _exchange_start([parts_in], gather=False, name="scatter_w_in_start")
    got = _exchange_wait(early_state, token3, gather=False, name="scatter_rest_wait")

    def update(n, parts, w, m, v, transposed):
        if transposed:
            return [a.T[None] for a in _adamw(parts, w[0].T, m[0].T, v[0].T, name="adamw_" + n)]
        return [a[None] for a in _adamw(parts, w[0], m[0], v[0], name="adamw_" + n)]

    big = dict(w_o=update("w_o", got[0], w_o, m_w_o, v_w_o, False),
               w_gate=update("w_gate", got[1], w_gate, m_w_gate, v_w_gate, True),
               w_up=update("w_up", got[2], w_up, m_w_up, v_w_up, True),
               w_down=update("w_down", got[3], w_down, m_w_down, v_w_down, False))

    unused = jnp.zeros((1,), F32)
    ws = dict(g_attn=g_attn, b_in=b_in, sinks=sinks, rel_table=rel_table, g_out_a=g_out_a, g_out_b=g_out_b,
              g_ffn=g_ffn, g_final=g_final, loss=unused)
    ms = dict(g_attn=m_g_attn, b_in=m_b_in, sinks=m_sinks, rel_table=m_rel_table, g_out_a=m_g_out_a,
              g_out_b=m_g_out_b, g_ffn=m_g_ffn, g_final=m_g_final, loss=unused)
    vs = dict(g_attn=v_g_attn, b_in=v_b_in, sinks=v_sinks, rel_table=v_rel_table, g_out_a=v_g_out_a,
              g_out_b=v_g_out_b, g_ffn=v_g_ffn, g_final=v_g_final, loss=unused)
    sparts = _all_gather(_pack_small(dict(small, loss=loss_part)), name="gather_small")
    sm_packed = _adamw_small(sparts, _pack_small(ws), _pack_small(ms), _pack_small(vs))
    sm = [_unpack_small(a, ws) for a in sm_packed]
    loss = sm[0]["loss"][0]

    done = sm_packed[1][:1, :1] + sum(big[n][1][0, :1, :1] for n in rest_names)
    got_in = _exchange_wait(in_state, done, gather=False, name="scatter_w_in_wait")[0]
    big["w_in"] = update("w_in", got_in, w_in, m_w_in, v_w_in, True)

    order = ("g_attn", "w_in", "b_in", "sinks", "rel_table", "g_out_a", "g_out_b", "w_o", "g_ffn", "w_gate", "w_up",
             "w_down", "g_final")
    outs = [loss, grad_x[None]]
    for k in range(4):
        outs += [big[n][k] if n in big else sm[k][n] for n in order]
    return tuple(outs)
```

```python
import math

import jax
import jax.numpy as jnp
from jax import lax
from jax.experimental import pallas as pl
from jax.experimental.pallas import tpu as pltpu

F32 = jnp.float32
BF16 = jnp.bfloat16

N_DEV = 8
D_MODEL = 1024
HEAD_DIM = 64
N_HEADS = 8
PAIR = 2 * HEAD_DIM
WIDTH = N_HEADS * HEAD_DIM
D_IN = 2304
D_INP = 2560
D_FF = 2816
BLK = 128
ROPE_THETA = 150000.0
REL_BUCKETS = 32
REL_MAX_DISTANCE = 2048
EPS = 1e-5
NEG = -1e30
BRANCHES = ((128, 1), (512, 4), (2048, 16))
Q_SCALE = HEAD_DIM ** -0.5

ADAM_LR = 0.001
ADAM_B1 = 0.9
ADAM_B2 = 0.999
ADAM_EPS = 1e-08
ADAM_WD = 0.01
ADAM_STEP = 10

VMEM_LIMIT = 56 * 1024 * 1024
MESH = pl.DeviceIdType.MESH

NT = (((1,), (1,)), ((), ()))
TN = (((0,), (0,)), ((), ()))

SMALL_ROWS = 56


def _params(sem=None):
    return pltpu.CompilerParams(dimension_semantics=sem, vmem_limit_bytes=VMEM_LIMIT)


def _sigmoid(x):
    return 1.0 / (1.0 + jnp.exp2(x * (-1.0 / math.log(2.0))))


def _rms_bwd(dh, xh, r, g):
    u = dh * g
    return r * (u - xh * jnp.mean(u * xh, axis=-1, keepdims=True))


def _rope_rot(t, first):
    return jnp.where(first, pltpu.roll(t, 96, 1), pltpu.roll(t, 32, 1))


N_CHUNK = WIDTH // PAIR


def _scr(tm):
    return pltpu.VMEM((N_CHUNK, tm, PAIR), F32)


def _scr_get(scr):
    return jnp.concatenate([scr[j] for j in range(N_CHUNK)], axis=1)


def _scr_put(scr, val):
    for j in range(N_CHUNK):
        scr[j] = val[:, j * PAIR:(j + 1) * PAIR]


def _unstride(view_ref, scr, dil, tm):
    n = tm // dil
    chunks = scr.shape[0]
    for r in range(dil):
        for j in range(chunks):
            col = (r * chunks + j) * PAIR
            scr.at[j][pl.ds(r, n, stride=dil), :] = view_ref[:, col:col + PAIR].astype(F32)


def _restride(scr, out_ref, dil, tm):
    n = tm // dil
    chunks = scr.shape[0]
    for r in range(dil):
        for j in range(chunks):
            col = (r * chunks + j) * PAIR
            rows = scr[j] if dil == 1 else scr.at[j][pl.ds(r, n, stride=dil), :]
            out_ref[:, col:col + PAIR] = rows.astype(out_ref.dtype)


def _view_specs(tm, width=WIDTH):
    return [pl.BlockSpec((tm // dil, dil * width), lambda i: (i, 0)) for _, dil in BRANCHES]


def _view_shapes(T, dtype, width=WIDTH):
    return [jax.ShapeDtypeStruct((T // dil, dil * width), dtype) for _, dil in BRANCHES]


def _norm_proj(x, g, w, b, cos, sin, *, tm=1024):
    T = x.shape[0]

    def body(x_ref, g_ref, w_ref, b_ref, cos_ref, sin_ref, h_ref, qa_ref, ka_ref, va_ref, *rest):
        outs_b, ys = rest[:9], rest[9]
        xv = x_ref[...]
        r = lax.rsqrt(jnp.mean(xv * xv, axis=-1, keepdims=True) + EPS)
        h = (xv * r * g_ref[...]).astype(BF16)
        h_ref[...] = h
        cosv = cos_ref[...]
        sinv = sin_ref[...]
        lane = lax.broadcasted_iota(jnp.int32, (tm, PAIR), 1)
        first = (lane % HEAD_DIM) < (HEAD_DIM // 2)

        def proj(off):
            return (lax.dot_general(h, w_ref[off:off + 256, :], NT, preferred_element_type=F32)
                    + b_ref[:, off:off + 256])

        for (off, width, rot, scale), o_ref in zip(((0, 512, True, Q_SCALE), (512, 256, True, 1.0), (768, 256, False, 1.0)),
                                                   (qa_ref, ka_ref, va_ref)):
            for c in range(0, width, 256):
                y = proj(off + c)
                for j in range(0, 256, PAIR):
                    t = y[:, j:j + PAIR]
                    if rot:
                        t = t * cosv + _rope_rot(t, first) * sinv
                    if scale != 1.0:
                        t = t * scale
                    o_ref[:, c + j:c + j + PAIR] = t.astype(BF16)
        for n, (off, scale) in enumerate(((1024, Q_SCALE), (1536, 1.0), (2048, 1.0))):
            for c in range(0, WIDTH, 256):
                y = proj(off + c)
                y = y * scale if scale != 1.0 else y
                for j in range(0, 256, PAIR):
                    ys[(c + j) // PAIR] = y[:, j:j + PAIR]
            for (_, dil), o_ref in zip(BRANCHES, outs_b[3 * n:3 * n + 3]):
                _restride(ys, o_ref, dil, tm)

    row = lambda w_: pl.BlockSpec((tm, w_), lambda i: (i, 0))
    full = lambda a: pl.BlockSpec(a.shape, lambda i: (0, 0))
    return pl.pallas_call(
        body, name="norm_proj", grid=(T // tm,),
        in_specs=[row(D_MODEL), full(g), full(w), full(b), row(PAIR), row(PAIR)],
        out_specs=[row(D_MODEL), row(512), row(256), row(256)] + _view_specs(tm) * 3,
        out_shape=[jax.ShapeDtypeStruct((T, n), BF16) for n in (D_MODEL, 512, 256, 256)] + _view_shapes(T, BF16) * 3,
        scratch_shapes=[_scr(tm)],
        compiler_params=_params(("arbitrary",)),
    )(x, g, w, b, cos, sin)


MAX_SUB = 8
AHEAD = 2


def _attn_specs(kvw, sub):
    q_spec = pl.BlockSpec((sub * BLK, WIDTH), lambda r, i: (i, r))
    kc_spec = pl.BlockSpec((sub * BLK, kvw), lambda r, i: (i, r))
    kp_spec = pl.BlockSpec((BLK, kvw), lambda r, i: (jnp.maximum(sub * i - 1, 0), r))
    b_spec = pl.BlockSpec((2, N_HEADS, BLK, 2 * BLK), lambda r, i: (0, 0, 0, 0))
    return q_spec, kp_spec, kc_spec, b_spec


def _window(prev_ref, cur_ref, j, ksl):
    before = prev_ref[:, ksl] if j == 0 else cur_ref[(j - 1) * BLK:j * BLK, ksl]
    return jnp.concatenate([before, cur_ref[j * BLK:(j + 1) * BLK, ksl]], axis=0)


def _attn_fwd(q, k, v, bias, sinks, *, dil, kv_pairs, use_sink, name):
    L = q.shape[0]
    sub = min(MAX_SUB, L // BLK)
    ns = L // (sub * BLK)
    kvw = kv_pairs * PAIR
    rep = 4 // kv_pairs

    def body(sink_ref, q_ref, kp_ref, kc_ref, vp_ref, vc_ref, b_ref, o_ref, lse_ref):
        lane = lax.broadcasted_iota(jnp.int32, (1, PAIR), 1)
        lo = lane < HEAD_DIM
        first = jnp.where(pl.program_id(1) == 0, 1, 0)
        def scores(j, hp):
            rows = slice(j * BLK, (j + 1) * BLK)
            sl = slice(hp * PAIR, (hp + 1) * PAIR)
            ksl = slice((hp // rep) * PAIR, (hp // rep + 1) * PAIR)
            qp = q_ref[rows, sl]
            kk = _window(kp_ref, kc_ref, j, ksl)
            vv = _window(vp_ref, vc_ref, j, ksl)
            heads = []
            for e in range(2):
                h = 2 * hp + e
                msk = lo if e == 0 else jnp.logical_not(lo)
                qm = jnp.where(msk, qp, jnp.zeros_like(qp))
                s = lax.dot_general(qm, kk, NT, preferred_element_type=F32) + (b_ref[first, h] if j == 0 else b_ref[0, h])
                heads.append((h, msk, s))
            return rows, sl, vv, heads

        def outputs(rows, sl, vv, heads):
            o_pair = None
            lse_pair = None
            for h, msk, s in heads:
                m = jnp.max(s, axis=-1, keepdims=True)
                if use_sink:
                    sk = sink_ref[h]
                    m = jnp.maximum(m, sk)
                p = jnp.exp(s - m)
                l = jnp.sum(p, axis=-1, keepdims=True)
                if use_sink:
                    l = l + jnp.exp(sk - m)
                vm = jnp.where(msk, vv, jnp.zeros_like(vv))
                oe = jnp.dot(p.astype(BF16), vm, preferred_element_type=F32) * (1.0 / l)
                ls = m + jnp.log(l)
                if o_pair is None:
                    o_pair = oe
                    lse_pair = jnp.broadcast_to(ls, (BLK, PAIR))
                else:
                    o_pair = o_pair + oe
                    lse_pair = jnp.where(lo, lse_pair, ls)
            o_ref[rows, sl] = o_pair.astype(BF16)
            lse_ref[rows, sl] = lse_pair

        items = [(j, hp) for j in range(sub) for hp in range(4)]
        queue = [scores(*it) for it in items[:AHEAD]]
        for n in range(len(items)):
            if n + AHEAD < len(items):
                queue.append(scores(*items[n + AHEAD]))
            outputs(*queue.pop(0))

    q_spec, kp_spec, kc_spec, b_spec = _attn_specs(kvw, sub)
    return pl.pallas_call(
        body, name=name, grid=(dil, ns),
        in_specs=[pl.BlockSpec(memory_space=pltpu.SMEM), q_spec, kp_spec, kc_spec, kp_spec, kc_spec, b_spec],
        out_specs=[q_spec, q_spec],
        out_shape=[jax.ShapeDtypeStruct((L, dil * WIDTH), BF16), jax.ShapeDtypeStruct((L, dil * WIDTH), F32)],
        compiler_params=_params(("arbitrary", "arbitrary")),
    )(sinks, q, k, k, v, v, bias)


def _attn_bwd(q, k, v, o, do, lse, bias, sinks, *, dil, kv_pairs, use_sink, name, max_sub=MAX_SUB):
    L = q.shape[0]
    sub = min(max_sub, L // BLK)
    ns = L // (sub * BLK)
    n_steps = dil * ns
    kvw = kv_pairs * PAIR
    rep = 4 // kv_pairs
    last = slice((sub - 1) * BLK, sub * BLK)

    def body(sink_ref, q_ref, kp_ref, kc_ref, vp_ref, vc_ref, o_ref, do_ref, lse_ref, b_ref,
             dq_ref, dk_ref, dv_ref, dsum_ref, dsk_ref, pk_ref, pv_ref):
        t = pl.program_id(0)
        i = t % ns

        @pl.when(t == 0)
        def _():
            dsum_ref[...] = jnp.zeros_like(dsum_ref)
            dsk_ref[...] = jnp.zeros_like(dsk_ref)
            pk_ref[...] = jnp.zeros_like(pk_ref)
            pv_ref[...] = jnp.zeros_like(pv_ref)

        @pl.when(t < n_steps)
        def _():
            lo = lax.broadcasted_iota(jnp.int32, (1, PAIR), 1) < HEAD_DIM
            first = jnp.where(i == 0, 1, 0)
            dks = [[None] * kv_pairs for _ in range(sub)]
            dvs = [[None] * kv_pairs for _ in range(sub)]
            def scores(j, hp):
                rows = slice(j * BLK, (j + 1) * BLK)
                kvp = hp // rep
                sl = slice(hp * PAIR, (hp + 1) * PAIR)
                ksl = slice(kvp * PAIR, (kvp + 1) * PAIR)
                qp = q_ref[rows, sl]
                dop = do_ref[rows, sl]
                prod = dop.astype(F32) * o_ref[rows, sl].astype(F32)
                kk = _window(kp_ref, kc_ref, j, ksl)
                vv = _window(vp_ref, vc_ref, j, ksl)
                heads = []
                for e in range(2):
                    h = 2 * hp + e
                    msk = lo if e == 0 else jnp.logical_not(lo)
                    qm = jnp.where(msk, qp, jnp.zeros_like(qp))
                    dom = jnp.where(msk, dop, jnp.zeros_like(dop))
                    km = jnp.where(msk, kk, jnp.zeros_like(kk))
                    s = (lax.dot_general(qm, kk, NT, preferred_element_type=F32)
                         + (b_ref[first, h] if j == 0 else b_ref[0, h]))
                    dp = lax.dot_general(dom, vv, NT, preferred_element_type=F32)
                    heads.append((h, msk, qm, dom, km, s, dp))
                return j, rows, kvp, sl, prod, heads

            def grads(j, rows, kvp, sl, prod, heads):
                dq_pair = None
                c_pair = None
                qms, doms, dsbs, pbs = [], [], [], []
                for h, msk, qm, dom, km, s, dp in heads:
                    ls = lse_ref[rows, h * HEAD_DIM:h * HEAD_DIM + 1]
                    p = jnp.exp(s - ls)
                    delta = jnp.sum(jnp.where(msk, prod, 0.0), axis=-1, keepdims=True)
                    ds = p * (dp - delta)
                    if use_sink:
                        ce = jnp.exp(sink_ref[h] - ls) * delta
                        c_pair = jnp.broadcast_to(ce, (BLK, PAIR)) if c_pair is None else jnp.where(msk, ce, c_pair)
                    else:
                        dsum_ref[h] += ds
                    dsb = ds.astype(BF16)
                    dqe = jnp.dot(dsb, km, preferred_element_type=F32)
                    dq_pair = dqe if dq_pair is None else dq_pair + dqe
                    qms.append(qm)
                    doms.append(dom)
                    dsbs.append(dsb)
                    pbs.append(p.astype(BF16))
                dke = lax.dot_general(jnp.concatenate(dsbs, axis=0), jnp.concatenate(qms, axis=0), TN,
                                      preferred_element_type=F32)
                dve = lax.dot_general(jnp.concatenate(pbs, axis=0), jnp.concatenate(doms, axis=0), TN,
                                      preferred_element_type=F32)
                dks[j][kvp] = dke if dks[j][kvp] is None else dks[j][kvp] + dke
                dvs[j][kvp] = dve if dvs[j][kvp] is None else dvs[j][kvp] + dve
                dq_ref[rows, sl] = (dq_pair * Q_SCALE).astype(BF16)
                if use_sink:
                    dsk_ref[:, sl] += c_pair

            items = [(j, hp) for j in range(sub) for hp in range(4)]
            ahead = AHEAD + 1 if use_sink else AHEAD
            queue = [scores(*it) for it in items[:ahead]]
            for n in range(len(items)):
                if n + ahead < len(items):
                    queue.append(scores(*items[n + ahead]))
                grads(*queue.pop(0))
            for kvp in range(kv_pairs):
                ksl = slice(kvp * PAIR, (kvp + 1) * PAIR)
                for pend_ref, out_ref, parts in ((pk_ref, dk_ref, [d[kvp] for d in dks]),
                                                 (pv_ref, dv_ref, [d[kvp] for d in dvs])):
                    if sub > 1:
                        out_ref[:(sub - 1) * BLK, ksl] = pend_ref[:(sub - 1) * BLK, ksl].astype(BF16)
                    out_ref[last, ksl] = (pend_ref[last, ksl] + parts[0][:BLK]).astype(BF16)
                    for j in range(sub):
                        own = parts[j][BLK:]
                        pend_ref[j * BLK:(j + 1) * BLK, ksl] = own + parts[j + 1][:BLK] if j + 1 < sub else own

        @pl.when(t == n_steps)
        def _():
            dk_ref[...] = pk_ref[...].astype(BF16)
            dv_ref[...] = pv_ref[...].astype(BF16)

    def at(t):
        t = jnp.minimum(t, n_steps - 1)
        return t % ns, t // ns

    def before(t):
        return at(jnp.maximum(t - 1, 0))

    q_spec = pl.BlockSpec((sub * BLK, WIDTH), at)
    kc_spec = pl.BlockSpec((sub * BLK, kvw), at)
    kp_spec = pl.BlockSpec((BLK, kvw), lambda t: (jnp.maximum(sub * at(t)[0] - 1, 0), at(t)[1]))
    b_spec = pl.BlockSpec((2, N_HEADS, BLK, 2 * BLK), lambda t: (0, 0, 0, 0))
    dkv_spec = pl.BlockSpec((sub * BLK, kvw), before)
    return pl.pallas_call(
        body, name=name, grid=(n_steps + 1,),
        in_specs=[pl.BlockSpec(memory_space=pltpu.SMEM), q_spec, kp_spec, kc_spec, kp_spec, kc_spec,
                  q_spec, q_spec, q_spec, b_spec],
        out_specs=[q_spec, dkv_spec, dkv_spec,
                   pl.BlockSpec((N_HEADS, BLK, 2 * BLK), lambda t: (0, 0, 0)),
                   pl.BlockSpec((BLK, WIDTH), lambda t: (0, 0))],
        out_shape=[jax.ShapeDtypeStruct((L, dil * WIDTH), BF16),
                   jax.ShapeDtypeStruct((L, dil * kvw), BF16),
                   jax.ShapeDtypeStruct((L, dil * kvw), BF16),
                   jax.ShapeDtypeStruct((N_HEADS, BLK, 2 * BLK), F32),
                   jax.ShapeDtypeStruct((BLK, WIDTH), F32)],
        scratch_shapes=[pltpu.VMEM((sub * BLK, kvw), F32), pltpu.VMEM((sub * BLK, kvw), F32)],
        compiler_params=_params(("arbitrary",)),
    )(sinks, q, k, k, v, v, o, do, lse, bias)


def _merge_wo(x, oa, o1, o2, o3, l1, l2, l3, ga, gb, wo, gf, *, tm=512):
    T = x.shape[0]

    def body(x_ref, oa_ref, o1_ref, o2_ref, o3_ref, l1_ref, l2_ref, l3_ref, ga_ref, gb_ref, wo_ref, gf_ref,
             x2_ref, mix_ref, h2_ref, ob1_ref, ob4_ref, ob16_ref, ls1_ref, ls4_ref, ls16_ref, so2, so3, sl2, sl3):
        _unstride(o2_ref, so2, BRANCHES[1][1], tm)
        _unstride(o3_ref, so3, BRANCHES[2][1], tm)
        _unstride(l2_ref, sl2, BRANCHES[1][1], tm)
        _unstride(l3_ref, sl3, BRANCHES[2][1], tm)
        la, lb, lc = l1_ref[...], _scr_get(sl2), _scr_get(sl3)
        m = jnp.maximum(jnp.maximum(la, lb), lc)
        ea, eb, ec = jnp.exp(la - m), jnp.exp(lb - m), jnp.exp(lc - m)
        den = ea + eb + ec
        inv = 1.0 / den
        ob = (ea * o1_ref[...].astype(F32) + eb * _scr_get(so2) + ec * _scr_get(so3)) * inv
        _scr_put(so2, ob)
        _scr_put(sl2, m + jnp.log(den))
        for (_, dil), o_ref, l_ref in zip(BRANCHES, (ob1_ref, ob4_ref, ob16_ref), (ls1_ref, ls4_ref, ls16_ref)):
            _restride(so2, o_ref, dil, tm)
            _restride(sl2, l_ref, dil, tm)
        oav = oa_ref[...].astype(F32)
        ra = lax.rsqrt(jnp.mean(oav * oav, axis=-1, keepdims=True) + EPS)
        rb = lax.rsqrt(jnp.mean(ob * ob, axis=-1, keepdims=True) + EPS)
        mix_ref[:, :WIDTH] = (oav * ra * ga_ref[...]).astype(BF16)
        mix_ref[:, WIDTH:] = (ob * rb * gb_ref[...]).astype(BF16)
        x2 = x_ref[...] + jnp.dot(mix_ref[...], wo_ref[...], preferred_element_type=F32)
        x2_ref[...] = x2
        r2 = lax.rsqrt(jnp.mean(x2 * x2, axis=-1, keepdims=True) + EPS)
        h2_ref[...] = (x2 * r2 * gf_ref[...]).astype(BF16)

    row = lambda w_: pl.BlockSpec((tm, w_), lambda i: (i, 0))
    full = lambda a: pl.BlockSpec(a.shape, lambda i: (0, 0))
    return pl.pallas_call(
        body, name="merge_wo", grid=(T // tm,),
        in_specs=[row(D_MODEL), row(WIDTH)] + _view_specs(tm) * 2 + [full(ga), full(gb), full(wo), full(gf)],
        out_specs=[row(D_MODEL), row(D_MODEL), row(D_MODEL)] + _view_specs(tm) * 2,
        out_shape=[jax.ShapeDtypeStruct((T, D_MODEL), F32), jax.ShapeDtypeStruct((T, D_MODEL), BF16),
                   jax.ShapeDtypeStruct((T, D_MODEL), BF16)] + _view_shapes(T, BF16) + _view_shapes(T, F32),
        scratch_shapes=[_scr(tm)] * 4,
        compiler_params=_params(("arbitrary",)),
    )(x, oa, o1, o2, o3, l1, l2, l3, ga, gb, wo, gf)


def _ffn_up(h2, wgt, wut, *, tm=512, fc=D_FF, rc=512, cc=256):
    T = h2.shape[0]

    def body(h_ref, wg_ref, wu_ref, gate_ref, up_ref, act_ref):
        for s in range(0, tm, rc):
            h = h_ref[s:s + rc, :]
            for c in range(0, fc, cc):
                gt = lax.dot_general(h, wg_ref[c:c + cc, :], NT, preferred_element_type=F32)
                u = lax.dot_general(h, wu_ref[c:c + cc, :], NT, preferred_element_type=F32)
                gate_ref[s:s + rc, c:c + cc] = gt.astype(BF16)
                up_ref[s:s + rc, c:c + cc] = u.astype(BF16)
                act_ref[s:s + rc, c:c + cc] = (gt * _sigmoid(gt) * u).astype(BF16)

    rowd = pl.BlockSpec((tm, D_MODEL), lambda i, c: (i, 0))
    wrow = pl.BlockSpec((fc, D_MODEL), lambda i, c: (c, 0))
    oc = pl.BlockSpec((tm, fc), lambda i, c: (i, c))
    return pl.pallas_call(
        body, name="ffn_up", grid=(T // tm, D_FF // fc),
        in_specs=[rowd, wrow, wrow],
        out_specs=[oc, oc, oc],
        out_shape=[jax.ShapeDtypeStruct((T, D_FF), BF16)] * 3,
        compiler_params=_params(("arbitrary", "arbitrary")),
    )(h2, wgt, wut)


def _ffn_down_loss(act, wd, x2, tgt, g, *, tm=1024, rc=256):
    T = x2.shape[0]

    def body(act_ref, wd_ref, x2_ref, tgt_ref, g_ref, dx_ref, dxb_ref, loss_ref, dg_ref):
        @pl.when(pl.program_id(0) == 0)
        def _():
            loss_ref[...] = jnp.zeros_like(loss_ref)
            dg_ref[...] = jnp.zeros_like(dg_ref)

        gv = g_ref[...]
        lsum = jnp.zeros((1, 1), F32)
        dgs = jnp.zeros((1, D_MODEL), F32)
        for c in range(0, tm, rc):
            x3 = x2_ref[c:c + rc, :] + jnp.dot(act_ref[c:c + rc, :], wd_ref[...], preferred_element_type=F32)
            r = lax.rsqrt(jnp.mean(x3 * x3, axis=-1, keepdims=True) + EPS)
            xh = x3 * r
            diff = xh * gv - tgt_ref[c:c + rc, :]
            lsum = lsum + jnp.sum(jnp.sum(diff * diff, axis=-1, keepdims=True), axis=0, keepdims=True)
            dy = diff * (1.0 / D_MODEL)
            dgs = dgs + jnp.sum(dy * xh, axis=0, keepdims=True)
            dx = _rms_bwd(dy, xh, r, gv)
            dx_ref[c:c + rc, :] = dx
            dxb_ref[c:c + rc, :] = dx.astype(BF16)
        loss_ref[...] += lsum * (0.5 / D_MODEL)
        dg_ref[...] += dgs

    rowd = pl.BlockSpec((tm, D_MODEL), lambda i: (i, 0))
    return pl.pallas_call(
        body, name="ffn_down_loss", grid=(T // tm,),
        in_specs=[pl.BlockSpec((tm, D_FF), lambda i: (i, 0)),
                  pl.BlockSpec((D_FF, D_MODEL), lambda i: (0, 0), pipeline_mode=pl.Buffered(1)),
                  rowd, rowd, pl.BlockSpec(g.shape, lambda i: (0, 0))],
        out_specs=[rowd, rowd, pl.BlockSpec((1, 1), lambda i: (0, 0)), pl.BlockSpec((1, D_MODEL), lambda i: (0, 0))],
        out_shape=[jax.ShapeDtypeStruct((T, D_MODEL), F32), jax.ShapeDtypeStruct((T, D_MODEL), BF16),
                   jax.ShapeDtypeStruct((1, 1), F32), jax.ShapeDtypeStruct((1, D_MODEL), F32)],
        compiler_params=_params(("arbitrary",)),
    )(act, wd, x2, tgt, g)


def _ffn_bwd(dx3, gate, up, wd, wgt, wut, x2, g, *, tm=256, cc=256):
    T = x2.shape[0]

    def body(dx_ref, gate_ref, up_ref, wd_ref, wg_ref, wu_ref, x2_ref, g_ref,
             dgate_ref, dup_ref, dx2_ref, dx2b_ref, dg_ref):
        @pl.when(pl.program_id(0) == 0)
        def _():
            dg_ref[...] = jnp.zeros_like(dg_ref)

        dxb = dx_ref[...].astype(BF16)
        for c in range(0, D_FF, cc):
            dact = lax.dot_general(dxb, wd_ref[c:c + cc, :], NT, preferred_element_type=F32)
            gt = gate_ref[:, c:c + cc].astype(F32)
            u = up_ref[:, c:c + cc].astype(F32)
            sg = _sigmoid(gt)
            a = dact * sg
            dgate_ref[:, c:c + cc] = (a * u * ((1.0 + gt) - gt * sg)).astype(BF16)
            dup_ref[:, c:c + cc] = (a * gt).astype(BF16)
        dh = (jnp.dot(dgate_ref[...], wg_ref[...], preferred_element_type=F32)
              + jnp.dot(dup_ref[...], wu_ref[...], preferred_element_type=F32))
        xv = x2_ref[...]
        r = lax.rsqrt(jnp.mean(xv * xv, axis=-1, keepdims=True) + EPS)
        xh = xv * r
        dg_ref[...] += jnp.sum(dh * xh, axis=0, keepdims=True)
        d = dx_ref[...] + _rms_bwd(dh, xh, r, g_ref[...])
        dx2_ref[...] = d
        dx2b_ref[...] = d.astype(BF16)

    rowd = pl.BlockSpec((tm, D_MODEL), lambda i: (i, 0))
    rowf = pl.BlockSpec((tm, D_FF), lambda i: (i, 0))
    wfull = pl.BlockSpec((D_FF, D_MODEL), lambda i: (0, 0), pipeline_mode=pl.Buffered(1))
    return pl.pallas_call(
        body, name="ffn_bwd", grid=(T // tm,),
        in_specs=[rowd, rowf, rowf, wfull, wfull, wfull, rowd, pl.BlockSpec(g.shape, lambda i: (0, 0))],
        out_specs=[rowf, rowf, rowd, rowd, pl.BlockSpec((1, D_MODEL), lambda i: (0, 0))],
        out_shape=[jax.ShapeDtypeStruct((T, D_FF), BF16), jax.ShapeDtypeStruct((T, D_FF), BF16),
                   jax.ShapeDtypeStruct((T, D_MODEL), F32), jax.ShapeDtypeStruct((T, D_MODEL), BF16),
                   jax.ShapeDtypeStruct((1, D_MODEL), F32)],
        compiler_params=_params(("arbitrary",)),
    )(dx3, gate, up, wd, wgt, wut, x2, g)


def _matmul_tn(a, b, *, tk, tn, tt=2048, out_dtype=BF16, name):
    T, K = a.shape
    N = b.shape[1]
    nt = T // tt

    def body(a_ref, b_ref, o_ref, acc_ref):
        part = lax.dot_general(a_ref[...], b_ref[...], TN, preferred_element_type=F32)

        @pl.when(pl.program_id(2) == 0)
        def _():
            acc_ref[...] = part

        @pl.when(pl.program_id(2) > 0)
        def _():
            acc_ref[...] += part

        @pl.when(pl.program_id(2) == nt - 1)
        def _():
            o_ref[...] = acc_ref[...].astype(out_dtype)

    return pl.pallas_call(
        body, name=name, grid=(K // tk, N // tn, nt),
        in_specs=[pl.BlockSpec((tt, tk), lambda i, j, t: (t, i)), pl.BlockSpec((tt, tn), lambda i, j, t: (t, j))],
        out_specs=pl.BlockSpec((tk, tn), lambda i, j, t: (i, j)),
        out_shape=jax.ShapeDtypeStruct((K, N), out_dtype),
        scratch_shapes=[pltpu.VMEM((tk, tn), F32)],
        compiler_params=_params(("arbitrary", "arbitrary", "arbitrary")),
    )(a, b)


def _wo_bwd(dx2b, wo, oa, ob, ga, gb, mixed, *, tm=1024, wc=256):
    T = dx2b.shape[0]
    n_tiles = T // tm

    def body(dx_ref, wo_ref, oa_ref, ob_ref, ga_ref, gb_ref, mix_ref,
             doa_ref, dob1_ref, dob4_ref, dob16_ref, dga_ref, dgb_ref, dwo_ref, scr, dw_acc):
        @pl.when(pl.program_id(0) == 0)
        def _():
            dga_ref[...] = jnp.zeros_like(dga_ref)
            dgb_ref[...] = jnp.zeros_like(dgb_ref)
            dw_acc[...] = jnp.zeros_like(dw_acc)

        dxv = dx_ref[...]
        for c in range(0, D_MODEL, wc):
            dw_acc[c:c + wc, :] += lax.dot_general(mix_ref[:, c:c + wc], dxv, TN, preferred_element_type=F32)

        @pl.when(pl.program_id(0) == n_tiles - 1)
        def _():
            dwo_ref[...] = dw_acc[...].astype(BF16)

        dm = lax.dot_general(dxv, wo_ref[...], NT, preferred_element_type=F32)
        for o_ref, g_ref, dg_ref, sl in ((oa_ref, ga_ref, dga_ref, slice(0, WIDTH)),
                                         (ob_ref, gb_ref, dgb_ref, slice(WIDTH, 2 * WIDTH))):
            ov = o_ref[...].astype(F32)
            r = lax.rsqrt(jnp.mean(ov * ov, axis=-1, keepdims=True) + EPS)
            xh = ov * r
            d = dm[:, sl]
            dg_ref[...] += jnp.sum(d * xh, axis=0, keepdims=True)
            do = _rms_bwd(d, xh, r, g_ref[...])
            if o_ref is oa_ref:
                doa_ref[...] = do.astype(BF16)
            else:
                _scr_put(scr, do)
                for (_, dil), v_ref in zip(BRANCHES, (dob1_ref, dob4_ref, dob16_ref)):
                    _restride(scr, v_ref, dil, tm)

    row = lambda w_: pl.BlockSpec((tm, w_), lambda i: (i, 0))
    full = lambda a: pl.BlockSpec(a.shape, lambda i: (0, 0))
    return pl.pallas_call(
        body, name="wo_bwd", grid=(T // tm,),
        in_specs=[row(D_MODEL), full(wo), row(WIDTH), row(WIDTH), full(ga), full(gb), row(D_MODEL)],
        out_specs=[row(WIDTH)] + _view_specs(tm)
        + [pl.BlockSpec((1, WIDTH), lambda i: (0, 0)), pl.BlockSpec((1, WIDTH), lambda i: (0, 0)), full(wo)],
        out_shape=[jax.ShapeDtypeStruct((T, WIDTH), BF16)] + _view_shapes(T, BF16)
        + [jax.ShapeDtypeStruct((1, WIDTH), F32), jax.ShapeDtypeStruct((1, WIDTH), F32),
           jax.ShapeDtypeStruct((D_MODEL, D_MODEL), BF16)],
        scratch_shapes=[_scr(tm), pltpu.VMEM((D_MODEL, D_MODEL), F32)],
        compiler_params=_params(("arbitrary",)),
    )(dx2b, wo, oa, ob, ga, gb, mixed)


def _inproj_bwd(dqa, dka, dva, dqs, dks, dvs, cos, sin, w, x, dx2, g, h1, *, tm=512, wc=256):
    T = dqa.shape[0]
    n_tiles = T // tm

    def body(dqa_ref, dka_ref, dva_ref, q1, q2, q3, k1, k2, k3, v1, v2, v3, cos_ref, sin_ref, w_ref, x_ref, dx2_ref,
             g_ref, h1_ref, db_ref, gx_ref, dg_ref, dw_ref, dp_ref, dw_acc, acc, tmp):
        @pl.when(pl.program_id(0) == 0)
        def _():
            db_ref[...] = jnp.zeros_like(db_ref)
            dg_ref[...] = jnp.zeros_like(dg_ref)
            dw_acc[...] = jnp.zeros_like(dw_acc)

        cosv = cos_ref[...]
        sinv = sin_ref[...]
        lane = lax.broadcasted_iota(jnp.int32, (tm, PAIR), 1)
        first = (lane % HEAD_DIM) < (HEAD_DIM // 2)

        def put(off, val):
            dp_ref[:, off:off + PAIR] = val.astype(BF16)
            db_ref[:, off:off + PAIR] += jnp.sum(val, axis=0, keepdims=True)

        for src, off, width in ((dqa_ref, 0, 512), (dka_ref, 512, 256)):
            for j in range(0, width, PAIR):
                d = src[:, j:j + PAIR].astype(F32)
                put(off + j, d * cosv - _rope_rot(d, first) * sinv)
        for j in range(0, 256, PAIR):
            put(768 + j, dva_ref[:, j:j + PAIR].astype(F32))
        for (a, b, c), off in (((q1, q2, q3), 1024), ((k1, k2, k3), 1536), ((v1, v2, v3), 2048)):
            _unstride(b, acc, BRANCHES[1][1], tm)
            _unstride(c, tmp, BRANCHES[2][1], tm)
            for j in range(N_CHUNK):
                put(off + j * PAIR, a[:, j * PAIR:(j + 1) * PAIR].astype(F32) + acc[j] + tmp[j])

        dh = jnp.dot(dp_ref[...], w_ref[...], preferred_element_type=F32)
        xv = x_ref[...]
        r = lax.rsqrt(jnp.mean(xv * xv, axis=-1, keepdims=True) + EPS)
        xh = xv * r
        dg_ref[...] += jnp.sum(dh * xh, axis=0, keepdims=True)
        gx_ref[...] = dx2_ref[...] + _rms_bwd(dh, xh, r, g_ref[...])

        h1v = h1_ref[...]
        for c in range(0, D_INP, wc):
            dw_acc[c:c + wc, :] += lax.dot_general(dp_ref[:, c:c + wc], h1v, TN, preferred_element_type=F32)

        @pl.when(pl.program_id(0) == n_tiles - 1)
        def _():
            pltpu.sync_copy(dw_acc, dw_ref)

    row = lambda w_: pl.BlockSpec((tm, w_), lambda i: (i, 0))
    full = lambda a: pl.BlockSpec(a.shape, lambda i: (0, 0))
    return pl.pallas_call(
        body, name="inproj_bwd", grid=(n_tiles,),
        in_specs=[row(512), row(256), row(256)] + _view_specs(tm) * 3 + [row(PAIR), row(PAIR)]
        + [pl.BlockSpec(w.shape, lambda i: (0, 0), pipeline_mode=pl.Buffered(1)), row(D_MODEL), row(D_MODEL), full(g),
           row(D_MODEL)],
        out_specs=[pl.BlockSpec((1, D_INP), lambda i: (0, 0)), row(D_MODEL),
                   pl.BlockSpec((1, D_MODEL), lambda i: (0, 0)), pl.BlockSpec(memory_space=pl.ANY)],
        out_shape=[jax.ShapeDtypeStruct((1, D_INP), F32), jax.ShapeDtypeStruct((T, D_MODEL), F32),
                   jax.ShapeDtypeStruct((1, D_MODEL), F32), jax.ShapeDtypeStruct((D_INP, D_MODEL), F32)],
        scratch_shapes=[pltpu.VMEM((tm, D_INP), BF16), pltpu.VMEM((D_INP, D_MODEL), F32), _scr(tm), _scr(tm)],
        compiler_params=_params(("arbitrary",)),
    )(dqa, dka, dva, *dqs, *dks, *dvs, cos, sin, w, x, dx2, g, h1)


def _bias_sink_grads(dsums, bmaps, dsk):
    def body(s1, s2, s3, m1, m2, m3, dsk_ref, drel_ref, dsink_ref):
        row = lax.broadcasted_iota(jnp.int32, (N_HEADS, 128), 0)
        lane = lax.broadcasted_iota(jnp.int32, (N_HEADS, 128), 1)
        out = jnp.zeros((N_HEADS, 128), F32)
        for s_ref, m_ref in ((s1, m1), (s2, m2), (s3, m3)):
            bm = m_ref[...]
            for h in range(N_HEADS):
                a = s_ref[h]
                for b in range(REL_BUCKETS):
                    v = jnp.sum(jnp.sum(jnp.where(bm == b, a, 0.0), axis=-1, keepdims=True), axis=0, keepdims=True)
                    out = out + jnp.where((row == h) & (lane == b), v, 0.0)
        drel_ref[...] = out
        dsink_ref[...] = -jnp.sum(dsk_ref[...], axis=0, keepdims=True)

    vm = pl.BlockSpec(memory_space=pltpu.VMEM)
    return pl.pallas_call(
        body, name="bias_sink_grads",
        in_specs=[vm] * 7, out_specs=[vm, vm],
        out_shape=[jax.ShapeDtypeStruct((N_HEADS, 128), F32), jax.ShapeDtypeStruct((1, WIDTH), F32)],
        compiler_params=_params(),
    )(*dsums, *bmaps, dsk)


def _all_gather(blk, *, name):
    R, C = blk.shape

    def body(x_ref, out_ref, send_sems, recv_sems, local_sem):
        x, y, c = lax.axis_index("x"), lax.axis_index("y"), lax.axis_index("c")
        me, sibling = (x, y, c), (x, y, 1 - c)
        chips = [(1 - x, y), (x, 1 - y), (1 - x, 1 - y)]

        def slot(px, py, pc):
            return out_ref.at[4 * px + 2 * py + pc]

        def copy(k, block, to, src=None):
            return pltpu.make_async_remote_copy(
                src_ref=slot(*block) if src is None else src, dst_ref=slot(*block),
                send_sem=send_sems.at[k], recv_sem=recv_sems.at[k], device_id=to, device_id_type=MESH)

        mine = pltpu.make_async_copy(x_ref, slot(*me), local_sem)
        mine.start()
        first = [copy(0, me, sibling, src=x_ref)]
        first += [copy(1 + j, me, (*chip, c), src=x_ref) for j, chip in enumerate(chips)]
        for cp in first:
            cp.start()
        passed = [copy(4 + j, (*chip, c), sibling) for j, chip in enumerate(chips)]
        for j, chip in enumerate(chips):
            copy(1 + j, (*chip, c), me).wait_recv()
            passed[j].start()
        copy(0, sibling, me).wait_recv()
        for j, chip in enumerate(chips):
            copy(4 + j, (*chip, 1 - c), me).wait_recv()
        for cp in first + passed:
            cp.wait_send()
        mine.wait()

    return pl.pallas_call(
        body, name=name,
        in_specs=[pl.BlockSpec(memory_space=pl.ANY)], out_specs=pl.BlockSpec(memory_space=pl.ANY),
        out_shape=jax.ShapeDtypeStruct((N_DEV, R, C), blk.dtype),
        scratch_shapes=[pltpu.SemaphoreType.DMA((7,)), pltpu.SemaphoreType.DMA((7,)), pltpu.SemaphoreType.DMA],
        compiler_params=pltpu.CompilerParams(has_side_effects=True),
    )(blk)


def _peers(x, y, c):
    return [(x ^ (k >> 2), y ^ ((k >> 1) & 1), c ^ (k & 1)) for k in range(1, N_DEV)]


_HBM = pl.BlockSpec(memory_space=pltpu.HBM)
_SEM = pl.BlockSpec(memory_space=pltpu.SEMAPHORE)
_EFFECT = pltpu.SideEffectType.DATAFLOW_SIDE_EFFECTING


def _peer_list(x, y, c, near):
    if near:
        return [(x, y, 1 - c), (1 - x, y, c), (x, 1 - y, c), (1 - x, 1 - y, c)]
    return _peers(x, y, c)


def _exchange_start(srcs, *, gather, name, near=False):
    n = len(srcs)
    n_peers = 4 if near else N_DEV - 1
    lands = [lax.empty((N_DEV,) + s.shape[-2:], s.dtype) for s in srcs]

    def body(*refs):
        src_refs, land_refs = refs[:n], refs[n:2 * n]
        send_sems, recv_sems = refs[2 * n], refs[2 * n + 1]
        token = refs[-1]
        x, y, c = lax.axis_index("x"), lax.axis_index("y"), lax.axis_index("c")
        mine = 4 * x + 2 * y + c
        for a in range(n):
            for k, peer in enumerate(_peer_list(x, y, c, near)):
                dest = 4 * peer[0] + 2 * peer[1] + peer[2]
                j = a * n_peers + k
                pltpu.make_async_remote_copy(
                    src_ref=src_refs[a] if gather else src_refs[a].at[dest], dst_ref=land_refs[a].at[mine],
                    send_sem=send_sems.at[j], recv_sem=recv_sems.at[j], device_id=peer, device_id_type=MESH).start()
        token[...] = jnp.zeros_like(token)

    sems = pltpu.SemaphoreType.DMA((n * n_peers,))
    out = pl.pallas_call(
        body, name=name,
        out_shape=(sems, sems) + tuple(pltpu.HBM(a.shape, a.dtype) for a in list(srcs) + lands)
        + (jax.ShapeDtypeStruct((8, 128), F32),),
        in_specs=(_HBM,) * (2 * n), out_specs=(_SEM, _SEM) + (_HBM,) * (2 * n) + (pl.BlockSpec(memory_space=pltpu.VMEM),),
        input_output_aliases={i: 2 + i for i in range(2 * n)},
        compiler_params=pltpu.CompilerParams(has_side_effects=_EFFECT),
    )(*[pltpu.with_memory_space_constraint(a, pltpu.HBM) for a in list(srcs) + lands])
    return out[:-1], out[-1]


def _exchange_wait(state, after, *, gather, name, near=False):
    send_sems, recv_sems = state[0], state[1]
    n = (len(state) - 2) // 2
    n_peers = 4 if near else N_DEV - 1
    arrays = state[2:]

    def body(*refs):
        src_refs, land_refs = refs[:n], refs[n:2 * n]
        send_sems, recv_sems = refs[2 * n], refs[2 * n + 1]
        x, y, c = lax.axis_index("x"), lax.axis_index("y"), lax.axis_index("c")
        for a in range(n):
            for k, peer in enumerate(_peer_list(x, y, c, near)):
                other = 4 * peer[0] + 2 * peer[1] + peer[2]
                j = a * n_peers + k
                copy = pltpu.make_async_remote_copy(
                    src_ref=src_refs[a] if gather else src_refs[a].at[other], dst_ref=land_refs[a].at[other],
                    send_sem=send_sems.at[j], recv_sem=recv_sems.at[j], device_id=peer, device_id_type=MESH)
                copy.wait_send()
                copy.wait_recv()

    out = pl.pallas_call(
        body, name=name,
        out_shape=tuple(pltpu.HBM(a.shape, a.dtype) for a in arrays),
        in_specs=(_HBM,) * (2 * n) + (_SEM, _SEM, pl.BlockSpec(memory_space=pl.ANY)), out_specs=(_HBM,) * (2 * n),
        input_output_aliases={i: i for i in range(2 * n)},
        compiler_params=pltpu.CompilerParams(has_side_effects=_EFFECT),
    )(*arrays, send_sems, recv_sems, after)
    mine = 4 * lax.axis_index("x") + 2 * lax.axis_index("y") + lax.axis_index("c")
    own = out[:n] if gather else [lax.dynamic_index_in_dim(s, mine, 0, keepdims=False) for s in out[:n]]
    return [lax.dynamic_update_slice(g, o[None], (mine, 0, 0)) for g, o in zip(out[n:], own)]


def _forward_start(lands, *, name):
    n = len(lands)

    def body(*refs):
        land_refs, send_sems, recv_sems, token = refs[:n], refs[n], refs[n + 1], refs[-1]
        x, y, c = lax.axis_index("x"), lax.axis_index("y"), lax.axis_index("c")
        for a in range(n):
            for j, (px, py) in enumerate(((1 - x, y), (x, 1 - y), (1 - x, 1 - y))):
                blk = 4 * px + 2 * py + c
                pltpu.make_async_remote_copy(
                    src_ref=land_refs[a].at[blk], dst_ref=land_refs[a].at[blk], send_sem=send_sems.at[3 * a + j],
                    recv_sem=recv_sems.at[3 * a + j], device_id=(x, y, 1 - c), device_id_type=MESH).start()
        token[...] = jnp.zeros_like(token)

    sems = pltpu.SemaphoreType.DMA((3 * n,))
    out = pl.pallas_call(
        body, name=name,
        out_shape=(sems, sems) + tuple(pltpu.HBM(a.shape, a.dtype) for a in lands) + (jax.ShapeDtypeStruct((8, 128), F32),),
        in_specs=(_HBM,) * n, out_specs=(_SEM, _SEM) + (_HBM,) * n + (pl.BlockSpec(memory_space=pltpu.VMEM),),
        input_output_aliases={i: 2 + i for i in range(n)},
        compiler_params=pltpu.CompilerParams(has_side_effects=_EFFECT),
    )(*[pltpu.with_memory_space_constraint(a, pltpu.HBM) for a in lands])
    return out[:-1], out[-1]


def _forward_wait(state, after, *, name):
    send_sems, recv_sems = state[0], state[1]
    lands = state[2:]
    n = len(lands)

    def body(*refs):
        land_refs, send_sems, recv_sems = refs[:n], refs[n], refs[n + 1]
        x, y, c = lax.axis_index("x"), lax.axis_index("y"), lax.axis_index("c")
        for a in range(n):
            for j, (px, py) in enumerate(((1 - x, y), (x, 1 - y), (1 - x, 1 - y))):
                copy = pltpu.make_async_remote_copy(
                    src_ref=land_refs[a].at[4 * px + 2 * py + c], dst_ref=land_refs[a].at[4 * px + 2 * py + 1 - c],
                    send_sem=send_sems.at[3 * a + j], recv_sem=recv_sems.at[3 * a + j], device_id=(x, y, 1 - c),
                    device_id_type=MESH)
                copy.wait_send()
                copy.wait_recv()

    return pl.pallas_call(
        body, name=name,
        out_shape=tuple(pltpu.HBM(a.shape, a.dtype) for a in lands),
        in_specs=(_HBM,) * n + (_SEM, _SEM, pl.BlockSpec(memory_space=pl.ANY)), out_specs=(_HBM,) * n,
        input_output_aliases={i: i for i in range(n)},
        compiler_params=pltpu.CompilerParams(has_side_effects=_EFFECT),
    )(*lands, send_sems, recv_sems, after)


def _adam_math(w, g, m, v):
    m = ADAM_B1 * m + (1.0 - ADAM_B1) * g
    v = ADAM_B2 * v + (1.0 - ADAM_B2) * (g * g)
    m_hat = m / (1.0 - ADAM_B1 ** ADAM_STEP)
    v_hat = v / (1.0 - ADAM_B2 ** ADAM_STEP)
    delta = -ADAM_LR * (m_hat / (jnp.sqrt(v_hat) + ADAM_EPS) + ADAM_WD * w)
    return delta, m, v


def _adamw(parts, w, m, v, *, name):
    R, C = w.shape
    n_parts = parts.shape[0]
    tr = R // 2
    assert tr % 16 == 0

    def body(p_ref, w_ref, m_ref, v_ref, g_ref, d_ref, nm_ref, nv_ref):
        g = p_ref[0].astype(F32)
        for s in range(1, n_parts):
            g = g + p_ref[s].astype(F32)
        d, nm, nv = _adam_math(w_ref[...], g, m_ref[...], v_ref[...])
        g_ref[...] = g
        d_ref[...] = d
        nm_ref[...] = nm
        nv_ref[...] = nv

    blk = pl.BlockSpec((tr, C), lambda i: (i, 0))
    return pl.pallas_call(
        body, name=name, grid=(R // tr,),
        in_specs=[pl.BlockSpec((n_parts, tr, C), lambda i: (0, i, 0)), blk, blk, blk],
        out_specs=[blk] * 4, out_shape=[jax.ShapeDtypeStruct((R, C), F32)] * 4,
        compiler_params=_params(("arbitrary",)),
    )(parts, w, m, v)


def _adamw_small(parts, w, m, v):
    def body(p_ref, w_ref, m_ref, v_ref, g_ref, d_ref, nm_ref, nv_ref):
        g = p_ref[0]
        for s in range(1, N_DEV):
            g = g + p_ref[s]
        d, nm, nv = _adam_math(w_ref[...], g, m_ref[...], v_ref[...])
        g_ref[...] = g
        d_ref[...] = d
        nm_ref[...] = nm
        nv_ref[...] = nv

    vm = pl.BlockSpec(memory_space=pltpu.VMEM)
    return pl.pallas_call(
        body, name="adamw_small", in_specs=[vm] * 4, out_specs=[vm] * 4,
        out_shape=[jax.ShapeDtypeStruct((SMALL_ROWS, 128), F32)] * 4, compiler_params=_params(),
    )(parts, w, m, v)


def _t5_bucket(dist):
    max_exact = REL_BUCKETS // 2
    df = jnp.maximum(dist, 1).astype(F32)
    large = max_exact + (jnp.log(df / max_exact) / math.log(REL_MAX_DISTANCE / max_exact)
                         * (REL_BUCKETS - max_exact)).astype(jnp.int32)
    large = jnp.minimum(large, REL_BUCKETS - 1)
    return jnp.where(dist < max_exact, dist, large)


def _band_tables(rel_table, dil, n_back):
    qi = jnp.arange(BLK)[:, None]
    kj = jnp.arange(2 * BLK)[None, :]
    delta = BLK + qi - kj
    in_band = (delta >= 0) & (delta <= n_back)
    if rel_table is None:
        vals = jnp.zeros((N_HEADS, BLK, 2 * BLK), F32)
        bmap = None
    else:
        bucket = _t5_bucket(jnp.clip(delta, 0, n_back) * dil)
        vals = jnp.zeros((N_HEADS, BLK, 2 * BLK), F32)
        for b in range(REL_BUCKETS):
            vals = jnp.where((bucket == b)[None], rel_table[b][:, None, None], vals)
        bmap = jnp.where(in_band, bucket, -1).astype(jnp.int32)
    later = jnp.where(in_band[None], vals, NEG)
    first = jnp.where((in_band & (kj >= BLK))[None], vals, NEG)
    return jnp.stack([later, first]), bmap


def _rope_tables(T):
    half = HEAD_DIM // 2
    inv_freq = ROPE_THETA ** (-jnp.arange(half, dtype=F32) / half)
    ang = jnp.arange(T, dtype=F32)[:, None] * inv_freq[None, :]
    cos, sin = jnp.cos(ang), jnp.sin(ang)
    return jnp.tile(cos, (1, 4)), jnp.tile(jnp.concatenate([-sin, sin], axis=1), (1, 2))


def _widen_in(a, axis):
    sl = lambda lo, hi: lax.slice_in_dim(a, lo, hi, axis=axis)
    dup = lambda lo: [sl(lo, lo + 64), sl(lo, lo + 64), sl(lo + 64, lo + 128), sl(lo + 64, lo + 128)]
    return jnp.concatenate([sl(0, 512)] + dup(512) + dup(640) + [sl(768, D_IN)], axis=axis)


def _fold_in(a, axis):
    sl = lambda lo, hi: lax.slice_in_dim(a, lo, hi, axis=axis)
    fold = lambda lo: [sl(lo, lo + 64) + sl(lo + 64, lo + 128), sl(lo + 128, lo + 192) + sl(lo + 192, lo + 256)]
    return jnp.concatenate([sl(0, 512)] + fold(512) + fold(768) + [sl(1024, D_INP)], axis=axis)


def _local_step(x, tgt, g_attn, b_in, sinks, rel_table, g_out_a, g_out_b, g_ffn, g_final,
                win_fn, wo_fn, ffn_fn, early_fn):
    T = x.shape[0]
    cos, sin = _rope_tables(T)
    g_final2 = g_final.reshape(1, D_MODEL)
    sink8 = sinks.reshape(N_HEADS)

    bias_a, _ = _band_tables(None, 1, BLK - 1)
    tabs = [_band_tables(rel_table, dil, window // dil) for window, dil in BRANCHES]
    wint, token = win_fn(tabs[2][0])
    winp = _widen_in(wint, 0)
    binp = _widen_in(b_in, 1) + token[0, 0]

    h1, qa, ka, va, *qkv_b = _norm_proj(x, g_attn, winp, binp, cos, sin)
    qbs, kbs, vbs = qkv_b[0:3], qkv_b[3:6], qkv_b[6:9]
    oa, lse_a = _attn_fwd(qa, ka, va, bias_a, sink8, dil=1, kv_pairs=2, use_sink=True, name="attn_a_fwd")
    outs = [_attn_fwd(qbs[n], kbs[n], vbs[n], tabs[n][0], sink8, dil=dil, kv_pairs=4, use_sink=False,
                      name=f"attn_b{n}_fwd") for n, (_, dil) in enumerate(BRANCHES)]
    wo = wo_fn(outs[2][1])
    x2, mixed, h2, *ob_lse = _merge_wo(x, oa, outs[0][0], outs[1][0], outs[2][0], outs[0][1], outs[1][1], outs[2][1],
                                       g_out_a, g_out_b, wo, g_ffn)
    obs, lses = ob_lse[0:3], ob_lse[3:6]
    wgt, wut, wd = ffn_fn(h2)
    gate, up, act = _ffn_up(h2, wgt, wut)
    dx3, dx3b, loss, dg_final = _ffn_down_loss(act, wd, x2, tgt, g_final2)

    dgate, dup, dx2, dx2b, dg_ffn = _ffn_bwd(dx3, gate, up, wd, wgt, wut, x2, g_ffn)
    dwd = _matmul_tn(act, dx3b, tk=1408, tn=1024, name="dw_down")
    dwgt = _matmul_tn(dgate, h2, tk=1408, tn=1024, name="dw_gate")
    dwut = _matmul_tn(dup, h2, tk=1408, tn=1024, name="dw_up")
    doa, *dobs, dg_out_a, dg_out_b, dwo = _wo_bwd(dx2b, wo, oa, obs[0], g_out_a, g_out_b, mixed)
    early, token2 = early_fn(dict(w_o=dwo, w_gate=dwgt, w_up=dwut, w_down=dwd))
    sink8b = sink8 + token2[0, 0]

    dqa, dka, dva, _, dsk = _attn_bwd(qa, ka, va, oa, doa, lse_a, bias_a, sink8b, dil=1, kv_pairs=2, use_sink=True,
                                      name="attn_a_bwd", max_sub=4)
    res = [_attn_bwd(qbs[n], kbs[n], vbs[n], obs[n], dobs[n], lses[n], tabs[n][0], sink8b, dil=dil, kv_pairs=4,
                     use_sink=False, name=f"attn_b{n}_bwd") for n, (_, dil) in enumerate(BRANCHES)]
    dbp, grad_x, dg_attn, dwinp = _inproj_bwd(dqa, dka, dva, [r[0] for r in res], [r[1] for r in res],
                                              [r[2] for r in res], cos, sin, winp, x, dx2, g_attn, h1)
    dwin = _fold_in(dwinp, 0)
    drel, dsink = _bias_sink_grads([r[3] for r in res], [t[1] for t in tabs], dsk)

    small = dict(
        g_attn=dg_attn, b_in=_fold_in(dbp, 1), sinks=dsink[:, ::HEAD_DIM], rel_table=drel[:, :REL_BUCKETS].T,
        g_out_a=dg_out_a, g_out_b=dg_out_b, g_ffn=dg_ffn, g_final=dg_final.reshape(D_MODEL))
    return loss[0, 0], grad_x, dwin, early, small


SMALL_NAMES = ("g_attn", "b_in", "sinks", "rel_table", "g_out_a", "g_out_b", "g_ffn", "g_final", "loss")


def _pack_small(vals):
    flat = jnp.concatenate([vals[n].reshape(-1).astype(F32) for n in SMALL_NAMES])
    return jnp.pad(flat, (0, SMALL_ROWS * 128 - flat.shape[0])).reshape(SMALL_ROWS, 128)


def _unpack_small(packed, like):
    flat = packed.reshape(-1)
    out, off = {}, 0
    for n in SMALL_NAMES:
        size = like[n].size
        out[n] = flat[off:off + size].reshape(like[n].shape)
        off += size
    return out


def kernel(x, g_attn, w_in, b_in, sinks, rel_table, g_out_a, g_out_b, w_o, g_ffn, w_gate, w_up, w_down, g_final, loss_target, m_g_attn, m_w_in, m_b_in, m_sinks, m_rel_table, m_g_out_a, m_g_out_b, m_w_o, m_g_ffn, m_w_gate, m_w_up, m_w_down, m_g_final, v_g_attn, v_w_in, v_b_in, v_sinks, v_rel_table, v_g_out_a, v_g_out_b, v_w_o, v_g_ffn, v_w_gate, v_w_up, v_w_down, v_g_final):
    rest_names = ("w_o", "w_gate", "w_up", "w_down")

    rest = [w_o[0].astype(BF16), w_gate[0].astype(BF16).T, w_up[0].astype(BF16).T, w_down[0].astype(BF16)]
    in_state, _ = _exchange_start([w_in[0].astype(BF16).T], gather=True, near=True, name="gather_w_in_start")
    later = {}

    def whole(got):
        return [g.reshape(N_DEV * g.shape[1], D_MODEL) for g in got]

    def win_fn(after):
        near = _exchange_wait(in_state, after, gather=True, near=True, name="gather_w_in_near")
        fwd_state, tok = _forward_start(near, name="gather_w_in_forward")
        wint = whole(_forward_wait(fwd_state, tok, name="gather_w_in_wait"))[0]
        wint, src = lax.optimization_barrier((wint, rest))
        later["wo"], token_o = _exchange_start(src[:1], gather=True, name="gather_w_o_start")
        token_o, ffn_src = lax.optimization_barrier((token_o, src[1:]))
        later["ffn"], token = _exchange_start(ffn_src, gather=True, name="gather_ffn_start")
        return wint, token + token_o

    def wo_fn(after):
        return whole(_exchange_wait(later["wo"], after, gather=True, name="gather_w_o_wait"))[0]

    def ffn_fn(after):
        return whole(_exchange_wait(later["ffn"], after, gather=True, name="gather_ffn_wait"))

    def early_fn(dws):
        return _exchange_start([dws[n].reshape(N_DEV, -1, D_MODEL) for n in rest_names], gather=False,
                               name="scatter_rest_start")

    loss_part, grad_x, dwint, early_state, small = _local_step(
        x[0], loss_target[0], g_attn, b_in, sinks, rel_table, g_out_a, g_out_b, g_ffn, g_final,
        win_fn, wo_fn, ffn_fn, early_fn)
    parts_in = dwint.astype(BF16).reshape(N_DEV, D_IN // N_DEV, D_MODEL)
    in_state, token3 = _exchange_start([parts_in], gather=False, name="scatter_w_in_start")
    got = _exchange_wait(early_state, token3, gather=False, name="scatter_rest_wait")

    def update(n, parts, w, m, v, transposed):
        if transposed:
            return [a.T[None] for a in _adamw(parts, w[0].T, m[0].T, v[0].T, name="adamw_" + n)]
        return [a[None] for a in _adamw(parts, w[0], m[0], v[0], name="adamw_" + n)]

    big = dict(w_o=update("w_o", got[0], w_o, m_w_o, v_w_o, False),
               w_gate=update("w_gate", got[1], w_gate, m_w_gate, v_w_gate, True),
               w_up=update("w_up", got[2], w_up, m_w_up, v_w_up, True),
               w_down=update("w_down", got[3], w_down, m_w_down, v_w_down, False))

    unused = jnp.zeros((1,), F32)
    ws = dict(g_attn=g_attn, b_in=b_in, sinks=sinks, rel_table=rel_table, g_out_a=g_out_a, g_out_b=g_out_b,
              g_ffn=g_ffn, g_final=g_final, loss=unused)
    ms = dict(g_attn=m_g_attn, b_in=m_b_in, sinks=m_sinks, rel_table=m_rel_table, g_out_a=m_g_out_a,
              g_out_b=m_g_out_b, g_ffn=m_g_ffn, g_final=m_g_final, loss=unused)
    vs = dict(g_attn=v_g_attn, b_in=v_b_in, sinks=v_sinks, rel_table=v_rel_table, g_out_a=v_g_out_a,
              g_out_b=v_g_out_b, g_ffn=v_g_ffn, g_final=v_g_final, loss=unused)
    sparts = _all_gather(_pack_small(dict(small, loss=loss_part)), name="gather_small")
    sm_packed = _adamw_small(sparts, _pack_small(ws), _pack_small(ms), _pack_small(vs))
    sm = [_unpack_small(a, ws) for a in sm_packed]
    loss = sm[0]["loss"][0]

    done = sm_packed[1][:1, :1] + sum(big[n][1][0, :1, :1] for n in rest_names)
    got_in = _exchange_wait(in_state, done, gather=False, name="scatter_w_in_wait")[0]
    big["w_in"] = update("w_in", got_in, w_in, m_w_in, v_w_in, True)

    order = ("g_attn", "w_in", "b_in", "sinks", "rel_table", "g_out_a", "g_out_b", "w_o", "g_ffn", "w_gate", "w_up",
             "w_down", "g_final")
    outs = [loss, grad_x[None]]
    for k in range(4):
        outs += [big[n][k] if n in big else sm[k][n] for n in order]
    return tuple(outs)
```

```python
import math

import jax
import jax.numpy as jnp
from jax import lax
from jax.experimental import pallas as pl
from jax.experimental.pallas import tpu as pltpu

F32 = jnp.float32
BF16 = jnp.bfloat16

N_DEV = 8
D_MODEL = 1024
HEAD_DIM = 64
N_HEADS = 8
PAIR = 2 * HEAD_DIM
WIDTH = N_HEADS * HEAD_DIM
D_IN = 2304
D_INP = 2560
D_FF = 2816
BLK = 128
ROPE_THETA = 150000.0
REL_BUCKETS = 32
REL_MAX_DISTANCE = 2048
EPS = 1e-5
NEG = -1e30
BRANCHES = ((128, 1), (512, 4), (2048, 16))
Q_SCALE = HEAD_DIM ** -0.5

ADAM_LR = 0.001
ADAM_B1 = 0.9
ADAM_B2 = 0.999
ADAM_EPS = 1e-08
ADAM_WD = 0.01
ADAM_STEP = 10

VMEM_LIMIT = 56 * 1024 * 1024
MESH = pl.DeviceIdType.MESH

NT = (((1,), (1,)), ((), ()))
TN = (((0,), (0,)), ((), ()))

SMALL_ROWS = 56


def _params(sem=None):
    return pltpu.CompilerParams(dimension_semantics=sem, vmem_limit_bytes=VMEM_LIMIT)


def _sigmoid(x):
    return 1.0 / (1.0 + jnp.exp2(x * (-1.0 / math.log(2.0))))


def _rms_bwd(dh, xh, r, g):
    u = dh * g
    return r * (u - xh * jnp.mean(u * xh, axis=-1, keepdims=True))


def _rope_rot(t, first):
    return jnp.where(first, pltpu.roll(t, 96, 1), pltpu.roll(t, 32, 1))


N_CHUNK = WIDTH // PAIR


def _scr(tm):
    return pltpu.VMEM((N_CHUNK, tm, PAIR), F32)


def _scr_get(scr):
    return jnp.concatenate([scr[j] for j in range(N_CHUNK)], axis=1)


def _scr_put(scr, val):
    for j in range(N_CHUNK):
        scr[j] = val[:, j * PAIR:(j + 1) * PAIR]


def _unstride(view_ref, scr, dil, tm):
    n = tm // dil
    chunks = scr.shape[0]
    for r in range(dil):
        for j in range(chunks):
            col = (r * chunks + j) * PAIR
            scr.at[j][pl.ds(r, n, stride=dil), :] = view_ref[:, col:col + PAIR].astype(F32)


def _restride(scr, out_ref, dil, tm):
    n = tm // dil
    chunks = scr.shape[0]
    for r in range(dil):
        for j in range(chunks):
            col = (r * chunks + j) * PAIR
            rows = scr[j] if dil == 1 else scr.at[j][pl.ds(r, n, stride=dil), :]
            out_ref[:, col:col + PAIR] = rows.astype(out_ref.dtype)


def _view_specs(tm, width=WIDTH):
    return [pl.BlockSpec((tm // dil, dil * width), lambda i: (i, 0)) for _, dil in BRANCHES]


def _view_shapes(T, dtype, width=WIDTH):
    return [jax.ShapeDtypeStruct((T // dil, dil * width), dtype) for _, dil in BRANCHES]


def _norm_proj(x, g, w, b, cos, sin, *, tm=512):
    T = x.shape[0]

    def body(x_ref, g_ref, w_ref, b_ref, cos_ref, sin_ref, h_ref, qa_ref, ka_ref, va_ref, *rest):
        outs_b, ys = rest[:9], rest[9]
        xv = x_ref[...]
        r = lax.rsqrt(jnp.mean(xv * xv, axis=-1, keepdims=True) + EPS)
        h = (xv * r * g_ref[...]).astype(BF16)
        h_ref[...] = h
        cosv = cos_ref[...]
        sinv = sin_ref[...]
        lane = lax.broadcasted_iota(jnp.int32, (tm, PAIR), 1)
        first = (lane % HEAD_DIM) < (HEAD_DIM // 2)

        def proj(off):
            return (lax.dot_general(h, w_ref[off:off + 256, :], NT, preferred_element_type=F32)
                    + b_ref[:, off:off + 256])

        for (off, width, rot, scale), o_ref in zip(((0, 512, True, Q_SCALE), (512, 256, True, 1.0), (768, 256, False, 1.0)),
                                                   (qa_ref, ka_ref, va_ref)):
            for c in range(0, width, 256):
                y = proj(off + c)
                for j in range(0, 256, PAIR):
                    t = y[:, j:j + PAIR]
                    if rot:
                        t = t * cosv + _rope_rot(t, first) * sinv
                    if scale != 1.0:
                        t = t * scale
                    o_ref[:, c + j:c + j + PAIR] = t.astype(BF16)
        for n, (off, scale) in enumerate(((1024, Q_SCALE), (1536, 1.0), (2048, 1.0))):
            for c in range(0, WIDTH, 256):
                y = proj(off + c)
                y = y * scale if scale != 1.0 else y
                for j in range(0, 256, PAIR):
                    ys[(c + j) // PAIR] = y[:, j:j + PAIR]
            for (_, dil), o_ref in zip(BRANCHES, outs_b[3 * n:3 * n + 3]):
                _restride(ys, o_ref, dil, tm)

    row = lambda w_: pl.BlockSpec((tm, w_), lambda i: (i, 0))
    full = lambda a: pl.BlockSpec(a.shape, lambda i: (0, 0))
    return pl.pallas_call(
        body, name="norm_proj", grid=(T // tm,),
        in_specs=[row(D_MODEL), full(g), full(w), full(b), row(PAIR), row(PAIR)],
        out_specs=[row(D_MODEL), row(512), row(256), row(256)] + _view_specs(tm) * 3,
        out_shape=[jax.ShapeDtypeStruct((T, n), BF16) for n in (D_MODEL, 512, 256, 256)] + _view_shapes(T, BF16) * 3,
        scratch_shapes=[_scr(tm)],
        compiler_params=_params(("arbitrary",)),
    )(x, g, w, b, cos, sin)


MAX_SUB = 8
AHEAD = 2


def _attn_specs(kvw, sub):
    q_spec = pl.BlockSpec((sub * BLK, WIDTH), lambda r, i: (i, r))
    kc_spec = pl.BlockSpec((sub * BLK, kvw), lambda r, i: (i, r))
    kp_spec = pl.BlockSpec((BLK, kvw), lambda r, i: (jnp.maximum(sub * i - 1, 0), r))
    b_spec = pl.BlockSpec((2, N_HEADS, BLK, 2 * BLK), lambda r, i: (0, 0, 0, 0))
    return q_spec, kp_spec, kc_spec, b_spec


def _window(prev_ref, cur_ref, j, ksl):
    before = prev_ref[:, ksl] if j == 0 else cur_ref[(j - 1) * BLK:j * BLK, ksl]
    return jnp.concatenate([before, cur_ref[j * BLK:(j + 1) * BLK, ksl]], axis=0)


def _attn_fwd(q, k, v, bias, sinks, *, dil, kv_pairs, use_sink, name):
    L = q.shape[0]
    sub = min(MAX_SUB, L // BLK)
    ns = L // (sub * BLK)
    kvw = kv_pairs * PAIR
    rep = 4 // kv_pairs

    def body(sink_ref, q_ref, kp_ref, kc_ref, vp_ref, vc_ref, b_ref, o_ref, lse_ref):
        lane = lax.broadcasted_iota(jnp.int32, (1, PAIR), 1)
        lo = lane < HEAD_DIM
        first = jnp.where(pl.program_id(1) == 0, 1, 0)
        def scores(j, hp):
            rows = slice(j * BLK, (j + 1) * BLK)
            sl = slice(hp * PAIR, (hp + 1) * PAIR)
            ksl = slice((hp // rep) * PAIR, (hp // rep + 1) * PAIR)
            qp = q_ref[rows, sl]
            kk = _window(kp_ref, kc_ref, j, ksl)
            vv = _window(vp_ref, vc_ref, j, ksl)
            heads = []
            for e in range(2):
                h = 2 * hp + e
                msk = lo if e == 0 else jnp.logical_not(lo)
                qm = jnp.where(msk, qp, jnp.zeros_like(qp))
                s = lax.dot_general(qm, kk, NT, preferred_element_type=F32) + (b_ref[first, h] if j == 0 else b_ref[0, h])
                heads.append((h, msk, s))
            return rows, sl, vv, heads

        def outputs(rows, sl, vv, heads):
            o_pair = None
            lse_pair = None
            for h, msk, s in heads:
                m = jnp.max(s, axis=-1, keepdims=True)
                if use_sink:
                    sk = sink_ref[h]
                    m = jnp.maximum(m, sk)
                p = jnp.exp(s - m)
                l = jnp.sum(p, axis=-1, keepdims=True)
                if use_sink:
                    l = l + jnp.exp(sk - m)
                vm = jnp.where(msk, vv, jnp.zeros_like(vv))
                oe = jnp.dot(p.astype(BF16), vm, preferred_element_type=F32) * (1.0 / l)
                ls = m + jnp.log(l)
                if o_pair is None:
                    o_pair = oe
                    lse_pair = jnp.broadcast_to(ls, (BLK, PAIR))
                else:
                    o_pair = o_pair + oe
                    lse_pair = jnp.where(lo, lse_pair, ls)
            o_ref[rows, sl] = o_pair.astype(BF16)
            lse_ref[rows, sl] = lse_pair

        items = [(j, hp) for j in range(sub) for hp in range(4)]
        queue = [scores(*it) for it in items[:AHEAD]]
        for n in range(len(items)):
            if n + AHEAD < len(items):
                queue.append(scores(*items[n + AHEAD]))
            outputs(*queue.pop(0))

    q_spec, kp_spec, kc_spec, b_spec = _attn_specs(kvw, sub)
    return pl.pallas_call(
        body, name=name, grid=(dil, ns),
        in_specs=[pl.BlockSpec(memory_space=pltpu.SMEM), q_spec, kp_spec, kc_spec, kp_spec, kc_spec, b_spec],
        out_specs=[q_spec, q_spec],
        out_shape=[jax.ShapeDtypeStruct((L, dil * WIDTH), BF16), jax.ShapeDtypeStruct((L, dil * WIDTH), F32)],
        compiler_params=_params(("arbitrary", "arbitrary")),
    )(sinks, q, k, k, v, v, bias)


def _attn_bwd(q, k, v, o, do, lse, bias, sinks, *, dil, kv_pairs, use_sink, name, max_sub=MAX_SUB):
    L = q.shape[0]
    sub = min(max_sub, L // BLK)
    ns = L // (sub * BLK)
    n_steps = dil * ns
    kvw = kv_pairs * PAIR
    rep = 4 // kv_pairs
    last = slice((sub - 1) * BLK, sub * BLK)

    def body(sink_ref, q_ref, kp_ref, kc_ref, vp_ref, vc_ref, o_ref, do_ref, lse_ref, b_ref,
             dq_ref, dk_ref, dv_ref, dsum_ref, dsk_ref, pk_ref, pv_ref):
        t = pl.program_id(0)
        i = t % ns

        @pl.when(t == 0)
        def _():
            dsum_ref[...] = jnp.zeros_like(dsum_ref)
            dsk_ref[...] = jnp.zeros_like(dsk_ref)
            pk_ref[...] = jnp.zeros_like(pk_ref)
            pv_ref[...] = jnp.zeros_like(pv_ref)

        @pl.when(t < n_steps)
        def _():
            lo = lax.broadcasted_iota(jnp.int32, (1, PAIR), 1) < HEAD_DIM
            first = jnp.where(i == 0, 1, 0)
            dks = [[None] * kv_pairs for _ in range(sub)]
            dvs = [[None] * kv_pairs for _ in range(sub)]
            def scores(j, hp):
                rows = slice(j * BLK, (j + 1) * BLK)
                kvp = hp // rep
                sl = slice(hp * PAIR, (hp + 1) * PAIR)
                ksl = slice(kvp * PAIR, (kvp + 1) * PAIR)
                qp = q_ref[rows, sl]
                dop = do_ref[rows, sl]
                prod = dop.astype(F32) * o_ref[rows, sl].astype(F32)
                kk = _window(kp_ref, kc_ref, j, ksl)
                vv = _window(vp_ref, vc_ref, j, ksl)
                heads = []
                for e in range(2):
                    h = 2 * hp + e
                    msk = lo if e == 0 else jnp.logical_not(lo)
                    qm = jnp.where(msk, qp, jnp.zeros_like(qp))
                    dom = jnp.where(msk, dop, jnp.zeros_like(dop))
                    km = jnp.where(msk, kk, jnp.zeros_like(kk))
                    s = (lax.dot_general(qm, kk, NT, preferred_element_type=F32)
                         + (b_ref[first, h] if j == 0 else b_ref[0, h]))
                    dp = lax.dot_general(dom, vv, NT, preferred_element_type=F32)
                    heads.append((h, msk, qm, dom, km, s, dp))
                return j, rows, kvp, sl, prod, heads

            def grads(j, rows, kvp, sl, prod, heads):
                dq_pair = None
                c_pair = None
                qms, doms, dsbs, pbs = [], [], [], []
                for h, msk, qm, dom, km, s, dp in heads:
                    ls = lse_ref[rows, h * HEAD_DIM:h * HEAD_DIM + 1]
                    p = jnp.exp(s - ls)
                    delta = jnp.sum(jnp.where(msk, prod, 0.0), axis=-1, keepdims=True)
                    ds = p * (dp - delta)
                    if use_sink:
                        ce = jnp.exp(sink_ref[h] - ls) * delta
                        c_pair = jnp.broadcast_to(ce, (BLK, PAIR)) if c_pair is None else jnp.where(msk, ce, c_pair)
                    else:
                        dsum_ref[h] += ds
                    dsb = ds.astype(BF16)
                    dqe = jnp.dot(dsb, km, preferred_element_type=F32)
                    dq_pair = dqe if dq_pair is None else dq_pair + dqe
                    qms.append(qm)
                    doms.append(dom)
                    dsbs.append(dsb)
                    pbs.append(p.astype(BF16))
                dke = lax.dot_general(jnp.concatenate(dsbs, axis=0), jnp.concatenate(qms, axis=0), TN,
                                      preferred_element_type=F32)
                dve = lax.dot_general(jnp.concatenate(pbs, axis=0), jnp.concatenate(doms, axis=0), TN,
                                      preferred_element_type=F32)
                dks[j][kvp] = dke if dks[j][kvp] is None else dks[j][kvp] + dke
                dvs[j][kvp] = dve if dvs[j][kvp] is None else dvs[j][kvp] + dve
                dq_ref[rows, sl] = (dq_pair * Q_SCALE).astype(BF16)
                if use_sink:
                    dsk_ref[:, sl] += c_pair

            items = [(j, hp) for j in range(sub) for hp in range(4)]
            ahead = AHEAD + 1 if use_sink else AHEAD
            queue = [scores(*it) for it in items[:ahead]]
            for n in range(len(items)):
                if n + ahead < len(items):
                    queue.append(scores(*items[n + ahead]))
                grads(*queue.pop(0))
            for kvp in range(kv_pairs):
                ksl = slice(kvp * PAIR, (kvp + 1) * PAIR)
                for pend_ref, out_ref, parts in ((pk_ref, dk_ref, [d[kvp] for d in dks]),
                                                 (pv_ref, dv_ref, [d[kvp] for d in dvs])):
                    if sub > 1:
                        out_ref[:(sub - 1) * BLK, ksl] = pend_ref[:(sub - 1) * BLK, ksl].astype(BF16)
                    out_ref[last, ksl] = (pend_ref[last, ksl] + parts[0][:BLK]).astype(BF16)
                    for j in range(sub):
                        own = parts[j][BLK:]
                        pend_ref[j * BLK:(j + 1) * BLK, ksl] = own + parts[j + 1][:BLK] if j + 1 < sub else own

        @pl.when(t == n_steps)
        def _():
            dk_ref[...] = pk_ref[...].astype(BF16)
            dv_ref[...] = pv_ref[...].astype(BF16)

    def at(t):
        t = jnp.minimum(t, n_steps - 1)
        return t % ns, t // ns

    def before(t):
        return at(jnp.maximum(t - 1, 0))

    q_spec = pl.BlockSpec((sub * BLK, WIDTH), at)
    kc_spec = pl.BlockSpec((sub * BLK, kvw), at)
    kp_spec = pl.BlockSpec((BLK, kvw), lambda t: (jnp.maximum(sub * at(t)[0] - 1, 0), at(t)[1]))
    b_spec = pl.BlockSpec((2, N_HEADS, BLK, 2 * BLK), lambda t: (0, 0, 0, 0))
    dkv_spec = pl.BlockSpec((sub * BLK, kvw), before)
    return pl.pallas_call(
        body, name=name, grid=(n_steps + 1,),
        in_specs=[pl.BlockSpec(memory_space=pltpu.SMEM), q_spec, kp_spec, kc_spec, kp_spec, kc_spec,
                  q_spec, q_spec, q_spec, b_spec],
        out_specs=[q_spec, dkv_spec, dkv_spec,
                   pl.BlockSpec((N_HEADS, BLK, 2 * BLK), lambda t: (0, 0, 0)),
                   pl.BlockSpec((BLK, WIDTH), lambda t: (0, 0))],
        out_shape=[jax.ShapeDtypeStruct((L, dil * WIDTH), BF16),
                   jax.ShapeDtypeStruct((L, dil * kvw), BF16),
                   jax.ShapeDtypeStruct((L, dil * kvw), BF16),
                   jax.ShapeDtypeStruct((N_HEADS, BLK, 2 * BLK), F32),
                   jax.ShapeDtypeStruct((BLK, WIDTH), F32)],
        scratch_shapes=[pltpu.VMEM((sub * BLK, kvw), F32), pltpu.VMEM((sub * BLK, kvw), F32)],
        compiler_params=_params(("arbitrary",)),
    )(sinks, q, k, k, v, v, o, do, lse, bias)


def _merge_wo(x, oa, o1, o2, o3, l1, l2, l3, ga, gb, wo, gf, *, tm=512):
    T = x.shape[0]

    def body(x_ref, oa_ref, o1_ref, o2_ref, o3_ref, l1_ref, l2_ref, l3_ref, ga_ref, gb_ref, wo_ref, gf_ref,
             x2_ref, mix_ref, h2_ref, ob1_ref, ob4_ref, ob16_ref, ls1_ref, ls4_ref, ls16_ref, so2, so3, sl2, sl3):
        _unstride(o2_ref, so2, BRANCHES[1][1], tm)
        _unstride(o3_ref, so3, BRANCHES[2][1], tm)
        _unstride(l2_ref, sl2, BRANCHES[1][1], tm)
        _unstride(l3_ref, sl3, BRANCHES[2][1], tm)
        la, lb, lc = l1_ref[...], _scr_get(sl2), _scr_get(sl3)
        m = jnp.maximum(jnp.maximum(la, lb), lc)
        ea, eb, ec = jnp.exp(la - m), jnp.exp(lb - m), jnp.exp(lc - m)
        den = ea + eb + ec
        inv = 1.0 / den
        ob = (ea * o1_ref[...].astype(F32) + eb * _scr_get(so2) + ec * _scr_get(so3)) * inv
        _scr_put(so2, ob)
        _scr_put(sl2, m + jnp.log(den))
        for (_, dil), o_ref, l_ref in zip(BRANCHES, (ob1_ref, ob4_ref, ob16_ref), (ls1_ref, ls4_ref, ls16_ref)):
            _restride(so2, o_ref, dil, tm)
            _restride(sl2, l_ref, dil, tm)
        oav = oa_ref[...].astype(F32)
        ra = lax.rsqrt(jnp.mean(oav * oav, axis=-1, keepdims=True) + EPS)
        rb = lax.rsqrt(jnp.mean(ob * ob, axis=-1, keepdims=True) + EPS)
        mix_ref[:, :WIDTH] = (oav * ra * ga_ref[...]).astype(BF16)
        mix_ref[:, WIDTH:] = (ob * rb * gb_ref[...]).astype(BF16)
        x2 = x_ref[...] + jnp.dot(mix_ref[...], wo_ref[...], preferred_element_type=F32)
        x2_ref[...] = x2
        r2 = lax.rsqrt(jnp.mean(x2 * x2, axis=-1, keepdims=True) + EPS)
        h2_ref[...] = (x2 * r2 * gf_ref[...]).astype(BF16)

    row = lambda w_: pl.BlockSpec((tm, w_), lambda i: (i, 0))
    full = lambda a: pl.BlockSpec(a.shape, lambda i: (0, 0))
    return pl.pallas_call(
        body, name="merge_wo", grid=(T // tm,),
        in_specs=[row(D_MODEL), row(WIDTH)] + _view_specs(tm) * 2 + [full(ga), full(gb), full(wo), full(gf)],
        out_specs=[row(D_MODEL), row(D_MODEL), row(D_MODEL)] + _view_specs(tm) * 2,
        out_shape=[jax.ShapeDtypeStruct((T, D_MODEL), F32), jax.ShapeDtypeStruct((T, D_MODEL), BF16),
                   jax.ShapeDtypeStruct((T, D_MODEL), BF16)] + _view_shapes(T, BF16) + _view_shapes(T, F32),
        scratch_shapes=[_scr(tm)] * 4,
        compiler_params=_params(("arbitrary",)),
    )(x, oa, o1, o2, o3, l1, l2, l3, ga, gb, wo, gf)


def _ffn_up(h2, wgt, wut, *, tm=512, fc=D_FF, rc=512, cc=256):
    T = h2.shape[0]

    def body(h_ref, wg_ref, wu_ref, gate_ref, up_ref, act_ref):
        for s in range(0, tm, rc):
            h = h_ref[s:s + rc, :]
            for c in range(0, fc, cc):
                gt = lax.dot_general(h, wg_ref[c:c + cc, :], NT, preferred_element_type=F32)
                u = lax.dot_general(h, wu_ref[c:c + cc, :], NT, preferred_element_type=F32)
                gate_ref[s:s + rc, c:c + cc] = gt.astype(BF16)
                up_ref[s:s + rc, c:c + cc] = u.astype(BF16)
                act_ref[s:s + rc, c:c + cc] = (gt * _sigmoid(gt) * u).astype(BF16)

    rowd = pl.BlockSpec((tm, D_MODEL), lambda i, c: (i, 0))
    wrow = pl.BlockSpec((fc, D_MODEL), lambda i, c: (c, 0))
    oc = pl.BlockSpec((tm, fc), lambda i, c: (i, c))
    return pl.pallas_call(
        body, name="ffn_up", grid=(T // tm, D_FF // fc),
        in_specs=[rowd, wrow, wrow],
        out_specs=[oc, oc, oc],
        out_shape=[jax.ShapeDtypeStruct((T, D_FF), BF16)] * 3,
        compiler_params=_params(("arbitrary", "arbitrary")),
    )(h2, wgt, wut)


def _ffn_down_loss(act, wd, x2, tgt, g, *, tm=1024, rc=256):
    T = x2.shape[0]

    def body(act_ref, wd_ref, x2_ref, tgt_ref, g_ref, dx_ref, dxb_ref, loss_ref, dg_ref):
        @pl.when(pl.program_id(0) == 0)
        def _():
            loss_ref[...] = jnp.zeros_like(loss_ref)
            dg_ref[...] = jnp.zeros_like(dg_ref)

        gv = g_ref[...]
        lsum = jnp.zeros((1, 1), F32)
        dgs = jnp.zeros((1, D_MODEL), F32)
        for c in range(0, tm, rc):
            x3 = x2_ref[c:c + rc, :] + jnp.dot(act_ref[c:c + rc, :], wd_ref[...], preferred_element_type=F32)
            r = lax.rsqrt(jnp.mean(x3 * x3, axis=-1, keepdims=True) + EPS)
            xh = x3 * r
            diff = xh * gv - tgt_ref[c:c + rc, :]
            lsum = lsum + jnp.sum(jnp.sum(diff * diff, axis=-1, keepdims=True), axis=0, keepdims=True)
            dy = diff * (1.0 / D_MODEL)
            dgs = dgs + jnp.sum(dy * xh, axis=0, keepdims=True)
            dx = _rms_bwd(dy, xh, r, gv)
            dx_ref[c:c + rc, :] = dx
            dxb_ref[c:c + rc, :] = dx.astype(BF16)
        loss_ref[...] += lsum * (0.5 / D_MODEL)
        dg_ref[...] += dgs

    rowd = pl.BlockSpec((tm, D_MODEL), lambda i: (i, 0))
    return pl.pallas_call(
        body, name="ffn_down_loss", grid=(T // tm,),
        in_specs=[pl.BlockSpec((tm, D_FF), lambda i: (i, 0)),
                  pl.BlockSpec((D_FF, D_MODEL), lambda i: (0, 0), pipeline_mode=pl.Buffered(1)),
                  rowd, rowd, pl.BlockSpec(g.shape, lambda i: (0, 0))],
        out_specs=[rowd, rowd, pl.BlockSpec((1, 1), lambda i: (0, 0)), pl.BlockSpec((1, D_MODEL), lambda i: (0, 0))],
        out_shape=[jax.ShapeDtypeStruct((T, D_MODEL), F32), jax.ShapeDtypeStruct((T, D_MODEL), BF16),
                   jax.ShapeDtypeStruct((1, 1), F32), jax.ShapeDtypeStruct((1, D_MODEL), F32)],
        compiler_params=_params(("arbitrary",)),
    )(act, wd, x2, tgt, g)


def _ffn_bwd(dx3, gate, up, wd, wgt, wut, x2, g, *, tm=256, cc=256):
    T = x2.shape[0]

    def body(dx_ref, gate_ref, up_ref, wd_ref, wg_ref, wu_ref, x2_ref, g_ref,
             dgate_ref, dup_ref, dx2b_ref, dg_ref):
        @pl.when(pl.program_id(0) == 0)
        def _():
            dg_ref[...] = jnp.zeros_like(dg_ref)

        dxb = dx_ref[...].astype(BF16)
        for c in range(0, D_FF, cc):
            dact = lax.dot_general(dxb, wd_ref[c:c + cc, :], NT, preferred_element_type=F32)
            gt = gate_ref[:, c:c + cc].astype(F32)
            u = up_ref[:, c:c + cc].astype(F32)
            sg = _sigmoid(gt)
            a = dact * sg
            dgate_ref[:, c:c + cc] = (a * u * ((1.0 + gt) - gt * sg)).astype(BF16)
            dup_ref[:, c:c + cc] = (a * gt).astype(BF16)
        dh = (jnp.dot(dgate_ref[...], wg_ref[...], preferred_element_type=F32)
              + jnp.dot(dup_ref[...], wu_ref[...], preferred_element_type=F32))
        xv = x2_ref[...]
        r = lax.rsqrt(jnp.mean(xv * xv, axis=-1, keepdims=True) + EPS)
        xh = xv * r
        dg_ref[...] += jnp.sum(dh * xh, axis=0, keepdims=True)
        dx2b_ref[...] = (dx_ref[...] + _rms_bwd(dh, xh, r, g_ref[...])).astype(BF16)

    rowd = pl.BlockSpec((tm, D_MODEL), lambda i: (i, 0))
    rowf = pl.BlockSpec((tm, D_FF), lambda i: (i, 0))
    wfull = pl.BlockSpec((D_FF, D_MODEL), lambda i: (0, 0), pipeline_mode=pl.Buffered(1))
    return pl.pallas_call(
        body, name="ffn_bwd", grid=(T // tm,),
        in_specs=[rowd, rowf, rowf, wfull, wfull, wfull, rowd, pl.BlockSpec(g.shape, lambda i: (0, 0))],
        out_specs=[rowf, rowf, rowd, pl.BlockSpec((1, D_MODEL), lambda i: (0, 0))],
        out_shape=[jax.ShapeDtypeStruct((T, D_FF), BF16), jax.ShapeDtypeStruct((T, D_FF), BF16),
                   jax.ShapeDtypeStruct((T, D_MODEL), BF16), jax.ShapeDtypeStruct((1, D_MODEL), F32)],
        compiler_params=_params(("arbitrary",)),
    )(dx3, gate, up, wd, wgt, wut, x2, g)


def _matmul_tn(a, b, *, tk, tn, tt=2048, out_dtype=BF16, name):
    T, K = a.shape
    N = b.shape[1]
    nt = T // tt

    def body(a_ref, b_ref, o_ref, acc_ref):
        part = lax.dot_general(a_ref[...], b_ref[...], TN, preferred_element_type=F32)

        @pl.when(pl.program_id(2) == 0)
        def _():
            acc_ref[...] = part

        @pl.when(pl.program_id(2) > 0)
        def _():
            acc_ref[...] += part

        @pl.when(pl.program_id(2) == nt - 1)
        def _():
            o_ref[...] = acc_ref[...].astype(out_dtype)

    return pl.pallas_call(
        body, name=name, grid=(K // tk, N // tn, nt),
        in_specs=[pl.BlockSpec((tt, tk), lambda i, j, t: (t, i)), pl.BlockSpec((tt, tn), lambda i, j, t: (t, j))],
        out_specs=pl.BlockSpec((tk, tn), lambda i, j, t: (i, j)),
        out_shape=jax.ShapeDtypeStruct((K, N), out_dtype),
        scratch_shapes=[pltpu.VMEM((tk, tn), F32)],
        compiler_params=_params(("arbitrary", "arbitrary", "arbitrary")),
    )(a, b)


def _wo_bwd(dx2b, wo, oa, ob, ga, gb, mixed, *, tm=1024, wc=256):
    T = dx2b.shape[0]
    n_tiles = T // tm

    def body(dx_ref, wo_ref, oa_ref, ob_ref, ga_ref, gb_ref, mix_ref,
             doa_ref, dob1_ref, dob4_ref, dob16_ref, dga_ref, dgb_ref, dwo_ref, scr, dw_acc):
        @pl.when(pl.program_id(0) == 0)
        def _():
            dga_ref[...] = jnp.zeros_like(dga_ref)
            dgb_ref[...] = jnp.zeros_like(dgb_ref)
            dw_acc[...] = jnp.zeros_like(dw_acc)

        dxv = dx_ref[...]
        for c in range(0, D_MODEL, wc):
            dw_acc[c:c + wc, :] += lax.dot_general(mix_ref[:, c:c + wc], dxv, TN, preferred_element_type=F32)

        @pl.when(pl.program_id(0) == n_tiles - 1)
        def _():
            dwo_ref[...] = dw_acc[...].astype(BF16)

        dm = lax.dot_general(dxv, wo_ref[...], NT, preferred_element_type=F32)
        for o_ref, g_ref, dg_ref, sl in ((oa_ref, ga_ref, dga_ref, slice(0, WIDTH)),
                                         (ob_ref, gb_ref, dgb_ref, slice(WIDTH, 2 * WIDTH))):
            ov = o_ref[...].astype(F32)
            r = lax.rsqrt(jnp.mean(ov * ov, axis=-1, keepdims=True) + EPS)
            xh = ov * r
            d = dm[:, sl]
            dg_ref[...] += jnp.sum(d * xh, axis=0, keepdims=True)
            do = _rms_bwd(d, xh, r, g_ref[...])
            if o_ref is oa_ref:
                doa_ref[...] = do.astype(BF16)
            else:
                _scr_put(scr, do)
                for (_, dil), v_ref in zip(BRANCHES, (dob1_ref, dob4_ref, dob16_ref)):
                    _restride(scr, v_ref, dil, tm)

    row = lambda w_: pl.BlockSpec((tm, w_), lambda i: (i, 0))
    full = lambda a: pl.BlockSpec(a.shape, lambda i: (0, 0))
    return pl.pallas_call(
        body, name="wo_bwd", grid=(T // tm,),
        in_specs=[row(D_MODEL), full(wo), row(WIDTH), row(WIDTH), full(ga), full(gb), row(D_MODEL)],
        out_specs=[row(WIDTH)] + _view_specs(tm)
        + [pl.BlockSpec((1, WIDTH), lambda i: (0, 0)), pl.BlockSpec((1, WIDTH), lambda i: (0, 0)), full(wo)],
        out_shape=[jax.ShapeDtypeStruct((T, WIDTH), BF16)] + _view_shapes(T, BF16)
        + [jax.ShapeDtypeStruct((1, WIDTH), F32), jax.ShapeDtypeStruct((1, WIDTH), F32),
           jax.ShapeDtypeStruct((D_MODEL, D_MODEL), BF16)],
        scratch_shapes=[_scr(tm), pltpu.VMEM((D_MODEL, D_MODEL), F32)],
        compiler_params=_params(("arbitrary",)),
    )(dx2b, wo, oa, ob, ga, gb, mixed)


def _inproj_bwd(dqa, dka, dva, dqs, dks, dvs, cos, sin, w, x, dx2, g, h1, *, tm=512, wc=256):
    T = dqa.shape[0]
    n_tiles = T // tm

    def body(dqa_ref, dka_ref, dva_ref, q1, q2, q3, k1, k2, k3, v1, v2, v3, cos_ref, sin_ref, w_ref, x_ref, dx2_ref,
             g_ref, h1_ref, db_ref, gx_ref, dg_ref, dw_ref, dp_ref, dw_acc, acc, tmp):
        @pl.when(pl.program_id(0) == 0)
        def _():
            db_ref[...] = jnp.zeros_like(db_ref)
            dg_ref[...] = jnp.zeros_like(dg_ref)
            dw_acc[...] = jnp.zeros_like(dw_acc)

        cosv = cos_ref[...]
        sinv = sin_ref[...]
        lane = lax.broadcasted_iota(jnp.int32, (tm, PAIR), 1)
        first = (lane % HEAD_DIM) < (HEAD_DIM // 2)

        def put(off, val):
            dp_ref[:, off:off + PAIR] = val.astype(BF16)
            db_ref[:, off:off + PAIR] += jnp.sum(val, axis=0, keepdims=True)

        for src, off, width in ((dqa_ref, 0, 512), (dka_ref, 512, 256)):
            for j in range(0, width, PAIR):
                d = src[:, j:j + PAIR].astype(F32)
                put(off + j, d * cosv - _rope_rot(d, first) * sinv)
        for j in range(0, 256, PAIR):
            put(768 + j, dva_ref[:, j:j + PAIR].astype(F32))
        for (a, b, c), off in (((q1, q2, q3), 1024), ((k1, k2, k3), 1536), ((v1, v2, v3), 2048)):
            _unstride(b, acc, BRANCHES[1][1], tm)
            _unstride(c, tmp, BRANCHES[2][1], tm)
            for j in range(N_CHUNK):
                put(off + j * PAIR, a[:, j * PAIR:(j + 1) * PAIR].astype(F32) + acc[j] + tmp[j])

        dh = jnp.dot(dp_ref[...], w_ref[...], preferred_element_type=F32)
        xv = x_ref[...]
        r = lax.rsqrt(jnp.mean(xv * xv, axis=-1, keepdims=True) + EPS)
        xh = xv * r
        dg_ref[...] += jnp.sum(dh * xh, axis=0, keepdims=True)
        gx_ref[...] = dx2_ref[...].astype(F32) + _rms_bwd(dh, xh, r, g_ref[...])

        h1v = h1_ref[...]
        for c in range(0, D_INP, wc):
            dw_acc[c:c + wc, :] += lax.dot_general(dp_ref[:, c:c + wc], h1v, TN, preferred_element_type=F32)

        @pl.when(pl.program_id(0) == n_tiles - 1)
        def _():
            pltpu.sync_copy(dw_acc, dw_ref)

    row = lambda w_: pl.BlockSpec((tm, w_), lambda i: (i, 0))
    full = lambda a: pl.BlockSpec(a.shape, lambda i: (0, 0))
    return pl.pallas_call(
        body, name="inproj_bwd", grid=(n_tiles,),
        in_specs=[row(512), row(256), row(256)] + _view_specs(tm) * 3 + [row(PAIR), row(PAIR)]
        + [pl.BlockSpec(w.shape, lambda i: (0, 0), pipeline_mode=pl.Buffered(1)), row(D_MODEL), row(D_MODEL), full(g),
           row(D_MODEL)],
        out_specs=[pl.BlockSpec((1, D_INP), lambda i: (0, 0)), row(D_MODEL),
                   pl.BlockSpec((1, D_MODEL), lambda i: (0, 0)), pl.BlockSpec(memory_space=pl.ANY)],
        out_shape=[jax.ShapeDtypeStruct((1, D_INP), F32), jax.ShapeDtypeStruct((T, D_MODEL), F32),
                   jax.ShapeDtypeStruct((1, D_MODEL), F32), jax.ShapeDtypeStruct((D_INP, D_MODEL), F32)],
        scratch_shapes=[pltpu.VMEM((tm, D_INP), BF16), pltpu.VMEM((D_INP, D_MODEL), F32), _scr(tm), _scr(tm)],
        compiler_params=_params(("arbitrary",)),
    )(dqa, dka, dva, *dqs, *dks, *dvs, cos, sin, w, x, dx2, g, h1)


def _bias_sink_grads(dsums, bmaps, dsk):
    def body(s1, s2, s3, m1, m2, m3, dsk_ref, drel_ref, dsink_ref):
        row = lax.broadcasted_iota(jnp.int32, (N_HEADS, 128), 0)
        lane = lax.broadcasted_iota(jnp.int32, (N_HEADS, 128), 1)
        out = jnp.zeros((N_HEADS, 128), F32)
        for s_ref, m_ref in ((s1, m1), (s2, m2), (s3, m3)):
            bm = m_ref[...]
            for h in range(N_HEADS):
                a = s_ref[h]
                for b in range(REL_BUCKETS):
                    v = jnp.sum(jnp.sum(jnp.where(bm == b, a, 0.0), axis=-1, keepdims=True), axis=0, keepdims=True)
                    out = out + jnp.where((row == h) & (lane == b), v, 0.0)
        drel_ref[...] = out
        dsink_ref[...] = -jnp.sum(dsk_ref[...], axis=0, keepdims=True)

    vm = pl.BlockSpec(memory_space=pltpu.VMEM)
    return pl.pallas_call(
        body, name="bias_sink_grads",
        in_specs=[vm] * 7, out_specs=[vm, vm],
        out_shape=[jax.ShapeDtypeStruct((N_HEADS, 128), F32), jax.ShapeDtypeStruct((1, WIDTH), F32)],
        compiler_params=_params(),
    )(*dsums, *bmaps, dsk)


def _all_gather(blk, *, name):
    R, C = blk.shape

    def body(x_ref, out_ref, send_sems, recv_sems, local_sem):
        x, y, c = lax.axis_index("x"), lax.axis_index("y"), lax.axis_index("c")
        me, sibling = (x, y, c), (x, y, 1 - c)
        chips = [(1 - x, y), (x, 1 - y), (1 - x, 1 - y)]

        def slot(px, py, pc):
            return out_ref.at[4 * px + 2 * py + pc]

        def copy(k, block, to, src=None):
            return pltpu.make_async_remote_copy(
                src_ref=slot(*block) if src is None else src, dst_ref=slot(*block),
                send_sem=send_sems.at[k], recv_sem=recv_sems.at[k], device_id=to, device_id_type=MESH)

        mine = pltpu.make_async_copy(x_ref, slot(*me), local_sem)
        mine.start()
        first = [copy(0, me, sibling, src=x_ref)]
        first += [copy(1 + j, me, (*chip, c), src=x_ref) for j, chip in enumerate(chips)]
        for cp in first:
            cp.start()
        passed = [copy(4 + j, (*chip, c), sibling) for j, chip in enumerate(chips)]
        for j, chip in enumerate(chips):
            copy(1 + j, (*chip, c), me).wait_recv()
            passed[j].start()
        copy(0, sibling, me).wait_recv()
        for j, chip in enumerate(chips):
            copy(4 + j, (*chip, 1 - c), me).wait_recv()
        for cp in first + passed:
            cp.wait_send()
        mine.wait()

    return pl.pallas_call(
        body, name=name,
        in_specs=[pl.BlockSpec(memory_space=pl.ANY)], out_specs=pl.BlockSpec(memory_space=pl.ANY),
        out_shape=jax.ShapeDtypeStruct((N_DEV, R, C), blk.dtype),
        scratch_shapes=[pltpu.SemaphoreType.DMA((7,)), pltpu.SemaphoreType.DMA((7,)), pltpu.SemaphoreType.DMA],
        compiler_params=pltpu.CompilerParams(has_side_effects=True),
    )(blk)


def _peers(x, y, c):
    return [(x ^ (k >> 2), y ^ ((k >> 1) & 1), c ^ (k & 1)) for k in range(1, N_DEV)]


_HBM = pl.BlockSpec(memory_space=pltpu.HBM)
_SEM = pl.BlockSpec(memory_space=pltpu.SEMAPHORE)
_EFFECT = pltpu.SideEffectType.DATAFLOW_SIDE_EFFECTING


def _peer_list(x, y, c, near):
    if near:
        return [(x, y, 1 - c), (1 - x, y, c), (x, 1 - y, c), (1 - x, 1 - y, c)]
    return _peers(x, y, c)


def _exchange_start(srcs, *, gather, name, near=False):
    n = len(srcs)
    n_peers = 4 if near else N_DEV - 1
    lands = [lax.empty((N_DEV,) + s.shape[-2:], s.dtype) for s in srcs]

    def body(*refs):
        src_refs, land_refs = refs[:n], refs[n:2 * n]
        send_sems, recv_sems = refs[2 * n], refs[2 * n + 1]
        token = refs[-1]
        x, y, c = lax.axis_index("x"), lax.axis_index("y"), lax.axis_index("c")
        mine = 4 * x + 2 * y + c
        for a in range(n):
            for k, peer in enumerate(_peer_list(x, y, c, near)):
                dest = 4 * peer[0] + 2 * peer[1] + peer[2]
                j = a * n_peers + k
                pltpu.make_async_remote_copy(
                    src_ref=src_refs[a] if gather else src_refs[a].at[dest], dst_ref=land_refs[a].at[mine],
                    send_sem=send_sems.at[j], recv_sem=recv_sems.at[j], device_id=peer, device_id_type=MESH).start()
        token[...] = jnp.zeros_like(token)

    sems = pltpu.SemaphoreType.DMA((n * n_peers,))
    out = pl.pallas_call(
        body, name=name,
        out_shape=(sems, sems) + tuple(pltpu.HBM(a.shape, a.dtype) for a in list(srcs) + lands)
        + (jax.ShapeDtypeStruct((8, 128), F32),),
        in_specs=(_HBM,) * (2 * n), out_specs=(_SEM, _SEM) + (_HBM,) * (2 * n) + (pl.BlockSpec(memory_space=pltpu.VMEM),),
        input_output_aliases={i: 2 + i for i in range(2 * n)},
        compiler_params=pltpu.CompilerParams(has_side_effects=_EFFECT),
    )(*[pltpu.with_memory_space_constraint(a, pltpu.HBM) for a in list(srcs) + lands])
    return out[:-1], out[-1]


def _exchange_wait(state, after, *, gather, name, near=False):
    send_sems, recv_sems = state[0], state[1]
    n = (len(state) - 2) // 2
    n_peers = 4 if near else N_DEV - 1
    arrays = state[2:]

    def body(*refs):
        src_refs, land_refs = refs[:n], refs[n:2 * n]
        send_sems, recv_sems = refs[2 * n], refs[2 * n + 1]
        x, y, c = lax.axis_index("x"), lax.axis_index("y"), lax.axis_index("c")
        for a in range(n):
            for k, peer in enumerate(_peer_list(x, y, c, near)):
                other = 4 * peer[0] + 2 * peer[1] + peer[2]
                j = a * n_peers + k
                copy = pltpu.make_async_remote_copy(
                    src_ref=src_refs[a] if gather else src_refs[a].at[other], dst_ref=land_refs[a].at[other],
                    send_sem=send_sems.at[j], recv_sem=recv_sems.at[j], device_id=peer, device_id_type=MESH)
                copy.wait_send()
                copy.wait_recv()

    out = pl.pallas_call(
        body, name=name,
        out_shape=tuple(pltpu.HBM(a.shape, a.dtype) for a in arrays),
        in_specs=(_HBM,) * (2 * n) + (_SEM, _SEM, pl.BlockSpec(memory_space=pl.ANY)), out_specs=(_HBM,) * (2 * n),
        input_output_aliases={i: i for i in range(2 * n)},
        compiler_params=pltpu.CompilerParams(has_side_effects=_EFFECT),
    )(*arrays, send_sems, recv_sems, after)
    mine = 4 * lax.axis_index("x") + 2 * lax.axis_index("y") + lax.axis_index("c")
    own = out[:n] if gather else [lax.dynamic_index_in_dim(s, mine, 0, keepdims=False) for s in out[:n]]
    return [lax.dynamic_update_slice(g, o[None], (mine, 0, 0)) for g, o in zip(out[n:], own)]


def _forward_start(lands, *, name):
    n = len(lands)

    def body(*refs):
        land_refs, send_sems, recv_sems, token = refs[:n], refs[n], refs[n + 1], refs[-1]
        x, y, c = lax.axis_index("x"), lax.axis_index("y"), lax.axis_index("c")
        for a in range(n):
            for j, (px, py) in enumerate(((1 - x, y), (x, 1 - y), (1 - x, 1 - y))):
                blk = 4 * px + 2 * py + c
                pltpu.make_async_remote_copy(
                    src_ref=land_refs[a].at[blk], dst_ref=land_refs[a].at[blk], send_sem=send_sems.at[3 * a + j],
                    recv_sem=recv_sems.at[3 * a + j], device_id=(x, y, 1 - c), device_id_type=MESH).start()
        token[...] = jnp.zeros_like(token)

    sems = pltpu.SemaphoreType.DMA((3 * n,))
    out = pl.pallas_call(
        body, name=name,
        out_shape=(sems, sems) + tuple(pltpu.HBM(a.shape, a.dtype) for a in lands) + (jax.ShapeDtypeStruct((8, 128), F32),),
        in_specs=(_HBM,) * n, out_specs=(_SEM, _SEM) + (_HBM,) * n + (pl.BlockSpec(memory_space=pltpu.VMEM),),
        input_output_aliases={i: 2 + i for i in range(n)},
        compiler_params=pltpu.CompilerParams(has_side_effects=_EFFECT),
    )(*[pltpu.with_memory_space_constraint(a, pltpu.HBM) for a in lands])
    return out[:-1], out[-1]


def _forward_wait(state, after, *, name):
    send_sems, recv_sems = state[0], state[1]
    lands = state[2:]
    n = len(lands)

    def body(*refs):
        land_refs, send_sems, recv_sems = refs[:n], refs[n], refs[n + 1]
        x, y, c = lax.axis_index("x"), lax.axis_index("y"), lax.axis_index("c")
        for a in range(n):
            for j, (px, py) in enumerate(((1 - x, y), (x, 1 - y), (1 - x, 1 - y))):
                copy = pltpu.make_async_remote_copy(
                    src_ref=land_refs[a].at[4 * px + 2 * py + c], dst_ref=land_refs[a].at[4 * px + 2 * py + 1 - c],
                    send_sem=send_sems.at[3 * a + j], recv_sem=recv_sems.at[3 * a + j], device_id=(x, y, 1 - c),
                    device_id_type=MESH)
                copy.wait_send()
                copy.wait_recv()

    return pl.pallas_call(
        body, name=name,
        out_shape=tuple(pltpu.HBM(a.shape, a.dtype) for a in lands),
        in_specs=(_HBM,) * n + (_SEM, _SEM, pl.BlockSpec(memory_space=pl.ANY)), out_specs=(_HBM,) * n,
        input_output_aliases={i: i for i in range(n)},
        compiler_params=pltpu.CompilerParams(has_side_effects=_EFFECT),
    )(*lands, send_sems, recv_sems, after)


def _adam_math(w, g, m, v):
    m = ADAM_B1 * m + (1.0 - ADAM_B1) * g
    v = ADAM_B2 * v + (1.0 - ADAM_B2) * (g * g)
    m_hat = m / (1.0 - ADAM_B1 ** ADAM_STEP)
    v_hat = v / (1.0 - ADAM_B2 ** ADAM_STEP)
    delta = -ADAM_LR * (m_hat / (jnp.sqrt(v_hat) + ADAM_EPS) + ADAM_WD * w)
    return delta, m, v


def _adamw(parts, w, m, v, *, name):
    R, C = w.shape
    n_parts = parts.shape[0]
    tr = R // 2
    assert tr % 16 == 0

    def body(p_ref, w_ref, m_ref, v_ref, g_ref, d_ref, nm_ref, nv_ref):
        g = p_ref[0].astype(F32)
        for s in range(1, n_parts):
            g = g + p_ref[s].astype(F32)
        d, nm, nv = _adam_math(w_ref[...], g, m_ref[...], v_ref[...])
        g_ref[...] = g
        d_ref[...] = d
        nm_ref[...] = nm
        nv_ref[...] = nv

    blk = pl.BlockSpec((tr, C), lambda i: (i, 0))
    return pl.pallas_call(
        body, name=name, grid=(R // tr,),
        in_specs=[pl.BlockSpec((n_parts, tr, C), lambda i: (0, i, 0)), blk, blk, blk],
        out_specs=[blk] * 4, out_shape=[jax.ShapeDtypeStruct((R, C), F32)] * 4,
        compiler_params=_params(("arbitrary",)),
    )(parts, w, m, v)


def _adamw_small(parts, w, m, v):
    def body(p_ref, w_ref, m_ref, v_ref, g_ref, d_ref, nm_ref, nv_ref):
        g = p_ref[0]
        for s in range(1, N_DEV):
            g = g + p_ref[s]
        d, nm, nv = _adam_math(w_ref[...], g, m_ref[...], v_ref[...])
        g_ref[...] = g
        d_ref[...] = d
        nm_ref[...] = nm
        nv_ref[...] = nv

    vm = pl.BlockSpec(memory_space=pltpu.VMEM)
    return pl.pallas_call(
        body, name="adamw_small", in_specs=[vm] * 4, out_specs=[vm] * 4,
        out_shape=[jax.ShapeDtypeStruct((SMALL_ROWS, 128), F32)] * 4, compiler_params=_params(),
    )(parts, w, m, v)


def _t5_bucket(dist):
    max_exact = REL_BUCKETS // 2
    df = jnp.maximum(dist, 1).astype(F32)
    large = max_exact + (jnp.log(df / max_exact) / math.log(REL_MAX_DISTANCE / max_exact)
                         * (REL_BUCKETS - max_exact)).astype(jnp.int32)
    large = jnp.minimum(large, REL_BUCKETS - 1)
    return jnp.where(dist < max_exact, dist, large)


def _band_tables(rel_table, dil, n_back):
    qi = jnp.arange(BLK)[:, None]
    kj = jnp.arange(2 * BLK)[None, :]
    delta = BLK + qi - kj
    in_band = (delta >= 0) & (delta <= n_back)
    if rel_table is None:
        vals = jnp.zeros((N_HEADS, BLK, 2 * BLK), F32)
        bmap = None
    else:
        bucket = _t5_bucket(jnp.clip(delta, 0, n_back) * dil)
        vals = jnp.zeros((N_HEADS, BLK, 2 * BLK), F32)
        for b in range(REL_BUCKETS):
            vals = jnp.where((bucket == b)[None], rel_table[b][:, None, None], vals)
        bmap = jnp.where(in_band, bucket, -1).astype(jnp.int32)
    later = jnp.where(in_band[None], vals, NEG)
    first = jnp.where((in_band & (kj >= BLK))[None], vals, NEG)
    return jnp.stack([later, first]), bmap


def _rope_tables(T):
    half = HEAD_DIM // 2
    inv_freq = ROPE_THETA ** (-jnp.arange(half, dtype=F32) / half)
    ang = jnp.arange(T, dtype=F32)[:, None] * inv_freq[None, :]
    cos, sin = jnp.cos(ang), jnp.sin(ang)
    return jnp.tile(cos, (1, 4)), jnp.tile(jnp.concatenate([-sin, sin], axis=1), (1, 2))


def _widen_in(a, axis):
    sl = lambda lo, hi: lax.slice_in_dim(a, lo, hi, axis=axis)
    dup = lambda lo: [sl(lo, lo + 64), sl(lo, lo + 64), sl(lo + 64, lo + 128), sl(lo + 64, lo + 128)]
    return jnp.concatenate([sl(0, 512)] + dup(512) + dup(640) + [sl(768, D_IN)], axis=axis)


def _fold_in(a, axis):
    sl = lambda lo, hi: lax.slice_in_dim(a, lo, hi, axis=axis)
    fold = lambda lo: [sl(lo, lo + 64) + sl(lo + 64, lo + 128), sl(lo + 128, lo + 192) + sl(lo + 192, lo + 256)]
    return jnp.concatenate([sl(0, 512)] + fold(512) + fold(768) + [sl(1024, D_INP)], axis=axis)


def _local_step(x, tgt, g_attn, b_in, sinks, rel_table, g_out_a, g_out_b, g_ffn, g_final,
                win_fn, wo_fn, ffn_fn, early_fn):
    T = x.shape[0]
    cos, sin = _rope_tables(T)
    g_final2 = g_final.reshape(1, D_MODEL)
    sink8 = sinks.reshape(N_HEADS)

    bias_a, _ = _band_tables(None, 1, BLK - 1)
    tabs = [_band_tables(rel_table, dil, window // dil) for window, dil in BRANCHES]
    wint, token = win_fn(tabs[2][0])
    winp = _widen_in(wint, 0)
    binp = _widen_in(b_in, 1) + token[0, 0]

    h1, qa, ka, va, *qkv_b = _norm_proj(x, g_attn, winp, binp, cos, sin)
    qbs, kbs, vbs = qkv_b[0:3], qkv_b[3:6], qkv_b[6:9]
    oa, lse_a = _attn_fwd(qa, ka, va, bias_a, sink8, dil=1, kv_pairs=2, use_sink=True, name="attn_a_fwd")
    outs = [_attn_fwd(qbs[n], kbs[n], vbs[n], tabs[n][0], sink8, dil=dil, kv_pairs=4, use_sink=False,
                      name=f"attn_b{n}_fwd") for n, (_, dil) in enumerate(BRANCHES)]
    wo = wo_fn(outs[2][1])
    x2, mixed, h2, *ob_lse = _merge_wo(x, oa, outs[0][0], outs[1][0], outs[2][0], outs[0][1], outs[1][1], outs[2][1],
                                       g_out_a, g_out_b, wo, g_ffn)
    obs, lses = ob_lse[0:3], ob_lse[3:6]
    wgt, wut, wd = ffn_fn(h2)
    gate, up, act = _ffn_up(h2, wgt, wut)
    dx3, dx3b, loss, dg_final = _ffn_down_loss(act, wd, x2, tgt, g_final2)

    dgate, dup, dx2b, dg_ffn = _ffn_bwd(dx3, gate, up, wd, wgt, wut, x2, g_ffn)
    dwd = _matmul_tn(act, dx3b, tk=1408, tn=1024, name="dw_down")
    dwgt = _matmul_tn(dgate, h2, tk=1408, tn=1024, name="dw_gate")
    dwut = _matmul_tn(dup, h2, tk=1408, tn=1024, name="dw_up")
    doa, *dobs, dg_out_a, dg_out_b, dwo = _wo_bwd(dx2b, wo, oa, obs[0], g_out_a, g_out_b, mixed)
    early, token2 = early_fn(dict(w_o=dwo, w_gate=dwgt, w_up=dwut, w_down=dwd))
    sink8b = sink8 + token2[0, 0]

    dqa, dka, dva, _, dsk = _attn_bwd(qa, ka, va, oa, doa, lse_a, bias_a, sink8b, dil=1, kv_pairs=2, use_sink=True,
                                      name="attn_a_bwd", max_sub=4)
    res = [_attn_bwd(qbs[n], kbs[n], vbs[n], obs[n], dobs[n], lses[n], tabs[n][0], sink8b, dil=dil, kv_pairs=4,
                     use_sink=False, name=f"attn_b{n}_bwd") for n, (_, dil) in enumerate(BRANCHES)]
    dbp, grad_x, dg_attn, dwinp = _inproj_bwd(dqa, dka, dva, [r[0] for r in res], [r[1] for r in res],
                                              [r[2] for r in res], cos, sin, winp, x, dx2b, g_attn, h1)
    dwin = _fold_in(dwinp, 0)
    drel, dsink = _bias_sink_grads([r[3] for r in res], [t[1] for t in tabs], dsk)

    small = dict(
        g_attn=dg_attn, b_in=_fold_in(dbp, 1), sinks=dsink[:, ::HEAD_DIM], rel_table=drel[:, :REL_BUCKETS].T,
        g_out_a=dg_out_a, g_out_b=dg_out_b, g_ffn=dg_ffn, g_final=dg_final.reshape(D_MODEL))
    return loss[0, 0], grad_x, dwin, early, small


SMALL_NAMES = ("g_attn", "b_in", "sinks", "rel_table", "g_out_a", "g_out_b", "g_ffn", "g_final", "loss")


def _pack_small(vals):
    flat = jnp.concatenate([vals[n].reshape(-1).astype(F32) for n in SMALL_NAMES])
    return jnp.pad(flat, (0, SMALL_ROWS * 128 - flat.shape[0])).reshape(SMALL_ROWS, 128)


def _unpack_small(packed, like):
    flat = packed.reshape(-1)
    out, off = {}, 0
    for n in SMALL_NAMES:
        size = like[n].size
        out[n] = flat[off:off + size].reshape(like[n].shape)
        off += size
    return out


def kernel(x, g_attn, w_in, b_in, sinks, rel_table, g_out_a, g_out_b, w_o, g_ffn, w_gate, w_up, w_down, g_final, loss_target, m_g_attn, m_w_in, m_b_in, m_sinks, m_rel_table, m_g_out_a, m_g_out_b, m_w_o, m_g_ffn, m_w_gate, m_w_up, m_w_down, m_g_final, v_g_attn, v_w_in, v_b_in, v_sinks, v_rel_table, v_g_out_a, v_g_out_b, v_w_o, v_g_ffn, v_w_gate, v_w_up, v_w_down, v_g_final):
    rest_names = ("w_o", "w_gate", "w_up", "w_down")

    rest = [w_o[0].astype(BF16), w_gate[0].astype(BF16).T, w_up[0].astype(BF16).T, w_down[0].astype(BF16)]
    in_state, _ = _exchange_start([w_in[0].astype(BF16).T], gather=True, near=True, name="gather_w_in_start")
    later = {}

    def whole(got):
        return [g.reshape(N_DEV * g.shape[1], D_MODEL) for g in got]

    def win_fn(after):
        near = _exchange_wait(in_state, after, gather=True, near=True, name="gather_w_in_near")
        fwd_state, tok = _forward_start(near, name="gather_w_in_forward")
        wint = whole(_forward_wait(fwd_state, tok, name="gather_w_in_wait"))[0]
        wint, src = lax.optimization_barrier((wint, rest))
        later["wo"], token_o = _exchange_start(src[:1], gather=True, name="gather_w_o_start")
        token_o, ffn_src = lax.optimization_barrier((token_o, src[1:]))
        later["ffn"], token = _exchange_start(ffn_src, gather=True, name="gather_ffn_start")
        return wint, token + token_o

    def wo_fn(after):
        return whole(_exchange_wait(later["wo"], after, gather=True, name="gather_w_o_wait"))[0]

    def ffn_fn(after):
        return whole(_exchange_wait(later["ffn"], after, gather=True, name="gather_ffn_wait"))

    def early_fn(dws):
        return _exchange_start([dws[n].reshape(N_DEV, -1, D_MODEL) for n in rest_names], gather=False,
                               name="scatter_rest_start")

    loss_part, grad_x, dwint, early_state, small = _local_step(
        x[0], loss_target[0], g_attn, b_in, sinks, rel_table, g_out_a, g_out_b, g_ffn, g_final,
        win_fn, wo_fn, ffn_fn, early_fn)
    parts_in = dwint.astype(BF16).reshape(N_DEV, D_IN // N_DEV, D_MODEL)
    in_state, token3 = _exchange_start([parts_in], gather=False, name="scatter_w_in_start")
    got = _exchange_wait(early_state, token3, gather=False, name="scatter_rest_wait")

    def update(n, parts, w, m, v, transposed):
        if transposed:
            return [a.T[None] for a in _adamw(parts, w[0].T, m[0].T, v[0].T, name="adamw_" + n)]
        return [a[None] for a in _adamw(parts, w[0], m[0], v[0], name="adamw_" + n)]

    big = dict(w_o=update("w_o", got[0], w_o, m_w_o, v_w_o, False),
               w_gate=update("w_gate", got[1], w_gate, m_w_gate, v_w_gate, True),
               w_up=update("w_up", got[2], w_up, m_w_up, v_w_up, True),
               w_down=update("w_down", got[3], w_down, m_w_down, v_w_down, False))

    unused = jnp.zeros((1,), F32)
    ws = dict(g_attn=g_attn, b_in=b_in, sinks=sinks, rel_table=rel_table, g_out_a=g_out_a, g_out_b=g_out_b,
              g_ffn=g_ffn, g_final=g_final, loss=unused)
    ms = dict(g_attn=m_g_attn, b_in=m_b_in, sinks=m_sinks, rel_table=m_rel_table, g_out_a=m_g_out_a,
              g_out_b=m_g_out_b, g_ffn=m_g_ffn, g_final=m_g_final, loss=unused)
    vs = dict(g_attn=v_g_attn, b_in=v_b_in, sinks=v_sinks, rel_table=v_rel_table, g_out_a=v_g_out_a,
              g_out_b=v_g_out_b, g_ffn=v_g_ffn, g_final=v_g_final, loss=unused)
    sparts = _all_gather(_pack_small(dict(small, loss=loss_part)), name="gather_small")
    sm_packed = _adamw_small(sparts, _pack_small(ws), _pack_small(ms), _pack_small(vs))
    sm = [_unpack_small(a, ws) for a in sm_packed]
    loss = sm[0]["loss"][0]

    done = sm_packed[1][:1, :1] + sum(big[n][1][0, :1, :1] for n in rest_names)
    got_in = _exchange_wait(in_state, done, gather=False, name="scatter_w_in_wait")[0]
    big["w_in"] = update("w_in", got_in, w_in, m_w_in, v_w_in, True)

    order = ("g_attn", "w_in", "b_in", "sinks", "rel_table", "g_out_a", "g_out_b", "w_o", "g_ffn", "w_gate", "w_up",
             "w_down", "g_final")
    outs = [loss, grad_x[None]]
    for k in range(4):
        outs += [big[n][k] if n in big else sm[k][n] for n in order]
    return tuple(outs)
```

```python
import math

import jax
import jax.numpy as jnp
from jax import lax
from jax.experimental import pallas as pl
from jax.experimental.pallas import tpu as pltpu

F32 = jnp.float32
BF16 = jnp.bfloat16

N_DEV = 8
D_MODEL = 1024
HEAD_DIM = 64
N_HEADS = 8
PAIR = 2 * HEAD_DIM
WIDTH = N_HEADS * HEAD_DIM
D_IN = 2304
D_INP = 2560
D_FF = 2816
BLK = 128
ROPE_THETA = 150000.0
REL_BUCKETS = 32
REL_MAX_DISTANCE = 2048
EPS = 1e-5
NEG = -1e30
BRANCHES = ((128, 1), (512, 4), (2048, 16))
Q_SCALE = HEAD_DIM ** -0.5

ADAM_LR = 0.001
ADAM_B1 = 0.9
ADAM_B2 = 0.999
ADAM_EPS = 1e-08
ADAM_WD = 0.01
ADAM_STEP = 10

VMEM_LIMIT = 56 * 1024 * 1024
MESH = pl.DeviceIdType.MESH

NT = (((1,), (1,)), ((), ()))
TN = (((0,), (0,)), ((), ()))

SMALL_ROWS = 56


def _params(sem=None):
    return pltpu.CompilerParams(dimension_semantics=sem, vmem_limit_bytes=VMEM_LIMIT)


def _sigmoid(x):
    return 1.0 / (1.0 + jnp.exp2(x * (-1.0 / math.log(2.0))))


def _rms_bwd(dh, xh, r, g):
    u = dh * g
    return r * (u - xh * jnp.mean(u * xh, axis=-1, keepdims=True))


def _rope_rot(t, first):
    return jnp.where(first, pltpu.roll(t, 96, 1), pltpu.roll(t, 32, 1))


N_CHUNK = WIDTH // PAIR


def _scr(tm):
    return pltpu.VMEM((N_CHUNK, tm, PAIR), F32)


def _scr_get(scr):
    return jnp.concatenate([scr[j] for j in range(N_CHUNK)], axis=1)


def _scr_put(scr, val):
    for j in range(N_CHUNK):
        scr[j] = val[:, j * PAIR:(j + 1) * PAIR]


def _unstride(view_ref, scr, dil, tm):
    n = tm // dil
    chunks = scr.shape[0]
    for r in range(dil):
        for j in range(chunks):
            col = (r * chunks + j) * PAIR
            scr.at[j][pl.ds(r, n, stride=dil), :] = view_ref[:, col:col + PAIR].astype(F32)


def _restride(scr, out_ref, dil, tm):
    n = tm // dil
    chunks = scr.shape[0]
    for r in range(dil):
        for j in range(chunks):
            col = (r * chunks + j) * PAIR
            rows = scr[j] if dil == 1 else scr.at[j][pl.ds(r, n, stride=dil), :]
            out_ref[:, col:col + PAIR] = rows.astype(out_ref.dtype)


def _view_specs(tm, width=WIDTH):
    return [pl.BlockSpec((tm // dil, dil * width), lambda i: (i, 0)) for _, dil in BRANCHES]


def _view_shapes(T, dtype, width=WIDTH):
    return [jax.ShapeDtypeStruct((T // dil, dil * width), dtype) for _, dil in BRANCHES]


def _norm_proj(x, g, w, b, cos, sin, *, tm=512):
    T = x.shape[0]

    def body(x_ref, g_ref, w_ref, b_ref, cos_ref, sin_ref, h_ref, qa_ref, ka_ref, va_ref, *rest):
        outs_b, ys = rest[:9], rest[9]
        xv = x_ref[...]
        r = lax.rsqrt(jnp.mean(xv * xv, axis=-1, keepdims=True) + EPS)
        h = (xv * r * g_ref[...]).astype(BF16)
        h_ref[...] = h
        cosv = cos_ref[...]
        sinv = sin_ref[...]
        lane = lax.broadcasted_iota(jnp.int32, (tm, PAIR), 1)
        first = (lane % HEAD_DIM) < (HEAD_DIM // 2)

        def proj(off):
            return (lax.dot_general(h, w_ref[off:off + 256, :], NT, preferred_element_type=F32)
                    + b_ref[:, off:off + 256])

        for (off, width, rot, scale), o_ref in zip(((0, 512, True, Q_SCALE), (512, 256, True, 1.0), (768, 256, False, 1.0)),
                                                   (qa_ref, ka_ref, va_ref)):
            for c in range(0, width, 256):
                y = proj(off + c)
                for j in range(0, 256, PAIR):
                    t = y[:, j:j + PAIR]
                    if rot:
                        t = t * cosv + _rope_rot(t, first) * sinv
                    if scale != 1.0:
                        t = t * scale
                    o_ref[:, c + j:c + j + PAIR] = t.astype(BF16)
        for n, (off, scale) in enumerate(((1024, Q_SCALE), (1536, 1.0), (2048, 1.0))):
            for c in range(0, WIDTH, 256):
                y = proj(off + c)
                y = y * scale if scale != 1.0 else y
                for j in range(0, 256, PAIR):
                    ys[(c + j) // PAIR] = y[:, j:j + PAIR]
            for (_, dil), o_ref in zip(BRANCHES, outs_b[3 * n:3 * n + 3]):
                _restride(ys, o_ref, dil, tm)

    row = lambda w_: pl.BlockSpec((tm, w_), lambda i: (i, 0))
    full = lambda a: pl.BlockSpec(a.shape, lambda i: (0, 0))
    return pl.pallas_call(
        body, name="norm_proj", grid=(T // tm,),
        in_specs=[row(D_MODEL), full(g), full(w), full(b), row(PAIR), row(PAIR)],
        out_specs=[row(D_MODEL), row(512), row(256), row(256)] + _view_specs(tm) * 3,
        out_shape=[jax.ShapeDtypeStruct((T, n), BF16) for n in (D_MODEL, 512, 256, 256)] + _view_shapes(T, BF16) * 3,
        scratch_shapes=[_scr(tm)],
        compiler_params=_params(("arbitrary",)),
    )(x, g, w, b, cos, sin)


MAX_SUB = 8
AHEAD = 2


def _attn_specs(kvw, sub):
    q_spec = pl.BlockSpec((sub * BLK, WIDTH), lambda r, i: (i, r))
    kc_spec = pl.BlockSpec((sub * BLK, kvw), lambda r, i: (i, r))
    kp_spec = pl.BlockSpec((BLK, kvw), lambda r, i: (jnp.maximum(sub * i - 1, 0), r))
    b_spec = pl.BlockSpec((2, N_HEADS, BLK, 2 * BLK), lambda r, i: (0, 0, 0, 0))
    return q_spec, kp_spec, kc_spec, b_spec


def _window(prev_ref, cur_ref, j, ksl):
    before = prev_ref[:, ksl] if j == 0 else cur_ref[(j - 1) * BLK:j * BLK, ksl]
    return jnp.concatenate([before, cur_ref[j * BLK:(j + 1) * BLK, ksl]], axis=0)


def _attn_fwd(q, k, v, bias, sinks, *, dil, kv_pairs, use_sink, name):
    L = q.shape[0]
    sub = min(MAX_SUB, L // BLK)
    ns = L // (sub * BLK)
    kvw = kv_pairs * PAIR
    rep = 4 // kv_pairs

    def body(sink_ref, q_ref, kp_ref, kc_ref, vp_ref, vc_ref, b_ref, o_ref, lse_ref):
        lane = lax.broadcasted_iota(jnp.int32, (1, PAIR), 1)
        lo = lane < HEAD_DIM
        first = jnp.where(pl.program_id(1) == 0, 1, 0)
        def scores(j, hp):
            rows = slice(j * BLK, (j + 1) * BLK)
            sl = slice(hp * PAIR, (hp + 1) * PAIR)
            ksl = slice((hp // rep) * PAIR, (hp // rep + 1) * PAIR)
            qp = q_ref[rows, sl]
            kk = _window(kp_ref, kc_ref, j, ksl)
            vv = _window(vp_ref, vc_ref, j, ksl)
            heads = []
            for e in range(2):
                h = 2 * hp + e
                msk = lo if e == 0 else jnp.logical_not(lo)
                qm = jnp.where(msk, qp, jnp.zeros_like(qp))
                s = lax.dot_general(qm, kk, NT, preferred_element_type=F32) + (b_ref[first, h] if j == 0 else b_ref[0, h])
                heads.append((h, msk, s))
            return rows, sl, vv, heads

        def outputs(rows, sl, vv, heads):
            o_pair = None
            lse_pair = None
            for h, msk, s in heads:
                m = jnp.max(s, axis=-1, keepdims=True)
                if use_sink:
                    sk = sink_ref[h]
                    m = jnp.maximum(m, sk)
                p = jnp.exp(s - m)
                l = jnp.sum(p, axis=-1, keepdims=True)
                if use_sink:
                    l = l + jnp.exp(sk - m)
                vm = jnp.where(msk, vv, jnp.zeros_like(vv))
                oe = jnp.dot(p.astype(BF16), vm, preferred_element_type=F32) * (1.0 / l)
                ls = m + jnp.log(l)
                if o_pair is None:
                    o_pair = oe
                    lse_pair = jnp.broadcast_to(ls, (BLK, PAIR))
                else:
                    o_pair = o_pair + oe
                    lse_pair = jnp.where(lo, lse_pair, ls)
            o_ref[rows, sl] = o_pair.astype(BF16)
            lse_ref[rows, sl] = lse_pair

        items = [(j, hp) for j in range(sub) for hp in range(4)]
        queue = [scores(*it) for it in items[:AHEAD]]
        for n in range(len(items)):
            if n + AHEAD < len(items):
                queue.append(scores(*items[n + AHEAD]))
            outputs(*queue.pop(0))

    q_spec, kp_spec, kc_spec, b_spec = _attn_specs(kvw, sub)
    return pl.pallas_call(
        body, name=name, grid=(dil, ns),
        in_specs=[pl.BlockSpec(memory_space=pltpu.SMEM), q_spec, kp_spec, kc_spec, kp_spec, kc_spec, b_spec],
        out_specs=[q_spec, q_spec],
        out_shape=[jax.ShapeDtypeStruct((L, dil * WIDTH), BF16), jax.ShapeDtypeStruct((L, dil * WIDTH), F32)],
        compiler_params=_params(("arbitrary", "arbitrary")),
    )(sinks, q, k, k, v, v, bias)


def _attn_bwd(q, k, v, o, do, lse, bias, sinks, *, dil, kv_pairs, use_sink, name, max_sub=MAX_SUB):
    L = q.shape[0]
    sub = min(max_sub, L // BLK)
    ns = L // (sub * BLK)
    n_steps = dil * ns
    kvw = kv_pairs * PAIR
    rep = 4 // kv_pairs
    last = slice((sub - 1) * BLK, sub * BLK)

    def body(sink_ref, q_ref, kp_ref, kc_ref, vp_ref, vc_ref, o_ref, do_ref, lse_ref, b_ref,
             dq_ref, dk_ref, dv_ref, dsum_ref, dsk_ref, pk_ref, pv_ref):
        t = pl.program_id(0)
        i = t % ns

        @pl.when(t == 0)
        def _():
            dsum_ref[...] = jnp.zeros_like(dsum_ref)
            dsk_ref[...] = jnp.zeros_like(dsk_ref)
            pk_ref[...] = jnp.zeros_like(pk_ref)
            pv_ref[...] = jnp.zeros_like(pv_ref)

        @pl.when(t < n_steps)
        def _():
            lo = lax.broadcasted_iota(jnp.int32, (1, PAIR), 1) < HEAD_DIM
            first = jnp.where(i == 0, 1, 0)
            dks = [[None] * kv_pairs for _ in range(sub)]
            dvs = [[None] * kv_pairs for _ in range(sub)]
            def scores(j, hp):
                rows = slice(j * BLK, (j + 1) * BLK)
                kvp = hp // rep
                sl = slice(hp * PAIR, (hp + 1) * PAIR)
                ksl = slice(kvp * PAIR, (kvp + 1) * PAIR)
                qp = q_ref[rows, sl]
                dop = do_ref[rows, sl]
                prod = dop.astype(F32) * o_ref[rows, sl].astype(F32)
                kk = _window(kp_ref, kc_ref, j, ksl)
                vv = _window(vp_ref, vc_ref, j, ksl)
                heads = []
                for e in range(2):
                    h = 2 * hp + e
                    msk = lo if e == 0 else jnp.logical_not(lo)
                    qm = jnp.where(msk, qp, jnp.zeros_like(qp))
                    dom = jnp.where(msk, dop, jnp.zeros_like(dop))
                    km = jnp.where(msk, kk, jnp.zeros_like(kk))
                    s = (lax.dot_general(qm, kk, NT, preferred_element_type=F32)
                         + (b_ref[first, h] if j == 0 else b_ref[0, h]))
                    dp = lax.dot_general(dom, vv, NT, preferred_element_type=F32)
                    heads.append((h, msk, qm, dom, km, s, dp))
                return j, rows, kvp, sl, prod, heads

            def grads(j, rows, kvp, sl, prod, heads):
                dq_pair = None
                c_pair = None
                qms, doms, dsbs, pbs = [], [], [], []
                for h, msk, qm, dom, km, s, dp in heads:
                    ls = lse_ref[rows, h * HEAD_DIM:h * HEAD_DIM + 1]
                    p = jnp.exp(s - ls)
                    delta = jnp.sum(jnp.where(msk, prod, 0.0), axis=-1, keepdims=True)
                    ds = p * (dp - delta)
                    if use_sink:
                        ce = jnp.exp(sink_ref[h] - ls) * delta
                        c_pair = jnp.broadcast_to(ce, (BLK, PAIR)) if c_pair is None else jnp.where(msk, ce, c_pair)
                    else:
                        dsum_ref[h] += ds
                    dsb = ds.astype(BF16)
                    dqe = jnp.dot(dsb, km, preferred_element_type=F32)
                    dq_pair = dqe if dq_pair is None else dq_pair + dqe
                    qms.append(qm)
                    doms.append(dom)
                    dsbs.append(dsb)
                    pbs.append(p.astype(BF16))
                dke = lax.dot_general(jnp.concatenate(dsbs, axis=0), jnp.concatenate(qms, axis=0), TN,
                                      preferred_element_type=F32)
                dve = lax.dot_general(jnp.concatenate(pbs, axis=0), jnp.concatenate(doms, axis=0), TN,
                                      preferred_element_type=F32)
                dks[j][kvp] = dke if dks[j][kvp] is None else dks[j][kvp] + dke
                dvs[j][kvp] = dve if dvs[j][kvp] is None else dvs[j][kvp] + dve
                dq_ref[rows, sl] = (dq_pair * Q_SCALE).astype(BF16)
                if use_sink:
                    dsk_ref[:, sl] += c_pair

            items = [(j, hp) for j in range(sub) for hp in range(4)]
            ahead = AHEAD + 1 if use_sink else AHEAD
            queue = [scores(*it) for it in items[:ahead]]
            for n in range(len(items)):
                if n + ahead < len(items):
                    queue.append(scores(*items[n + ahead]))
                grads(*queue.pop(0))
            for kvp in range(kv_pairs):
                ksl = slice(kvp * PAIR, (kvp + 1) * PAIR)
                for pend_ref, out_ref, parts in ((pk_ref, dk_ref, [d[kvp] for d in dks]),
                                                 (pv_ref, dv_ref, [d[kvp] for d in dvs])):
                    if sub > 1:
                        out_ref[:(sub - 1) * BLK, ksl] = pend_ref[:(sub - 1) * BLK, ksl].astype(BF16)
                    out_ref[last, ksl] = (pend_ref[last, ksl] + parts[0][:BLK]).astype(BF16)
                    for j in range(sub):
                        own = parts[j][BLK:]
                        pend_ref[j * BLK:(j + 1) * BLK, ksl] = own + parts[j + 1][:BLK] if j + 1 < sub else own

        @pl.when(t == n_steps)
        def _():
            dk_ref[...] = pk_ref[...].astype(BF16)
            dv_ref[...] = pv_ref[...].astype(BF16)

    def at(t):
        t = jnp.minimum(t, n_steps - 1)
        return t % ns, t // ns

    def before(t):
        return at(jnp.maximum(t - 1, 0))

    q_spec = pl.BlockSpec((sub * BLK, WIDTH), at)
    kc_spec = pl.BlockSpec((sub * BLK, kvw), at)
    kp_spec = pl.BlockSpec((BLK, kvw), lambda t: (jnp.maximum(sub * at(t)[0] - 1, 0), at(t)[1]))
    b_spec = pl.BlockSpec((2, N_HEADS, BLK, 2 * BLK), lambda t: (0, 0, 0, 0))
    dkv_spec = pl.BlockSpec((sub * BLK, kvw), before)
    return pl.pallas_call(
        body, name=name, grid=(n_steps + 1,),
        in_specs=[pl.BlockSpec(memory_space=pltpu.SMEM), q_spec, kp_spec, kc_spec, kp_spec, kc_spec,
                  q_spec, q_spec, q_spec, b_spec],
        out_specs=[q_spec, dkv_spec, dkv_spec,
                   pl.BlockSpec((N_HEADS, BLK, 2 * BLK), lambda t: (0, 0, 0)),
                   pl.BlockSpec((BLK, WIDTH), lambda t: (0, 0))],
        out_shape=[jax.ShapeDtypeStruct((L, dil * WIDTH), BF16),
                   jax.ShapeDtypeStruct((L, dil * kvw), BF16),
                   jax.ShapeDtypeStruct((L, dil * kvw), BF16),
                   jax.ShapeDtypeStruct((N_HEADS, BLK, 2 * BLK), F32),
                   jax.ShapeDtypeStruct((BLK, WIDTH), F32)],
        scratch_shapes=[pltpu.VMEM((sub * BLK, kvw), F32), pltpu.VMEM((sub * BLK, kvw), F32)],
        compiler_params=_params(("arbitrary",)),
    )(sinks, q, k, k, v, v, o, do, lse, bias)


def _merge_wo(x, oa, o1, o2, o3, l1, l2, l3, ga, gb, wo, gf, *, tm=512):
    T = x.shape[0]

    def body(x_ref, oa_ref, o1_ref, o2_ref, o3_ref, l1_ref, l2_ref, l3_ref, ga_ref, gb_ref, wo_ref, gf_ref,
             x2_ref, mix_ref, h2_ref, ob1_ref, ob4_ref, ob16_ref, ls1_ref, ls4_ref, ls16_ref, so2, so3, sl2, sl3):
        _unstride(o2_ref, so2, BRANCHES[1][1], tm)
        _unstride(o3_ref, so3, BRANCHES[2][1], tm)
        _unstride(l2_ref, sl2, BRANCHES[1][1], tm)
        _unstride(l3_ref, sl3, BRANCHES[2][1], tm)
        la, lb, lc = l1_ref[...], _scr_get(sl2), _scr_get(sl3)
        m = jnp.maximum(jnp.maximum(la, lb), lc)
        ea, eb, ec = jnp.exp(la - m), jnp.exp(lb - m), jnp.exp(lc - m)
        den = ea + eb + ec
        inv = 1.0 / den
        ob = (ea * o1_ref[...].astype(F32) + eb * _scr_get(so2) + ec * _scr_get(so3)) * inv
        _scr_put(so2, ob)
        _scr_put(sl2, m + jnp.log(den))
        for (_, dil), o_ref, l_ref in zip(BRANCHES, (ob1_ref, ob4_ref, ob16_ref), (ls1_ref, ls4_ref, ls16_ref)):
            _restride(so2, o_ref, dil, tm)
            _restride(sl2, l_ref, dil, tm)
        oav = oa_ref[...].astype(F32)
        ra = lax.rsqrt(jnp.mean(oav * oav, axis=-1, keepdims=True) + EPS)
        rb = lax.rsqrt(jnp.mean(ob * ob, axis=-1, keepdims=True) + EPS)
        mix_ref[:, :WIDTH] = (oav * ra * ga_ref[...]).astype(BF16)
        mix_ref[:, WIDTH:] = (ob * rb * gb_ref[...]).astype(BF16)
        x2 = x_ref[...] + jnp.dot(mix_ref[...], wo_ref[...], preferred_element_type=F32)
        x2_ref[...] = x2
        r2 = lax.rsqrt(jnp.mean(x2 * x2, axis=-1, keepdims=True) + EPS)
        h2_ref[...] = (x2 * r2 * gf_ref[...]).astype(BF16)

    row = lambda w_: pl.BlockSpec((tm, w_), lambda i: (i, 0))
    full = lambda a: pl.BlockSpec(a.shape, lambda i: (0, 0))
    return pl.pallas_call(
        body, name="merge_wo", grid=(T // tm,),
        in_specs=[row(D_MODEL), row(WIDTH)] + _view_specs(tm) * 2 + [full(ga), full(gb), full(wo), full(gf)],
        out_specs=[row(D_MODEL), row(D_MODEL), row(D_MODEL)] + _view_specs(tm) * 2,
        out_shape=[jax.ShapeDtypeStruct((T, D_MODEL), F32), jax.ShapeDtypeStruct((T, D_MODEL), BF16),
                   jax.ShapeDtypeStruct((T, D_MODEL), BF16)] + _view_shapes(T, BF16) + _view_shapes(T, F32),
        scratch_shapes=[_scr(tm)] * 4,
        compiler_params=_params(("arbitrary",)),
    )(x, oa, o1, o2, o3, l1, l2, l3, ga, gb, wo, gf)


def _ffn_up(h2, wgt, wut, *, tm=1024, fc=D_FF, rc=512, cc=256):
    T = h2.shape[0]

    def body(h_ref, wg_ref, wu_ref, gate_ref, up_ref, act_ref):
        for s in range(0, tm, rc):
            h = h_ref[s:s + rc, :]
            for c in range(0, fc, cc):
                gt = lax.dot_general(h, wg_ref[c:c + cc, :], NT, preferred_element_type=F32)
                u = lax.dot_general(h, wu_ref[c:c + cc, :], NT, preferred_element_type=F32)
                gate_ref[s:s + rc, c:c + cc] = gt.astype(BF16)
                up_ref[s:s + rc, c:c + cc] = u.astype(BF16)
                act_ref[s:s + rc, c:c + cc] = (gt * _sigmoid(gt) * u).astype(BF16)

    rowd = pl.BlockSpec((tm, D_MODEL), lambda i, c: (i, 0))
    wrow = pl.BlockSpec((fc, D_MODEL), lambda i, c: (c, 0), pipeline_mode=pl.Buffered(1))
    oc = pl.BlockSpec((tm, fc), lambda i, c: (i, c))
    return pl.pallas_call(
        body, name="ffn_up", grid=(T // tm, D_FF // fc),
        in_specs=[rowd, wrow, wrow],
        out_specs=[oc, oc, oc],
        out_shape=[jax.ShapeDtypeStruct((T, D_FF), BF16)] * 3,
        compiler_params=_params(("arbitrary", "arbitrary")),
    )(h2, wgt, wut)


def _ffn_down_loss(act, wd, x2, tgt, g, *, tm=1024, rc=256):
    T = x2.shape[0]

    def body(act_ref, wd_ref, x2_ref, tgt_ref, g_ref, dx_ref, dxb_ref, loss_ref, dg_ref):
        @pl.when(pl.program_id(0) == 0)
        def _():
            loss_ref[...] = jnp.zeros_like(loss_ref)
            dg_ref[...] = jnp.zeros_like(dg_ref)

        gv = g_ref[...]
        lsum = jnp.zeros((1, 1), F32)
        dgs = jnp.zeros((1, D_MODEL), F32)
        for c in range(0, tm, rc):
            x3 = x2_ref[c:c + rc, :] + jnp.dot(act_ref[c:c + rc, :], wd_ref[...], preferred_element_type=F32)
            r = lax.rsqrt(jnp.mean(x3 * x3, axis=-1, keepdims=True) + EPS)
            xh = x3 * r
            diff = xh * gv - tgt_ref[c:c + rc, :]
            lsum = lsum + jnp.sum(jnp.sum(diff * diff, axis=-1, keepdims=True), axis=0, keepdims=True)
            dy = diff * (1.0 / D_MODEL)
            dgs = dgs + jnp.sum(dy * xh, axis=0, keepdims=True)
            dx = _rms_bwd(dy, xh, r, gv)
            dx_ref[c:c + rc, :] = dx
            dxb_ref[c:c + rc, :] = dx.astype(BF16)
        loss_ref[...] += lsum * (0.5 / D_MODEL)
        dg_ref[...] += dgs

    rowd = pl.BlockSpec((tm, D_MODEL), lambda i: (i, 0))
    return pl.pallas_call(
        body, name="ffn_down_loss", grid=(T // tm,),
        in_specs=[pl.BlockSpec((tm, D_FF), lambda i: (i, 0)),
                  pl.BlockSpec((D_FF, D_MODEL), lambda i: (0, 0), pipeline_mode=pl.Buffered(1)),
                  rowd, rowd, pl.BlockSpec(g.shape, lambda i: (0, 0))],
        out_specs=[rowd, rowd, pl.BlockSpec((1, 1), lambda i: (0, 0)), pl.BlockSpec((1, D_MODEL), lambda i: (0, 0))],
        out_shape=[jax.ShapeDtypeStruct((T, D_MODEL), F32), jax.ShapeDtypeStruct((T, D_MODEL), BF16),
                   jax.ShapeDtypeStruct((1, 1), F32), jax.ShapeDtypeStruct((1, D_MODEL), F32)],
        compiler_params=_params(("arbitrary",)),
    )(act, wd, x2, tgt, g)


def _ffn_bwd(dx3, gate, up, wd, wgt, wut, x2, g, *, tm=256, cc=256):
    T = x2.shape[0]

    def body(dx_ref, gate_ref, up_ref, wd_ref, wg_ref, wu_ref, x2_ref, g_ref,
             dgate_ref, dup_ref, dx2_ref, dx2b_ref, dg_ref):
        @pl.when(pl.program_id(0) == 0)
        def _():
            dg_ref[...] = jnp.zeros_like(dg_ref)

        dxb = dx_ref[...].astype(BF16)
        for c in range(0, D_FF, cc):
            dact = lax.dot_general(dxb, wd_ref[c:c + cc, :], NT, preferred_element_type=F32)
            gt = gate_ref[:, c:c + cc].astype(F32)
            u = up_ref[:, c:c + cc].astype(F32)
            sg = _sigmoid(gt)
            a = dact * sg
            dgate_ref[:, c:c + cc] = (a * u * ((1.0 + gt) - gt * sg)).astype(BF16)
            dup_ref[:, c:c + cc] = (a * gt).astype(BF16)
        dh = (jnp.dot(dgate_ref[...], wg_ref[...], preferred_element_type=F32)
              + jnp.dot(dup_ref[...], wu_ref[...], preferred_element_type=F32))
        xv = x2_ref[...]
        r = lax.rsqrt(jnp.mean(xv * xv, axis=-1, keepdims=True) + EPS)
        xh = xv * r
        dg_ref[...] += jnp.sum(dh * xh, axis=0, keepdims=True)
        d = dx_ref[...] + _rms_bwd(dh, xh, r, g_ref[...])
        dx2_ref[...] = d
        dx2b_ref[...] = d.astype(BF16)

    rowd = pl.BlockSpec((tm, D_MODEL), lambda i: (i, 0))
    rowf = pl.BlockSpec((tm, D_FF), lambda i: (i, 0))
    wfull = pl.BlockSpec((D_FF, D_MODEL), lambda i: (0, 0), pipeline_mode=pl.Buffered(1))
    return pl.pallas_call(
        body, name="ffn_bwd", grid=(T // tm,),
        in_specs=[rowd, rowf, rowf, wfull, wfull, wfull, rowd, pl.BlockSpec(g.shape, lambda i: (0, 0))],
        out_specs=[rowf, rowf, rowd, rowd, pl.BlockSpec((1, D_MODEL), lambda i: (0, 0))],
        out_shape=[jax.ShapeDtypeStruct((T, D_FF), BF16), jax.ShapeDtypeStruct((T, D_FF), BF16),
                   jax.ShapeDtypeStruct((T, D_MODEL), F32), jax.ShapeDtypeStruct((T, D_MODEL), BF16),
                   jax.ShapeDtypeStruct((1, D_MODEL), F32)],
        compiler_params=_params(("arbitrary",)),
    )(dx3, gate, up, wd, wgt, wut, x2, g)


def _matmul_tn(a, b, *, tk, tn, tt=2048, out_dtype=BF16, name):
    T, K = a.shape
    N = b.shape[1]
    nt = T // tt

    def body(a_ref, b_ref, o_ref, acc_ref):
        part = lax.dot_general(a_ref[...], b_ref[...], TN, preferred_element_type=F32)

        @pl.when(pl.program_id(2) == 0)
        def _():
            acc_ref[...] = part

        @pl.when(pl.program_id(2) > 0)
        def _():
            acc_ref[...] += part

        @pl.when(pl.program_id(2) == nt - 1)
        def _():
            o_ref[...] = acc_ref[...].astype(out_dtype)

    return pl.pallas_call(
        body, name=name, grid=(K // tk, N // tn, nt),
        in_specs=[pl.BlockSpec((tt, tk), lambda i, j, t: (t, i)), pl.BlockSpec((tt, tn), lambda i, j, t: (t, j))],
        out_specs=pl.BlockSpec((tk, tn), lambda i, j, t: (i, j)),
        out_shape=jax.ShapeDtypeStruct((K, N), out_dtype),
        scratch_shapes=[pltpu.VMEM((tk, tn), F32)],
        compiler_params=_params(("arbitrary", "arbitrary", "arbitrary")),
    )(a, b)


def _wo_bwd(dx2b, wo, oa, ob, ga, gb, mixed, *, tm=1024, wc=256):
    T = dx2b.shape[0]
    n_tiles = T // tm

    def body(dx_ref, wo_ref, oa_ref, ob_ref, ga_ref, gb_ref, mix_ref,
             doa_ref, dob1_ref, dob4_ref, dob16_ref, dga_ref, dgb_ref, dwo_ref, scr, dw_acc):
        @pl.when(pl.program_id(0) == 0)
        def _():
            dga_ref[...] = jnp.zeros_like(dga_ref)
            dgb_ref[...] = jnp.zeros_like(dgb_ref)
            dw_acc[...] = jnp.zeros_like(dw_acc)

        dxv = dx_ref[...]
        for c in range(0, D_MODEL, wc):
            dw_acc[c:c + wc, :] += lax.dot_general(mix_ref[:, c:c + wc], dxv, TN, preferred_element_type=F32)

        @pl.when(pl.program_id(0) == n_tiles - 1)
        def _():
            dwo_ref[...] = dw_acc[...].astype(BF16)

        dm = lax.dot_general(dxv, wo_ref[...], NT, preferred_element_type=F32)
        for o_ref, g_ref, dg_ref, sl in ((oa_ref, ga_ref, dga_ref, slice(0, WIDTH)),
                                         (ob_ref, gb_ref, dgb_ref, slice(WIDTH, 2 * WIDTH))):
            ov = o_ref[...].astype(F32)
            r = lax.rsqrt(jnp.mean(ov * ov, axis=-1, keepdims=True) + EPS)
            xh = ov * r
            d = dm[:, sl]
            dg_ref[...] += jnp.sum(d * xh, axis=0, keepdims=True)
            do = _rms_bwd(d, xh, r, g_ref[...])
            if o_ref is oa_ref:
                doa_ref[...] = do.astype(BF16)
            else:
                _scr_put(scr, do)
                for (_, dil), v_ref in zip(BRANCHES, (dob1_ref, dob4_ref, dob16_ref)):
                    _restride(scr, v_ref, dil, tm)

    row = lambda w_: pl.BlockSpec((tm, w_), lambda i: (i, 0))
    full = lambda a: pl.BlockSpec(a.shape, lambda i: (0, 0))
    return pl.pallas_call(
        body, name="wo_bwd", grid=(T // tm,),
        in_specs=[row(D_MODEL), full(wo), row(WIDTH), row(WIDTH), full(ga), full(gb), row(D_MODEL)],
        out_specs=[row(WIDTH)] + _view_specs(tm)
        + [pl.BlockSpec((1, WIDTH), lambda i: (0, 0)), pl.BlockSpec((1, WIDTH), lambda i: (0, 0)), full(wo)],
        out_shape=[jax.ShapeDtypeStruct((T, WIDTH), BF16)] + _view_shapes(T, BF16)
        + [jax.ShapeDtypeStruct((1, WIDTH), F32), jax.ShapeDtypeStruct((1, WIDTH), F32),
           jax.ShapeDtypeStruct((D_MODEL, D_MODEL), BF16)],
        scratch_shapes=[_scr(tm), pltpu.VMEM((D_MODEL, D_MODEL), F32)],
        compiler_params=_params(("arbitrary",)),
    )(dx2b, wo, oa, ob, ga, gb, mixed)


def _inproj_bwd(dqa, dka, dva, dqs, dks, dvs, cos, sin, w, x, dx2, g, h1, *, tm=512, wc=256):
    T = dqa.shape[0]
    n_tiles = T // tm

    def body(dqa_ref, dka_ref, dva_ref, q1, q2, q3, k1, k2, k3, v1, v2, v3, cos_ref, sin_ref, w_ref, x_ref, dx2_ref,
             g_ref, h1_ref, db_ref, gx_ref, dg_ref, dw_ref, dp_ref, dw_acc, acc, tmp):
        @pl.when(pl.program_id(0) == 0)
        def _():
            db_ref[...] = jnp.zeros_like(db_ref)
            dg_ref[...] = jnp.zeros_like(dg_ref)
            dw_acc[...] = jnp.zeros_like(dw_acc)

        cosv = cos_ref[...]
        sinv = sin_ref[...]
        lane = lax.broadcasted_iota(jnp.int32, (tm, PAIR), 1)
        first = (lane % HEAD_DIM) < (HEAD_DIM // 2)

        def put(off, val):
            dp_ref[:, off:off + PAIR] = val.astype(BF16)
            db_ref[:, off:off + PAIR] += jnp.sum(val, axis=0, keepdims=True)

        for src, off, width in ((dqa_ref, 0, 512), (dka_ref, 512, 256)):
            for j in range(0, width, PAIR):
                d = src[:, j:j + PAIR].astype(F32)
                put(off + j, d * cosv - _rope_rot(d, first) * sinv)
        for j in range(0, 256, PAIR):
            put(768 + j, dva_ref[:, j:j + PAIR].astype(F32))
        for (a, b, c), off in (((q1, q2, q3), 1024), ((k1, k2, k3), 1536), ((v1, v2, v3), 2048)):
            _unstride(b, acc, BRANCHES[1][1], tm)
            _unstride(c, tmp, BRANCHES[2][1], tm)
            for j in range(N_CHUNK):
                put(off + j * PAIR, a[:, j * PAIR:(j + 1) * PAIR].astype(F32) + acc[j] + tmp[j])

        dh = jnp.dot(dp_ref[...], w_ref[...], preferred_element_type=F32)
        xv = x_ref[...]
        r = lax.rsqrt(jnp.mean(xv * xv, axis=-1, keepdims=True) + EPS)
        xh = xv * r
        dg_ref[...] += jnp.sum(dh * xh, axis=0, keepdims=True)
        gx_ref[...] = dx2_ref[...] + _rms_bwd(dh, xh, r, g_ref[...])

        h1v = h1_ref[...]
        for c in range(0, D_INP, wc):
            dw_acc[c:c + wc, :] += lax.dot_general(dp_ref[:, c:c + wc], h1v, TN, preferred_element_type=F32)

        @pl.when(pl.program_id(0) == n_tiles - 1)
        def _():
            pltpu.sync_copy(dw_acc, dw_ref)

    row = lambda w_: pl.BlockSpec((tm, w_), lambda i: (i, 0))
    full = lambda a: pl.BlockSpec(a.shape, lambda i: (0, 0))
    return pl.pallas_call(
        body, name="inproj_bwd", grid=(n_tiles,),
        in_specs=[row(512), row(256), row(256)] + _view_specs(tm) * 3 + [row(PAIR), row(PAIR)]
        + [pl.BlockSpec(w.shape, lambda i: (0, 0), pipeline_mode=pl.Buffered(1)), row(D_MODEL), row(D_MODEL), full(g),
           row(D_MODEL)],
        out_specs=[pl.BlockSpec((1, D_INP), lambda i: (0, 0)), row(D_MODEL),
                   pl.BlockSpec((1, D_MODEL), lambda i: (0, 0)), pl.BlockSpec(memory_space=pl.ANY)],
        out_shape=[jax.ShapeDtypeStruct((1, D_INP), F32), jax.ShapeDtypeStruct((T, D_MODEL), F32),
                   jax.ShapeDtypeStruct((1, D_MODEL), F32), jax.ShapeDtypeStruct((D_INP, D_MODEL), F32)],
        scratch_shapes=[pltpu.VMEM((tm, D_INP), BF16), pltpu.VMEM((D_INP, D_MODEL), F32), _scr(tm), _scr(tm)],
        compiler_params=_params(("arbitrary",)),
    )(dqa, dka, dva, *dqs, *dks, *dvs, cos, sin, w, x, dx2, g, h1)


def _bias_sink_grads(dsums, bmaps, dsk):
    def body(s1, s2, s3, m1, m2, m3, dsk_ref, drel_ref, dsink_ref):
        row = lax.broadcasted_iota(jnp.int32, (N_HEADS, 128), 0)
        lane = lax.broadcasted_iota(jnp.int32, (N_HEADS, 128), 1)
        out = jnp.zeros((N_HEADS, 128), F32)
        for s_ref, m_ref in ((s1, m1), (s2, m2), (s3, m3)):
            bm = m_ref[...]
            for h in range(N_HEADS):
                a = s_ref[h]
                for b in range(REL_BUCKETS):
                    v = jnp.sum(jnp.sum(jnp.where(bm == b, a, 0.0), axis=-1, keepdims=True), axis=0, keepdims=True)
                    out = out + jnp.where((row == h) & (lane == b), v, 0.0)
        drel_ref[...] = out
        dsink_ref[...] = -jnp.sum(dsk_ref[...], axis=0, keepdims=True)

    vm = pl.BlockSpec(memory_space=pltpu.VMEM)
    return pl.pallas_call(
        body, name="bias_sink_grads",
        in_specs=[vm] * 7, out_specs=[vm, vm],
        out_shape=[jax.ShapeDtypeStruct((N_HEADS, 128), F32), jax.ShapeDtypeStruct((1, WIDTH), F32)],
        compiler_params=_params(),
    )(*dsums, *bmaps, dsk)


def _all_gather(blk, *, name):
    R, C = blk.shape

    def body(x_ref, out_ref, send_sems, recv_sems, local_sem):
        x, y, c = lax.axis_index("x"), lax.axis_index("y"), lax.axis_index("c")
        me, sibling = (x, y, c), (x, y, 1 - c)
        chips = [(1 - x, y), (x, 1 - y), (1 - x, 1 - y)]

        def slot(px, py, pc):
            return out_ref.at[4 * px + 2 * py + pc]

        def copy(k, block, to, src=None):
            return pltpu.make_async_remote_copy(
                src_ref=slot(*block) if src is None else src, dst_ref=slot(*block),
                send_sem=send_sems.at[k], recv_sem=recv_sems.at[k], device_id=to, device_id_type=MESH)

        mine = pltpu.make_async_copy(x_ref, slot(*me), local_sem)
        mine.start()
        first = [copy(0, me, sibling, src=x_ref)]
        first += [copy(1 + j, me, (*chip, c), src=x_ref) for j, chip in enumerate(chips)]
        for cp in first:
            cp.start()
        passed = [copy(4 + j, (*chip, c), sibling) for j, chip in enumerate(chips)]
        for j, chip in enumerate(chips):
            copy(1 + j, (*chip, c), me).wait_recv()
            passed[j].start()
        copy(0, sibling, me).wait_recv()
        for j, chip in enumerate(chips):
            copy(4 + j, (*chip, 1 - c), me).wait_recv()
        for cp in first + passed:
            cp.wait_send()
        mine.wait()

    return pl.pallas_call(
        body, name=name,
        in_specs=[pl.BlockSpec(memory_space=pl.ANY)], out_specs=pl.BlockSpec(memory_space=pl.ANY),
        out_shape=jax.ShapeDtypeStruct((N_DEV, R, C), blk.dtype),
        scratch_shapes=[pltpu.SemaphoreType.DMA((7,)), pltpu.SemaphoreType.DMA((7,)), pltpu.SemaphoreType.DMA],
        compiler_params=pltpu.CompilerParams(has_side_effects=True),
    )(blk)


def _peers(x, y, c):
    return [(x ^ (k >> 2), y ^ ((k >> 1) & 1), c ^ (k & 1)) for k in range(1, N_DEV)]


_HBM = pl.BlockSpec(memory_space=pltpu.HBM)
_SEM = pl.BlockSpec(memory_space=pltpu.SEMAPHORE)
_EFFECT = pltpu.SideEffectType.DATAFLOW_SIDE_EFFECTING


def _peer_list(x, y, c, near):
    if near:
        return [(x, y, 1 - c), (1 - x, y, c), (x, 1 - y, c), (1 - x, 1 - y, c)]
    return _peers(x, y, c)


def _exchange_start(srcs, *, gather, name, near=False):
    n = len(srcs)
    n_peers = 4 if near else N_DEV - 1
    lands = [lax.empty((N_DEV,) + s.shape[-2:], s.dtype) for s in srcs]

    def body(*refs):
        src_refs, land_refs = refs[:n], refs[n:2 * n]
        send_sems, recv_sems = refs[2 * n], refs[2 * n + 1]
        token = refs[-1]
        x, y, c = lax.axis_index("x"), lax.axis_index("y"), lax.axis_index("c")
        mine = 4 * x + 2 * y + c
        for a in range(n):
            for k, peer in enumerate(_peer_list(x, y, c, near)):
                dest = 4 * peer[0] + 2 * peer[1] + peer[2]
                j = a * n_peers + k
                pltpu.make_async_remote_copy(
                    src_ref=src_refs[a] if gather else src_refs[a].at[dest], dst_ref=land_refs[a].at[mine],
                    send_sem=send_sems.at[j], recv_sem=recv_sems.at[j], device_id=peer, device_id_type=MESH).start()
        token[...] = jnp.zeros_like(token)

    sems = pltpu.SemaphoreType.DMA((n * n_peers,))
    out = pl.pallas_call(
        body, name=name,
        out_shape=(sems, sems) + tuple(pltpu.HBM(a.shape, a.dtype) for a in list(srcs) + lands)
        + (jax.ShapeDtypeStruct((8, 128), F32),),
        in_specs=(_HBM,) * (2 * n), out_specs=(_SEM, _SEM) + (_HBM,) * (2 * n) + (pl.BlockSpec(memory_space=pltpu.VMEM),),
        input_output_aliases={i: 2 + i for i in range(2 * n)},
        compiler_params=pltpu.CompilerParams(has_side_effects=_EFFECT),
    )(*[pltpu.with_memory_space_constraint(a, pltpu.HBM) for a in list(srcs) + lands])
    return out[:-1], out[-1]


def _exchange_wait(state, after, *, gather, name, near=False):
    send_sems, recv_sems = state[0], state[1]
    n = (len(state) - 2) // 2
    n_peers = 4 if near else N_DEV - 1
    arrays = state[2:]

    def body(*refs):
        src_refs, land_refs = refs[:n], refs[n:2 * n]
        send_sems, recv_sems = refs[2 * n], refs[2 * n + 1]
        x, y, c = lax.axis_index("x"), lax.axis_index("y"), lax.axis_index("c")
        for a in range(n):
            for k, peer in enumerate(_peer_list(x, y, c, near)):
                other = 4 * peer[0] + 2 * peer[1] + peer[2]
                j = a * n_peers + k
                copy = pltpu.make_async_remote_copy(
                    src_ref=src_refs[a] if gather else src_refs[a].at[other], dst_ref=land_refs[a].at[other],
                    send_sem=send_sems.at[j], recv_sem=recv_sems.at[j], device_id=peer, device_id_type=MESH)
                copy.wait_send()
                copy.wait_recv()

    out = pl.pallas_call(
        body, name=name,
        out_shape=tuple(pltpu.HBM(a.shape, a.dtype) for a in arrays),
        in_specs=(_HBM,) * (2 * n) + (_SEM, _SEM, pl.BlockSpec(memory_space=pl.ANY)), out_specs=(_HBM,) * (2 * n),
        input_output_aliases={i: i for i in range(2 * n)},
        compiler_params=pltpu.CompilerParams(has_side_effects=_EFFECT),
    )(*arrays, send_sems, recv_sems, after)
    mine = 4 * lax.axis_index("x") + 2 * lax.axis_index("y") + lax.axis_index("c")
    own = out[:n] if gather else [lax.dynamic_index_in_dim(s, mine, 0, keepdims=False) for s in out[:n]]
    return [lax.dynamic_update_slice(g, o[None], (mine, 0, 0)) for g, o in zip(out[n:], own)]


def _forward_start(lands, *, name):
    n = len(lands)

    def body(*refs):
        land_refs, send_sems, recv_sems, token = refs[:n], refs[n], refs[n + 1], refs[-1]
        x, y, c = lax.axis_index("x"), lax.axis_index("y"), lax.axis_index("c")
        for a in range(n):
            for j, (px, py) in enumerate(((1 - x, y), (x, 1 - y), (1 - x, 1 - y))):
                blk = 4 * px + 2 * py + c
                pltpu.make_async_remote_copy(
                    src_ref=land_refs[a].at[blk], dst_ref=land_refs[a].at[blk], send_sem=send_sems.at[3 * a + j],
                    recv_sem=recv_sems.at[3 * a + j], device_id=(x, y, 1 - c), device_id_type=MESH).start()
        token[...] = jnp.zeros_like(token)

    sems = pltpu.SemaphoreType.DMA((3 * n,))
    out = pl.pallas_call(
        body, name=name,
        out_shape=(sems, sems) + tuple(pltpu.HBM(a.shape, a.dtype) for a in lands) + (jax.ShapeDtypeStruct((8, 128), F32),),
        in_specs=(_HBM,) * n, out_specs=(_SEM, _SEM) + (_HBM,) * n + (pl.BlockSpec(memory_space=pltpu.VMEM),),
        input_output_aliases={i: 2 + i for i in range(n)},
        compiler_params=pltpu.CompilerParams(has_side_effects=_EFFECT),
    )(*[pltpu.with_memory_space_constraint(a, pltpu.HBM) for a in lands])
    return out[:-1], out[-1]


def _forward_wait(state, after, *, name):
    send_sems, recv_sems = state[0], state[1]
    lands = state[2:]
    n = len(lands)

    def body(*refs):
        land_refs, send_sems, recv_sems = refs[:n], refs[n], refs[n + 1]
        x, y, c = lax.axis_index("x"), lax.axis_index("y"), lax.axis_index("c")
        for a in range(n):
            for j, (px, py) in enumerate(((1 - x, y), (x, 1 - y), (1 - x, 1 - y))):
                copy = pltpu.make_async_remote_copy(
                    src_ref=land_refs[a].at[4 * px + 2 * py + c], dst_ref=land_refs[a].at[4 * px + 2 * py + 1 - c],
                    send_sem=send_sems.at[3 * a + j], recv_sem=recv_sems.at[3 * a + j], device_id=(x, y, 1 - c),
                    device_id_type=MESH)
                copy.wait_send()
                copy.wait_recv()

    return pl.pallas_call(
        body, name=name,
        out_shape=tuple(pltpu.HBM(a.shape, a.dtype) for a in lands),
        in_specs=(_HBM,) * n + (_SEM, _SEM, pl.BlockSpec(memory_space=pl.ANY)), out_specs=(_HBM,) * n,
        input_output_aliases={i: i for i in range(n)},
        compiler_params=pltpu.CompilerParams(has_side_effects=_EFFECT),
    )(*lands, send_sems, recv_sems, after)


def _adam_math(w, g, m, v):
    m = ADAM_B1 * m + (1.0 - ADAM_B1) * g
    v = ADAM_B2 * v + (1.0 - ADAM_B2) * (g * g)
    m_hat = m / (1.0 - ADAM_B1 ** ADAM_STEP)
    v_hat = v / (1.0 - ADAM_B2 ** ADAM_STEP)
    delta = -ADAM_LR * (m_hat / (jnp.sqrt(v_hat) + ADAM_EPS) + ADAM_WD * w)
    return delta, m, v


def _adamw(parts, w, m, v, *, name):
    R, C = w.shape
    n_parts = parts.shape[0]
    tr = R // 2
    assert tr % 16 == 0

    def body(p_ref, w_ref, m_ref, v_ref, g_ref, d_ref, nm_ref, nv_ref):
        g = p_ref[0].astype(F32)
        for s in range(1, n_parts):
            g = g + p_ref[s].astype(F32)
        d, nm, nv = _adam_math(w_ref[...], g, m_ref[...], v_ref[...])
        g_ref[...] = g
        d_ref[...] = d
        nm_ref[...] = nm
        nv_ref[...] = nv

    blk = pl.BlockSpec((tr, C), lambda i: (i, 0))
    return pl.pallas_call(
        body, name=name, grid=(R // tr,),
        in_specs=[pl.BlockSpec((n_parts, tr, C), lambda i: (0, i, 0)), blk, blk, blk],
        out_specs=[blk] * 4, out_shape=[jax.ShapeDtypeStruct((R, C), F32)] * 4,
        compiler_params=_params(("arbitrary",)),
    )(parts, w, m, v)


def _adamw_small(parts, w, m, v):
    def body(p_ref, w_ref, m_ref, v_ref, g_ref, d_ref, nm_ref, nv_ref):
        g = p_ref[0]
        for s in range(1, N_DEV):
            g = g + p_ref[s]
        d, nm, nv = _adam_math(w_ref[...], g, m_ref[...], v_ref[...])
        g_ref[...] = g
        d_ref[...] = d
        nm_ref[...] = nm
        nv_ref[...] = nv

    vm = pl.BlockSpec(memory_space=pltpu.VMEM)
    return pl.pallas_call(
        body, name="adamw_small", in_specs=[vm] * 4, out_specs=[vm] * 4,
        out_shape=[jax.ShapeDtypeStruct((SMALL_ROWS, 128), F32)] * 4, compiler_params=_params(),
    )(parts, w, m, v)


def _t5_bucket(dist):
    max_exact = REL_BUCKETS // 2
    df = jnp.maximum(dist, 1).astype(F32)
    large = max_exact + (jnp.log(df / max_exact) / math.log(REL_MAX_DISTANCE / max_exact)
                         * (REL_BUCKETS - max_exact)).astype(jnp.int32)
    large = jnp.minimum(large, REL_BUCKETS - 1)
    return jnp.where(dist < max_exact, dist, large)


def _band_tables(rel_table, dil, n_back):
    qi = jnp.arange(BLK)[:, None]
    kj = jnp.arange(2 * BLK)[None, :]
    delta = BLK + qi - kj
    in_band = (delta >= 0) & (delta <= n_back)
    if rel_table is None:
        vals = jnp.zeros((N_HEADS, BLK, 2 * BLK), F32)
        bmap = None
    else:
        bucket = _t5_bucket(jnp.clip(delta, 0, n_back) * dil)
        vals = jnp.zeros((N_HEADS, BLK, 2 * BLK), F32)
        for b in range(REL_BUCKETS):
            vals = jnp.where((bucket == b)[None], rel_table[b][:, None, None], vals)
        bmap = jnp.where(in_band, bucket, -1).astype(jnp.int32)
    later = jnp.where(in_band[None], vals, NEG)
    first = jnp.where((in_band & (kj >= BLK))[None], vals, NEG)
    return jnp.stack([later, first]), bmap


def _rope_tables(T):
    half = HEAD_DIM // 2
    inv_freq = ROPE_THETA ** (-jnp.arange(half, dtype=F32) / half)
    ang = jnp.arange(T, dtype=F32)[:, None] * inv_freq[None, :]
    cos, sin = jnp.cos(ang), jnp.sin(ang)
    return jnp.tile(cos, (1, 4)), jnp.tile(jnp.concatenate([-sin, sin], axis=1), (1, 2))


def _widen_in(a, axis):
    sl = lambda lo, hi: lax.slice_in_dim(a, lo, hi, axis=axis)
    dup = lambda lo: [sl(lo, lo + 64), sl(lo, lo + 64), sl(lo + 64, lo + 128), sl(lo + 64, lo + 128)]
    return jnp.concatenate([sl(0, 512)] + dup(512) + dup(640) + [sl(768, D_IN)], axis=axis)


def _fold_in(a, axis):
    sl = lambda lo, hi: lax.slice_in_dim(a, lo, hi, axis=axis)
    fold = lambda lo: [sl(lo, lo + 64) + sl(lo + 64, lo + 128), sl(lo + 128, lo + 192) + sl(lo + 192, lo + 256)]
    return jnp.concatenate([sl(0, 512)] + fold(512) + fold(768) + [sl(1024, D_INP)], axis=axis)


def _local_step(x, tgt, g_attn, b_in, sinks, rel_table, g_out_a, g_out_b, g_ffn, g_final,
                win_fn, wo_fn, ffn_fn, early_fn):
    T = x.shape[0]
    cos, sin = _rope_tables(T)
    g_final2 = g_final.reshape(1, D_MODEL)
    sink8 = sinks.reshape(N_HEADS)

    bias_a, _ = _band_tables(None, 1, BLK - 1)
    tabs = [_band_tables(rel_table, dil, window // dil) for window, dil in BRANCHES]
    wint, token = win_fn(tabs[2][0])
    winp = _widen_in(wint, 0)
    binp = _widen_in(b_in, 1) + token[0, 0]

    h1, qa, ka, va, *qkv_b = _norm_proj(x, g_attn, winp, binp, cos, sin)
    qbs, kbs, vbs = qkv_b[0:3], qkv_b[3:6], qkv_b[6:9]
    oa, lse_a = _attn_fwd(qa, ka, va, bias_a, sink8, dil=1, kv_pairs=2, use_sink=True, name="attn_a_fwd")
    outs = [_attn_fwd(qbs[n], kbs[n], vbs[n], tabs[n][0], sink8, dil=dil, kv_pairs=4, use_sink=False,
                      name=f"attn_b{n}_fwd") for n, (_, dil) in enumerate(BRANCHES)]
    wo = wo_fn(outs[2][1])
    x2, mixed, h2, *ob_lse = _merge_wo(x, oa, outs[0][0], outs[1][0], outs[2][0], outs[0][1], outs[1][1], outs[2][1],
                                       g_out_a, g_out_b, wo, g_ffn)
    obs, lses = ob_lse[0:3], ob_lse[3:6]
    wgt, wut, wd = ffn_fn(h2)
    gate, up, act = _ffn_up(h2, wgt, wut)
    dx3, dx3b, loss, dg_final = _ffn_down_loss(act, wd, x2, tgt, g_final2)

    dgate, dup, dx2, dx2b, dg_ffn = _ffn_bwd(dx3, gate, up, wd, wgt, wut, x2, g_ffn)
    dwd = _matmul_tn(act, dx3b, tk=1408, tn=1024, name="dw_down")
    dwgt = _matmul_tn(dgate, h2, tk=1408, tn=1024, name="dw_gate")
    dwut = _matmul_tn(dup, h2, tk=1408, tn=1024, name="dw_up")
    doa, *dobs, dg_out_a, dg_out_b, dwo = _wo_bwd(dx2b, wo, oa, obs[0], g_out_a, g_out_b, mixed)
    early, token2 = early_fn(dict(w_o=dwo, w_gate=dwgt, w_up=dwut, w_down=dwd))
    sink8b = sink8 + token2[0, 0]

    dqa, dka, dva, _, dsk = _attn_bwd(qa, ka, va, oa, doa, lse_a, bias_a, sink8b, dil=1, kv_pairs=2, use_sink=True,
                                      name="attn_a_bwd", max_sub=4)
    res = [_attn_bwd(qbs[n], kbs[n], vbs[n], obs[n], dobs[n], lses[n], tabs[n][0], sink8b, dil=dil, kv_pairs=4,
                     use_sink=False, name=f"attn_b{n}_bwd") for n, (_, dil) in enumerate(BRANCHES)]
    dbp, grad_x, dg_attn, dwinp = _inproj_bwd(dqa, dka, dva, [r[0] for r in res], [r[1] for r in res],
                                              [r[2] for r in res], cos, sin, winp, x, dx2, g_attn, h1)
    dwin = _fold_in(dwinp, 0)
    drel, dsink = _bias_sink_grads([r[3] for r in res], [t[1] for t in tabs], dsk)

    small = dict(
        g_attn=dg_attn, b_in=_fold_in(dbp, 1), sinks=dsink[:, ::HEAD_DIM], rel_table=drel[:, :REL_BUCKETS].T,
        g_out_a=dg_out_a, g_out_b=dg_out_b, g_ffn=dg_ffn, g_final=dg_final.reshape(D_MODEL))
    return loss[0, 0], grad_x, dwin, early, small


SMALL_NAMES = ("g_attn", "b_in", "sinks", "rel_table", "g_out_a", "g_out_b", "g_ffn", "g_final", "loss")


def _pack_small(vals):
    flat = jnp.concatenate([vals[n].reshape(-1).astype(F32) for n in SMALL_NAMES])
    return jnp.pad(flat, (0, SMALL_ROWS * 128 - flat.shape[0])).reshape(SMALL_ROWS, 128)


def _unpack_small(packed, like):
    flat = packed.reshape(-1)
    out, off = {}, 0
    for n in SMALL_NAMES:
        size = like[n].size
        out[n] = flat[off:off + size].reshape(like[n].shape)
        off += size
    return out


def kernel(x, g_attn, w_in, b_in, sinks, rel_table, g_out_a, g_out_b, w_o, g_ffn, w_gate, w_up, w_down, g_final, loss_target, m_g_attn, m_w_in, m_b_in, m_sinks, m_rel_table, m_g_out_a, m_g_out_b, m_w_o, m_g_ffn, m_w_gate, m_w_up, m_w_down, m_g_final, v_g_attn, v_w_in, v_b_in, v_sinks, v_rel_table, v_g_out_a, v_g_out_b, v_w_o, v_g_ffn, v_w_gate, v_w_up, v_w_down, v_g_final):
    rest_names = ("w_o", "w_gate", "w_up", "w_down")

    rest = [w_o[0].astype(BF16), w_gate[0].astype(BF16).T, w_up[0].astype(BF16).T, w_down[0].astype(BF16)]
    in_state, _ = _exchange_start([w_in[0].astype(BF16).T], gather=True, near=True, name="gather_w_in_start")
    later = {}

    def whole(got):
        return [g.reshape(N_DEV * g.shape[1], D_MODEL) for g in got]

    def win_fn(after):
        near = _exchange_wait(in_state, after, gather=True, near=True, name="gather_w_in_near")
        fwd_state, tok = _forward_start(near, name="gather_w_in_forward")
        wint = whole(_forward_wait(fwd_state, tok, name="gather_w_in_wait"))[0]
        wint, src = lax.optimization_barrier((wint, rest))
        later["wo"], token_o = _exchange_start(src[:1], gather=True, name="gather_w_o_start")
        token_o, ffn_src = lax.optimization_barrier((token_o, src[1:]))
        later["ffn"], token = _exchange_start(ffn_src, gather=True, name="gather_ffn_start")
        return wint, token + token_o

    def wo_fn(after):
        return whole(_exchange_wait(later["wo"], after, gather=True, name="gather_w_o_wait"))[0]

    def ffn_fn(after):
        return whole(_exchange_wait(later["ffn"], after, gather=True, name="gather_ffn_wait"))

    def early_fn(dws):
        return _exchange_start([dws[n].reshape(N_DEV, -1, D_MODEL) for n in rest_names], gather=False,
                               name="scatter_rest_start")

    loss_part, grad_x, dwint, early_state, small = _local_step(
        x[0], loss_target[0], g_attn, b_in, sinks, rel_table, g_out_a, g_out_b, g_ffn, g_final,
        win_fn, wo_fn, ffn_fn, early_fn)
    parts_in = dwint.astype(BF16).reshape(N_DEV, D_IN // N_DEV, D_MODEL)
    in_state, token3 = _exchange_start([parts_in], gather=False, name="scatter_w_in_start")
    got = _exchange_wait(early_state, token3, gather=False, name="scatter_rest_wait")

    def update(n, parts, w, m, v, transposed):
        if transposed:
            return [a.T[None] for a in _adamw(parts, w[0].T, m[0].T, v[0].T, name="adamw_" + n)]
        return [a[None] for a in _adamw(parts, w[0], m[0], v[0], name="adamw_" + n)]

    big = dict(w_o=update("w_o", got[0], w_o, m_w_o, v_w_o, False),
               w_gate=update("w_gate", got[1], w_gate, m_w_gate, v_w_gate, True),
               w_up=update("w_up", got[2], w_up, m_w_up, v_w_up, True),
               w_down=update("w_down", got[3], w_down, m_w_down, v_w_down, False))

    unused = jnp.zeros((1,), F32)
    ws = dict(g_attn=g_attn, b_in=b_in, sinks=sinks, rel_table=rel_table, g_out_a=g_out_a, g_out_b=g_out_b,
              g_ffn=g_ffn, g_final=g_final, loss=unused)
    ms = dict(g_attn=m_g_attn, b_in=m_b_in, sinks=m_sinks, rel_table=m_rel_table, g_out_a=m_g_out_a,
              g_out_b=m_g_out_b, g_ffn=m_g_ffn, g_final=m_g_final, loss=unused)
    vs = dict(g_attn=v_g_attn, b_in=v_b_in, sinks=v_sinks, rel_table=v_rel_table, g_out_a=v_g_out_a,
              g_out_b=v_g_out_b, g_ffn=v_g_ffn, g_final=v_g_final, loss=unused)
    sparts = _all_gather(_pack_small(dict(small, loss=loss_part)), name="gather_small")
    sm_packed = _adamw_small(sparts, _pack_small(ws), _pack_small(ms), _pack_small(vs))
    sm = [_unpack_small(a, ws) for a in sm_packed]
    loss = sm[0]["loss"][0]

    done = sm_packed[1][:1, :1] + sum(big[n][1][0, :1, :1] for n in rest_names)
    got_in = _exchange_wait(in_state, done, gather=False, name="scatter_w_in_wait")[0]
    big["w_in"] = update("w_in", got_in, w_in, m_w_in, v_w_in, True)

    order = ("g_attn", "w_in", "b_in", "sinks", "rel_table", "g_out_a", "g_out_b", "w_o", "g_ffn", "w_gate", "w_up",
             "w_down", "g_final")
    outs = [loss, grad_x[None]]
    for k in range(4):
        outs += [big[n][k] if n in big else sm[k][n] for n in order]
    return tuple(outs)
```

```python
import math

import jax
import jax.numpy as jnp
from jax import lax
from jax.experimental import pallas as pl
from jax.experimental.pallas import tpu as pltpu

F32 = jnp.float32
BF16 = jnp.bfloat16

N_DEV = 8
D_MODEL = 1024
HEAD_DIM = 64
N_HEADS = 8
PAIR = 2 * HEAD_DIM
WIDTH = N_HEADS * HEAD_DIM
D_IN = 2304
D_INP = 2560
D_FF = 2816
BLK = 128
ROPE_THETA = 150000.0
REL_BUCKETS = 32
REL_MAX_DISTANCE = 2048
EPS = 1e-5
NEG = -1e30
BRANCHES = ((128, 1), (512, 4), (2048, 16))
Q_SCALE = HEAD_DIM ** -0.5

ADAM_LR = 0.001
ADAM_B1 = 0.9
ADAM_B2 = 0.999
ADAM_EPS = 1e-08
ADAM_WD = 0.01
ADAM_STEP = 10

VMEM_LIMIT = 56 * 1024 * 1024
MESH = pl.DeviceIdType.MESH

NT = (((1,), (1,)), ((), ()))
TN = (((0,), (0,)), ((), ()))

SMALL_ROWS = 56


def _params(sem=None):
    return pltpu.CompilerParams(dimension_semantics=sem, vmem_limit_bytes=VMEM_LIMIT)


def _sigmoid(x):
    return 1.0 / (1.0 + jnp.exp2(x * (-1.0 / math.log(2.0))))


def _rms_bwd(dh, xh, r, g):
    u = dh * g
    return r * (u - xh * jnp.mean(u * xh, axis=-1, keepdims=True))


def _rope_rot(t, first):
    return jnp.where(first, pltpu.roll(t, 96, 1), pltpu.roll(t, 32, 1))


N_CHUNK = WIDTH // PAIR


def _scr(tm):
    return pltpu.VMEM((N_CHUNK, tm, PAIR), F32)


def _scr_get(scr):
    return jnp.concatenate([scr[j] for j in range(N_CHUNK)], axis=1)


def _scr_put(scr, val):
    for j in range(N_CHUNK):
        scr[j] = val[:, j * PAIR:(j + 1) * PAIR]


def _unstride(view_ref, scr, dil, tm):
    n = tm // dil
    chunks = scr.shape[0]
    for r in range(dil):
        for j in range(chunks):
            col = (r * chunks + j) * PAIR
            scr.at[j][pl.ds(r, n, stride=dil), :] = view_ref[:, col:col + PAIR].astype(F32)


def _restride(scr, out_ref, dil, tm):
    n = tm // dil
    chunks = scr.shape[0]
    for r in range(dil):
        for j in range(chunks):
            col = (r * chunks + j) * PAIR
            rows = scr[j] if dil == 1 else scr.at[j][pl.ds(r, n, stride=dil), :]
            out_ref[:, col:col + PAIR] = rows.astype(out_ref.dtype)


def _view_specs(tm, width=WIDTH):
    return [pl.BlockSpec((tm // dil, dil * width), lambda i: (i, 0)) for _, dil in BRANCHES]


def _view_shapes(T, dtype, width=WIDTH):
    return [jax.ShapeDtypeStruct((T // dil, dil * width), dtype) for _, dil in BRANCHES]


def _norm_proj(x, g, w, b, cos, sin, *, tm=512):
    T = x.shape[0]

    def body(x_ref, g_ref, w_ref, b_ref, cos_ref, sin_ref, h_ref, qa_ref, ka_ref, va_ref, *rest):
        outs_b, ys = rest[:9], rest[9]
        xv = x_ref[...]
        r = lax.rsqrt(jnp.mean(xv * xv, axis=-1, keepdims=True) + EPS)
        h = (xv * r * g_ref[...]).astype(BF16)
        h_ref[...] = h
        cosv = cos_ref[...]
        sinv = sin_ref[...]
        lane = lax.broadcasted_iota(jnp.int32, (tm, PAIR), 1)
        first = (lane % HEAD_DIM) < (HEAD_DIM // 2)

        def proj(off):
            return (lax.dot_general(h, w_ref[off:off + 256, :], NT, preferred_element_type=F32)
                    + b_ref[:, off:off + 256])

        for (off, width, rot, scale), o_ref in zip(((0, 512, True, Q_SCALE), (512, 256, True, 1.0), (768, 256, False, 1.0)),
                                                   (qa_ref, ka_ref, va_ref)):
            for c in range(0, width, 256):
                y = proj(off + c)
                for j in range(0, 256, PAIR):
                    t = y[:, j:j + PAIR]
                    if rot:
                        t = t * cosv + _rope_rot(t, first) * sinv
                    if scale != 1.0:
                        t = t * scale
                    o_ref[:, c + j:c + j + PAIR] = t.astype(BF16)
        for n, (off, scale) in enumerate(((1024, Q_SCALE), (1536, 1.0), (2048, 1.0))):
            for c in range(0, WIDTH, 256):
                y = proj(off + c)
                y = y * scale if scale != 1.0 else y
                for j in range(0, 256, PAIR):
                    ys[(c + j) // PAIR] = y[:, j:j + PAIR]
            for (_, dil), o_ref in zip(BRANCHES, outs_b[3 * n:3 * n + 3]):
                _restride(ys, o_ref, dil, tm)

    row = lambda w_: pl.BlockSpec((tm, w_), lambda i: (i, 0))
    full = lambda a: pl.BlockSpec(a.shape, lambda i: (0, 0))
    return pl.pallas_call(
        body, name="norm_proj", grid=(T // tm,),
        in_specs=[row(D_MODEL), full(g), full(w), full(b), row(PAIR), row(PAIR)],
        out_specs=[row(D_MODEL), row(512), row(256), row(256)] + _view_specs(tm) * 3,
        out_shape=[jax.ShapeDtypeStruct((T, n), BF16) for n in (D_MODEL, 512, 256, 256)] + _view_shapes(T, BF16) * 3,
        scratch_shapes=[_scr(tm)],
        compiler_params=_params(("arbitrary",)),
    )(x, g, w, b, cos, sin)


MAX_SUB = 8
AHEAD = 2


def _attn_specs(kvw, sub):
    q_spec = pl.BlockSpec((sub * BLK, WIDTH), lambda r, i: (i, r))
    kc_spec = pl.BlockSpec((sub * BLK, kvw), lambda r, i: (i, r))
    kp_spec = pl.BlockSpec((BLK, kvw), lambda r, i: (jnp.maximum(sub * i - 1, 0), r))
    b_spec = pl.BlockSpec((2, N_HEADS, BLK, 2 * BLK), lambda r, i: (0, 0, 0, 0))
    return q_spec, kp_spec, kc_spec, b_spec


def _window(prev_ref, cur_ref, j, ksl):
    before = prev_ref[:, ksl] if j == 0 else cur_ref[(j - 1) * BLK:j * BLK, ksl]
    return jnp.concatenate([before, cur_ref[j * BLK:(j + 1) * BLK, ksl]], axis=0)


def _attn_fwd(q, k, v, bias, sinks, *, dil, kv_pairs, use_sink, name):
    L = q.shape[0]
    sub = min(MAX_SUB, L // BLK)
    ns = L // (sub * BLK)
    kvw = kv_pairs * PAIR
    rep = 4 // kv_pairs

    def body(sink_ref, q_ref, kp_ref, kc_ref, vp_ref, vc_ref, b_ref, o_ref, lse_ref):
        lane = lax.broadcasted_iota(jnp.int32, (1, PAIR), 1)
        lo = lane < HEAD_DIM
        first = jnp.where(pl.program_id(1) == 0, 1, 0)
        def scores(j, hp):
            rows = slice(j * BLK, (j + 1) * BLK)
            sl = slice(hp * PAIR, (hp + 1) * PAIR)
            ksl = slice((hp // rep) * PAIR, (hp // rep + 1) * PAIR)
            qp = q_ref[rows, sl]
            kk = _window(kp_ref, kc_ref, j, ksl)
            vv = _window(vp_ref, vc_ref, j, ksl)
            heads = []
            for e in range(2):
                h = 2 * hp + e
                msk = lo if e == 0 else jnp.logical_not(lo)
                qm = jnp.where(msk, qp, jnp.zeros_like(qp))
                s = lax.dot_general(qm, kk, NT, preferred_element_type=F32) + (b_ref[first, h] if j == 0 else b_ref[0, h])
                heads.append((h, msk, s))
            return rows, sl, vv, heads

        def outputs(rows, sl, vv, heads):
            o_pair = None
            lse_pair = None
            for h, msk, s in heads:
                m = jnp.max(s, axis=-1, keepdims=True)
                if use_sink:
                    sk = sink_ref[h]
                    m = jnp.maximum(m, sk)
                p = jnp.exp(s - m)
                l = jnp.sum(p, axis=-1, keepdims=True)
                if use_sink:
                    l = l + jnp.exp(sk - m)
                vm = jnp.where(msk, vv, jnp.zeros_like(vv))
                oe = jnp.dot(p.astype(BF16), vm, preferred_element_type=F32) * (1.0 / l)
                ls = m + jnp.log(l)
                if o_pair is None:
                    o_pair = oe
                    lse_pair = jnp.broadcast_to(ls, (BLK, PAIR))
                else:
                    o_pair = o_pair + oe
                    lse_pair = jnp.where(lo, lse_pair, ls)
            o_ref[rows, sl] = o_pair.astype(BF16)
            lse_ref[rows, sl] = lse_pair

        items = [(j, hp) for j in range(sub) for hp in range(4)]
        queue = [scores(*it) for it in items[:AHEAD]]
        for n in range(len(items)):
            if n + AHEAD < len(items):
                queue.append(scores(*items[n + AHEAD]))
            outputs(*queue.pop(0))

    q_spec, kp_spec, kc_spec, b_spec = _attn_specs(kvw, sub)
    return pl.pallas_call(
        body, name=name, grid=(dil, ns),
        in_specs=[pl.BlockSpec(memory_space=pltpu.SMEM), q_spec, kp_spec, kc_spec, kp_spec, kc_spec, b_spec],
        out_specs=[q_spec, q_spec],
        out_shape=[jax.ShapeDtypeStruct((L, dil * WIDTH), BF16), jax.ShapeDtypeStruct((L, dil * WIDTH), F32)],
        compiler_params=_params(("arbitrary", "arbitrary")),
    )(sinks, q, k, k, v, v, bias)


def _attn_bwd(q, k, v, o, do, lse, bias, sinks, *, dil, kv_pairs, use_sink, name, max_sub=MAX_SUB):
    L = q.shape[0]
    sub = min(max_sub, L // BLK)
    ns = L // (sub * BLK)
    n_steps = dil * ns
    kvw = kv_pairs * PAIR
    rep = 4 // kv_pairs
    last = slice((sub - 1) * BLK, sub * BLK)

    def body(sink_ref, q_ref, kp_ref, kc_ref, vp_ref, vc_ref, o_ref, do_ref, lse_ref, b_ref,
             dq_ref, dk_ref, dv_ref, dsum_ref, dsk_ref, pk_ref, pv_ref):
        t = pl.program_id(0)
        i = t % ns

        @pl.when(t == 0)
        def _():
            dsum_ref[...] = jnp.zeros_like(dsum_ref)
            dsk_ref[...] = jnp.zeros_like(dsk_ref)
            pk_ref[...] = jnp.zeros_like(pk_ref)
            pv_ref[...] = jnp.zeros_like(pv_ref)

        @pl.when(t < n_steps)
        def _():
            lo = lax.broadcasted_iota(jnp.int32, (1, PAIR), 1) < HEAD_DIM
            first = jnp.where(i == 0, 1, 0)
            dks = [[None] * kv_pairs for _ in range(sub)]
            dvs = [[None] * kv_pairs for _ in range(sub)]
            def scores(j, hp):
                rows = slice(j * BLK, (j + 1) * BLK)
                kvp = hp // rep
                sl = slice(hp * PAIR, (hp + 1) * PAIR)
                ksl = slice(kvp * PAIR, (kvp + 1) * PAIR)
                qp = q_ref[rows, sl]
                dop = do_ref[rows, sl]
                prod = dop.astype(F32) * o_ref[rows, sl].astype(F32)
                kk = _window(kp_ref, kc_ref, j, ksl)
                vv = _window(vp_ref, vc_ref, j, ksl)
                heads = []
                for e in range(2):
                    h = 2 * hp + e
                    msk = lo if e == 0 else jnp.logical_not(lo)
                    qm = jnp.where(msk, qp, jnp.zeros_like(qp))
                    dom = jnp.where(msk, dop, jnp.zeros_like(dop))
                    km = jnp.where(msk, kk, jnp.zeros_like(kk))
                    s = (lax.dot_general(qm, kk, NT, preferred_element_type=F32)
                         + (b_ref[first, h] if j == 0 else b_ref[0, h]))
                    dp = lax.dot_general(dom, vv, NT, preferred_element_type=F32)
                    heads.append((h, msk, qm, dom, km, s, dp))
                return j, rows, kvp, sl, prod, heads

            def grads(j, rows, kvp, sl, prod, heads):
                dq_pair = None
                c_pair = None
                qms, doms, dsbs, pbs = [], [], [], []
                for h, msk, qm, dom, km, s, dp in heads:
                    ls = lse_ref[rows, h * HEAD_DIM:h * HEAD_DIM + 1]
                    p = jnp.exp(s - ls)
                    delta = jnp.sum(jnp.where(msk, prod, 0.0), axis=-1, keepdims=True)
                    ds = p * (dp - delta)
                    if use_sink:
                        ce = jnp.exp(sink_ref[h] - ls) * delta
                        c_pair = jnp.broadcast_to(ce, (BLK, PAIR)) if c_pair is None else jnp.where(msk, ce, c_pair)
                    else:
                        dsum_ref[h] += ds
                    dsb = ds.astype(BF16)
                    dqe = jnp.dot(dsb, km, preferred_element_type=F32)
                    dq_pair = dqe if dq_pair is None else dq_pair + dqe
                    qms.append(qm)
                    doms.append(dom)
                    dsbs.append(dsb)
                    pbs.append(p.astype(BF16))
                dke = lax.dot_general(jnp.concatenate(dsbs, axis=0), jnp.concatenate(qms, axis=0), TN,
                                      preferred_element_type=F32)
                dve = lax.dot_general(jnp.concatenate(pbs, axis=0), jnp.concatenate(doms, axis=0), TN,
                                      preferred_element_type=F32)
                dks[j][kvp] = dke if dks[j][kvp] is None else dks[j][kvp] + dke
                dvs[j][kvp] = dve if dvs[j][kvp] is None else dvs[j][kvp] + dve
                dq_ref[rows, sl] = (dq_pair * Q_SCALE).astype(BF16)
                if use_sink:
                    dsk_ref[:, sl] += c_pair

            items = [(j, hp) for j in range(sub) for hp in range(4)]
            ahead = AHEAD + 1 if use_sink else AHEAD
            queue = [scores(*it) for it in items[:ahead]]
            for n in range(len(items)):
                if n + ahead < len(items):
                    queue.append(scores(*items[n + ahead]))
                grads(*queue.pop(0))
            for kvp in range(kv_pairs):
                ksl = slice(kvp * PAIR, (kvp + 1) * PAIR)
                for pend_ref, out_ref, parts in ((pk_ref, dk_ref, [d[kvp] for d in dks]),
                                                 (pv_ref, dv_ref, [d[kvp] for d in dvs])):
                    if sub > 1:
                        out_ref[:(sub - 1) * BLK, ksl] = pend_ref[:(sub - 1) * BLK, ksl].astype(BF16)
                    out_ref[last, ksl] = (pend_ref[last, ksl] + parts[0][:BLK]).astype(BF16)
                    for j in range(sub):
                        own = parts[j][BLK:]
                        pend_ref[j * BLK:(j + 1) * BLK, ksl] = own + parts[j + 1][:BLK] if j + 1 < sub else own

        @pl.when(t == n_steps)
        def _():
            dk_ref[...] = pk_ref[...].astype(BF16)
            dv_ref[...] = pv_ref[...].astype(BF16)

    def at(t):
        t = jnp.minimum(t, n_steps - 1)
        return t % ns, t // ns

    def before(t):
        return at(jnp.maximum(t - 1, 0))

    q_spec = pl.BlockSpec((sub * BLK, WIDTH), at)
    kc_spec = pl.BlockSpec((sub * BLK, kvw), at)
    kp_spec = pl.BlockSpec((BLK, kvw), lambda t: (jnp.maximum(sub * at(t)[0] - 1, 0), at(t)[1]))
    b_spec = pl.BlockSpec((2, N_HEADS, BLK, 2 * BLK), lambda t: (0, 0, 0, 0))
    dkv_spec = pl.BlockSpec((sub * BLK, kvw), before)
    return pl.pallas_call(
        body, name=name, grid=(n_steps + 1,),
        in_specs=[pl.BlockSpec(memory_space=pltpu.SMEM), q_spec, kp_spec, kc_spec, kp_spec, kc_spec,
                  q_spec, q_spec, q_spec, b_spec],
        out_specs=[q_spec, dkv_spec, dkv_spec,
                   pl.BlockSpec((N_HEADS, BLK, 2 * BLK), lambda t: (0, 0, 0)),
                   pl.BlockSpec((BLK, WIDTH), lambda t: (0, 0))],
        out_shape=[jax.ShapeDtypeStruct((L, dil * WIDTH), BF16),
                   jax.ShapeDtypeStruct((L, dil * kvw), BF16),
                   jax.ShapeDtypeStruct((L, dil * kvw), BF16),
                   jax.ShapeDtypeStruct((N_HEADS, BLK, 2 * BLK), F32),
                   jax.ShapeDtypeStruct((BLK, WIDTH), F32)],
        scratch_shapes=[pltpu.VMEM((sub * BLK, kvw), F32), pltpu.VMEM((sub * BLK, kvw), F32)],
        compiler_params=_params(("arbitrary",)),
    )(sinks, q, k, k, v, v, o, do, lse, bias)


def _merge_wo(x, oa, o1, o2, o3, l1, l2, l3, ga, gb, wo, gf, *, tm=512):
    T = x.shape[0]

    def body(x_ref, oa_ref, o1_ref, o2_ref, o3_ref, l1_ref, l2_ref, l3_ref, ga_ref, gb_ref, wo_ref, gf_ref,
             x2_ref, mix_ref, h2_ref, ob1_ref, ob4_ref, ob16_ref, ls1_ref, ls4_ref, ls16_ref, so2, so3, sl2, sl3):
        _unstride(o2_ref, so2, BRANCHES[1][1], tm)
        _unstride(o3_ref, so3, BRANCHES[2][1], tm)
        _unstride(l2_ref, sl2, BRANCHES[1][1], tm)
        _unstride(l3_ref, sl3, BRANCHES[2][1], tm)
        la, lb, lc = l1_ref[...], _scr_get(sl2), _scr_get(sl3)
        m = jnp.maximum(jnp.maximum(la, lb), lc)
        ea, eb, ec = jnp.exp(la - m), jnp.exp(lb - m), jnp.exp(lc - m)
        den = ea + eb + ec
        inv = 1.0 / den
        ob = (ea * o1_ref[...].astype(F32) + eb * _scr_get(so2) + ec * _scr_get(so3)) * inv
        _scr_put(so2, ob)
        _scr_put(sl2, m + jnp.log(den))
        for (_, dil), o_ref, l_ref in zip(BRANCHES, (ob1_ref, ob4_ref, ob16_ref), (ls1_ref, ls4_ref, ls16_ref)):
            _restride(so2, o_ref, dil, tm)
            _restride(sl2, l_ref, dil, tm)
        oav = oa_ref[...].astype(F32)
        ra = lax.rsqrt(jnp.mean(oav * oav, axis=-1, keepdims=True) + EPS)
        rb = lax.rsqrt(jnp.mean(ob * ob, axis=-1, keepdims=True) + EPS)
        mix_ref[:, :WIDTH] = (oav * ra * ga_ref[...]).astype(BF16)
        mix_ref[:, WIDTH:] = (ob * rb * gb_ref[...]).astype(BF16)
        x2 = x_ref[...] + jnp.dot(mix_ref[...], wo_ref[...], preferred_element_type=F32)
        x2_ref[...] = x2
        r2 = lax.rsqrt(jnp.mean(x2 * x2, axis=-1, keepdims=True) + EPS)
        h2_ref[...] = (x2 * r2 * gf_ref[...]).astype(BF16)

    row = lambda w_: pl.BlockSpec((tm, w_), lambda i: (i, 0))
    full = lambda a: pl.BlockSpec(a.shape, lambda i: (0, 0))
    return pl.pallas_call(
        body, name="merge_wo", grid=(T // tm,),
        in_specs=[row(D_MODEL), row(WIDTH)] + _view_specs(tm) * 2 + [full(ga), full(gb), full(wo), full(gf)],
        out_specs=[row(D_MODEL), row(D_MODEL), row(D_MODEL)] + _view_specs(tm) * 2,
        out_shape=[jax.ShapeDtypeStruct((T, D_MODEL), F32), jax.ShapeDtypeStruct((T, D_MODEL), BF16),
                   jax.ShapeDtypeStruct((T, D_MODEL), BF16)] + _view_shapes(T, BF16) + _view_shapes(T, F32),
        scratch_shapes=[_scr(tm)] * 4,
        compiler_params=_params(("arbitrary",)),
    )(x, oa, o1, o2, o3, l1, l2, l3, ga, gb, wo, gf)


def _ffn_up(h2, wgt, wut, *, tm=512, fc=D_FF, rc=512, cc=256):
    T = h2.shape[0]

    def body(h_ref, wg_ref, wu_ref, gate_ref, up_ref, act_ref):
        for s in range(0, tm, rc):
            h = h_ref[s:s + rc, :]
            for c in range(0, fc, cc):
                gt = lax.dot_general(h, wg_ref[c:c + cc, :], NT, preferred_element_type=F32)
                u = lax.dot_general(h, wu_ref[c:c + cc, :], NT, preferred_element_type=F32)
                gate_ref[s:s + rc, c:c + cc] = gt.astype(BF16)
                up_ref[s:s + rc, c:c + cc] = u.astype(BF16)
                act_ref[s:s + rc, c:c + cc] = (gt * _sigmoid(gt) * u).astype(BF16)

    rowd = pl.BlockSpec((tm, D_MODEL), lambda i, c: (i, 0))
    wrow = pl.BlockSpec((fc, D_MODEL), lambda i, c: (c, 0))
    oc = pl.BlockSpec((tm, fc), lambda i, c: (i, c))
    return pl.pallas_call(
        body, name="ffn_up", grid=(T // tm, D_FF // fc),
        in_specs=[rowd, wrow, wrow],
        out_specs=[oc, oc, oc],
        out_shape=[jax.ShapeDtypeStruct((T, D_FF), BF16)] * 3,
        compiler_params=_params(("arbitrary", "arbitrary")),
    )(h2, wgt, wut)


def _ffn_down_loss(act, wd, x2, tgt, g, *, tm=1024, rc=256):
    T = x2.shape[0]

    def body(act_ref, wd_ref, x2_ref, tgt_ref, g_ref, dx_ref, dxb_ref, loss_ref, dg_ref):
        @pl.when(pl.program_id(0) == 0)
        def _():
            loss_ref[...] = jnp.zeros_like(loss_ref)
            dg_ref[...] = jnp.zeros_like(dg_ref)

        gv = g_ref[...]
        lsum = jnp.zeros((1, 1), F32)
        dgs = jnp.zeros((1, D_MODEL), F32)
        for c in range(0, tm, rc):
            x3 = x2_ref[c:c + rc, :] + jnp.dot(act_ref[c:c + rc, :], wd_ref[...], preferred_element_type=F32)
            r = lax.rsqrt(jnp.mean(x3 * x3, axis=-1, keepdims=True) + EPS)
            xh = x3 * r
            diff = xh * gv - tgt_ref[c:c + rc, :]
            lsum = lsum + jnp.sum(jnp.sum(diff * diff, axis=-1, keepdims=True), axis=0, keepdims=True)
            dy = diff * (1.0 / D_MODEL)
            dgs = dgs + jnp.sum(dy * xh, axis=0, keepdims=True)
            dx = _rms_bwd(dy, xh, r, gv)
            dx_ref[c:c + rc, :] = dx
            dxb_ref[c:c + rc, :] = dx.astype(BF16)
        loss_ref[...] += lsum * (0.5 / D_MODEL)
        dg_ref[...] += dgs

    rowd = pl.BlockSpec((tm, D_MODEL), lambda i: (i, 0))
    return pl.pallas_call(
        body, name="ffn_down_loss", grid=(T // tm,),
        in_specs=[pl.BlockSpec((tm, D_FF), lambda i: (i, 0)),
                  pl.BlockSpec((D_FF, D_MODEL), lambda i: (0, 0), pipeline_mode=pl.Buffered(1)),
                  rowd, rowd, pl.BlockSpec(g.shape, lambda i: (0, 0))],
        out_specs=[rowd, rowd, pl.BlockSpec((1, 1), lambda i: (0, 0)), pl.BlockSpec((1, D_MODEL), lambda i: (0, 0))],
        out_shape=[jax.ShapeDtypeStruct((T, D_MODEL), F32), jax.ShapeDtypeStruct((T, D_MODEL), BF16),
                   jax.ShapeDtypeStruct((1, 1), F32), jax.ShapeDtypeStruct((1, D_MODEL), F32)],
        compiler_params=_params(("arbitrary",)),
    )(act, wd, x2, tgt, g)


def _ffn_bwd(dx3, gate, up, wd, wgt, wut, x2, g, *, tm=256, cc=256):
    T = x2.shape[0]
    n_tiles = T // tm
    RING = 3

    def body(dx_ref, gate_hbm, up_hbm, wd_ref, wg_ref, wu_ref, x2_ref, g_ref,
             dgate_ref, dup_ref, dx2_ref, dx2b_ref, dg_ref, gate_buf, up_buf, sems):
        i = pl.program_id(0)

        def copies(step, slot):
            rows = pl.ds(step * tm if isinstance(step, int) else pl.multiple_of(step * tm, tm), tm)
            return (pltpu.make_async_copy(gate_hbm.at[rows, :], gate_buf.at[slot], sems.at[0, slot]),
                    pltpu.make_async_copy(up_hbm.at[rows, :], up_buf.at[slot], sems.at[1, slot]))

        @pl.when(i == 0)
        def _():
            dg_ref[...] = jnp.zeros_like(dg_ref)
            for s in range(min(RING - 1, n_tiles)):
                for cp in copies(s, s):
                    cp.start()

        @pl.when(i + RING - 1 < n_tiles)
        def _():
            for cp in copies(i + RING - 1, (i + RING - 1) % RING):
                cp.start()

        slot = i % RING
        for cp in copies(i, slot):
            cp.wait()
        gate_ref = gate_buf.at[slot]
        up_ref = up_buf.at[slot]

        dxb = dx_ref[...].astype(BF16)
        for c in range(0, D_FF, cc):
            dact = lax.dot_general(dxb, wd_ref[c:c + cc, :], NT, preferred_element_type=F32)
            gt = gate_ref[:, c:c + cc].astype(F32)
            u = up_ref[:, c:c + cc].astype(F32)
            sg = _sigmoid(gt)
            a = dact * sg
            dgate_ref[:, c:c + cc] = (a * u * ((1.0 + gt) - gt * sg)).astype(BF16)
            dup_ref[:, c:c + cc] = (a * gt).astype(BF16)
        dh = (jnp.dot(dgate_ref[...], wg_ref[...], preferred_element_type=F32)
              + jnp.dot(dup_ref[...], wu_ref[...], preferred_element_type=F32))
        xv = x2_ref[...]
        r = lax.rsqrt(jnp.mean(xv * xv, axis=-1, keepdims=True) + EPS)
        xh = xv * r
        dg_ref[...] += jnp.sum(dh * xh, axis=0, keepdims=True)
        d = dx_ref[...] + _rms_bwd(dh, xh, r, g_ref[...])
        dx2_ref[...] = d
        dx2b_ref[...] = d.astype(BF16)

    rowd = pl.BlockSpec((tm, D_MODEL), lambda i: (i, 0))
    rowf = pl.BlockSpec((tm, D_FF), lambda i: (i, 0))
    wfull = pl.BlockSpec((D_FF, D_MODEL), lambda i: (0, 0), pipeline_mode=pl.Buffered(1))
    return pl.pallas_call(
        body, name="ffn_bwd", grid=(n_tiles,),
        in_specs=[rowd, pl.BlockSpec(memory_space=pl.ANY), pl.BlockSpec(memory_space=pl.ANY), wfull, wfull, wfull, rowd,
                  pl.BlockSpec(g.shape, lambda i: (0, 0))],
        out_specs=[rowf, rowf, rowd, rowd, pl.BlockSpec((1, D_MODEL), lambda i: (0, 0))],
        out_shape=[jax.ShapeDtypeStruct((T, D_FF), BF16), jax.ShapeDtypeStruct((T, D_FF), BF16),
                   jax.ShapeDtypeStruct((T, D_MODEL), F32), jax.ShapeDtypeStruct((T, D_MODEL), BF16),
                   jax.ShapeDtypeStruct((1, D_MODEL), F32)],
        scratch_shapes=[pltpu.VMEM((RING, tm, D_FF), BF16), pltpu.VMEM((RING, tm, D_FF), BF16),
                        pltpu.SemaphoreType.DMA((2, RING))],
        compiler_params=_params(("arbitrary",)),
    )(dx3, gate, up, wd, wgt, wut, x2, g)


def _matmul_tn(a, b, *, tk, tn, tt=2048, out_dtype=BF16, name):
    T, K = a.shape
    N = b.shape[1]
    nt = T // tt

    def body(a_ref, b_ref, o_ref, acc_ref):
        part = lax.dot_general(a_ref[...], b_ref[...], TN, preferred_element_type=F32)

        @pl.when(pl.program_id(2) == 0)
        def _():
            acc_ref[...] = part

        @pl.when(pl.program_id(2) > 0)
        def _():
            acc_ref[...] += part

        @pl.when(pl.program_id(2) == nt - 1)
        def _():
            o_ref[...] = acc_ref[...].astype(out_dtype)

    return pl.pallas_call(
        body, name=name, grid=(K // tk, N // tn, nt),
        in_specs=[pl.BlockSpec((tt, tk), lambda i, j, t: (t, i)), pl.BlockSpec((tt, tn), lambda i, j, t: (t, j))],
        out_specs=pl.BlockSpec((tk, tn), lambda i, j, t: (i, j)),
        out_shape=jax.ShapeDtypeStruct((K, N), out_dtype),
        scratch_shapes=[pltpu.VMEM((tk, tn), F32)],
        compiler_params=_params(("arbitrary", "arbitrary", "arbitrary")),
    )(a, b)


def _wo_bwd(dx2b, wo, oa, ob, ga, gb, mixed, *, tm=1024, wc=256):
    T = dx2b.shape[0]
    n_tiles = T // tm

    def body(dx_ref, wo_ref, oa_ref, ob_ref, ga_ref, gb_ref, mix_ref,
             doa_ref, dob1_ref, dob4_ref, dob16_ref, dga_ref, dgb_ref, dwo_ref, scr, dw_acc):
        @pl.when(pl.program_id(0) == 0)
        def _():
            dga_ref[...] = jnp.zeros_like(dga_ref)
            dgb_ref[...] = jnp.zeros_like(dgb_ref)
            dw_acc[...] = jnp.zeros_like(dw_acc)

        dxv = dx_ref[...]
        for c in range(0, D_MODEL, wc):
            dw_acc[c:c + wc, :] += lax.dot_general(mix_ref[:, c:c + wc], dxv, TN, preferred_element_type=F32)

        @pl.when(pl.program_id(0) == n_tiles - 1)
        def _():
            dwo_ref[...] = dw_acc[...].astype(BF16)

        dm = lax.dot_general(dxv, wo_ref[...], NT, preferred_element_type=F32)
        for o_ref, g_ref, dg_ref, sl in ((oa_ref, ga_ref, dga_ref, slice(0, WIDTH)),
                                         (ob_ref, gb_ref, dgb_ref, slice(WIDTH, 2 * WIDTH))):
            ov = o_ref[...].astype(F32)
            r = lax.rsqrt(jnp.mean(ov * ov, axis=-1, keepdims=True) + EPS)
            xh = ov * r
            d = dm[:, sl]
            dg_ref[...] += jnp.sum(d * xh, axis=0, keepdims=True)
            do = _rms_bwd(d, xh, r, g_ref[...])
            if o_ref is oa_ref:
                doa_ref[...] = do.astype(BF16)
            else:
                _scr_put(scr, do)
                for (_, dil), v_ref in zip(BRANCHES, (dob1_ref, dob4_ref, dob16_ref)):
                    _restride(scr, v_ref, dil, tm)

    row = lambda w_: pl.BlockSpec((tm, w_), lambda i: (i, 0))
    full = lambda a: pl.BlockSpec(a.shape, lambda i: (0, 0))
    return pl.pallas_call(
        body, name="wo_bwd", grid=(T // tm,),
        in_specs=[row(D_MODEL), full(wo), row(WIDTH), row(WIDTH), full(ga), full(gb), row(D_MODEL)],
        out_specs=[row(WIDTH)] + _view_specs(tm)
        + [pl.BlockSpec((1, WIDTH), lambda i: (0, 0)), pl.BlockSpec((1, WIDTH), lambda i: (0, 0)), full(wo)],
        out_shape=[jax.ShapeDtypeStruct((T, WIDTH), BF16)] + _view_shapes(T, BF16)
        + [jax.ShapeDtypeStruct((1, WIDTH), F32), jax.ShapeDtypeStruct((1, WIDTH), F32),
           jax.ShapeDtypeStruct((D_MODEL, D_MODEL), BF16)],
        scratch_shapes=[_scr(tm), pltpu.VMEM((D_MODEL, D_MODEL), F32)],
        compiler_params=_params(("arbitrary",)),
    )(dx2b, wo, oa, ob, ga, gb, mixed)


def _inproj_bwd(dqa, dka, dva, dqs, dks, dvs, cos, sin, w, x, dx2, g, h1, *, tm=512, wc=256):
    T = dqa.shape[0]
    n_tiles = T // tm

    def body(dqa_ref, dka_ref, dva_ref, q1, q2, q3, k1, k2, k3, v1, v2, v3, cos_ref, sin_ref, w_ref, x_ref, dx2_ref,
             g_ref, h1_ref, db_ref, gx_ref, dg_ref, dw_ref, dp_ref, dw_acc, acc, tmp):
        @pl.when(pl.program_id(0) == 0)
        def _():
            db_ref[...] = jnp.zeros_like(db_ref)
            dg_ref[...] = jnp.zeros_like(dg_ref)
            dw_acc[...] = jnp.zeros_like(dw_acc)

        cosv = cos_ref[...]
        sinv = sin_ref[...]
        lane = lax.broadcasted_iota(jnp.int32, (tm, PAIR), 1)
        first = (lane % HEAD_DIM) < (HEAD_DIM // 2)

        def put(off, val):
            dp_ref[:, off:off + PAIR] = val.astype(BF16)
            db_ref[:, off:off + PAIR] += jnp.sum(val, axis=0, keepdims=True)

        for src, off, width in ((dqa_ref, 0, 512), (dka_ref, 512, 256)):
            for j in range(0, width, PAIR):
                d = src[:, j:j + PAIR].astype(F32)
                put(off + j, d * cosv - _rope_rot(d, first) * sinv)
        for j in range(0, 256, PAIR):
            put(768 + j, dva_ref[:, j:j + PAIR].astype(F32))
        for (a, b, c), off in (((q1, q2, q3), 1024), ((k1, k2, k3), 1536), ((v1, v2, v3), 2048)):
            _unstride(b, acc, BRANCHES[1][1], tm)
            _unstride(c, tmp, BRANCHES[2][1], tm)
            for j in range(N_CHUNK):
                put(off + j * PAIR, a[:, j * PAIR:(j + 1) * PAIR].astype(F32) + acc[j] + tmp[j])

        dh = jnp.dot(dp_ref[...], w_ref[...], preferred_element_type=F32)
        xv = x_ref[...]
        r = lax.rsqrt(jnp.mean(xv * xv, axis=-1, keepdims=True) + EPS)
        xh = xv * r
        dg_ref[...] += jnp.sum(dh * xh, axis=0, keepdims=True)
        gx_ref[...] = dx2_ref[...] + _rms_bwd(dh, xh, r, g_ref[...])

        h1v = h1_ref[...]
        for c in range(0, D_INP, wc):
            dw_acc[c:c + wc, :] += lax.dot_general(dp_ref[:, c:c + wc], h1v, TN, preferred_element_type=F32)

        @pl.when(pl.program_id(0) == n_tiles - 1)
        def _():
            pltpu.sync_copy(dw_acc, dw_ref)

    row = lambda w_: pl.BlockSpec((tm, w_), lambda i: (i, 0))
    full = lambda a: pl.BlockSpec(a.shape, lambda i: (0, 0))
    return pl.pallas_call(
        body, name="inproj_bwd", grid=(n_tiles,),
        in_specs=[row(512), row(256), row(256)] + _view_specs(tm) * 3 + [row(PAIR), row(PAIR)]
        + [pl.BlockSpec(w.shape, lambda i: (0, 0), pipeline_mode=pl.Buffered(1)), row(D_MODEL), row(D_MODEL), full(g),
           row(D_MODEL)],
        out_specs=[pl.BlockSpec((1, D_INP), lambda i: (0, 0)), row(D_MODEL),
                   pl.BlockSpec((1, D_MODEL), lambda i: (0, 0)), pl.BlockSpec(memory_space=pl.ANY)],
        out_shape=[jax.ShapeDtypeStruct((1, D_INP), F32), jax.ShapeDtypeStruct((T, D_MODEL), F32),
                   jax.ShapeDtypeStruct((1, D_MODEL), F32), jax.ShapeDtypeStruct((D_INP, D_MODEL), F32)],
        scratch_shapes=[pltpu.VMEM((tm, D_INP), BF16), pltpu.VMEM((D_INP, D_MODEL), F32), _scr(tm), _scr(tm)],
        compiler_params=_params(("arbitrary",)),
    )(dqa, dka, dva, *dqs, *dks, *dvs, cos, sin, w, x, dx2, g, h1)


def _bias_sink_grads(dsums, bmaps, dsk):
    def body(s1, s2, s3, m1, m2, m3, dsk_ref, drel_ref, dsink_ref):
        row = lax.broadcasted_iota(jnp.int32, (N_HEADS, 128), 0)
        lane = lax.broadcasted_iota(jnp.int32, (N_HEADS, 128), 1)
        out = jnp.zeros((N_HEADS, 128), F32)
        for s_ref, m_ref in ((s1, m1), (s2, m2), (s3, m3)):
            bm = m_ref[...]
            for h in range(N_HEADS):
                a = s_ref[h]
                for b in range(REL_BUCKETS):
                    v = jnp.sum(jnp.sum(jnp.where(bm == b, a, 0.0), axis=-1, keepdims=True), axis=0, keepdims=True)
                    out = out + jnp.where((row == h) & (lane == b), v, 0.0)
        drel_ref[...] = out
        dsink_ref[...] = -jnp.sum(dsk_ref[...], axis=0, keepdims=True)

    vm = pl.BlockSpec(memory_space=pltpu.VMEM)
    return pl.pallas_call(
        body, name="bias_sink_grads",
        in_specs=[vm] * 7, out_specs=[vm, vm],
        out_shape=[jax.ShapeDtypeStruct((N_HEADS, 128), F32), jax.ShapeDtypeStruct((1, WIDTH), F32)],
        compiler_params=_params(),
    )(*dsums, *bmaps, dsk)


def _all_gather(blk, *, name):
    R, C = blk.shape

    def body(x_ref, out_ref, send_sems, recv_sems, local_sem):
        x, y, c = lax.axis_index("x"), lax.axis_index("y"), lax.axis_index("c")
        me, sibling = (x, y, c), (x, y, 1 - c)
        chips = [(1 - x, y), (x, 1 - y), (1 - x, 1 - y)]

        def slot(px, py, pc):
            return out_ref.at[4 * px + 2 * py + pc]

        def copy(k, block, to, src=None):
            return pltpu.make_async_remote_copy(
                src_ref=slot(*block) if src is None else src, dst_ref=slot(*block),
                send_sem=send_sems.at[k], recv_sem=recv_sems.at[k], device_id=to, device_id_type=MESH)

        mine = pltpu.make_async_copy(x_ref, slot(*me), local_sem)
        mine.start()
        first = [copy(0, me, sibling, src=x_ref)]
        first += [copy(1 + j, me, (*chip, c), src=x_ref) for j, chip in enumerate(chips)]
        for cp in first:
            cp.start()
        passed = [copy(4 + j, (*chip, c), sibling) for j, chip in enumerate(chips)]
        for j, chip in enumerate(chips):
            copy(1 + j, (*chip, c), me).wait_recv()
            passed[j].start()
        copy(0, sibling, me).wait_recv()
        for j, chip in enumerate(chips):
            copy(4 + j, (*chip, 1 - c), me).wait_recv()
        for cp in first + passed:
            cp.wait_send()
        mine.wait()

    return pl.pallas_call(
        body, name=name,
        in_specs=[pl.BlockSpec(memory_space=pl.ANY)], out_specs=pl.BlockSpec(memory_space=pl.ANY),
        out_shape=jax.ShapeDtypeStruct((N_DEV, R, C), blk.dtype),
        scratch_shapes=[pltpu.SemaphoreType.DMA((7,)), pltpu.SemaphoreType.DMA((7,)), pltpu.SemaphoreType.DMA],
        compiler_params=pltpu.CompilerParams(has_side_effects=True),
    )(blk)


def _peers(x, y, c):
    return [(x ^ (k >> 2), y ^ ((k >> 1) & 1), c ^ (k & 1)) for k in range(1, N_DEV)]


_HBM = pl.BlockSpec(memory_space=pltpu.HBM)
_SEM = pl.BlockSpec(memory_space=pltpu.SEMAPHORE)
_EFFECT = pltpu.SideEffectType.DATAFLOW_SIDE_EFFECTING


def _peer_list(x, y, c, near):
    if near:
        return [(x, y, 1 - c), (1 - x, y, c), (x, 1 - y, c), (1 - x, 1 - y, c)]
    return _peers(x, y, c)


def _exchange_start(srcs, *, gather, name, near=False):
    n = len(srcs)
    n_peers = 4 if near else N_DEV - 1
    lands = [lax.empty((N_DEV,) + s.shape[-2:], s.dtype) for s in srcs]

    def body(*refs):
        src_refs, land_refs = refs[:n], refs[n:2 * n]
        send_sems, recv_sems = refs[2 * n], refs[2 * n + 1]
        token = refs[-1]
        x, y, c = lax.axis_index("x"), lax.axis_index("y"), lax.axis_index("c")
        mine = 4 * x + 2 * y + c
        for a in range(n):
            for k, peer in enumerate(_peer_list(x, y, c, near)):
                dest = 4 * peer[0] + 2 * peer[1] + peer[2]
                j = a * n_peers + k
                pltpu.make_async_remote_copy(
                    src_ref=src_refs[a] if gather else src_refs[a].at[dest], dst_ref=land_refs[a].at[mine],
                    send_sem=send_sems.at[j], recv_sem=recv_sems.at[j], device_id=peer, device_id_type=MESH).start()
        token[...] = jnp.zeros_like(token)

    sems = pltpu.SemaphoreType.DMA((n * n_peers,))
    out = pl.pallas_call(
        body, name=name,
        out_shape=(sems, sems) + tuple(pltpu.HBM(a.shape, a.dtype) for a in list(srcs) + lands)
        + (jax.ShapeDtypeStruct((8, 128), F32),),
        in_specs=(_HBM,) * (2 * n), out_specs=(_SEM, _SEM) + (_HBM,) * (2 * n) + (pl.BlockSpec(memory_space=pltpu.VMEM),),
        input_output_aliases={i: 2 + i for i in range(2 * n)},
        compiler_params=pltpu.CompilerParams(has_side_effects=_EFFECT),
    )(*[pltpu.with_memory_space_constraint(a, pltpu.HBM) for a in list(srcs) + lands])
    return out[:-1], out[-1]


def _exchange_wait(state, after, *, gather, name, near=False):
    send_sems, recv_sems = state[0], state[1]
    n = (len(state) - 2) // 2
    n_peers = 4 if near else N_DEV - 1
    arrays = state[2:]

    def body(*refs):
        src_refs, land_refs = refs[:n], refs[n:2 * n]
        send_sems, recv_sems = refs[2 * n], refs[2 * n + 1]
        x, y, c = lax.axis_index("x"), lax.axis_index("y"), lax.axis_index("c")
        for a in range(n):
            for k, peer in enumerate(_peer_list(x, y, c, near)):
                other = 4 * peer[0] + 2 * peer[1] + peer[2]
                j = a * n_peers + k
                copy = pltpu.make_async_remote_copy(
                    src_ref=src_refs[a] if gather else src_refs[a].at[other], dst_ref=land_refs[a].at[other],
                    send_sem=send_sems.at[j], recv_sem=recv_sems.at[j], device_id=peer, device_id_type=MESH)
                copy.wait_send()
                copy.wait_recv()

    out = pl.pallas_call(
        body, name=name,
        out_shape=tuple(pltpu.HBM(a.shape, a.dtype) for a in arrays),
        in_specs=(_HBM,) * (2 * n) + (_SEM, _SEM, pl.BlockSpec(memory_space=pl.ANY)), out_specs=(_HBM,) * (2 * n),
        input_output_aliases={i: i for i in range(2 * n)},
        compiler_params=pltpu.CompilerParams(has_side_effects=_EFFECT),
    )(*arrays, send_sems, recv_sems, after)
    mine = 4 * lax.axis_index("x") + 2 * lax.axis_index("y") + lax.axis_index("c")
    own = out[:n] if gather else [lax.dynamic_index_in_dim(s, mine, 0, keepdims=False) for s in out[:n]]
    return [lax.dynamic_update_slice(g, o[None], (mine, 0, 0)) for g, o in zip(out[n:], own)]


def _forward_start(lands, *, name):
    n = len(lands)

    def body(*refs):
        land_refs, send_sems, recv_sems, token = refs[:n], refs[n], refs[n + 1], refs[-1]
        x, y, c = lax.axis_index("x"), lax.axis_index("y"), lax.axis_index("c")
        for a in range(n):
            for j, (px, py) in enumerate(((1 - x, y), (x, 1 - y), (1 - x, 1 - y))):
                blk = 4 * px + 2 * py + c
                pltpu.make_async_remote_copy(
                    src_ref=land_refs[a].at[blk], dst_ref=land_refs[a].at[blk], send_sem=send_sems.at[3 * a + j],
                    recv_sem=recv_sems.at[3 * a + j], device_id=(x, y, 1 - c), device_id_type=MESH).start()
        token[...] = jnp.zeros_like(token)

    sems = pltpu.SemaphoreType.DMA((3 * n,))
    out = pl.pallas_call(
        body, name=name,
        out_shape=(sems, sems) + tuple(pltpu.HBM(a.shape, a.dtype) for a in lands) + (jax.ShapeDtypeStruct((8, 128), F32),),
        in_specs=(_HBM,) * n, out_specs=(_SEM, _SEM) + (_HBM,) * n + (pl.BlockSpec(memory_space=pltpu.VMEM),),
        input_output_aliases={i: 2 + i for i in range(n)},
        compiler_params=pltpu.CompilerParams(has_side_effects=_EFFECT),
    )(*[pltpu.with_memory_space_constraint(a, pltpu.HBM) for a in lands])
    return out[:-1], out[-1]


def _forward_wait(state, after, *, name):
    send_sems, recv_sems = state[0], state[1]
    lands = state[2:]
    n = len(lands)

    def body(*refs):
        land_refs, send_sems, recv_sems = refs[:n], refs[n], refs[n + 1]
        x, y, c = lax.axis_index("x"), lax.axis_index("y"), lax.axis_index("c")
        for a in range(n):
            for j, (px, py) in enumerate(((1 - x, y), (x, 1 - y), (1 - x, 1 - y))):
                copy = pltpu.make_async_remote_copy(
                    src_ref=land_refs[a].at[4 * px + 2 * py + c], dst_ref=land_refs[a].at[4 * px + 2 * py + 1 - c],
                    send_sem=send_sems.at[3 * a + j], recv_sem=recv_sems.at[3 * a + j], device_id=(x, y, 1 - c),
                    device_id_type=MESH)
                copy.wait_send()
                copy.wait_recv()

    return pl.pallas_call(
        body, name=name,
        out_shape=tuple(pltpu.HBM(a.shape, a.dtype) for a in lands),
        in_specs=(_HBM,) * n + (_SEM, _SEM, pl.BlockSpec(memory_space=pl.ANY)), out_specs=(_HBM,) * n,
        input_output_aliases={i: i for i in range(n)},
        compiler_params=pltpu.CompilerParams(has_side_effects=_EFFECT),
    )(*lands, send_sems, recv_sems, after)


def _adam_math(w, g, m, v):
    m = ADAM_B1 * m + (1.0 - ADAM_B1) * g
    v = ADAM_B2 * v + (1.0 - ADAM_B2) * (g * g)
    m_hat = m / (1.0 - ADAM_B1 ** ADAM_STEP)
    v_hat = v / (1.0 - ADAM_B2 ** ADAM_STEP)
    delta = -ADAM_LR * (m_hat / (jnp.sqrt(v_hat) + ADAM_EPS) + ADAM_WD * w)
    return delta, m, v


def _adamw(parts, w, m, v, *, name):
    R, C = w.shape
    n_parts = parts.shape[0]
    tr = R // 2
    assert tr % 16 == 0

    def body(p_ref, w_ref, m_ref, v_ref, g_ref, d_ref, nm_ref, nv_ref):
        g = p_ref[0].astype(F32)
        for s in range(1, n_parts):
            g = g + p_ref[s].astype(F32)
        d, nm, nv = _adam_math(w_ref[...], g, m_ref[...], v_ref[...])
        g_ref[...] = g
        d_ref[...] = d
        nm_ref[...] = nm
        nv_ref[...] = nv

    blk = pl.BlockSpec((tr, C), lambda i: (i, 0))
    return pl.pallas_call(
        body, name=name, grid=(R // tr,),
        in_specs=[pl.BlockSpec((n_parts, tr, C), lambda i: (0, i, 0)), blk, blk, blk],
        out_specs=[blk] * 4, out_shape=[jax.ShapeDtypeStruct((R, C), F32)] * 4,
        compiler_params=_params(("arbitrary",)),
    )(parts, w, m, v)


def _adamw_small(parts, w, m, v):
    def body(p_ref, w_ref, m_ref, v_ref, g_ref, d_ref, nm_ref, nv_ref):
        g = p_ref[0]
        for s in range(1, N_DEV):
            g = g + p_ref[s]
        d, nm, nv = _adam_math(w_ref[...], g, m_ref[...], v_ref[...])
        g_ref[...] = g
        d_ref[...] = d
        nm_ref[...] = nm
        nv_ref[...] = nv

    vm = pl.BlockSpec(memory_space=pltpu.VMEM)
    return pl.pallas_call(
        body, name="adamw_small", in_specs=[vm] * 4, out_specs=[vm] * 4,
        out_shape=[jax.ShapeDtypeStruct((SMALL_ROWS, 128), F32)] * 4, compiler_params=_params(),
    )(parts, w, m, v)


def _t5_bucket(dist):
    max_exact = REL_BUCKETS // 2
    df = jnp.maximum(dist, 1).astype(F32)
    large = max_exact + (jnp.log(df / max_exact) / math.log(REL_MAX_DISTANCE / max_exact)
                         * (REL_BUCKETS - max_exact)).astype(jnp.int32)
    large = jnp.minimum(large, REL_BUCKETS - 1)
    return jnp.where(dist < max_exact, dist, large)


def _band_tables(rel_table, dil, n_back):
    qi = jnp.arange(BLK)[:, None]
    kj = jnp.arange(2 * BLK)[None, :]
    delta = BLK + qi - kj
    in_band = (delta >= 0) & (delta <= n_back)
    if rel_table is None:
        vals = jnp.zeros((N_HEADS, BLK, 2 * BLK), F32)
        bmap = None
    else:
        bucket = _t5_bucket(jnp.clip(delta, 0, n_back) * dil)
        vals = jnp.zeros((N_HEADS, BLK, 2 * BLK), F32)
        for b in range(REL_BUCKETS):
            vals = jnp.where((bucket == b)[None], rel_table[b][:, None, None], vals)
        bmap = jnp.where(in_band, bucket, -1).astype(jnp.int32)
    later = jnp.where(in_band[None], vals, NEG)
    first = jnp.where((in_band & (kj >= BLK))[None], vals, NEG)
    return jnp.stack([later, first]), bmap


def _rope_tables(T):
    half = HEAD_DIM // 2
    inv_freq = ROPE_THETA ** (-jnp.arange(half, dtype=F32) / half)
    ang = jnp.arange(T, dtype=F32)[:, None] * inv_freq[None, :]
    cos, sin = jnp.cos(ang), jnp.sin(ang)
    return jnp.tile(cos, (1, 4)), jnp.tile(jnp.concatenate([-sin, sin], axis=1), (1, 2))


def _widen_in(a, axis):
    sl = lambda lo, hi: lax.slice_in_dim(a, lo, hi, axis=axis)
    dup = lambda lo: [sl(lo, lo + 64), sl(lo, lo + 64), sl(lo + 64, lo + 128), sl(lo + 64, lo + 128)]
    return jnp.concatenate([sl(0, 512)] + dup(512) + dup(640) + [sl(768, D_IN)], axis=axis)


def _fold_in(a, axis):
    sl = lambda lo, hi: lax.slice_in_dim(a, lo, hi, axis=axis)
    fold = lambda lo: [sl(lo, lo + 64) + sl(lo + 64, lo + 128), sl(lo + 128, lo + 192) + sl(lo + 192, lo + 256)]
    return jnp.concatenate([sl(0, 512)] + fold(512) + fold(768) + [sl(1024, D_INP)], axis=axis)


def _local_step(x, tgt, g_attn, b_in, sinks, rel_table, g_out_a, g_out_b, g_ffn, g_final,
                win_fn, wo_fn, ffn_fn, early_fn):
    T = x.shape[0]
    cos, sin = _rope_tables(T)
    g_final2 = g_final.reshape(1, D_MODEL)
    sink8 = sinks.reshape(N_HEADS)

    bias_a, _ = _band_tables(None, 1, BLK - 1)
    tabs = [_band_tables(rel_table, dil, window // dil) for window, dil in BRANCHES]
    wint, token = win_fn(tabs[2][0])
    winp = _widen_in(wint, 0)
    binp = _widen_in(b_in, 1) + token[0, 0]

    h1, qa, ka, va, *qkv_b = _norm_proj(x, g_attn, winp, binp, cos, sin)
    qbs, kbs, vbs = qkv_b[0:3], qkv_b[3:6], qkv_b[6:9]
    oa, lse_a = _attn_fwd(qa, ka, va, bias_a, sink8, dil=1, kv_pairs=2, use_sink=True, name="attn_a_fwd")
    outs = [_attn_fwd(qbs[n], kbs[n], vbs[n], tabs[n][0], sink8, dil=dil, kv_pairs=4, use_sink=False,
                      name=f"attn_b{n}_fwd") for n, (_, dil) in enumerate(BRANCHES)]
    wo = wo_fn(outs[2][1])
    x2, mixed, h2, *ob_lse = _merge_wo(x, oa, outs[0][0], outs[1][0], outs[2][0], outs[0][1], outs[1][1], outs[2][1],
                                       g_out_a, g_out_b, wo, g_ffn)
    obs, lses = ob_lse[0:3], ob_lse[3:6]
    wgt, wut, wd = ffn_fn(h2)
    gate, up, act = _ffn_up(h2, wgt, wut)
    dx3, dx3b, loss, dg_final = _ffn_down_loss(act, wd, x2, tgt, g_final2)

    dgate, dup, dx2, dx2b, dg_ffn = _ffn_bwd(dx3, gate, up, wd, wgt, wut, x2, g_ffn)
    dwd = _matmul_tn(act, dx3b, tk=1408, tn=1024, name="dw_down")
    dwgt = _matmul_tn(dgate, h2, tk=1408, tn=1024, name="dw_gate")
    dwut = _matmul_tn(dup, h2, tk=1408, tn=1024, name="dw_up")
    doa, *dobs, dg_out_a, dg_out_b, dwo = _wo_bwd(dx2b, wo, oa, obs[0], g_out_a, g_out_b, mixed)
    early, token2 = early_fn(dict(w_o=dwo, w_gate=dwgt, w_up=dwut, w_down=dwd))
    sink8b = sink8 + token2[0, 0]

    dqa, dka, dva, _, dsk = _attn_bwd(qa, ka, va, oa, doa, lse_a, bias_a, sink8b, dil=1, kv_pairs=2, use_sink=True,
                                      name="attn_a_bwd", max_sub=4)
    res = [_attn_bwd(qbs[n], kbs[n], vbs[n], obs[n], dobs[n], lses[n], tabs[n][0], sink8b, dil=dil, kv_pairs=4,
                     use_sink=False, name=f"attn_b{n}_bwd") for n, (_, dil) in enumerate(BRANCHES)]
    dbp, grad_x, dg_attn, dwinp = _inproj_bwd(dqa, dka, dva, [r[0] for r in res], [r[1] for r in res],
                                              [r[2] for r in res], cos, sin, winp, x, dx2, g_attn, h1)
    dwin = _fold_in(dwinp, 0)
    drel, dsink = _bias_sink_grads([r[3] for r in res], [t[1] for t in tabs], dsk)

    small = dict(
        g_attn=dg_attn, b_in=_fold_in(dbp, 1), sinks=dsink[:, ::HEAD_DIM], rel_table=drel[:, :REL_BUCKETS].T,
        g_out_a=dg_out_a, g_out_b=dg_out_b, g_ffn=dg_ffn, g_final=dg_final.reshape(D_MODEL))
    return loss[0, 0], grad_x, dwin, early, small


SMALL_NAMES = ("g_attn", "b_in", "sinks", "rel_table", "g_out_a", "g_out_b", "g_ffn", "g_final", "loss")


def _pack_small(vals):
    flat = jnp.concatenate([vals[n].reshape(-1).astype(F32) for n in SMALL_NAMES])
    return jnp.pad(flat, (0, SMALL_ROWS * 128 - flat.shape[0])).reshape(SMALL_ROWS, 128)


def _unpack_small(packed, like):
    flat = packed.reshape(-1)
    out, off = {}, 0
    for n in SMALL_NAMES:
        size = like[n].size
        out[n] = flat[off:off + size].reshape(like[n].shape)
        off += size
    return out


def kernel(x, g_attn, w_in, b_in, sinks, rel_table, g_out_a, g_out_b, w_o, g_ffn, w_gate, w_up, w_down, g_final, loss_target, m_g_attn, m_w_in, m_b_in, m_sinks, m_rel_table, m_g_out_a, m_g_out_b, m_w_o, m_g_ffn, m_w_gate, m_w_up, m_w_down, m_g_final, v_g_attn, v_w_in, v_b_in, v_sinks, v_rel_table, v_g_out_a, v_g_out_b, v_w_o, v_g_ffn, v_w_gate, v_w_up, v_w_down, v_g_final):
    rest_names = ("w_o", "w_gate", "w_up", "w_down")

    rest = [w_o[0].astype(BF16), w_gate[0].astype(BF16).T, w_up[0].astype(BF16).T, w_down[0].astype(BF16)]
    in_state, _ = _exchange_start([w_in[0].astype(BF16).T], gather=True, near=True, name="gather_w_in_start")
    later = {}

    def whole(got):
        return [g.reshape(N_DEV * g.shape[1], D_MODEL) for g in got]

    def win_fn(after):
        near = _exchange_wait(in_state, after, gather=True, near=True, name="gather_w_in_near")
        fwd_state, tok = _forward_start(near, name="gather_w_in_forward")
        wint = whole(_forward_wait(fwd_state, tok, name="gather_w_in_wait"))[0]
        wint, src = lax.optimization_barrier((wint, rest))
        later["wo"], token_o = _exchange_start(src[:1], gather=True, name="gather_w_o_start")
        token_o, ffn_src = lax.optimization_barrier((token_o, src[1:]))
        later["ffn"], token = _exchange_start(ffn_src, gather=True, name="gather_ffn_start")
        return wint, token + token_o

    def wo_fn(after):
        return whole(_exchange_wait(later["wo"], after, gather=True, name="gather_w_o_wait"))[0]

    def ffn_fn(after):
        return whole(_exchange_wait(later["ffn"], after, gather=True, name="gather_ffn_wait"))

    def early_fn(dws):
        return _exchange_start([dws[n].reshape(N_DEV, -1, D_MODEL) for n in rest_names], gather=False,
                               name="scatter_rest_start")

    loss_part, grad_x, dwint, early_state, small = _local_step(
        x[0], loss_target[0], g_attn, b_in, sinks, rel_table, g_out_a, g_out_b, g_ffn, g_final,
        win_fn, wo_fn, ffn_fn, early_fn)
    parts_in = dwint.astype(BF16).reshape(N_DEV, D_IN // N_DEV, D_MODEL)
    in_state, token3 = _exchange_start([parts_in], gather=False, name="scatter_w_in_start")
    got = _exchange_wait(early_state, token3, gather=False, name="scatter_rest_wait")

    def update(n, parts, w, m, v, transposed):
        if transposed:
            return [a.T[None] for a in _adamw(parts, w[0].T, m[0].T, v[0].T, name="adamw_" + n)]
        return [a[None] for a in _adamw(parts, w[0], m[0], v[0], name="adamw_" + n)]

    big = dict(w_o=update("w_o", got[0], w_o, m_w_o, v_w_o, False),
               w_gate=update("w_gate", got[1], w_gate, m_w_gate, v_w_gate, True),
               w_up=update("w_up", got[2], w_up, m_w_up, v_w_up, True),
               w_down=update("w_down", got[3], w_down, m_w_down, v_w_down, False))

    unused = jnp.zeros((1,), F32)
    ws = dict(g_attn=g_attn, b_in=b_in, sinks=sinks, rel_table=rel_table, g_out_a=g_out_a, g_out_b=g_out_b,
              g_ffn=g_ffn, g_final=g_final, loss=unused)
    ms = dict(g_attn=m_g_attn, b_in=m_b_in, sinks=m_sinks, rel_table=m_rel_table, g_out_a=m_g_out_a,
              g_out_b=m_g_out_b, g_ffn=m_g_ffn, g_final=m_g_final, loss=unused)
    vs = dict(g_attn=v_g_attn, b_in=v_b_in, sinks=v_sinks, rel_table=v_rel_table, g_out_a=v_g_out_a,
              g_out_b=v_g_out_b, g_ffn=v_g_ffn, g_final=v_g_final, loss=unused)
    sparts = _all_gather(_pack_small(dict(small, loss=loss_part)), name="gather_small")
    sm_packed = _adamw_small(sparts, _pack_small(ws), _pack_small(ms), _pack_small(vs))
    sm = [_unpack_small(a, ws) for a in sm_packed]
    loss = sm[0]["loss"][0]

    done = sm_packed[1][:1, :1] + sum(big[n][1][0, :1, :1] for n in rest_names)
    got_in = _exchange_wait(in_state, done, gather=False, name="scatter_w_in_wait")[0]
    big["w_in"] = update("w_in", got_in, w_in, m_w_in, v_w_in, True)

    order = ("g_attn", "w_in", "b_in", "sinks", "rel_table", "g_out_a", "g_out_b", "w_o", "g_ffn", "w_gate", "w_up",
             "w_down", "g_final")
    outs = [loss, grad_x[None]]
    for k in range(4):
        outs += [big[n][k] if n in big else sm[k][n] for n in order]
    return tuple(outs)
```
